```python
import math
import jax, jax.numpy as jnp
from jax import lax
import numpy as np

D_MODEL = 1024
BATCH = 8
SEQ = 2048
DEPTH = 2

A_HEAD_DIM = 64
A_HEADS = D_MODEL // A_HEAD_DIM
DILATED_PAIRS = ((128, 1), (512, 4), (2048, 16))
DBLK = 128
MLA_HEADS = D_MODEL // 64
Q_RANK = 3 * D_MODEL // 8
KV_RANK = D_MODEL // 4
NOPE_DIM = 64
ROPE_DIM = 32
QK_DIM = NOPE_DIM + ROPE_DIM
V_DIM = 64
QBLK = 128
ROPE_THETA = 10000.0
N_EXPERTS = 64
N_GROUPS = 8
TOPK_GROUPS = 4
TOP_K = 6
EXPERT_DIM = D_MODEL // 4
SHARED_DIM = D_MODEL // 4
ROUTED_SCALE = 2.5
EXPERT_BLOCK = 256
NORM_EPS = 1e-6

kernel_name = 'yoco_dilated_mla_moe_adaln_block'


def _layer_split():
    n_a = (DEPTH + 1) // 2
    return n_a, DEPTH - n_a


def rms_norm(x, g):
    x32 = x.astype(jnp.float32)
    y = x32 * lax.rsqrt(jnp.mean(x32 * x32, axis=-1, keepdims=True) + NORM_EPS)
    return (y * g.astype(jnp.float32)).astype(x.dtype)


def apply_rope(t, pos):
    half = t.shape[-1] // 2
    inv = jnp.power(ROPE_THETA, -jnp.arange(half, dtype=jnp.float32) / half)
    ang = pos.astype(jnp.float32)[..., None] * inv
    cos = jnp.cos(ang)[:, :, None, :]
    sin = jnp.sin(ang)[:, :, None, :]
    t1 = t[..., :half].astype(jnp.float32)
    t2 = t[..., half:].astype(jnp.float32)
    return jnp.concatenate([t1 * cos - t2 * sin, t1 * sin + t2 * cos], axis=-1).astype(t.dtype)


def alibi_slopes(n_heads):
    return jnp.exp2(-8.0 * jnp.arange(1, n_heads + 1, dtype=jnp.float32) / n_heads)


def dilated_branch(q, k, v, pos, slopes, dil, span):
    B, S, H, Dh = q.shape
    L = S // dil
    nb = -(-L // DBLK)
    Lp = nb * DBLK

    def residues(t):
        return jnp.swapaxes(t.reshape((B, L, dil) + t.shape[2:]), 1, 2)

    def pad_len(t, front, back):
        widths = [(0, 0)] * t.ndim
        widths[2] = (front, back)
        return jnp.pad(t, widths)

    def key_windows(t):
        t = pad_len(t, DBLK, Lp - L).reshape((B, dil, nb + 1, DBLK) + t.shape[3:])
        return jnp.concatenate([t[:, :, :-1], t[:, :, 1:]], axis=3)

    qb = pad_len(residues(q), 0, Lp - L).reshape(B, dil, nb, DBLK, H, Dh)
    kw = key_windows(residues(k))
    vw = key_windows(residues(v))
    pr = residues(pos)
    pq = pad_len(pr, 0, Lp - L).reshape(B, dil, nb, DBLK)
    pk = key_windows(pr)

    s = jnp.einsum('brnqhd,brnkhd->brnhqk', qb, kw,
                   preferred_element_type=jnp.float32) * (Dh ** -0.5)
    rel = (pq[..., :, None] - pk[..., None, :]).astype(jnp.float32)
    s = s - slopes[:, None, None] * rel[:, :, :, None]
    a_idx = jnp.arange(DBLK)[:, None]
    c_idx = jnp.arange(2 * DBLK)[None, :]
    dist = a_idx - c_idx + DBLK
    key_idx = jnp.arange(nb)[:, None, None] * DBLK + c_idx[None] - DBLK
    valid = (dist >= 0) & (dist <= span) & (key_idx >= 0)
    s = jnp.where(valid[:, None], s, -jnp.inf)
    m = jnp.max(s, axis=-1, keepdims=True)
    p = jnp.exp(s - m)
    l = jnp.sum(p, axis=-1)
    o = jnp.einsum('brnhqk,brnkhd->brnqhd', p.astype(v.dtype), vw,
                   preferred_element_type=jnp.float32)
    o = o / jnp.moveaxis(l, 3, 4)[..., None]
    lse = jnp.moveaxis(m[..., 0] + jnp.log(l), 3, 4)

    def back(t):
        rest = t.shape[4:]
        t = t.reshape((B, dil, Lp) + rest)[:, :, :L]
        return jnp.swapaxes(t, 1, 2).reshape((B, S) + rest)

    return back(o), back(lse)


def dilated_attention(h, pos, w_qkv, q_norm, k_norm, w_o):
    B, S, _ = h.shape
    qkv = (h @ w_qkv).reshape(B, S, 3, A_HEADS, A_HEAD_DIM)
    q = rms_norm(qkv[:, :, 0], q_norm)
    k = rms_norm(qkv[:, :, 1], k_norm)
    v = qkv[:, :, 2]
    slopes = alibi_slopes(A_HEADS)
    outs, lses = [], []
    for win, dil in DILATED_PAIRS:
        o_b, lse_b = dilated_branch(q, k, v, pos, slopes, dil, win // dil)
        outs.append(o_b)
        lses.append(lse_b)
    wts = jax.nn.softmax(jnp.stack(lses), axis=0)
    o = jnp.einsum('gbsh,gbshd->bshd', wts, jnp.stack(outs))
    return o.reshape(B, S, A_HEADS * A_HEAD_DIM).astype(h.dtype) @ w_o


def shared_kv(x, c, pos, kv_ada_w, kv_ada_b, kv_norm_g, kv_w_down, kv_latent_norm,
              kv_w_uk, kv_w_uv, kv_k_norm):
    B, S, _ = x.shape
    mod = jax.nn.silu(c) @ kv_ada_w + kv_ada_b
    shift, scale = jnp.split(mod, 2, axis=-1)
    h = rms_norm(x, kv_norm_g) * (1.0 + scale[:, None]) + shift[:, None]
    down = h @ kv_w_down
    c_kv = rms_norm(down[..., :KV_RANK], kv_latent_norm)
    k_pe = down[..., KV_RANK:]
    k_nope = (c_kv @ kv_w_uk).reshape(B, S, MLA_HEADS, NOPE_DIM)
    v = (c_kv @ kv_w_uv).reshape(B, S, MLA_HEADS, V_DIM)
    k = jnp.concatenate(
        [k_nope, jnp.broadcast_to(k_pe[:, :, None, :], (B, S, MLA_HEADS, ROPE_DIM))], axis=-1)
    k = rms_norm(k, kv_k_norm)
    k = jnp.concatenate([k[..., :NOPE_DIM], apply_rope(k[..., NOPE_DIM:], pos)], axis=-1)
    return k, v


def mla_attention(h, k, v, pos, w_dq, q_latent_norm, w_uq, q_norm, w_o):
    B, S, _ = h.shape
    c_q = rms_norm(h @ w_dq, q_latent_norm)
    q = (c_q @ w_uq).reshape(B, S, MLA_HEADS, QK_DIM)
    q = rms_norm(q, q_norm)
    q = jnp.concatenate([q[..., :NOPE_DIM], apply_rope(q[..., NOPE_DIM:], pos)], axis=-1)
    nq = S // QBLK
    qb = jnp.moveaxis(q.reshape(B, nq, QBLK, MLA_HEADS, QK_DIM), 1, 0)
    key_pos = jnp.arange(S)

    def block(args):
        qi, start = args
        s = jnp.einsum('bqhd,bkhd->bhqk', qi, k,
                       preferred_element_type=jnp.float32) * (QK_DIM ** -0.5)
        mask = (start + jnp.arange(QBLK))[:, None] >= key_pos[None, :]
        p = jax.nn.softmax(jnp.where(mask, s, -jnp.inf), axis=-1)
        return jnp.einsum('bhqk,bkhd->bqhd', p.astype(v.dtype), v)

    o = lax.map(block, (qb, jnp.arange(nq) * QBLK))
    o = jnp.moveaxis(o, 0, 1).reshape(B, S, MLA_HEADS * V_DIM)
    return o @ w_o


def swiglu(x, w_gu, w_dn):
    g, u = jnp.split(x @ w_gu, 2, axis=-1)
    return (jax.nn.silu(g) * u) @ w_dn


def routed_experts(xf, idx, wts, w_gu, w_dn):
    N, D = xf.shape
    E = w_gu.shape[0]
    K = idx.shape[1]
    A = N * K
    e = idx.reshape(-1)
    tok = jnp.repeat(jnp.arange(N, dtype=jnp.int32), K)
    w = wts.reshape(-1)
    order = jnp.argsort(e, stable=True)
    e_s, tok_s, w_s = e[order], tok[order], w[order]
    counts = jnp.bincount(e, length=E)
    padded = ((counts + EXPERT_BLOCK - 1) // EXPERT_BLOCK) * EXPERT_BLOCK
    pad_end = jnp.cumsum(padded)
    pad_start = pad_end - padded
    raw_start = jnp.cumsum(counts) - counts
    dest = pad_start[e_s] + jnp.arange(A) - raw_start[e_s]
    nblk = (A + E * (EXPERT_BLOCK - 1) + EXPERT_BLOCK - 1) // EXPERT_BLOCK
    P = nblk * EXPERT_BLOCK
    buf_tok = jnp.full((P,), N, dtype=jnp.int32).at[dest].set(tok_s)
    buf_w = jnp.zeros((P,), jnp.float32).at[dest].set(w_s.astype(jnp.float32))
    blk_e = jnp.minimum(
        jnp.searchsorted(pad_end, jnp.arange(nblk) * EXPERT_BLOCK, side='right'), E - 1)
    x_pad = jnp.concatenate([xf, jnp.zeros((1, D), xf.dtype)], axis=0)

    def step(acc, inp):
        t, wb, ei = inp
        y = swiglu(x_pad[t], w_gu[ei], w_dn[ei])
        return acc.at[t].add(y.astype(jnp.float32) * wb[:, None]), None

    acc, _ = lax.scan(step, jnp.zeros((N + 1, D), jnp.float32),
                      (buf_tok.reshape(nblk, EXPERT_BLOCK), buf_w.reshape(nblk, EXPERT_BLOCK), blk_e))
    return acc[:N].astype(xf.dtype)


def moe_ffn(h, w_router, router_bias, w_gate_up, w_down, shared_gate_up, shared_down):
    B, S, D = h.shape
    xf = h.reshape(-1, D)
    N = xf.shape[0]
    scores = jax.nn.sigmoid(jnp.einsum('nd,de->ne', xf, w_router,
                                       preferred_element_type=jnp.float32))
    sel = scores + router_bias.astype(jnp.float32)
    grp = sel.reshape(N, N_GROUPS, N_EXPERTS // N_GROUPS)
    gscore = jnp.sum(lax.top_k(grp, 2)[0], axis=-1)
    _, gidx = lax.top_k(gscore, TOPK_GROUPS)
    gmask = jnp.sum(jax.nn.one_hot(gidx, N_GROUPS), axis=1) > 0
    emask = jnp.repeat(gmask, N_EXPERTS // N_GROUPS, axis=1)
    _, idx = lax.top_k(jnp.where(emask, sel, -jnp.inf), TOP_K)
    w = jnp.take_along_axis(scores, idx, axis=1)
    w = w / jnp.sum(w, axis=-1, keepdims=True) * ROUTED_SCALE
    routed = routed_experts(xf, idx, w, w_gate_up, w_down)
    shared = swiglu(xf, shared_gate_up, shared_down)
    return (routed + shared).reshape(B, S, D)


def setup_inputs(seed: int = 0) -> dict:
    key = jax.random.key(seed)
    ks = jax.random.split(key, 40)
    ctr = [0]

    def nxt():
        ctr[0] += 1
        return ks[ctr[0] - 1]

    f32 = jnp.float32
    D = D_MODEL
    n_a, n_b = _layer_split()

    def w(shape, fan_in, scale=1.0):
        return jax.random.normal(nxt(), shape, f32) * (scale * fan_in ** -0.5)

    def gain(shape):
        return 1.0 + 0.05 * jax.random.normal(nxt(), shape, f32)

    def small(shape, s):
        return s * jax.random.normal(nxt(), shape, f32)

    x = jax.random.normal(nxt(), (BATCH, SEQ, D), f32)
    c = jax.random.normal(nxt(), (BATCH, D), f32)
    pos = (jnp.arange(SEQ, dtype=jnp.int32)[None, :]
           + jax.random.randint(nxt(), (BATCH, 1), 0, 64, dtype=jnp.int32))
    return {
        'x': x, 'c': c, 'pos': pos,
        'ada_w': w((DEPTH, D, 6 * D), D, 0.5),
        'ada_b': small((DEPTH, 6 * D), 0.05),
        'norm1_g': gain((DEPTH, D)),
        'norm2_g': gain((DEPTH, D)),
        'a_w_qkv': w((n_a, D, 3 * A_HEADS * A_HEAD_DIM), D),
        'a_q_norm': gain((n_a, A_HEAD_DIM)),
        'a_k_norm': gain((n_a, A_HEAD_DIM)),
        'a_w_o': w((n_a, A_HEADS * A_HEAD_DIM, D), A_HEADS * A_HEAD_DIM),
        'kv_ada_w': w((D, 2 * D), D, 0.5),
        'kv_ada_b': small((2 * D,), 0.05),
        'kv_norm_g': gain((D,)),
        'kv_w_down': w((D, KV_RANK + ROPE_DIM), D),
        'kv_latent_norm': gain((KV_RANK,)),
        'kv_w_uk': w((KV_RANK, MLA_HEADS * NOPE_DIM), KV_RANK),
        'kv_w_uv': w((KV_RANK, MLA_HEADS * V_DIM), KV_RANK),
        'kv_k_norm': gain((QK_DIM,)),
        'b_w_dq': w((n_b, D, Q_RANK), D),
        'b_q_latent_norm': gain((n_b, Q_RANK)),
        'b_w_uq': w((n_b, Q_RANK, MLA_HEADS * QK_DIM), Q_RANK),
        'b_q_norm': gain((n_b, QK_DIM)),
        'b_w_o': w((n_b, MLA_HEADS * V_DIM, D), MLA_HEADS * V_DIM),
        'moe_w_router': w((DEPTH, D, N_EXPERTS), D),
        'moe_router_bias': small((DEPTH, N_EXPERTS), 0.01),
        'moe_w_gate_up': w((DEPTH, N_EXPERTS, D, 2 * EXPERT_DIM), D),
        'moe_w_down': w((DEPTH, N_EXPERTS, EXPERT_DIM, D), EXPERT_DIM),
        'moe_shared_gate_up': w((DEPTH, D, 2 * SHARED_DIM), D),
        'moe_shared_down': w((DEPTH, SHARED_DIM, D), SHARED_DIM),
    }


def reference(x, c, pos, ada_w, ada_b, norm1_g, norm2_g, a_w_qkv, a_q_norm, a_k_norm, a_w_o,
              kv_ada_w, kv_ada_b, kv_norm_g, kv_w_down, kv_latent_norm, kv_w_uk, kv_w_uv,
              kv_k_norm, b_w_dq, b_q_latent_norm, b_w_uq, b_q_norm, b_w_o, moe_w_router,
              moe_router_bias, moe_w_gate_up, moe_w_down, moe_shared_gate_up, moe_shared_down):
    n_a, _ = _layer_split()
    cond = jax.nn.silu(c)
    k_sh = None
    v_sh = None
    for l in range(DEPTH):
        mod = cond @ ada_w[l] + ada_b[l]
        sh1, sc1, g1, sh2, sc2, g2 = [m[:, None, :] for m in jnp.split(mod, 6, axis=-1)]
        if l == n_a:
            k_sh, v_sh = shared_kv(x, c, pos, kv_ada_w, kv_ada_b, kv_norm_g, kv_w_down,
                                   kv_latent_norm, kv_w_uk, kv_w_uv, kv_k_norm)
        h = rms_norm(x, norm1_g[l]) * (1.0 + sc1) + sh1
        if l < n_a:
            y = dilated_attention(h, pos, a_w_qkv[l], a_q_norm[l], a_k_norm[l], a_w_o[l])
        else:
            j = l - n_a
            y = mla_attention(h, k_sh, v_sh, pos, b_w_dq[j], b_q_latent_norm[j], b_w_uq[j],
                              b_q_norm[j], b_w_o[j])
        x = x + g1 * y
        h = rms_norm(x, norm2_g[l]) * (1.0 + sc2) + sh2
        x = x + g2 * moe_ffn(h, moe_w_router[l], moe_router_bias[l], moe_w_gate_up[l],
                             moe_w_down[l], moe_shared_gate_up[l], moe_shared_down[l])
    return x
```

```python
import numpy as np
import jax
import jax.numpy as jnp
from jax import lax
from jax.experimental import pallas as pl
from jax.experimental.pallas import tpu as pltpu

F32 = jnp.float32
BF16 = jnp.bfloat16

NORM_EPS = 1e-6
MASKED = -1e30

LANES = 128
SUBLANES = 8
VMEM_LIMIT = 56 * 1024 * 1024

A_HEAD_DIM = 64
DILATED_PAIRS = ((128, 1), (512, 4), (2048, 16))
QBLK = 256
NOPE_DIM = 64
ROPE_DIM = 32
QK_DIM = NOPE_DIM + ROPE_DIM
V_DIM = 64
KV_RANK = 256
ROPE_THETA = 10000.0
N_EXPERTS = 64
N_GROUPS = 8
TOPK_GROUPS = 4
TOP_K = 6
EXPERT_DIM = 256
ROUTED_SCALE = 2.5
MOE_TILE = 1024
EROWS = 128
CROWS = 256

ROW_TILE = 512


def _cparams(*sem):
    return pltpu.CompilerParams(dimension_semantics=sem, vmem_limit_bytes=VMEM_LIMIT)


def _silu(v):
    return v * jax.nn.sigmoid(v)


def _rms(v):
    return v * lax.rsqrt(jnp.mean(v * v, axis=-1, keepdims=True) + NORM_EPS)


def _dot_nt(a, b):
    return lax.dot_general(a, b, (((1,), (1,)), ((), ())), preferred_element_type=F32)


def _mod_kernel(c_ref, w_ref, b_ref, o_ref):
    cond = _silu(c_ref[...])
    o_ref[...] = jnp.dot(cond, w_ref[...], preferred_element_type=F32,
                         precision=lax.Precision.HIGHEST) + b_ref[...]


def _modulation(c, w, b, layer):
    bsz, d = c.shape
    m = w.shape[-1]
    tn = 1024
    return pl.pallas_call(
        _mod_kernel,
        grid=(m // tn,),
        in_specs=[pl.BlockSpec((bsz, d), lambda j: (0, 0)),
                  pl.BlockSpec((None, d, tn), lambda j: (layer, 0, j)),
                  pl.BlockSpec((None, 1, tn), lambda j: (layer, 0, j))],
        out_specs=pl.BlockSpec((bsz, tn), lambda j: (0, j)),
        out_shape=jax.ShapeDtypeStruct((bsz, m), F32),
        compiler_params=_cparams("arbitrary"),
    )(c, w, b.reshape(b.shape[0], 1, m))


def _qkv_kernel(x_ref, g_ref, sc_ref, sh_ref, w_ref, o_ref):
    h = _rms(x_ref[...]) * g_ref[...]
    h = h * (1.0 + sc_ref[...]) + sh_ref[...]
    o_ref[...] = jnp.dot(h.astype(BF16), w_ref[...], preferred_element_type=F32).astype(o_ref.dtype)


def _qkv_proj(xf, g, sc, sh, w, seq):
    n, d = xf.shape
    m = w.shape[1]
    tm = min(ROW_TILE, seq)
    per_b = seq // tm
    vec = pl.BlockSpec((None, 1, d), lambda i: (i // per_b, 0, 0))
    return pl.pallas_call(
        _qkv_kernel,
        grid=(n // tm,),
        in_specs=[pl.BlockSpec((tm, d), lambda i: (i, 0)),
                  pl.BlockSpec((1, d), lambda i: (0, 0)),
                  vec, vec,
                  pl.BlockSpec((d, m), lambda i: (0, 0))],
        out_specs=pl.BlockSpec((tm, m), lambda i: (i, 0)),
        out_shape=jax.ShapeDtypeStruct((n, m), BF16),
        compiler_params=_cparams("parallel"),
    )(xf, g, sc, sh, w)


def _attend(q, k, v, bias):
    s = _dot_nt(q, k) + bias
    m = jnp.max(s, axis=-1, keepdims=True)
    p = jnp.exp(s - m)
    l = jnp.sum(p, axis=-1, keepdims=True)
    return jnp.dot(p.astype(BF16), v, preferred_element_type=F32) / l


def _band_offset(seq, w):
    return seq - w


def _dilated_log_multiplicity(seq):
    a = np.arange(QBLK)[:, None]
    c = np.arange(seq)[None, :]
    d = (seq - QBLK) + a - c
    mult = np.zeros((QBLK, seq), np.int32)
    for win, dil in DILATED_PAIRS:
        mult += ((d >= 0) & (d % dil == 0) & (d <= win)).astype(np.int32)
    return np.where(mult > 0, np.log(np.maximum(mult, 1)), MASKED).astype(np.float32)


def _dil_attn_kernel(slopes_ref, q_ref, k_ref, v_ref, posr_ref, gq_ref, gk_ref, logm_ref,
                     o_ref, qa_ref, qb_ref, ks_ref):
    seq = q_ref.shape[0]
    hp = pl.program_id(1)
    lane_lo = lax.broadcasted_iota(jnp.int32, (QBLK, LANES), 1) < A_HEAD_DIM

    def head_norm(ref, r0, g_ref, mul):
        t = ref[r0:r0 + QBLK, :].astype(F32)
        sq = t * t
        ssa = jnp.sum(jnp.where(lane_lo, sq, 0.0), axis=-1, keepdims=True)
        ssb = jnp.sum(jnp.where(lane_lo, 0.0, sq), axis=-1, keepdims=True)
        r = jnp.where(lane_lo, lax.rsqrt(ssa / A_HEAD_DIM + NORM_EPS),
                      lax.rsqrt(ssb / A_HEAD_DIM + NORM_EPS))
        return t * r * (g_ref[...] * mul)

    for r0 in range(0, seq, QBLK):
        qn = head_norm(q_ref, r0, gq_ref, A_HEAD_DIM ** -0.5)
        qa_ref[r0:r0 + QBLK, :] = jnp.where(lane_lo, qn, 0.0).astype(BF16)
        qb_ref[r0:r0 + QBLK, :] = jnp.where(lane_lo, 0.0, qn).astype(BF16)
        ks_ref[r0:r0 + QBLK, :] = head_norm(k_ref, r0, gk_ref, 1.0).astype(BF16)

    slope_a = slopes_ref[2 * hp]
    slope_b = slopes_ref[2 * hp + 1]
    for r0 in range(0, seq, QBLK):
        w = r0 + QBLK
        off = _band_offset(seq, w)
        lm = logm_ref[:, off:off + w]
        kk = ks_ref[0:w, :]
        vv = v_ref[0:w, :]
        pkw = (posr_ref[:, 0:w] - posr_ref[:, r0:r0 + 1]).astype(F32)
        o_a = _attend(qa_ref[r0:w, :], kk, vv, lm + slope_a * pkw)
        o_b = _attend(qb_ref[r0:w, :], kk, vv, lm + slope_b * pkw)
        o_ref[r0:w, :] = jnp.where(lane_lo, o_a, o_b).astype(o_ref.dtype)


def _dilated_attention(qkv, pos, q_norm, k_norm):
    bsz, seq, three_d = qkv.shape
    d = three_d // 3
    n_pairs = d // LANES
    n_heads = d // A_HEAD_DIM
    slopes = jnp.exp2(-8.0 * jnp.arange(1, n_heads + 1, dtype=F32) / n_heads)
    logm = jnp.asarray(_dilated_log_multiplicity(seq))
    gq = jnp.tile(q_norm.reshape(1, A_HEAD_DIM), (1, 2))
    gk = jnp.tile(k_norm.reshape(1, A_HEAD_DIM), (1, 2))
    blk = lambda off: pl.BlockSpec((None, seq, LANES), lambda b, h: (b, 0, off + h))
    const2 = pl.BlockSpec((1, LANES), lambda b, h: (0, 0))
    return pl.pallas_call(
        _dil_attn_kernel,
        grid=(bsz, n_pairs),
        in_specs=[pl.BlockSpec(memory_space=pltpu.SMEM),
                  blk(0), blk(n_pairs), blk(2 * n_pairs),
                  pl.BlockSpec((None, 1, seq), lambda b, h: (b, 0, 0)),
                  const2, const2,
                  pl.BlockSpec((QBLK, seq), lambda b, h: (0, 0))],
        out_specs=pl.BlockSpec((None, seq, LANES), lambda b, h: (b, 0, h)),
        scratch_shapes=[pltpu.VMEM((seq, LANES), BF16), pltpu.VMEM((seq, LANES), BF16),
                        pltpu.VMEM((seq, LANES), BF16)],
        out_shape=jax.ShapeDtypeStruct((bsz, seq, d), BF16),
        compiler_params=_cparams("parallel", "parallel"),
    )(slopes, qkv, qkv, qkv, pos.reshape(bsz, 1, seq), gq, gk, logm)


def _post_attn_kernel(o_ref, wo_ref, x_ref, g1_ref, n2_ref, sc_ref, sh_ref, wr_ref,
                      xm_ref, h_ref, lg_ref):
    y = jnp.dot(o_ref[...], wo_ref[...], preferred_element_type=F32)
    xm = x_ref[...] + g1_ref[...] * y
    xm_ref[...] = xm
    h = _rms(xm) * n2_ref[...]
    h = h * (1.0 + sc_ref[...]) + sh_ref[...]
    h_ref[...] = h
    lg_ref[...] = jnp.dot(h, wr_ref[...], preferred_element_type=F32,
                          precision=lax.Precision.HIGHEST)


def _post_attn(o, w_o, xf, g1, n2, sc2, sh2, w_router, seq):
    n, d = xf.shape
    ne = w_router.shape[1]
    tm = min(ROW_TILE, seq)
    per_b = seq // tm
    vec = pl.BlockSpec((None, 1, d), lambda i: (i // per_b, 0, 0))
    row = lambda w: pl.BlockSpec((tm, w), lambda i: (i, 0))
    return pl.pallas_call(
        _post_attn_kernel,
        grid=(n // tm,),
        in_specs=[row(d), pl.BlockSpec((d, d), lambda i: (0, 0)), row(d), vec,
                  pl.BlockSpec((1, d), lambda i: (0, 0)), vec, vec,
                  pl.BlockSpec((d, ne), lambda i: (0, 0))],
        out_specs=[row(d), row(d), row(ne)],
        out_shape=[jax.ShapeDtypeStruct((n, d), F32), jax.ShapeDtypeStruct((n, d), F32),
                   jax.ShapeDtypeStruct((n, ne), F32)],
        compiler_params=_cparams("parallel"),
    )(o, w_o, xf, g1, n2, sc2, sh2, w_router)


def _route(logits, router_bias):
    n = logits.shape[0]
    scores = jax.nn.sigmoid(logits)
    sel = scores + router_bias.astype(F32)
    per_g = N_EXPERTS // N_GROUPS
    grp = sel.reshape(n, N_GROUPS, per_g)
    gscore = jnp.sum(lax.top_k(grp, 2)[0], axis=-1)
    _, gidx = lax.top_k(gscore, TOPK_GROUPS)
    gmask = jnp.sum(jax.nn.one_hot(gidx, N_GROUPS), axis=1) > 0
    emask = jnp.repeat(gmask, per_g, axis=1)
    _, idx = lax.top_k(jnp.where(emask, sel, -jnp.inf), TOP_K)
    w = jnp.take_along_axis(scores, idx, axis=1)
    w = w / jnp.sum(w, axis=-1, keepdims=True) * ROUTED_SCALE
    return idx.astype(jnp.int32), w


def _moe_rows(tile):
    rows = tile * TOP_K + N_EXPERTS * (SUBLANES - 1) + EROWS
    return -(-rows // SUBLANES) * SUBLANES


def _dispatch_tables(idx, w, tile):
    n = idx.shape[0]
    nt = n // tile
    onehot = (idx[:, :, None] == jnp.arange(N_EXPERTS, dtype=jnp.int32)).astype(jnp.int32)
    chosen = jnp.sum(onehot, axis=1).reshape(nt, tile, N_EXPERTS)
    incl = jnp.cumsum(chosen, axis=1)
    counts = incl[:, -1, :]
    padded = -(-counts // SUBLANES) * SUBLANES
    seg_start = jnp.cumsum(padded, axis=1) - padded
    row_of = (seg_start[:, None, :] + incl - chosen).reshape(n, 1, N_EXPERTS)
    dest = jnp.sum(onehot * row_of, axis=2)
    na = tile * TOP_K
    return (seg_start.reshape(-1).astype(jnp.int32), counts.reshape(-1).astype(jnp.int32),
            dest.reshape(nt, 1, na).astype(jnp.int32), w.reshape(nt, 1, na).astype(F32))


def _swiglu(xb, wgu_ref, wdn_ref):
    gu = jnp.dot(xb, wgu_ref[...], preferred_element_type=F32)
    half = wgu_ref.shape[1] // 2
    act = _silu(gu[:, :half]) * gu[:, half:]
    return jnp.dot(act.astype(BF16), wdn_ref[...], preferred_element_type=F32)


def _moe_kernel(start_ref, count_ref, h_ref, dest_ref, w_ref, wgu_ref, wdn_ref,
                sgu_ref, sdn_ref, xm_ref, g2_ref, o_ref, xy_ref, y_ref, comb_ref):
    t = pl.program_id(0)
    s = pl.program_id(1)
    tile = h_ref.shape[0]

    @pl.when(jnp.logical_and(t == 0, s == 0))
    def _clear():
        xy_ref[...] = jnp.zeros(xy_ref.shape, F32)

    @pl.when(s == 0)
    def _dispatch():
        def group(j, carry):
            r0 = pl.multiple_of(j * SUBLANES, SUBLANES)
            rows = h_ref[pl.ds(r0, SUBLANES), :]
            for u in range(SUBLANES):
                for k in range(TOP_K):
                    xy_ref[pl.ds(dest_ref[0, (r0 + u) * TOP_K + k], 1), :] = rows[u:u + 1, :]
            return carry

        lax.fori_loop(0, tile // SUBLANES, group, 0)

    @pl.when(s < N_EXPERTS)
    def _expert():
        start = start_ref[t * N_EXPERTS + s]
        count = count_ref[t * N_EXPERTS + s]

        def chunk(c, carry):
            base = pl.multiple_of(start + c * EROWS, SUBLANES)
            groups = (jnp.minimum(count - c * EROWS, EROWS) + SUBLANES - 1) // SUBLANES
            y_ref[...] = _swiglu(xy_ref[pl.ds(base, EROWS), :].astype(BF16), wgu_ref, wdn_ref)

            def put(j, c2):
                r0 = pl.multiple_of(j * SUBLANES, SUBLANES)
                xy_ref[pl.ds(base + r0, SUBLANES), :] = y_ref[pl.ds(r0, SUBLANES), :]
                return c2

            lax.fori_loop(0, groups, put, 0)
            return carry

        lax.fori_loop(0, (count + EROWS - 1) // EROWS, chunk, 0)

    @pl.when(s >= N_EXPERTS)
    def _combine():
        sb = s - N_EXPERTS

        def group(j, carry):
            r0 = pl.multiple_of(j * SUBLANES, SUBLANES)
            for u in range(SUBLANES):
                a0 = (sb * CROWS + r0 + u) * TOP_K
                acc = w_ref[0, a0] * xy_ref[pl.ds(dest_ref[0, a0], 1), :]
                for k in range(1, TOP_K):
                    acc = acc + w_ref[0, a0 + k] * xy_ref[pl.ds(dest_ref[0, a0 + k], 1), :]
                comb_ref[pl.ds(r0 + u, 1), :] = acc
            return carry

        lax.fori_loop(0, CROWS // SUBLANES, group, 0)
        hb = h_ref[pl.ds(pl.multiple_of(sb * CROWS, CROWS), CROWS), :].astype(BF16)
        shared = _swiglu(hb, sgu_ref, sdn_ref)
        o_ref[...] = xm_ref[...] + g2_ref[...] * (comb_ref[...] + shared)


def _moe(h, xm, g2, idx, w, w_gu, w_dn, s_gu, s_dn, layer, seq):
    n, d = h.shape
    tile = min(MOE_TILE, seq)
    nt = n // tile
    per_b = seq // tile
    crows = min(CROWS, tile)
    csteps = tile // crows
    assert crows == CROWS and seq % tile == 0 and tile % CROWS == 0
    seg_start, seg_count, dest, wts = _dispatch_tables(idx, w, tile)
    rows = _moe_rows(tile)
    na = tile * TOP_K
    ecl = lambda s: jnp.minimum(s, N_EXPERTS - 1)
    cst = lambda s: jnp.maximum(s - N_EXPERTS, 0)
    smem = pl.BlockSpec((None, 1, na), lambda t, s, a, b: (t, 0, 0), memory_space=pltpu.SMEM)
    return pl.pallas_call(
        _moe_kernel,
        grid_spec=pltpu.PrefetchScalarGridSpec(
            num_scalar_prefetch=2,
            grid=(nt, N_EXPERTS + csteps),
            in_specs=[pl.BlockSpec((tile, d), lambda t, s, a, b: (t, 0)),
                      smem, smem,
                      pl.BlockSpec((None, None, d, 2 * EXPERT_DIM),
                                   lambda t, s, a, b: (layer, ecl(s), 0, 0)),
                      pl.BlockSpec((None, None, EXPERT_DIM, d),
                                   lambda t, s, a, b: (layer, ecl(s), 0, 0)),
                      pl.BlockSpec((None,) + s_gu.shape[1:], lambda t, s, a, b: (layer, 0, 0)),
                      pl.BlockSpec((None,) + s_dn.shape[1:], lambda t, s, a, b: (layer, 0, 0)),
                      pl.BlockSpec((CROWS, d), lambda t, s, a, b: (t * csteps + cst(s), 0)),
                      pl.BlockSpec((None, 1, d), lambda t, s, a, b: (t // per_b, 0, 0))],
            out_specs=pl.BlockSpec((CROWS, d), lambda t, s, a, b: (t * csteps + cst(s), 0)),
            scratch_shapes=[pltpu.VMEM((rows, d), F32), pltpu.VMEM((EROWS, d), F32),
                            pltpu.VMEM((CROWS, d), F32)]),
        out_shape=jax.ShapeDtypeStruct((n, d), F32),
        compiler_params=_cparams("arbitrary", "arbitrary"),
    )(seg_start, seg_count, h, dest, wts, w_gu, w_dn, s_gu, s_dn, xm, g2)


def _head_norm_rope(t, gain, cos, sin_a, sin_b, mul):
    r = lax.rsqrt(jnp.sum(t * t, axis=-1, keepdims=True) / QK_DIM + NORM_EPS)
    tn = t * r * gain
    half = ROPE_DIM // 2
    out = tn * cos + pltpu.roll(tn, LANES - half, 1) * sin_a + pltpu.roll(tn, half, 1) * sin_b
    return out * mul


def _mla_prep_kernel(x_ref, n1_ref, sc1_ref, sh1_ref, nk_ref, sck_ref, shk_ref,
                     wdq_ref, gql_ref, wuq_ref, gq_ref, wdown_ref, gkl_ref, wk_ref, gk_ref, wuv_ref,
                     cos_ref, sina_ref, sinb_ref, q_ref, k_ref, v_ref):
    xn = _rms(x_ref[...])
    hq = xn * n1_ref[...] * (1.0 + sc1_ref[...]) + sh1_ref[...]
    hk = xn * nk_ref[...] * (1.0 + sck_ref[...]) + shk_ref[...]
    cos, sin_a, sin_b = cos_ref[...], sina_ref[...], sinb_ref[...]
    n_heads = q_ref.shape[1] // LANES

    cq = _rms(jnp.dot(hq.astype(BF16), wdq_ref[...], preferred_element_type=F32)) * gql_ref[...]
    q_raw = jnp.dot(cq.astype(BF16), wuq_ref[...], preferred_element_type=F32)
    for h in range(n_heads):
        sl = slice(h * LANES, (h + 1) * LANES)
        q_ref[:, sl] = _head_norm_rope(q_raw[:, sl], gq_ref[...], cos, sin_a, sin_b,
                                       QK_DIM ** -0.5).astype(q_ref.dtype)

    down = jnp.dot(hk.astype(BF16), wdown_ref[...], preferred_element_type=F32)
    ckv = (_rms(down[:, :KV_RANK]) * gkl_ref[...]).astype(BF16)
    pe = down[:, KV_RANK:]
    pe_hi = pe.astype(BF16)
    pe_lo = (pe - pe_hi.astype(F32)).astype(BF16)
    k_raw = jnp.dot(jnp.concatenate([ckv, pe_hi, pe_lo], axis=1), wk_ref[...],
                    preferred_element_type=F32)
    for h in range(n_heads):
        sl = slice(h * LANES, (h + 1) * LANES)
        k_ref[:, sl] = _head_norm_rope(k_raw[:, sl], gk_ref[...], cos, sin_a, sin_b,
                                       1.0).astype(k_ref.dtype)
    v_ref[...] = jnp.dot(ckv, wuv_ref[...], preferred_element_type=F32).astype(v_ref.dtype)


def _pad_heads(w, n_heads, width):
    r = w.shape[0]
    return jnp.pad(w.reshape(r, n_heads, width), ((0, 0), (0, 0), (0, LANES - width))).reshape(r, n_heads * LANES)


def _mla_prep(xf, pos, n1, sc1, sh1, nk, sck, shk, w_dq, g_ql, w_uq, g_q, w_down, g_kl, w_uk, w_uv,
              g_k, seq):
    n, d = xf.shape
    n_heads = w_uv.shape[1] // V_DIM
    q_rank = w_dq.shape[1]
    hw = n_heads * LANES
    tm = 256
    per_b = seq // tm

    half = ROPE_DIM // 2
    inv = jnp.power(ROPE_THETA, -jnp.arange(half, dtype=F32) / half)
    ang = pos.astype(F32).reshape(n, 1) * inv
    cos, sin = jnp.cos(ang), jnp.sin(ang)
    ones = jnp.ones((n, NOPE_DIM), F32)
    zeros = lambda wdt: jnp.zeros((n, wdt), F32)
    tail = LANES - QK_DIM
    cos_t = jnp.concatenate([ones, cos, cos, zeros(tail)], axis=1)
    sina_t = jnp.concatenate([zeros(NOPE_DIM), -sin, zeros(half), zeros(tail)], axis=1)
    sinb_t = jnp.concatenate([zeros(NOPE_DIM), zeros(half), sin, zeros(tail)], axis=1)

    pad_gain = lambda g: jnp.pad(g.reshape(1, QK_DIM), ((0, 0), (0, tail)))
    wuq_p = _pad_heads(w_uq, n_heads, QK_DIM).astype(BF16)
    wdown_p = jnp.pad(w_down, ((0, 0), (0, LANES - ROPE_DIM))).astype(BF16)
    eye = jnp.pad(jnp.eye(ROPE_DIM, dtype=F32), ((0, LANES - ROPE_DIM), (NOPE_DIM, tail)))
    pe_rows = jnp.tile(eye, (1, n_heads))
    wk = jnp.concatenate([_pad_heads(w_uk, n_heads, NOPE_DIM), pe_rows, pe_rows], axis=0).astype(BF16)

    vec = pl.BlockSpec((None, 1, d), lambda i: (i // per_b, 0, 0))
    full = lambda a: pl.BlockSpec(a.shape, lambda i: (0,) * a.ndim)
    row = lambda wdt: pl.BlockSpec((tm, wdt), lambda i: (i, 0))
    args = [xf, n1, sc1, sh1, nk, sck, shk,
            w_dq.astype(BF16), g_ql.reshape(1, q_rank), wuq_p, pad_gain(g_q),
            wdown_p, g_kl.reshape(1, KV_RANK), wk, pad_gain(g_k), w_uv.astype(BF16),
            cos_t, sina_t, sinb_t]
    specs = [row(d), full(n1), vec, vec, full(nk), vec, vec] + [full(a) for a in args[7:16]] + [row(LANES)] * 3
    return pl.pallas_call(
        _mla_prep_kernel,
        grid=(n // tm,),
        in_specs=specs,
        out_specs=[row(hw), row(hw), row(n_heads * V_DIM)],
        out_shape=[jax.ShapeDtypeStruct((n, hw), BF16), jax.ShapeDtypeStruct((n, hw), BF16),
                   jax.ShapeDtypeStruct((n, n_heads * V_DIM), BF16)],
        compiler_params=_cparams("parallel"),
    )(*args)


def _causal_bias(seq):
    a = np.arange(QBLK)[:, None]
    c = np.arange(seq)[None, :]
    return np.where(c <= (seq - QBLK) + a, 0.0, MASKED).astype(np.float32)


def _mla_attn_kernel(q_ref, k_ref, v_ref, mask_ref, o_ref):
    seq = q_ref.shape[0]
    lane_lo = lax.broadcasted_iota(jnp.int32, (QBLK, LANES), 1) < V_DIM
    for r0 in range(0, seq, QBLK):
        w = r0 + QBLK
        off = _band_offset(seq, w)
        bias = mask_ref[:, off:off + w]
        vv = v_ref[0:w, :]
        o_a = _attend(q_ref[r0:w, :LANES], k_ref[0:w, :LANES], vv, bias)
        o_b = _attend(q_ref[r0:w, LANES:], k_ref[0:w, LANES:], vv, bias)
        o_ref[r0:w, :] = jnp.where(lane_lo, o_a, o_b).astype(o_ref.dtype)


def _mla_attention(q, k, v):
    bsz, seq, hw = q.shape
    n_pairs = hw // (2 * LANES)
    qk = pl.BlockSpec((None, seq, 2 * LANES), lambda b, h: (b, 0, h))
    vo = pl.BlockSpec((None, seq, LANES), lambda b, h: (b, 0, h))
    return pl.pallas_call(
        _mla_attn_kernel,
        grid=(bsz, n_pairs),
        in_specs=[qk, qk, vo, pl.BlockSpec((QBLK, seq), lambda b, h: (0, 0))],
        out_specs=vo,
        out_shape=jax.ShapeDtypeStruct(v.shape, BF16),
        compiler_params=_cparams("parallel", "parallel"),
    )(q, k, v, jnp.asarray(_causal_bias(seq)))


def kernel(x, c, pos, ada_w, ada_b, norm1_g, norm2_g, a_w_qkv, a_q_norm, a_k_norm, a_w_o, kv_ada_w, kv_ada_b, kv_norm_g, kv_w_down, kv_latent_norm, kv_w_uk, kv_w_uv, kv_k_norm, b_w_dq, b_q_latent_norm, b_w_uq, b_q_norm, b_w_o, moe_w_router, moe_router_bias, moe_w_gate_up, moe_w_down, moe_shared_gate_up, moe_shared_down):
    bsz, seq, d = x.shape
    n = bsz * seq
    depth = ada_w.shape[0]
    n_a = a_w_qkv.shape[0]
    assert seq % QBLK == 0
    xf = x.reshape(n, d)

    w_gu = moe_w_gate_up.astype(BF16)
    w_dn = moe_w_down.astype(BF16)
    s_gu = moe_shared_gate_up.astype(BF16)
    s_dn = moe_shared_down.astype(BF16)
    row = lambda g: g.reshape(1, d)

    for layer in range(depth):
        mod = _modulation(c, ada_w, ada_b, layer).reshape(bsz, 6, 1, d)
        sh1, sc1, g1, sh2, sc2, g2 = [mod[:, i] for i in range(6)]
        if layer < n_a:
            qkv = _qkv_proj(xf, row(norm1_g[layer]), sc1, sh1, a_w_qkv[layer].astype(BF16), seq)
            o = _dilated_attention(qkv.reshape(bsz, seq, -1), pos, a_q_norm[layer], a_k_norm[layer])
            w_o = a_w_o[layer]
        else:
            j = layer - n_a
            kmod = _modulation(c, kv_ada_w[None], kv_ada_b[None], 0).reshape(bsz, 2, 1, d)
            q, k, v = _mla_prep(xf, pos, row(norm1_g[layer]), sc1, sh1, row(kv_norm_g), kmod[:, 1],
                                kmod[:, 0], b_w_dq[j], b_q_latent_norm[j], b_w_uq[j], b_q_norm[j],
                                kv_w_down, kv_latent_norm, kv_w_uk, kv_w_uv, kv_k_norm, seq)
            o = _mla_attention(q.reshape(bsz, seq, -1), k.reshape(bsz, seq, -1),
                               v.reshape(bsz, seq, -1))
            w_o = b_w_o[j]
        xm, h2, logits = _post_attn(o.reshape(n, d), w_o.astype(BF16), xf, g1, row(norm2_g[layer]),
                                    sc2, sh2, moe_w_router[layer], seq)
        idx, w = _route(logits, moe_router_bias[layer])
        xf = _moe(h2, xm, g2, idx, w, w_gu, w_dn, s_gu, s_dn, layer, seq)
    return xf.reshape(bsz, seq, d)
```

```python
import numpy as np
import jax
import jax.numpy as jnp
from jax import lax
from jax.experimental import pallas as pl
from jax.experimental.pallas import tpu as pltpu

F32 = jnp.float32
BF16 = jnp.bfloat16

NORM_EPS = 1e-6
MASKED = -1e30

LANES = 128
SUBLANES = 8
VMEM_LIMIT = 56 * 1024 * 1024

A_HEAD_DIM = 64
DILATED_PAIRS = ((128, 1), (512, 4), (2048, 16))
QBLK = 256
NOPE_DIM = 64
ROPE_DIM = 32
QK_DIM = NOPE_DIM + ROPE_DIM
V_DIM = 64
KV_RANK = 256
ROPE_THETA = 10000.0
N_EXPERTS = 64
N_GROUPS = 8
TOPK_GROUPS = 4
TOP_K = 6
EXPERT_DIM = 256
ROUTED_SCALE = 2.5
MOE_TILE = 1024
EROWS = 128
CROWS = 256

ROW_TILE = 512


def _cparams(*sem):
    return pltpu.CompilerParams(dimension_semantics=sem, vmem_limit_bytes=VMEM_LIMIT)


def _silu(v):
    return v * jax.nn.sigmoid(v)


def _rms(v):
    return v * lax.rsqrt(jnp.mean(v * v, axis=-1, keepdims=True) + NORM_EPS)


def _dot_nt(a, b):
    return lax.dot_general(a, b, (((1,), (1,)), ((), ())), preferred_element_type=F32)


def _mod_kernel(c_ref, w_ref, b_ref, o_ref):
    cond = _silu(c_ref[...])
    o_ref[...] = jnp.dot(cond, w_ref[...], preferred_element_type=F32,
                         precision=lax.Precision.HIGHEST) + b_ref[...]


def _modulation(c, w, b, layer):
    bsz, d = c.shape
    m = w.shape[-1]
    tn = 1024
    return pl.pallas_call(
        _mod_kernel,
        grid=(m // tn,),
        in_specs=[pl.BlockSpec((bsz, d), lambda j: (0, 0)),
                  pl.BlockSpec((None, d, tn), lambda j: (layer, 0, j)),
                  pl.BlockSpec((None, 1, tn), lambda j: (layer, 0, j))],
        out_specs=pl.BlockSpec((bsz, tn), lambda j: (0, j)),
        out_shape=jax.ShapeDtypeStruct((bsz, m), F32),
        compiler_params=_cparams("arbitrary"),
    )(c, w, b.reshape(b.shape[0], 1, m))


def _qkv_kernel(x_ref, g_ref, sc_ref, sh_ref, w_ref, o_ref):
    h = _rms(x_ref[...]) * g_ref[...]
    h = h * (1.0 + sc_ref[...]) + sh_ref[...]
    o_ref[...] = jnp.dot(h.astype(BF16), w_ref[...], preferred_element_type=F32).astype(o_ref.dtype)


def _qkv_proj(xf, g, sc, sh, w, seq):
    n, d = xf.shape
    m = w.shape[1]
    tm = min(ROW_TILE, seq)
    per_b = seq // tm
    vec = pl.BlockSpec((None, 1, d), lambda i: (i // per_b, 0, 0))
    return pl.pallas_call(
        _qkv_kernel,
        grid=(n // tm,),
        in_specs=[pl.BlockSpec((tm, d), lambda i: (i, 0)),
                  pl.BlockSpec((1, d), lambda i: (0, 0)),
                  vec, vec,
                  pl.BlockSpec((d, m), lambda i: (0, 0))],
        out_specs=pl.BlockSpec((tm, m), lambda i: (i, 0)),
        out_shape=jax.ShapeDtypeStruct((n, m), BF16),
        compiler_params=_cparams("parallel"),
    )(xf, g, sc, sh, w)


def _attend(q, k, v, bias):
    s = _dot_nt(q, k) + bias
    m = jnp.max(s, axis=-1, keepdims=True)
    p = jnp.exp(s - m)
    l = jnp.sum(p, axis=-1, keepdims=True)
    return jnp.dot(p.astype(BF16), v, preferred_element_type=F32) / l


def _band_offset(seq, w):
    return seq - w


def _dilated_log_multiplicity(seq):
    a = np.arange(QBLK)[:, None]
    c = np.arange(seq)[None, :]
    d = (seq - QBLK) + a - c
    mult = np.zeros((QBLK, seq), np.int32)
    for win, dil in DILATED_PAIRS:
        mult += ((d >= 0) & (d % dil == 0) & (d <= win)).astype(np.int32)
    return np.where(mult > 0, np.log(np.maximum(mult, 1)), MASKED).astype(np.float32)


def _dil_attn_kernel(slopes_ref, q_ref, k_ref, v_ref, posr_ref, gq_ref, gk_ref, logm_ref,
                     o_ref, qa_ref, qb_ref, ks_ref):
    seq = q_ref.shape[0]
    hp = pl.program_id(1)
    lane_lo = lax.broadcasted_iota(jnp.int32, (QBLK, LANES), 1) < A_HEAD_DIM

    def head_norm(ref, r0, g_ref, mul):
        t = ref[r0:r0 + QBLK, :].astype(F32)
        sq = t * t
        ssa = jnp.sum(jnp.where(lane_lo, sq, 0.0), axis=-1, keepdims=True)
        ssb = jnp.sum(jnp.where(lane_lo, 0.0, sq), axis=-1, keepdims=True)
        r = jnp.where(lane_lo, lax.rsqrt(ssa / A_HEAD_DIM + NORM_EPS),
                      lax.rsqrt(ssb / A_HEAD_DIM + NORM_EPS))
        return t * r * (g_ref[...] * mul)

    for r0 in range(0, seq, QBLK):
        qn = head_norm(q_ref, r0, gq_ref, A_HEAD_DIM ** -0.5)
        qa_ref[r0:r0 + QBLK, :] = jnp.where(lane_lo, qn, 0.0).astype(BF16)
        qb_ref[r0:r0 + QBLK, :] = jnp.where(lane_lo, 0.0, qn).astype(BF16)
        ks_ref[r0:r0 + QBLK, :] = head_norm(k_ref, r0, gk_ref, 1.0).astype(BF16)

    slope_a = slopes_ref[2 * hp]
    slope_b = slopes_ref[2 * hp + 1]
    for r0 in range(0, seq, QBLK):
        w = r0 + QBLK
        off = _band_offset(seq, w)
        lm = logm_ref[:, off:off + w]
        kk = ks_ref[0:w, :]
        vv = v_ref[0:w, :]
        pkw = (posr_ref[:, 0:w] - posr_ref[:, r0:r0 + 1]).astype(F32)
        o_a = _attend(qa_ref[r0:w, :], kk, vv, lm + slope_a * pkw)
        o_b = _attend(qb_ref[r0:w, :], kk, vv, lm + slope_b * pkw)
        o_ref[r0:w, :] = jnp.where(lane_lo, o_a, o_b).astype(o_ref.dtype)


def _dilated_attention(qkv, pos, q_norm, k_norm):
    bsz, seq, three_d = qkv.shape
    d = three_d // 3
    n_pairs = d // LANES
    n_heads = d // A_HEAD_DIM
    slopes = jnp.exp2(-8.0 * jnp.arange(1, n_heads + 1, dtype=F32) / n_heads)
    logm = jnp.asarray(_dilated_log_multiplicity(seq))
    gq = jnp.tile(q_norm.reshape(1, A_HEAD_DIM), (1, 2))
    gk = jnp.tile(k_norm.reshape(1, A_HEAD_DIM), (1, 2))
    blk = lambda off: pl.BlockSpec((None, seq, LANES), lambda b, h: (b, 0, off + h))
    const2 = pl.BlockSpec((1, LANES), lambda b, h: (0, 0))
    return pl.pallas_call(
        _dil_attn_kernel,
        grid=(bsz, n_pairs),
        in_specs=[pl.BlockSpec(memory_space=pltpu.SMEM),
                  blk(0), blk(n_pairs), blk(2 * n_pairs),
                  pl.BlockSpec((None, 1, seq), lambda b, h: (b, 0, 0)),
                  const2, const2,
                  pl.BlockSpec((QBLK, seq), lambda b, h: (0, 0))],
        out_specs=pl.BlockSpec((None, seq, LANES), lambda b, h: (b, 0, h)),
        scratch_shapes=[pltpu.VMEM((seq, LANES), BF16), pltpu.VMEM((seq, LANES), BF16),
                        pltpu.VMEM((seq, LANES), BF16)],
        out_shape=jax.ShapeDtypeStruct((bsz, seq, d), BF16),
        compiler_params=_cparams("parallel", "parallel"),
    )(slopes, qkv, qkv, qkv, pos.reshape(bsz, 1, seq), gq, gk, logm)


def _post_attn_kernel(o_ref, wo_ref, x_ref, g1_ref, n2_ref, sc_ref, sh_ref, wr_ref,
                      xm_ref, h_ref, lg_ref):
    y = jnp.dot(o_ref[...], wo_ref[...], preferred_element_type=F32)
    xm = x_ref[...] + g1_ref[...] * y
    xm_ref[...] = xm
    h = _rms(xm) * n2_ref[...]
    h = h * (1.0 + sc_ref[...]) + sh_ref[...]
    h_ref[...] = h
    lg_ref[...] = lax.dot_general(wr_ref[...], h, (((1,), (1,)), ((), ())),
                                  preferred_element_type=F32, precision=lax.Precision.HIGHEST)


def _post_attn(o, w_o, xf, g1, n2, sc2, sh2, w_router, seq):
    n, d = xf.shape
    ne = w_router.shape[1]
    tm = min(ROW_TILE, seq)
    per_b = seq // tm
    vec = pl.BlockSpec((None, 1, d), lambda i: (i // per_b, 0, 0))
    row = lambda w: pl.BlockSpec((tm, w), lambda i: (i, 0))
    return pl.pallas_call(
        _post_attn_kernel,
        grid=(n // tm,),
        in_specs=[row(d), pl.BlockSpec((d, d), lambda i: (0, 0)), row(d), vec,
                  pl.BlockSpec((1, d), lambda i: (0, 0)), vec, vec,
                  pl.BlockSpec((ne, d), lambda i: (0, 0))],
        out_specs=[row(d), row(d), pl.BlockSpec((ne, tm), lambda i: (0, i))],
        out_shape=[jax.ShapeDtypeStruct((n, d), F32), jax.ShapeDtypeStruct((n, d), F32),
                   jax.ShapeDtypeStruct((ne, n), F32)],
        compiler_params=_cparams("parallel"),
    )(o, w_o, xf, g1, n2, sc2, sh2, w_router.T)


def _moe_rows(tile):
    rows = tile * TOP_K + N_EXPERTS * (SUBLANES - 1) + EROWS
    return -(-rows // SUBLANES) * SUBLANES


def _first_index(hit, index, limit):
    return jnp.min(jnp.where(hit, index, limit), axis=0, keepdims=True)


def _route_kernel(lg_ref, bias_ref, tri_ref, dest_ref, w_ref, seg_ref, cnt_ref):
    t = lg_ref.shape[1]
    per_g = N_EXPERTS // N_GROUPS
    sub = lax.broadcasted_iota(jnp.int32, (per_g, t), 0).astype(F32)
    neg = -jnp.inf
    scores, sel = [], []
    for g in range(N_GROUPS):
        rows = slice(g * per_g, (g + 1) * per_g)
        sc = jax.nn.sigmoid(lg_ref[rows, :])
        scores.append(sc)
        sel.append(sc + bias_ref[rows, :])

    gs = []
    for g in range(N_GROUPS):
        m1 = jnp.max(sel[g], axis=0, keepdims=True)
        first = _first_index(sel[g] == m1, sub, float(per_g))
        m2 = jnp.max(jnp.where(sub == first, neg, sel[g]), axis=0, keepdims=True)
        gs.append(m1 + m2)
    gsc = jnp.concatenate(gs, axis=0)
    gsub = lax.broadcasted_iota(jnp.int32, (N_GROUPS, t), 0).astype(F32)
    gkeep = jnp.zeros((N_GROUPS, t), F32)
    for _ in range(TOPK_GROUPS):
        m = jnp.max(gsc, axis=0, keepdims=True)
        pick = gsub == _first_index(gsc == m, gsub, float(N_GROUPS))
        gkeep = jnp.where(pick, 1.0, gkeep)
        gsc = jnp.where(pick, neg, gsc)

    eidx = [sub + float(g * per_g) for g in range(N_GROUPS)]
    masked = [jnp.where(jnp.broadcast_to(gkeep[g:g + 1, :], (per_g, t)) > 0.5, sel[g], neg)
              for g in range(N_GROUPS)]
    chosen = [jnp.zeros((per_g, t), F32) for _ in range(N_GROUPS)]
    wsel = [jnp.zeros((per_g, t), F32) for _ in range(N_GROUPS)]
    picked = []
    for _ in range(TOP_K):
        m = jnp.max(masked[0], axis=0, keepdims=True)
        for g in range(1, N_GROUPS):
            m = jnp.maximum(m, jnp.max(masked[g], axis=0, keepdims=True))
        e_first = _first_index(masked[0] == m, eidx[0], float(N_EXPERTS))
        for g in range(1, N_GROUPS):
            e_first = jnp.minimum(e_first, _first_index(masked[g] == m, eidx[g], float(N_EXPERTS)))
        picked.append(e_first)
        for g in range(N_GROUPS):
            pick = eidx[g] == e_first
            chosen[g] = jnp.where(pick, 1.0, chosen[g])
            wsel[g] = jnp.where(pick, scores[g], wsel[g])
            masked[g] = jnp.where(pick, neg, masked[g])

    denom = jnp.sum(wsel[0], axis=0, keepdims=True)
    for g in range(1, N_GROUPS):
        denom = denom + jnp.sum(wsel[g], axis=0, keepdims=True)

    c_all = jnp.concatenate(chosen, axis=0)
    rank = jnp.dot(c_all.astype(BF16), tri_ref[...], preferred_element_type=F32)
    counts = jnp.sum(c_all, axis=1, keepdims=True)
    padded = jnp.ceil(counts / SUBLANES) * SUBLANES
    below = (lax.broadcasted_iota(jnp.int32, (N_EXPERTS, N_EXPERTS), 1)
             < lax.broadcasted_iota(jnp.int32, (N_EXPERTS, N_EXPERTS), 0)).astype(F32)
    seg = jnp.dot(below, jnp.broadcast_to(padded, (N_EXPERTS, LANES)), preferred_element_type=F32,
                  precision=lax.Precision.HIGHEST)
    row_of = seg[:, :1] + rank

    dest_rows, w_rows = [], []
    for k in range(TOP_K):
        d_k = jnp.zeros((1, t), F32)
        w_k = jnp.zeros((1, t), F32)
        for g in range(N_GROUPS):
            hit = eidx[g] == picked[k]
            d_k = d_k + jnp.sum(jnp.where(hit, row_of[g * per_g:(g + 1) * per_g, :], 0.0), axis=0,
                                keepdims=True)
            w_k = w_k + jnp.sum(jnp.where(hit, wsel[g], 0.0), axis=0, keepdims=True)
        dest_rows.append(d_k)
        w_rows.append(w_k / denom * ROUTED_SCALE)
    fill = [jnp.zeros((SUBLANES - TOP_K, t), F32)]
    dest_ref[...] = jnp.concatenate(dest_rows + fill, axis=0).astype(jnp.int32)
    w_ref[...] = jnp.concatenate(w_rows + fill, axis=0)
    seg_ref[...] = seg.astype(jnp.int32)
    cnt_ref[...] = jnp.broadcast_to(counts, (N_EXPERTS, LANES)).astype(jnp.int32)


def _route(logits_t, router_bias, tile):
    ne, n = logits_t.shape
    nt = n // tile
    tri = jnp.asarray(np.triu(np.ones((tile, tile), np.float32), 1), dtype=BF16)
    per_tile = lambda rows, width: pl.BlockSpec((None, rows, width), lambda i: (i, 0, 0))
    dest, wts, seg, cnt = pl.pallas_call(
        _route_kernel,
        grid=(nt,),
        in_specs=[pl.BlockSpec((ne, tile), lambda i: (0, i)),
                  pl.BlockSpec((ne, 1), lambda i: (0, 0)),
                  pl.BlockSpec((tile, tile), lambda i: (0, 0))],
        out_specs=[per_tile(SUBLANES, tile), per_tile(SUBLANES, tile),
                   per_tile(ne, LANES), per_tile(ne, LANES)],
        out_shape=[jax.ShapeDtypeStruct((nt, SUBLANES, tile), jnp.int32),
                   jax.ShapeDtypeStruct((nt, SUBLANES, tile), F32),
                   jax.ShapeDtypeStruct((nt, ne, LANES), jnp.int32),
                   jax.ShapeDtypeStruct((nt, ne, LANES), jnp.int32)],
        compiler_params=_cparams("parallel"),
    )(logits_t, router_bias.astype(F32).reshape(ne, 1), tri)
    return seg[:, :, 0].reshape(-1), cnt[:, :, 0].reshape(-1), dest, wts


def _swiglu(xb, wgu_ref, wdn_ref):
    gu = jnp.dot(xb, wgu_ref[...], preferred_element_type=F32)
    half = wgu_ref.shape[1] // 2
    act = _silu(gu[:, :half]) * gu[:, half:]
    return jnp.dot(act.astype(BF16), wdn_ref[...], preferred_element_type=F32)


def _moe_kernel(start_ref, count_ref, h_ref, dest_ref, w_ref, wgu_ref, wdn_ref,
                sgu_ref, sdn_ref, xm_ref, g2_ref, o_ref, xy_ref, y_ref, comb_ref):
    t = pl.program_id(0)
    s = pl.program_id(1)
    tile = h_ref.shape[0]

    @pl.when(jnp.logical_and(t == 0, s == 0))
    def _clear():
        xy_ref[...] = jnp.zeros(xy_ref.shape, F32)

    @pl.when(s == 0)
    def _dispatch():
        def group(j, carry):
            r0 = pl.multiple_of(j * SUBLANES, SUBLANES)
            rows = h_ref[pl.ds(r0, SUBLANES), :]
            for u in range(SUBLANES):
                for k in range(TOP_K):
                    xy_ref[pl.ds(dest_ref[k, r0 + u], 1), :] = rows[u:u + 1, :]
            return carry

        lax.fori_loop(0, tile // SUBLANES, group, 0)

    @pl.when(s < N_EXPERTS)
    def _expert():
        start = start_ref[t * N_EXPERTS + s]
        count = count_ref[t * N_EXPERTS + s]

        def chunk(c, carry):
            base = pl.multiple_of(start + c * EROWS, SUBLANES)
            groups = (jnp.minimum(count - c * EROWS, EROWS) + SUBLANES - 1) // SUBLANES
            y_ref[...] = _swiglu(xy_ref[pl.ds(base, EROWS), :].astype(BF16), wgu_ref, wdn_ref)

            def put(j, c2):
                r0 = pl.multiple_of(j * SUBLANES, SUBLANES)
                xy_ref[pl.ds(base + r0, SUBLANES), :] = y_ref[pl.ds(r0, SUBLANES), :]
                return c2

            lax.fori_loop(0, groups, put, 0)
            return carry

        lax.fori_loop(0, (count + EROWS - 1) // EROWS, chunk, 0)

    @pl.when(s >= N_EXPERTS)
    def _combine():
        sb = s - N_EXPERTS

        def group(j, carry):
            r0 = pl.multiple_of(j * SUBLANES, SUBLANES)
            for u in range(SUBLANES):
                tok = sb * CROWS + r0 + u
                acc = w_ref[0, tok] * xy_ref[pl.ds(dest_ref[0, tok], 1), :]
                for k in range(1, TOP_K):
                    acc = acc + w_ref[k, tok] * xy_ref[pl.ds(dest_ref[k, tok], 1), :]
                comb_ref[pl.ds(r0 + u, 1), :] = acc
            return carry

        lax.fori_loop(0, CROWS // SUBLANES, group, 0)
        hb = h_ref[pl.ds(pl.multiple_of(sb * CROWS, CROWS), CROWS), :].astype(BF16)
        shared = _swiglu(hb, sgu_ref, sdn_ref)
        o_ref[...] = xm_ref[...] + g2_ref[...] * (comb_ref[...] + shared)


def _moe(h, xm, g2, logits_t, router_bias, w_gu, w_dn, s_gu, s_dn, layer, seq):
    n, d = h.shape
    tile = min(MOE_TILE, seq)
    nt = n // tile
    per_b = seq // tile
    csteps = tile // CROWS
    assert seq % tile == 0 and tile % CROWS == 0
    seg_start, seg_count, dest, wts = _route(logits_t, router_bias, tile)
    rows = _moe_rows(tile)
    ecl = lambda s: jnp.minimum(s, N_EXPERTS - 1)
    cst = lambda s: jnp.maximum(s - N_EXPERTS, 0)
    smem = pl.BlockSpec((None, SUBLANES, tile), lambda t, s, a, b: (t, 0, 0), memory_space=pltpu.SMEM)
    return pl.pallas_call(
        _moe_kernel,
        grid_spec=pltpu.PrefetchScalarGridSpec(
            num_scalar_prefetch=2,
            grid=(nt, N_EXPERTS + csteps),
            in_specs=[pl.BlockSpec((tile, d), lambda t, s, a, b: (t, 0)),
                      smem, smem,
                      pl.BlockSpec((None, None, d, 2 * EXPERT_DIM),
                                   lambda t, s, a, b: (layer, ecl(s), 0, 0)),
                      pl.BlockSpec((None, None, EXPERT_DIM, d),
                                   lambda t, s, a, b: (layer, ecl(s), 0, 0)),
                      pl.BlockSpec((None,) + s_gu.shape[1:], lambda t, s, a, b: (layer, 0, 0)),
                      pl.BlockSpec((None,) + s_dn.shape[1:], lambda t, s, a, b: (layer, 0, 0)),
                      pl.BlockSpec((CROWS, d), lambda t, s, a, b: (t * csteps + cst(s), 0)),
                      pl.BlockSpec((None, 1, d), lambda t, s, a, b: (t // per_b, 0, 0))],
            out_specs=pl.BlockSpec((CROWS, d), lambda t, s, a, b: (t * csteps + cst(s), 0)),
            scratch_shapes=[pltpu.VMEM((rows, d), F32), pltpu.VMEM((EROWS, d), F32),
                            pltpu.VMEM((CROWS, d), F32)]),
        out_shape=jax.ShapeDtypeStruct((n, d), F32),
        compiler_params=_cparams("arbitrary", "arbitrary"),
    )(seg_start, seg_count, h, dest, wts, w_gu, w_dn, s_gu, s_dn, xm, g2)


def _head_norm_rope(t, gain, cos, sin_a, sin_b, mul):
    r = lax.rsqrt(jnp.sum(t * t, axis=-1, keepdims=True) / QK_DIM + NORM_EPS)
    tn = t * r * gain
    half = ROPE_DIM // 2
    out = tn * cos + pltpu.roll(tn, LANES - half, 1) * sin_a + pltpu.roll(tn, half, 1) * sin_b
    return out * mul


def _mla_prep_kernel(x_ref, n1_ref, sc1_ref, sh1_ref, nk_ref, sck_ref, shk_ref,
                     wdq_ref, gql_ref, wuq_ref, gq_ref, wdown_ref, gkl_ref, wk_ref, gk_ref, wuv_ref,
                     cos_ref, sina_ref, sinb_ref, q_ref, k_ref, v_ref):
    xn = _rms(x_ref[...])
    hq = xn * n1_ref[...] * (1.0 + sc1_ref[...]) + sh1_ref[...]
    hk = xn * nk_ref[...] * (1.0 + sck_ref[...]) + shk_ref[...]
    cos, sin_a, sin_b = cos_ref[...], sina_ref[...], sinb_ref[...]
    n_heads = q_ref.shape[1] // LANES

    cq = _rms(jnp.dot(hq.astype(BF16), wdq_ref[...], preferred_element_type=F32)) * gql_ref[...]
    q_raw = jnp.dot(cq.astype(BF16), wuq_ref[...], preferred_element_type=F32)
    for h in range(n_heads):
        sl = slice(h * LANES, (h + 1) * LANES)
        q_ref[:, sl] = _head_norm_rope(q_raw[:, sl], gq_ref[...], cos, sin_a, sin_b,
                                       QK_DIM ** -0.5).astype(q_ref.dtype)

    down = jnp.dot(hk.astype(BF16), wdown_ref[...], preferred_element_type=F32)
    ckv = (_rms(down[:, :KV_RANK]) * gkl_ref[...]).astype(BF16)
    pe = down[:, KV_RANK:]
    pe_hi = pe.astype(BF16)
    pe_lo = (pe - pe_hi.astype(F32)).astype(BF16)
    k_raw = jnp.dot(jnp.concatenate([ckv, pe_hi, pe_lo], axis=1), wk_ref[...],
                    preferred_element_type=F32)
    for h in range(n_heads):
        sl = slice(h * LANES, (h + 1) * LANES)
        k_ref[:, sl] = _head_norm_rope(k_raw[:, sl], gk_ref[...], cos, sin_a, sin_b,
                                       1.0).astype(k_ref.dtype)
    v_ref[...] = jnp.dot(ckv, wuv_ref[...], preferred_element_type=F32).astype(v_ref.dtype)


def _pad_heads(w, n_heads, width):
    r = w.shape[0]
    return jnp.pad(w.reshape(r, n_heads, width), ((0, 0), (0, 0), (0, LANES - width))).reshape(r, n_heads * LANES)


def _mla_prep(xf, pos, n1, sc1, sh1, nk, sck, shk, w_dq, g_ql, w_uq, g_q, w_down, g_kl, w_uk, w_uv,
              g_k, seq):
    n, d = xf.shape
    n_heads = w_uv.shape[1] // V_DIM
    q_rank = w_dq.shape[1]
    hw = n_heads * LANES
    tm = 256
    per_b = seq // tm

    half = ROPE_DIM // 2
    inv = jnp.power(ROPE_THETA, -jnp.arange(half, dtype=F32) / half)
    ang = pos.astype(F32).reshape(n, 1) * inv
    cos, sin = jnp.cos(ang), jnp.sin(ang)
    ones = jnp.ones((n, NOPE_DIM), F32)
    zeros = lambda wdt: jnp.zeros((n, wdt), F32)
    tail = LANES - QK_DIM
    cos_t = jnp.concatenate([ones, cos, cos, zeros(tail)], axis=1)
    sina_t = jnp.concatenate([zeros(NOPE_DIM), -sin, zeros(half), zeros(tail)], axis=1)
    sinb_t = jnp.concatenate([zeros(NOPE_DIM), zeros(half), sin, zeros(tail)], axis=1)

    pad_gain = lambda g: jnp.pad(g.reshape(1, QK_DIM), ((0, 0), (0, tail)))
    wuq_p = _pad_heads(w_uq, n_heads, QK_DIM).astype(BF16)
    wdown_p = jnp.pad(w_down, ((0, 0), (0, LANES - ROPE_DIM))).astype(BF16)
    eye = jnp.pad(jnp.eye(ROPE_DIM, dtype=F32), ((0, LANES - ROPE_DIM), (NOPE_DIM, tail)))
    pe_rows = jnp.tile(eye, (1, n_heads))
    wk = jnp.concatenate([_pad_heads(w_uk, n_heads, NOPE_DIM), pe_rows, pe_rows], axis=0).astype(BF16)

    vec = pl.BlockSpec((None, 1, d), lambda i: (i // per_b, 0, 0))
    full = lambda a: pl.BlockSpec(a.shape, lambda i: (0,) * a.ndim)
    row = lambda wdt: pl.BlockSpec((tm, wdt), lambda i: (i, 0))
    args = [xf, n1, sc1, sh1, nk, sck, shk,
            w_dq.astype(BF16), g_ql.reshape(1, q_rank), wuq_p, pad_gain(g_q),
            wdown_p, g_kl.reshape(1, KV_RANK), wk, pad_gain(g_k), w_uv.astype(BF16),
            cos_t, sina_t, sinb_t]
    specs = [row(d), full(n1), vec, vec, full(nk), vec, vec] + [full(a) for a in args[7:16]] + [row(LANES)] * 3
    return pl.pallas_call(
        _mla_prep_kernel,
        grid=(n // tm,),
        in_specs=specs,
        out_specs=[row(hw), row(hw), row(n_heads * V_DIM)],
        out_shape=[jax.ShapeDtypeStruct((n, hw), BF16), jax.ShapeDtypeStruct((n, hw), BF16),
                   jax.ShapeDtypeStruct((n, n_heads * V_DIM), BF16)],
        compiler_params=_cparams("parallel"),
    )(*args)


def _causal_bias(seq):
    a = np.arange(QBLK)[:, None]
    c = np.arange(seq)[None, :]
    return np.where(c <= (seq - QBLK) + a, 0.0, MASKED).astype(np.float32)


def _mla_attn_kernel(q_ref, k_ref, v_ref, mask_ref, o_ref):
    seq = q_ref.shape[0]
    lane_lo = lax.broadcasted_iota(jnp.int32, (QBLK, LANES), 1) < V_DIM
    for r0 in range(0, seq, QBLK):
        w = r0 + QBLK
        off = _band_offset(seq, w)
        bias = mask_ref[:, off:off + w]
        vv = v_ref[0:w, :]
        o_a = _attend(q_ref[r0:w, :LANES], k_ref[0:w, :LANES], vv, bias)
        o_b = _attend(q_ref[r0:w, LANES:], k_ref[0:w, LANES:], vv, bias)
        o_ref[r0:w, :] = jnp.where(lane_lo, o_a, o_b).astype(o_ref.dtype)


def _mla_attention(q, k, v):
    bsz, seq, hw = q.shape
    n_pairs = hw // (2 * LANES)
    qk = pl.BlockSpec((None, seq, 2 * LANES), lambda b, h: (b, 0, h))
    vo = pl.BlockSpec((None, seq, LANES), lambda b, h: (b, 0, h))
    return pl.pallas_call(
        _mla_attn_kernel,
        grid=(bsz, n_pairs),
        in_specs=[qk, qk, vo, pl.BlockSpec((QBLK, seq), lambda b, h: (0, 0))],
        out_specs=vo,
        out_shape=jax.ShapeDtypeStruct(v.shape, BF16),
        compiler_params=_cparams("parallel", "parallel"),
    )(q, k, v, jnp.asarray(_causal_bias(seq)))


def kernel(x, c, pos, ada_w, ada_b, norm1_g, norm2_g, a_w_qkv, a_q_norm, a_k_norm, a_w_o, kv_ada_w, kv_ada_b, kv_norm_g, kv_w_down, kv_latent_norm, kv_w_uk, kv_w_uv, kv_k_norm, b_w_dq, b_q_latent_norm, b_w_uq, b_q_norm, b_w_o, moe_w_router, moe_router_bias, moe_w_gate_up, moe_w_down, moe_shared_gate_up, moe_shared_down):
    bsz, seq, d = x.shape
    n = bsz * seq
    depth = ada_w.shape[0]
    n_a = a_w_qkv.shape[0]
    assert seq % QBLK == 0
    xf = x.reshape(n, d)

    w_gu = moe_w_gate_up.astype(BF16)
    w_dn = moe_w_down.astype(BF16)
    s_gu = moe_shared_gate_up.astype(BF16)
    s_dn = moe_shared_down.astype(BF16)
    row = lambda g: g.reshape(1, d)

    for layer in range(depth):
        mod = _modulation(c, ada_w, ada_b, layer).reshape(bsz, 6, 1, d)
        sh1, sc1, g1, sh2, sc2, g2 = [mod[:, i] for i in range(6)]
        if layer < n_a:
            qkv = _qkv_proj(xf, row(norm1_g[layer]), sc1, sh1, a_w_qkv[layer].astype(BF16), seq)
            o = _dilated_attention(qkv.reshape(bsz, seq, -1), pos, a_q_norm[layer], a_k_norm[layer])
            w_o = a_w_o[layer]
        else:
            j = layer - n_a
            kmod = _modulation(c, kv_ada_w[None], kv_ada_b[None], 0).reshape(bsz, 2, 1, d)
            q, k, v = _mla_prep(xf, pos, row(norm1_g[layer]), sc1, sh1, row(kv_norm_g), kmod[:, 1],
                                kmod[:, 0], b_w_dq[j], b_q_latent_norm[j], b_w_uq[j], b_q_norm[j],
                                kv_w_down, kv_latent_norm, kv_w_uk, kv_w_uv, kv_k_norm, seq)
            o = _mla_attention(q.reshape(bsz, seq, -1), k.reshape(bsz, seq, -1),
                               v.reshape(bsz, seq, -1))
            w_o = b_w_o[j]
        xm, h2, logits = _post_attn(o.reshape(n, d), w_o.astype(BF16), xf, g1, row(norm2_g[layer]),
                                    sc2, sh2, moe_w_router[layer], seq)
        xf = _moe(h2, xm, g2, logits, moe_router_bias[layer], w_gu, w_dn, s_gu, s_dn, layer, seq)
    return xf.reshape(bsz, seq, d)
```

```python
import numpy as np
import jax
import jax.numpy as jnp
from jax import lax
from jax.experimental import pallas as pl
from jax.experimental.pallas import tpu as pltpu

F32 = jnp.float32
BF16 = jnp.bfloat16
U32 = jnp.uint32

NORM_EPS = 1e-6
MASKED = -1e30

LANES = 128
SUBLANES = 8
VMEM_LIMIT = 56 * 1024 * 1024

A_HEAD_DIM = 64
DILATED_PAIRS = ((128, 1), (512, 4), (2048, 16))
QBLK = 256
NOPE_DIM = 64
ROPE_DIM = 32
QK_DIM = NOPE_DIM + ROPE_DIM
V_DIM = 64
KV_RANK = 256
ROPE_THETA = 10000.0
N_EXPERTS = 64
N_GROUPS = 8
TOPK_GROUPS = 4
TOP_K = 6
EXPERT_DIM = 256
ROUTED_SCALE = 2.5
MOE_TILE = 2048
EROWS = 256
CROWS = 256

ROW_TILE = 512


def _cparams(*sem):
    return pltpu.CompilerParams(dimension_semantics=sem, vmem_limit_bytes=VMEM_LIMIT)


def _silu(v):
    return v * jax.nn.sigmoid(v)


def _rms(v):
    return v * lax.rsqrt(jnp.mean(v * v, axis=-1, keepdims=True) + NORM_EPS)


def _dot_nt(a, b):
    return lax.dot_general(a, b, (((1,), (1,)), ((), ())), preferred_element_type=F32)


def _mod_kernel(c_ref, w_ref, b_ref, o_ref):
    cond = _silu(c_ref[...])
    o_ref[...] = jnp.dot(cond, w_ref[...], preferred_element_type=F32,
                         precision=lax.Precision.HIGHEST) + b_ref[...]


def _modulation(c, w, b, layer):
    bsz, d = c.shape
    m = w.shape[-1]
    tn = 1024
    return pl.pallas_call(
        _mod_kernel,
        grid=(m // tn,),
        in_specs=[pl.BlockSpec((bsz, d), lambda j: (0, 0)),
                  pl.BlockSpec((None, d, tn), lambda j: (layer, 0, j)),
                  pl.BlockSpec((None, 1, tn), lambda j: (layer, 0, j))],
        out_specs=pl.BlockSpec((bsz, tn), lambda j: (0, j)),
        out_shape=jax.ShapeDtypeStruct((bsz, m), F32),
        compiler_params=_cparams("arbitrary"),
    )(c, w, b.reshape(b.shape[0], 1, m))


def _qkv_kernel(x_ref, g_ref, sc_ref, sh_ref, w_ref, o_ref):
    h = _rms(x_ref[...]) * g_ref[...]
    h = h * (1.0 + sc_ref[...]) + sh_ref[...]
    o_ref[...] = jnp.dot(h.astype(BF16), w_ref[...], preferred_element_type=F32).astype(o_ref.dtype)


def _qkv_proj(xf, g, sc, sh, w, seq):
    n, d = xf.shape
    m = w.shape[1]
    tm = min(ROW_TILE, seq)
    per_b = seq // tm
    vec = pl.BlockSpec((None, 1, d), lambda i: (i // per_b, 0, 0))
    return pl.pallas_call(
        _qkv_kernel,
        grid=(n // tm,),
        in_specs=[pl.BlockSpec((tm, d), lambda i: (i, 0)),
                  pl.BlockSpec((1, d), lambda i: (0, 0)),
                  vec, vec,
                  pl.BlockSpec((d, m), lambda i: (0, 0))],
        out_specs=pl.BlockSpec((tm, m), lambda i: (i, 0)),
        out_shape=jax.ShapeDtypeStruct((n, m), BF16),
        compiler_params=_cparams("parallel"),
    )(xf, g, sc, sh, w)


def _attend(q, k, v, bias):
    s = _dot_nt(q, k) + bias
    m = jnp.max(s, axis=-1, keepdims=True)
    p = jnp.exp(s - m)
    l = jnp.sum(p, axis=-1, keepdims=True)
    return jnp.dot(p.astype(BF16), v, preferred_element_type=F32) / l


def _band_offset(seq, w):
    return seq - w


def _dilated_log_multiplicity(seq):
    a = np.arange(QBLK)[:, None]
    c = np.arange(seq)[None, :]
    d = (seq - QBLK) + a - c
    mult = np.zeros((QBLK, seq), np.int32)
    for win, dil in DILATED_PAIRS:
        mult += ((d >= 0) & (d % dil == 0) & (d <= win)).astype(np.int32)
    return np.where(mult > 0, np.log(np.maximum(mult, 1)), MASKED).astype(np.float32)


def _dil_attn_kernel(slopes_ref, q_ref, k_ref, v_ref, posr_ref, gq_ref, gk_ref, logm_ref,
                     o_ref, qa_ref, qb_ref, ks_ref):
    seq = q_ref.shape[0]
    hp = pl.program_id(1)
    lane_lo = lax.broadcasted_iota(jnp.int32, (QBLK, LANES), 1) < A_HEAD_DIM

    def head_norm(ref, r0, g_ref, mul):
        t = ref[r0:r0 + QBLK, :].astype(F32)
        sq = t * t
        ssa = jnp.sum(jnp.where(lane_lo, sq, 0.0), axis=-1, keepdims=True)
        ssb = jnp.sum(jnp.where(lane_lo, 0.0, sq), axis=-1, keepdims=True)
        r = jnp.where(lane_lo, lax.rsqrt(ssa / A_HEAD_DIM + NORM_EPS),
                      lax.rsqrt(ssb / A_HEAD_DIM + NORM_EPS))
        return t * r * (g_ref[...] * mul)

    for r0 in range(0, seq, QBLK):
        qn = head_norm(q_ref, r0, gq_ref, A_HEAD_DIM ** -0.5)
        qa_ref[r0:r0 + QBLK, :] = jnp.where(lane_lo, qn, 0.0).astype(BF16)
        qb_ref[r0:r0 + QBLK, :] = jnp.where(lane_lo, 0.0, qn).astype(BF16)
        ks_ref[r0:r0 + QBLK, :] = head_norm(k_ref, r0, gk_ref, 1.0).astype(BF16)

    slope_a = slopes_ref[2 * hp]
    slope_b = slopes_ref[2 * hp + 1]
    for r0 in range(0, seq, QBLK):
        w = r0 + QBLK
        off = _band_offset(seq, w)
        lm = logm_ref[:, off:off + w]
        kk = ks_ref[0:w, :]
        vv = v_ref[0:w, :]
        pkw = (posr_ref[:, 0:w] - posr_ref[:, r0:r0 + 1]).astype(F32)
        o_a = _attend(qa_ref[r0:w, :], kk, vv, lm + slope_a * pkw)
        o_b = _attend(qb_ref[r0:w, :], kk, vv, lm + slope_b * pkw)
        o_ref[r0:w, :] = jnp.where(lane_lo, o_a, o_b).astype(o_ref.dtype)


def _dilated_attention(qkv, pos, q_norm, k_norm):
    bsz, seq, three_d = qkv.shape
    d = three_d // 3
    n_pairs = d // LANES
    n_heads = d // A_HEAD_DIM
    slopes = jnp.exp2(-8.0 * jnp.arange(1, n_heads + 1, dtype=F32) / n_heads)
    logm = jnp.asarray(_dilated_log_multiplicity(seq))
    gq = jnp.tile(q_norm.reshape(1, A_HEAD_DIM), (1, 2))
    gk = jnp.tile(k_norm.reshape(1, A_HEAD_DIM), (1, 2))
    blk = lambda off: pl.BlockSpec((None, seq, LANES), lambda b, h: (b, 0, off + h))
    const2 = pl.BlockSpec((1, LANES), lambda b, h: (0, 0))
    return pl.pallas_call(
        _dil_attn_kernel,
        grid=(bsz, n_pairs),
        in_specs=[pl.BlockSpec(memory_space=pltpu.SMEM),
                  blk(0), blk(n_pairs), blk(2 * n_pairs),
                  pl.BlockSpec((None, 1, seq), lambda b, h: (b, 0, 0)),
                  const2, const2,
                  pl.BlockSpec((QBLK, seq), lambda b, h: (0, 0))],
        out_specs=pl.BlockSpec((None, seq, LANES), lambda b, h: (b, 0, h)),
        scratch_shapes=[pltpu.VMEM((seq, LANES), BF16), pltpu.VMEM((seq, LANES), BF16),
                        pltpu.VMEM((seq, LANES), BF16)],
        out_shape=jax.ShapeDtypeStruct((bsz, seq, d), BF16),
        compiler_params=_cparams("parallel", "parallel"),
    )(slopes, qkv, qkv, qkv, pos.reshape(bsz, 1, seq), gq, gk, logm)


def _post_attn_kernel(o_ref, wo_ref, x_ref, g1_ref, n2_ref, sc_ref, sh_ref, wr_ref,
                      xm_ref, h_ref, lg_ref):
    y = jnp.dot(o_ref[...], wo_ref[...], preferred_element_type=F32)
    xm = x_ref[...] + g1_ref[...] * y
    xm_ref[...] = xm
    h = _rms(xm) * n2_ref[...]
    h = h * (1.0 + sc_ref[...]) + sh_ref[...]
    h_ref[...] = h.astype(h_ref.dtype)
    lg_ref[...] = lax.dot_general(wr_ref[...], h, (((1,), (1,)), ((), ())),
                                  preferred_element_type=F32, precision=lax.Precision.HIGHEST)


def _post_attn(o, w_o, xf, g1, n2, sc2, sh2, w_router, seq):
    n, d = xf.shape
    ne = w_router.shape[1]
    tm = min(ROW_TILE, seq)
    per_b = seq // tm
    vec = pl.BlockSpec((None, 1, d), lambda i: (i // per_b, 0, 0))
    row = lambda w: pl.BlockSpec((tm, w), lambda i: (i, 0))
    return pl.pallas_call(
        _post_attn_kernel,
        grid=(n // tm,),
        in_specs=[row(d), pl.BlockSpec((d, d), lambda i: (0, 0)), row(d), vec,
                  pl.BlockSpec((1, d), lambda i: (0, 0)), vec, vec,
                  pl.BlockSpec((ne, d), lambda i: (0, 0))],
        out_specs=[row(d), row(d), pl.BlockSpec((ne, tm), lambda i: (0, i))],
        out_shape=[jax.ShapeDtypeStruct((n, d), F32), jax.ShapeDtypeStruct((n, d), BF16),
                   jax.ShapeDtypeStruct((ne, n), F32)],
        compiler_params=_cparams("parallel"),
    )(o, w_o, xf, g1, n2, sc2, sh2, w_router.T)


def _moe_rows(tile):
    rows = tile * TOP_K + N_EXPERTS * (SUBLANES - 1) + EROWS
    return -(-rows // SUBLANES) * SUBLANES


def _first_index(hit, index, limit):
    return jnp.min(jnp.where(hit, index, limit), axis=0, keepdims=True)


def _route_kernel(lg_ref, bias_ref, tri_ref, dest_ref, w_ref, seg_ref, cnt_ref):
    t = lg_ref.shape[1]
    per_g = N_EXPERTS // N_GROUPS
    sub = lax.broadcasted_iota(jnp.int32, (per_g, t), 0).astype(F32)
    neg = -jnp.inf
    scores, sel = [], []
    for g in range(N_GROUPS):
        rows = slice(g * per_g, (g + 1) * per_g)
        sc = jax.nn.sigmoid(lg_ref[rows, :])
        scores.append(sc)
        sel.append(sc + bias_ref[rows, :])

    gs = []
    for g in range(N_GROUPS):
        m1 = jnp.max(sel[g], axis=0, keepdims=True)
        first = _first_index(sel[g] == m1, sub, float(per_g))
        m2 = jnp.max(jnp.where(sub == first, neg, sel[g]), axis=0, keepdims=True)
        gs.append(m1 + m2)
    gsc = jnp.concatenate(gs, axis=0)
    gsub = lax.broadcasted_iota(jnp.int32, (N_GROUPS, t), 0).astype(F32)
    gkeep = jnp.zeros((N_GROUPS, t), F32)
    for _ in range(TOPK_GROUPS):
        m = jnp.max(gsc, axis=0, keepdims=True)
        pick = gsub == _first_index(gsc == m, gsub, float(N_GROUPS))
        gkeep = jnp.where(pick, 1.0, gkeep)
        gsc = jnp.where(pick, neg, gsc)

    eidx = [sub + float(g * per_g) for g in range(N_GROUPS)]
    masked = [jnp.where(jnp.broadcast_to(gkeep[g:g + 1, :], (per_g, t)) > 0.5, sel[g], neg)
              for g in range(N_GROUPS)]
    chosen = [jnp.zeros((per_g, t), F32) for _ in range(N_GROUPS)]
    wsel = [jnp.zeros((per_g, t), F32) for _ in range(N_GROUPS)]
    picked = []
    for _ in range(TOP_K):
        m = jnp.max(masked[0], axis=0, keepdims=True)
        for g in range(1, N_GROUPS):
            m = jnp.maximum(m, jnp.max(masked[g], axis=0, keepdims=True))
        e_first = _first_index(masked[0] == m, eidx[0], float(N_EXPERTS))
        for g in range(1, N_GROUPS):
            e_first = jnp.minimum(e_first, _first_index(masked[g] == m, eidx[g], float(N_EXPERTS)))
        picked.append(e_first)
        for g in range(N_GROUPS):
            pick = eidx[g] == e_first
            chosen[g] = jnp.where(pick, 1.0, chosen[g])
            wsel[g] = jnp.where(pick, scores[g], wsel[g])
            masked[g] = jnp.where(pick, neg, masked[g])

    denom = jnp.sum(wsel[0], axis=0, keepdims=True)
    for g in range(1, N_GROUPS):
        denom = denom + jnp.sum(wsel[g], axis=0, keepdims=True)

    c_all = jnp.concatenate(chosen, axis=0)
    rank = jnp.dot(c_all.astype(BF16), tri_ref[...], preferred_element_type=F32)
    counts = jnp.sum(c_all, axis=1, keepdims=True)
    padded = jnp.ceil(counts / SUBLANES) * SUBLANES
    below = (lax.broadcasted_iota(jnp.int32, (N_EXPERTS, N_EXPERTS), 1)
             < lax.broadcasted_iota(jnp.int32, (N_EXPERTS, N_EXPERTS), 0)).astype(F32)
    seg = jnp.dot(below, jnp.broadcast_to(padded, (N_EXPERTS, LANES)), preferred_element_type=F32,
                  precision=lax.Precision.HIGHEST)
    row_of = seg[:, :1] + rank

    dest_rows, w_rows = [], []
    for k in range(TOP_K):
        d_k = jnp.zeros((1, t), F32)
        w_k = jnp.zeros((1, t), F32)
        for g in range(N_GROUPS):
            hit = eidx[g] == picked[k]
            d_k = d_k + jnp.sum(jnp.where(hit, row_of[g * per_g:(g + 1) * per_g, :], 0.0), axis=0,
                                keepdims=True)
            w_k = w_k + jnp.sum(jnp.where(hit, wsel[g], 0.0), axis=0, keepdims=True)
        dest_rows.append(d_k)
        w_rows.append(w_k / denom * ROUTED_SCALE)
    fill = [jnp.zeros((SUBLANES - TOP_K, t), F32)]
    dest_ref[...] = jnp.concatenate(dest_rows + fill, axis=0).astype(jnp.int32)
    w_ref[...] = jnp.concatenate(w_rows + fill, axis=0)
    seg_ref[...] = seg.astype(jnp.int32)
    cnt_ref[...] = jnp.broadcast_to(counts, (N_EXPERTS, LANES)).astype(jnp.int32)


def _route(logits_t, router_bias, tile):
    ne, n = logits_t.shape
    nt = n // tile
    tri = jnp.asarray(np.triu(np.ones((tile, tile), np.float32), 1), dtype=BF16)
    per_tile = lambda rows, width: pl.BlockSpec((None, rows, width), lambda i: (i, 0, 0))
    dest, wts, seg, cnt = pl.pallas_call(
        _route_kernel,
        grid=(nt,),
        in_specs=[pl.BlockSpec((ne, tile), lambda i: (0, i)),
                  pl.BlockSpec((ne, 1), lambda i: (0, 0)),
                  pl.BlockSpec((tile, tile), lambda i: (0, 0))],
        out_specs=[per_tile(SUBLANES, tile), per_tile(SUBLANES, tile),
                   per_tile(ne, LANES), per_tile(ne, LANES)],
        out_shape=[jax.ShapeDtypeStruct((nt, SUBLANES, tile), jnp.int32),
                   jax.ShapeDtypeStruct((nt, SUBLANES, tile), F32),
                   jax.ShapeDtypeStruct((nt, ne, LANES), jnp.int32),
                   jax.ShapeDtypeStruct((nt, ne, LANES), jnp.int32)],
        compiler_params=_cparams("parallel"),
    )(logits_t, router_bias.astype(F32).reshape(ne, 1), tri)
    return seg[:, :, 0].reshape(-1), cnt[:, :, 0].reshape(-1), dest, wts


def _swiglu(xb, wgu_ref, wdn_ref):
    gu = jnp.dot(xb, wgu_ref[...], preferred_element_type=F32)
    half = wgu_ref.shape[1] // 2
    act = _silu(gu[:, :half]) * gu[:, half:]
    return jnp.dot(act.astype(BF16), wdn_ref[...], preferred_element_type=F32)


def _split_rows(z, il_ref, r):
    nblk = il_ref.shape[0]
    for c in range(nblk):
        il_ref[c, 0:2 * r, :] = z[:, c * LANES:(c + 1) * LANES]
    halves = [[il_ref[c, pl.ds(par, r, stride=2), :] for c in range(nblk)] for par in (0, 1)]
    return jnp.concatenate(halves[0] + halves[1], axis=1)


def _pairs_to_rows(p, il_ref):
    return _split_rows(pltpu.bitcast(p, BF16).astype(F32), il_ref, p.shape[0]).astype(BF16)


def _rows_to_pairs(v, il_ref):
    r, w = v.shape[0], v.shape[1] // 2
    nblk = il_ref.shape[0]
    for c in range(nblk):
        il_ref[c, pl.ds(0, r, stride=2), :] = v[:, c * LANES:(c + 1) * LANES]
        il_ref[c, pl.ds(1, r, stride=2), :] = v[:, w + c * LANES:w + (c + 1) * LANES]
    z = jnp.concatenate([il_ref[c, 0:2 * r, :] for c in range(nblk)], axis=1)
    return pltpu.bitcast(z.astype(BF16), U32)


def _moe_kernel(start_ref, count_ref, hp_ref, hb_ref, dest_ref, wt_ref, wgu_ref, wdn_ref,
                sgu_ref, sdn_ref, xm_ref, g2_ref, o_ref, xy_ref, il_ref, yp_ref, stage_ref, zf_ref,
                comb_ref):
    t = pl.program_id(0)
    s = pl.program_id(1)
    tile = hp_ref.shape[0] // 2
    w = hp_ref.shape[1]

    @pl.when(jnp.logical_and(t == 0, s == 0))
    def _clear():
        xy_ref[...] = jnp.zeros(xy_ref.shape, U32)

    @pl.when(s == 0)
    def _dispatch():
        def group(j, carry):
            r0 = pl.multiple_of(j * SUBLANES, SUBLANES)
            rows = pltpu.bitcast(hp_ref[pl.ds(pl.multiple_of(j * 2 * SUBLANES, 2 * SUBLANES),
                                              2 * SUBLANES), :], U32)
            for u in range(SUBLANES):
                for k in range(TOP_K):
                    xy_ref[pl.ds(dest_ref[k, r0 + u], 1), :] = rows[u:u + 1, :]
            return carry

        lax.fori_loop(0, tile // SUBLANES, group, 0)

    @pl.when(s < N_EXPERTS)
    def _expert():
        start = start_ref[t * N_EXPERTS + s]
        count = count_ref[t * N_EXPERTS + s]

        def chunk(c, carry):
            base = pl.multiple_of(start + c * EROWS, SUBLANES)
            groups = (jnp.minimum(count - c * EROWS, EROWS) + SUBLANES - 1) // SUBLANES
            xb = _pairs_to_rows(xy_ref[pl.ds(base, EROWS), :], il_ref)
            yp_ref[...] = _rows_to_pairs(_swiglu(xb, wgu_ref, wdn_ref), il_ref)

            def put(j, c2):
                r0 = pl.multiple_of(j * SUBLANES, SUBLANES)
                xy_ref[pl.ds(base + r0, SUBLANES), :] = yp_ref[pl.ds(r0, SUBLANES), :]
                return c2

            lax.fori_loop(0, groups, put, 0)
            return carry

        lax.fori_loop(0, (count + EROWS - 1) // EROWS, chunk, 0)

    @pl.when(s >= N_EXPERTS)
    def _combine():
        sb = s - N_EXPERTS

        def group(j, carry):
            r0 = pl.multiple_of(j * SUBLANES, SUBLANES)
            t0 = pl.multiple_of(sb * CROWS + r0, SUBLANES)
            for k in range(TOP_K):
                for u in range(SUBLANES):
                    stage_ref[pl.ds(k * SUBLANES + u, 1), :] = xy_ref[pl.ds(dest_ref[k, t0 + u], 1), :]
            rows = _split_rows(pltpu.bitcast(stage_ref[...], BF16).astype(F32), zf_ref,
                               TOP_K * SUBLANES)
            wv = wt_ref[pl.ds(t0, SUBLANES), :]
            acc = wv[:, 0:1] * rows[0:SUBLANES, :]
            for k in range(1, TOP_K):
                acc = acc + wv[:, k:k + 1] * rows[k * SUBLANES:(k + 1) * SUBLANES, :]
            comb_ref[pl.ds(r0, SUBLANES), :] = acc
            return carry

        lax.fori_loop(0, CROWS // SUBLANES, group, 0)
        shared = _swiglu(hb_ref[...], sgu_ref, sdn_ref)
        o_ref[...] = xm_ref[...] + g2_ref[...] * (comb_ref[...] + shared)


def _moe(h, xm, g2, logits_t, router_bias, w_gu, w_dn, s_gu, s_dn, layer, seq):
    n, d = h.shape
    w = d // 2
    tile = min(MOE_TILE, seq)
    nt = n // tile
    per_b = seq // tile
    csteps = tile // CROWS
    assert seq % tile == 0 and tile % CROWS == 0
    seg_start, seg_count, dest, wts = _route(logits_t, router_bias, tile)
    rows = _moe_rows(tile)
    ecl = lambda s: jnp.minimum(s, N_EXPERTS - 1)
    cst = lambda s: jnp.maximum(s - N_EXPERTS, 0)
    crow = pl.BlockSpec((CROWS, d), lambda t, s, a, b: (t * csteps + cst(s), 0))
    return pl.pallas_call(
        _moe_kernel,
        grid_spec=pltpu.PrefetchScalarGridSpec(
            num_scalar_prefetch=2,
            grid=(nt, N_EXPERTS + csteps),
            in_specs=[pl.BlockSpec((2 * tile, w), lambda t, s, a, b: (t, 0)),
                      crow,
                      pl.BlockSpec((None, SUBLANES, tile), lambda t, s, a, b: (t, 0, 0),
                                   memory_space=pltpu.SMEM),
                      pl.BlockSpec((None, tile, SUBLANES), lambda t, s, a, b: (t, 0, 0)),
                      pl.BlockSpec((None, None, d, 2 * EXPERT_DIM),
                                   lambda t, s, a, b: (layer, ecl(s), 0, 0)),
                      pl.BlockSpec((None, None, EXPERT_DIM, d),
                                   lambda t, s, a, b: (layer, ecl(s), 0, 0)),
                      pl.BlockSpec((None,) + s_gu.shape[1:], lambda t, s, a, b: (layer, 0, 0)),
                      pl.BlockSpec((None,) + s_dn.shape[1:], lambda t, s, a, b: (layer, 0, 0)),
                      crow,
                      pl.BlockSpec((None, 1, d), lambda t, s, a, b: (t // per_b, 0, 0))],
            out_specs=crow,
            scratch_shapes=[pltpu.VMEM((rows, w), U32), pltpu.VMEM((w // LANES, 2 * EROWS, LANES), F32),
                            pltpu.VMEM((EROWS, w), U32), pltpu.VMEM((TOP_K * SUBLANES, w), U32),
                            pltpu.VMEM((w // LANES, 2 * TOP_K * SUBLANES, LANES), F32),
                            pltpu.VMEM((CROWS, d), F32)]),
        out_shape=jax.ShapeDtypeStruct((n, d), F32),
        compiler_params=_cparams("arbitrary", "arbitrary"),
    )(seg_start, seg_count, h.reshape(2 * n, w), h, dest, jnp.swapaxes(wts, 1, 2),
      w_gu, w_dn, s_gu, s_dn, xm, g2)


def _head_norm_rope(t, gain, cos, sin_a, sin_b, mul):
    r = lax.rsqrt(jnp.sum(t * t, axis=-1, keepdims=True) / QK_DIM + NORM_EPS)
    tn = t * r * gain
    half = ROPE_DIM // 2
    out = tn * cos + pltpu.roll(tn, LANES - half, 1) * sin_a + pltpu.roll(tn, half, 1) * sin_b
    return out * mul


def _mla_prep_kernel(x_ref, n1_ref, sc1_ref, sh1_ref, nk_ref, sck_ref, shk_ref,
                     wdq_ref, gql_ref, wuq_ref, gq_ref, wdown_ref, gkl_ref, wk_ref, gk_ref, wuv_ref,
                     cos_ref, sina_ref, sinb_ref, q_ref, k_ref, v_ref):
    xn = _rms(x_ref[...])
    hq = xn * n1_ref[...] * (1.0 + sc1_ref[...]) + sh1_ref[...]
    hk = xn * nk_ref[...] * (1.0 + sck_ref[...]) + shk_ref[...]
    cos, sin_a, sin_b = cos_ref[...], sina_ref[...], sinb_ref[...]
    n_heads = q_ref.shape[1] // LANES

    cq = _rms(jnp.dot(hq.astype(BF16), wdq_ref[...], preferred_element_type=F32)) * gql_ref[...]
    q_raw = jnp.dot(cq.astype(BF16), wuq_ref[...], preferred_element_type=F32)
    for h in range(n_heads):
        sl = slice(h * LANES, (h + 1) * LANES)
        q_ref[:, sl] = _head_norm_rope(q_raw[:, sl], gq_ref[...], cos, sin_a, sin_b,
                                       QK_DIM ** -0.5).astype(q_ref.dtype)

    down = jnp.dot(hk.astype(BF16), wdown_ref[...], preferred_element_type=F32)
    ckv = (_rms(down[:, :KV_RANK]) * gkl_ref[...]).astype(BF16)
    pe = down[:, KV_RANK:]
    pe_hi = pe.astype(BF16)
    pe_lo = (pe - pe_hi.astype(F32)).astype(BF16)
    k_raw = jnp.dot(jnp.concatenate([ckv, pe_hi, pe_lo], axis=1), wk_ref[...],
                    preferred_element_type=F32)
    for h in range(n_heads):
        sl = slice(h * LANES, (h + 1) * LANES)
        k_ref[:, sl] = _head_norm_rope(k_raw[:, sl], gk_ref[...], cos, sin_a, sin_b,
                                       1.0).astype(k_ref.dtype)
    v_ref[...] = jnp.dot(ckv, wuv_ref[...], preferred_element_type=F32).astype(v_ref.dtype)


def _pad_heads(w, n_heads, width):
    r = w.shape[0]
    return jnp.pad(w.reshape(r, n_heads, width), ((0, 0), (0, 0), (0, LANES - width))).reshape(r, n_heads * LANES)


def _mla_prep(xf, pos, n1, sc1, sh1, nk, sck, shk, w_dq, g_ql, w_uq, g_q, w_down, g_kl, w_uk, w_uv,
              g_k, seq):
    n, d = xf.shape
    n_heads = w_uv.shape[1] // V_DIM
    q_rank = w_dq.shape[1]
    hw = n_heads * LANES
    tm = 256
    per_b = seq // tm

    half = ROPE_DIM // 2
    inv = jnp.power(ROPE_THETA, -jnp.arange(half, dtype=F32) / half)
    ang = pos.astype(F32).reshape(n, 1) * inv
    cos, sin = jnp.cos(ang), jnp.sin(ang)
    ones = jnp.ones((n, NOPE_DIM), F32)
    zeros = lambda wdt: jnp.zeros((n, wdt), F32)
    tail = LANES - QK_DIM
    cos_t = jnp.concatenate([ones, cos, cos, zeros(tail)], axis=1)
    sina_t = jnp.concatenate([zeros(NOPE_DIM), -sin, zeros(half), zeros(tail)], axis=1)
    sinb_t = jnp.concatenate([zeros(NOPE_DIM), zeros(half), sin, zeros(tail)], axis=1)

    pad_gain = lambda g: jnp.pad(g.reshape(1, QK_DIM), ((0, 0), (0, tail)))
    wuq_p = _pad_heads(w_uq, n_heads, QK_DIM).astype(BF16)
    wdown_p = jnp.pad(w_down, ((0, 0), (0, LANES - ROPE_DIM))).astype(BF16)
    eye = jnp.pad(jnp.eye(ROPE_DIM, dtype=F32), ((0, LANES - ROPE_DIM), (NOPE_DIM, tail)))
    pe_rows = jnp.tile(eye, (1, n_heads))
    wk = jnp.concatenate([_pad_heads(w_uk, n_heads, NOPE_DIM), pe_rows, pe_rows], axis=0).astype(BF16)

    vec = pl.BlockSpec((None, 1, d), lambda i: (i // per_b, 0, 0))
    full = lambda a: pl.BlockSpec(a.shape, lambda i: (0,) * a.ndim)
    row = lambda wdt: pl.BlockSpec((tm, wdt), lambda i: (i, 0))
    args = [xf, n1, sc1, sh1, nk, sck, shk,
            w_dq.astype(BF16), g_ql.reshape(1, q_rank), wuq_p, pad_gain(g_q),
            wdown_p, g_kl.reshape(1, KV_RANK), wk, pad_gain(g_k), w_uv.astype(BF16),
            cos_t, sina_t, sinb_t]
    specs = [row(d), full(n1), vec, vec, full(nk), vec, vec] + [full(a) for a in args[7:16]] + [row(LANES)] * 3
    return pl.pallas_call(
        _mla_prep_kernel,
        grid=(n // tm,),
        in_specs=specs,
        out_specs=[row(hw), row(hw), row(n_heads * V_DIM)],
        out_shape=[jax.ShapeDtypeStruct((n, hw), BF16), jax.ShapeDtypeStruct((n, hw), BF16),
                   jax.ShapeDtypeStruct((n, n_heads * V_DIM), BF16)],
        compiler_params=_cparams("parallel"),
    )(*args)


def _causal_bias(seq):
    a = np.arange(QBLK)[:, None]
    c = np.arange(seq)[None, :]
    return np.where(c <= (seq - QBLK) + a, 0.0, MASKED).astype(np.float32)


def _mla_attn_kernel(q_ref, k_ref, v_ref, mask_ref, o_ref):
    seq = q_ref.shape[0]
    lane_lo = lax.broadcasted_iota(jnp.int32, (QBLK, LANES), 1) < V_DIM
    for r0 in range(0, seq, QBLK):
        w = r0 + QBLK
        off = _band_offset(seq, w)
        bias = mask_ref[:, off:off + w]
        vv = v_ref[0:w, :]
        o_a = _attend(q_ref[r0:w, :LANES], k_ref[0:w, :LANES], vv, bias)
        o_b = _attend(q_ref[r0:w, LANES:], k_ref[0:w, LANES:], vv, bias)
        o_ref[r0:w, :] = jnp.where(lane_lo, o_a, o_b).astype(o_ref.dtype)


def _mla_attention(q, k, v):
    bsz, seq, hw = q.shape
    n_pairs = hw // (2 * LANES)
    qk = pl.BlockSpec((None, seq, 2 * LANES), lambda b, h: (b, 0, h))
    vo = pl.BlockSpec((None, seq, LANES), lambda b, h: (b, 0, h))
    return pl.pallas_call(
        _mla_attn_kernel,
        grid=(bsz, n_pairs),
        in_specs=[qk, qk, vo, pl.BlockSpec((QBLK, seq), lambda b, h: (0, 0))],
        out_specs=vo,
        out_shape=jax.ShapeDtypeStruct(v.shape, BF16),
        compiler_params=_cparams("parallel", "parallel"),
    )(q, k, v, jnp.asarray(_causal_bias(seq)))


def kernel(x, c, pos, ada_w, ada_b, norm1_g, norm2_g, a_w_qkv, a_q_norm, a_k_norm, a_w_o, kv_ada_w, kv_ada_b, kv_norm_g, kv_w_down, kv_latent_norm, kv_w_uk, kv_w_uv, kv_k_norm, b_w_dq, b_q_latent_norm, b_w_uq, b_q_norm, b_w_o, moe_w_router, moe_router_bias, moe_w_gate_up, moe_w_down, moe_shared_gate_up, moe_shared_down):
    bsz, seq, d = x.shape
    n = bsz * seq
    depth = ada_w.shape[0]
    n_a = a_w_qkv.shape[0]
    assert seq % QBLK == 0
    xf = x.reshape(n, d)

    w_gu = moe_w_gate_up.astype(BF16)
    w_dn = moe_w_down.astype(BF16)
    s_gu = moe_shared_gate_up.astype(BF16)
    s_dn = moe_shared_down.astype(BF16)
    row = lambda g: g.reshape(1, d)

    for layer in range(depth):
        mod = _modulation(c, ada_w, ada_b, layer).reshape(bsz, 6, 1, d)
        sh1, sc1, g1, sh2, sc2, g2 = [mod[:, i] for i in range(6)]
        if layer < n_a:
            qkv = _qkv_proj(xf, row(norm1_g[layer]), sc1, sh1, a_w_qkv[layer].astype(BF16), seq)
            o = _dilated_attention(qkv.reshape(bsz, seq, -1), pos, a_q_norm[layer], a_k_norm[layer])
            w_o = a_w_o[layer]
        else:
            j = layer - n_a
            kmod = _modulation(c, kv_ada_w[None], kv_ada_b[None], 0).reshape(bsz, 2, 1, d)
            q, k, v = _mla_prep(xf, pos, row(norm1_g[layer]), sc1, sh1, row(kv_norm_g), kmod[:, 1],
                                kmod[:, 0], b_w_dq[j], b_q_latent_norm[j], b_w_uq[j], b_q_norm[j],
                                kv_w_down, kv_latent_norm, kv_w_uk, kv_w_uv, kv_k_norm, seq)
            o = _mla_attention(q.reshape(bsz, seq, -1), k.reshape(bsz, seq, -1),
                               v.reshape(bsz, seq, -1))
            w_o = b_w_o[j]
        xm, h2, logits = _post_attn(o.reshape(n, d), w_o.astype(BF16), xf, g1, row(norm2_g[layer]),
                                    sc2, sh2, moe_w_router[layer], seq)
        xf = _moe(h2, xm, g2, logits, moe_router_bias[layer], w_gu, w_dn, s_gu, s_dn, layer, seq)
    return xf.reshape(bsz, seq, d)
```

```python
import functools
import math

import numpy as np
import jax
import jax.numpy as jnp
from jax import lax
from jax.experimental import pallas as pl
from jax.experimental.pallas import tpu as pltpu

F32 = jnp.float32
BF16 = jnp.bfloat16
U32 = jnp.uint32

NORM_EPS = 1e-6
MASKED = -1e30
LOG2E = math.log2(math.e)

LANES = 128
SUBLANES = 8
VMEM_LIMIT = 56 * 1024 * 1024

A_HEAD_DIM = 64
DILATED_PAIRS = ((128, 1), (512, 4), (2048, 16))
QBLK = 256
NOPE_DIM = 64
ROPE_DIM = 32
QK_DIM = NOPE_DIM + ROPE_DIM
V_DIM = 64
KV_RANK = 256
ROPE_THETA = 10000.0
N_EXPERTS = 64
N_GROUPS = 8
TOPK_GROUPS = 4
TOP_K = 6
EXPERT_DIM = 256
ROUTED_SCALE = 2.5
MOE_TILE = 2048
EROWS = 256
CROWS = 256

ROW_TILE = 512


def _cparams(*sem):
    return pltpu.CompilerParams(dimension_semantics=sem, vmem_limit_bytes=VMEM_LIMIT)


def _silu(v):
    return v * jax.nn.sigmoid(v)


def _rms(v):
    return v * lax.rsqrt(jnp.mean(v * v, axis=-1, keepdims=True) + NORM_EPS)


def _dot_nt(a, b):
    return lax.dot_general(a, b, (((1,), (1,)), ((), ())), preferred_element_type=F32)


def _mod_kernel(c_ref, w_ref, b_ref, o_ref):
    cond = _silu(c_ref[...])
    o_ref[...] = jnp.dot(cond, w_ref[...], preferred_element_type=F32,
                         precision=lax.Precision.HIGHEST) + b_ref[...]


def _modulation(c, w, b, layer):
    bsz, d = c.shape
    m = w.shape[-1]
    tn = 1024
    return pl.pallas_call(
        _mod_kernel,
        grid=(m // tn,),
        in_specs=[pl.BlockSpec((bsz, d), lambda j: (0, 0)),
                  pl.BlockSpec((None, d, tn), lambda j: (layer, 0, j)),
                  pl.BlockSpec((None, 1, tn), lambda j: (layer, 0, j))],
        out_specs=pl.BlockSpec((bsz, tn), lambda j: (0, j)),
        out_shape=jax.ShapeDtypeStruct((bsz, m), F32),
        compiler_params=_cparams("arbitrary"),
    )(c, w, b.reshape(b.shape[0], 1, m))


def _qkv_kernel(x_ref, g_ref, sc_ref, sh_ref, w_ref, o_ref):
    h = _rms(x_ref[...]) * g_ref[...]
    h = h * (1.0 + sc_ref[...]) + sh_ref[...]
    o_ref[...] = jnp.dot(h.astype(BF16), w_ref[...], preferred_element_type=F32).astype(o_ref.dtype)


def _qkv_proj(xf, g, sc, sh, w, seq):
    n, d = xf.shape
    m = w.shape[1]
    tm = min(ROW_TILE, seq)
    per_b = seq // tm
    vec = pl.BlockSpec((None, 1, d), lambda i: (i // per_b, 0, 0))
    return pl.pallas_call(
        _qkv_kernel,
        grid=(n // tm,),
        in_specs=[pl.BlockSpec((tm, d), lambda i: (i, 0)),
                  pl.BlockSpec((1, d), lambda i: (0, 0)),
                  vec, vec,
                  pl.BlockSpec((d, m), lambda i: (0, 0))],
        out_specs=pl.BlockSpec((tm, m), lambda i: (i, 0)),
        out_shape=jax.ShapeDtypeStruct((n, m), BF16),
        compiler_params=_cparams("parallel"),
    )(xf, g, sc, sh, w)


def _attend(q, k, v, bias):
    s = _dot_nt(q, k) + bias
    m = jnp.max(s, axis=-1, keepdims=True)
    p = jnp.exp2(s - m)
    l = jnp.sum(p, axis=-1, keepdims=True)
    return jnp.dot(p.astype(BF16), v, preferred_element_type=F32) / l


def _split3(v):
    p1 = v.astype(BF16).astype(F32)
    p2 = (v - p1).astype(BF16).astype(F32)
    return p1, p2, v - p1 - p2


def _band_offset(seq, w):
    return seq - w


def _dilated_log_multiplicity(seq):
    a = np.arange(QBLK)[:, None]
    c = np.arange(seq)[None, :]
    d = (seq - QBLK) + a - c
    mult = np.zeros((QBLK, seq), np.int32)
    for win, dil in DILATED_PAIRS:
        mult += ((d >= 0) & (d % dil == 0) & (d <= win)).astype(np.int32)
    return np.where(mult > 0, np.log2(np.maximum(mult, 1)), MASKED).astype(np.float32)


N_FEAT = 9


def _dil_attn_kernel(q_ref, k_ref, v_ref, posc_ref, sfeat_ref, gq_ref, gk_ref, logm_ref,
                     o_ref, qa_ref, qb_ref, ks_ref):
    seq = q_ref.shape[0]
    lane = lax.broadcasted_iota(jnp.int32, (QBLK, LANES), 1)
    lane_lo = lane < A_HEAD_DIM
    piece = [functools.reduce(jnp.logical_or, [lane == i + 3 * j for j in range(N_FEAT // 3)])
             for i in range(3)]

    def head_norm(ref, r0, g_ref, mul):
        t = ref[r0:r0 + QBLK, :].astype(F32)
        sq = t * t
        ssa = jnp.sum(jnp.where(lane_lo, sq, 0.0), axis=-1, keepdims=True)
        ssb = jnp.sum(jnp.where(lane_lo, 0.0, sq), axis=-1, keepdims=True)
        r = jnp.where(lane_lo, lax.rsqrt(ssa / A_HEAD_DIM + NORM_EPS),
                      lax.rsqrt(ssb / A_HEAD_DIM + NORM_EPS))
        return t * r * (g_ref[...] * mul)

    rel0 = posc_ref[0:1, :]
    for r0 in range(0, seq, QBLK):
        rows = slice(r0, r0 + QBLK)
        qn = head_norm(q_ref, r0, gq_ref, A_HEAD_DIM ** -0.5 * LOG2E)
        qa_ref[rows, :LANES] = jnp.where(lane_lo, qn, 0.0).astype(BF16)
        qb_ref[rows, :LANES] = jnp.where(lane_lo, 0.0, qn).astype(BF16)
        qa_ref[rows, LANES:] = jnp.broadcast_to(sfeat_ref[0:1, :], (QBLK, LANES)).astype(BF16)
        qb_ref[rows, LANES:] = jnp.broadcast_to(sfeat_ref[1:2, :], (QBLK, LANES)).astype(BF16)
        ks_ref[rows, :LANES] = head_norm(k_ref, r0, gk_ref, 1.0).astype(BF16)
        p1, p2, p3 = _split3((posc_ref[rows, :] - rel0).astype(F32))
        feat = jnp.where(piece[0], p1, jnp.where(piece[1], p2, jnp.where(piece[2], p3, 0.0)))
        ks_ref[rows, LANES:] = feat.astype(BF16)

    for r0 in range(0, seq, QBLK):
        w = r0 + QBLK
        off = _band_offset(seq, w)
        lm = logm_ref[:, off:off + w]
        kk = ks_ref[0:w, :]
        vv = v_ref[0:w, :]
        o_a = _attend(qa_ref[r0:w, :], kk, vv, lm)
        o_b = _attend(qb_ref[r0:w, :], kk, vv, lm)
        o_ref[r0:w, :] = jnp.where(lane_lo, o_a, o_b).astype(o_ref.dtype)


def _dilated_attention(qkv, pos, q_norm, k_norm):
    bsz, seq, three_d = qkv.shape
    d = three_d // 3
    n_pairs = d // LANES
    n_heads = d // A_HEAD_DIM
    slopes = jnp.exp2(-8.0 * jnp.arange(1, n_heads + 1, dtype=F32) / n_heads) * LOG2E
    sfeat = jnp.repeat(jnp.stack(_split3(slopes), axis=1), 3, axis=1)
    sfeat = jnp.pad(sfeat, ((0, 0), (0, LANES - N_FEAT))).reshape(n_pairs, 2, LANES)
    logm = jnp.asarray(_dilated_log_multiplicity(seq))
    gq = jnp.tile(q_norm.reshape(1, A_HEAD_DIM), (1, 2))
    gk = jnp.tile(k_norm.reshape(1, A_HEAD_DIM), (1, 2))
    blk = lambda off: pl.BlockSpec((None, seq, LANES), lambda b, h: (b, 0, off + h))
    const2 = pl.BlockSpec((1, LANES), lambda b, h: (0, 0))
    ext = pltpu.VMEM((seq, 2 * LANES), BF16)
    return pl.pallas_call(
        _dil_attn_kernel,
        grid=(bsz, n_pairs),
        in_specs=[blk(0), blk(n_pairs), blk(2 * n_pairs),
                  pl.BlockSpec((None, seq, 1), lambda b, h: (b, 0, 0)),
                  pl.BlockSpec((None, 2, LANES), lambda b, h: (h, 0, 0)),
                  const2, const2,
                  pl.BlockSpec((QBLK, seq), lambda b, h: (0, 0))],
        out_specs=pl.BlockSpec((None, seq, LANES), lambda b, h: (b, 0, h)),
        scratch_shapes=[ext, ext, ext],
        out_shape=jax.ShapeDtypeStruct((bsz, seq, d), BF16),
        compiler_params=_cparams("parallel", "parallel"),
    )(qkv, qkv, qkv, pos.reshape(bsz, seq, 1), sfeat, gq, gk, logm)


def _post_attn_kernel(o_ref, wo_ref, x_ref, g1_ref, n2_ref, sc_ref, sh_ref, wr_ref,
                      xm_ref, h_ref, lg_ref):
    y = jnp.dot(o_ref[...], wo_ref[...], preferred_element_type=F32)
    xm = x_ref[...] + g1_ref[...] * y
    xm_ref[...] = xm
    h = _rms(xm) * n2_ref[...]
    h = h * (1.0 + sc_ref[...]) + sh_ref[...]
    h_ref[...] = h.astype(h_ref.dtype)
    lg_ref[...] = lax.dot_general(wr_ref[...], h, (((1,), (1,)), ((), ())),
                                  preferred_element_type=F32, precision=lax.Precision.HIGHEST)


def _post_attn(o, w_o, xf, g1, n2, sc2, sh2, w_router, seq):
    n, d = xf.shape
    ne = w_router.shape[1]
    tm = min(ROW_TILE, seq)
    per_b = seq // tm
    vec = pl.BlockSpec((None, 1, d), lambda i: (i // per_b, 0, 0))
    row = lambda w: pl.BlockSpec((tm, w), lambda i: (i, 0))
    return pl.pallas_call(
        _post_attn_kernel,
        grid=(n // tm,),
        in_specs=[row(d), pl.BlockSpec((d, d), lambda i: (0, 0)), row(d), vec,
                  pl.BlockSpec((1, d), lambda i: (0, 0)), vec, vec,
                  pl.BlockSpec((ne, d), lambda i: (0, 0))],
        out_specs=[row(d), row(d), pl.BlockSpec((ne, tm), lambda i: (0, i))],
        out_shape=[jax.ShapeDtypeStruct((n, d), F32), jax.ShapeDtypeStruct((n, d), BF16),
                   jax.ShapeDtypeStruct((ne, n), F32)],
        compiler_params=_cparams("parallel"),
    )(o, w_o, xf, g1, n2, sc2, sh2, w_router.T)


def _moe_rows(tile):
    rows = tile * TOP_K + N_EXPERTS * (SUBLANES - 1) + EROWS
    return -(-rows // SUBLANES) * SUBLANES


def _first_index(hit, index, limit):
    return jnp.min(jnp.where(hit, index, limit), axis=0, keepdims=True)


def _route_kernel(lg_ref, bias_ref, tri_ref, dest_ref, w_ref, seg_ref, cnt_ref):
    t = lg_ref.shape[1]
    per_g = N_EXPERTS // N_GROUPS
    sub = lax.broadcasted_iota(jnp.int32, (per_g, t), 0).astype(F32)
    neg = -jnp.inf
    scores, sel = [], []
    for g in range(N_GROUPS):
        rows = slice(g * per_g, (g + 1) * per_g)
        sc = jax.nn.sigmoid(lg_ref[rows, :])
        scores.append(sc)
        sel.append(sc + bias_ref[rows, :])

    gs = []
    for g in range(N_GROUPS):
        m1 = jnp.max(sel[g], axis=0, keepdims=True)
        first = _first_index(sel[g] == m1, sub, float(per_g))
        m2 = jnp.max(jnp.where(sub == first, neg, sel[g]), axis=0, keepdims=True)
        gs.append(m1 + m2)
    gsc = jnp.concatenate(gs, axis=0)
    gsub = lax.broadcasted_iota(jnp.int32, (N_GROUPS, t), 0).astype(F32)
    gkeep = jnp.zeros((N_GROUPS, t), F32)
    for _ in range(TOPK_GROUPS):
        m = jnp.max(gsc, axis=0, keepdims=True)
        pick = gsub == _first_index(gsc == m, gsub, float(N_GROUPS))
        gkeep = jnp.where(pick, 1.0, gkeep)
        gsc = jnp.where(pick, neg, gsc)

    eidx = [sub + float(g * per_g) for g in range(N_GROUPS)]
    masked = [jnp.where(jnp.broadcast_to(gkeep[g:g + 1, :], (per_g, t)) > 0.5, sel[g], neg)
              for g in range(N_GROUPS)]
    chosen = [jnp.zeros((per_g, t), F32) for _ in range(N_GROUPS)]
    wsel = [jnp.zeros((per_g, t), F32) for _ in range(N_GROUPS)]
    picked = []
    for _ in range(TOP_K):
        m = jnp.max(masked[0], axis=0, keepdims=True)
        for g in range(1, N_GROUPS):
            m = jnp.maximum(m, jnp.max(masked[g], axis=0, keepdims=True))
        e_first = _first_index(masked[0] == m, eidx[0], float(N_EXPERTS))
        for g in range(1, N_GROUPS):
            e_first = jnp.minimum(e_first, _first_index(masked[g] == m, eidx[g], float(N_EXPERTS)))
        picked.append(e_first)
        for g in range(N_GROUPS):
            pick = eidx[g] == e_first
            chosen[g] = jnp.where(pick, 1.0, chosen[g])
            wsel[g] = jnp.where(pick, scores[g], wsel[g])
            masked[g] = jnp.where(pick, neg, masked[g])

    denom = jnp.sum(wsel[0], axis=0, keepdims=True)
    for g in range(1, N_GROUPS):
        denom = denom + jnp.sum(wsel[g], axis=0, keepdims=True)

    c_all = jnp.concatenate(chosen, axis=0)
    rank = jnp.dot(c_all.astype(BF16), tri_ref[...], preferred_element_type=F32)
    counts = jnp.sum(c_all, axis=1, keepdims=True)
    padded = jnp.ceil(counts / SUBLANES) * SUBLANES
    below = (lax.broadcasted_iota(jnp.int32, (N_EXPERTS, N_EXPERTS), 1)
             < lax.broadcasted_iota(jnp.int32, (N_EXPERTS, N_EXPERTS), 0)).astype(F32)
    seg = jnp.dot(below, jnp.broadcast_to(padded, (N_EXPERTS, LANES)), preferred_element_type=F32,
                  precision=lax.Precision.HIGHEST)
    row_of = seg[:, :1] + rank

    dest_rows, w_rows = [], []
    for k in range(TOP_K):
        d_k = jnp.zeros((1, t), F32)
        w_k = jnp.zeros((1, t), F32)
        for g in range(N_GROUPS):
            hit = eidx[g] == picked[k]
            d_k = d_k + jnp.sum(jnp.where(hit, row_of[g * per_g:(g + 1) * per_g, :], 0.0), axis=0,
                                keepdims=True)
            w_k = w_k + jnp.sum(jnp.where(hit, wsel[g], 0.0), axis=0, keepdims=True)
        dest_rows.append(d_k)
        w_rows.append(w_k / denom * ROUTED_SCALE)
    fill = [jnp.zeros((SUBLANES - TOP_K, t), F32)]
    dest_ref[...] = jnp.concatenate(dest_rows + fill, axis=0).astype(jnp.int32)
    w_ref[...] = jnp.concatenate(w_rows + fill, axis=0)
    seg_ref[...] = seg.astype(jnp.int32)
    cnt_ref[...] = jnp.broadcast_to(counts, (N_EXPERTS, LANES)).astype(jnp.int32)


def _route(logits_t, router_bias, tile):
    ne, n = logits_t.shape
    nt = n // tile
    tri = jnp.asarray(np.triu(np.ones((tile, tile), np.float32), 1), dtype=BF16)
    per_tile = lambda rows, width: pl.BlockSpec((None, rows, width), lambda i: (i, 0, 0))
    dest, wts, seg, cnt = pl.pallas_call(
        _route_kernel,
        grid=(nt,),
        in_specs=[pl.BlockSpec((ne, tile), lambda i: (0, i)),
                  pl.BlockSpec((ne, 1), lambda i: (0, 0)),
                  pl.BlockSpec((tile, tile), lambda i: (0, 0))],
        out_specs=[per_tile(SUBLANES, tile), per_tile(SUBLANES, tile),
                   per_tile(ne, LANES), per_tile(ne, LANES)],
        out_shape=[jax.ShapeDtypeStruct((nt, SUBLANES, tile), jnp.int32),
                   jax.ShapeDtypeStruct((nt, SUBLANES, tile), F32),
                   jax.ShapeDtypeStruct((nt, ne, LANES), jnp.int32),
                   jax.ShapeDtypeStruct((nt, ne, LANES), jnp.int32)],
        compiler_params=_cparams("parallel"),
    )(logits_t, router_bias.astype(F32).reshape(ne, 1), tri)
    return seg[:, :, 0].reshape(-1), cnt[:, :, 0].reshape(-1), dest, wts


def _swiglu(xb, wgu_ref, wdn_ref):
    gu = jnp.dot(xb, wgu_ref[...], preferred_element_type=F32)
    half = wgu_ref.shape[1] // 2
    act = _silu(gu[:, :half]) * gu[:, half:]
    return jnp.dot(act.astype(BF16), wdn_ref[...], preferred_element_type=F32)


def _split_rows(z, il_ref, r):
    nblk = il_ref.shape[0]
    for c in range(nblk):
        il_ref[c, 0:2 * r, :] = z[:, c * LANES:(c + 1) * LANES]
    halves = [[il_ref[c, pl.ds(par, r, stride=2), :] for c in range(nblk)] for par in (0, 1)]
    return jnp.concatenate(halves[0] + halves[1], axis=1)


def _pairs_to_rows(p, il_ref):
    return _split_rows(pltpu.bitcast(p, BF16).astype(F32), il_ref, p.shape[0]).astype(BF16)


def _rows_to_pairs(v, il_ref):
    r, w = v.shape[0], v.shape[1] // 2
    nblk = il_ref.shape[0]
    for c in range(nblk):
        il_ref[c, pl.ds(0, r, stride=2), :] = v[:, c * LANES:(c + 1) * LANES]
        il_ref[c, pl.ds(1, r, stride=2), :] = v[:, w + c * LANES:w + (c + 1) * LANES]
    z = jnp.concatenate([il_ref[c, 0:2 * r, :] for c in range(nblk)], axis=1)
    return pltpu.bitcast(z.astype(BF16), U32)


def _dest_col(half, k, u):
    return (half * SUBLANES + k) * SUBLANES + u


def _moe_kernel(start_ref, count_ref, hp_ref, hb_ref, dest_ref, wt_ref, wgu_ref, wdn_ref,
                sgu_ref, sdn_ref, xm_ref, g2_ref, o_ref, xy_ref, il_ref, yp_ref, stage_ref, zf_ref,
                comb_ref):
    t = pl.program_id(0)
    s = pl.program_id(1)
    tile = hp_ref.shape[0] // 2

    @pl.when(jnp.logical_and(t == 0, s == 0))
    def _clear():
        xy_ref[...] = jnp.zeros(xy_ref.shape, U32)

    @pl.when(s == 0)
    def _dispatch():
        def group(j, carry):
            for half in range(2):
                p0 = pl.multiple_of(j * 4 * SUBLANES + half * 2 * SUBLANES, 2 * SUBLANES)
                rows = pltpu.bitcast(hp_ref[pl.ds(p0, 2 * SUBLANES), :], U32)
                for u in range(SUBLANES):
                    for k in range(TOP_K):
                        xy_ref[pl.ds(dest_ref[j, _dest_col(half, k, u)], 1), :] = rows[u:u + 1, :]
            return carry

        lax.fori_loop(0, tile // (2 * SUBLANES), group, 0)

    @pl.when(s < N_EXPERTS)
    def _expert():
        start = start_ref[t * N_EXPERTS + s]
        count = count_ref[t * N_EXPERTS + s]

        def chunk(c, carry):
            base = pl.multiple_of(start + c * EROWS, SUBLANES)
            groups = (jnp.minimum(count - c * EROWS, EROWS) + SUBLANES - 1) // SUBLANES
            xb = _pairs_to_rows(xy_ref[pl.ds(base, EROWS), :], il_ref)
            yp_ref[...] = _rows_to_pairs(_swiglu(xb, wgu_ref, wdn_ref), il_ref)

            def put(j, c2):
                r0 = pl.multiple_of(j * SUBLANES, SUBLANES)
                xy_ref[pl.ds(base + r0, SUBLANES), :] = yp_ref[pl.ds(r0, SUBLANES), :]
                return c2

            lax.fori_loop(0, groups, put, 0)
            return carry

        lax.fori_loop(0, (count + EROWS - 1) // EROWS, chunk, 0)

    @pl.when(s >= N_EXPERTS)
    def _combine():
        sb = s - N_EXPERTS

        def group(j, carry):
            g2 = sb * (CROWS // (2 * SUBLANES)) + j
            for half in range(2):
                r0 = pl.multiple_of(j * 2 * SUBLANES + half * SUBLANES, SUBLANES)
                t0 = pl.multiple_of(sb * CROWS + r0, SUBLANES)
                for k in range(TOP_K):
                    for u in range(SUBLANES):
                        stage_ref[pl.ds(k * SUBLANES + u, 1), :] = (
                            xy_ref[pl.ds(dest_ref[g2, _dest_col(half, k, u)], 1), :])
                rows = _split_rows(pltpu.bitcast(stage_ref[...], BF16).astype(F32), zf_ref,
                                   TOP_K * SUBLANES)
                wv = wt_ref[pl.ds(t0, SUBLANES), :]
                acc = wv[:, 0:1] * rows[0:SUBLANES, :]
                for k in range(1, TOP_K):
                    acc = acc + wv[:, k:k + 1] * rows[k * SUBLANES:(k + 1) * SUBLANES, :]
                comb_ref[pl.ds(r0, SUBLANES), :] = acc
            return carry

        lax.fori_loop(0, CROWS // (2 * SUBLANES), group, 0)
        shared = _swiglu(hb_ref[...], sgu_ref, sdn_ref)
        o_ref[...] = xm_ref[...] + g2_ref[...] * (comb_ref[...] + shared)


def _moe(h, xm, g2, logits_t, router_bias, w_gu, w_dn, s_gu, s_dn, layer, seq):
    n, d = h.shape
    w = d // 2
    tile = min(MOE_TILE, seq)
    nt = n // tile
    per_b = seq // tile
    csteps = tile // CROWS
    assert seq % tile == 0 and tile % CROWS == 0
    seg_start, seg_count, dest, wts = _route(logits_t, router_bias, tile)
    dest_rows = dest.reshape(nt, SUBLANES, tile // (2 * SUBLANES), 2, SUBLANES)
    dest_rows = dest_rows.transpose(0, 2, 3, 1, 4).reshape(nt, tile // (2 * SUBLANES), -1)
    rows = _moe_rows(tile)
    ecl = lambda s: jnp.minimum(s, N_EXPERTS - 1)
    cst = lambda s: jnp.maximum(s - N_EXPERTS, 0)
    crow = pl.BlockSpec((CROWS, d), lambda t, s, a, b: (t * csteps + cst(s), 0))
    return pl.pallas_call(
        _moe_kernel,
        grid_spec=pltpu.PrefetchScalarGridSpec(
            num_scalar_prefetch=2,
            grid=(nt, N_EXPERTS + csteps),
            in_specs=[pl.BlockSpec((2 * tile, w), lambda t, s, a, b: (t, 0)),
                      crow,
                      pl.BlockSpec((None, tile // (2 * SUBLANES), 2 * SUBLANES * SUBLANES),
                                   lambda t, s, a, b: (t, 0, 0), memory_space=pltpu.SMEM),
                      pl.BlockSpec((None, tile, SUBLANES), lambda t, s, a, b: (t, 0, 0)),
                      pl.BlockSpec((None, None, d, 2 * EXPERT_DIM),
                                   lambda t, s, a, b: (layer, ecl(s), 0, 0)),
                      pl.BlockSpec((None, None, EXPERT_DIM, d),
                                   lambda t, s, a, b: (layer, ecl(s), 0, 0)),
                      pl.BlockSpec((None,) + s_gu.shape[1:], lambda t, s, a, b: (layer, 0, 0)),
                      pl.BlockSpec((None,) + s_dn.shape[1:], lambda t, s, a, b: (layer, 0, 0)),
                      crow,
                      pl.BlockSpec((None, 1, d), lambda t, s, a, b: (t // per_b, 0, 0))],
            out_specs=crow,
            scratch_shapes=[pltpu.VMEM((rows, w), U32), pltpu.VMEM((w // LANES, 2 * EROWS, LANES), F32),
                            pltpu.VMEM((EROWS, w), U32), pltpu.VMEM((TOP_K * SUBLANES, w), U32),
                            pltpu.VMEM((w // LANES, 2 * TOP_K * SUBLANES, LANES), F32),
                            pltpu.VMEM((CROWS, d), F32)]),
        out_shape=jax.ShapeDtypeStruct((n, d), F32),
        compiler_params=_cparams("arbitrary", "arbitrary"),
    )(seg_start, seg_count, h.reshape(2 * n, w), h, dest_rows, jnp.swapaxes(wts, 1, 2),
      w_gu, w_dn, s_gu, s_dn, xm, g2)


def _head_norm_rope(t, gain, cos, sin_a, sin_b, mul):
    r = lax.rsqrt(jnp.sum(t * t, axis=-1, keepdims=True) / QK_DIM + NORM_EPS)
    tn = t * r * gain
    half = ROPE_DIM // 2
    out = tn * cos + pltpu.roll(tn, LANES - half, 1) * sin_a + pltpu.roll(tn, half, 1) * sin_b
    return out * mul


def _mla_prep_kernel(x_ref, n1_ref, sc1_ref, sh1_ref, nk_ref, sck_ref, shk_ref,
                     wdq_ref, gql_ref, wuq_ref, gq_ref, wdown_ref, gkl_ref, wk_ref, gk_ref, wuv_ref,
                     cos_ref, sina_ref, sinb_ref, q_ref, k_ref, v_ref):
    xn = _rms(x_ref[...])
    hq = xn * n1_ref[...] * (1.0 + sc1_ref[...]) + sh1_ref[...]
    hk = xn * nk_ref[...] * (1.0 + sck_ref[...]) + shk_ref[...]
    cos, sin_a, sin_b = cos_ref[...], sina_ref[...], sinb_ref[...]
    n_heads = q_ref.shape[1] // LANES

    cq = _rms(jnp.dot(hq.astype(BF16), wdq_ref[...], preferred_element_type=F32)) * gql_ref[...]
    q_raw = jnp.dot(cq.astype(BF16), wuq_ref[...], preferred_element_type=F32)
    for h in range(n_heads):
        sl = slice(h * LANES, (h + 1) * LANES)
        q_ref[:, sl] = _head_norm_rope(q_raw[:, sl], gq_ref[...], cos, sin_a, sin_b,
                                       QK_DIM ** -0.5 * LOG2E).astype(q_ref.dtype)

    down = jnp.dot(hk.astype(BF16), wdown_ref[...], preferred_element_type=F32)
    ckv = (_rms(down[:, :KV_RANK]) * gkl_ref[...]).astype(BF16)
    pe = down[:, KV_RANK:]
    pe_hi = pe.astype(BF16)
    pe_lo = (pe - pe_hi.astype(F32)).astype(BF16)
    k_raw = jnp.dot(jnp.concatenate([ckv, pe_hi, pe_lo], axis=1), wk_ref[...],
                    preferred_element_type=F32)
    for h in range(n_heads):
        sl = slice(h * LANES, (h + 1) * LANES)
        k_ref[:, sl] = _head_norm_rope(k_raw[:, sl], gk_ref[...], cos, sin_a, sin_b,
                                       1.0).astype(k_ref.dtype)
    v_ref[...] = jnp.dot(ckv, wuv_ref[...], preferred_element_type=F32).astype(v_ref.dtype)


def _pad_heads(w, n_heads, width):
    r = w.shape[0]
    return jnp.pad(w.reshape(r, n_heads, width), ((0, 0), (0, 0), (0, LANES - width))).reshape(r, n_heads * LANES)


def _mla_prep(xf, pos, n1, sc1, sh1, nk, sck, shk, w_dq, g_ql, w_uq, g_q, w_down, g_kl, w_uk, w_uv,
              g_k, seq):
    n, d = xf.shape
    n_heads = w_uv.shape[1] // V_DIM
    q_rank = w_dq.shape[1]
    hw = n_heads * LANES
    tm = 256
    per_b = seq // tm

    half = ROPE_DIM // 2
    inv = jnp.power(ROPE_THETA, -jnp.arange(half, dtype=F32) / half)
    ang = pos.astype(F32).reshape(n, 1) * inv
    cos, sin = jnp.cos(ang), jnp.sin(ang)
    ones = jnp.ones((n, NOPE_DIM), F32)
    zeros = lambda wdt: jnp.zeros((n, wdt), F32)
    tail = LANES - QK_DIM
    cos_t = jnp.concatenate([ones, cos, cos, zeros(tail)], axis=1)
    sina_t = jnp.concatenate([zeros(NOPE_DIM), -sin, zeros(half), zeros(tail)], axis=1)
    sinb_t = jnp.concatenate([zeros(NOPE_DIM), zeros(half), sin, zeros(tail)], axis=1)

    pad_gain = lambda g: jnp.pad(g.reshape(1, QK_DIM), ((0, 0), (0, tail)))
    wuq_p = _pad_heads(w_uq, n_heads, QK_DIM).astype(BF16)
    wdown_p = jnp.pad(w_down, ((0, 0), (0, LANES - ROPE_DIM))).astype(BF16)
    eye = jnp.pad(jnp.eye(ROPE_DIM, dtype=F32), ((0, LANES - ROPE_DIM), (NOPE_DIM, tail)))
    pe_rows = jnp.tile(eye, (1, n_heads))
    wk = jnp.concatenate([_pad_heads(w_uk, n_heads, NOPE_DIM), pe_rows, pe_rows], axis=0).astype(BF16)

    vec = pl.BlockSpec((None, 1, d), lambda i: (i // per_b, 0, 0))
    full = lambda a: pl.BlockSpec(a.shape, lambda i: (0,) * a.ndim)
    row = lambda wdt: pl.BlockSpec((tm, wdt), lambda i: (i, 0))
    args = [xf, n1, sc1, sh1, nk, sck, shk,
            w_dq.astype(BF16), g_ql.reshape(1, q_rank), wuq_p, pad_gain(g_q),
            wdown_p, g_kl.reshape(1, KV_RANK), wk, pad_gain(g_k), w_uv.astype(BF16),
            cos_t, sina_t, sinb_t]
    specs = [row(d), full(n1), vec, vec, full(nk), vec, vec] + [full(a) for a in args[7:16]] + [row(LANES)] * 3
    return pl.pallas_call(
        _mla_prep_kernel,
        grid=(n // tm,),
        in_specs=specs,
        out_specs=[row(hw), row(hw), row(n_heads * V_DIM)],
        out_shape=[jax.ShapeDtypeStruct((n, hw), BF16), jax.ShapeDtypeStruct((n, hw), BF16),
                   jax.ShapeDtypeStruct((n, n_heads * V_DIM), BF16)],
        compiler_params=_cparams("parallel"),
    )(*args)


def _causal_bias(seq):
    a = np.arange(QBLK)[:, None]
    c = np.arange(seq)[None, :]
    return np.where(c <= (seq - QBLK) + a, 0.0, MASKED).astype(np.float32)


def _mla_attn_kernel(q_ref, k_ref, v_ref, mask_ref, o_ref):
    seq = q_ref.shape[0]
    lane_lo = lax.broadcasted_iota(jnp.int32, (QBLK, LANES), 1) < V_DIM
    for r0 in range(0, seq, QBLK):
        w = r0 + QBLK
        off = _band_offset(seq, w)
        bias = mask_ref[:, off:off + w]
        vv = v_ref[0:w, :]
        o_a = _attend(q_ref[r0:w, :LANES], k_ref[0:w, :LANES], vv, bias)
        o_b = _attend(q_ref[r0:w, LANES:], k_ref[0:w, LANES:], vv, bias)
        o_ref[r0:w, :] = jnp.where(lane_lo, o_a, o_b).astype(o_ref.dtype)


def _mla_attention(q, k, v):
    bsz, seq, hw = q.shape
    n_pairs = hw // (2 * LANES)
    qk = pl.BlockSpec((None, seq, 2 * LANES), lambda b, h: (b, 0, h))
    vo = pl.BlockSpec((None, seq, LANES), lambda b, h: (b, 0, h))
    return pl.pallas_call(
        _mla_attn_kernel,
        grid=(bsz, n_pairs),
        in_specs=[qk, qk, vo, pl.BlockSpec((QBLK, seq), lambda b, h: (0, 0))],
        out_specs=vo,
        out_shape=jax.ShapeDtypeStruct(v.shape, BF16),
        compiler_params=_cparams("parallel", "parallel"),
    )(q, k, v, jnp.asarray(_causal_bias(seq)))


def kernel(x, c, pos, ada_w, ada_b, norm1_g, norm2_g, a_w_qkv, a_q_norm, a_k_norm, a_w_o, kv_ada_w, kv_ada_b, kv_norm_g, kv_w_down, kv_latent_norm, kv_w_uk, kv_w_uv, kv_k_norm, b_w_dq, b_q_latent_norm, b_w_uq, b_q_norm, b_w_o, moe_w_router, moe_router_bias, moe_w_gate_up, moe_w_down, moe_shared_gate_up, moe_shared_down):
    bsz, seq, d = x.shape
    n = bsz * seq
    depth = ada_w.shape[0]
    n_a = a_w_qkv.shape[0]
    assert seq % QBLK == 0
    xf = x.reshape(n, d)

    w_gu = moe_w_gate_up.astype(BF16)
    w_dn = moe_w_down.astype(BF16)
    s_gu = moe_shared_gate_up.astype(BF16)
    s_dn = moe_shared_down.astype(BF16)
    row = lambda g: g.reshape(1, d)

    for layer in range(depth):
        mod = _modulation(c, ada_w, ada_b, layer).reshape(bsz, 6, 1, d)
        sh1, sc1, g1, sh2, sc2, g2 = [mod[:, i] for i in range(6)]
        if layer < n_a:
            qkv = _qkv_proj(xf, row(norm1_g[layer]), sc1, sh1, a_w_qkv[layer].astype(BF16), seq)
            o = _dilated_attention(qkv.reshape(bsz, seq, -1), pos, a_q_norm[layer], a_k_norm[layer])
            w_o = a_w_o[layer]
        else:
            j = layer - n_a
            kmod = _modulation(c, kv_ada_w[None], kv_ada_b[None], 0).reshape(bsz, 2, 1, d)
            q, k, v = _mla_prep(xf, pos, row(norm1_g[layer]), sc1, sh1, row(kv_norm_g), kmod[:, 1],
                                kmod[:, 0], b_w_dq[j], b_q_latent_norm[j], b_w_uq[j], b_q_norm[j],
                                kv_w_down, kv_latent_norm, kv_w_uk, kv_w_uv, kv_k_norm, seq)
            o = _mla_attention(q.reshape(bsz, seq, -1), k.reshape(bsz, seq, -1),
                               v.reshape(bsz, seq, -1))
            w_o = b_w_o[j]
        xm, h2, logits = _post_attn(o.reshape(n, d), w_o.astype(BF16), xf, g1, row(norm2_g[layer]),
                                    sc2, sh2, moe_w_router[layer], seq)
        xf = _moe(h2, xm, g2, logits, moe_router_bias[layer], w_gu, w_dn, s_gu, s_dn, layer, seq)
    return xf.reshape(bsz, seq, d)
```

```python
import functools
import math

import numpy as np
import jax
import jax.numpy as jnp
from jax import lax
from jax.experimental import pallas as pl
from jax.experimental.pallas import tpu as pltpu

F32 = jnp.float32
BF16 = jnp.bfloat16
U32 = jnp.uint32

NORM_EPS = 1e-6
MASKED = -1e30
LOG2E = math.log2(math.e)

LANES = 128
SUBLANES = 8
VMEM_LIMIT = 56 * 1024 * 1024

A_HEAD_DIM = 64
DILATED_PAIRS = ((128, 1), (512, 4), (2048, 16))
QBLK = 256
NOPE_DIM = 64
ROPE_DIM = 32
QK_DIM = NOPE_DIM + ROPE_DIM
V_DIM = 64
KV_RANK = 256
ROPE_THETA = 10000.0
N_EXPERTS = 64
N_GROUPS = 8
TOPK_GROUPS = 4
TOP_K = 6
EXPERT_DIM = 256
ROUTED_SCALE = 2.5
MOE_TILE = 2048
EROWS = 256
CROWS = 256

ROW_TILE = 512


def _cparams(*sem):
    return pltpu.CompilerParams(dimension_semantics=sem, vmem_limit_bytes=VMEM_LIMIT)


def _silu(v):
    return v * jax.nn.sigmoid(v)


def _rms(v):
    return v * lax.rsqrt(jnp.mean(v * v, axis=-1, keepdims=True) + NORM_EPS)


def _dot_nt(a, b):
    return lax.dot_general(a, b, (((1,), (1,)), ((), ())), preferred_element_type=F32)


def _mod_kernel(c_ref, w_ref, b_ref, o_ref):
    cond = _silu(c_ref[...])
    o_ref[...] = jnp.dot(cond, w_ref[...], preferred_element_type=F32,
                         precision=lax.Precision.HIGHEST) + b_ref[...]


def _modulation(c, w, b, layer):
    bsz, d = c.shape
    m = w.shape[-1]
    tn = 1024
    return pl.pallas_call(
        _mod_kernel,
        grid=(m // tn,),
        in_specs=[pl.BlockSpec((bsz, d), lambda j: (0, 0)),
                  pl.BlockSpec((None, d, tn), lambda j: (layer, 0, j)),
                  pl.BlockSpec((None, 1, tn), lambda j: (layer, 0, j))],
        out_specs=pl.BlockSpec((bsz, tn), lambda j: (0, j)),
        out_shape=jax.ShapeDtypeStruct((bsz, m), F32),
        compiler_params=_cparams("arbitrary"),
    )(c, w, b.reshape(b.shape[0], 1, m))


def _qkv_kernel(x_ref, g_ref, sc_ref, sh_ref, w_ref, o_ref):
    h = _rms(x_ref[...]) * g_ref[...]
    h = h * (1.0 + sc_ref[...]) + sh_ref[...]
    o_ref[...] = jnp.dot(h.astype(BF16), w_ref[...], preferred_element_type=F32).astype(o_ref.dtype)


def _qkv_proj(xf, g, sc, sh, w, seq):
    n, d = xf.shape
    m = w.shape[1]
    tm = min(ROW_TILE, seq)
    per_b = seq // tm
    vec = pl.BlockSpec((None, 1, d), lambda i: (i // per_b, 0, 0))
    return pl.pallas_call(
        _qkv_kernel,
        grid=(n // tm,),
        in_specs=[pl.BlockSpec((tm, d), lambda i: (i, 0)),
                  pl.BlockSpec((1, d), lambda i: (0, 0)),
                  vec, vec,
                  pl.BlockSpec((d, m), lambda i: (0, 0))],
        out_specs=pl.BlockSpec((tm, m), lambda i: (i, 0)),
        out_shape=jax.ShapeDtypeStruct((n, m), BF16),
        compiler_params=_cparams("parallel"),
    )(xf, g, sc, sh, w)


def _attend(q, k, v, bias):
    s = _dot_nt(q, k) + bias
    m = jnp.max(s, axis=-1, keepdims=True)
    p = jnp.exp2(s - m)
    l = jnp.sum(p, axis=-1, keepdims=True)
    return jnp.dot(p.astype(BF16), v, preferred_element_type=F32) / l


def _split3(v):
    p1 = v.astype(BF16).astype(F32)
    p2 = (v - p1).astype(BF16).astype(F32)
    return p1, p2, v - p1 - p2


def _band_offset(seq, w):
    return seq - w


def _dilated_log_multiplicity(seq):
    a = np.arange(QBLK)[:, None]
    c = np.arange(seq)[None, :]
    d = (seq - QBLK) + a - c
    mult = np.zeros((QBLK, seq), np.int32)
    for win, dil in DILATED_PAIRS:
        mult += ((d >= 0) & (d % dil == 0) & (d <= win)).astype(np.int32)
    return np.where(mult > 0, np.log2(np.maximum(mult, 1)), MASKED).astype(np.float32)


N_FEAT = 9


def _dil_attn_kernel(q_ref, k_ref, v_ref, posc_ref, sfeat_ref, gq_ref, gk_ref, logm_ref,
                     o_ref, qa_ref, qb_ref, ks_ref):
    seq = q_ref.shape[0]
    lane = lax.broadcasted_iota(jnp.int32, (QBLK, LANES), 1)
    lane_lo = lane < A_HEAD_DIM
    piece = [functools.reduce(jnp.logical_or, [lane == i + 3 * j for j in range(N_FEAT // 3)])
             for i in range(3)]

    def head_norm(ref, r0, g_ref, mul):
        t = ref[r0:r0 + QBLK, :].astype(F32)
        sq = t * t
        ssa = jnp.sum(jnp.where(lane_lo, sq, 0.0), axis=-1, keepdims=True)
        ssb = jnp.sum(jnp.where(lane_lo, 0.0, sq), axis=-1, keepdims=True)
        r = jnp.where(lane_lo, lax.rsqrt(ssa / A_HEAD_DIM + NORM_EPS),
                      lax.rsqrt(ssb / A_HEAD_DIM + NORM_EPS))
        return t * r * (g_ref[...] * mul)

    rel0 = posc_ref[0:1, :]
    for r0 in range(0, seq, QBLK):
        rows = slice(r0, r0 + QBLK)
        qn = head_norm(q_ref, r0, gq_ref, A_HEAD_DIM ** -0.5 * LOG2E)
        qa_ref[rows, :LANES] = jnp.where(lane_lo, qn, 0.0).astype(BF16)
        qb_ref[rows, :LANES] = jnp.where(lane_lo, 0.0, qn).astype(BF16)
        qa_ref[rows, LANES:] = jnp.broadcast_to(sfeat_ref[0:1, :], (QBLK, LANES)).astype(BF16)
        qb_ref[rows, LANES:] = jnp.broadcast_to(sfeat_ref[1:2, :], (QBLK, LANES)).astype(BF16)
        ks_ref[rows, :LANES] = head_norm(k_ref, r0, gk_ref, 1.0).astype(BF16)
        p1, p2, p3 = _split3((posc_ref[rows, :] - rel0).astype(F32))
        feat = jnp.where(piece[0], p1, jnp.where(piece[1], p2, jnp.where(piece[2], p3, 0.0)))
        ks_ref[rows, LANES:] = feat.astype(BF16)

    for r0 in range(0, seq, QBLK):
        w = r0 + QBLK
        off = _band_offset(seq, w)
        lm = logm_ref[:, off:off + w]
        kk = ks_ref[0:w, :]
        vv = v_ref[0:w, :]
        o_a = _attend(qa_ref[r0:w, :], kk, vv, lm)
        o_b = _attend(qb_ref[r0:w, :], kk, vv, lm)
        o_ref[r0:w, :] = jnp.where(lane_lo, o_a, o_b).astype(o_ref.dtype)


def _dilated_attention(qkv, pos, q_norm, k_norm):
    bsz, seq, three_d = qkv.shape
    d = three_d // 3
    n_pairs = d // LANES
    n_heads = d // A_HEAD_DIM
    slopes = jnp.exp2(-8.0 * jnp.arange(1, n_heads + 1, dtype=F32) / n_heads) * LOG2E
    sfeat = jnp.repeat(jnp.stack(_split3(slopes), axis=1), 3, axis=1)
    sfeat = jnp.pad(sfeat, ((0, 0), (0, LANES - N_FEAT))).reshape(n_pairs, 2, LANES)
    logm = jnp.asarray(_dilated_log_multiplicity(seq))
    gq = jnp.tile(q_norm.reshape(1, A_HEAD_DIM), (1, 2))
    gk = jnp.tile(k_norm.reshape(1, A_HEAD_DIM), (1, 2))
    blk = lambda off: pl.BlockSpec((None, seq, LANES), lambda b, h: (b, 0, off + h))
    const2 = pl.BlockSpec((1, LANES), lambda b, h: (0, 0))
    ext = pltpu.VMEM((seq, 2 * LANES), BF16)
    return pl.pallas_call(
        _dil_attn_kernel,
        grid=(bsz, n_pairs),
        in_specs=[blk(0), blk(n_pairs), blk(2 * n_pairs),
                  pl.BlockSpec((None, seq, 1), lambda b, h: (b, 0, 0)),
                  pl.BlockSpec((None, 2, LANES), lambda b, h: (h, 0, 0)),
                  const2, const2,
                  pl.BlockSpec((QBLK, seq), lambda b, h: (0, 0))],
        out_specs=pl.BlockSpec((None, seq, LANES), lambda b, h: (b, 0, h)),
        scratch_shapes=[ext, ext, ext],
        out_shape=jax.ShapeDtypeStruct((bsz, seq, d), BF16),
        compiler_params=_cparams("parallel", "parallel"),
    )(qkv, qkv, qkv, pos.reshape(bsz, seq, 1), sfeat, gq, gk, logm)


def _post_attn_kernel(o_ref, wo_ref, x_ref, g1_ref, n2_ref, sc_ref, sh_ref, wr_ref,
                      xm_ref, h_ref, lg_ref):
    y = jnp.dot(o_ref[...], wo_ref[...], preferred_element_type=F32)
    xm = x_ref[...] + g1_ref[...] * y
    xm_ref[...] = xm
    h = _rms(xm) * n2_ref[...]
    h = h * (1.0 + sc_ref[...]) + sh_ref[...]
    h_ref[...] = h.astype(h_ref.dtype)
    lg_ref[...] = lax.dot_general(wr_ref[...], h, (((1,), (1,)), ((), ())),
                                  preferred_element_type=F32, precision=lax.Precision.HIGHEST)


def _post_attn(o, w_o, xf, g1, n2, sc2, sh2, w_router, seq):
    n, d = xf.shape
    ne = w_router.shape[1]
    tm = min(ROW_TILE, seq)
    per_b = seq // tm
    vec = pl.BlockSpec((None, 1, d), lambda i: (i // per_b, 0, 0))
    row = lambda w: pl.BlockSpec((tm, w), lambda i: (i, 0))
    return pl.pallas_call(
        _post_attn_kernel,
        grid=(n // tm,),
        in_specs=[row(d), pl.BlockSpec((d, d), lambda i: (0, 0)), row(d), vec,
                  pl.BlockSpec((1, d), lambda i: (0, 0)), vec, vec,
                  pl.BlockSpec((ne, d), lambda i: (0, 0))],
        out_specs=[row(d), row(d), pl.BlockSpec((ne, tm), lambda i: (0, i))],
        out_shape=[jax.ShapeDtypeStruct((n, d), F32), jax.ShapeDtypeStruct((n, d), BF16),
                   jax.ShapeDtypeStruct((ne, n), F32)],
        compiler_params=_cparams("parallel"),
    )(o, w_o, xf, g1, n2, sc2, sh2, w_router.T)


def _moe_rows(tile):
    rows = tile * TOP_K + N_EXPERTS * (SUBLANES - 1) + EROWS
    return -(-rows // SUBLANES) * SUBLANES


def _first_index(hit, index, limit):
    return jnp.min(jnp.where(hit, index, limit), axis=0, keepdims=True)


def _route_kernel(lg_ref, bias_ref, tri_ref, dest_ref, w_ref, seg_ref, cnt_ref):
    t = lg_ref.shape[1]
    per_g = N_EXPERTS // N_GROUPS
    sub = lax.broadcasted_iota(jnp.int32, (per_g, t), 0).astype(F32)
    neg = -jnp.inf
    scores, sel = [], []
    for g in range(N_GROUPS):
        rows = slice(g * per_g, (g + 1) * per_g)
        sc = jax.nn.sigmoid(lg_ref[rows, :])
        scores.append(sc)
        sel.append(sc + bias_ref[rows, :])

    gs = []
    for g in range(N_GROUPS):
        m1 = jnp.max(sel[g], axis=0, keepdims=True)
        first = _first_index(sel[g] == m1, sub, float(per_g))
        m2 = jnp.max(jnp.where(sub == first, neg, sel[g]), axis=0, keepdims=True)
        gs.append(m1 + m2)
    gsc = jnp.concatenate(gs, axis=0)
    gsub = lax.broadcasted_iota(jnp.int32, (N_GROUPS, t), 0).astype(F32)
    gkeep = jnp.zeros((N_GROUPS, t), F32)
    for _ in range(TOPK_GROUPS):
        m = jnp.max(gsc, axis=0, keepdims=True)
        pick = gsub == _first_index(gsc == m, gsub, float(N_GROUPS))
        gkeep = jnp.where(pick, 1.0, gkeep)
        gsc = jnp.where(pick, neg, gsc)

    eidx = [sub + float(g * per_g) for g in range(N_GROUPS)]
    masked = [jnp.where(jnp.broadcast_to(gkeep[g:g + 1, :], (per_g, t)) > 0.5, sel[g], neg)
              for g in range(N_GROUPS)]
    chosen = [jnp.zeros((per_g, t), F32) for _ in range(N_GROUPS)]
    wsel = [jnp.zeros((per_g, t), F32) for _ in range(N_GROUPS)]
    picked = []
    for _ in range(TOP_K):
        m = jnp.max(masked[0], axis=0, keepdims=True)
        for g in range(1, N_GROUPS):
            m = jnp.maximum(m, jnp.max(masked[g], axis=0, keepdims=True))
        e_first = _first_index(masked[0] == m, eidx[0], float(N_EXPERTS))
        for g in range(1, N_GROUPS):
            e_first = jnp.minimum(e_first, _first_index(masked[g] == m, eidx[g], float(N_EXPERTS)))
        picked.append(e_first)
        for g in range(N_GROUPS):
            pick = eidx[g] == e_first
            chosen[g] = jnp.where(pick, 1.0, chosen[g])
            wsel[g] = jnp.where(pick, scores[g], wsel[g])
            masked[g] = jnp.where(pick, neg, masked[g])

    denom = jnp.sum(wsel[0], axis=0, keepdims=True)
    for g in range(1, N_GROUPS):
        denom = denom + jnp.sum(wsel[g], axis=0, keepdims=True)

    c_all = jnp.concatenate(chosen, axis=0)
    rank = jnp.dot(c_all.astype(BF16), tri_ref[...], preferred_element_type=F32)
    counts = jnp.sum(c_all, axis=1, keepdims=True)
    padded = jnp.ceil(counts / SUBLANES) * SUBLANES
    below = (lax.broadcasted_iota(jnp.int32, (N_EXPERTS, N_EXPERTS), 1)
             < lax.broadcasted_iota(jnp.int32, (N_EXPERTS, N_EXPERTS), 0)).astype(F32)
    seg = jnp.dot(below, jnp.broadcast_to(padded, (N_EXPERTS, LANES)), preferred_element_type=F32,
                  precision=lax.Precision.HIGHEST)
    row_of = seg[:, :1] + rank

    dest_rows, w_rows = [], []
    for k in range(TOP_K):
        d_k = jnp.zeros((1, t), F32)
        w_k = jnp.zeros((1, t), F32)
        for g in range(N_GROUPS):
            hit = eidx[g] == picked[k]
            d_k = d_k + jnp.sum(jnp.where(hit, row_of[g * per_g:(g + 1) * per_g, :], 0.0), axis=0,
                                keepdims=True)
            w_k = w_k + jnp.sum(jnp.where(hit, wsel[g], 0.0), axis=0, keepdims=True)
        dest_rows.append(d_k)
        w_rows.append(w_k / denom * ROUTED_SCALE)
    fill = [jnp.zeros((SUBLANES - TOP_K, t), F32)]
    dest_ref[...] = jnp.concatenate(dest_rows + fill, axis=0).astype(jnp.int32)
    w_ref[...] = jnp.concatenate(w_rows + fill, axis=0)
    seg_ref[...] = seg.astype(jnp.int32)
    cnt_ref[...] = jnp.broadcast_to(counts, (N_EXPERTS, LANES)).astype(jnp.int32)


def _route(logits_t, router_bias, tile):
    ne, n = logits_t.shape
    nt = n // tile
    tri = jnp.asarray(np.triu(np.ones((tile, tile), np.float32), 1), dtype=BF16)
    per_tile = lambda rows, width: pl.BlockSpec((None, rows, width), lambda i: (i, 0, 0))
    dest, wts, seg, cnt = pl.pallas_call(
        _route_kernel,
        grid=(nt,),
        in_specs=[pl.BlockSpec((ne, tile), lambda i: (0, i)),
                  pl.BlockSpec((ne, 1), lambda i: (0, 0)),
                  pl.BlockSpec((tile, tile), lambda i: (0, 0))],
        out_specs=[per_tile(SUBLANES, tile), per_tile(SUBLANES, tile),
                   per_tile(ne, LANES), per_tile(ne, LANES)],
        out_shape=[jax.ShapeDtypeStruct((nt, SUBLANES, tile), jnp.int32),
                   jax.ShapeDtypeStruct((nt, SUBLANES, tile), F32),
                   jax.ShapeDtypeStruct((nt, ne, LANES), jnp.int32),
                   jax.ShapeDtypeStruct((nt, ne, LANES), jnp.int32)],
        compiler_params=_cparams("parallel"),
    )(logits_t, router_bias.astype(F32).reshape(ne, 1), tri)
    return seg[:, :, 0].reshape(-1), cnt[:, :, 0].reshape(-1), dest, wts


def _swiglu(xb, wgu_ref, wdn_ref):
    gu = jnp.dot(xb, wgu_ref[...], preferred_element_type=F32)
    half = wgu_ref.shape[1] // 2
    act = _silu(gu[:, :half]) * gu[:, half:]
    return jnp.dot(act.astype(BF16), wdn_ref[...], preferred_element_type=F32)


def _split_rows(z, il_ref, r):
    nblk = il_ref.shape[0]
    for c in range(nblk):
        il_ref[c, 0:2 * r, :] = z[:, c * LANES:(c + 1) * LANES]
    halves = [[il_ref[c, pl.ds(par, r, stride=2), :] for c in range(nblk)] for par in (0, 1)]
    return jnp.concatenate(halves[0] + halves[1], axis=1)


def _pairs_to_rows(p, il_ref):
    return _split_rows(pltpu.bitcast(p, BF16).astype(F32), il_ref, p.shape[0]).astype(BF16)


def _rows_to_pairs(v, il_ref):
    r, w = v.shape[0], v.shape[1] // 2
    nblk = il_ref.shape[0]
    for c in range(nblk):
        il_ref[c, pl.ds(0, r, stride=2), :] = v[:, c * LANES:(c + 1) * LANES]
        il_ref[c, pl.ds(1, r, stride=2), :] = v[:, w + c * LANES:w + (c + 1) * LANES]
    z = jnp.concatenate([il_ref[c, 0:2 * r, :] for c in range(nblk)], axis=1)
    return pltpu.bitcast(z.astype(BF16), U32)


def _dest_col(half, k, u):
    return (half * SUBLANES + k) * SUBLANES + u


def _moe_kernel(start_ref, count_ref, hp_ref, hb_ref, dest_ref, wt_ref, wgu_hbm, wdn_hbm,
                sgu_ref, sdn_ref, xm_ref, g2_ref, o_ref, xy_ref, il_ref, yp_ref, stage_ref, zf_ref,
                comb_ref, wgu_buf, wdn_buf, wsem, *, layer):
    t = pl.program_id(0)
    s = pl.program_id(1)
    tile = hp_ref.shape[0] // 2

    def weight_copies(e, slot):
        return (pltpu.make_async_copy(wgu_hbm.at[layer, e], wgu_buf.at[slot], wsem.at[0, slot]),
                pltpu.make_async_copy(wdn_hbm.at[layer, e], wdn_buf.at[slot], wsem.at[1, slot]))

    @pl.when(s == 0)
    def _first_weights():
        for cp in weight_copies(0, 0):
            cp.start()

    @pl.when(jnp.logical_and(t == 0, s == 0))
    def _clear():
        xy_ref[...] = jnp.zeros(xy_ref.shape, U32)

    @pl.when(s == 0)
    def _dispatch():
        def group(j, carry):
            for half in range(2):
                p0 = pl.multiple_of(j * 4 * SUBLANES + half * 2 * SUBLANES, 2 * SUBLANES)
                rows = pltpu.bitcast(hp_ref[pl.ds(p0, 2 * SUBLANES), :], U32)
                for u in range(SUBLANES):
                    for k in range(TOP_K):
                        xy_ref[pl.ds(dest_ref[j, _dest_col(half, k, u)], 1), :] = rows[u:u + 1, :]
            return carry

        lax.fori_loop(0, tile // (2 * SUBLANES), group, 0)

    @pl.when(s == 0)
    def _experts():
        def expert(e, carry0):
            slot = lax.rem(e, 2)

            @pl.when(e + 1 < N_EXPERTS)
            def _prefetch():
                for cp in weight_copies(e + 1, 1 - slot):
                    cp.start()

            for cp in weight_copies(e, slot):
                cp.wait()
            start = start_ref[t * N_EXPERTS + e]
            count = count_ref[t * N_EXPERTS + e]

            def chunk(c, carry):
                base = pl.multiple_of(start + c * EROWS, SUBLANES)
                groups = (jnp.minimum(count - c * EROWS, EROWS) + SUBLANES - 1) // SUBLANES
                xb = _pairs_to_rows(xy_ref[pl.ds(base, EROWS), :], il_ref)
                y = _swiglu(xb, wgu_buf.at[slot], wdn_buf.at[slot])
                yp_ref[...] = _rows_to_pairs(y, il_ref)

                def put(j, c2):
                    r0 = pl.multiple_of(j * SUBLANES, SUBLANES)
                    xy_ref[pl.ds(base + r0, SUBLANES), :] = yp_ref[pl.ds(r0, SUBLANES), :]
                    return c2

                lax.fori_loop(0, groups, put, 0)
                return carry

            lax.fori_loop(0, (count + EROWS - 1) // EROWS, chunk, 0)
            return carry0

        lax.fori_loop(0, N_EXPERTS, expert, 0)

    @pl.when(s >= 1)
    def _combine():
        sb = s - 1

        def group(j, carry):
            g2 = sb * (CROWS // (2 * SUBLANES)) + j
            for half in range(2):
                r0 = pl.multiple_of(j * 2 * SUBLANES + half * SUBLANES, SUBLANES)
                t0 = pl.multiple_of(sb * CROWS + r0, SUBLANES)
                for k in range(TOP_K):
                    for u in range(SUBLANES):
                        stage_ref[pl.ds(k * SUBLANES + u, 1), :] = (
                            xy_ref[pl.ds(dest_ref[g2, _dest_col(half, k, u)], 1), :])
                rows = _split_rows(pltpu.bitcast(stage_ref[...], BF16).astype(F32), zf_ref,
                                   TOP_K * SUBLANES)
                wv = wt_ref[pl.ds(t0, SUBLANES), :]
                acc = wv[:, 0:1] * rows[0:SUBLANES, :]
                for k in range(1, TOP_K):
                    acc = acc + wv[:, k:k + 1] * rows[k * SUBLANES:(k + 1) * SUBLANES, :]
                comb_ref[pl.ds(r0, SUBLANES), :] = acc
            return carry

        lax.fori_loop(0, CROWS // (2 * SUBLANES), group, 0)
        shared = _swiglu(hb_ref[...], sgu_ref, sdn_ref)
        o_ref[...] = xm_ref[...] + g2_ref[...] * (comb_ref[...] + shared)


def _moe(h, xm, g2, logits_t, router_bias, w_gu, w_dn, s_gu, s_dn, layer, seq):
    n, d = h.shape
    w = d // 2
    tile = min(MOE_TILE, seq)
    nt = n // tile
    per_b = seq // tile
    csteps = tile // CROWS
    assert seq % tile == 0 and tile % CROWS == 0
    seg_start, seg_count, dest, wts = _route(logits_t, router_bias, tile)
    dest_rows = dest.reshape(nt, SUBLANES, tile // (2 * SUBLANES), 2, SUBLANES)
    dest_rows = dest_rows.transpose(0, 2, 3, 1, 4).reshape(nt, tile // (2 * SUBLANES), -1)
    rows = _moe_rows(tile)
    cst = lambda s: jnp.maximum(s - 1, 0)
    crow = pl.BlockSpec((CROWS, d), lambda t, s, a, b: (t * csteps + cst(s), 0))
    hbm = pl.BlockSpec(memory_space=pl.ANY)
    return pl.pallas_call(
        functools.partial(_moe_kernel, layer=layer),
        grid_spec=pltpu.PrefetchScalarGridSpec(
            num_scalar_prefetch=2,
            grid=(nt, 1 + csteps),
            in_specs=[pl.BlockSpec((2 * tile, w), lambda t, s, a, b: (t, 0)),
                      crow,
                      pl.BlockSpec((None, tile // (2 * SUBLANES), 2 * SUBLANES * SUBLANES),
                                   lambda t, s, a, b: (t, 0, 0), memory_space=pltpu.SMEM),
                      pl.BlockSpec((None, tile, SUBLANES), lambda t, s, a, b: (t, 0, 0)),
                      hbm, hbm,
                      pl.BlockSpec((None,) + s_gu.shape[1:], lambda t, s, a, b: (layer, 0, 0)),
                      pl.BlockSpec((None,) + s_dn.shape[1:], lambda t, s, a, b: (layer, 0, 0)),
                      crow,
                      pl.BlockSpec((None, 1, d), lambda t, s, a, b: (t // per_b, 0, 0))],
            out_specs=crow,
            scratch_shapes=[pltpu.VMEM((rows, w), U32), pltpu.VMEM((w // LANES, 2 * EROWS, LANES), F32),
                            pltpu.VMEM((EROWS, w), U32), pltpu.VMEM((TOP_K * SUBLANES, w), U32),
                            pltpu.VMEM((w // LANES, 2 * TOP_K * SUBLANES, LANES), F32),
                            pltpu.VMEM((CROWS, d), F32),
                            pltpu.VMEM((2, d, 2 * EXPERT_DIM), BF16),
                            pltpu.VMEM((2, EXPERT_DIM, d), BF16),
                            pltpu.SemaphoreType.DMA((2, 2))]),
        out_shape=jax.ShapeDtypeStruct((n, d), F32),
        compiler_params=_cparams("arbitrary", "arbitrary"),
    )(seg_start, seg_count, h.reshape(2 * n, w), h, dest_rows, jnp.swapaxes(wts, 1, 2),
      w_gu, w_dn, s_gu, s_dn, xm, g2)


def _head_norm_rope(t, gain, cos, sin_a, sin_b, mul):
    r = lax.rsqrt(jnp.sum(t * t, axis=-1, keepdims=True) / QK_DIM + NORM_EPS)
    tn = t * r * gain
    half = ROPE_DIM // 2
    out = tn * cos + pltpu.roll(tn, LANES - half, 1) * sin_a + pltpu.roll(tn, half, 1) * sin_b
    return out * mul


def _mla_prep_kernel(x_ref, n1_ref, sc1_ref, sh1_ref, nk_ref, sck_ref, shk_ref,
                     wdq_ref, gql_ref, wuq_ref, gq_ref, wdown_ref, gkl_ref, wk_ref, gk_ref, wuv_ref,
                     cos_ref, sina_ref, sinb_ref, q_ref, k_ref, v_ref):
    xn = _rms(x_ref[...])
    hq = xn * n1_ref[...] * (1.0 + sc1_ref[...]) + sh1_ref[...]
    hk = xn * nk_ref[...] * (1.0 + sck_ref[...]) + shk_ref[...]
    cos, sin_a, sin_b = cos_ref[...], sina_ref[...], sinb_ref[...]
    n_heads = q_ref.shape[1] // LANES

    cq = _rms(jnp.dot(hq.astype(BF16), wdq_ref[...], preferred_element_type=F32)) * gql_ref[...]
    q_raw = jnp.dot(cq.astype(BF16), wuq_ref[...], preferred_element_type=F32)
    for h in range(n_heads):
        sl = slice(h * LANES, (h + 1) * LANES)
        q_ref[:, sl] = _head_norm_rope(q_raw[:, sl], gq_ref[...], cos, sin_a, sin_b,
                                       QK_DIM ** -0.5 * LOG2E).astype(q_ref.dtype)

    down = jnp.dot(hk.astype(BF16), wdown_ref[...], preferred_element_type=F32)
    ckv = (_rms(down[:, :KV_RANK]) * gkl_ref[...]).astype(BF16)
    pe = down[:, KV_RANK:]
    pe_hi = pe.astype(BF16)
    pe_lo = (pe - pe_hi.astype(F32)).astype(BF16)
    k_raw = jnp.dot(jnp.concatenate([ckv, pe_hi, pe_lo], axis=1), wk_ref[...],
                    preferred_element_type=F32)
    for h in range(n_heads):
        sl = slice(h * LANES, (h + 1) * LANES)
        k_ref[:, sl] = _head_norm_rope(k_raw[:, sl], gk_ref[...], cos, sin_a, sin_b,
                                       1.0).astype(k_ref.dtype)
    v_ref[...] = jnp.dot(ckv, wuv_ref[...], preferred_element_type=F32).astype(v_ref.dtype)


def _pad_heads(w, n_heads, width):
    r = w.shape[0]
    return jnp.pad(w.reshape(r, n_heads, width), ((0, 0), (0, 0), (0, LANES - width))).reshape(r, n_heads * LANES)


def _mla_prep(xf, pos, n1, sc1, sh1, nk, sck, shk, w_dq, g_ql, w_uq, g_q, w_down, g_kl, w_uk, w_uv,
              g_k, seq):
    n, d = xf.shape
    n_heads = w_uv.shape[1] // V_DIM
    q_rank = w_dq.shape[1]
    hw = n_heads * LANES
    tm = 256
    per_b = seq // tm

    half = ROPE_DIM // 2
    inv = jnp.power(ROPE_THETA, -jnp.arange(half, dtype=F32) / half)
    ang = pos.astype(F32).reshape(n, 1) * inv
    cos, sin = jnp.cos(ang), jnp.sin(ang)
    ones = jnp.ones((n, NOPE_DIM), F32)
    zeros = lambda wdt: jnp.zeros((n, wdt), F32)
    tail = LANES - QK_DIM
    cos_t = jnp.concatenate([ones, cos, cos, zeros(tail)], axis=1)
    sina_t = jnp.concatenate([zeros(NOPE_DIM), -sin, zeros(half), zeros(tail)], axis=1)
    sinb_t = jnp.concatenate([zeros(NOPE_DIM), zeros(half), sin, zeros(tail)], axis=1)

    pad_gain = lambda g: jnp.pad(g.reshape(1, QK_DIM), ((0, 0), (0, tail)))
    wuq_p = _pad_heads(w_uq, n_heads, QK_DIM).astype(BF16)
    wdown_p = jnp.pad(w_down, ((0, 0), (0, LANES - ROPE_DIM))).astype(BF16)
    eye = jnp.pad(jnp.eye(ROPE_DIM, dtype=F32), ((0, LANES - ROPE_DIM), (NOPE_DIM, tail)))
    pe_rows = jnp.tile(eye, (1, n_heads))
    wk = jnp.concatenate([_pad_heads(w_uk, n_heads, NOPE_DIM), pe_rows, pe_rows], axis=0).astype(BF16)

    vec = pl.BlockSpec((None, 1, d), lambda i: (i // per_b, 0, 0))
    full = lambda a: pl.BlockSpec(a.shape, lambda i: (0,) * a.ndim)
    row = lambda wdt: pl.BlockSpec((tm, wdt), lambda i: (i, 0))
    args = [xf, n1, sc1, sh1, nk, sck, shk,
            w_dq.astype(BF16), g_ql.reshape(1, q_rank), wuq_p, pad_gain(g_q),
            wdown_p, g_kl.reshape(1, KV_RANK), wk, pad_gain(g_k), w_uv.astype(BF16),
            cos_t, sina_t, sinb_t]
    specs = [row(d), full(n1), vec, vec, full(nk), vec, vec] + [full(a) for a in args[7:16]] + [row(LANES)] * 3
    return pl.pallas_call(
        _mla_prep_kernel,
        grid=(n // tm,),
        in_specs=specs,
        out_specs=[row(hw), row(hw), row(n_heads * V_DIM)],
        out_shape=[jax.ShapeDtypeStruct((n, hw), BF16), jax.ShapeDtypeStruct((n, hw), BF16),
                   jax.ShapeDtypeStruct((n, n_heads * V_DIM), BF16)],
        compiler_params=_cparams("parallel"),
    )(*args)


def _causal_bias(seq):
    a = np.arange(QBLK)[:, None]
    c = np.arange(seq)[None, :]
    return np.where(c <= (seq - QBLK) + a, 0.0, MASKED).astype(np.float32)


def _mla_attn_kernel(q_ref, k_ref, v_ref, mask_ref, o_ref):
    seq = q_ref.shape[0]
    lane_lo = lax.broadcasted_iota(jnp.int32, (QBLK, LANES), 1) < V_DIM
    for r0 in range(0, seq, QBLK):
        w = r0 + QBLK
        off = _band_offset(seq, w)
        bias = mask_ref[:, off:off + w]
        vv = v_ref[0:w, :]
        o_a = _attend(q_ref[r0:w, :LANES], k_ref[0:w, :LANES], vv, bias)
        o_b = _attend(q_ref[r0:w, LANES:], k_ref[0:w, LANES:], vv, bias)
        o_ref[r0:w, :] = jnp.where(lane_lo, o_a, o_b).astype(o_ref.dtype)


def _mla_attention(q, k, v):
    bsz, seq, hw = q.shape
    n_pairs = hw // (2 * LANES)
    qk = pl.BlockSpec((None, seq, 2 * LANES), lambda b, h: (b, 0, h))
    vo = pl.BlockSpec((None, seq, LANES), lambda b, h: (b, 0, h))
    return pl.pallas_call(
        _mla_attn_kernel,
        grid=(bsz, n_pairs),
        in_specs=[qk, qk, vo, pl.BlockSpec((QBLK, seq), lambda b, h: (0, 0))],
        out_specs=vo,
        out_shape=jax.ShapeDtypeStruct(v.shape, BF16),
        compiler_params=_cparams("parallel", "parallel"),
    )(q, k, v, jnp.asarray(_causal_bias(seq)))


def kernel(x, c, pos, ada_w, ada_b, norm1_g, norm2_g, a_w_qkv, a_q_norm, a_k_norm, a_w_o, kv_ada_w, kv_ada_b, kv_norm_g, kv_w_down, kv_latent_norm, kv_w_uk, kv_w_uv, kv_k_norm, b_w_dq, b_q_latent_norm, b_w_uq, b_q_norm, b_w_o, moe_w_router, moe_router_bias, moe_w_gate_up, moe_w_down, moe_shared_gate_up, moe_shared_down):
    bsz, seq, d = x.shape
    n = bsz * seq
    depth = ada_w.shape[0]
    n_a = a_w_qkv.shape[0]
    assert seq % QBLK == 0
    xf = x.reshape(n, d)

    w_gu = moe_w_gate_up.astype(BF16)
    w_dn = moe_w_down.astype(BF16)
    s_gu = moe_shared_gate_up.astype(BF16)
    s_dn = moe_shared_down.astype(BF16)
    row = lambda g: g.reshape(1, d)

    for layer in range(depth):
        mod = _modulation(c, ada_w, ada_b, layer).reshape(bsz, 6, 1, d)
        sh1, sc1, g1, sh2, sc2, g2 = [mod[:, i] for i in range(6)]
        if layer < n_a:
            qkv = _qkv_proj(xf, row(norm1_g[layer]), sc1, sh1, a_w_qkv[layer].astype(BF16), seq)
            o = _dilated_attention(qkv.reshape(bsz, seq, -1), pos, a_q_norm[layer], a_k_norm[layer])
            w_o = a_w_o[layer]
        else:
            j = layer - n_a
            kmod = _modulation(c, kv_ada_w[None], kv_ada_b[None], 0).reshape(bsz, 2, 1, d)
            q, k, v = _mla_prep(xf, pos, row(norm1_g[layer]), sc1, sh1, row(kv_norm_g), kmod[:, 1],
                                kmod[:, 0], b_w_dq[j], b_q_latent_norm[j], b_w_uq[j], b_q_norm[j],
                                kv_w_down, kv_latent_norm, kv_w_uk, kv_w_uv, kv_k_norm, seq)
            o = _mla_attention(q.reshape(bsz, seq, -1), k.reshape(bsz, seq, -1),
                               v.reshape(bsz, seq, -1))
            w_o = b_w_o[j]
        xm, h2, logits = _post_attn(o.reshape(n, d), w_o.astype(BF16), xf, g1, row(norm2_g[layer]),
                                    sc2, sh2, moe_w_router[layer], seq)
        xf = _moe(h2, xm, g2, logits, moe_router_bias[layer], w_gu, w_dn, s_gu, s_dn, layer, seq)
    return xf.reshape(bsz, seq, d)
```

```python
import functools
import math

import numpy as np
import jax
import jax.numpy as jnp
from jax import lax
from jax.experimental import pallas as pl
from jax.experimental.pallas import tpu as pltpu

F32 = jnp.float32
BF16 = jnp.bfloat16
U32 = jnp.uint32

NORM_EPS = 1e-6
MASKED = -1e30
LOG2E = math.log2(math.e)

LANES = 128
SUBLANES = 8
VMEM_LIMIT = 56 * 1024 * 1024

A_HEAD_DIM = 64
DILATED_PAIRS = ((128, 1), (512, 4), (2048, 16))
QBLK = 256
NOPE_DIM = 64
ROPE_DIM = 32
QK_DIM = NOPE_DIM + ROPE_DIM
V_DIM = 64
KV_RANK = 256
ROPE_THETA = 10000.0
N_EXPERTS = 64
N_GROUPS = 8
TOPK_GROUPS = 4
TOP_K = 6
EXPERT_DIM = 256
ROUTED_SCALE = 2.5
MOE_TILE = 2048
EROWS = 256
CROWS = 256

ROW_TILE = 512


def _cparams(*sem):
    return pltpu.CompilerParams(dimension_semantics=sem, vmem_limit_bytes=VMEM_LIMIT)


def _silu(v):
    return v * jax.nn.sigmoid(v)


def _rms(v):
    return v * lax.rsqrt(jnp.mean(v * v, axis=-1, keepdims=True) + NORM_EPS)


def _dot_nt(a, b):
    return lax.dot_general(a, b, (((1,), (1,)), ((), ())), preferred_element_type=F32)


def _mod_kernel(c_ref, w_ref, b_ref, o_ref):
    cond = _silu(c_ref[...])
    o_ref[...] = jnp.dot(cond, w_ref[...], preferred_element_type=F32,
                         precision=lax.Precision.HIGHEST) + b_ref[...]


def _modulation(c, w, b, layer):
    bsz, d = c.shape
    m = w.shape[-1]
    tn = 1024
    return pl.pallas_call(
        _mod_kernel,
        grid=(m // tn,),
        in_specs=[pl.BlockSpec((bsz, d), lambda j: (0, 0)),
                  pl.BlockSpec((None, d, tn), lambda j: (layer, 0, j)),
                  pl.BlockSpec((None, 1, tn), lambda j: (layer, 0, j))],
        out_specs=pl.BlockSpec((bsz, tn), lambda j: (0, j)),
        out_shape=jax.ShapeDtypeStruct((bsz, m), F32),
        compiler_params=_cparams("arbitrary"),
    )(c, w, b.reshape(b.shape[0], 1, m))


def _qkv_kernel(x_ref, g_ref, sc_ref, sh_ref, w_ref, o_ref):
    h = _rms(x_ref[...]) * g_ref[...]
    h = h * (1.0 + sc_ref[...]) + sh_ref[...]
    o_ref[...] = jnp.dot(h.astype(BF16), w_ref[...], preferred_element_type=F32).astype(o_ref.dtype)


def _qkv_proj(xf, g, sc, sh, w, seq):
    n, d = xf.shape
    m = w.shape[1]
    tm = min(ROW_TILE, seq)
    per_b = seq // tm
    vec = pl.BlockSpec((None, 1, d), lambda i: (i // per_b, 0, 0))
    return pl.pallas_call(
        _qkv_kernel,
        grid=(n // tm,),
        in_specs=[pl.BlockSpec((tm, d), lambda i: (i, 0)),
                  pl.BlockSpec((1, d), lambda i: (0, 0)),
                  vec, vec,
                  pl.BlockSpec((d, m), lambda i: (0, 0))],
        out_specs=pl.BlockSpec((tm, m), lambda i: (i, 0)),
        out_shape=jax.ShapeDtypeStruct((n, m), BF16),
        compiler_params=_cparams("parallel"),
    )(xf, g, sc, sh, w)


ATT_SLOTS = 4


def _attend_pair(qs, ks, v, bias, s_ref, p_ref, blk, w):
    slots = [(blk % 2) * 2 + h for h in range(2)]
    for h in range(2):
        s_ref[slots[h], :, 0:w] = _dot_nt(qs[h], ks[h]) + bias
    denom = []
    for h in range(2):
        s = s_ref[slots[h], :, 0:w]
        p = jnp.exp2(s - jnp.max(s, axis=-1, keepdims=True))
        denom.append(jnp.sum(p, axis=-1, keepdims=True))
        p_ref[slots[h], :, 0:w] = p.astype(BF16)
    return [jnp.dot(p_ref[slots[h], :, 0:w], v, preferred_element_type=F32) / denom[h]
            for h in range(2)]


def _attn_scratch(seq):
    return [pltpu.VMEM((ATT_SLOTS, QBLK, seq), F32), pltpu.VMEM((ATT_SLOTS, QBLK, seq), BF16)]


def _split3(v):
    p1 = v.astype(BF16).astype(F32)
    p2 = (v - p1).astype(BF16).astype(F32)
    return p1, p2, v - p1 - p2


def _band_offset(seq, w):
    return seq - w


def _dilated_log_multiplicity(seq):
    a = np.arange(QBLK)[:, None]
    c = np.arange(seq)[None, :]
    d = (seq - QBLK) + a - c
    mult = np.zeros((QBLK, seq), np.int32)
    for win, dil in DILATED_PAIRS:
        mult += ((d >= 0) & (d % dil == 0) & (d <= win)).astype(np.int32)
    return np.where(mult > 0, np.log2(np.maximum(mult, 1)), MASKED).astype(np.float32)


N_FEAT = 9


def _dil_attn_kernel(q_ref, k_ref, v_ref, posc_ref, sfeat_ref, gq_ref, gk_ref, logm_ref,
                     o_ref, qa_ref, qb_ref, ks_ref, s_ref, p_ref):
    seq = q_ref.shape[0]
    lane = lax.broadcasted_iota(jnp.int32, (QBLK, LANES), 1)
    lane_lo = lane < A_HEAD_DIM
    piece = [functools.reduce(jnp.logical_or, [lane == i + 3 * j for j in range(N_FEAT // 3)])
             for i in range(3)]

    def head_norm(ref, r0, g_ref, mul):
        t = ref[r0:r0 + QBLK, :].astype(F32)
        sq = t * t
        ssa = jnp.sum(jnp.where(lane_lo, sq, 0.0), axis=-1, keepdims=True)
        ssb = jnp.sum(jnp.where(lane_lo, 0.0, sq), axis=-1, keepdims=True)
        r = jnp.where(lane_lo, lax.rsqrt(ssa / A_HEAD_DIM + NORM_EPS),
                      lax.rsqrt(ssb / A_HEAD_DIM + NORM_EPS))
        return t * r * (g_ref[...] * mul)

    rel0 = posc_ref[0:1, :]
    for r0 in range(0, seq, QBLK):
        rows = slice(r0, r0 + QBLK)
        qn = head_norm(q_ref, r0, gq_ref, A_HEAD_DIM ** -0.5 * LOG2E)
        qa_ref[rows, :LANES] = jnp.where(lane_lo, qn, 0.0).astype(BF16)
        qb_ref[rows, :LANES] = jnp.where(lane_lo, 0.0, qn).astype(BF16)
        qa_ref[rows, LANES:] = jnp.broadcast_to(sfeat_ref[0:1, :], (QBLK, LANES)).astype(BF16)
        qb_ref[rows, LANES:] = jnp.broadcast_to(sfeat_ref[1:2, :], (QBLK, LANES)).astype(BF16)
        ks_ref[rows, :LANES] = head_norm(k_ref, r0, gk_ref, 1.0).astype(BF16)
        p1, p2, p3 = _split3((posc_ref[rows, :] - rel0).astype(F32))
        feat = jnp.where(piece[0], p1, jnp.where(piece[1], p2, jnp.where(piece[2], p3, 0.0)))
        ks_ref[rows, LANES:] = feat.astype(BF16)

    for r0 in range(0, seq, QBLK):
        w = r0 + QBLK
        off = _band_offset(seq, w)
        lm = logm_ref[:, off:off + w]
        kk = ks_ref[0:w, :]
        o_a, o_b = _attend_pair((qa_ref[r0:w, :], qb_ref[r0:w, :]), (kk, kk), v_ref[0:w, :], lm,
                                s_ref, p_ref, r0 // QBLK, w)
        o_ref[r0:w, :] = jnp.where(lane_lo, o_a, o_b).astype(o_ref.dtype)


def _dilated_attention(qkv, pos, q_norm, k_norm):
    bsz, seq, three_d = qkv.shape
    d = three_d // 3
    n_pairs = d // LANES
    n_heads = d // A_HEAD_DIM
    slopes = jnp.exp2(-8.0 * jnp.arange(1, n_heads + 1, dtype=F32) / n_heads) * LOG2E
    sfeat = jnp.repeat(jnp.stack(_split3(slopes), axis=1), 3, axis=1)
    sfeat = jnp.pad(sfeat, ((0, 0), (0, LANES - N_FEAT))).reshape(n_pairs, 2, LANES)
    logm = jnp.asarray(_dilated_log_multiplicity(seq))
    gq = jnp.tile(q_norm.reshape(1, A_HEAD_DIM), (1, 2))
    gk = jnp.tile(k_norm.reshape(1, A_HEAD_DIM), (1, 2))
    blk = lambda off: pl.BlockSpec((None, seq, LANES), lambda b, h: (b, 0, off + h))
    const2 = pl.BlockSpec((1, LANES), lambda b, h: (0, 0))
    ext = pltpu.VMEM((seq, 2 * LANES), BF16)
    return pl.pallas_call(
        _dil_attn_kernel,
        grid=(bsz, n_pairs),
        in_specs=[blk(0), blk(n_pairs), blk(2 * n_pairs),
                  pl.BlockSpec((None, seq, 1), lambda b, h: (b, 0, 0)),
                  pl.BlockSpec((None, 2, LANES), lambda b, h: (h, 0, 0)),
                  const2, const2,
                  pl.BlockSpec((QBLK, seq), lambda b, h: (0, 0))],
        out_specs=pl.BlockSpec((None, seq, LANES), lambda b, h: (b, 0, h)),
        scratch_shapes=[ext, ext, ext] + _attn_scratch(seq),
        out_shape=jax.ShapeDtypeStruct((bsz, seq, d), BF16),
        compiler_params=_cparams("parallel", "parallel"),
    )(qkv, qkv, qkv, pos.reshape(bsz, seq, 1), sfeat, gq, gk, logm)


def _post_attn_kernel(o_ref, wo_ref, x_ref, g1_ref, n2_ref, sc_ref, sh_ref, wr_ref,
                      xm_ref, h_ref, lg_ref):
    y = jnp.dot(o_ref[...], wo_ref[...], preferred_element_type=F32)
    xm = x_ref[...] + g1_ref[...] * y
    xm_ref[...] = xm
    h = _rms(xm) * n2_ref[...]
    h = h * (1.0 + sc_ref[...]) + sh_ref[...]
    h_ref[...] = h.astype(h_ref.dtype)
    lg_ref[...] = lax.dot_general(wr_ref[...], h, (((1,), (1,)), ((), ())),
                                  preferred_element_type=F32, precision=lax.Precision.HIGHEST)


def _post_attn(o, w_o, xf, g1, n2, sc2, sh2, w_router, seq):
    n, d = xf.shape
    ne = w_router.shape[1]
    tm = min(ROW_TILE, seq)
    per_b = seq // tm
    vec = pl.BlockSpec((None, 1, d), lambda i: (i // per_b, 0, 0))
    row = lambda w: pl.BlockSpec((tm, w), lambda i: (i, 0))
    return pl.pallas_call(
        _post_attn_kernel,
        grid=(n // tm,),
        in_specs=[row(d), pl.BlockSpec((d, d), lambda i: (0, 0)), row(d), vec,
                  pl.BlockSpec((1, d), lambda i: (0, 0)), vec, vec,
                  pl.BlockSpec((ne, d), lambda i: (0, 0))],
        out_specs=[row(d), row(d), pl.BlockSpec((ne, tm), lambda i: (0, i))],
        out_shape=[jax.ShapeDtypeStruct((n, d), F32), jax.ShapeDtypeStruct((n, d), BF16),
                   jax.ShapeDtypeStruct((ne, n), F32)],
        compiler_params=_cparams("parallel"),
    )(o, w_o, xf, g1, n2, sc2, sh2, w_router.T)


def _moe_rows(tile):
    rows = tile * TOP_K + N_EXPERTS * (SUBLANES - 1) + EROWS
    return -(-rows // SUBLANES) * SUBLANES


def _first_index(hit, index, limit):
    return jnp.min(jnp.where(hit, index, limit), axis=0, keepdims=True)


def _route_kernel(lg_ref, bias_ref, tri_ref, dest_ref, w_ref, seg_ref, cnt_ref):
    t = lg_ref.shape[1]
    per_g = N_EXPERTS // N_GROUPS
    sub = lax.broadcasted_iota(jnp.int32, (per_g, t), 0).astype(F32)
    neg = -jnp.inf
    scores, sel = [], []
    for g in range(N_GROUPS):
        rows = slice(g * per_g, (g + 1) * per_g)
        sc = jax.nn.sigmoid(lg_ref[rows, :])
        scores.append(sc)
        sel.append(sc + bias_ref[rows, :])

    gs = []
    for g in range(N_GROUPS):
        m1 = jnp.max(sel[g], axis=0, keepdims=True)
        first = _first_index(sel[g] == m1, sub, float(per_g))
        m2 = jnp.max(jnp.where(sub == first, neg, sel[g]), axis=0, keepdims=True)
        gs.append(m1 + m2)
    gsc = jnp.concatenate(gs, axis=0)
    gsub = lax.broadcasted_iota(jnp.int32, (N_GROUPS, t), 0).astype(F32)
    gkeep = jnp.zeros((N_GROUPS, t), F32)
    for _ in range(TOPK_GROUPS):
        m = jnp.max(gsc, axis=0, keepdims=True)
        pick = gsub == _first_index(gsc == m, gsub, float(N_GROUPS))
        gkeep = jnp.where(pick, 1.0, gkeep)
        gsc = jnp.where(pick, neg, gsc)

    eidx = [sub + float(g * per_g) for g in range(N_GROUPS)]
    masked = [jnp.where(jnp.broadcast_to(gkeep[g:g + 1, :], (per_g, t)) > 0.5, sel[g], neg)
              for g in range(N_GROUPS)]
    chosen = [jnp.zeros((per_g, t), F32) for _ in range(N_GROUPS)]
    wsel = [jnp.zeros((per_g, t), F32) for _ in range(N_GROUPS)]
    picked = []
    for _ in range(TOP_K):
        m = jnp.max(masked[0], axis=0, keepdims=True)
        for g in range(1, N_GROUPS):
            m = jnp.maximum(m, jnp.max(masked[g], axis=0, keepdims=True))
        e_first = _first_index(masked[0] == m, eidx[0], float(N_EXPERTS))
        for g in range(1, N_GROUPS):
            e_first = jnp.minimum(e_first, _first_index(masked[g] == m, eidx[g], float(N_EXPERTS)))
        picked.append(e_first)
        for g in range(N_GROUPS):
            pick = eidx[g] == e_first
            chosen[g] = jnp.where(pick, 1.0, chosen[g])
            wsel[g] = jnp.where(pick, scores[g], wsel[g])
            masked[g] = jnp.where(pick, neg, masked[g])

    denom = jnp.sum(wsel[0], axis=0, keepdims=True)
    for g in range(1, N_GROUPS):
        denom = denom + jnp.sum(wsel[g], axis=0, keepdims=True)

    c_all = jnp.concatenate(chosen, axis=0)
    rank = jnp.dot(c_all.astype(BF16), tri_ref[...], preferred_element_type=F32)
    counts = jnp.sum(c_all, axis=1, keepdims=True)
    padded = jnp.ceil(counts / SUBLANES) * SUBLANES
    below = (lax.broadcasted_iota(jnp.int32, (N_EXPERTS, N_EXPERTS), 1)
             < lax.broadcasted_iota(jnp.int32, (N_EXPERTS, N_EXPERTS), 0)).astype(F32)
    seg = jnp.dot(below, jnp.broadcast_to(padded, (N_EXPERTS, LANES)), preferred_element_type=F32,
                  precision=lax.Precision.HIGHEST)
    row_of = seg[:, :1] + rank

    dest_rows, w_rows = [], []
    for k in range(TOP_K):
        d_k = jnp.zeros((1, t), F32)
        w_k = jnp.zeros((1, t), F32)
        for g in range(N_GROUPS):
            hit = eidx[g] == picked[k]
            d_k = d_k + jnp.sum(jnp.where(hit, row_of[g * per_g:(g + 1) * per_g, :], 0.0), axis=0,
                                keepdims=True)
            w_k = w_k + jnp.sum(jnp.where(hit, wsel[g], 0.0), axis=0, keepdims=True)
        dest_rows.append(d_k)
        w_rows.append(w_k / denom * ROUTED_SCALE)
    fill = [jnp.zeros((SUBLANES - TOP_K, t), F32)]
    dest_ref[...] = jnp.concatenate(dest_rows + fill, axis=0).astype(jnp.int32)
    w_ref[...] = jnp.concatenate(w_rows + fill, axis=0)
    seg_ref[...] = seg.astype(jnp.int32)
    cnt_ref[...] = jnp.broadcast_to(counts, (N_EXPERTS, LANES)).astype(jnp.int32)


def _route(logits_t, router_bias, tile):
    ne, n = logits_t.shape
    nt = n // tile
    tri = jnp.asarray(np.triu(np.ones((tile, tile), np.float32), 1), dtype=BF16)
    per_tile = lambda rows, width: pl.BlockSpec((None, rows, width), lambda i: (i, 0, 0))
    dest, wts, seg, cnt = pl.pallas_call(
        _route_kernel,
        grid=(nt,),
        in_specs=[pl.BlockSpec((ne, tile), lambda i: (0, i)),
                  pl.BlockSpec((ne, 1), lambda i: (0, 0)),
                  pl.BlockSpec((tile, tile), lambda i: (0, 0))],
        out_specs=[per_tile(SUBLANES, tile), per_tile(SUBLANES, tile),
                   per_tile(ne, LANES), per_tile(ne, LANES)],
        out_shape=[jax.ShapeDtypeStruct((nt, SUBLANES, tile), jnp.int32),
                   jax.ShapeDtypeStruct((nt, SUBLANES, tile), F32),
                   jax.ShapeDtypeStruct((nt, ne, LANES), jnp.int32),
                   jax.ShapeDtypeStruct((nt, ne, LANES), jnp.int32)],
        compiler_params=_cparams("parallel"),
    )(logits_t, router_bias.astype(F32).reshape(ne, 1), tri)
    return seg[:, :, 0].reshape(-1), cnt[:, :, 0].reshape(-1), dest, wts


def _swiglu(xb, wgu_ref, wdn_ref):
    gu = jnp.dot(xb, wgu_ref[...], preferred_element_type=F32)
    half = wgu_ref.shape[1] // 2
    act = _silu(gu[:, :half]) * gu[:, half:]
    return jnp.dot(act.astype(BF16), wdn_ref[...], preferred_element_type=F32)


def _split_rows(z, il_ref, r):
    nblk = il_ref.shape[0]
    for c in range(nblk):
        il_ref[c, 0:2 * r, :] = z[:, c * LANES:(c + 1) * LANES]
    halves = [[il_ref[c, pl.ds(par, r, stride=2), :] for c in range(nblk)] for par in (0, 1)]
    return jnp.concatenate(halves[0] + halves[1], axis=1)


def _pairs_to_rows(p, il_ref):
    return _split_rows(pltpu.bitcast(p, BF16).astype(F32), il_ref, p.shape[0]).astype(BF16)


def _rows_to_pairs(v, il_ref):
    r, w = v.shape[0], v.shape[1] // 2
    nblk = il_ref.shape[0]
    for c in range(nblk):
        il_ref[c, pl.ds(0, r, stride=2), :] = v[:, c * LANES:(c + 1) * LANES]
        il_ref[c, pl.ds(1, r, stride=2), :] = v[:, w + c * LANES:w + (c + 1) * LANES]
    z = jnp.concatenate([il_ref[c, 0:2 * r, :] for c in range(nblk)], axis=1)
    return pltpu.bitcast(z.astype(BF16), U32)


def _dest_col(half, k, u):
    return (half * SUBLANES + k) * SUBLANES + u


def _moe_kernel(start_ref, count_ref, hp_ref, hb_ref, dest_ref, wt_ref, wgu_hbm, wdn_hbm,
                sgu_ref, sdn_ref, xm_ref, g2_ref, o_ref, xy_ref, il_ref, yp_ref, stage_ref, zf_ref,
                comb_ref, wgu_buf, wdn_buf, wsem, *, layer):
    t = pl.program_id(0)
    s = pl.program_id(1)
    tile = hp_ref.shape[0] // 2

    def weight_copies(e, slot):
        return (pltpu.make_async_copy(wgu_hbm.at[layer, e], wgu_buf.at[slot], wsem.at[0, slot]),
                pltpu.make_async_copy(wdn_hbm.at[layer, e], wdn_buf.at[slot], wsem.at[1, slot]))

    @pl.when(s == 0)
    def _first_weights():
        for cp in weight_copies(0, 0):
            cp.start()

    @pl.when(jnp.logical_and(t == 0, s == 0))
    def _clear():
        xy_ref[...] = jnp.zeros(xy_ref.shape, U32)

    @pl.when(s == 0)
    def _dispatch():
        def group(j, carry):
            for half in range(2):
                p0 = pl.multiple_of(j * 4 * SUBLANES + half * 2 * SUBLANES, 2 * SUBLANES)
                rows = pltpu.bitcast(hp_ref[pl.ds(p0, 2 * SUBLANES), :], U32)
                for u in range(SUBLANES):
                    for k in range(TOP_K):
                        xy_ref[pl.ds(dest_ref[j, _dest_col(half, k, u)], 1), :] = rows[u:u + 1, :]
            return carry

        lax.fori_loop(0, tile // (2 * SUBLANES), group, 0)

    @pl.when(s == 0)
    def _experts():
        def expert(e, carry0):
            slot = lax.rem(e, 2)

            @pl.when(e + 1 < N_EXPERTS)
            def _prefetch():
                for cp in weight_copies(e + 1, 1 - slot):
                    cp.start()

            for cp in weight_copies(e, slot):
                cp.wait()
            start = start_ref[t * N_EXPERTS + e]
            count = count_ref[t * N_EXPERTS + e]

            def chunk(c, carry):
                base = pl.multiple_of(start + c * EROWS, SUBLANES)
                groups = (jnp.minimum(count - c * EROWS, EROWS) + SUBLANES - 1) // SUBLANES
                xb = _pairs_to_rows(xy_ref[pl.ds(base, EROWS), :], il_ref)
                y = _swiglu(xb, wgu_buf.at[slot], wdn_buf.at[slot])
                yp_ref[...] = _rows_to_pairs(y, il_ref)

                def put(j, c2):
                    r0 = pl.multiple_of(j * SUBLANES, SUBLANES)
                    xy_ref[pl.ds(base + r0, SUBLANES), :] = yp_ref[pl.ds(r0, SUBLANES), :]
                    return c2

                lax.fori_loop(0, groups, put, 0)
                return carry

            lax.fori_loop(0, (count + EROWS - 1) // EROWS, chunk, 0)
            return carry0

        lax.fori_loop(0, N_EXPERTS, expert, 0)

    @pl.when(s >= 1)
    def _combine():
        sb = s - 1

        def group(j, carry):
            g2 = sb * (CROWS // (2 * SUBLANES)) + j
            for half in range(2):
                r0 = pl.multiple_of(j * 2 * SUBLANES + half * SUBLANES, SUBLANES)
                t0 = pl.multiple_of(sb * CROWS + r0, SUBLANES)
                for k in range(TOP_K):
                    for u in range(SUBLANES):
                        stage_ref[pl.ds(k * SUBLANES + u, 1), :] = (
                            xy_ref[pl.ds(dest_ref[g2, _dest_col(half, k, u)], 1), :])
                rows = _split_rows(pltpu.bitcast(stage_ref[...], BF16).astype(F32), zf_ref,
                                   TOP_K * SUBLANES)
                wv = wt_ref[pl.ds(t0, SUBLANES), :]
                acc = wv[:, 0:1] * rows[0:SUBLANES, :]
                for k in range(1, TOP_K):
                    acc = acc + wv[:, k:k + 1] * rows[k * SUBLANES:(k + 1) * SUBLANES, :]
                comb_ref[pl.ds(r0, SUBLANES), :] = acc
            return carry

        lax.fori_loop(0, CROWS // (2 * SUBLANES), group, 0)
        shared = _swiglu(hb_ref[...], sgu_ref, sdn_ref)
        o_ref[...] = xm_ref[...] + g2_ref[...] * (comb_ref[...] + shared)


def _moe(h, xm, g2, logits_t, router_bias, w_gu, w_dn, s_gu, s_dn, layer, seq):
    n, d = h.shape
    w = d // 2
    tile = min(MOE_TILE, seq)
    nt = n // tile
    per_b = seq // tile
    csteps = tile // CROWS
    assert seq % tile == 0 and tile % CROWS == 0
    seg_start, seg_count, dest, wts = _route(logits_t, router_bias, tile)
    dest_rows = dest.reshape(nt, SUBLANES, tile // (2 * SUBLANES), 2, SUBLANES)
    dest_rows = dest_rows.transpose(0, 2, 3, 1, 4).reshape(nt, tile // (2 * SUBLANES), -1)
    rows = _moe_rows(tile)
    cst = lambda s: jnp.maximum(s - 1, 0)
    crow = pl.BlockSpec((CROWS, d), lambda t, s, a, b: (t * csteps + cst(s), 0))
    hbm = pl.BlockSpec(memory_space=pl.ANY)
    return pl.pallas_call(
        functools.partial(_moe_kernel, layer=layer),
        grid_spec=pltpu.PrefetchScalarGridSpec(
            num_scalar_prefetch=2,
            grid=(nt, 1 + csteps),
            in_specs=[pl.BlockSpec((2 * tile, w), lambda t, s, a, b: (t, 0)),
                      crow,
                      pl.BlockSpec((None, tile // (2 * SUBLANES), 2 * SUBLANES * SUBLANES),
                                   lambda t, s, a, b: (t, 0, 0), memory_space=pltpu.SMEM),
                      pl.BlockSpec((None, tile, SUBLANES), lambda t, s, a, b: (t, 0, 0)),
                      hbm, hbm,
                      pl.BlockSpec((None,) + s_gu.shape[1:], lambda t, s, a, b: (layer, 0, 0)),
                      pl.BlockSpec((None,) + s_dn.shape[1:], lambda t, s, a, b: (layer, 0, 0)),
                      crow,
                      pl.BlockSpec((None, 1, d), lambda t, s, a, b: (t // per_b, 0, 0))],
            out_specs=crow,
            scratch_shapes=[pltpu.VMEM((rows, w), U32), pltpu.VMEM((w // LANES, 2 * EROWS, LANES), F32),
                            pltpu.VMEM((EROWS, w), U32), pltpu.VMEM((TOP_K * SUBLANES, w), U32),
                            pltpu.VMEM((w // LANES, 2 * TOP_K * SUBLANES, LANES), F32),
                            pltpu.VMEM((CROWS, d), F32),
                            pltpu.VMEM((2, d, 2 * EXPERT_DIM), BF16),
                            pltpu.VMEM((2, EXPERT_DIM, d), BF16),
                            pltpu.SemaphoreType.DMA((2, 2))]),
        out_shape=jax.ShapeDtypeStruct((n, d), F32),
        compiler_params=_cparams("arbitrary", "arbitrary"),
    )(seg_start, seg_count, h.reshape(2 * n, w), h, dest_rows, jnp.swapaxes(wts, 1, 2),
      w_gu, w_dn, s_gu, s_dn, xm, g2)


def _head_norm_rope(t, gain, cos, sin_a, sin_b, mul):
    r = lax.rsqrt(jnp.sum(t * t, axis=-1, keepdims=True) / QK_DIM + NORM_EPS)
    tn = t * r * gain
    half = ROPE_DIM // 2
    out = tn * cos + pltpu.roll(tn, LANES - half, 1) * sin_a + pltpu.roll(tn, half, 1) * sin_b
    return out * mul


def _mla_prep_kernel(x_ref, n1_ref, sc1_ref, sh1_ref, nk_ref, sck_ref, shk_ref,
                     wdq_ref, gql_ref, wuq_ref, gq_ref, wdown_ref, gkl_ref, wk_ref, gk_ref, wuv_ref,
                     cos_ref, sina_ref, sinb_ref, q_ref, k_ref, v_ref):
    xn = _rms(x_ref[...])
    hq = xn * n1_ref[...] * (1.0 + sc1_ref[...]) + sh1_ref[...]
    hk = xn * nk_ref[...] * (1.0 + sck_ref[...]) + shk_ref[...]
    cos, sin_a, sin_b = cos_ref[...], sina_ref[...], sinb_ref[...]
    n_heads = q_ref.shape[1] // LANES

    cq = _rms(jnp.dot(hq.astype(BF16), wdq_ref[...], preferred_element_type=F32)) * gql_ref[...]
    q_raw = jnp.dot(cq.astype(BF16), wuq_ref[...], preferred_element_type=F32)
    for h in range(n_heads):
        sl = slice(h * LANES, (h + 1) * LANES)
        q_ref[:, sl] = _head_norm_rope(q_raw[:, sl], gq_ref[...], cos, sin_a, sin_b,
                                       QK_DIM ** -0.5 * LOG2E).astype(q_ref.dtype)

    down = jnp.dot(hk.astype(BF16), wdown_ref[...], preferred_element_type=F32)
    ckv = (_rms(down[:, :KV_RANK]) * gkl_ref[...]).astype(BF16)
    pe = down[:, KV_RANK:]
    pe_hi = pe.astype(BF16)
    pe_lo = (pe - pe_hi.astype(F32)).astype(BF16)
    k_raw = jnp.dot(jnp.concatenate([ckv, pe_hi, pe_lo], axis=1), wk_ref[...],
                    preferred_element_type=F32)
    for h in range(n_heads):
        sl = slice(h * LANES, (h + 1) * LANES)
        k_ref[:, sl] = _head_norm_rope(k_raw[:, sl], gk_ref[...], cos, sin_a, sin_b,
                                       1.0).astype(k_ref.dtype)
    v_ref[...] = jnp.dot(ckv, wuv_ref[...], preferred_element_type=F32).astype(v_ref.dtype)


def _pad_heads(w, n_heads, width):
    r = w.shape[0]
    return jnp.pad(w.reshape(r, n_heads, width), ((0, 0), (0, 0), (0, LANES - width))).reshape(r, n_heads * LANES)


def _mla_prep(xf, pos, n1, sc1, sh1, nk, sck, shk, w_dq, g_ql, w_uq, g_q, w_down, g_kl, w_uk, w_uv,
              g_k, seq):
    n, d = xf.shape
    n_heads = w_uv.shape[1] // V_DIM
    q_rank = w_dq.shape[1]
    hw = n_heads * LANES
    tm = 256
    per_b = seq // tm

    half = ROPE_DIM // 2
    inv = jnp.power(ROPE_THETA, -jnp.arange(half, dtype=F32) / half)
    ang = pos.astype(F32).reshape(n, 1) * inv
    cos, sin = jnp.cos(ang), jnp.sin(ang)
    ones = jnp.ones((n, NOPE_DIM), F32)
    zeros = lambda wdt: jnp.zeros((n, wdt), F32)
    tail = LANES - QK_DIM
    cos_t = jnp.concatenate([ones, cos, cos, zeros(tail)], axis=1)
    sina_t = jnp.concatenate([zeros(NOPE_DIM), -sin, zeros(half), zeros(tail)], axis=1)
    sinb_t = jnp.concatenate([zeros(NOPE_DIM), zeros(half), sin, zeros(tail)], axis=1)

    pad_gain = lambda g: jnp.pad(g.reshape(1, QK_DIM), ((0, 0), (0, tail)))
    wuq_p = _pad_heads(w_uq, n_heads, QK_DIM).astype(BF16)
    wdown_p = jnp.pad(w_down, ((0, 0), (0, LANES - ROPE_DIM))).astype(BF16)
    eye = jnp.pad(jnp.eye(ROPE_DIM, dtype=F32), ((0, LANES - ROPE_DIM), (NOPE_DIM, tail)))
    pe_rows = jnp.tile(eye, (1, n_heads))
    wk = jnp.concatenate([_pad_heads(w_uk, n_heads, NOPE_DIM), pe_rows, pe_rows], axis=0).astype(BF16)

    vec = pl.BlockSpec((None, 1, d), lambda i: (i // per_b, 0, 0))
    full = lambda a: pl.BlockSpec(a.shape, lambda i: (0,) * a.ndim)
    row = lambda wdt: pl.BlockSpec((tm, wdt), lambda i: (i, 0))
    args = [xf, n1, sc1, sh1, nk, sck, shk,
            w_dq.astype(BF16), g_ql.reshape(1, q_rank), wuq_p, pad_gain(g_q),
            wdown_p, g_kl.reshape(1, KV_RANK), wk, pad_gain(g_k), w_uv.astype(BF16),
            cos_t, sina_t, sinb_t]
    specs = [row(d), full(n1), vec, vec, full(nk), vec, vec] + [full(a) for a in args[7:16]] + [row(LANES)] * 3
    return pl.pallas_call(
        _mla_prep_kernel,
        grid=(n // tm,),
        in_specs=specs,
        out_specs=[row(hw), row(hw), row(n_heads * V_DIM)],
        out_shape=[jax.ShapeDtypeStruct((n, hw), BF16), jax.ShapeDtypeStruct((n, hw), BF16),
                   jax.ShapeDtypeStruct((n, n_heads * V_DIM), BF16)],
        compiler_params=_cparams("parallel"),
    )(*args)


def _causal_bias(seq):
    a = np.arange(QBLK)[:, None]
    c = np.arange(seq)[None, :]
    return np.where(c <= (seq - QBLK) + a, 0.0, MASKED).astype(np.float32)


def _mla_attn_kernel(q_ref, k_ref, v_ref, mask_ref, o_ref, s_ref, p_ref):
    seq = q_ref.shape[0]
    lane_lo = lax.broadcasted_iota(jnp.int32, (QBLK, LANES), 1) < V_DIM
    for r0 in range(0, seq, QBLK):
        w = r0 + QBLK
        off = _band_offset(seq, w)
        o_a, o_b = _attend_pair((q_ref[r0:w, :LANES], q_ref[r0:w, LANES:]),
                                (k_ref[0:w, :LANES], k_ref[0:w, LANES:]), v_ref[0:w, :],
                                mask_ref[:, off:off + w], s_ref, p_ref, r0 // QBLK, w)
        o_ref[r0:w, :] = jnp.where(lane_lo, o_a, o_b).astype(o_ref.dtype)


def _mla_attention(q, k, v):
    bsz, seq, hw = q.shape
    n_pairs = hw // (2 * LANES)
    qk = pl.BlockSpec((None, seq, 2 * LANES), lambda b, h: (b, 0, h))
    vo = pl.BlockSpec((None, seq, LANES), lambda b, h: (b, 0, h))
    return pl.pallas_call(
        _mla_attn_kernel,
        grid=(bsz, n_pairs),
        in_specs=[qk, qk, vo, pl.BlockSpec((QBLK, seq), lambda b, h: (0, 0))],
        out_specs=vo,
        scratch_shapes=_attn_scratch(seq),
        out_shape=jax.ShapeDtypeStruct(v.shape, BF16),
        compiler_params=_cparams("parallel", "parallel"),
    )(q, k, v, jnp.asarray(_causal_bias(seq)))


def kernel(x, c, pos, ada_w, ada_b, norm1_g, norm2_g, a_w_qkv, a_q_norm, a_k_norm, a_w_o, kv_ada_w, kv_ada_b, kv_norm_g, kv_w_down, kv_latent_norm, kv_w_uk, kv_w_uv, kv_k_norm, b_w_dq, b_q_latent_norm, b_w_uq, b_q_norm, b_w_o, moe_w_router, moe_router_bias, moe_w_gate_up, moe_w_down, moe_shared_gate_up, moe_shared_down):
    bsz, seq, d = x.shape
    n = bsz * seq
    depth = ada_w.shape[0]
    n_a = a_w_qkv.shape[0]
    assert seq % QBLK == 0
    xf = x.reshape(n, d)

    w_gu = moe_w_gate_up.astype(BF16)
    w_dn = moe_w_down.astype(BF16)
    s_gu = moe_shared_gate_up.astype(BF16)
    s_dn = moe_shared_down.astype(BF16)
    row = lambda g: g.reshape(1, d)

    for layer in range(depth):
        mod = _modulation(c, ada_w, ada_b, layer).reshape(bsz, 6, 1, d)
        sh1, sc1, g1, sh2, sc2, g2 = [mod[:, i] for i in range(6)]
        if layer < n_a:
            qkv = _qkv_proj(xf, row(norm1_g[layer]), sc1, sh1, a_w_qkv[layer].astype(BF16), seq)
            o = _dilated_attention(qkv.reshape(bsz, seq, -1), pos, a_q_norm[layer], a_k_norm[layer])
            w_o = a_w_o[layer]
        else:
            j = layer - n_a
            kmod = _modulation(c, kv_ada_w[None], kv_ada_b[None], 0).reshape(bsz, 2, 1, d)
            q, k, v = _mla_prep(xf, pos, row(norm1_g[layer]), sc1, sh1, row(kv_norm_g), kmod[:, 1],
                                kmod[:, 0], b_w_dq[j], b_q_latent_norm[j], b_w_uq[j], b_q_norm[j],
                                kv_w_down, kv_latent_norm, kv_w_uk, kv_w_uv, kv_k_norm, seq)
            o = _mla_attention(q.reshape(bsz, seq, -1), k.reshape(bsz, seq, -1),
                               v.reshape(bsz, seq, -1))
            w_o = b_w_o[j]
        xm, h2, logits = _post_attn(o.reshape(n, d), w_o.astype(BF16), xf, g1, row(norm2_g[layer]),
                                    sc2, sh2, moe_w_router[layer], seq)
        xf = _moe(h2, xm, g2, logits, moe_router_bias[layer], w_gu, w_dn, s_gu, s_dn, layer, seq)
    return xf.reshape(bsz, seq, d)
```

```python
import functools
import math

import numpy as np
import jax
import jax.numpy as jnp
from jax import lax
from jax.experimental import pallas as pl
from jax.experimental.pallas import tpu as pltpu

F32 = jnp.float32
BF16 = jnp.bfloat16
U32 = jnp.uint32

NORM_EPS = 1e-6
MASKED = -1e30
LOG2E = math.log2(math.e)

LANES = 128
SUBLANES = 8
VMEM_LIMIT = 56 * 1024 * 1024

A_HEAD_DIM = 64
DILATED_PAIRS = ((128, 1), (512, 4), (2048, 16))
QBLK = 256
NOPE_DIM = 64
ROPE_DIM = 32
QK_DIM = NOPE_DIM + ROPE_DIM
V_DIM = 64
KV_RANK = 256
ROPE_THETA = 10000.0
N_EXPERTS = 64
N_GROUPS = 8
TOPK_GROUPS = 4
TOP_K = 6
EXPERT_DIM = 256
ROUTED_SCALE = 2.5
MOE_TILE = 2048
EROWS = 256
CROWS = 256

ROW_TILE = 512


def _cparams(*sem):
    return pltpu.CompilerParams(dimension_semantics=sem, vmem_limit_bytes=VMEM_LIMIT)


def _silu(v):
    return v * jax.nn.sigmoid(v)


def _rms(v):
    return v * lax.rsqrt(jnp.mean(v * v, axis=-1, keepdims=True) + NORM_EPS)


def _dot_nt(a, b):
    return lax.dot_general(a, b, (((1,), (1,)), ((), ())), preferred_element_type=F32)


def _mod_kernel(c_ref, w_ref, b_ref, o_ref):
    cond = _silu(c_ref[...])
    o_ref[...] = jnp.dot(cond, w_ref[...], preferred_element_type=F32,
                         precision=lax.Precision.HIGHEST) + b_ref[...]


def _modulation(c, w, b, layer):
    bsz, d = c.shape
    m = w.shape[-1]
    tn = 1024
    return pl.pallas_call(
        _mod_kernel,
        grid=(m // tn,),
        in_specs=[pl.BlockSpec((bsz, d), lambda j: (0, 0)),
                  pl.BlockSpec((None, d, tn), lambda j: (layer, 0, j)),
                  pl.BlockSpec((None, 1, tn), lambda j: (layer, 0, j))],
        out_specs=pl.BlockSpec((bsz, tn), lambda j: (0, j)),
        out_shape=jax.ShapeDtypeStruct((bsz, m), F32),
        compiler_params=_cparams("arbitrary"),
    )(c, w, b.reshape(b.shape[0], 1, m))


def _qkv_kernel(x_ref, g_ref, sc_ref, sh_ref, w_ref, o_ref):
    h = _rms(x_ref[...]) * g_ref[...]
    h = h * (1.0 + sc_ref[...]) + sh_ref[...]
    o_ref[...] = jnp.dot(h.astype(BF16), w_ref[...], preferred_element_type=F32).astype(o_ref.dtype)


def _qkv_proj(xf, g, sc, sh, w, seq):
    n, d = xf.shape
    m = w.shape[1]
    tm = min(ROW_TILE, seq)
    per_b = seq // tm
    vec = pl.BlockSpec((None, 1, d), lambda i: (i // per_b, 0, 0))
    return pl.pallas_call(
        _qkv_kernel,
        grid=(n // tm,),
        in_specs=[pl.BlockSpec((tm, d), lambda i: (i, 0)),
                  pl.BlockSpec((1, d), lambda i: (0, 0)),
                  vec, vec,
                  pl.BlockSpec((d, m), lambda i: (0, 0))],
        out_specs=pl.BlockSpec((tm, m), lambda i: (i, 0)),
        out_shape=jax.ShapeDtypeStruct((n, m), BF16),
        compiler_params=_cparams("parallel"),
    )(xf, g, sc, sh, w)


ATT_SLOTS = 4


def _attend_pair(qs, ks, v, bias, s_ref, p_ref, blk, w):
    slots = [(blk % 2) * 2 + h for h in range(2)]
    for h in range(2):
        s_ref[slots[h], :, 0:w] = _dot_nt(qs[h], ks[h]) + bias
    denom = []
    for h in range(2):
        s = s_ref[slots[h], :, 0:w]
        p = jnp.exp2(s - jnp.max(s, axis=-1, keepdims=True))
        denom.append(jnp.sum(p, axis=-1, keepdims=True))
        p_ref[slots[h], :, 0:w] = p.astype(BF16)
    return [jnp.dot(p_ref[slots[h], :, 0:w], v, preferred_element_type=F32) / denom[h]
            for h in range(2)]


def _attn_scratch(seq):
    return [pltpu.VMEM((ATT_SLOTS, QBLK, seq), F32), pltpu.VMEM((ATT_SLOTS, QBLK, seq), BF16)]


def _split3(v):
    p1 = v.astype(BF16).astype(F32)
    p2 = (v - p1).astype(BF16).astype(F32)
    return p1, p2, v - p1 - p2


def _band_offset(seq, w):
    return seq - w


def _dilated_log_multiplicity(seq):
    a = np.arange(QBLK)[:, None]
    c = np.arange(seq)[None, :]
    d = (seq - QBLK) + a - c
    mult = np.zeros((QBLK, seq), np.int32)
    for win, dil in DILATED_PAIRS:
        mult += ((d >= 0) & (d % dil == 0) & (d <= win)).astype(np.int32)
    return np.where(mult > 0, np.log2(np.maximum(mult, 1)), MASKED).astype(np.float32)


N_FEAT = 9


def _dil_attn_kernel(q_ref, k_ref, v_ref, posc_ref, sfeat_ref, gq_ref, gk_ref, logm_ref,
                     o_ref, qa_ref, qb_ref, ks_ref, s_ref, p_ref):
    seq = q_ref.shape[0]
    lane = lax.broadcasted_iota(jnp.int32, (QBLK, LANES), 1)
    lane_lo = lane < A_HEAD_DIM
    piece = [functools.reduce(jnp.logical_or, [lane == i + 3 * j for j in range(N_FEAT // 3)])
             for i in range(3)]

    def head_norm(ref, r0, g_ref, mul):
        t = ref[r0:r0 + QBLK, :].astype(F32)
        sq = t * t
        ssa = jnp.sum(jnp.where(lane_lo, sq, 0.0), axis=-1, keepdims=True)
        ssb = jnp.sum(jnp.where(lane_lo, 0.0, sq), axis=-1, keepdims=True)
        r = jnp.where(lane_lo, lax.rsqrt(ssa / A_HEAD_DIM + NORM_EPS),
                      lax.rsqrt(ssb / A_HEAD_DIM + NORM_EPS))
        return t * r * (g_ref[...] * mul)

    rel0 = posc_ref[0:1, :]
    for r0 in range(0, seq, QBLK):
        rows = slice(r0, r0 + QBLK)
        qn = head_norm(q_ref, r0, gq_ref, A_HEAD_DIM ** -0.5 * LOG2E)
        qa_ref[rows, :LANES] = jnp.where(lane_lo, qn, 0.0).astype(BF16)
        qb_ref[rows, :LANES] = jnp.where(lane_lo, 0.0, qn).astype(BF16)
        qa_ref[rows, LANES:] = jnp.broadcast_to(sfeat_ref[0:1, :], (QBLK, LANES)).astype(BF16)
        qb_ref[rows, LANES:] = jnp.broadcast_to(sfeat_ref[1:2, :], (QBLK, LANES)).astype(BF16)
        ks_ref[rows, :LANES] = head_norm(k_ref, r0, gk_ref, 1.0).astype(BF16)
        p1, p2, p3 = _split3((posc_ref[rows, :] - rel0).astype(F32))
        feat = jnp.where(piece[0], p1, jnp.where(piece[1], p2, jnp.where(piece[2], p3, 0.0)))
        ks_ref[rows, LANES:] = feat.astype(BF16)

    for r0 in range(0, seq, QBLK):
        w = r0 + QBLK
        off = _band_offset(seq, w)
        lm = logm_ref[:, off:off + w]
        kk = ks_ref[0:w, :]
        o_a, o_b = _attend_pair((qa_ref[r0:w, :], qb_ref[r0:w, :]), (kk, kk), v_ref[0:w, :], lm,
                                s_ref, p_ref, r0 // QBLK, w)
        o_ref[r0:w, :] = jnp.where(lane_lo, o_a, o_b).astype(o_ref.dtype)


def _dilated_attention(qkv, pos, q_norm, k_norm):
    bsz, seq, three_d = qkv.shape
    d = three_d // 3
    n_pairs = d // LANES
    n_heads = d // A_HEAD_DIM
    slopes = jnp.exp2(-8.0 * jnp.arange(1, n_heads + 1, dtype=F32) / n_heads) * LOG2E
    sfeat = jnp.repeat(jnp.stack(_split3(slopes), axis=1), 3, axis=1)
    sfeat = jnp.pad(sfeat, ((0, 0), (0, LANES - N_FEAT))).reshape(n_pairs, 2, LANES)
    logm = jnp.asarray(_dilated_log_multiplicity(seq))
    gq = jnp.tile(q_norm.reshape(1, A_HEAD_DIM), (1, 2))
    gk = jnp.tile(k_norm.reshape(1, A_HEAD_DIM), (1, 2))
    blk = lambda off: pl.BlockSpec((None, seq, LANES), lambda b, h: (b, 0, off + h))
    const2 = pl.BlockSpec((1, LANES), lambda b, h: (0, 0))
    ext = pltpu.VMEM((seq, 2 * LANES), BF16)
    return pl.pallas_call(
        _dil_attn_kernel,
        grid=(bsz, n_pairs),
        in_specs=[blk(0), blk(n_pairs), blk(2 * n_pairs),
                  pl.BlockSpec((None, seq, 1), lambda b, h: (b, 0, 0)),
                  pl.BlockSpec((None, 2, LANES), lambda b, h: (h, 0, 0)),
                  const2, const2,
                  pl.BlockSpec((QBLK, seq), lambda b, h: (0, 0))],
        out_specs=pl.BlockSpec((None, seq, LANES), lambda b, h: (b, 0, h)),
        scratch_shapes=[ext, ext, ext] + _attn_scratch(seq),
        out_shape=jax.ShapeDtypeStruct((bsz, seq, d), BF16),
        compiler_params=_cparams("parallel", "parallel"),
    )(qkv, qkv, qkv, pos.reshape(bsz, seq, 1), sfeat, gq, gk, logm)


def _post_attn_kernel(o_ref, wo_ref, x_ref, g1_ref, n2_ref, sc_ref, sh_ref, wr_ref,
                      xm_ref, h_ref, lg_ref):
    y = jnp.dot(o_ref[...], wo_ref[...], preferred_element_type=F32)
    xm = x_ref[...] + g1_ref[...] * y
    xm_ref[...] = xm
    h = _rms(xm) * n2_ref[...]
    h = h * (1.0 + sc_ref[...]) + sh_ref[...]
    h_ref[...] = h.astype(h_ref.dtype)
    lg_ref[...] = lax.dot_general(wr_ref[...], h, (((1,), (1,)), ((), ())),
                                  preferred_element_type=F32, precision=lax.Precision.HIGHEST)


def _post_attn(o, w_o, xf, g1, n2, sc2, sh2, w_router, seq):
    n, d = xf.shape
    ne = w_router.shape[1]
    tm = min(ROW_TILE, seq)
    per_b = seq // tm
    vec = pl.BlockSpec((None, 1, d), lambda i: (i // per_b, 0, 0))
    row = lambda w: pl.BlockSpec((tm, w), lambda i: (i, 0))
    return pl.pallas_call(
        _post_attn_kernel,
        grid=(n // tm,),
        in_specs=[row(d), pl.BlockSpec((d, d), lambda i: (0, 0)), row(d), vec,
                  pl.BlockSpec((1, d), lambda i: (0, 0)), vec, vec,
                  pl.BlockSpec((ne, d), lambda i: (0, 0))],
        out_specs=[row(d), row(d), pl.BlockSpec((ne, tm), lambda i: (0, i))],
        out_shape=[jax.ShapeDtypeStruct((n, d), F32), jax.ShapeDtypeStruct((n, d), BF16),
                   jax.ShapeDtypeStruct((ne, n), F32)],
        compiler_params=_cparams("parallel"),
    )(o, w_o, xf, g1, n2, sc2, sh2, w_router.T)


def _moe_rows(tile):
    rows = tile * TOP_K + N_EXPERTS * (SUBLANES - 1) + EROWS
    return -(-rows // SUBLANES) * SUBLANES


def _first_index(hit, index, limit):
    return jnp.min(jnp.where(hit, index, limit), axis=0, keepdims=True)


def _route_kernel(lg_ref, bias_ref, tri_ref, dest_ref, w_ref, seg_ref, cnt_ref):
    t = lg_ref.shape[1]
    per_g = N_EXPERTS // N_GROUPS
    sub = lax.broadcasted_iota(jnp.int32, (per_g, t), 0).astype(F32)
    neg = -jnp.inf
    scores, sel = [], []
    for g in range(N_GROUPS):
        rows = slice(g * per_g, (g + 1) * per_g)
        sc = jax.nn.sigmoid(lg_ref[rows, :])
        scores.append(sc)
        sel.append(sc + bias_ref[rows, :])

    gs = []
    for g in range(N_GROUPS):
        m1 = jnp.max(sel[g], axis=0, keepdims=True)
        first = _first_index(sel[g] == m1, sub, float(per_g))
        m2 = jnp.max(jnp.where(sub == first, neg, sel[g]), axis=0, keepdims=True)
        gs.append(m1 + m2)
    gsc = jnp.concatenate(gs, axis=0)
    gsub = lax.broadcasted_iota(jnp.int32, (N_GROUPS, t), 0).astype(F32)
    gkeep = jnp.zeros((N_GROUPS, t), F32)
    for _ in range(TOPK_GROUPS):
        m = jnp.max(gsc, axis=0, keepdims=True)
        pick = gsub == _first_index(gsc == m, gsub, float(N_GROUPS))
        gkeep = jnp.where(pick, 1.0, gkeep)
        gsc = jnp.where(pick, neg, gsc)

    eidx = [sub + float(g * per_g) for g in range(N_GROUPS)]
    masked = [jnp.where(jnp.broadcast_to(gkeep[g:g + 1, :], (per_g, t)) > 0.5, sel[g], neg)
              for g in range(N_GROUPS)]
    chosen = [jnp.zeros((per_g, t), F32) for _ in range(N_GROUPS)]
    wsel = [jnp.zeros((per_g, t), F32) for _ in range(N_GROUPS)]
    picked = []
    for _ in range(TOP_K):
        m = jnp.max(masked[0], axis=0, keepdims=True)
        for g in range(1, N_GROUPS):
            m = jnp.maximum(m, jnp.max(masked[g], axis=0, keepdims=True))
        e_first = _first_index(masked[0] == m, eidx[0], float(N_EXPERTS))
        for g in range(1, N_GROUPS):
            e_first = jnp.minimum(e_first, _first_index(masked[g] == m, eidx[g], float(N_EXPERTS)))
        picked.append(e_first)
        for g in range(N_GROUPS):
            pick = eidx[g] == e_first
            chosen[g] = jnp.where(pick, 1.0, chosen[g])
            wsel[g] = jnp.where(pick, scores[g], wsel[g])
            masked[g] = jnp.where(pick, neg, masked[g])

    denom = jnp.sum(wsel[0], axis=0, keepdims=True)
    for g in range(1, N_GROUPS):
        denom = denom + jnp.sum(wsel[g], axis=0, keepdims=True)

    c_all = jnp.concatenate(chosen, axis=0)
    rank = jnp.dot(c_all.astype(BF16), tri_ref[...], preferred_element_type=F32)
    counts = jnp.sum(c_all, axis=1, keepdims=True)
    padded = jnp.ceil(counts / SUBLANES) * SUBLANES
    below = (lax.broadcasted_iota(jnp.int32, (N_EXPERTS, N_EXPERTS), 1)
             < lax.broadcasted_iota(jnp.int32, (N_EXPERTS, N_EXPERTS), 0)).astype(F32)
    seg = jnp.dot(below, jnp.broadcast_to(padded, (N_EXPERTS, LANES)), preferred_element_type=F32,
                  precision=lax.Precision.HIGHEST)
    row_of = seg[:, :1] + rank

    dest_rows, w_rows = [], []
    for k in range(TOP_K):
        d_k = jnp.zeros((1, t), F32)
        w_k = jnp.zeros((1, t), F32)
        for g in range(N_GROUPS):
            hit = eidx[g] == picked[k]
            d_k = d_k + jnp.sum(jnp.where(hit, row_of[g * per_g:(g + 1) * per_g, :], 0.0), axis=0,
                                keepdims=True)
            w_k = w_k + jnp.sum(jnp.where(hit, wsel[g], 0.0), axis=0, keepdims=True)
        dest_rows.append(d_k)
        w_rows.append(w_k / denom * ROUTED_SCALE)
    fill = [jnp.zeros((SUBLANES - TOP_K, t), F32)]
    dest_ref[...] = jnp.concatenate(dest_rows + fill, axis=0).astype(jnp.int32)
    w_ref[...] = jnp.concatenate(w_rows + fill, axis=0)
    seg_ref[...] = seg.astype(jnp.int32)
    cnt_ref[...] = jnp.broadcast_to(counts, (N_EXPERTS, LANES)).astype(jnp.int32)


def _route(logits_t, router_bias, tile):
    ne, n = logits_t.shape
    nt = n // tile
    tri = jnp.asarray(np.triu(np.ones((tile, tile), np.float32), 1), dtype=BF16)
    per_tile = lambda rows, width: pl.BlockSpec((None, rows, width), lambda i: (i, 0, 0))
    dest, wts, seg, cnt = pl.pallas_call(
        _route_kernel,
        grid=(nt,),
        in_specs=[pl.BlockSpec((ne, tile), lambda i: (0, i)),
                  pl.BlockSpec((ne, 1), lambda i: (0, 0)),
                  pl.BlockSpec((tile, tile), lambda i: (0, 0))],
        out_specs=[per_tile(SUBLANES, tile), per_tile(SUBLANES, tile),
                   per_tile(ne, LANES), per_tile(ne, LANES)],
        out_shape=[jax.ShapeDtypeStruct((nt, SUBLANES, tile), jnp.int32),
                   jax.ShapeDtypeStruct((nt, SUBLANES, tile), F32),
                   jax.ShapeDtypeStruct((nt, ne, LANES), jnp.int32),
                   jax.ShapeDtypeStruct((nt, ne, LANES), jnp.int32)],
        compiler_params=_cparams("parallel"),
    )(logits_t, router_bias.astype(F32).reshape(ne, 1), tri)
    return seg[:, :, 0].reshape(-1), cnt[:, :, 0].reshape(-1), dest, wts


def _swiglu(xb, wgu_ref, wdn_ref):
    gu = jnp.dot(xb, wgu_ref[...], preferred_element_type=F32)
    half = wgu_ref.shape[1] // 2
    act = _silu(gu[:, :half]) * gu[:, half:]
    return jnp.dot(act.astype(BF16), wdn_ref[...], preferred_element_type=F32)


def _slabs_to_rows(p, il_ref, r):
    n = il_ref.shape[0] // r
    il_ref[...] = pltpu.bitcast(p, BF16).astype(F32)
    return jnp.concatenate([il_ref[pl.ds(2 * c + h, r, stride=n), :]
                            for h in range(2) for c in range(n // 2)], axis=1)


def _rows_to_slabs(v, il_ref):
    r = v.shape[0]
    n = il_ref.shape[0] // r
    for h in range(2):
        for c in range(n // 2):
            lane0 = (h * (n // 2) + c) * LANES
            il_ref[pl.ds(2 * c + h, r, stride=n), :] = v[:, lane0:lane0 + LANES]
    return pltpu.bitcast(il_ref[...].astype(BF16), U32)


def _dest_col(half, k, u):
    return (half * SUBLANES + k) * SUBLANES + u


def _moe_kernel(start_ref, count_ref, hp_ref, hb_ref, dest_ref, wt_ref, wgu_hbm, wdn_hbm,
                sgu_ref, sdn_ref, xm_ref, g2_ref, o_ref, xy_ref, il_ref, yp_ref, stage_ref, zf_ref,
                comb_ref, wgu_buf, wdn_buf, wsem, *, layer):
    t = pl.program_id(0)
    s = pl.program_id(1)
    slab = xm_ref.shape[1] // (2 * LANES)
    tile = hp_ref.shape[0] // (2 * slab)
    group_rows = SUBLANES * slab

    def weight_copies(e, slot):
        return (pltpu.make_async_copy(wgu_hbm.at[layer, e], wgu_buf.at[slot], wsem.at[0, slot]),
                pltpu.make_async_copy(wdn_hbm.at[layer, e], wdn_buf.at[slot], wsem.at[1, slot]))

    def slab_at(ref, first):
        return ref.at[pl.ds(pl.multiple_of(first, slab), slab), :]

    @pl.when(s == 0)
    def _first_weights():
        for cp in weight_copies(0, 0):
            cp.start()

    @pl.when(jnp.logical_and(t == 0, s == 0))
    def _clear():
        xy_ref[...] = jnp.zeros(xy_ref.shape, U32)

    @pl.when(s == 0)
    def _dispatch():
        def group(j, carry):
            for half in range(2):
                p0 = pl.multiple_of((2 * j + half) * 2 * group_rows, 2 * group_rows)
                rows = pltpu.bitcast(hp_ref[pl.ds(p0, 2 * group_rows), :], U32)
                for u in range(SUBLANES):
                    for k in range(TOP_K):
                        slab_at(xy_ref, dest_ref[j, _dest_col(half, k, u)])[...] = (
                            rows[u * slab:(u + 1) * slab, :])
            return carry

        lax.fori_loop(0, tile // (2 * SUBLANES), group, 0)

    @pl.when(s == 0)
    def _experts():
        def expert(e, carry0):
            slot = lax.rem(e, 2)

            @pl.when(e + 1 < N_EXPERTS)
            def _prefetch():
                for cp in weight_copies(e + 1, 1 - slot):
                    cp.start()

            for cp in weight_copies(e, slot):
                cp.wait()
            start = start_ref[t * N_EXPERTS + e]
            count = count_ref[t * N_EXPERTS + e]

            def chunk(c, carry):
                base = pl.multiple_of((start + c * EROWS) * slab, group_rows)
                groups = (jnp.minimum(count - c * EROWS, EROWS) + SUBLANES - 1) // SUBLANES
                xb = _slabs_to_rows(xy_ref[pl.ds(base, EROWS * slab), :], il_ref, EROWS).astype(BF16)
                y = _swiglu(xb, wgu_buf.at[slot], wdn_buf.at[slot])
                yp_ref[...] = _rows_to_slabs(y, il_ref)

                def put(j, c2):
                    r0 = pl.multiple_of(j * group_rows, group_rows)
                    xy_ref[pl.ds(base + r0, group_rows), :] = yp_ref[pl.ds(r0, group_rows), :]
                    return c2

                lax.fori_loop(0, groups, put, 0)
                return carry

            lax.fori_loop(0, (count + EROWS - 1) // EROWS, chunk, 0)
            return carry0

        lax.fori_loop(0, N_EXPERTS, expert, 0)

    @pl.when(s >= 1)
    def _combine():
        sb = s - 1

        def group(j, carry):
            g2 = sb * (CROWS // (2 * SUBLANES)) + j
            for half in range(2):
                r0 = pl.multiple_of(j * 2 * SUBLANES + half * SUBLANES, SUBLANES)
                t0 = pl.multiple_of(sb * CROWS + r0, SUBLANES)
                for k in range(TOP_K):
                    for u in range(SUBLANES):
                        first = (k * SUBLANES + u) * slab
                        stage_ref[first:first + slab, :] = (
                            slab_at(xy_ref, dest_ref[g2, _dest_col(half, k, u)])[...])
                rows = _slabs_to_rows(stage_ref[...], zf_ref, TOP_K * SUBLANES)
                wv = wt_ref[pl.ds(t0, SUBLANES), :]
                acc = wv[:, 0:1] * rows[0:SUBLANES, :]
                for k in range(1, TOP_K):
                    acc = acc + wv[:, k:k + 1] * rows[k * SUBLANES:(k + 1) * SUBLANES, :]
                comb_ref[pl.ds(r0, SUBLANES), :] = acc
            return carry

        lax.fori_loop(0, CROWS // (2 * SUBLANES), group, 0)
        shared = _swiglu(hb_ref[...], sgu_ref, sdn_ref)
        o_ref[...] = xm_ref[...] + g2_ref[...] * (comb_ref[...] + shared)


def _moe(h, xm, g2, logits_t, router_bias, w_gu, w_dn, s_gu, s_dn, layer, seq):
    n, d = h.shape
    slab = d // (2 * LANES)
    tile = min(MOE_TILE, seq)
    nt = n // tile
    per_b = seq // tile
    csteps = tile // CROWS
    assert seq % tile == 0 and tile % CROWS == 0
    seg_start, seg_count, dest, wts = _route(logits_t, router_bias, tile)
    dest_rows = dest.reshape(nt, SUBLANES, tile // (2 * SUBLANES), 2, SUBLANES) * slab
    dest_rows = dest_rows.transpose(0, 2, 3, 1, 4).reshape(nt, tile // (2 * SUBLANES), -1)
    h_slabs = h.reshape(n, 2, slab, LANES).transpose(0, 2, 1, 3).reshape(n * 2 * slab, LANES)
    rows = _moe_rows(tile)
    cst = lambda s: jnp.maximum(s - 1, 0)
    crow = pl.BlockSpec((CROWS, d), lambda t, s, a, b: (t * csteps + cst(s), 0))
    hbm = pl.BlockSpec(memory_space=pl.ANY)
    return pl.pallas_call(
        functools.partial(_moe_kernel, layer=layer),
        grid_spec=pltpu.PrefetchScalarGridSpec(
            num_scalar_prefetch=2,
            grid=(nt, 1 + csteps),
            in_specs=[pl.BlockSpec((tile * 2 * slab, LANES), lambda t, s, a, b: (t, 0)),
                      crow,
                      pl.BlockSpec((None, tile // (2 * SUBLANES), 2 * SUBLANES * SUBLANES),
                                   lambda t, s, a, b: (t, 0, 0), memory_space=pltpu.SMEM),
                      pl.BlockSpec((None, tile, SUBLANES), lambda t, s, a, b: (t, 0, 0)),
                      hbm, hbm,
                      pl.BlockSpec((None,) + s_gu.shape[1:], lambda t, s, a, b: (layer, 0, 0)),
                      pl.BlockSpec((None,) + s_dn.shape[1:], lambda t, s, a, b: (layer, 0, 0)),
                      crow,
                      pl.BlockSpec((None, 1, d), lambda t, s, a, b: (t // per_b, 0, 0))],
            out_specs=crow,
            scratch_shapes=[pltpu.VMEM((rows * slab, LANES), U32),
                            pltpu.VMEM((EROWS * 2 * slab, LANES), F32),
                            pltpu.VMEM((EROWS * slab, LANES), U32),
                            pltpu.VMEM((TOP_K * SUBLANES * slab, LANES), U32),
                            pltpu.VMEM((TOP_K * SUBLANES * 2 * slab, LANES), F32),
                            pltpu.VMEM((CROWS, d), F32),
                            pltpu.VMEM((2, d, 2 * EXPERT_DIM), BF16),
                            pltpu.VMEM((2, EXPERT_DIM, d), BF16),
                            pltpu.SemaphoreType.DMA((2, 2))]),
        out_shape=jax.ShapeDtypeStruct((n, d), F32),
        compiler_params=_cparams("arbitrary", "arbitrary"),
    )(seg_start, seg_count, h_slabs, h, dest_rows, jnp.swapaxes(wts, 1, 2),
      w_gu, w_dn, s_gu, s_dn, xm, g2)


def _head_norm_rope(t, gain, cos, sin_a, sin_b, mul):
    r = lax.rsqrt(jnp.sum(t * t, axis=-1, keepdims=True) / QK_DIM + NORM_EPS)
    tn = t * r * gain
    half = ROPE_DIM // 2
    out = tn * cos + pltpu.roll(tn, LANES - half, 1) * sin_a + pltpu.roll(tn, half, 1) * sin_b
    return out * mul


def _mla_prep_kernel(x_ref, n1_ref, sc1_ref, sh1_ref, nk_ref, sck_ref, shk_ref,
                     wdq_ref, gql_ref, wuq_ref, gq_ref, wdown_ref, gkl_ref, wk_ref, gk_ref, wuv_ref,
                     cos_ref, sina_ref, sinb_ref, q_ref, k_ref, v_ref):
    xn = _rms(x_ref[...])
    hq = xn * n1_ref[...] * (1.0 + sc1_ref[...]) + sh1_ref[...]
    hk = xn * nk_ref[...] * (1.0 + sck_ref[...]) + shk_ref[...]
    cos, sin_a, sin_b = cos_ref[...], sina_ref[...], sinb_ref[...]
    n_heads = q_ref.shape[1] // LANES

    cq = _rms(jnp.dot(hq.astype(BF16), wdq_ref[...], preferred_element_type=F32)) * gql_ref[...]
    q_raw = jnp.dot(cq.astype(BF16), wuq_ref[...], preferred_element_type=F32)
    for h in range(n_heads):
        sl = slice(h * LANES, (h + 1) * LANES)
        q_ref[:, sl] = _head_norm_rope(q_raw[:, sl], gq_ref[...], cos, sin_a, sin_b,
                                       QK_DIM ** -0.5 * LOG2E).astype(q_ref.dtype)

    down = jnp.dot(hk.astype(BF16), wdown_ref[...], preferred_element_type=F32)
    ckv = (_rms(down[:, :KV_RANK]) * gkl_ref[...]).astype(BF16)
    pe = down[:, KV_RANK:]
    pe_hi = pe.astype(BF16)
    pe_lo = (pe - pe_hi.astype(F32)).astype(BF16)
    k_raw = jnp.dot(jnp.concatenate([ckv, pe_hi, pe_lo], axis=1), wk_ref[...],
                    preferred_element_type=F32)
    for h in range(n_heads):
        sl = slice(h * LANES, (h + 1) * LANES)
        k_ref[:, sl] = _head_norm_rope(k_raw[:, sl], gk_ref[...], cos, sin_a, sin_b,
                                       1.0).astype(k_ref.dtype)
    v_ref[...] = jnp.dot(ckv, wuv_ref[...], preferred_element_type=F32).astype(v_ref.dtype)


def _pad_heads(w, n_heads, width):
    r = w.shape[0]
    return jnp.pad(w.reshape(r, n_heads, width), ((0, 0), (0, 0), (0, LANES - width))).reshape(r, n_heads * LANES)


def _mla_prep(xf, pos, n1, sc1, sh1, nk, sck, shk, w_dq, g_ql, w_uq, g_q, w_down, g_kl, w_uk, w_uv,
              g_k, seq):
    n, d = xf.shape
    n_heads = w_uv.shape[1] // V_DIM
    q_rank = w_dq.shape[1]
    hw = n_heads * LANES
    tm = 256
    per_b = seq // tm

    half = ROPE_DIM // 2
    inv = jnp.power(ROPE_THETA, -jnp.arange(half, dtype=F32) / half)
    ang = pos.astype(F32).reshape(n, 1) * inv
    cos, sin = jnp.cos(ang), jnp.sin(ang)
    ones = jnp.ones((n, NOPE_DIM), F32)
    zeros = lambda wdt: jnp.zeros((n, wdt), F32)
    tail = LANES - QK_DIM
    cos_t = jnp.concatenate([ones, cos, cos, zeros(tail)], axis=1)
    sina_t = jnp.concatenate([zeros(NOPE_DIM), -sin, zeros(half), zeros(tail)], axis=1)
    sinb_t = jnp.concatenate([zeros(NOPE_DIM), zeros(half), sin, zeros(tail)], axis=1)

    pad_gain = lambda g: jnp.pad(g.reshape(1, QK_DIM), ((0, 0), (0, tail)))
    wuq_p = _pad_heads(w_uq, n_heads, QK_DIM).astype(BF16)
    wdown_p = jnp.pad(w_down, ((0, 0), (0, LANES - ROPE_DIM))).astype(BF16)
    eye = jnp.pad(jnp.eye(ROPE_DIM, dtype=F32), ((0, LANES - ROPE_DIM), (NOPE_DIM, tail)))
    pe_rows = jnp.tile(eye, (1, n_heads))
    wk = jnp.concatenate([_pad_heads(w_uk, n_heads, NOPE_DIM), pe_rows, pe_rows], axis=0).astype(BF16)

    vec = pl.BlockSpec((None, 1, d), lambda i: (i // per_b, 0, 0))
    full = lambda a: pl.BlockSpec(a.shape, lambda i: (0,) * a.ndim)
    row = lambda wdt: pl.BlockSpec((tm, wdt), lambda i: (i, 0))
    args = [xf, n1, sc1, sh1, nk, sck, shk,
            w_dq.astype(BF16), g_ql.reshape(1, q_rank), wuq_p, pad_gain(g_q),
            wdown_p, g_kl.reshape(1, KV_RANK), wk, pad_gain(g_k), w_uv.astype(BF16),
            cos_t, sina_t, sinb_t]
    specs = [row(d), full(n1), vec, vec, full(nk), vec, vec] + [full(a) for a in args[7:16]] + [row(LANES)] * 3
    return pl.pallas_call(
        _mla_prep_kernel,
        grid=(n // tm,),
        in_specs=specs,
        out_specs=[row(hw), row(hw), row(n_heads * V_DIM)],
        out_shape=[jax.ShapeDtypeStruct((n, hw), BF16), jax.ShapeDtypeStruct((n, hw), BF16),
                   jax.ShapeDtypeStruct((n, n_heads * V_DIM), BF16)],
        compiler_params=_cparams("parallel"),
    )(*args)


def _causal_bias(seq):
    a = np.arange(QBLK)[:, None]
    c = np.arange(seq)[None, :]
    return np.where(c <= (seq - QBLK) + a, 0.0, MASKED).astype(np.float32)


def _mla_attn_kernel(q_ref, k_ref, v_ref, mask_ref, o_ref, s_ref, p_ref):
    seq = q_ref.shape[0]
    lane_lo = lax.broadcasted_iota(jnp.int32, (QBLK, LANES), 1) < V_DIM
    for r0 in range(0, seq, QBLK):
        w = r0 + QBLK
        off = _band_offset(seq, w)
        o_a, o_b = _attend_pair((q_ref[r0:w, :LANES], q_ref[r0:w, LANES:]),
                                (k_ref[0:w, :LANES], k_ref[0:w, LANES:]), v_ref[0:w, :],
                                mask_ref[:, off:off + w], s_ref, p_ref, r0 // QBLK, w)
        o_ref[r0:w, :] = jnp.where(lane_lo, o_a, o_b).astype(o_ref.dtype)


def _mla_attention(q, k, v):
    bsz, seq, hw = q.shape
    n_pairs = hw // (2 * LANES)
    qk = pl.BlockSpec((None, seq, 2 * LANES), lambda b, h: (b, 0, h))
    vo = pl.BlockSpec((None, seq, LANES), lambda b, h: (b, 0, h))
    return pl.pallas_call(
        _mla_attn_kernel,
        grid=(bsz, n_pairs),
        in_specs=[qk, qk, vo, pl.BlockSpec((QBLK, seq), lambda b, h: (0, 0))],
        out_specs=vo,
        scratch_shapes=_attn_scratch(seq),
        out_shape=jax.ShapeDtypeStruct(v.shape, BF16),
        compiler_params=_cparams("parallel", "parallel"),
    )(q, k, v, jnp.asarray(_causal_bias(seq)))


def kernel(x, c, pos, ada_w, ada_b, norm1_g, norm2_g, a_w_qkv, a_q_norm, a_k_norm, a_w_o, kv_ada_w, kv_ada_b, kv_norm_g, kv_w_down, kv_latent_norm, kv_w_uk, kv_w_uv, kv_k_norm, b_w_dq, b_q_latent_norm, b_w_uq, b_q_norm, b_w_o, moe_w_router, moe_router_bias, moe_w_gate_up, moe_w_down, moe_shared_gate_up, moe_shared_down):
    bsz, seq, d = x.shape
    n = bsz * seq
    depth = ada_w.shape[0]
    n_a = a_w_qkv.shape[0]
    assert seq % QBLK == 0
    xf = x.reshape(n, d)

    w_gu = moe_w_gate_up.astype(BF16)
    w_dn = moe_w_down.astype(BF16)
    s_gu = moe_shared_gate_up.astype(BF16)
    s_dn = moe_shared_down.astype(BF16)
    row = lambda g: g.reshape(1, d)

    for layer in range(depth):
        mod = _modulation(c, ada_w, ada_b, layer).reshape(bsz, 6, 1, d)
        sh1, sc1, g1, sh2, sc2, g2 = [mod[:, i] for i in range(6)]
        if layer < n_a:
            qkv = _qkv_proj(xf, row(norm1_g[layer]), sc1, sh1, a_w_qkv[layer].astype(BF16), seq)
            o = _dilated_attention(qkv.reshape(bsz, seq, -1), pos, a_q_norm[layer], a_k_norm[layer])
            w_o = a_w_o[layer]
        else:
            j = layer - n_a
            kmod = _modulation(c, kv_ada_w[None], kv_ada_b[None], 0).reshape(bsz, 2, 1, d)
            q, k, v = _mla_prep(xf, pos, row(norm1_g[layer]), sc1, sh1, row(kv_norm_g), kmod[:, 1],
                                kmod[:, 0], b_w_dq[j], b_q_latent_norm[j], b_w_uq[j], b_q_norm[j],
                                kv_w_down, kv_latent_norm, kv_w_uk, kv_w_uv, kv_k_norm, seq)
            o = _mla_attention(q.reshape(bsz, seq, -1), k.reshape(bsz, seq, -1),
                               v.reshape(bsz, seq, -1))
            w_o = b_w_o[j]
        xm, h2, logits = _post_attn(o.reshape(n, d), w_o.astype(BF16), xf, g1, row(norm2_g[layer]),
                                    sc2, sh2, moe_w_router[layer], seq)
        xf = _moe(h2, xm, g2, logits, moe_router_bias[layer], w_gu, w_dn, s_gu, s_dn, layer, seq)
    return xf.reshape(bsz, seq, d)
```

```python
import functools
import math

import numpy as np
import jax
import jax.numpy as jnp
from jax import lax
from jax.experimental import pallas as pl
from jax.experimental.pallas import tpu as pltpu

F32 = jnp.float32
BF16 = jnp.bfloat16
U32 = jnp.uint32

NORM_EPS = 1e-6
MASKED = -1e30
LOG2E = math.log2(math.e)

LANES = 128
SUBLANES = 8
VMEM_LIMIT = 56 * 1024 * 1024

A_HEAD_DIM = 64
DILATED_PAIRS = ((128, 1), (512, 4), (2048, 16))
QBLK = 256
NOPE_DIM = 64
ROPE_DIM = 32
QK_DIM = NOPE_DIM + ROPE_DIM
V_DIM = 64
KV_RANK = 256
ROPE_THETA = 10000.0
N_EXPERTS = 64
N_GROUPS = 8
TOPK_GROUPS = 4
TOP_K = 6
EXPERT_DIM = 256
ROUTED_SCALE = 2.5
MOE_TILE = 2048
EROWS = 256
CROWS = 256

ROW_TILE = 512


def _cparams(*sem):
    return pltpu.CompilerParams(dimension_semantics=sem, vmem_limit_bytes=VMEM_LIMIT)


def _silu(v):
    return v * jax.nn.sigmoid(v)


def _rms(v):
    return v * lax.rsqrt(jnp.mean(v * v, axis=-1, keepdims=True) + NORM_EPS)


def _dot_nt(a, b):
    return lax.dot_general(a, b, (((1,), (1,)), ((), ())), preferred_element_type=F32)


def _mod_kernel(c_ref, w_ref, b_ref, o_ref):
    cond = _silu(c_ref[...])
    o_ref[...] = jnp.dot(cond, w_ref[...], preferred_element_type=F32,
                         precision=lax.Precision.HIGHEST) + b_ref[...]


def _modulation(c, w, b, layer):
    bsz, d = c.shape
    m = w.shape[-1]
    tn = 1024
    return pl.pallas_call(
        _mod_kernel,
        grid=(m // tn,),
        in_specs=[pl.BlockSpec((bsz, d), lambda j: (0, 0)),
                  pl.BlockSpec((None, d, tn), lambda j: (layer, 0, j)),
                  pl.BlockSpec((None, 1, tn), lambda j: (layer, 0, j))],
        out_specs=pl.BlockSpec((bsz, tn), lambda j: (0, j)),
        out_shape=jax.ShapeDtypeStruct((bsz, m), F32),
        compiler_params=_cparams("arbitrary"),
    )(c, w, b.reshape(b.shape[0], 1, m))


def _qkv_kernel(x_ref, g_ref, sc_ref, sh_ref, w_ref, o_ref):
    h = _rms(x_ref[...]) * g_ref[...]
    h = h * (1.0 + sc_ref[...]) + sh_ref[...]
    o_ref[...] = jnp.dot(h.astype(BF16), w_ref[...], preferred_element_type=F32).astype(o_ref.dtype)


def _qkv_proj(xf, g, sc, sh, w, seq):
    n, d = xf.shape
    m = w.shape[1]
    tm = min(ROW_TILE, seq)
    per_b = seq // tm
    vec = pl.BlockSpec((None, 1, d), lambda i: (i // per_b, 0, 0))
    return pl.pallas_call(
        _qkv_kernel,
        grid=(n // tm,),
        in_specs=[pl.BlockSpec((tm, d), lambda i: (i, 0)),
                  pl.BlockSpec((1, d), lambda i: (0, 0)),
                  vec, vec,
                  pl.BlockSpec((d, m), lambda i: (0, 0))],
        out_specs=pl.BlockSpec((tm, m), lambda i: (i, 0)),
        out_shape=jax.ShapeDtypeStruct((n, m), BF16),
        compiler_params=_cparams("parallel"),
    )(xf, g, sc, sh, w)


ATT_SLOTS = 4


def _attention_blocks(seq, qk_of, v_ref, bias_ref, s_ref, p_ref, o_ref, lane_lo):
    nblk = seq // QBLK

    def slot(blk, h):
        return (blk % 2) * 2 + h

    def scores(blk):
        w = (blk + 1) * QBLK
        bias = bias_ref[:, seq - w:seq]
        for h in range(2):
            q, k = qk_of(blk, h, w)
            s_ref[slot(blk, h), :, 0:w] = _dot_nt(q, k) + bias

    for blk in range(nblk):
        w = (blk + 1) * QBLK
        scores(blk)
        denom = []
        for h in range(2):
            s = s_ref[slot(blk, h), :, 0:w]
            p = jnp.exp2(s - jnp.max(s, axis=-1, keepdims=True))
            denom.append(jnp.sum(p, axis=-1, keepdims=True))
            p_ref[slot(blk, h), :, 0:w] = p.astype(BF16)
        o_a, o_b = [jnp.dot(p_ref[slot(blk, h), :, 0:w], v_ref[0:w, :],
                            preferred_element_type=F32) / denom[h] for h in range(2)]
        o_ref[blk * QBLK:w, :] = jnp.where(lane_lo, o_a, o_b).astype(o_ref.dtype)


def _attn_scratch(seq):
    return [pltpu.VMEM((ATT_SLOTS, QBLK, seq), F32), pltpu.VMEM((ATT_SLOTS, QBLK, seq), BF16)]


def _split3(v):
    p1 = v.astype(BF16).astype(F32)
    p2 = (v - p1).astype(BF16).astype(F32)
    return p1, p2, v - p1 - p2


def _dilated_log_multiplicity(seq):
    a = np.arange(QBLK)[:, None]
    c = np.arange(seq)[None, :]
    d = (seq - QBLK) + a - c
    mult = np.zeros((QBLK, seq), np.int32)
    for win, dil in DILATED_PAIRS:
        mult += ((d >= 0) & (d % dil == 0) & (d <= win)).astype(np.int32)
    return np.where(mult > 0, np.log2(np.maximum(mult, 1)), MASKED).astype(np.float32)


N_FEAT = 9


def _dil_attn_kernel(q_ref, k_ref, v_ref, posc_ref, sfeat_ref, gq_ref, gk_ref, logm_ref,
                     o_ref, qa_ref, qb_ref, ks_ref, s_ref, p_ref):
    seq = q_ref.shape[0]
    lane = lax.broadcasted_iota(jnp.int32, (QBLK, LANES), 1)
    lane_lo = lane < A_HEAD_DIM
    piece = [functools.reduce(jnp.logical_or, [lane == i + 3 * j for j in range(N_FEAT // 3)])
             for i in range(3)]

    def head_norm(ref, r0, g_ref, mul):
        t = ref[r0:r0 + QBLK, :].astype(F32)
        sq = t * t
        ssa = jnp.sum(jnp.where(lane_lo, sq, 0.0), axis=-1, keepdims=True)
        ssb = jnp.sum(jnp.where(lane_lo, 0.0, sq), axis=-1, keepdims=True)
        r = jnp.where(lane_lo, lax.rsqrt(ssa / A_HEAD_DIM + NORM_EPS),
                      lax.rsqrt(ssb / A_HEAD_DIM + NORM_EPS))
        return t * r * (g_ref[...] * mul)

    rel0 = posc_ref[0:1, :]
    for r0 in range(0, seq, QBLK):
        rows = slice(r0, r0 + QBLK)
        qn = head_norm(q_ref, r0, gq_ref, A_HEAD_DIM ** -0.5 * LOG2E)
        qa_ref[rows, :LANES] = jnp.where(lane_lo, qn, 0.0).astype(BF16)
        qb_ref[rows, :LANES] = jnp.where(lane_lo, 0.0, qn).astype(BF16)
        qa_ref[rows, LANES:] = jnp.broadcast_to(sfeat_ref[0:1, :], (QBLK, LANES)).astype(BF16)
        qb_ref[rows, LANES:] = jnp.broadcast_to(sfeat_ref[1:2, :], (QBLK, LANES)).astype(BF16)
        ks_ref[rows, :LANES] = head_norm(k_ref, r0, gk_ref, 1.0).astype(BF16)
        p1, p2, p3 = _split3((posc_ref[rows, :] - rel0).astype(F32))
        feat = jnp.where(piece[0], p1, jnp.where(piece[1], p2, jnp.where(piece[2], p3, 0.0)))
        ks_ref[rows, LANES:] = feat.astype(BF16)

    def qk_of(blk, h, w):
        return (qa_ref, qb_ref)[h][w - QBLK:w, :], ks_ref[0:w, :]

    _attention_blocks(seq, qk_of, v_ref, logm_ref, s_ref, p_ref, o_ref, lane_lo)


def _dilated_attention(qkv, pos, q_norm, k_norm):
    bsz, seq, three_d = qkv.shape
    d = three_d // 3
    n_pairs = d // LANES
    n_heads = d // A_HEAD_DIM
    slopes = jnp.exp2(-8.0 * jnp.arange(1, n_heads + 1, dtype=F32) / n_heads) * LOG2E
    sfeat = jnp.repeat(jnp.stack(_split3(slopes), axis=1), 3, axis=1)
    sfeat = jnp.pad(sfeat, ((0, 0), (0, LANES - N_FEAT))).reshape(n_pairs, 2, LANES)
    logm = jnp.asarray(_dilated_log_multiplicity(seq))
    gq = jnp.tile(q_norm.reshape(1, A_HEAD_DIM), (1, 2))
    gk = jnp.tile(k_norm.reshape(1, A_HEAD_DIM), (1, 2))
    blk = lambda off: pl.BlockSpec((None, seq, LANES), lambda b, h: (b, 0, off + h))
    const2 = pl.BlockSpec((1, LANES), lambda b, h: (0, 0))
    ext = pltpu.VMEM((seq, 2 * LANES), BF16)
    return pl.pallas_call(
        _dil_attn_kernel,
        grid=(bsz, n_pairs),
        in_specs=[blk(0), blk(n_pairs), blk(2 * n_pairs),
                  pl.BlockSpec((None, seq, 1), lambda b, h: (b, 0, 0)),
                  pl.BlockSpec((None, 2, LANES), lambda b, h: (h, 0, 0)),
                  const2, const2,
                  pl.BlockSpec((QBLK, seq), lambda b, h: (0, 0))],
        out_specs=pl.BlockSpec((None, seq, LANES), lambda b, h: (b, 0, h)),
        scratch_shapes=[ext, ext, ext] + _attn_scratch(seq),
        out_shape=jax.ShapeDtypeStruct((bsz, seq, d), BF16),
        compiler_params=_cparams("parallel", "parallel"),
    )(qkv, qkv, qkv, pos.reshape(bsz, seq, 1), sfeat, gq, gk, logm)


def _post_attn_kernel(o_ref, wo_ref, x_ref, g1_ref, n2_ref, sc_ref, sh_ref, wr_ref,
                      xm_ref, h_ref, lg_ref):
    y = jnp.dot(o_ref[...], wo_ref[...], preferred_element_type=F32)
    xm = x_ref[...] + g1_ref[...] * y
    xm_ref[...] = xm
    h = _rms(xm) * n2_ref[...]
    h = h * (1.0 + sc_ref[...]) + sh_ref[...]
    h_ref[...] = h.astype(h_ref.dtype)
    lg_ref[...] = lax.dot_general(wr_ref[...], h, (((1,), (1,)), ((), ())),
                                  preferred_element_type=F32, precision=lax.Precision.HIGHEST)


def _post_attn(o, w_o, xf, g1, n2, sc2, sh2, w_router, seq):
    n, d = xf.shape
    ne = w_router.shape[1]
    tm = min(ROW_TILE, seq)
    per_b = seq // tm
    vec = pl.BlockSpec((None, 1, d), lambda i: (i // per_b, 0, 0))
    row = lambda w: pl.BlockSpec((tm, w), lambda i: (i, 0))
    return pl.pallas_call(
        _post_attn_kernel,
        grid=(n // tm,),
        in_specs=[row(d), pl.BlockSpec((d, d), lambda i: (0, 0)), row(d), vec,
                  pl.BlockSpec((1, d), lambda i: (0, 0)), vec, vec,
                  pl.BlockSpec((ne, d), lambda i: (0, 0))],
        out_specs=[row(d), row(d), pl.BlockSpec((ne, tm), lambda i: (0, i))],
        out_shape=[jax.ShapeDtypeStruct((n, d), F32), jax.ShapeDtypeStruct((n, d), BF16),
                   jax.ShapeDtypeStruct((ne, n), F32)],
        compiler_params=_cparams("parallel"),
    )(o, w_o, xf, g1, n2, sc2, sh2, w_router.T)


def _moe_rows(tile):
    rows = tile * TOP_K + N_EXPERTS * (SUBLANES - 1) + EROWS
    return -(-rows // SUBLANES) * SUBLANES


def _first_index(hit, index, limit):
    return jnp.min(jnp.where(hit, index, limit), axis=0, keepdims=True)


def _route_kernel(lg_ref, bias_ref, tri_ref, dest_ref, w_ref, seg_ref, cnt_ref):
    t = lg_ref.shape[1]
    per_g = N_EXPERTS // N_GROUPS
    sub = lax.broadcasted_iota(jnp.int32, (per_g, t), 0).astype(F32)
    neg = -jnp.inf
    scores, sel = [], []
    for g in range(N_GROUPS):
        rows = slice(g * per_g, (g + 1) * per_g)
        sc = jax.nn.sigmoid(lg_ref[rows, :])
        scores.append(sc)
        sel.append(sc + bias_ref[rows, :])

    gs = []
    for g in range(N_GROUPS):
        m1 = jnp.max(sel[g], axis=0, keepdims=True)
        first = _first_index(sel[g] == m1, sub, float(per_g))
        m2 = jnp.max(jnp.where(sub == first, neg, sel[g]), axis=0, keepdims=True)
        gs.append(m1 + m2)
    gsc = jnp.concatenate(gs, axis=0)
    gsub = lax.broadcasted_iota(jnp.int32, (N_GROUPS, t), 0).astype(F32)
    gkeep = jnp.zeros((N_GROUPS, t), F32)
    for _ in range(TOPK_GROUPS):
        m = jnp.max(gsc, axis=0, keepdims=True)
        pick = gsub == _first_index(gsc == m, gsub, float(N_GROUPS))
        gkeep = jnp.where(pick, 1.0, gkeep)
        gsc = jnp.where(pick, neg, gsc)

    eidx = [sub + float(g * per_g) for g in range(N_GROUPS)]
    masked = [jnp.where(jnp.broadcast_to(gkeep[g:g + 1, :], (per_g, t)) > 0.5, sel[g], neg)
              for g in range(N_GROUPS)]
    chosen = [jnp.zeros((per_g, t), F32) for _ in range(N_GROUPS)]
    wsel = [jnp.zeros((per_g, t), F32) for _ in range(N_GROUPS)]
    picked = []
    for _ in range(TOP_K):
        m = jnp.max(masked[0], axis=0, keepdims=True)
        for g in range(1, N_GROUPS):
            m = jnp.maximum(m, jnp.max(masked[g], axis=0, keepdims=True))
        e_first = _first_index(masked[0] == m, eidx[0], float(N_EXPERTS))
        for g in range(1, N_GROUPS):
            e_first = jnp.minimum(e_first, _first_index(masked[g] == m, eidx[g], float(N_EXPERTS)))
        picked.append(e_first)
        for g in range(N_GROUPS):
            pick = eidx[g] == e_first
            chosen[g] = jnp.where(pick, 1.0, chosen[g])
            wsel[g] = jnp.where(pick, scores[g], wsel[g])
            masked[g] = jnp.where(pick, neg, masked[g])

    denom = jnp.sum(wsel[0], axis=0, keepdims=True)
    for g in range(1, N_GROUPS):
        denom = denom + jnp.sum(wsel[g], axis=0, keepdims=True)

    c_all = jnp.concatenate(chosen, axis=0)
    rank = jnp.dot(c_all.astype(BF16), tri_ref[...], preferred_element_type=F32)
    counts = jnp.sum(c_all, axis=1, keepdims=True)
    padded = jnp.ceil(counts / SUBLANES) * SUBLANES
    below = (lax.broadcasted_iota(jnp.int32, (N_EXPERTS, N_EXPERTS), 1)
             < lax.broadcasted_iota(jnp.int32, (N_EXPERTS, N_EXPERTS), 0)).astype(F32)
    seg = jnp.dot(below, jnp.broadcast_to(padded, (N_EXPERTS, LANES)), preferred_element_type=F32,
                  precision=lax.Precision.HIGHEST)
    row_of = seg[:, :1] + rank

    dest_rows, w_rows = [], []
    for k in range(TOP_K):
        d_k = jnp.zeros((1, t), F32)
        w_k = jnp.zeros((1, t), F32)
        for g in range(N_GROUPS):
            hit = eidx[g] == picked[k]
            d_k = d_k + jnp.sum(jnp.where(hit, row_of[g * per_g:(g + 1) * per_g, :], 0.0), axis=0,
                                keepdims=True)
            w_k = w_k + jnp.sum(jnp.where(hit, wsel[g], 0.0), axis=0, keepdims=True)
        dest_rows.append(d_k)
        w_rows.append(w_k / denom * ROUTED_SCALE)
    fill = [jnp.zeros((SUBLANES - TOP_K, t), F32)]
    dest_ref[...] = jnp.concatenate(dest_rows + fill, axis=0).astype(jnp.int32)
    w_ref[...] = jnp.concatenate(w_rows + fill, axis=0)
    seg_ref[...] = seg.astype(jnp.int32)
    cnt_ref[...] = jnp.broadcast_to(counts, (N_EXPERTS, LANES)).astype(jnp.int32)


def _route(logits_t, router_bias, tile):
    ne, n = logits_t.shape
    nt = n // tile
    tri = jnp.asarray(np.triu(np.ones((tile, tile), np.float32), 1), dtype=BF16)
    per_tile = lambda rows, width: pl.BlockSpec((None, rows, width), lambda i: (i, 0, 0))
    dest, wts, seg, cnt = pl.pallas_call(
        _route_kernel,
        grid=(nt,),
        in_specs=[pl.BlockSpec((ne, tile), lambda i: (0, i)),
                  pl.BlockSpec((ne, 1), lambda i: (0, 0)),
                  pl.BlockSpec((tile, tile), lambda i: (0, 0))],
        out_specs=[per_tile(SUBLANES, tile), per_tile(SUBLANES, tile),
                   per_tile(ne, LANES), per_tile(ne, LANES)],
        out_shape=[jax.ShapeDtypeStruct((nt, SUBLANES, tile), jnp.int32),
                   jax.ShapeDtypeStruct((nt, SUBLANES, tile), F32),
                   jax.ShapeDtypeStruct((nt, ne, LANES), jnp.int32),
                   jax.ShapeDtypeStruct((nt, ne, LANES), jnp.int32)],
        compiler_params=_cparams("parallel"),
    )(logits_t, router_bias.astype(F32).reshape(ne, 1), tri)
    return seg[:, :, 0].reshape(-1), cnt[:, :, 0].reshape(-1), dest, wts


def _swiglu(xb, wgu_ref, wdn_ref):
    gu = jnp.dot(xb, wgu_ref[...], preferred_element_type=F32)
    half = wgu_ref.shape[1] // 2
    act = _silu(gu[:, :half]) * gu[:, half:]
    return jnp.dot(act.astype(BF16), wdn_ref[...], preferred_element_type=F32)


def _slabs_to_rows(p, il_ref, r):
    n = il_ref.shape[0] // r
    il_ref[...] = pltpu.bitcast(p, BF16).astype(F32)
    return jnp.concatenate([il_ref[pl.ds(2 * c + h, r, stride=n), :]
                            for h in range(2) for c in range(n // 2)], axis=1)


def _rows_to_slabs(v, il_ref):
    r = v.shape[0]
    n = il_ref.shape[0] // r
    for h in range(2):
        for c in range(n // 2):
            lane0 = (h * (n // 2) + c) * LANES
            il_ref[pl.ds(2 * c + h, r, stride=n), :] = v[:, lane0:lane0 + LANES]
    return pltpu.bitcast(il_ref[...].astype(BF16), U32)


def _dest_col(half, k, u):
    return (half * SUBLANES + k) * SUBLANES + u


def _moe_kernel(start_ref, count_ref, hp_ref, hb_ref, dest_ref, wt_ref, wgu_hbm, wdn_hbm,
                sgu_ref, sdn_ref, xm_ref, g2_ref, o_ref, xy_ref, il_ref, yp_ref, stage_ref, zf_ref,
                comb_ref, wgu_buf, wdn_buf, wsem, *, layer):
    t = pl.program_id(0)
    s = pl.program_id(1)
    slab = xm_ref.shape[1] // (2 * LANES)
    tile = hp_ref.shape[0] // (2 * slab)
    group_rows = SUBLANES * slab

    def weight_copies(e, slot):
        return (pltpu.make_async_copy(wgu_hbm.at[layer, e], wgu_buf.at[slot], wsem.at[0, slot]),
                pltpu.make_async_copy(wdn_hbm.at[layer, e], wdn_buf.at[slot], wsem.at[1, slot]))

    def slab_at(ref, first):
        return ref.at[pl.ds(pl.multiple_of(first, slab), slab), :]

    @pl.when(s == 0)
    def _first_weights():
        for cp in weight_copies(0, 0):
            cp.start()

    @pl.when(jnp.logical_and(t == 0, s == 0))
    def _clear():
        xy_ref[...] = jnp.zeros(xy_ref.shape, U32)

    @pl.when(s == 0)
    def _dispatch():
        def group(j, carry):
            for half in range(2):
                p0 = pl.multiple_of((2 * j + half) * 2 * group_rows, 2 * group_rows)
                rows = pltpu.bitcast(hp_ref[pl.ds(p0, 2 * group_rows), :], U32)
                for u in range(SUBLANES):
                    for k in range(TOP_K):
                        slab_at(xy_ref, dest_ref[j, _dest_col(half, k, u)])[...] = (
                            rows[u * slab:(u + 1) * slab, :])
            return carry

        lax.fori_loop(0, tile // (2 * SUBLANES), group, 0)

    @pl.when(s == 0)
    def _experts():
        def expert(e, carry0):
            slot = lax.rem(e, 2)

            @pl.when(e + 1 < N_EXPERTS)
            def _prefetch():
                for cp in weight_copies(e + 1, 1 - slot):
                    cp.start()

            for cp in weight_copies(e, slot):
                cp.wait()
            start = start_ref[t * N_EXPERTS + e]
            count = count_ref[t * N_EXPERTS + e]

            def chunk(c, carry):
                base = pl.multiple_of((start + c * EROWS) * slab, group_rows)
                groups = (jnp.minimum(count - c * EROWS, EROWS) + SUBLANES - 1) // SUBLANES
                xb = _slabs_to_rows(xy_ref[pl.ds(base, EROWS * slab), :], il_ref, EROWS).astype(BF16)
                y = _swiglu(xb, wgu_buf.at[slot], wdn_buf.at[slot])
                yp_ref[...] = _rows_to_slabs(y, il_ref)

                def put(j, c2):
                    r0 = pl.multiple_of(j * group_rows, group_rows)
                    xy_ref[pl.ds(base + r0, group_rows), :] = yp_ref[pl.ds(r0, group_rows), :]
                    return c2

                lax.fori_loop(0, groups, put, 0)
                return carry

            lax.fori_loop(0, (count + EROWS - 1) // EROWS, chunk, 0)
            return carry0

        lax.fori_loop(0, N_EXPERTS, expert, 0)

    @pl.when(s >= 1)
    def _combine():
        sb = s - 1

        def group(j, carry):
            g2 = sb * (CROWS // (2 * SUBLANES)) + j
            for half in range(2):
                r0 = pl.multiple_of(j * 2 * SUBLANES + half * SUBLANES, SUBLANES)
                t0 = pl.multiple_of(sb * CROWS + r0, SUBLANES)
                for k in range(TOP_K):
                    for u in range(SUBLANES):
                        first = (k * SUBLANES + u) * slab
                        stage_ref[first:first + slab, :] = (
                            slab_at(xy_ref, dest_ref[g2, _dest_col(half, k, u)])[...])
                rows = _slabs_to_rows(stage_ref[...], zf_ref, TOP_K * SUBLANES)
                wv = wt_ref[pl.ds(t0, SUBLANES), :]
                acc = wv[:, 0:1] * rows[0:SUBLANES, :]
                for k in range(1, TOP_K):
                    acc = acc + wv[:, k:k + 1] * rows[k * SUBLANES:(k + 1) * SUBLANES, :]
                comb_ref[pl.ds(r0, SUBLANES), :] = acc
            return carry

        lax.fori_loop(0, CROWS // (2 * SUBLANES), group, 0)
        shared = _swiglu(hb_ref[...], sgu_ref, sdn_ref)
        o_ref[...] = xm_ref[...] + g2_ref[...] * (comb_ref[...] + shared)


def _moe(h, xm, g2, logits_t, router_bias, w_gu, w_dn, s_gu, s_dn, layer, seq):
    n, d = h.shape
    slab = d // (2 * LANES)
    tile = min(MOE_TILE, seq)
    nt = n // tile
    per_b = seq // tile
    csteps = tile // CROWS
    assert seq % tile == 0 and tile % CROWS == 0
    seg_start, seg_count, dest, wts = _route(logits_t, router_bias, tile)
    dest_rows = dest.reshape(nt, SUBLANES, tile // (2 * SUBLANES), 2, SUBLANES) * slab
    dest_rows = dest_rows.transpose(0, 2, 3, 1, 4).reshape(nt, tile // (2 * SUBLANES), -1)
    h_slabs = h.reshape(n, 2, slab, LANES).transpose(0, 2, 1, 3).reshape(n * 2 * slab, LANES)
    rows = _moe_rows(tile)
    cst = lambda s: jnp.maximum(s - 1, 0)
    crow = pl.BlockSpec((CROWS, d), lambda t, s, a, b: (t * csteps + cst(s), 0))
    hbm = pl.BlockSpec(memory_space=pl.ANY)
    return pl.pallas_call(
        functools.partial(_moe_kernel, layer=layer),
        grid_spec=pltpu.PrefetchScalarGridSpec(
            num_scalar_prefetch=2,
            grid=(nt, 1 + csteps),
            in_specs=[pl.BlockSpec((tile * 2 * slab, LANES), lambda t, s, a, b: (t, 0)),
                      crow,
                      pl.BlockSpec((None, tile // (2 * SUBLANES), 2 * SUBLANES * SUBLANES),
                                   lambda t, s, a, b: (t, 0, 0), memory_space=pltpu.SMEM),
                      pl.BlockSpec((None, tile, SUBLANES), lambda t, s, a, b: (t, 0, 0)),
                      hbm, hbm,
                      pl.BlockSpec((None,) + s_gu.shape[1:], lambda t, s, a, b: (layer, 0, 0)),
                      pl.BlockSpec((None,) + s_dn.shape[1:], lambda t, s, a, b: (layer, 0, 0)),
                      crow,
                      pl.BlockSpec((None, 1, d), lambda t, s, a, b: (t // per_b, 0, 0))],
            out_specs=crow,
            scratch_shapes=[pltpu.VMEM((rows * slab, LANES), U32),
                            pltpu.VMEM((EROWS * 2 * slab, LANES), F32),
                            pltpu.VMEM((EROWS * slab, LANES), U32),
                            pltpu.VMEM((TOP_K * SUBLANES * slab, LANES), U32),
                            pltpu.VMEM((TOP_K * SUBLANES * 2 * slab, LANES), F32),
                            pltpu.VMEM((CROWS, d), F32),
                            pltpu.VMEM((2, d, 2 * EXPERT_DIM), BF16),
                            pltpu.VMEM((2, EXPERT_DIM, d), BF16),
                            pltpu.SemaphoreType.DMA((2, 2))]),
        out_shape=jax.ShapeDtypeStruct((n, d), F32),
        compiler_params=_cparams("arbitrary", "arbitrary"),
    )(seg_start, seg_count, h_slabs, h, dest_rows, jnp.swapaxes(wts, 1, 2),
      w_gu, w_dn, s_gu, s_dn, xm, g2)


def _mla_prep_kernel(x_ref, n1_ref, sc1_ref, sh1_ref, nk_ref, sck_ref, shk_ref,
                     wdq_ref, gql_ref, wuq_ref, wuqs_ref, wdown_ref, gkl_ref, wk_ref, gk_ref, wuv_ref,
                     qa_ref, qb_ref, kc_ref, ksa_ref, ksb_ref, q_ref, k_ref, v_ref):
    xn = _rms(x_ref[...])
    hq = xn * n1_ref[...] * (1.0 + sc1_ref[...]) + sh1_ref[...]
    hk = xn * nk_ref[...] * (1.0 + sck_ref[...]) + shk_ref[...]
    n_heads = q_ref.shape[1] // LANES
    half = ROPE_DIM // 2

    cq = (_rms(jnp.dot(hq.astype(BF16), wdq_ref[...], preferred_element_type=F32)) * gql_ref[...]).astype(BF16)
    q_raw = jnp.dot(cq, wuq_ref[...], preferred_element_type=F32)
    q_swp = jnp.dot(cq, wuqs_ref[...], preferred_element_type=F32)
    qa, qb = qa_ref[...], qb_ref[...]
    for h in range(n_heads):
        sl = slice(h * LANES, (h + 1) * LANES)
        t = q_raw[:, sl]
        r = lax.rsqrt(jnp.sum(t * t, axis=-1, keepdims=True) / QK_DIM + NORM_EPS)
        q_ref[:, sl] = (r * (t * qa + q_swp[:, sl] * qb)).astype(q_ref.dtype)

    down = jnp.dot(hk.astype(BF16), wdown_ref[...], preferred_element_type=F32)
    ckv = (_rms(down[:, :KV_RANK]) * gkl_ref[...]).astype(BF16)
    pe = down[:, KV_RANK:]
    ss_pe = jnp.sum(pe * pe, axis=-1, keepdims=True)
    rot = (pe * kc_ref[...] + pltpu.roll(pe, LANES - half, 1) * ksa_ref[...]
           + pltpu.roll(pe, half, 1) * ksb_ref[...])
    rot_hi = rot.astype(BF16)
    rot_lo = (rot - rot_hi.astype(F32)).astype(BF16)
    k_raw = jnp.dot(jnp.concatenate([ckv, rot_hi, rot_lo], axis=1), wk_ref[...],
                    preferred_element_type=F32)
    nope = lax.broadcasted_iota(jnp.int32, (x_ref.shape[0], LANES), 1) < NOPE_DIM
    for h in range(n_heads):
        sl = slice(h * LANES, (h + 1) * LANES)
        t = k_raw[:, sl]
        ss = jnp.sum(jnp.where(nope, t * t, 0.0), axis=-1, keepdims=True) + ss_pe
        k_ref[:, sl] = (t * lax.rsqrt(ss / QK_DIM + NORM_EPS) * gk_ref[...]).astype(k_ref.dtype)
    v_ref[...] = jnp.dot(ckv, wuv_ref[...], preferred_element_type=F32).astype(v_ref.dtype)


def _pad_heads(w, n_heads, width):
    r = w.shape[0]
    return jnp.pad(w.reshape(r, n_heads, width), ((0, 0), (0, 0), (0, LANES - width))).reshape(r, n_heads * LANES)


def _swap_rope_halves(a):
    half = ROPE_DIM // 2
    lead = a.shape[:-1]
    return jnp.concatenate([jnp.zeros(lead + (NOPE_DIM,), a.dtype), a[..., NOPE_DIM + half:QK_DIM],
                            a[..., NOPE_DIM:NOPE_DIM + half],
                            jnp.zeros(lead + (LANES - QK_DIM,), a.dtype)], axis=-1)


def _mla_prep(xf, pos, n1, sc1, sh1, nk, sck, shk, w_dq, g_ql, w_uq, g_q, w_down, g_kl, w_uk, w_uv,
              g_k, seq):
    n, d = xf.shape
    n_heads = w_uv.shape[1] // V_DIM
    q_rank = w_dq.shape[1]
    hw = n_heads * LANES
    tm = 256
    per_b = seq // tm

    half = ROPE_DIM // 2
    inv = jnp.power(ROPE_THETA, -jnp.arange(half, dtype=F32) / half)
    ang = pos.astype(F32).reshape(n, 1) * inv
    cos, sin = jnp.cos(ang), jnp.sin(ang)
    ones = jnp.ones((n, NOPE_DIM), F32)
    zeros = lambda wdt: jnp.zeros((n, wdt), F32)
    tail = LANES - QK_DIM
    pad_gain = lambda g: jnp.pad(g.reshape(1, QK_DIM), ((0, 0), (0, tail)))
    q_mul = QK_DIM ** -0.5 * LOG2E
    qa_t = jnp.concatenate([ones, cos, cos, zeros(tail)], axis=1) * pad_gain(g_q) * q_mul
    qb_t = (jnp.concatenate([zeros(NOPE_DIM), -sin, sin, zeros(tail)], axis=1)
            * _swap_rope_halves(g_q.reshape(1, QK_DIM)) * q_mul)
    gk1, gk2 = g_k[NOPE_DIM:NOPE_DIM + half], g_k[NOPE_DIM + half:]
    rest = zeros(LANES - ROPE_DIM)
    kc_t = jnp.concatenate([cos * gk1, cos * gk2, rest], axis=1)
    ksa_t = jnp.concatenate([-sin * gk2, zeros(half), rest], axis=1)
    ksb_t = jnp.concatenate([zeros(half), sin * gk1, rest], axis=1)
    gk_mask = jnp.concatenate([g_k[:NOPE_DIM], jnp.ones((ROPE_DIM,), F32),
                               jnp.zeros((tail,), F32)]).reshape(1, LANES)

    wuq_p = _pad_heads(w_uq, n_heads, QK_DIM).astype(BF16)
    wuq_s = _swap_rope_halves(w_uq.reshape(q_rank, n_heads, QK_DIM)).reshape(q_rank, hw).astype(BF16)
    wdown_p = jnp.pad(w_down, ((0, 0), (0, LANES - ROPE_DIM))).astype(BF16)
    eye = jnp.pad(jnp.eye(ROPE_DIM, dtype=F32), ((0, LANES - ROPE_DIM), (NOPE_DIM, tail)))
    pe_rows = jnp.tile(eye, (1, n_heads))
    wk = jnp.concatenate([_pad_heads(w_uk, n_heads, NOPE_DIM), pe_rows, pe_rows], axis=0).astype(BF16)

    vec = pl.BlockSpec((None, 1, d), lambda i: (i // per_b, 0, 0))
    full = lambda a: pl.BlockSpec(a.shape, lambda i: (0,) * a.ndim)
    row = lambda wdt: pl.BlockSpec((tm, wdt), lambda i: (i, 0))
    args = [xf, n1, sc1, sh1, nk, sck, shk,
            w_dq.astype(BF16), g_ql.reshape(1, q_rank), wuq_p, wuq_s,
            wdown_p, g_kl.reshape(1, KV_RANK), wk, gk_mask, w_uv.astype(BF16),
            qa_t, qb_t, kc_t, ksa_t, ksb_t]
    specs = [row(d), full(n1), vec, vec, full(nk), vec, vec] + [full(a) for a in args[7:16]] + [row(LANES)] * 5
    return pl.pallas_call(
        _mla_prep_kernel,
        grid=(n // tm,),
        in_specs=specs,
        out_specs=[row(hw), row(hw), row(n_heads * V_DIM)],
        out_shape=[jax.ShapeDtypeStruct((n, hw), BF16), jax.ShapeDtypeStruct((n, hw), BF16),
                   jax.ShapeDtypeStruct((n, n_heads * V_DIM), BF16)],
        compiler_params=_cparams("parallel"),
    )(*args)


def _causal_bias(seq):
    a = np.arange(QBLK)[:, None]
    c = np.arange(seq)[None, :]
    return np.where(c <= (seq - QBLK) + a, 0.0, MASKED).astype(np.float32)


def _mla_attn_kernel(q_ref, k_ref, v_ref, mask_ref, o_ref, s_ref, p_ref):
    seq = q_ref.shape[0]
    lane_lo = lax.broadcasted_iota(jnp.int32, (QBLK, LANES), 1) < V_DIM

    def qk_of(blk, h, w):
        lanes = slice(h * LANES, (h + 1) * LANES)
        return q_ref[w - QBLK:w, lanes], k_ref[0:w, lanes]

    _attention_blocks(seq, qk_of, v_ref, mask_ref, s_ref, p_ref, o_ref, lane_lo)


def _mla_attention(q, k, v):
    bsz, seq, hw = q.shape
    n_pairs = hw // (2 * LANES)
    qk = pl.BlockSpec((None, seq, 2 * LANES), lambda b, h: (b, 0, h))
    vo = pl.BlockSpec((None, seq, LANES), lambda b, h: (b, 0, h))
    return pl.pallas_call(
        _mla_attn_kernel,
        grid=(bsz, n_pairs),
        in_specs=[qk, qk, vo, pl.BlockSpec((QBLK, seq), lambda b, h: (0, 0))],
        out_specs=vo,
        scratch_shapes=_attn_scratch(seq),
        out_shape=jax.ShapeDtypeStruct(v.shape, BF16),
        compiler_params=_cparams("parallel", "parallel"),
    )(q, k, v, jnp.asarray(_causal_bias(seq)))


def kernel(x, c, pos, ada_w, ada_b, norm1_g, norm2_g, a_w_qkv, a_q_norm, a_k_norm, a_w_o, kv_ada_w, kv_ada_b, kv_norm_g, kv_w_down, kv_latent_norm, kv_w_uk, kv_w_uv, kv_k_norm, b_w_dq, b_q_latent_norm, b_w_uq, b_q_norm, b_w_o, moe_w_router, moe_router_bias, moe_w_gate_up, moe_w_down, moe_shared_gate_up, moe_shared_down):
    bsz, seq, d = x.shape
    n = bsz * seq
    depth = ada_w.shape[0]
    n_a = a_w_qkv.shape[0]
    assert seq % QBLK == 0
    xf = x.reshape(n, d)

    w_gu = moe_w_gate_up.astype(BF16)
    w_dn = moe_w_down.astype(BF16)
    s_gu = moe_shared_gate_up.astype(BF16)
    s_dn = moe_shared_down.astype(BF16)
    row = lambda g: g.reshape(1, d)

    for layer in range(depth):
        mod = _modulation(c, ada_w, ada_b, layer).reshape(bsz, 6, 1, d)
        sh1, sc1, g1, sh2, sc2, g2 = [mod[:, i] for i in range(6)]
        if layer < n_a:
            qkv = _qkv_proj(xf, row(norm1_g[layer]), sc1, sh1, a_w_qkv[layer].astype(BF16), seq)
            o = _dilated_attention(qkv.reshape(bsz, seq, -1), pos, a_q_norm[layer], a_k_norm[layer])
            w_o = a_w_o[layer]
        else:
            j = layer - n_a
            kmod = _modulation(c, kv_ada_w[None], kv_ada_b[None], 0).reshape(bsz, 2, 1, d)
            q, k, v = _mla_prep(xf, pos, row(norm1_g[layer]), sc1, sh1, row(kv_norm_g), kmod[:, 1],
                                kmod[:, 0], b_w_dq[j], b_q_latent_norm[j], b_w_uq[j], b_q_norm[j],
                                kv_w_down, kv_latent_norm, kv_w_uk, kv_w_uv, kv_k_norm, seq)
            o = _mla_attention(q.reshape(bsz, seq, -1), k.reshape(bsz, seq, -1),
                               v.reshape(bsz, seq, -1))
            w_o = b_w_o[j]
        xm, h2, logits = _post_attn(o.reshape(n, d), w_o.astype(BF16), xf, g1, row(norm2_g[layer]),
                                    sc2, sh2, moe_w_router[layer], seq)
        xf = _moe(h2, xm, g2, logits, moe_router_bias[layer], w_gu, w_dn, s_gu, s_dn, layer, seq)
    return xf.reshape(bsz, seq, d)
```

```python
import functools
import math

import numpy as np
import jax
import jax.numpy as jnp
from jax import lax
from jax.experimental import pallas as pl
from jax.experimental.pallas import tpu as pltpu

F32 = jnp.float32
BF16 = jnp.bfloat16
U32 = jnp.uint32

NORM_EPS = 1e-6
MASKED = -1e30
LOG2E = math.log2(math.e)

LANES = 128
SUBLANES = 8
VMEM_LIMIT = 56 * 1024 * 1024

A_HEAD_DIM = 64
DILATED_PAIRS = ((128, 1), (512, 4), (2048, 16))
QBLK = 256
NOPE_DIM = 64
ROPE_DIM = 32
QK_DIM = NOPE_DIM + ROPE_DIM
V_DIM = 64
KV_RANK = 256
ROPE_THETA = 10000.0
N_EXPERTS = 64
N_GROUPS = 8
TOPK_GROUPS = 4
TOP_K = 6
EXPERT_DIM = 256
ROUTED_SCALE = 2.5
MOE_TILE = 2048
EROWS = 256
CROWS = 256

ROW_TILE = 512


def _cparams(*sem):
    return pltpu.CompilerParams(dimension_semantics=sem, vmem_limit_bytes=VMEM_LIMIT)


def _silu(v):
    return v * jax.nn.sigmoid(v)


def _rms(v):
    return v * lax.rsqrt(jnp.mean(v * v, axis=-1, keepdims=True) + NORM_EPS)


def _dot_nt(a, b):
    return lax.dot_general(a, b, (((1,), (1,)), ((), ())), preferred_element_type=F32)


def _mod_kernel(c_ref, w_ref, b_ref, o_ref):
    cond = _silu(c_ref[...])
    o_ref[...] = jnp.dot(cond, w_ref[...], preferred_element_type=F32,
                         precision=lax.Precision.HIGHEST) + b_ref[...]


def _modulation(c, w, b, layer):
    bsz, d = c.shape
    m = w.shape[-1]
    tn = 1024
    return pl.pallas_call(
        _mod_kernel,
        grid=(m // tn,),
        in_specs=[pl.BlockSpec((bsz, d), lambda j: (0, 0)),
                  pl.BlockSpec((None, d, tn), lambda j: (layer, 0, j)),
                  pl.BlockSpec((None, 1, tn), lambda j: (layer, 0, j))],
        out_specs=pl.BlockSpec((bsz, tn), lambda j: (0, j)),
        out_shape=jax.ShapeDtypeStruct((bsz, m), F32),
        compiler_params=_cparams("arbitrary"),
    )(c, w, b.reshape(b.shape[0], 1, m))


def _qkv_kernel(x_ref, g_ref, sc_ref, sh_ref, w_ref, o_ref):
    h = _rms(x_ref[...]) * g_ref[...]
    h = h * (1.0 + sc_ref[...]) + sh_ref[...]
    o_ref[...] = jnp.dot(h.astype(BF16), w_ref[...], preferred_element_type=F32).astype(o_ref.dtype)


def _qkv_proj(xf, g, sc, sh, w, seq):
    n, d = xf.shape
    m = w.shape[1]
    tm = min(ROW_TILE, seq)
    per_b = seq // tm
    vec = pl.BlockSpec((None, 1, d), lambda i: (i // per_b, 0, 0))
    return pl.pallas_call(
        _qkv_kernel,
        grid=(n // tm,),
        in_specs=[pl.BlockSpec((tm, d), lambda i: (i, 0)),
                  pl.BlockSpec((1, d), lambda i: (0, 0)),
                  vec, vec,
                  pl.BlockSpec((d, m), lambda i: (0, 0))],
        out_specs=pl.BlockSpec((tm, m), lambda i: (i, 0)),
        out_shape=jax.ShapeDtypeStruct((n, m), BF16),
        compiler_params=_cparams("parallel"),
    )(xf, g, sc, sh, w)


ATT_SLOTS = 4


def _attention_blocks(seq, qk_of, v_ref, bias_ref, s_ref, p_ref, o_ref, lane_lo):
    nblk = seq // QBLK

    def slot(blk, h):
        return (blk % 2) * 2 + h

    def scores(blk):
        w = (blk + 1) * QBLK
        bias = bias_ref[:, seq - w:seq]
        for h in range(2):
            q, k = qk_of(blk, h, w)
            s_ref[slot(blk, h), :, 0:w] = _dot_nt(q, k) + bias

    for blk in range(nblk):
        w = (blk + 1) * QBLK
        scores(blk)
        denom = []
        for h in range(2):
            s = s_ref[slot(blk, h), :, 0:w]
            p = jnp.exp2(s - jnp.max(s, axis=-1, keepdims=True))
            denom.append(jnp.sum(p, axis=-1, keepdims=True))
            p_ref[slot(blk, h), :, 0:w] = p.astype(BF16)
        o_a, o_b = [jnp.dot(p_ref[slot(blk, h), :, 0:w], v_ref[0:w, :],
                            preferred_element_type=F32) / denom[h] for h in range(2)]
        o_ref[blk * QBLK:w, :] = jnp.where(lane_lo, o_a, o_b).astype(o_ref.dtype)


def _attn_scratch(seq):
    return [pltpu.VMEM((ATT_SLOTS, QBLK, seq), F32), pltpu.VMEM((ATT_SLOTS, QBLK, seq), BF16)]


def _split3(v):
    p1 = v.astype(BF16).astype(F32)
    p2 = (v - p1).astype(BF16).astype(F32)
    return p1, p2, v - p1 - p2


def _dilated_log_multiplicity(seq):
    a = np.arange(QBLK)[:, None]
    c = np.arange(seq)[None, :]
    d = (seq - QBLK) + a - c
    mult = np.zeros((QBLK, seq), np.int32)
    for win, dil in DILATED_PAIRS:
        mult += ((d >= 0) & (d % dil == 0) & (d <= win)).astype(np.int32)
    return np.where(mult > 0, np.log2(np.maximum(mult, 1)), MASKED).astype(np.float32)


N_FEAT = 9


def _dil_attn_kernel(q_ref, k_ref, v_ref, posc_ref, sfeat_ref, gq_ref, gk_ref, logm_ref,
                     o_ref, qa_ref, qb_ref, ks_ref, s_ref, p_ref):
    seq = q_ref.shape[0]
    lane = lax.broadcasted_iota(jnp.int32, (QBLK, LANES), 1)
    lane_lo = lane < A_HEAD_DIM
    piece = [functools.reduce(jnp.logical_or, [lane == i + 3 * j for j in range(N_FEAT // 3)])
             for i in range(3)]

    def head_norm(ref, r0, g_ref, mul):
        t = ref[r0:r0 + QBLK, :].astype(F32)
        sq = t * t
        ssa = jnp.sum(jnp.where(lane_lo, sq, 0.0), axis=-1, keepdims=True)
        ssb = jnp.sum(jnp.where(lane_lo, 0.0, sq), axis=-1, keepdims=True)
        r = jnp.where(lane_lo, lax.rsqrt(ssa / A_HEAD_DIM + NORM_EPS),
                      lax.rsqrt(ssb / A_HEAD_DIM + NORM_EPS))
        return t * r * (g_ref[...] * mul)

    rel0 = posc_ref[0:1, :]
    for r0 in range(0, seq, QBLK):
        rows = slice(r0, r0 + QBLK)
        qn = head_norm(q_ref, r0, gq_ref, A_HEAD_DIM ** -0.5 * LOG2E)
        qa_ref[rows, :LANES] = jnp.where(lane_lo, qn, 0.0).astype(BF16)
        qb_ref[rows, :LANES] = jnp.where(lane_lo, 0.0, qn).astype(BF16)
        qa_ref[rows, LANES:] = jnp.broadcast_to(sfeat_ref[0:1, :], (QBLK, LANES)).astype(BF16)
        qb_ref[rows, LANES:] = jnp.broadcast_to(sfeat_ref[1:2, :], (QBLK, LANES)).astype(BF16)
        ks_ref[rows, :LANES] = head_norm(k_ref, r0, gk_ref, 1.0).astype(BF16)
        p1, p2, p3 = _split3((posc_ref[rows, :] - rel0).astype(F32))
        feat = jnp.where(piece[0], p1, jnp.where(piece[1], p2, jnp.where(piece[2], p3, 0.0)))
        ks_ref[rows, LANES:] = feat.astype(BF16)

    def qk_of(blk, h, w):
        return (qa_ref, qb_ref)[h][w - QBLK:w, :], ks_ref[0:w, :]

    _attention_blocks(seq, qk_of, v_ref, logm_ref, s_ref, p_ref, o_ref, lane_lo)


def _dilated_attention(qkv, pos, q_norm, k_norm):
    bsz, seq, three_d = qkv.shape
    d = three_d // 3
    n_pairs = d // LANES
    n_heads = d // A_HEAD_DIM
    slopes = jnp.exp2(-8.0 * jnp.arange(1, n_heads + 1, dtype=F32) / n_heads) * LOG2E
    sfeat = jnp.repeat(jnp.stack(_split3(slopes), axis=1), 3, axis=1)
    sfeat = jnp.pad(sfeat, ((0, 0), (0, LANES - N_FEAT))).reshape(n_pairs, 2, LANES)
    logm = jnp.asarray(_dilated_log_multiplicity(seq))
    gq = jnp.tile(q_norm.reshape(1, A_HEAD_DIM), (1, 2))
    gk = jnp.tile(k_norm.reshape(1, A_HEAD_DIM), (1, 2))
    blk = lambda off: pl.BlockSpec((None, seq, LANES), lambda b, h: (b, 0, off + h))
    const2 = pl.BlockSpec((1, LANES), lambda b, h: (0, 0))
    ext = pltpu.VMEM((seq, 2 * LANES), BF16)
    return pl.pallas_call(
        _dil_attn_kernel,
        grid=(bsz, n_pairs),
        in_specs=[blk(0), blk(n_pairs), blk(2 * n_pairs),
                  pl.BlockSpec((None, seq, 1), lambda b, h: (b, 0, 0)),
                  pl.BlockSpec((None, 2, LANES), lambda b, h: (h, 0, 0)),
                  const2, const2,
                  pl.BlockSpec((QBLK, seq), lambda b, h: (0, 0))],
        out_specs=pl.BlockSpec((None, seq, LANES), lambda b, h: (b, 0, h)),
        scratch_shapes=[ext, ext, ext] + _attn_scratch(seq),
        out_shape=jax.ShapeDtypeStruct((bsz, seq, d), BF16),
        compiler_params=_cparams("parallel", "parallel"),
    )(qkv, qkv, qkv, pos.reshape(bsz, seq, 1), sfeat, gq, gk, logm)


def _post_attn_kernel(o_ref, wo_ref, x_ref, g1_ref, n2_ref, sc_ref, sh_ref, wr_ref,
                      xm_ref, h_ref, hs_ref, lg_ref, il_ref):
    y = jnp.dot(o_ref[...], wo_ref[...], preferred_element_type=F32)
    xm = x_ref[...] + g1_ref[...] * y
    xm_ref[...] = xm
    h = _rms(xm) * n2_ref[...]
    h = h * (1.0 + sc_ref[...]) + sh_ref[...]
    h_ref[...] = h.astype(h_ref.dtype)
    hs_ref[...] = _rows_to_slab_rows(h, il_ref)
    lg_ref[...] = lax.dot_general(wr_ref[...], h, (((1,), (1,)), ((), ())),
                                  preferred_element_type=F32, precision=lax.Precision.HIGHEST)


def _post_attn(o, w_o, xf, g1, n2, sc2, sh2, w_router, seq):
    n, d = xf.shape
    ne = w_router.shape[1]
    tm = min(ROW_TILE, seq)
    per_b = seq // tm
    slab_rows = d // LANES
    vec = pl.BlockSpec((None, 1, d), lambda i: (i // per_b, 0, 0))
    row = lambda w: pl.BlockSpec((tm, w), lambda i: (i, 0))
    return pl.pallas_call(
        _post_attn_kernel,
        grid=(n // tm,),
        in_specs=[row(d), pl.BlockSpec((d, d), lambda i: (0, 0)), row(d), vec,
                  pl.BlockSpec((1, d), lambda i: (0, 0)), vec, vec,
                  pl.BlockSpec((ne, d), lambda i: (0, 0))],
        out_specs=[row(d), row(d), pl.BlockSpec((tm * slab_rows, LANES), lambda i: (i, 0)),
                   pl.BlockSpec((ne, tm), lambda i: (0, i))],
        out_shape=[jax.ShapeDtypeStruct((n, d), F32), jax.ShapeDtypeStruct((n, d), BF16),
                   jax.ShapeDtypeStruct((n * slab_rows, LANES), BF16),
                   jax.ShapeDtypeStruct((ne, n), F32)],
        scratch_shapes=[pltpu.VMEM((tm * slab_rows, LANES), F32)],
        compiler_params=_cparams("parallel"),
    )(o, w_o, xf, g1, n2, sc2, sh2, w_router.T)


def _moe_rows(tile):
    rows = tile * TOP_K + N_EXPERTS * (SUBLANES - 1) + EROWS
    return -(-rows // SUBLANES) * SUBLANES


def _first_index(hit, index, limit):
    return jnp.min(jnp.where(hit, index, limit), axis=0, keepdims=True)


def _route_kernel(lg_ref, bias_ref, tri_ref, dest_ref, w_ref, seg_ref, cnt_ref):
    t = lg_ref.shape[1]
    per_g = N_EXPERTS // N_GROUPS
    sub = lax.broadcasted_iota(jnp.int32, (per_g, t), 0).astype(F32)
    neg = -jnp.inf
    scores, sel = [], []
    for g in range(N_GROUPS):
        rows = slice(g * per_g, (g + 1) * per_g)
        sc = jax.nn.sigmoid(lg_ref[rows, :])
        scores.append(sc)
        sel.append(sc + bias_ref[rows, :])

    gs = []
    for g in range(N_GROUPS):
        m1 = jnp.max(sel[g], axis=0, keepdims=True)
        first = _first_index(sel[g] == m1, sub, float(per_g))
        m2 = jnp.max(jnp.where(sub == first, neg, sel[g]), axis=0, keepdims=True)
        gs.append(m1 + m2)
    gsc = jnp.concatenate(gs, axis=0)
    gsub = lax.broadcasted_iota(jnp.int32, (N_GROUPS, t), 0).astype(F32)
    gkeep = jnp.zeros((N_GROUPS, t), F32)
    for _ in range(TOPK_GROUPS):
        m = jnp.max(gsc, axis=0, keepdims=True)
        pick = gsub == _first_index(gsc == m, gsub, float(N_GROUPS))
        gkeep = jnp.where(pick, 1.0, gkeep)
        gsc = jnp.where(pick, neg, gsc)

    eidx = [sub + float(g * per_g) for g in range(N_GROUPS)]
    masked = [jnp.where(jnp.broadcast_to(gkeep[g:g + 1, :], (per_g, t)) > 0.5, sel[g], neg)
              for g in range(N_GROUPS)]
    chosen = [jnp.zeros((per_g, t), F32) for _ in range(N_GROUPS)]
    wsel = [jnp.zeros((per_g, t), F32) for _ in range(N_GROUPS)]
    picked = []
    for _ in range(TOP_K):
        m = jnp.max(masked[0], axis=0, keepdims=True)
        for g in range(1, N_GROUPS):
            m = jnp.maximum(m, jnp.max(masked[g], axis=0, keepdims=True))
        e_first = _first_index(masked[0] == m, eidx[0], float(N_EXPERTS))
        for g in range(1, N_GROUPS):
            e_first = jnp.minimum(e_first, _first_index(masked[g] == m, eidx[g], float(N_EXPERTS)))
        picked.append(e_first)
        for g in range(N_GROUPS):
            pick = eidx[g] == e_first
            chosen[g] = jnp.where(pick, 1.0, chosen[g])
            wsel[g] = jnp.where(pick, scores[g], wsel[g])
            masked[g] = jnp.where(pick, neg, masked[g])

    denom = jnp.sum(wsel[0], axis=0, keepdims=True)
    for g in range(1, N_GROUPS):
        denom = denom + jnp.sum(wsel[g], axis=0, keepdims=True)

    c_all = jnp.concatenate(chosen, axis=0)
    rank = jnp.dot(c_all.astype(BF16), tri_ref[...], preferred_element_type=F32)
    counts = jnp.sum(c_all, axis=1, keepdims=True)
    padded = jnp.ceil(counts / SUBLANES) * SUBLANES
    below = (lax.broadcasted_iota(jnp.int32, (N_EXPERTS, N_EXPERTS), 1)
             < lax.broadcasted_iota(jnp.int32, (N_EXPERTS, N_EXPERTS), 0)).astype(F32)
    seg = jnp.dot(below, jnp.broadcast_to(padded, (N_EXPERTS, LANES)), preferred_element_type=F32,
                  precision=lax.Precision.HIGHEST)
    row_of = seg[:, :1] + rank

    dest_rows, w_rows = [], []
    for k in range(TOP_K):
        d_k = jnp.zeros((1, t), F32)
        w_k = jnp.zeros((1, t), F32)
        for g in range(N_GROUPS):
            hit = eidx[g] == picked[k]
            d_k = d_k + jnp.sum(jnp.where(hit, row_of[g * per_g:(g + 1) * per_g, :], 0.0), axis=0,
                                keepdims=True)
            w_k = w_k + jnp.sum(jnp.where(hit, wsel[g], 0.0), axis=0, keepdims=True)
        dest_rows.append(d_k)
        w_rows.append(w_k / denom * ROUTED_SCALE)
    fill = [jnp.zeros((SUBLANES - TOP_K, t), F32)]
    dest_ref[...] = jnp.concatenate(dest_rows + fill, axis=0).astype(jnp.int32)
    w_ref[...] = jnp.concatenate(w_rows + fill, axis=0)
    seg_ref[...] = seg.astype(jnp.int32)
    cnt_ref[...] = jnp.broadcast_to(counts, (N_EXPERTS, LANES)).astype(jnp.int32)


def _route(logits_t, router_bias, tile):
    ne, n = logits_t.shape
    nt = n // tile
    tri = jnp.asarray(np.triu(np.ones((tile, tile), np.float32), 1), dtype=BF16)
    per_tile = lambda rows, width: pl.BlockSpec((None, rows, width), lambda i: (i, 0, 0))
    dest, wts, seg, cnt = pl.pallas_call(
        _route_kernel,
        grid=(nt,),
        in_specs=[pl.BlockSpec((ne, tile), lambda i: (0, i)),
                  pl.BlockSpec((ne, 1), lambda i: (0, 0)),
                  pl.BlockSpec((tile, tile), lambda i: (0, 0))],
        out_specs=[per_tile(SUBLANES, tile), per_tile(SUBLANES, tile),
                   per_tile(ne, LANES), per_tile(ne, LANES)],
        out_shape=[jax.ShapeDtypeStruct((nt, SUBLANES, tile), jnp.int32),
                   jax.ShapeDtypeStruct((nt, SUBLANES, tile), F32),
                   jax.ShapeDtypeStruct((nt, ne, LANES), jnp.int32),
                   jax.ShapeDtypeStruct((nt, ne, LANES), jnp.int32)],
        compiler_params=_cparams("parallel"),
    )(logits_t, router_bias.astype(F32).reshape(ne, 1), tri)
    return seg[:, :, 0].reshape(-1), cnt[:, :, 0].reshape(-1), dest, wts


def _swiglu(xb, wgu_ref, wdn_ref):
    gu = jnp.dot(xb, wgu_ref[...], preferred_element_type=F32)
    half = wgu_ref.shape[1] // 2
    act = _silu(gu[:, :half]) * gu[:, half:]
    return jnp.dot(act.astype(BF16), wdn_ref[...], preferred_element_type=F32)


def _slabs_to_rows(p, il_ref, r):
    n = il_ref.shape[0] // r
    il_ref[...] = pltpu.bitcast(p, BF16).astype(F32)
    return jnp.concatenate([il_ref[pl.ds(2 * c + h, r, stride=n), :]
                            for h in range(2) for c in range(n // 2)], axis=1)


def _rows_to_slab_rows(v, il_ref):
    r = v.shape[0]
    n = il_ref.shape[0] // r
    for h in range(2):
        for c in range(n // 2):
            lane0 = (h * (n // 2) + c) * LANES
            il_ref[pl.ds(2 * c + h, r, stride=n), :] = v[:, lane0:lane0 + LANES]
    return il_ref[...].astype(BF16)


def _rows_to_slabs(v, il_ref):
    return pltpu.bitcast(_rows_to_slab_rows(v, il_ref), U32)


def _dest_col(half, k, u):
    return (half * SUBLANES + k) * SUBLANES + u


def _moe_kernel(start_ref, count_ref, hp_ref, hb_ref, dest_ref, wt_ref, wgu_hbm, wdn_hbm,
                sgu_ref, sdn_ref, xm_ref, g2_ref, o_ref, xy_ref, il_ref, yp_ref, stage_ref, zf_ref,
                comb_ref, wgu_buf, wdn_buf, wsem, *, layer):
    t = pl.program_id(0)
    s = pl.program_id(1)
    slab = xm_ref.shape[1] // (2 * LANES)
    tile = hp_ref.shape[0] // (2 * slab)
    group_rows = SUBLANES * slab

    def weight_copies(e, slot):
        return (pltpu.make_async_copy(wgu_hbm.at[layer, e], wgu_buf.at[slot], wsem.at[0, slot]),
                pltpu.make_async_copy(wdn_hbm.at[layer, e], wdn_buf.at[slot], wsem.at[1, slot]))

    def slab_at(ref, first):
        return ref.at[pl.ds(pl.multiple_of(first, slab), slab), :]

    @pl.when(s == 0)
    def _first_weights():
        for p in range(2):
            for cp in weight_copies(p, p):
                cp.start()

    @pl.when(jnp.logical_and(t == 0, s == 0))
    def _clear():
        xy_ref[...] = jnp.zeros(xy_ref.shape, U32)

    @pl.when(s == 0)
    def _dispatch():
        def group(j, carry):
            for half in range(2):
                p0 = pl.multiple_of((2 * j + half) * 2 * group_rows, 2 * group_rows)
                rows = pltpu.bitcast(hp_ref[pl.ds(p0, 2 * group_rows), :], U32)
                for u in range(SUBLANES):
                    for k in range(TOP_K):
                        slab_at(xy_ref, dest_ref[j, _dest_col(half, k, u)])[...] = (
                            rows[u * slab:(u + 1) * slab, :])
            return carry

        lax.fori_loop(0, tile // (2 * SUBLANES), group, 0)

    @pl.when(s == 0)
    def _experts():
        def chunk_base(e, c):
            return pl.multiple_of((start_ref[t * N_EXPERTS + e] + c * EROWS) * slab, group_rows)

        def load_rows(base, p):
            return _slabs_to_rows(xy_ref[pl.ds(base, EROWS * slab), :], il_ref.at[p], EROWS).astype(BF16)

        def put_rows(p, base, valid):
            def put(j, c2):
                r0 = pl.multiple_of(j * group_rows, group_rows)
                xy_ref[pl.ds(base + r0, group_rows), :] = yp_ref[p, pl.ds(r0, group_rows), :]
                return c2

            lax.fori_loop(0, (jnp.clip(valid, 0, EROWS) + SUBLANES - 1) // SUBLANES, put, 0)

        def expert_pair(i, carry0):
            wslot = lax.rem(i, 2) * 2

            @pl.when(i + 1 < N_EXPERTS // 2)
            def _prefetch():
                for p in range(2):
                    for cp in weight_copies(2 * i + 2 + p, 2 - wslot + p):
                        cp.start()

            for p in range(2):
                for cp in weight_copies(2 * i + p, wslot + p):
                    cp.wait()
            counts = [count_ref[t * N_EXPERTS + 2 * i + p] for p in range(2)]
            bases = [chunk_base(2 * i + p, 0) for p in range(2)]
            xs = [load_rows(bases[p], p) for p in range(2)]
            ys = [_swiglu(xs[p], wgu_buf.at[wslot + p], wdn_buf.at[wslot + p]) for p in range(2)]
            for p in range(2):
                yp_ref[p] = _rows_to_slabs(ys[p], il_ref.at[p])
            for p in range(2):
                put_rows(p, bases[p], counts[p])

            for p in range(2):
                def more(c, carry, p=p):
                    base = chunk_base(2 * i + p, c)
                    y = _swiglu(load_rows(base, p), wgu_buf.at[wslot + p], wdn_buf.at[wslot + p])
                    yp_ref[p] = _rows_to_slabs(y, il_ref.at[p])
                    put_rows(p, base, counts[p] - c * EROWS)
                    return carry

                lax.fori_loop(1, (counts[p] + EROWS - 1) // EROWS, more, 0)
            return carry0

        lax.fori_loop(0, N_EXPERTS // 2, expert_pair, 0)

    @pl.when(s >= 1)
    def _combine():
        sb = s - 1

        def group(j, carry):
            g2 = sb * (CROWS // (2 * SUBLANES)) + j
            for half in range(2):
                r0 = pl.multiple_of(j * 2 * SUBLANES + half * SUBLANES, SUBLANES)
                t0 = pl.multiple_of(sb * CROWS + r0, SUBLANES)
                for k in range(TOP_K):
                    for u in range(SUBLANES):
                        first = (k * SUBLANES + u) * slab
                        stage_ref[first:first + slab, :] = (
                            slab_at(xy_ref, dest_ref[g2, _dest_col(half, k, u)])[...])
                rows = _slabs_to_rows(stage_ref[...], zf_ref, TOP_K * SUBLANES)
                wv = wt_ref[pl.ds(t0, SUBLANES), :]
                acc = wv[:, 0:1] * rows[0:SUBLANES, :]
                for k in range(1, TOP_K):
                    acc = acc + wv[:, k:k + 1] * rows[k * SUBLANES:(k + 1) * SUBLANES, :]
                comb_ref[pl.ds(r0, SUBLANES), :] = acc
            return carry

        lax.fori_loop(0, CROWS // (2 * SUBLANES), group, 0)
        shared = _swiglu(hb_ref[...], sgu_ref, sdn_ref)
        o_ref[...] = xm_ref[...] + g2_ref[...] * (comb_ref[...] + shared)


def _moe(h, h_slabs, xm, g2, logits_t, router_bias, w_gu, w_dn, s_gu, s_dn, layer, seq):
    n, d = h.shape
    slab = d // (2 * LANES)
    tile = min(MOE_TILE, seq)
    nt = n // tile
    per_b = seq // tile
    csteps = tile // CROWS
    assert seq % tile == 0 and tile % CROWS == 0
    seg_start, seg_count, dest, wts = _route(logits_t, router_bias, tile)
    dest_rows = dest.reshape(nt, SUBLANES, tile // (2 * SUBLANES), 2, SUBLANES) * slab
    dest_rows = dest_rows.transpose(0, 2, 3, 1, 4).reshape(nt, tile // (2 * SUBLANES), -1)
    rows = _moe_rows(tile)
    cst = lambda s: jnp.maximum(s - 1, 0)
    crow = pl.BlockSpec((CROWS, d), lambda t, s, a, b: (t * csteps + cst(s), 0))
    hbm = pl.BlockSpec(memory_space=pl.ANY)
    return pl.pallas_call(
        functools.partial(_moe_kernel, layer=layer),
        grid_spec=pltpu.PrefetchScalarGridSpec(
            num_scalar_prefetch=2,
            grid=(nt, 1 + csteps),
            in_specs=[pl.BlockSpec((tile * 2 * slab, LANES), lambda t, s, a, b: (t, 0)),
                      crow,
                      pl.BlockSpec((None, tile // (2 * SUBLANES), 2 * SUBLANES * SUBLANES),
                                   lambda t, s, a, b: (t, 0, 0), memory_space=pltpu.SMEM),
                      pl.BlockSpec((None, tile, SUBLANES), lambda t, s, a, b: (t, 0, 0)),
                      hbm, hbm,
                      pl.BlockSpec((None,) + s_gu.shape[1:], lambda t, s, a, b: (layer, 0, 0)),
                      pl.BlockSpec((None,) + s_dn.shape[1:], lambda t, s, a, b: (layer, 0, 0)),
                      crow,
                      pl.BlockSpec((None, 1, d), lambda t, s, a, b: (t // per_b, 0, 0))],
            out_specs=crow,
            scratch_shapes=[pltpu.VMEM((rows * slab, LANES), U32),
                            pltpu.VMEM((2, EROWS * 2 * slab, LANES), F32),
                            pltpu.VMEM((2, EROWS * slab, LANES), U32),
                            pltpu.VMEM((TOP_K * SUBLANES * slab, LANES), U32),
                            pltpu.VMEM((TOP_K * SUBLANES * 2 * slab, LANES), F32),
                            pltpu.VMEM((CROWS, d), F32),
                            pltpu.VMEM((4, d, 2 * EXPERT_DIM), BF16),
                            pltpu.VMEM((4, EXPERT_DIM, d), BF16),
                            pltpu.SemaphoreType.DMA((2, 4))]),
        out_shape=jax.ShapeDtypeStruct((n, d), F32),
        compiler_params=_cparams("arbitrary", "arbitrary"),
    )(seg_start, seg_count, h_slabs, h, dest_rows, jnp.swapaxes(wts, 1, 2),
      w_gu, w_dn, s_gu, s_dn, xm, g2)


def _mla_prep_kernel(x_ref, n1_ref, sc1_ref, sh1_ref, nk_ref, sck_ref, shk_ref,
                     wdq_ref, gql_ref, wuq_ref, wuqs_ref, wdown_ref, gkl_ref, wk_ref, gk_ref, wuv_ref,
                     qa_ref, qb_ref, kc_ref, ksa_ref, ksb_ref, q_ref, k_ref, v_ref):
    xn = _rms(x_ref[...])
    hq = xn * n1_ref[...] * (1.0 + sc1_ref[...]) + sh1_ref[...]
    hk = xn * nk_ref[...] * (1.0 + sck_ref[...]) + shk_ref[...]
    n_heads = q_ref.shape[1] // LANES
    half = ROPE_DIM // 2

    cq = (_rms(jnp.dot(hq.astype(BF16), wdq_ref[...], preferred_element_type=F32)) * gql_ref[...]).astype(BF16)
    q_raw = jnp.dot(cq, wuq_ref[...], preferred_element_type=F32)
    q_swp = jnp.dot(cq, wuqs_ref[...], preferred_element_type=F32)
    qa, qb = qa_ref[...], qb_ref[...]
    for h in range(n_heads):
        sl = slice(h * LANES, (h + 1) * LANES)
        t = q_raw[:, sl]
        r = lax.rsqrt(jnp.sum(t * t, axis=-1, keepdims=True) / QK_DIM + NORM_EPS)
        q_ref[:, sl] = (r * (t * qa + q_swp[:, sl] * qb)).astype(q_ref.dtype)

    down = jnp.dot(hk.astype(BF16), wdown_ref[...], preferred_element_type=F32)
    ckv = (_rms(down[:, :KV_RANK]) * gkl_ref[...]).astype(BF16)
    pe = down[:, KV_RANK:]
    ss_pe = jnp.sum(pe * pe, axis=-1, keepdims=True)
    rot = (pe * kc_ref[...] + pltpu.roll(pe, LANES - half, 1) * ksa_ref[...]
           + pltpu.roll(pe, half, 1) * ksb_ref[...])
    rot_hi = rot.astype(BF16)
    rot_lo = (rot - rot_hi.astype(F32)).astype(BF16)
    k_raw = jnp.dot(jnp.concatenate([ckv, rot_hi, rot_lo], axis=1), wk_ref[...],
                    preferred_element_type=F32)
    nope = lax.broadcasted_iota(jnp.int32, (x_ref.shape[0], LANES), 1) < NOPE_DIM
    for h in range(n_heads):
        sl = slice(h * LANES, (h + 1) * LANES)
        t = k_raw[:, sl]
        ss = jnp.sum(jnp.where(nope, t * t, 0.0), axis=-1, keepdims=True) + ss_pe
        k_ref[:, sl] = (t * lax.rsqrt(ss / QK_DIM + NORM_EPS) * gk_ref[...]).astype(k_ref.dtype)
    v_ref[...] = jnp.dot(ckv, wuv_ref[...], preferred_element_type=F32).astype(v_ref.dtype)


def _pad_heads(w, n_heads, width):
    r = w.shape[0]
    return jnp.pad(w.reshape(r, n_heads, width), ((0, 0), (0, 0), (0, LANES - width))).reshape(r, n_heads * LANES)


def _swap_rope_halves(a):
    half = ROPE_DIM // 2
    lead = a.shape[:-1]
    return jnp.concatenate([jnp.zeros(lead + (NOPE_DIM,), a.dtype), a[..., NOPE_DIM + half:QK_DIM],
                            a[..., NOPE_DIM:NOPE_DIM + half],
                            jnp.zeros(lead + (LANES - QK_DIM,), a.dtype)], axis=-1)


def _mla_prep(xf, pos, n1, sc1, sh1, nk, sck, shk, w_dq, g_ql, w_uq, g_q, w_down, g_kl, w_uk, w_uv,
              g_k, seq):
    n, d = xf.shape
    n_heads = w_uv.shape[1] // V_DIM
    q_rank = w_dq.shape[1]
    hw = n_heads * LANES
    tm = 256
    per_b = seq // tm

    half = ROPE_DIM // 2
    inv = jnp.power(ROPE_THETA, -jnp.arange(half, dtype=F32) / half)
    ang = pos.astype(F32).reshape(n, 1) * inv
    cos, sin = jnp.cos(ang), jnp.sin(ang)
    ones = jnp.ones((n, NOPE_DIM), F32)
    zeros = lambda wdt: jnp.zeros((n, wdt), F32)
    tail = LANES - QK_DIM
    pad_gain = lambda g: jnp.pad(g.reshape(1, QK_DIM), ((0, 0), (0, tail)))
    q_mul = QK_DIM ** -0.5 * LOG2E
    qa_t = jnp.concatenate([ones, cos, cos, zeros(tail)], axis=1) * pad_gain(g_q) * q_mul
    qb_t = (jnp.concatenate([zeros(NOPE_DIM), -sin, sin, zeros(tail)], axis=1)
            * _swap_rope_halves(g_q.reshape(1, QK_DIM)) * q_mul)
    gk1, gk2 = g_k[NOPE_DIM:NOPE_DIM + half], g_k[NOPE_DIM + half:]
    rest = zeros(LANES - ROPE_DIM)
    kc_t = jnp.concatenate([cos * gk1, cos * gk2, rest], axis=1)
    ksa_t = jnp.concatenate([-sin * gk2, zeros(half), rest], axis=1)
    ksb_t = jnp.concatenate([zeros(half), sin * gk1, rest], axis=1)
    gk_mask = jnp.concatenate([g_k[:NOPE_DIM], jnp.ones((ROPE_DIM,), F32),
                               jnp.zeros((tail,), F32)]).reshape(1, LANES)

    wuq_p = _pad_heads(w_uq, n_heads, QK_DIM).astype(BF16)
    wuq_s = _swap_rope_halves(w_uq.reshape(q_rank, n_heads, QK_DIM)).reshape(q_rank, hw).astype(BF16)
    wdown_p = jnp.pad(w_down, ((0, 0), (0, LANES - ROPE_DIM))).astype(BF16)
    eye = jnp.pad(jnp.eye(ROPE_DIM, dtype=F32), ((0, LANES - ROPE_DIM), (NOPE_DIM, tail)))
    pe_rows = jnp.tile(eye, (1, n_heads))
    wk = jnp.concatenate([_pad_heads(w_uk, n_heads, NOPE_DIM), pe_rows, pe_rows], axis=0).astype(BF16)

    vec = pl.BlockSpec((None, 1, d), lambda i: (i // per_b, 0, 0))
    full = lambda a: pl.BlockSpec(a.shape, lambda i: (0,) * a.ndim)
    row = lambda wdt: pl.BlockSpec((tm, wdt), lambda i: (i, 0))
    args = [xf, n1, sc1, sh1, nk, sck, shk,
            w_dq.astype(BF16), g_ql.reshape(1, q_rank), wuq_p, wuq_s,
            wdown_p, g_kl.reshape(1, KV_RANK), wk, gk_mask, w_uv.astype(BF16),
            qa_t, qb_t, kc_t, ksa_t, ksb_t]
    specs = [row(d), full(n1), vec, vec, full(nk), vec, vec] + [full(a) for a in args[7:16]] + [row(LANES)] * 5
    return pl.pallas_call(
        _mla_prep_kernel,
        grid=(n // tm,),
        in_specs=specs,
        out_specs=[row(hw), row(hw), row(n_heads * V_DIM)],
        out_shape=[jax.ShapeDtypeStruct((n, hw), BF16), jax.ShapeDtypeStruct((n, hw), BF16),
                   jax.ShapeDtypeStruct((n, n_heads * V_DIM), BF16)],
        compiler_params=_cparams("parallel"),
    )(*args)


def _causal_bias(seq):
    a = np.arange(QBLK)[:, None]
    c = np.arange(seq)[None, :]
    return np.where(c <= (seq - QBLK) + a, 0.0, MASKED).astype(np.float32)


def _mla_attn_kernel(q_ref, k_ref, v_ref, mask_ref, o_ref, s_ref, p_ref):
    seq = q_ref.shape[0]
    lane_lo = lax.broadcasted_iota(jnp.int32, (QBLK, LANES), 1) < V_DIM

    def qk_of(blk, h, w):
        lanes = slice(h * LANES, (h + 1) * LANES)
        return q_ref[w - QBLK:w, lanes], k_ref[0:w, lanes]

    _attention_blocks(seq, qk_of, v_ref, mask_ref, s_ref, p_ref, o_ref, lane_lo)


def _mla_attention(q, k, v):
    bsz, seq, hw = q.shape
    n_pairs = hw // (2 * LANES)
    qk = pl.BlockSpec((None, seq, 2 * LANES), lambda b, h: (b, 0, h))
    vo = pl.BlockSpec((None, seq, LANES), lambda b, h: (b, 0, h))
    return pl.pallas_call(
        _mla_attn_kernel,
        grid=(bsz, n_pairs),
        in_specs=[qk, qk, vo, pl.BlockSpec((QBLK, seq), lambda b, h: (0, 0))],
        out_specs=vo,
        scratch_shapes=_attn_scratch(seq),
        out_shape=jax.ShapeDtypeStruct(v.shape, BF16),
        compiler_params=_cparams("parallel", "parallel"),
    )(q, k, v, jnp.asarray(_causal_bias(seq)))


def kernel(x, c, pos, ada_w, ada_b, norm1_g, norm2_g, a_w_qkv, a_q_norm, a_k_norm, a_w_o, kv_ada_w, kv_ada_b, kv_norm_g, kv_w_down, kv_latent_norm, kv_w_uk, kv_w_uv, kv_k_norm, b_w_dq, b_q_latent_norm, b_w_uq, b_q_norm, b_w_o, moe_w_router, moe_router_bias, moe_w_gate_up, moe_w_down, moe_shared_gate_up, moe_shared_down):
    bsz, seq, d = x.shape
    n = bsz * seq
    depth = ada_w.shape[0]
    n_a = a_w_qkv.shape[0]
    assert seq % QBLK == 0
    xf = x.reshape(n, d)

    w_gu = moe_w_gate_up.astype(BF16)
    w_dn = moe_w_down.astype(BF16)
    s_gu = moe_shared_gate_up.astype(BF16)
    s_dn = moe_shared_down.astype(BF16)
    row = lambda g: g.reshape(1, d)

    for layer in range(depth):
        mod = _modulation(c, ada_w, ada_b, layer).reshape(bsz, 6, 1, d)
        sh1, sc1, g1, sh2, sc2, g2 = [mod[:, i] for i in range(6)]
        if layer < n_a:
            qkv = _qkv_proj(xf, row(norm1_g[layer]), sc1, sh1, a_w_qkv[layer].astype(BF16), seq)
            o = _dilated_attention(qkv.reshape(bsz, seq, -1), pos, a_q_norm[layer], a_k_norm[layer])
            w_o = a_w_o[layer]
        else:
            j = layer - n_a
            kmod = _modulation(c, kv_ada_w[None], kv_ada_b[None], 0).reshape(bsz, 2, 1, d)
            q, k, v = _mla_prep(xf, pos, row(norm1_g[layer]), sc1, sh1, row(kv_norm_g), kmod[:, 1],
                                kmod[:, 0], b_w_dq[j], b_q_latent_norm[j], b_w_uq[j], b_q_norm[j],
                                kv_w_down, kv_latent_norm, kv_w_uk, kv_w_uv, kv_k_norm, seq)
            o = _mla_attention(q.reshape(bsz, seq, -1), k.reshape(bsz, seq, -1),
                               v.reshape(bsz, seq, -1))
            w_o = b_w_o[j]
        xm, h2, h2_slabs, logits = _post_attn(o.reshape(n, d), w_o.astype(BF16), xf, g1,
                                              row(norm2_g[layer]), sc2, sh2, moe_w_router[layer], seq)
        xf = _moe(h2, h2_slabs, xm, g2, logits, moe_router_bias[layer], w_gu, w_dn, s_gu, s_dn,
                  layer, seq)
    return xf.reshape(bsz, seq, d)
```

```python
import functools
import math

import numpy as np
import jax
import jax.numpy as jnp
from jax import lax
from jax.experimental import pallas as pl
from jax.experimental.pallas import tpu as pltpu

F32 = jnp.float32
BF16 = jnp.bfloat16
U32 = jnp.uint32

NORM_EPS = 1e-6
MASKED = -1e30
LOG2E = math.log2(math.e)

LANES = 128
SUBLANES = 8
VMEM_LIMIT = 56 * 1024 * 1024

A_HEAD_DIM = 64
DILATED_PAIRS = ((128, 1), (512, 4), (2048, 16))
QBLK = 256
NOPE_DIM = 64
ROPE_DIM = 32
QK_DIM = NOPE_DIM + ROPE_DIM
V_DIM = 64
KV_RANK = 256
ROPE_THETA = 10000.0
N_EXPERTS = 64
N_GROUPS = 8
TOPK_GROUPS = 4
TOP_K = 6
EXPERT_DIM = 256
ROUTED_SCALE = 2.5
MOE_TILE = 2048
EROWS = 256
CROWS = 256

ROW_TILE = 512


def _cparams(*sem):
    return pltpu.CompilerParams(dimension_semantics=sem, vmem_limit_bytes=VMEM_LIMIT)


def _silu(v):
    return v * jax.nn.sigmoid(v)


def _rms(v):
    return v * lax.rsqrt(jnp.mean(v * v, axis=-1, keepdims=True) + NORM_EPS)


def _dot_nt(a, b):
    return lax.dot_general(a, b, (((1,), (1,)), ((), ())), preferred_element_type=F32)


def _mod_kernel(c_ref, w_ref, b_ref, o_ref):
    cond = _silu(c_ref[...])
    o_ref[...] = jnp.dot(cond, w_ref[...], preferred_element_type=F32,
                         precision=lax.Precision.HIGHEST) + b_ref[...]


def _modulation(c, w, b, layer):
    bsz, d = c.shape
    m = w.shape[-1]
    tn = 1024
    return pl.pallas_call(
        _mod_kernel,
        grid=(m // tn,),
        in_specs=[pl.BlockSpec((bsz, d), lambda j: (0, 0)),
                  pl.BlockSpec((None, d, tn), lambda j: (layer, 0, j)),
                  pl.BlockSpec((None, 1, tn), lambda j: (layer, 0, j))],
        out_specs=pl.BlockSpec((bsz, tn), lambda j: (0, j)),
        out_shape=jax.ShapeDtypeStruct((bsz, m), F32),
        compiler_params=_cparams("arbitrary"),
    )(c, w, b.reshape(b.shape[0], 1, m))


def _qkv_kernel(x_ref, g_ref, sc_ref, sh_ref, w_ref, o_ref):
    h = _rms(x_ref[...]) * g_ref[...]
    h = h * (1.0 + sc_ref[...]) + sh_ref[...]
    o_ref[...] = jnp.dot(h.astype(BF16), w_ref[...], preferred_element_type=F32).astype(o_ref.dtype)


def _qkv_proj(xf, g, sc, sh, w, seq):
    n, d = xf.shape
    m = w.shape[1]
    tm = min(ROW_TILE, seq)
    per_b = seq // tm
    vec = pl.BlockSpec((None, 1, d), lambda i: (i // per_b, 0, 0))
    return pl.pallas_call(
        _qkv_kernel,
        grid=(n // tm,),
        in_specs=[pl.BlockSpec((tm, d), lambda i: (i, 0)),
                  pl.BlockSpec((1, d), lambda i: (0, 0)),
                  vec, vec,
                  pl.BlockSpec((d, m), lambda i: (0, 0))],
        out_specs=pl.BlockSpec((tm, m), lambda i: (i, 0)),
        out_shape=jax.ShapeDtypeStruct((n, m), BF16),
        compiler_params=_cparams("parallel"),
    )(xf, g, sc, sh, w)


ATT_SLOTS = 4


def _attention_blocks(seq, qk_of, v_ref, bias_ref, s_ref, p_ref, o_ref, lane_lo):
    nblk = seq // QBLK

    def slot(blk, h):
        return (blk % 2) * 2 + h

    def scores(blk):
        w = (blk + 1) * QBLK
        bias = bias_ref[:, seq - w:seq]
        for h in range(2):
            q, k = qk_of(blk, h, w)
            s_ref[slot(blk, h), :, 0:w] = _dot_nt(q, k) + bias

    for blk in range(nblk):
        w = (blk + 1) * QBLK
        scores(blk)
        denom = []
        for h in range(2):
            s = s_ref[slot(blk, h), :, 0:w]
            p = jnp.exp2(s - jnp.max(s, axis=-1, keepdims=True))
            denom.append(jnp.sum(p, axis=-1, keepdims=True))
            p_ref[slot(blk, h), :, 0:w] = p.astype(BF16)
        o_a, o_b = [jnp.dot(p_ref[slot(blk, h), :, 0:w], v_ref[0:w, :],
                            preferred_element_type=F32) / denom[h] for h in range(2)]
        o_ref[blk * QBLK:w, :] = jnp.where(lane_lo, o_a, o_b).astype(o_ref.dtype)


def _attn_scratch(seq):
    return [pltpu.VMEM((ATT_SLOTS, QBLK, seq), F32), pltpu.VMEM((ATT_SLOTS, QBLK, seq), BF16)]


def _split3(v):
    p1 = v.astype(BF16).astype(F32)
    p2 = (v - p1).astype(BF16).astype(F32)
    return p1, p2, v - p1 - p2


def _dilated_log_multiplicity(seq):
    a = np.arange(QBLK)[:, None]
    c = np.arange(seq)[None, :]
    d = (seq - QBLK) + a - c
    mult = np.zeros((QBLK, seq), np.int32)
    for win, dil in DILATED_PAIRS:
        mult += ((d >= 0) & (d % dil == 0) & (d <= win)).astype(np.int32)
    return np.where(mult > 0, np.log2(np.maximum(mult, 1)), MASKED).astype(np.float32)


N_FEAT = 9


def _dil_attn_kernel(q_ref, k_ref, v_ref, posc_ref, sfeat_ref, gq_ref, gk_ref, logm_ref,
                     o_ref, qa_ref, qb_ref, ks_ref, s_ref, p_ref):
    seq = q_ref.shape[0]
    lane = lax.broadcasted_iota(jnp.int32, (QBLK, LANES), 1)
    lane_lo = lane < A_HEAD_DIM
    piece = [functools.reduce(jnp.logical_or, [lane == i + 3 * j for j in range(N_FEAT // 3)])
             for i in range(3)]

    def head_norm(ref, r0, g_ref, mul):
        t = ref[r0:r0 + QBLK, :].astype(F32)
        sq = t * t
        ssa = jnp.sum(jnp.where(lane_lo, sq, 0.0), axis=-1, keepdims=True)
        ssb = jnp.sum(jnp.where(lane_lo, 0.0, sq), axis=-1, keepdims=True)
        r = jnp.where(lane_lo, lax.rsqrt(ssa / A_HEAD_DIM + NORM_EPS),
                      lax.rsqrt(ssb / A_HEAD_DIM + NORM_EPS))
        return t * r * (g_ref[...] * mul)

    rel0 = posc_ref[0:1, :]
    for r0 in range(0, seq, QBLK):
        rows = slice(r0, r0 + QBLK)
        qn = head_norm(q_ref, r0, gq_ref, A_HEAD_DIM ** -0.5 * LOG2E)
        qa_ref[rows, :LANES] = jnp.where(lane_lo, qn, 0.0).astype(BF16)
        qb_ref[rows, :LANES] = jnp.where(lane_lo, 0.0, qn).astype(BF16)
        qa_ref[rows, LANES:] = jnp.broadcast_to(sfeat_ref[0:1, :], (QBLK, LANES)).astype(BF16)
        qb_ref[rows, LANES:] = jnp.broadcast_to(sfeat_ref[1:2, :], (QBLK, LANES)).astype(BF16)
        ks_ref[rows, :LANES] = head_norm(k_ref, r0, gk_ref, 1.0).astype(BF16)
        p1, p2, p3 = _split3((posc_ref[rows, :] - rel0).astype(F32))
        feat = jnp.where(piece[0], p1, jnp.where(piece[1], p2, jnp.where(piece[2], p3, 0.0)))
        ks_ref[rows, LANES:] = feat.astype(BF16)

    def qk_of(blk, h, w):
        return (qa_ref, qb_ref)[h][w - QBLK:w, :], ks_ref[0:w, :]

    _attention_blocks(seq, qk_of, v_ref, logm_ref, s_ref, p_ref, o_ref, lane_lo)


def _dilated_attention(qkv, pos, q_norm, k_norm):
    bsz, seq, three_d = qkv.shape
    d = three_d // 3
    n_pairs = d // LANES
    n_heads = d // A_HEAD_DIM
    slopes = jnp.exp2(-8.0 * jnp.arange(1, n_heads + 1, dtype=F32) / n_heads) * LOG2E
    sfeat = jnp.repeat(jnp.stack(_split3(slopes), axis=1), 3, axis=1)
    sfeat = jnp.pad(sfeat, ((0, 0), (0, LANES - N_FEAT))).reshape(n_pairs, 2, LANES)
    logm = jnp.asarray(_dilated_log_multiplicity(seq))
    gq = jnp.tile(q_norm.reshape(1, A_HEAD_DIM), (1, 2))
    gk = jnp.tile(k_norm.reshape(1, A_HEAD_DIM), (1, 2))
    blk = lambda off: pl.BlockSpec((None, seq, LANES), lambda b, h: (b, 0, off + h))
    const2 = pl.BlockSpec((1, LANES), lambda b, h: (0, 0))
    ext = pltpu.VMEM((seq, 2 * LANES), BF16)
    return pl.pallas_call(
        _dil_attn_kernel,
        grid=(bsz, n_pairs),
        in_specs=[blk(0), blk(n_pairs), blk(2 * n_pairs),
                  pl.BlockSpec((None, seq, 1), lambda b, h: (b, 0, 0)),
                  pl.BlockSpec((None, 2, LANES), lambda b, h: (h, 0, 0)),
                  const2, const2,
                  pl.BlockSpec((QBLK, seq), lambda b, h: (0, 0))],
        out_specs=pl.BlockSpec((None, seq, LANES), lambda b, h: (b, 0, h)),
        scratch_shapes=[ext, ext, ext] + _attn_scratch(seq),
        out_shape=jax.ShapeDtypeStruct((bsz, seq, d), BF16),
        compiler_params=_cparams("parallel", "parallel"),
    )(qkv, qkv, qkv, pos.reshape(bsz, seq, 1), sfeat, gq, gk, logm)


def _post_attn_kernel(o_ref, wo_ref, x_ref, g1_ref, n2_ref, sc_ref, sh_ref, wr_ref,
                      xm_ref, h_ref, hs_ref, lg_ref, il_ref):
    y = jnp.dot(o_ref[...], wo_ref[...], preferred_element_type=F32)
    xm = x_ref[...] + g1_ref[...] * y
    xm_ref[...] = xm
    h = _rms(xm) * n2_ref[...]
    h = h * (1.0 + sc_ref[...]) + sh_ref[...]
    h_ref[...] = h.astype(h_ref.dtype)
    hs_ref[...] = _rows_to_slab_rows(h, il_ref)
    lg_ref[...] = lax.dot_general(wr_ref[...], h, (((1,), (1,)), ((), ())),
                                  preferred_element_type=F32, precision=lax.Precision.HIGHEST)


def _post_attn(o, w_o, xf, g1, n2, sc2, sh2, w_router, seq):
    n, d = xf.shape
    ne = w_router.shape[1]
    tm = min(ROW_TILE, seq)
    per_b = seq // tm
    slab_rows = d // LANES
    vec = pl.BlockSpec((None, 1, d), lambda i: (i // per_b, 0, 0))
    row = lambda w: pl.BlockSpec((tm, w), lambda i: (i, 0))
    return pl.pallas_call(
        _post_attn_kernel,
        grid=(n // tm,),
        in_specs=[row(d), pl.BlockSpec((d, d), lambda i: (0, 0)), row(d), vec,
                  pl.BlockSpec((1, d), lambda i: (0, 0)), vec, vec,
                  pl.BlockSpec((ne, d), lambda i: (0, 0))],
        out_specs=[row(d), row(d), pl.BlockSpec((tm * slab_rows, LANES), lambda i: (i, 0)),
                   pl.BlockSpec((ne, tm), lambda i: (0, i))],
        out_shape=[jax.ShapeDtypeStruct((n, d), F32), jax.ShapeDtypeStruct((n, d), BF16),
                   jax.ShapeDtypeStruct((n * slab_rows, LANES), BF16),
                   jax.ShapeDtypeStruct((ne, n), F32)],
        scratch_shapes=[pltpu.VMEM((tm * slab_rows, LANES), F32)],
        compiler_params=_cparams("parallel"),
    )(o, w_o, xf, g1, n2, sc2, sh2, w_router.T)


def _moe_rows(tile):
    rows = tile * TOP_K + N_EXPERTS * (SUBLANES - 1) + EROWS
    return -(-rows // SUBLANES) * SUBLANES


def _first_index(hit, index, limit):
    return jnp.min(jnp.where(hit, index, limit), axis=0, keepdims=True)


def _route_kernel(lg_ref, bias_ref, tri_ref, dest_ref, w_ref, seg_ref, cnt_ref):
    t = lg_ref.shape[1]
    per_g = N_EXPERTS // N_GROUPS
    sub = lax.broadcasted_iota(jnp.int32, (per_g, t), 0).astype(F32)
    neg = -jnp.inf
    scores, sel = [], []
    for g in range(N_GROUPS):
        rows = slice(g * per_g, (g + 1) * per_g)
        sc = jax.nn.sigmoid(lg_ref[rows, :])
        scores.append(sc)
        sel.append(sc + bias_ref[rows, :])

    gs = []
    for g in range(N_GROUPS):
        m1 = jnp.max(sel[g], axis=0, keepdims=True)
        first = _first_index(sel[g] == m1, sub, float(per_g))
        m2 = jnp.max(jnp.where(sub == first, neg, sel[g]), axis=0, keepdims=True)
        gs.append(m1 + m2)
    gsc = jnp.concatenate(gs, axis=0)
    gsub = lax.broadcasted_iota(jnp.int32, (N_GROUPS, t), 0).astype(F32)
    gkeep = jnp.zeros((N_GROUPS, t), F32)
    for _ in range(TOPK_GROUPS):
        m = jnp.max(gsc, axis=0, keepdims=True)
        pick = gsub == _first_index(gsc == m, gsub, float(N_GROUPS))
        gkeep = jnp.where(pick, 1.0, gkeep)
        gsc = jnp.where(pick, neg, gsc)

    eidx = [sub + float(g * per_g) for g in range(N_GROUPS)]
    masked = [jnp.where(jnp.broadcast_to(gkeep[g:g + 1, :], (per_g, t)) > 0.5, sel[g], neg)
              for g in range(N_GROUPS)]
    chosen = [jnp.zeros((per_g, t), F32) for _ in range(N_GROUPS)]
    wsel = [jnp.zeros((per_g, t), F32) for _ in range(N_GROUPS)]
    picked = []
    for _ in range(TOP_K):
        m = jnp.max(masked[0], axis=0, keepdims=True)
        for g in range(1, N_GROUPS):
            m = jnp.maximum(m, jnp.max(masked[g], axis=0, keepdims=True))
        e_first = _first_index(masked[0] == m, eidx[0], float(N_EXPERTS))
        for g in range(1, N_GROUPS):
            e_first = jnp.minimum(e_first, _first_index(masked[g] == m, eidx[g], float(N_EXPERTS)))
        picked.append(e_first)
        for g in range(N_GROUPS):
            pick = eidx[g] == e_first
            chosen[g] = jnp.where(pick, 1.0, chosen[g])
            wsel[g] = jnp.where(pick, scores[g], wsel[g])
            masked[g] = jnp.where(pick, neg, masked[g])

    denom = jnp.sum(wsel[0], axis=0, keepdims=True)
    for g in range(1, N_GROUPS):
        denom = denom + jnp.sum(wsel[g], axis=0, keepdims=True)

    c_all = jnp.concatenate(chosen, axis=0)
    rank = jnp.dot(c_all.astype(BF16), tri_ref[...], preferred_element_type=F32)
    counts = jnp.sum(c_all, axis=1, keepdims=True)
    padded = jnp.ceil(counts / SUBLANES) * SUBLANES
    below = (lax.broadcasted_iota(jnp.int32, (N_EXPERTS, N_EXPERTS), 1)
             < lax.broadcasted_iota(jnp.int32, (N_EXPERTS, N_EXPERTS), 0)).astype(F32)
    seg = jnp.dot(below, jnp.broadcast_to(padded, (N_EXPERTS, LANES)), preferred_element_type=F32,
                  precision=lax.Precision.HIGHEST)
    row_of = seg[:, :1] + rank

    dest_rows, w_rows = [], []
    for k in range(TOP_K):
        d_k = jnp.zeros((1, t), F32)
        w_k = jnp.zeros((1, t), F32)
        for g in range(N_GROUPS):
            hit = eidx[g] == picked[k]
            d_k = d_k + jnp.sum(jnp.where(hit, row_of[g * per_g:(g + 1) * per_g, :], 0.0), axis=0,
                                keepdims=True)
            w_k = w_k + jnp.sum(jnp.where(hit, wsel[g], 0.0), axis=0, keepdims=True)
        dest_rows.append(d_k)
        w_rows.append(w_k / denom * ROUTED_SCALE)
    fill = [jnp.zeros((SUBLANES - TOP_K, t), F32)]
    dest_ref[...] = jnp.concatenate(dest_rows + fill, axis=0).astype(jnp.int32)
    w_ref[...] = jnp.concatenate(w_rows + fill, axis=0)
    seg_ref[...] = seg.astype(jnp.int32)
    cnt_ref[...] = jnp.broadcast_to(counts, (N_EXPERTS, LANES)).astype(jnp.int32)


def _route(logits_t, router_bias, tile):
    ne, n = logits_t.shape
    nt = n // tile
    tri = jnp.asarray(np.triu(np.ones((tile, tile), np.float32), 1), dtype=BF16)
    per_tile = lambda rows, width: pl.BlockSpec((None, rows, width), lambda i: (i, 0, 0))
    dest, wts, seg, cnt = pl.pallas_call(
        _route_kernel,
        grid=(nt,),
        in_specs=[pl.BlockSpec((ne, tile), lambda i: (0, i)),
                  pl.BlockSpec((ne, 1), lambda i: (0, 0)),
                  pl.BlockSpec((tile, tile), lambda i: (0, 0))],
        out_specs=[per_tile(SUBLANES, tile), per_tile(SUBLANES, tile),
                   per_tile(ne, LANES), per_tile(ne, LANES)],
        out_shape=[jax.ShapeDtypeStruct((nt, SUBLANES, tile), jnp.int32),
                   jax.ShapeDtypeStruct((nt, SUBLANES, tile), F32),
                   jax.ShapeDtypeStruct((nt, ne, LANES), jnp.int32),
                   jax.ShapeDtypeStruct((nt, ne, LANES), jnp.int32)],
        compiler_params=_cparams("parallel"),
    )(logits_t, router_bias.astype(F32).reshape(ne, 1), tri)
    return seg[:, :, 0].reshape(-1), cnt[:, :, 0].reshape(-1), dest, wts


def _swiglu(xb, wgu_ref, wdn_ref):
    gu = jnp.dot(xb, wgu_ref[...], preferred_element_type=F32)
    half = wgu_ref.shape[1] // 2
    act = _silu(gu[:, :half]) * gu[:, half:]
    return jnp.dot(act.astype(BF16), wdn_ref[...], preferred_element_type=F32)


def _slabs_to_rows(p, il_ref, r):
    n = il_ref.shape[0] // r
    il_ref[...] = pltpu.bitcast(p, BF16).astype(F32)
    return jnp.concatenate([il_ref[pl.ds(2 * c + h, r, stride=n), :]
                            for h in range(2) for c in range(n // 2)], axis=1)


def _rows_to_slab_rows(v, il_ref):
    r = v.shape[0]
    n = il_ref.shape[0] // r
    for h in range(2):
        for c in range(n // 2):
            lane0 = (h * (n // 2) + c) * LANES
            il_ref[pl.ds(2 * c + h, r, stride=n), :] = v[:, lane0:lane0 + LANES]
    return il_ref[...].astype(BF16)


def _rows_to_slabs(v, il_ref):
    return pltpu.bitcast(_rows_to_slab_rows(v, il_ref), U32)


def _dest_col(half, k, u):
    return (half * SUBLANES + k) * SUBLANES + u


def _moe_kernel(start_ref, count_ref, hp_ref, hb_ref, dest_ref, wt_ref, wgu_hbm, wdn_hbm,
                sgu_ref, sdn_ref, xm_ref, g2_ref, o_ref, xy_ref, il_ref, yp_ref, stage_ref, zf_ref,
                comb_ref, wgu_buf, wdn_buf, wsem, *, layer):
    t = pl.program_id(0)
    s = pl.program_id(1)
    slab = xm_ref.shape[1] // (2 * LANES)
    tile = hp_ref.shape[0] // (2 * slab)
    group_rows = SUBLANES * slab

    def weight_copies(e, slot):
        return (pltpu.make_async_copy(wgu_hbm.at[layer, e], wgu_buf.at[slot], wsem.at[0, slot]),
                pltpu.make_async_copy(wdn_hbm.at[layer, e], wdn_buf.at[slot], wsem.at[1, slot]))

    def slab_at(ref, first):
        return ref.at[pl.ds(pl.multiple_of(first, slab), slab), :]

    @pl.when(s == 0)
    def _first_weights():
        for p in range(2):
            for cp in weight_copies(p, p):
                cp.start()

    @pl.when(jnp.logical_and(t == 0, s == 0))
    def _clear():
        xy_ref[...] = jnp.zeros(xy_ref.shape, U32)

    @pl.when(s == 0)
    def _dispatch():
        def group(j, carry):
            for half in range(2):
                p0 = pl.multiple_of((2 * j + half) * 2 * group_rows, 2 * group_rows)
                rows = pltpu.bitcast(hp_ref[pl.ds(p0, 2 * group_rows), :], U32)
                for u in range(SUBLANES):
                    for k in range(TOP_K):
                        slab_at(xy_ref, dest_ref[j, _dest_col(half, k, u)])[...] = (
                            rows[u * slab:(u + 1) * slab, :])
            return carry

        lax.fori_loop(0, tile // (2 * SUBLANES), group, 0)

    @pl.when(s == 0)
    def _experts():
        def chunk_base(e, c):
            return pl.multiple_of((start_ref[t * N_EXPERTS + e] + c * EROWS) * slab, group_rows)

        def load_rows(base, p):
            return _slabs_to_rows(xy_ref[pl.ds(base, EROWS * slab), :], il_ref.at[p], EROWS).astype(BF16)

        def put_rows(p, base, valid):
            def put(j, c2):
                r0 = pl.multiple_of(j * group_rows, group_rows)
                xy_ref[pl.ds(base + r0, group_rows), :] = yp_ref[p, pl.ds(r0, group_rows), :]
                return c2

            lax.fori_loop(0, (jnp.clip(valid, 0, EROWS) + SUBLANES - 1) // SUBLANES, put, 0)

        def expert_pair(i, carry0):
            wslot = lax.rem(i, 2) * 2

            @pl.when(i + 1 < N_EXPERTS // 2)
            def _prefetch():
                for p in range(2):
                    for cp in weight_copies(2 * i + 2 + p, 2 - wslot + p):
                        cp.start()

            for p in range(2):
                for cp in weight_copies(2 * i + p, wslot + p):
                    cp.wait()
            counts = [count_ref[t * N_EXPERTS + 2 * i + p] for p in range(2)]
            bases = [chunk_base(2 * i + p, 0) for p in range(2)]
            xs = [load_rows(bases[p], p) for p in range(2)]
            ys = [_swiglu(xs[p], wgu_buf.at[wslot + p], wdn_buf.at[wslot + p]) for p in range(2)]
            for p in range(2):
                yp_ref[p] = _rows_to_slabs(ys[p], il_ref.at[p])
            for p in range(2):
                put_rows(p, bases[p], counts[p])

            for p in range(2):
                def more(c, carry, p=p):
                    base = chunk_base(2 * i + p, c)
                    y = _swiglu(load_rows(base, p), wgu_buf.at[wslot + p], wdn_buf.at[wslot + p])
                    yp_ref[p] = _rows_to_slabs(y, il_ref.at[p])
                    put_rows(p, base, counts[p] - c * EROWS)
                    return carry

                lax.fori_loop(1, (counts[p] + EROWS - 1) // EROWS, more, 0)
            return carry0

        lax.fori_loop(0, N_EXPERTS // 2, expert_pair, 0)

    @pl.when(s >= 1)
    def _combine():
        sb = s - 1

        shared = _swiglu(hb_ref[...], sgu_ref, sdn_ref)
        for j in range(CROWS // (2 * SUBLANES)):
            g2 = sb * (CROWS // (2 * SUBLANES)) + j
            for half in range(2):
                r0 = j * 2 * SUBLANES + half * SUBLANES
                t0 = pl.multiple_of(sb * CROWS + r0, SUBLANES)
                for k in range(TOP_K):
                    for u in range(SUBLANES):
                        first = (k * SUBLANES + u) * slab
                        stage_ref[half, first:first + slab, :] = (
                            slab_at(xy_ref, dest_ref[g2, _dest_col(half, k, u)])[...])
                rows = _slabs_to_rows(stage_ref[half], zf_ref.at[half], TOP_K * SUBLANES)
                wv = wt_ref[pl.ds(t0, SUBLANES), :]
                acc = wv[:, 0:1] * rows[0:SUBLANES, :]
                for k in range(1, TOP_K):
                    acc = acc + wv[:, k:k + 1] * rows[k * SUBLANES:(k + 1) * SUBLANES, :]
                comb_ref[r0:r0 + SUBLANES, :] = acc
        o_ref[...] = xm_ref[...] + g2_ref[...] * (comb_ref[...] + shared)


def _moe(h, h_slabs, xm, g2, logits_t, router_bias, w_gu, w_dn, s_gu, s_dn, layer, seq):
    n, d = h.shape
    slab = d // (2 * LANES)
    tile = min(MOE_TILE, seq)
    nt = n // tile
    per_b = seq // tile
    csteps = tile // CROWS
    assert seq % tile == 0 and tile % CROWS == 0
    seg_start, seg_count, dest, wts = _route(logits_t, router_bias, tile)
    dest_rows = dest.reshape(nt, SUBLANES, tile // (2 * SUBLANES), 2, SUBLANES) * slab
    dest_rows = dest_rows.transpose(0, 2, 3, 1, 4).reshape(nt, tile // (2 * SUBLANES), -1)
    rows = _moe_rows(tile)
    cst = lambda s: jnp.maximum(s - 1, 0)
    crow = pl.BlockSpec((CROWS, d), lambda t, s, a, b: (t * csteps + cst(s), 0))
    hbm = pl.BlockSpec(memory_space=pl.ANY)
    single = pl.Buffered(1)
    return pl.pallas_call(
        functools.partial(_moe_kernel, layer=layer),
        grid_spec=pltpu.PrefetchScalarGridSpec(
            num_scalar_prefetch=2,
            grid=(nt, 1 + csteps),
            in_specs=[pl.BlockSpec((tile * 2 * slab, LANES), lambda t, s, a, b: (t, 0),
                                   pipeline_mode=single),
                      crow,
                      pl.BlockSpec((None, tile // (2 * SUBLANES), 2 * SUBLANES * SUBLANES),
                                   lambda t, s, a, b: (t, 0, 0), memory_space=pltpu.SMEM),
                      pl.BlockSpec((None, tile, SUBLANES), lambda t, s, a, b: (t, 0, 0),
                                   pipeline_mode=single),
                      hbm, hbm,
                      pl.BlockSpec((None,) + s_gu.shape[1:], lambda t, s, a, b: (layer, 0, 0),
                                   pipeline_mode=single),
                      pl.BlockSpec((None,) + s_dn.shape[1:], lambda t, s, a, b: (layer, 0, 0),
                                   pipeline_mode=single),
                      crow,
                      pl.BlockSpec((None, 1, d), lambda t, s, a, b: (t // per_b, 0, 0))],
            out_specs=crow,
            scratch_shapes=[pltpu.VMEM((rows * slab, LANES), U32),
                            pltpu.VMEM((2, EROWS * 2 * slab, LANES), F32),
                            pltpu.VMEM((2, EROWS * slab, LANES), U32),
                            pltpu.VMEM((2, TOP_K * SUBLANES * slab, LANES), U32),
                            pltpu.VMEM((2, TOP_K * SUBLANES * 2 * slab, LANES), F32),
                            pltpu.VMEM((CROWS, d), F32),
                            pltpu.VMEM((4, d, 2 * EXPERT_DIM), BF16),
                            pltpu.VMEM((4, EXPERT_DIM, d), BF16),
                            pltpu.SemaphoreType.DMA((2, 4))]),
        out_shape=jax.ShapeDtypeStruct((n, d), F32),
        compiler_params=_cparams("arbitrary", "arbitrary"),
    )(seg_start, seg_count, h_slabs, h, dest_rows, jnp.swapaxes(wts, 1, 2),
      w_gu, w_dn, s_gu, s_dn, xm, g2)


def _mla_prep_kernel(x_ref, n1_ref, sc1_ref, sh1_ref, nk_ref, sck_ref, shk_ref,
                     wdq_ref, gql_ref, wuq_ref, wuqs_ref, wdown_ref, gkl_ref, wk_ref, gk_ref, wuv_ref,
                     qa_ref, qb_ref, kc_ref, ksa_ref, ksb_ref, q_ref, k_ref, v_ref):
    xn = _rms(x_ref[...])
    hq = xn * n1_ref[...] * (1.0 + sc1_ref[...]) + sh1_ref[...]
    hk = xn * nk_ref[...] * (1.0 + sck_ref[...]) + shk_ref[...]
    n_heads = q_ref.shape[1] // LANES
    half = ROPE_DIM // 2

    cq = (_rms(jnp.dot(hq.astype(BF16), wdq_ref[...], preferred_element_type=F32)) * gql_ref[...]).astype(BF16)
    q_raw = jnp.dot(cq, wuq_ref[...], preferred_element_type=F32)
    q_swp = jnp.dot(cq, wuqs_ref[...], preferred_element_type=F32)
    qa, qb = qa_ref[...], qb_ref[...]
    for h in range(n_heads):
        sl = slice(h * LANES, (h + 1) * LANES)
        t = q_raw[:, sl]
        r = lax.rsqrt(jnp.sum(t * t, axis=-1, keepdims=True) / QK_DIM + NORM_EPS)
        q_ref[:, sl] = (r * (t * qa + q_swp[:, sl] * qb)).astype(q_ref.dtype)

    down = jnp.dot(hk.astype(BF16), wdown_ref[...], preferred_element_type=F32)
    ckv = (_rms(down[:, :KV_RANK]) * gkl_ref[...]).astype(BF16)
    pe = down[:, KV_RANK:]
    ss_pe = jnp.sum(pe * pe, axis=-1, keepdims=True)
    rot = (pe * kc_ref[...] + pltpu.roll(pe, LANES - half, 1) * ksa_ref[...]
           + pltpu.roll(pe, half, 1) * ksb_ref[...])
    rot_hi = rot.astype(BF16)
    rot_lo = (rot - rot_hi.astype(F32)).astype(BF16)
    k_raw = jnp.dot(jnp.concatenate([ckv, rot_hi, rot_lo], axis=1), wk_ref[...],
                    preferred_element_type=F32)
    nope = lax.broadcasted_iota(jnp.int32, (x_ref.shape[0], LANES), 1) < NOPE_DIM
    for h in range(n_heads):
        sl = slice(h * LANES, (h + 1) * LANES)
        t = k_raw[:, sl]
        ss = jnp.sum(jnp.where(nope, t * t, 0.0), axis=-1, keepdims=True) + ss_pe
        k_ref[:, sl] = (t * lax.rsqrt(ss / QK_DIM + NORM_EPS) * gk_ref[...]).astype(k_ref.dtype)
    v_ref[...] = jnp.dot(ckv, wuv_ref[...], preferred_element_type=F32).astype(v_ref.dtype)


def _pad_heads(w, n_heads, width):
    r = w.shape[0]
    return jnp.pad(w.reshape(r, n_heads, width), ((0, 0), (0, 0), (0, LANES - width))).reshape(r, n_heads * LANES)


def _swap_rope_halves(a):
    half = ROPE_DIM // 2
    lead = a.shape[:-1]
    return jnp.concatenate([jnp.zeros(lead + (NOPE_DIM,), a.dtype), a[..., NOPE_DIM + half:QK_DIM],
                            a[..., NOPE_DIM:NOPE_DIM + half],
                            jnp.zeros(lead + (LANES - QK_DIM,), a.dtype)], axis=-1)


def _mla_prep(xf, pos, n1, sc1, sh1, nk, sck, shk, w_dq, g_ql, w_uq, g_q, w_down, g_kl, w_uk, w_uv,
              g_k, seq):
    n, d = xf.shape
    n_heads = w_uv.shape[1] // V_DIM
    q_rank = w_dq.shape[1]
    hw = n_heads * LANES
    tm = 256
    per_b = seq // tm

    half = ROPE_DIM // 2
    inv = jnp.power(ROPE_THETA, -jnp.arange(half, dtype=F32) / half)
    ang = pos.astype(F32).reshape(n, 1) * inv
    cos, sin = jnp.cos(ang), jnp.sin(ang)
    ones = jnp.ones((n, NOPE_DIM), F32)
    zeros = lambda wdt: jnp.zeros((n, wdt), F32)
    tail = LANES - QK_DIM
    pad_gain = lambda g: jnp.pad(g.reshape(1, QK_DIM), ((0, 0), (0, tail)))
    q_mul = QK_DIM ** -0.5 * LOG2E
    qa_t = jnp.concatenate([ones, cos, cos, zeros(tail)], axis=1) * pad_gain(g_q) * q_mul
    qb_t = (jnp.concatenate([zeros(NOPE_DIM), -sin, sin, zeros(tail)], axis=1)
            * _swap_rope_halves(g_q.reshape(1, QK_DIM)) * q_mul)
    gk1, gk2 = g_k[NOPE_DIM:NOPE_DIM + half], g_k[NOPE_DIM + half:]
    rest = zeros(LANES - ROPE_DIM)
    kc_t = jnp.concatenate([cos * gk1, cos * gk2, rest], axis=1)
    ksa_t = jnp.concatenate([-sin * gk2, zeros(half), rest], axis=1)
    ksb_t = jnp.concatenate([zeros(half), sin * gk1, rest], axis=1)
    gk_mask = jnp.concatenate([g_k[:NOPE_DIM], jnp.ones((ROPE_DIM,), F32),
                               jnp.zeros((tail,), F32)]).reshape(1, LANES)

    wuq_p = _pad_heads(w_uq, n_heads, QK_DIM).astype(BF16)
    wuq_s = _swap_rope_halves(w_uq.reshape(q_rank, n_heads, QK_DIM)).reshape(q_rank, hw).astype(BF16)
    wdown_p = jnp.pad(w_down, ((0, 0), (0, LANES - ROPE_DIM))).astype(BF16)
    eye = jnp.pad(jnp.eye(ROPE_DIM, dtype=F32), ((0, LANES - ROPE_DIM), (NOPE_DIM, tail)))
    pe_rows = jnp.tile(eye, (1, n_heads))
    wk = jnp.concatenate([_pad_heads(w_uk, n_heads, NOPE_DIM), pe_rows, pe_rows], axis=0).astype(BF16)

    vec = pl.BlockSpec((None, 1, d), lambda i: (i // per_b, 0, 0))
    full = lambda a: pl.BlockSpec(a.shape, lambda i: (0,) * a.ndim)
    row = lambda wdt: pl.BlockSpec((tm, wdt), lambda i: (i, 0))
    args = [xf, n1, sc1, sh1, nk, sck, shk,
            w_dq.astype(BF16), g_ql.reshape(1, q_rank), wuq_p, wuq_s,
            wdown_p, g_kl.reshape(1, KV_RANK), wk, gk_mask, w_uv.astype(BF16),
            qa_t, qb_t, kc_t, ksa_t, ksb_t]
    specs = [row(d), full(n1), vec, vec, full(nk), vec, vec] + [full(a) for a in args[7:16]] + [row(LANES)] * 5
    return pl.pallas_call(
        _mla_prep_kernel,
        grid=(n // tm,),
        in_specs=specs,
        out_specs=[row(hw), row(hw), row(n_heads * V_DIM)],
        out_shape=[jax.ShapeDtypeStruct((n, hw), BF16), jax.ShapeDtypeStruct((n, hw), BF16),
                   jax.ShapeDtypeStruct((n, n_heads * V_DIM), BF16)],
        compiler_params=_cparams("parallel"),
    )(*args)


def _causal_bias(seq):
    a = np.arange(QBLK)[:, None]
    c = np.arange(seq)[None, :]
    return np.where(c <= (seq - QBLK) + a, 0.0, MASKED).astype(np.float32)


def _mla_attn_kernel(q_ref, k_ref, v_ref, mask_ref, o_ref, s_ref, p_ref):
    seq = q_ref.shape[0]
    lane_lo = lax.broadcasted_iota(jnp.int32, (QBLK, LANES), 1) < V_DIM

    def qk_of(blk, h, w):
        lanes = slice(h * LANES, (h + 1) * LANES)
        return q_ref[w - QBLK:w, lanes], k_ref[0:w, lanes]

    _attention_blocks(seq, qk_of, v_ref, mask_ref, s_ref, p_ref, o_ref, lane_lo)


def _mla_attention(q, k, v):
    bsz, seq, hw = q.shape
    n_pairs = hw // (2 * LANES)
    qk = pl.BlockSpec((None, seq, 2 * LANES), lambda b, h: (b, 0, h))
    vo = pl.BlockSpec((None, seq, LANES), lambda b, h: (b, 0, h))
    return pl.pallas_call(
        _mla_attn_kernel,
        grid=(bsz, n_pairs),
        in_specs=[qk, qk, vo, pl.BlockSpec((QBLK, seq), lambda b, h: (0, 0))],
        out_specs=vo,
        scratch_shapes=_attn_scratch(seq),
        out_shape=jax.ShapeDtypeStruct(v.shape, BF16),
        compiler_params=_cparams("parallel", "parallel"),
    )(q, k, v, jnp.asarray(_causal_bias(seq)))


def kernel(x, c, pos, ada_w, ada_b, norm1_g, norm2_g, a_w_qkv, a_q_norm, a_k_norm, a_w_o, kv_ada_w, kv_ada_b, kv_norm_g, kv_w_down, kv_latent_norm, kv_w_uk, kv_w_uv, kv_k_norm, b_w_dq, b_q_latent_norm, b_w_uq, b_q_norm, b_w_o, moe_w_router, moe_router_bias, moe_w_gate_up, moe_w_down, moe_shared_gate_up, moe_shared_down):
    bsz, seq, d = x.shape
    n = bsz * seq
    depth = ada_w.shape[0]
    n_a = a_w_qkv.shape[0]
    assert seq % QBLK == 0
    xf = x.reshape(n, d)

    w_gu = moe_w_gate_up.astype(BF16)
    w_dn = moe_w_down.astype(BF16)
    s_gu = moe_shared_gate_up.astype(BF16)
    s_dn = moe_shared_down.astype(BF16)
    row = lambda g: g.reshape(1, d)

    for layer in range(depth):
        mod = _modulation(c, ada_w, ada_b, layer).reshape(bsz, 6, 1, d)
        sh1, sc1, g1, sh2, sc2, g2 = [mod[:, i] for i in range(6)]
        if layer < n_a:
            qkv = _qkv_proj(xf, row(norm1_g[layer]), sc1, sh1, a_w_qkv[layer].astype(BF16), seq)
            o = _dilated_attention(qkv.reshape(bsz, seq, -1), pos, a_q_norm[layer], a_k_norm[layer])
            w_o = a_w_o[layer]
        else:
            j = layer - n_a
            kmod = _modulation(c, kv_ada_w[None], kv_ada_b[None], 0).reshape(bsz, 2, 1, d)
            q, k, v = _mla_prep(xf, pos, row(norm1_g[layer]), sc1, sh1, row(kv_norm_g), kmod[:, 1],
                                kmod[:, 0], b_w_dq[j], b_q_latent_norm[j], b_w_uq[j], b_q_norm[j],
                                kv_w_down, kv_latent_norm, kv_w_uk, kv_w_uv, kv_k_norm, seq)
            o = _mla_attention(q.reshape(bsz, seq, -1), k.reshape(bsz, seq, -1),
                               v.reshape(bsz, seq, -1))
            w_o = b_w_o[j]
        xm, h2, h2_slabs, logits = _post_attn(o.reshape(n, d), w_o.astype(BF16), xf, g1,
                                              row(norm2_g[layer]), sc2, sh2, moe_w_router[layer], seq)
        xf = _moe(h2, h2_slabs, xm, g2, logits, moe_router_bias[layer], w_gu, w_dn, s_gu, s_dn,
                  layer, seq)
    return xf.reshape(bsz, seq, d)
```

```python
import functools
import math

import numpy as np
import jax
import jax.numpy as jnp
from jax import lax
from jax.experimental import pallas as pl
from jax.experimental.pallas import tpu as pltpu

F32 = jnp.float32
BF16 = jnp.bfloat16
U32 = jnp.uint32

NORM_EPS = 1e-6
MASKED = -1e30
LOG2E = math.log2(math.e)

LANES = 128
SUBLANES = 8
VMEM_LIMIT = 56 * 1024 * 1024

A_HEAD_DIM = 64
DILATED_PAIRS = ((128, 1), (512, 4), (2048, 16))
QBLK = 256
NOPE_DIM = 64
ROPE_DIM = 32
QK_DIM = NOPE_DIM + ROPE_DIM
V_DIM = 64
KV_RANK = 256
ROPE_THETA = 10000.0
N_EXPERTS = 64
N_GROUPS = 8
TOPK_GROUPS = 4
TOP_K = 6
EXPERT_DIM = 256
ROUTED_SCALE = 2.5
MOE_TILE = 2048
EROWS = 256
CROWS = 256

ROW_TILE = 512


def _cparams(*sem):
    return pltpu.CompilerParams(dimension_semantics=sem, vmem_limit_bytes=VMEM_LIMIT)


def _silu(v):
    return v * jax.nn.sigmoid(v)


def _rms(v):
    return v * lax.rsqrt(jnp.mean(v * v, axis=-1, keepdims=True) + NORM_EPS)


def _dot_nt(a, b):
    return lax.dot_general(a, b, (((1,), (1,)), ((), ())), preferred_element_type=F32)


def _mod_kernel(c_ref, w_ref, b_ref, o_ref):
    cond = _silu(c_ref[...])
    o_ref[...] = jnp.dot(cond, w_ref[...], preferred_element_type=F32,
                         precision=lax.Precision.HIGHEST) + b_ref[...]


def _modulation(c, w, b, layer):
    bsz, d = c.shape
    m = w.shape[-1]
    tn = 1024
    return pl.pallas_call(
        _mod_kernel,
        grid=(m // tn,),
        in_specs=[pl.BlockSpec((bsz, d), lambda j: (0, 0)),
                  pl.BlockSpec((None, d, tn), lambda j: (layer, 0, j)),
                  pl.BlockSpec((None, 1, tn), lambda j: (layer, 0, j))],
        out_specs=pl.BlockSpec((bsz, tn), lambda j: (0, j)),
        out_shape=jax.ShapeDtypeStruct((bsz, m), F32),
        compiler_params=_cparams("arbitrary"),
    )(c, w, b.reshape(b.shape[0], 1, m))


def _qkv_kernel(x_ref, g_ref, sc_ref, sh_ref, w_ref, o_ref):
    h = _rms(x_ref[...]) * g_ref[...]
    h = h * (1.0 + sc_ref[...]) + sh_ref[...]
    o_ref[...] = jnp.dot(h.astype(BF16), w_ref[...], preferred_element_type=F32).astype(o_ref.dtype)


def _qkv_proj(xf, g, sc, sh, w, seq):
    n, d = xf.shape
    m = w.shape[1]
    tm = min(ROW_TILE, seq)
    per_b = seq // tm
    vec = pl.BlockSpec((None, 1, d), lambda i: (i // per_b, 0, 0))
    return pl.pallas_call(
        _qkv_kernel,
        grid=(n // tm,),
        in_specs=[pl.BlockSpec((tm, d), lambda i: (i, 0)),
                  pl.BlockSpec((1, d), lambda i: (0, 0)),
                  vec, vec,
                  pl.BlockSpec((d, m), lambda i: (0, 0))],
        out_specs=pl.BlockSpec((tm, m), lambda i: (i, 0)),
        out_shape=jax.ShapeDtypeStruct((n, m), BF16),
        compiler_params=_cparams("parallel"),
    )(xf, g, sc, sh, w)


ATT_SLOTS = 4


def _attention_blocks(seq, qk_of, v_ref, bias_ref, s_ref, p_ref, o_ref, lane_lo):
    nblk = seq // QBLK

    def slot(blk, h):
        return (blk % 2) * 2 + h

    def scores(blk):
        w = (blk + 1) * QBLK
        bias = bias_ref[:, seq - w:seq]
        for h in range(2):
            q, k = qk_of(blk, h, w)
            s_ref[slot(blk, h), :, 0:w] = _dot_nt(q, k) + bias

    for blk in range(nblk):
        w = (blk + 1) * QBLK
        scores(blk)
        denom = []
        for h in range(2):
            s = s_ref[slot(blk, h), :, 0:w]
            p = jnp.exp2(s - jnp.max(s, axis=-1, keepdims=True))
            denom.append(jnp.sum(p, axis=-1, keepdims=True))
            p_ref[slot(blk, h), :, 0:w] = p.astype(BF16)
        o_a, o_b = [jnp.dot(p_ref[slot(blk, h), :, 0:w], v_ref[0:w, :],
                            preferred_element_type=F32) / denom[h] for h in range(2)]
        o_ref[blk * QBLK:w, :] = jnp.where(lane_lo, o_a, o_b).astype(o_ref.dtype)


def _attn_scratch(seq):
    return [pltpu.VMEM((ATT_SLOTS, QBLK, seq), F32), pltpu.VMEM((ATT_SLOTS, QBLK, seq), BF16)]


def _cast_rider(weights, layer, grid):
    steps = grid[0] * grid[1]
    in_specs, out_specs, out_shapes = [], [], []
    for w in weights:
        n_exp = w.shape[1]
        assert n_exp % steps == 0
        per = n_exp // steps
        blk = (per,) + w.shape[2:]
        in_specs.append(pl.BlockSpec((None,) + blk, lambda b, h: (layer, b * grid[1] + h, 0, 0)))
        out_specs.append(pl.BlockSpec(blk, lambda b, h: (b * grid[1] + h, 0, 0)))
        out_shapes.append(jax.ShapeDtypeStruct(w.shape[1:], BF16))
    return in_specs, out_specs, out_shapes


def _split3(v):
    p1 = v.astype(BF16).astype(F32)
    p2 = (v - p1).astype(BF16).astype(F32)
    return p1, p2, v - p1 - p2


def _dilated_log_multiplicity(seq):
    a = np.arange(QBLK)[:, None]
    c = np.arange(seq)[None, :]
    d = (seq - QBLK) + a - c
    mult = np.zeros((QBLK, seq), np.int32)
    for win, dil in DILATED_PAIRS:
        mult += ((d >= 0) & (d % dil == 0) & (d <= win)).astype(np.int32)
    return np.where(mult > 0, np.log2(np.maximum(mult, 1)), MASKED).astype(np.float32)


N_FEAT = 9


def _dil_attn_kernel(q_ref, k_ref, v_ref, posc_ref, sfeat_ref, gq_ref, gk_ref, logm_ref,
                     wa_ref, wb_ref, o_ref, wa_out, wb_out, qa_ref, qb_ref, ks_ref, s_ref, p_ref):
    wa_out[...] = wa_ref[...].astype(BF16)
    wb_out[...] = wb_ref[...].astype(BF16)
    seq = q_ref.shape[0]
    lane = lax.broadcasted_iota(jnp.int32, (QBLK, LANES), 1)
    lane_lo = lane < A_HEAD_DIM
    piece = [functools.reduce(jnp.logical_or, [lane == i + 3 * j for j in range(N_FEAT // 3)])
             for i in range(3)]

    def head_norm(ref, r0, g_ref, mul):
        t = ref[r0:r0 + QBLK, :].astype(F32)
        sq = t * t
        ssa = jnp.sum(jnp.where(lane_lo, sq, 0.0), axis=-1, keepdims=True)
        ssb = jnp.sum(jnp.where(lane_lo, 0.0, sq), axis=-1, keepdims=True)
        r = jnp.where(lane_lo, lax.rsqrt(ssa / A_HEAD_DIM + NORM_EPS),
                      lax.rsqrt(ssb / A_HEAD_DIM + NORM_EPS))
        return t * r * (g_ref[...] * mul)

    rel0 = posc_ref[0:1, :]
    for r0 in range(0, seq, QBLK):
        rows = slice(r0, r0 + QBLK)
        qn = head_norm(q_ref, r0, gq_ref, A_HEAD_DIM ** -0.5 * LOG2E)
        qa_ref[rows, :LANES] = jnp.where(lane_lo, qn, 0.0).astype(BF16)
        qb_ref[rows, :LANES] = jnp.where(lane_lo, 0.0, qn).astype(BF16)
        qa_ref[rows, LANES:] = jnp.broadcast_to(sfeat_ref[0:1, :], (QBLK, LANES)).astype(BF16)
        qb_ref[rows, LANES:] = jnp.broadcast_to(sfeat_ref[1:2, :], (QBLK, LANES)).astype(BF16)
        ks_ref[rows, :LANES] = head_norm(k_ref, r0, gk_ref, 1.0).astype(BF16)
        p1, p2, p3 = _split3((posc_ref[rows, :] - rel0).astype(F32))
        feat = jnp.where(piece[0], p1, jnp.where(piece[1], p2, jnp.where(piece[2], p3, 0.0)))
        ks_ref[rows, LANES:] = feat.astype(BF16)

    def qk_of(blk, h, w):
        return (qa_ref, qb_ref)[h][w - QBLK:w, :], ks_ref[0:w, :]

    _attention_blocks(seq, qk_of, v_ref, logm_ref, s_ref, p_ref, o_ref, lane_lo)


def _dilated_attention(qkv, pos, q_norm, k_norm, cast_weights, layer):
    bsz, seq, three_d = qkv.shape
    d = three_d // 3
    n_pairs = d // LANES
    n_heads = d // A_HEAD_DIM
    slopes = jnp.exp2(-8.0 * jnp.arange(1, n_heads + 1, dtype=F32) / n_heads) * LOG2E
    sfeat = jnp.repeat(jnp.stack(_split3(slopes), axis=1), 3, axis=1)
    sfeat = jnp.pad(sfeat, ((0, 0), (0, LANES - N_FEAT))).reshape(n_pairs, 2, LANES)
    logm = jnp.asarray(_dilated_log_multiplicity(seq))
    gq = jnp.tile(q_norm.reshape(1, A_HEAD_DIM), (1, 2))
    gk = jnp.tile(k_norm.reshape(1, A_HEAD_DIM), (1, 2))
    blk = lambda off: pl.BlockSpec((None, seq, LANES), lambda b, h: (b, 0, off + h))
    const2 = pl.BlockSpec((1, LANES), lambda b, h: (0, 0))
    ext = pltpu.VMEM((seq, 2 * LANES), BF16)
    grid = (bsz, n_pairs)
    w_in, w_out, w_shapes = _cast_rider(cast_weights, layer, grid)
    return pl.pallas_call(
        _dil_attn_kernel,
        grid=grid,
        in_specs=[blk(0), blk(n_pairs), blk(2 * n_pairs),
                  pl.BlockSpec((None, seq, 1), lambda b, h: (b, 0, 0)),
                  pl.BlockSpec((None, 2, LANES), lambda b, h: (h, 0, 0)),
                  const2, const2,
                  pl.BlockSpec((QBLK, seq), lambda b, h: (0, 0))] + w_in,
        out_specs=[pl.BlockSpec((None, seq, LANES), lambda b, h: (b, 0, h))] + w_out,
        scratch_shapes=[ext, ext, ext] + _attn_scratch(seq),
        out_shape=[jax.ShapeDtypeStruct((bsz, seq, d), BF16)] + w_shapes,
        compiler_params=_cparams("parallel", "parallel"),
    )(qkv, qkv, qkv, pos.reshape(bsz, seq, 1), sfeat, gq, gk, logm, *cast_weights)


def _post_attn_kernel(o_ref, wo_ref, x_ref, g1_ref, n2_ref, sc_ref, sh_ref, wr_ref,
                      xm_ref, h_ref, hs_ref, lg_ref, il_ref):
    y = jnp.dot(o_ref[...], wo_ref[...], preferred_element_type=F32)
    xm = x_ref[...] + g1_ref[...] * y
    xm_ref[...] = xm
    h = _rms(xm) * n2_ref[...]
    h = h * (1.0 + sc_ref[...]) + sh_ref[...]
    h_ref[...] = h.astype(h_ref.dtype)
    hs_ref[...] = _rows_to_slab_rows(h, il_ref)
    lg_ref[...] = lax.dot_general(wr_ref[...], h, (((1,), (1,)), ((), ())),
                                  preferred_element_type=F32, precision=lax.Precision.HIGHEST)


def _post_attn(o, w_o, xf, g1, n2, sc2, sh2, w_router, seq):
    n, d = xf.shape
    ne = w_router.shape[1]
    tm = min(ROW_TILE, seq)
    per_b = seq // tm
    slab_rows = d // LANES
    vec = pl.BlockSpec((None, 1, d), lambda i: (i // per_b, 0, 0))
    row = lambda w: pl.BlockSpec((tm, w), lambda i: (i, 0))
    return pl.pallas_call(
        _post_attn_kernel,
        grid=(n // tm,),
        in_specs=[row(d), pl.BlockSpec((d, d), lambda i: (0, 0)), row(d), vec,
                  pl.BlockSpec((1, d), lambda i: (0, 0)), vec, vec,
                  pl.BlockSpec((ne, d), lambda i: (0, 0))],
        out_specs=[row(d), row(d), pl.BlockSpec((tm * slab_rows, LANES), lambda i: (i, 0)),
                   pl.BlockSpec((ne, tm), lambda i: (0, i))],
        out_shape=[jax.ShapeDtypeStruct((n, d), F32), jax.ShapeDtypeStruct((n, d), BF16),
                   jax.ShapeDtypeStruct((n * slab_rows, LANES), BF16),
                   jax.ShapeDtypeStruct((ne, n), F32)],
        scratch_shapes=[pltpu.VMEM((tm * slab_rows, LANES), F32)],
        compiler_params=_cparams("parallel"),
    )(o, w_o, xf, g1, n2, sc2, sh2, w_router.T)


def _moe_rows(tile):
    rows = tile * TOP_K + N_EXPERTS * (SUBLANES - 1) + EROWS
    return -(-rows // SUBLANES) * SUBLANES


def _first_index(hit, index, limit):
    return jnp.min(jnp.where(hit, index, limit), axis=0, keepdims=True)


def _route_kernel(lg_ref, bias_ref, tri_ref, dest_ref, w_ref, seg_ref, cnt_ref):
    t = lg_ref.shape[1]
    per_g = N_EXPERTS // N_GROUPS
    sub = lax.broadcasted_iota(jnp.int32, (per_g, t), 0).astype(F32)
    neg = -jnp.inf
    scores, sel = [], []
    for g in range(N_GROUPS):
        rows = slice(g * per_g, (g + 1) * per_g)
        sc = jax.nn.sigmoid(lg_ref[rows, :])
        scores.append(sc)
        sel.append(sc + bias_ref[rows, :])

    gs = []
    for g in range(N_GROUPS):
        m1 = jnp.max(sel[g], axis=0, keepdims=True)
        first = _first_index(sel[g] == m1, sub, float(per_g))
        m2 = jnp.max(jnp.where(sub == first, neg, sel[g]), axis=0, keepdims=True)
        gs.append(m1 + m2)
    gsc = jnp.concatenate(gs, axis=0)
    gsub = lax.broadcasted_iota(jnp.int32, (N_GROUPS, t), 0).astype(F32)
    gkeep = jnp.zeros((N_GROUPS, t), F32)
    for _ in range(TOPK_GROUPS):
        m = jnp.max(gsc, axis=0, keepdims=True)
        pick = gsub == _first_index(gsc == m, gsub, float(N_GROUPS))
        gkeep = jnp.where(pick, 1.0, gkeep)
        gsc = jnp.where(pick, neg, gsc)

    eidx = [sub + float(g * per_g) for g in range(N_GROUPS)]
    masked = [jnp.where(jnp.broadcast_to(gkeep[g:g + 1, :], (per_g, t)) > 0.5, sel[g], neg)
              for g in range(N_GROUPS)]
    chosen = [jnp.zeros((per_g, t), F32) for _ in range(N_GROUPS)]
    wsel = [jnp.zeros((per_g, t), F32) for _ in range(N_GROUPS)]
    picked = []
    for _ in range(TOP_K):
        m = jnp.max(masked[0], axis=0, keepdims=True)
        for g in range(1, N_GROUPS):
            m = jnp.maximum(m, jnp.max(masked[g], axis=0, keepdims=True))
        e_first = _first_index(masked[0] == m, eidx[0], float(N_EXPERTS))
        for g in range(1, N_GROUPS):
            e_first = jnp.minimum(e_first, _first_index(masked[g] == m, eidx[g], float(N_EXPERTS)))
        picked.append(e_first)
        for g in range(N_GROUPS):
            pick = eidx[g] == e_first
            chosen[g] = jnp.where(pick, 1.0, chosen[g])
            wsel[g] = jnp.where(pick, scores[g], wsel[g])
            masked[g] = jnp.where(pick, neg, masked[g])

    denom = jnp.sum(wsel[0], axis=0, keepdims=True)
    for g in range(1, N_GROUPS):
        denom = denom + jnp.sum(wsel[g], axis=0, keepdims=True)

    c_all = jnp.concatenate(chosen, axis=0)
    rank = jnp.dot(c_all.astype(BF16), tri_ref[...], preferred_element_type=F32)
    counts = jnp.sum(c_all, axis=1, keepdims=True)
    padded = jnp.ceil(counts / SUBLANES) * SUBLANES
    below = (lax.broadcasted_iota(jnp.int32, (N_EXPERTS, N_EXPERTS), 1)
             < lax.broadcasted_iota(jnp.int32, (N_EXPERTS, N_EXPERTS), 0)).astype(F32)
    seg = jnp.dot(below, jnp.broadcast_to(padded, (N_EXPERTS, LANES)), preferred_element_type=F32,
                  precision=lax.Precision.HIGHEST)
    row_of = seg[:, :1] + rank

    dest_rows, w_rows = [], []
    for k in range(TOP_K):
        d_k = jnp.zeros((1, t), F32)
        w_k = jnp.zeros((1, t), F32)
        for g in range(N_GROUPS):
            hit = eidx[g] == picked[k]
            d_k = d_k + jnp.sum(jnp.where(hit, row_of[g * per_g:(g + 1) * per_g, :], 0.0), axis=0,
                                keepdims=True)
            w_k = w_k + jnp.sum(jnp.where(hit, wsel[g], 0.0), axis=0, keepdims=True)
        dest_rows.append(d_k)
        w_rows.append(w_k / denom * ROUTED_SCALE)
    fill = [jnp.zeros((SUBLANES - TOP_K, t), F32)]
    dest_ref[...] = jnp.concatenate(dest_rows + fill, axis=0).astype(jnp.int32)
    w_ref[...] = jnp.concatenate(w_rows + fill, axis=0)
    seg_ref[...] = seg.astype(jnp.int32)
    cnt_ref[...] = jnp.broadcast_to(counts, (N_EXPERTS, LANES)).astype(jnp.int32)


def _route(logits_t, router_bias, tile):
    ne, n = logits_t.shape
    nt = n // tile
    tri = jnp.asarray(np.triu(np.ones((tile, tile), np.float32), 1), dtype=BF16)
    per_tile = lambda rows, width: pl.BlockSpec((None, rows, width), lambda i: (i, 0, 0))
    dest, wts, seg, cnt = pl.pallas_call(
        _route_kernel,
        grid=(nt,),
        in_specs=[pl.BlockSpec((ne, tile), lambda i: (0, i)),
                  pl.BlockSpec((ne, 1), lambda i: (0, 0)),
                  pl.BlockSpec((tile, tile), lambda i: (0, 0))],
        out_specs=[per_tile(SUBLANES, tile), per_tile(SUBLANES, tile),
                   per_tile(ne, LANES), per_tile(ne, LANES)],
        out_shape=[jax.ShapeDtypeStruct((nt, SUBLANES, tile), jnp.int32),
                   jax.ShapeDtypeStruct((nt, SUBLANES, tile), F32),
                   jax.ShapeDtypeStruct((nt, ne, LANES), jnp.int32),
                   jax.ShapeDtypeStruct((nt, ne, LANES), jnp.int32)],
        compiler_params=_cparams("parallel"),
    )(logits_t, router_bias.astype(F32).reshape(ne, 1), tri)
    return seg[:, :, 0].reshape(-1), cnt[:, :, 0].reshape(-1), dest, wts


def _swiglu(xb, wgu_ref, wdn_ref):
    gu = jnp.dot(xb, wgu_ref[...], preferred_element_type=F32)
    half = wgu_ref.shape[1] // 2
    act = _silu(gu[:, :half]) * gu[:, half:]
    return jnp.dot(act.astype(BF16), wdn_ref[...], preferred_element_type=F32)


def _slabs_to_rows(p, il_ref, r):
    n = il_ref.shape[0] // r
    il_ref[...] = pltpu.bitcast(p, BF16).astype(F32)
    return jnp.concatenate([il_ref[pl.ds(2 * c + h, r, stride=n), :]
                            for h in range(2) for c in range(n // 2)], axis=1)


def _rows_to_slab_rows(v, il_ref):
    r = v.shape[0]
    n = il_ref.shape[0] // r
    for h in range(2):
        for c in range(n // 2):
            lane0 = (h * (n // 2) + c) * LANES
            il_ref[pl.ds(2 * c + h, r, stride=n), :] = v[:, lane0:lane0 + LANES]
    return il_ref[...].astype(BF16)


def _rows_to_slabs(v, il_ref):
    return pltpu.bitcast(_rows_to_slab_rows(v, il_ref), U32)


def _dest_col(half, k, u):
    return (half * SUBLANES + k) * SUBLANES + u


def _moe_kernel(start_ref, count_ref, hp_ref, hb_ref, dest_ref, wt_ref, wgu_hbm, wdn_hbm,
                sgu_ref, sdn_ref, xm_ref, g2_ref, o_ref, xy_ref, il_ref, yp_ref, stage_ref, zf_ref,
                comb_ref, wgu_buf, wdn_buf, wsem):
    t = pl.program_id(0)
    s = pl.program_id(1)
    slab = xm_ref.shape[1] // (2 * LANES)
    tile = hp_ref.shape[0] // (2 * slab)
    group_rows = SUBLANES * slab

    def weight_copies(e, slot):
        return (pltpu.make_async_copy(wgu_hbm.at[e], wgu_buf.at[slot], wsem.at[0, slot]),
                pltpu.make_async_copy(wdn_hbm.at[e], wdn_buf.at[slot], wsem.at[1, slot]))

    def slab_at(ref, first):
        return ref.at[pl.ds(pl.multiple_of(first, slab), slab), :]

    @pl.when(s == 0)
    def _first_weights():
        for p in range(2):
            for cp in weight_copies(p, p):
                cp.start()

    @pl.when(jnp.logical_and(t == 0, s == 0))
    def _clear():
        xy_ref[...] = jnp.zeros(xy_ref.shape, U32)

    @pl.when(s == 0)
    def _dispatch():
        def group(j, carry):
            for half in range(2):
                p0 = pl.multiple_of((2 * j + half) * 2 * group_rows, 2 * group_rows)
                rows = pltpu.bitcast(hp_ref[pl.ds(p0, 2 * group_rows), :], U32)
                for u in range(SUBLANES):
                    for k in range(TOP_K):
                        slab_at(xy_ref, dest_ref[j, _dest_col(half, k, u)])[...] = (
                            rows[u * slab:(u + 1) * slab, :])
            return carry

        lax.fori_loop(0, tile // (2 * SUBLANES), group, 0)

    @pl.when(s == 0)
    def _experts():
        def chunk_base(e, c):
            return pl.multiple_of((start_ref[t * N_EXPERTS + e] + c * EROWS) * slab, group_rows)

        def load_rows(base, p):
            return _slabs_to_rows(xy_ref[pl.ds(base, EROWS * slab), :], il_ref.at[p], EROWS).astype(BF16)

        def put_rows(p, base, valid):
            def put(j, c2):
                r0 = pl.multiple_of(j * group_rows, group_rows)
                xy_ref[pl.ds(base + r0, group_rows), :] = yp_ref[p, pl.ds(r0, group_rows), :]
                return c2

            lax.fori_loop(0, (jnp.clip(valid, 0, EROWS) + SUBLANES - 1) // SUBLANES, put, 0)

        def expert_pair(i, carry0):
            wslot = lax.rem(i, 2) * 2

            @pl.when(i + 1 < N_EXPERTS // 2)
            def _prefetch():
                for p in range(2):
                    for cp in weight_copies(2 * i + 2 + p, 2 - wslot + p):
                        cp.start()

            for p in range(2):
                for cp in weight_copies(2 * i + p, wslot + p):
                    cp.wait()
            counts = [count_ref[t * N_EXPERTS + 2 * i + p] for p in range(2)]
            bases = [chunk_base(2 * i + p, 0) for p in range(2)]
            xs = [load_rows(bases[p], p) for p in range(2)]
            ys = [_swiglu(xs[p], wgu_buf.at[wslot + p], wdn_buf.at[wslot + p]) for p in range(2)]
            for p in range(2):
                yp_ref[p] = _rows_to_slabs(ys[p], il_ref.at[p])
            for p in range(2):
                put_rows(p, bases[p], counts[p])

            for p in range(2):
                def more(c, carry, p=p):
                    base = chunk_base(2 * i + p, c)
                    y = _swiglu(load_rows(base, p), wgu_buf.at[wslot + p], wdn_buf.at[wslot + p])
                    yp_ref[p] = _rows_to_slabs(y, il_ref.at[p])
                    put_rows(p, base, counts[p] - c * EROWS)
                    return carry

                lax.fori_loop(1, (counts[p] + EROWS - 1) // EROWS, more, 0)
            return carry0

        lax.fori_loop(0, N_EXPERTS // 2, expert_pair, 0)

    @pl.when(s >= 1)
    def _combine():
        sb = s - 1

        shared = _swiglu(hb_ref[...], sgu_ref, sdn_ref)
        for j in range(CROWS // (2 * SUBLANES)):
            g2 = sb * (CROWS // (2 * SUBLANES)) + j
            for half in range(2):
                r0 = j * 2 * SUBLANES + half * SUBLANES
                t0 = pl.multiple_of(sb * CROWS + r0, SUBLANES)
                for k in range(TOP_K):
                    for u in range(SUBLANES):
                        first = (k * SUBLANES + u) * slab
                        stage_ref[half, first:first + slab, :] = (
                            slab_at(xy_ref, dest_ref[g2, _dest_col(half, k, u)])[...])
                rows = _slabs_to_rows(stage_ref[half], zf_ref.at[half], TOP_K * SUBLANES)
                wv = wt_ref[pl.ds(t0, SUBLANES), :]
                acc = wv[:, 0:1] * rows[0:SUBLANES, :]
                for k in range(1, TOP_K):
                    acc = acc + wv[:, k:k + 1] * rows[k * SUBLANES:(k + 1) * SUBLANES, :]
                comb_ref[r0:r0 + SUBLANES, :] = acc
        o_ref[...] = xm_ref[...] + g2_ref[...] * (comb_ref[...] + shared)


def _moe(h, h_slabs, xm, g2, logits_t, router_bias, w_gu, w_dn, s_gu, s_dn, layer, seq):
    n, d = h.shape
    slab = d // (2 * LANES)
    tile = min(MOE_TILE, seq)
    nt = n // tile
    per_b = seq // tile
    csteps = tile // CROWS
    assert seq % tile == 0 and tile % CROWS == 0
    seg_start, seg_count, dest, wts = _route(logits_t, router_bias, tile)
    dest_rows = dest.reshape(nt, SUBLANES, tile // (2 * SUBLANES), 2, SUBLANES) * slab
    dest_rows = dest_rows.transpose(0, 2, 3, 1, 4).reshape(nt, tile // (2 * SUBLANES), -1)
    rows = _moe_rows(tile)
    cst = lambda s: jnp.maximum(s - 1, 0)
    crow = pl.BlockSpec((CROWS, d), lambda t, s, a, b: (t * csteps + cst(s), 0))
    hbm = pl.BlockSpec(memory_space=pl.ANY)
    single = pl.Buffered(1)
    return pl.pallas_call(
        _moe_kernel,
        grid_spec=pltpu.PrefetchScalarGridSpec(
            num_scalar_prefetch=2,
            grid=(nt, 1 + csteps),
            in_specs=[pl.BlockSpec((tile * 2 * slab, LANES), lambda t, s, a, b: (t, 0),
                                   pipeline_mode=single),
                      crow,
                      pl.BlockSpec((None, tile // (2 * SUBLANES), 2 * SUBLANES * SUBLANES),
                                   lambda t, s, a, b: (t, 0, 0), memory_space=pltpu.SMEM),
                      pl.BlockSpec((None, tile, SUBLANES), lambda t, s, a, b: (t, 0, 0),
                                   pipeline_mode=single),
                      hbm, hbm,
                      pl.BlockSpec((None,) + s_gu.shape[1:], lambda t, s, a, b: (layer, 0, 0),
                                   pipeline_mode=single),
                      pl.BlockSpec((None,) + s_dn.shape[1:], lambda t, s, a, b: (layer, 0, 0),
                                   pipeline_mode=single),
                      crow,
                      pl.BlockSpec((None, 1, d), lambda t, s, a, b: (t // per_b, 0, 0))],
            out_specs=crow,
            scratch_shapes=[pltpu.VMEM((rows * slab, LANES), U32),
                            pltpu.VMEM((2, EROWS * 2 * slab, LANES), F32),
                            pltpu.VMEM((2, EROWS * slab, LANES), U32),
                            pltpu.VMEM((2, TOP_K * SUBLANES * slab, LANES), U32),
                            pltpu.VMEM((2, TOP_K * SUBLANES * 2 * slab, LANES), F32),
                            pltpu.VMEM((CROWS, d), F32),
                            pltpu.VMEM((4, d, 2 * EXPERT_DIM), BF16),
                            pltpu.VMEM((4, EXPERT_DIM, d), BF16),
                            pltpu.SemaphoreType.DMA((2, 4))]),
        out_shape=jax.ShapeDtypeStruct((n, d), F32),
        compiler_params=_cparams("arbitrary", "arbitrary"),
    )(seg_start, seg_count, h_slabs, h, dest_rows, jnp.swapaxes(wts, 1, 2),
      w_gu, w_dn, s_gu, s_dn, xm, g2)


def _mla_prep_kernel(x_ref, n1_ref, sc1_ref, sh1_ref, nk_ref, sck_ref, shk_ref,
                     wdq_ref, gql_ref, wuq_ref, wuqs_ref, wdown_ref, gkl_ref, wk_ref, gk_ref, wuv_ref,
                     qa_ref, qb_ref, kc_ref, ksa_ref, ksb_ref, q_ref, k_ref, v_ref):
    xn = _rms(x_ref[...])
    hq = xn * n1_ref[...] * (1.0 + sc1_ref[...]) + sh1_ref[...]
    hk = xn * nk_ref[...] * (1.0 + sck_ref[...]) + shk_ref[...]
    n_heads = q_ref.shape[1] // LANES
    half = ROPE_DIM // 2

    cq = (_rms(jnp.dot(hq.astype(BF16), wdq_ref[...], preferred_element_type=F32)) * gql_ref[...]).astype(BF16)
    q_raw = jnp.dot(cq, wuq_ref[...], preferred_element_type=F32)
    q_swp = jnp.dot(cq, wuqs_ref[...], preferred_element_type=F32)
    qa, qb = qa_ref[...], qb_ref[...]
    for h in range(n_heads):
        sl = slice(h * LANES, (h + 1) * LANES)
        t = q_raw[:, sl]
        r = lax.rsqrt(jnp.sum(t * t, axis=-1, keepdims=True) / QK_DIM + NORM_EPS)
        q_ref[:, sl] = (r * (t * qa + q_swp[:, sl] * qb)).astype(q_ref.dtype)

    down = jnp.dot(hk.astype(BF16), wdown_ref[...], preferred_element_type=F32)
    ckv = (_rms(down[:, :KV_RANK]) * gkl_ref[...]).astype(BF16)
    pe = down[:, KV_RANK:]
    ss_pe = jnp.sum(pe * pe, axis=-1, keepdims=True)
    rot = (pe * kc_ref[...] + pltpu.roll(pe, LANES - half, 1) * ksa_ref[...]
           + pltpu.roll(pe, half, 1) * ksb_ref[...])
    rot_hi = rot.astype(BF16)
    rot_lo = (rot - rot_hi.astype(F32)).astype(BF16)
    k_raw = jnp.dot(jnp.concatenate([ckv, rot_hi, rot_lo], axis=1), wk_ref[...],
                    preferred_element_type=F32)
    nope = lax.broadcasted_iota(jnp.int32, (x_ref.shape[0], LANES), 1) < NOPE_DIM
    for h in range(n_heads):
        sl = slice(h * LANES, (h + 1) * LANES)
        t = k_raw[:, sl]
        ss = jnp.sum(jnp.where(nope, t * t, 0.0), axis=-1, keepdims=True) + ss_pe
        k_ref[:, sl] = (t * lax.rsqrt(ss / QK_DIM + NORM_EPS) * gk_ref[...]).astype(k_ref.dtype)
    v_ref[...] = jnp.dot(ckv, wuv_ref[...], preferred_element_type=F32).astype(v_ref.dtype)


def _pad_heads(w, n_heads, width):
    r = w.shape[0]
    return jnp.pad(w.reshape(r, n_heads, width), ((0, 0), (0, 0), (0, LANES - width))).reshape(r, n_heads * LANES)


def _swap_rope_halves(a):
    half = ROPE_DIM // 2
    lead = a.shape[:-1]
    return jnp.concatenate([jnp.zeros(lead + (NOPE_DIM,), a.dtype), a[..., NOPE_DIM + half:QK_DIM],
                            a[..., NOPE_DIM:NOPE_DIM + half],
                            jnp.zeros(lead + (LANES - QK_DIM,), a.dtype)], axis=-1)


def _mla_prep(xf, pos, n1, sc1, sh1, nk, sck, shk, w_dq, g_ql, w_uq, g_q, w_down, g_kl, w_uk, w_uv,
              g_k, seq):
    n, d = xf.shape
    n_heads = w_uv.shape[1] // V_DIM
    q_rank = w_dq.shape[1]
    hw = n_heads * LANES
    tm = 256
    per_b = seq // tm

    half = ROPE_DIM // 2
    inv = jnp.power(ROPE_THETA, -jnp.arange(half, dtype=F32) / half)
    ang = pos.astype(F32).reshape(n, 1) * inv
    cos, sin = jnp.cos(ang), jnp.sin(ang)
    ones = jnp.ones((n, NOPE_DIM), F32)
    zeros = lambda wdt: jnp.zeros((n, wdt), F32)
    tail = LANES - QK_DIM
    pad_gain = lambda g: jnp.pad(g.reshape(1, QK_DIM), ((0, 0), (0, tail)))
    q_mul = QK_DIM ** -0.5 * LOG2E
    qa_t = jnp.concatenate([ones, cos, cos, zeros(tail)], axis=1) * pad_gain(g_q) * q_mul
    qb_t = (jnp.concatenate([zeros(NOPE_DIM), -sin, sin, zeros(tail)], axis=1)
            * _swap_rope_halves(g_q.reshape(1, QK_DIM)) * q_mul)
    gk1, gk2 = g_k[NOPE_DIM:NOPE_DIM + half], g_k[NOPE_DIM + half:]
    rest = zeros(LANES - ROPE_DIM)
    kc_t = jnp.concatenate([cos * gk1, cos * gk2, rest], axis=1)
    ksa_t = jnp.concatenate([-sin * gk2, zeros(half), rest], axis=1)
    ksb_t = jnp.concatenate([zeros(half), sin * gk1, rest], axis=1)
    gk_mask = jnp.concatenate([g_k[:NOPE_DIM], jnp.ones((ROPE_DIM,), F32),
                               jnp.zeros((tail,), F32)]).reshape(1, LANES)

    wuq_p = _pad_heads(w_uq, n_heads, QK_DIM).astype(BF16)
    wuq_s = _swap_rope_halves(w_uq.reshape(q_rank, n_heads, QK_DIM)).reshape(q_rank, hw).astype(BF16)
    wdown_p = jnp.pad(w_down, ((0, 0), (0, LANES - ROPE_DIM))).astype(BF16)
    eye = jnp.pad(jnp.eye(ROPE_DIM, dtype=F32), ((0, LANES - ROPE_DIM), (NOPE_DIM, tail)))
    pe_rows = jnp.tile(eye, (1, n_heads))
    wk = jnp.concatenate([_pad_heads(w_uk, n_heads, NOPE_DIM), pe_rows, pe_rows], axis=0).astype(BF16)

    vec = pl.BlockSpec((None, 1, d), lambda i: (i // per_b, 0, 0))
    full = lambda a: pl.BlockSpec(a.shape, lambda i: (0,) * a.ndim)
    row = lambda wdt: pl.BlockSpec((tm, wdt), lambda i: (i, 0))
    args = [xf, n1, sc1, sh1, nk, sck, shk,
            w_dq.astype(BF16), g_ql.reshape(1, q_rank), wuq_p, wuq_s,
            wdown_p, g_kl.reshape(1, KV_RANK), wk, gk_mask, w_uv.astype(BF16),
            qa_t, qb_t, kc_t, ksa_t, ksb_t]
    specs = [row(d), full(n1), vec, vec, full(nk), vec, vec] + [full(a) for a in args[7:16]] + [row(LANES)] * 5
    return pl.pallas_call(
        _mla_prep_kernel,
        grid=(n // tm,),
        in_specs=specs,
        out_specs=[row(hw), row(hw), row(n_heads * V_DIM)],
        out_shape=[jax.ShapeDtypeStruct((n, hw), BF16), jax.ShapeDtypeStruct((n, hw), BF16),
                   jax.ShapeDtypeStruct((n, n_heads * V_DIM), BF16)],
        compiler_params=_cparams("parallel"),
    )(*args)


def _causal_bias(seq):
    a = np.arange(QBLK)[:, None]
    c = np.arange(seq)[None, :]
    return np.where(c <= (seq - QBLK) + a, 0.0, MASKED).astype(np.float32)


def _mla_attn_kernel(q_ref, k_ref, v_ref, mask_ref, wa_ref, wb_ref, o_ref, wa_out, wb_out,
                     s_ref, p_ref):
    wa_out[...] = wa_ref[...].astype(BF16)
    wb_out[...] = wb_ref[...].astype(BF16)
    seq = q_ref.shape[0]
    lane_lo = lax.broadcasted_iota(jnp.int32, (QBLK, LANES), 1) < V_DIM

    def qk_of(blk, h, w):
        lanes = slice(h * LANES, (h + 1) * LANES)
        return q_ref[w - QBLK:w, lanes], k_ref[0:w, lanes]

    _attention_blocks(seq, qk_of, v_ref, mask_ref, s_ref, p_ref, o_ref, lane_lo)


def _mla_attention(q, k, v, cast_weights, layer):
    bsz, seq, hw = q.shape
    n_pairs = hw // (2 * LANES)
    qk = pl.BlockSpec((None, seq, 2 * LANES), lambda b, h: (b, 0, h))
    vo = pl.BlockSpec((None, seq, LANES), lambda b, h: (b, 0, h))
    grid = (bsz, n_pairs)
    w_in, w_out, w_shapes = _cast_rider(cast_weights, layer, grid)
    return pl.pallas_call(
        _mla_attn_kernel,
        grid=grid,
        in_specs=[qk, qk, vo, pl.BlockSpec((QBLK, seq), lambda b, h: (0, 0))] + w_in,
        out_specs=[vo] + w_out,
        scratch_shapes=_attn_scratch(seq),
        out_shape=[jax.ShapeDtypeStruct(v.shape, BF16)] + w_shapes,
        compiler_params=_cparams("parallel", "parallel"),
    )(q, k, v, jnp.asarray(_causal_bias(seq)), *cast_weights)


def kernel(x, c, pos, ada_w, ada_b, norm1_g, norm2_g, a_w_qkv, a_q_norm, a_k_norm, a_w_o, kv_ada_w, kv_ada_b, kv_norm_g, kv_w_down, kv_latent_norm, kv_w_uk, kv_w_uv, kv_k_norm, b_w_dq, b_q_latent_norm, b_w_uq, b_q_norm, b_w_o, moe_w_router, moe_router_bias, moe_w_gate_up, moe_w_down, moe_shared_gate_up, moe_shared_down):
    bsz, seq, d = x.shape
    n = bsz * seq
    depth = ada_w.shape[0]
    n_a = a_w_qkv.shape[0]
    assert seq % QBLK == 0
    xf = x.reshape(n, d)

    expert_weights = (moe_w_gate_up, moe_w_down)
    s_gu = moe_shared_gate_up.astype(BF16)
    s_dn = moe_shared_down.astype(BF16)
    row = lambda g: g.reshape(1, d)

    for layer in range(depth):
        mod = _modulation(c, ada_w, ada_b, layer).reshape(bsz, 6, 1, d)
        sh1, sc1, g1, sh2, sc2, g2 = [mod[:, i] for i in range(6)]
        if layer < n_a:
            qkv = _qkv_proj(xf, row(norm1_g[layer]), sc1, sh1, a_w_qkv[layer].astype(BF16), seq)
            o, w_gu, w_dn = _dilated_attention(qkv.reshape(bsz, seq, -1), pos, a_q_norm[layer],
                                               a_k_norm[layer], expert_weights, layer)
            w_o = a_w_o[layer]
        else:
            j = layer - n_a
            kmod = _modulation(c, kv_ada_w[None], kv_ada_b[None], 0).reshape(bsz, 2, 1, d)
            q, k, v = _mla_prep(xf, pos, row(norm1_g[layer]), sc1, sh1, row(kv_norm_g), kmod[:, 1],
                                kmod[:, 0], b_w_dq[j], b_q_latent_norm[j], b_w_uq[j], b_q_norm[j],
                                kv_w_down, kv_latent_norm, kv_w_uk, kv_w_uv, kv_k_norm, seq)
            o, w_gu, w_dn = _mla_attention(q.reshape(bsz, seq, -1), k.reshape(bsz, seq, -1),
                                           v.reshape(bsz, seq, -1), expert_weights, layer)
            w_o = b_w_o[j]
        xm, h2, h2_slabs, logits = _post_attn(o.reshape(n, d), w_o.astype(BF16), xf, g1,
                                              row(norm2_g[layer]), sc2, sh2, moe_w_router[layer], seq)
        xf = _moe(h2, h2_slabs, xm, g2, logits, moe_router_bias[layer], w_gu, w_dn, s_gu, s_dn,
                  layer, seq)
    return xf.reshape(bsz, seq, d)
```

```python
import functools
import math

import numpy as np
import jax
import jax.numpy as jnp
from jax import lax
from jax.experimental import pallas as pl
from jax.experimental.pallas import tpu as pltpu

F32 = jnp.float32
BF16 = jnp.bfloat16
U32 = jnp.uint32

NORM_EPS = 1e-6
MASKED = -1e30
LOG2E = math.log2(math.e)

LANES = 128
SUBLANES = 8
VMEM_LIMIT = 56 * 1024 * 1024

A_HEAD_DIM = 64
DILATED_PAIRS = ((128, 1), (512, 4), (2048, 16))
QBLK = 256
NOPE_DIM = 64
ROPE_DIM = 32
QK_DIM = NOPE_DIM + ROPE_DIM
V_DIM = 64
KV_RANK = 256
ROPE_THETA = 10000.0
N_EXPERTS = 64
N_GROUPS = 8
TOPK_GROUPS = 4
TOP_K = 6
EXPERT_DIM = 256
ROUTED_SCALE = 2.5
MOE_TILE = 2048
EROWS = 256
CROWS = 256

ROW_TILE = 512


def _cparams(*sem):
    return pltpu.CompilerParams(dimension_semantics=sem, vmem_limit_bytes=VMEM_LIMIT)


def _silu(v):
    return v * jax.nn.sigmoid(v)


def _rms(v):
    return v * lax.rsqrt(jnp.mean(v * v, axis=-1, keepdims=True) + NORM_EPS)


def _dot_nt(a, b):
    return lax.dot_general(a, b, (((1,), (1,)), ((), ())), preferred_element_type=F32)


def _mod_kernel(c_ref, w_ref, b_ref, o_ref):
    cond = _silu(c_ref[...])
    o_ref[...] = jnp.dot(cond, w_ref[...], preferred_element_type=F32,
                         precision=lax.Precision.HIGHEST) + b_ref[...]


def _modulation(c, w, b, layer):
    bsz, d = c.shape
    m = w.shape[-1]
    tn = 1024
    return pl.pallas_call(
        _mod_kernel,
        grid=(m // tn,),
        in_specs=[pl.BlockSpec((bsz, d), lambda j: (0, 0)),
                  pl.BlockSpec((None, d, tn), lambda j: (layer, 0, j)),
                  pl.BlockSpec((None, 1, tn), lambda j: (layer, 0, j))],
        out_specs=pl.BlockSpec((bsz, tn), lambda j: (0, j)),
        out_shape=jax.ShapeDtypeStruct((bsz, m), F32),
        compiler_params=_cparams("arbitrary"),
    )(c, w, b.reshape(b.shape[0], 1, m))


def _qkv_kernel(x_ref, g_ref, sc_ref, sh_ref, w_ref, o_ref):
    h = _rms(x_ref[...]) * g_ref[...]
    h = h * (1.0 + sc_ref[...]) + sh_ref[...]
    o_ref[...] = jnp.dot(h.astype(BF16), w_ref[...], preferred_element_type=F32).astype(o_ref.dtype)


def _qkv_proj(xf, g, sc, sh, w, seq):
    n, d = xf.shape
    m = w.shape[1]
    tm = min(ROW_TILE, seq)
    per_b = seq // tm
    vec = pl.BlockSpec((None, 1, d), lambda i: (i // per_b, 0, 0))
    return pl.pallas_call(
        _qkv_kernel,
        grid=(n // tm,),
        in_specs=[pl.BlockSpec((tm, d), lambda i: (i, 0)),
                  pl.BlockSpec((1, d), lambda i: (0, 0)),
                  vec, vec,
                  pl.BlockSpec((d, m), lambda i: (0, 0))],
        out_specs=pl.BlockSpec((tm, m), lambda i: (i, 0)),
        out_shape=jax.ShapeDtypeStruct((n, m), BF16),
        compiler_params=_cparams("parallel"),
    )(xf, g, sc, sh, w)


ATT_SLOTS = 4


def _attention_blocks(seq, qk_of, v_ref, bias_ref, s_ref, p_ref, o_ref, lane_lo):
    nblk = seq // QBLK

    def slot(blk, h):
        return (blk % 2) * 2 + h

    def scores(blk):
        w = (blk + 1) * QBLK
        bias = bias_ref[:, seq - w:seq]
        for h in range(2):
            q, k = qk_of(blk, h, w)
            s_ref[slot(blk, h), :, 0:w] = _dot_nt(q, k) + bias

    for blk in range(nblk):
        w = (blk + 1) * QBLK
        scores(blk)
        denom = []
        for h in range(2):
            s = s_ref[slot(blk, h), :, 0:w]
            p = jnp.exp2(s - jnp.max(s, axis=-1, keepdims=True))
            denom.append(jnp.sum(p, axis=-1, keepdims=True))
            p_ref[slot(blk, h), :, 0:w] = p.astype(BF16)
        o_a, o_b = [jnp.dot(p_ref[slot(blk, h), :, 0:w], v_ref[0:w, :],
                            preferred_element_type=F32) / denom[h] for h in range(2)]
        o_ref[blk * QBLK:w, :] = jnp.where(lane_lo, o_a, o_b).astype(o_ref.dtype)


def _attn_scratch(seq):
    return [pltpu.VMEM((ATT_SLOTS, QBLK, seq), F32), pltpu.VMEM((ATT_SLOTS, QBLK, seq), BF16)]


def _cast_rider(weights, layer, grid):
    steps = grid[0] * grid[1]
    in_specs, out_specs, out_shapes = [], [], []
    for w in weights:
        n_exp = w.shape[1]
        assert n_exp % steps == 0
        per = n_exp // steps
        blk = (per,) + w.shape[2:]
        in_specs.append(pl.BlockSpec((None,) + blk, lambda b, h: (layer, b * grid[1] + h, 0, 0)))
        out_specs.append(pl.BlockSpec(blk, lambda b, h: (b * grid[1] + h, 0, 0)))
        out_shapes.append(jax.ShapeDtypeStruct(w.shape[1:], BF16))
    return in_specs, out_specs, out_shapes


def _split3(v):
    p1 = v.astype(BF16).astype(F32)
    p2 = (v - p1).astype(BF16).astype(F32)
    return p1, p2, v - p1 - p2


def _dilated_log_multiplicity(seq):
    a = np.arange(QBLK)[:, None]
    c = np.arange(seq)[None, :]
    d = (seq - QBLK) + a - c
    mult = np.zeros((QBLK, seq), np.int32)
    for win, dil in DILATED_PAIRS:
        mult += ((d >= 0) & (d % dil == 0) & (d <= win)).astype(np.int32)
    return np.where(mult > 0, np.log2(np.maximum(mult, 1)), MASKED).astype(np.float32)


N_FEAT = 9


def _dil_attn_kernel(q_ref, k_ref, v_ref, posc_ref, sfeat_ref, gq_ref, gk_ref, logm_ref,
                     wa_ref, wb_ref, o_ref, wa_out, wb_out, qa_ref, qb_ref, ks_ref, s_ref, p_ref):
    wa_out[...] = wa_ref[...].astype(BF16)
    wb_out[...] = wb_ref[...].astype(BF16)
    seq = q_ref.shape[0]
    lane = lax.broadcasted_iota(jnp.int32, (QBLK, LANES), 1)
    lane_lo = lane < A_HEAD_DIM
    piece = [functools.reduce(jnp.logical_or, [lane == i + 3 * j for j in range(N_FEAT // 3)])
             for i in range(3)]

    def head_norm(ref, r0, g_ref, mul):
        t = ref[r0:r0 + QBLK, :].astype(F32)
        sq = t * t
        ssa = jnp.sum(jnp.where(lane_lo, sq, 0.0), axis=-1, keepdims=True)
        ssb = jnp.sum(jnp.where(lane_lo, 0.0, sq), axis=-1, keepdims=True)
        r = jnp.where(lane_lo, lax.rsqrt(ssa / A_HEAD_DIM + NORM_EPS),
                      lax.rsqrt(ssb / A_HEAD_DIM + NORM_EPS))
        return t * r * (g_ref[...] * mul)

    rel0 = posc_ref[0:1, :]
    for r0 in range(0, seq, QBLK):
        rows = slice(r0, r0 + QBLK)
        qn = head_norm(q_ref, r0, gq_ref, A_HEAD_DIM ** -0.5 * LOG2E)
        qa_ref[rows, :LANES] = jnp.where(lane_lo, qn, 0.0).astype(BF16)
        qb_ref[rows, :LANES] = jnp.where(lane_lo, 0.0, qn).astype(BF16)
        qa_ref[rows, LANES:] = jnp.broadcast_to(sfeat_ref[0:1, :], (QBLK, LANES)).astype(BF16)
        qb_ref[rows, LANES:] = jnp.broadcast_to(sfeat_ref[1:2, :], (QBLK, LANES)).astype(BF16)
        ks_ref[rows, :LANES] = head_norm(k_ref, r0, gk_ref, 1.0).astype(BF16)
        p1, p2, p3 = _split3((posc_ref[rows, :] - rel0).astype(F32))
        feat = jnp.where(piece[0], p1, jnp.where(piece[1], p2, jnp.where(piece[2], p3, 0.0)))
        ks_ref[rows, LANES:] = feat.astype(BF16)

    def qk_of(blk, h, w):
        return (qa_ref, qb_ref)[h][w - QBLK:w, :], ks_ref[0:w, :]

    _attention_blocks(seq, qk_of, v_ref, logm_ref, s_ref, p_ref, o_ref, lane_lo)


def _dilated_attention(qkv, pos, q_norm, k_norm, cast_weights, layer):
    bsz, seq, three_d = qkv.shape
    d = three_d // 3
    n_pairs = d // LANES
    n_heads = d // A_HEAD_DIM
    slopes = jnp.exp2(-8.0 * jnp.arange(1, n_heads + 1, dtype=F32) / n_heads) * LOG2E
    sfeat = jnp.repeat(jnp.stack(_split3(slopes), axis=1), 3, axis=1)
    sfeat = jnp.pad(sfeat, ((0, 0), (0, LANES - N_FEAT))).reshape(n_pairs, 2, LANES)
    logm = jnp.asarray(_dilated_log_multiplicity(seq))
    gq = jnp.tile(q_norm.reshape(1, A_HEAD_DIM), (1, 2))
    gk = jnp.tile(k_norm.reshape(1, A_HEAD_DIM), (1, 2))
    blk = lambda off: pl.BlockSpec((None, seq, LANES), lambda b, h: (b, 0, off + h))
    const2 = pl.BlockSpec((1, LANES), lambda b, h: (0, 0))
    ext = pltpu.VMEM((seq, 2 * LANES), BF16)
    grid = (bsz, n_pairs)
    w_in, w_out, w_shapes = _cast_rider(cast_weights, layer, grid)
    return pl.pallas_call(
        _dil_attn_kernel,
        grid=grid,
        in_specs=[blk(0), blk(n_pairs), blk(2 * n_pairs),
                  pl.BlockSpec((None, seq, 1), lambda b, h: (b, 0, 0)),
                  pl.BlockSpec((None, 2, LANES), lambda b, h: (h, 0, 0)),
                  const2, const2,
                  pl.BlockSpec((QBLK, seq), lambda b, h: (0, 0))] + w_in,
        out_specs=[pl.BlockSpec((None, seq, LANES), lambda b, h: (b, 0, h))] + w_out,
        scratch_shapes=[ext, ext, ext] + _attn_scratch(seq),
        out_shape=[jax.ShapeDtypeStruct((bsz, seq, d), BF16)] + w_shapes,
        compiler_params=_cparams("parallel", "parallel"),
    )(qkv, qkv, qkv, pos.reshape(bsz, seq, 1), sfeat, gq, gk, logm, *cast_weights)


def _post_attn_kernel(o_ref, wo_ref, x_ref, g1_ref, n2_ref, sc_ref, sh_ref, wrh_ref, wrl_ref,
                      xm_ref, h_ref, hs_ref, lg_ref, il_ref):
    y = jnp.dot(o_ref[...], wo_ref[...], preferred_element_type=F32)
    xm = x_ref[...] + g1_ref[...] * y
    xm_ref[...] = xm
    h = _rms(xm) * n2_ref[...]
    h = h * (1.0 + sc_ref[...]) + sh_ref[...]
    h_ref[...] = h.astype(h_ref.dtype)
    hs_ref[...] = _rows_to_slab_rows(h, il_ref)
    h_hi = h.astype(BF16)
    h_lo = (h - h_hi.astype(F32)).astype(BF16)
    lg_ref[...] = (_dot_nt(wrh_ref[...], h_hi) + _dot_nt(wrh_ref[...], h_lo)
                   + _dot_nt(wrl_ref[...], h_hi))


def _post_attn(o, w_o, xf, g1, n2, sc2, sh2, w_router, seq):
    n, d = xf.shape
    ne = w_router.shape[1]
    tm = min(ROW_TILE, seq)
    per_b = seq // tm
    slab_rows = d // LANES
    wr_t = w_router.T.astype(F32)
    wr_hi = wr_t.astype(BF16)
    vec = pl.BlockSpec((None, 1, d), lambda i: (i // per_b, 0, 0))
    row = lambda w: pl.BlockSpec((tm, w), lambda i: (i, 0))
    return pl.pallas_call(
        _post_attn_kernel,
        grid=(n // tm,),
        in_specs=[row(d), pl.BlockSpec((d, d), lambda i: (0, 0)), row(d), vec,
                  pl.BlockSpec((1, d), lambda i: (0, 0)), vec, vec,
                  pl.BlockSpec((ne, d), lambda i: (0, 0)), pl.BlockSpec((ne, d), lambda i: (0, 0))],
        out_specs=[row(d), row(d), pl.BlockSpec((tm * slab_rows, LANES), lambda i: (i, 0)),
                   pl.BlockSpec((ne, tm), lambda i: (0, i))],
        out_shape=[jax.ShapeDtypeStruct((n, d), F32), jax.ShapeDtypeStruct((n, d), BF16),
                   jax.ShapeDtypeStruct((n * slab_rows, LANES), BF16),
                   jax.ShapeDtypeStruct((ne, n), F32)],
        scratch_shapes=[pltpu.VMEM((tm * slab_rows, LANES), F32)],
        compiler_params=_cparams("parallel"),
    )(o, w_o, xf, g1, n2, sc2, sh2, wr_hi, (wr_t - wr_hi.astype(F32)).astype(BF16))


def _moe_rows(tile):
    rows = tile * TOP_K + N_EXPERTS * (SUBLANES - 1) + EROWS
    return -(-rows // SUBLANES) * SUBLANES


def _first_index(hit, index, limit):
    return jnp.min(jnp.where(hit, index, limit), axis=0, keepdims=True)


def _route_kernel(lg_ref, bias_ref, tri_ref, dest_ref, w_ref, seg_ref, cnt_ref):
    t = lg_ref.shape[1]
    per_g = N_EXPERTS // N_GROUPS
    sub = lax.broadcasted_iota(jnp.int32, (per_g, t), 0).astype(F32)
    neg = -jnp.inf
    scores, sel = [], []
    for g in range(N_GROUPS):
        rows = slice(g * per_g, (g + 1) * per_g)
        sc = jax.nn.sigmoid(lg_ref[rows, :])
        scores.append(sc)
        sel.append(sc + bias_ref[rows, :])

    gs = []
    for g in range(N_GROUPS):
        m1 = jnp.max(sel[g], axis=0, keepdims=True)
        first = _first_index(sel[g] == m1, sub, float(per_g))
        m2 = jnp.max(jnp.where(sub == first, neg, sel[g]), axis=0, keepdims=True)
        gs.append(m1 + m2)
    gsc = jnp.concatenate(gs, axis=0)
    gsub = lax.broadcasted_iota(jnp.int32, (N_GROUPS, t), 0).astype(F32)
    gkeep = jnp.zeros((N_GROUPS, t), F32)
    for _ in range(TOPK_GROUPS):
        m = jnp.max(gsc, axis=0, keepdims=True)
        pick = gsub == _first_index(gsc == m, gsub, float(N_GROUPS))
        gkeep = jnp.where(pick, 1.0, gkeep)
        gsc = jnp.where(pick, neg, gsc)

    eidx = [sub + float(g * per_g) for g in range(N_GROUPS)]
    masked = [jnp.where(jnp.broadcast_to(gkeep[g:g + 1, :], (per_g, t)) > 0.5, sel[g], neg)
              for g in range(N_GROUPS)]
    chosen = [jnp.zeros((per_g, t), F32) for _ in range(N_GROUPS)]
    wsel = [jnp.zeros((per_g, t), F32) for _ in range(N_GROUPS)]
    picked = []
    for _ in range(TOP_K):
        m = jnp.max(masked[0], axis=0, keepdims=True)
        for g in range(1, N_GROUPS):
            m = jnp.maximum(m, jnp.max(masked[g], axis=0, keepdims=True))
        e_first = _first_index(masked[0] == m, eidx[0], float(N_EXPERTS))
        for g in range(1, N_GROUPS):
            e_first = jnp.minimum(e_first, _first_index(masked[g] == m, eidx[g], float(N_EXPERTS)))
        picked.append(e_first)
        for g in range(N_GROUPS):
            pick = eidx[g] == e_first
            chosen[g] = jnp.where(pick, 1.0, chosen[g])
            wsel[g] = jnp.where(pick, scores[g], wsel[g])
            masked[g] = jnp.where(pick, neg, masked[g])

    denom = jnp.sum(wsel[0], axis=0, keepdims=True)
    for g in range(1, N_GROUPS):
        denom = denom + jnp.sum(wsel[g], axis=0, keepdims=True)

    c_all = jnp.concatenate(chosen, axis=0)
    rank = jnp.dot(c_all.astype(BF16), tri_ref[...], preferred_element_type=F32)
    counts = jnp.sum(c_all, axis=1, keepdims=True)
    padded = jnp.ceil(counts / SUBLANES) * SUBLANES
    below = (lax.broadcasted_iota(jnp.int32, (N_EXPERTS, N_EXPERTS), 1)
             < lax.broadcasted_iota(jnp.int32, (N_EXPERTS, N_EXPERTS), 0)).astype(F32)
    seg = jnp.dot(below, jnp.broadcast_to(padded, (N_EXPERTS, LANES)), preferred_element_type=F32,
                  precision=lax.Precision.HIGHEST)
    row_of = seg[:, :1] + rank

    dest_rows, w_rows = [], []
    for k in range(TOP_K):
        d_k = jnp.zeros((1, t), F32)
        w_k = jnp.zeros((1, t), F32)
        for g in range(N_GROUPS):
            hit = eidx[g] == picked[k]
            d_k = d_k + jnp.sum(jnp.where(hit, row_of[g * per_g:(g + 1) * per_g, :], 0.0), axis=0,
                                keepdims=True)
            w_k = w_k + jnp.sum(jnp.where(hit, wsel[g], 0.0), axis=0, keepdims=True)
        dest_rows.append(d_k)
        w_rows.append(w_k / denom * ROUTED_SCALE)
    fill = [jnp.zeros((SUBLANES - TOP_K, t), F32)]
    dest_ref[...] = jnp.concatenate(dest_rows + fill, axis=0).astype(jnp.int32)
    w_ref[...] = jnp.concatenate(w_rows + fill, axis=0)
    seg_ref[...] = seg.astype(jnp.int32)
    cnt_ref[...] = jnp.broadcast_to(counts, (N_EXPERTS, LANES)).astype(jnp.int32)


def _route(logits_t, router_bias, tile):
    ne, n = logits_t.shape
    nt = n // tile
    tri = jnp.asarray(np.triu(np.ones((tile, tile), np.float32), 1), dtype=BF16)
    per_tile = lambda rows, width: pl.BlockSpec((None, rows, width), lambda i: (i, 0, 0))
    dest, wts, seg, cnt = pl.pallas_call(
        _route_kernel,
        grid=(nt,),
        in_specs=[pl.BlockSpec((ne, tile), lambda i: (0, i)),
                  pl.BlockSpec((ne, 1), lambda i: (0, 0)),
                  pl.BlockSpec((tile, tile), lambda i: (0, 0))],
        out_specs=[per_tile(SUBLANES, tile), per_tile(SUBLANES, tile),
                   per_tile(ne, LANES), per_tile(ne, LANES)],
        out_shape=[jax.ShapeDtypeStruct((nt, SUBLANES, tile), jnp.int32),
                   jax.ShapeDtypeStruct((nt, SUBLANES, tile), F32),
                   jax.ShapeDtypeStruct((nt, ne, LANES), jnp.int32),
                   jax.ShapeDtypeStruct((nt, ne, LANES), jnp.int32)],
        compiler_params=_cparams("parallel"),
    )(logits_t, router_bias.astype(F32).reshape(ne, 1), tri)
    return seg[:, :, 0].reshape(-1), cnt[:, :, 0].reshape(-1), dest, wts


def _swiglu(xb, wgu_ref, wdn_ref):
    gu = jnp.dot(xb, wgu_ref[...], preferred_element_type=F32)
    half = wgu_ref.shape[1] // 2
    act = _silu(gu[:, :half]) * gu[:, half:]
    return jnp.dot(act.astype(BF16), wdn_ref[...], preferred_element_type=F32)


def _slabs_to_rows(p, il_ref, r):
    n = il_ref.shape[0] // r
    il_ref[...] = pltpu.bitcast(p, BF16).astype(F32)
    return jnp.concatenate([il_ref[pl.ds(2 * c + h, r, stride=n), :]
                            for h in range(2) for c in range(n // 2)], axis=1)


def _rows_to_slab_rows(v, il_ref):
    r = v.shape[0]
    n = il_ref.shape[0] // r
    for h in range(2):
        for c in range(n // 2):
            lane0 = (h * (n // 2) + c) * LANES
            il_ref[pl.ds(2 * c + h, r, stride=n), :] = v[:, lane0:lane0 + LANES]
    return il_ref[...].astype(BF16)


def _rows_to_slabs(v, il_ref):
    return pltpu.bitcast(_rows_to_slab_rows(v, il_ref), U32)


def _dest_col(half, k, u):
    return (half * SUBLANES + k) * SUBLANES + u


def _moe_kernel(start_ref, count_ref, hp_ref, hb_ref, dest_ref, wt_ref, wgu_hbm, wdn_hbm,
                sgu_ref, sdn_ref, xm_ref, g2_ref, o_ref, xy_ref, il_ref, yp_ref, stage_ref, zf_ref,
                comb_ref, wgu_buf, wdn_buf, wsem):
    t = pl.program_id(0)
    s = pl.program_id(1)
    slab = xm_ref.shape[1] // (2 * LANES)
    tile = hp_ref.shape[0] // (2 * slab)
    group_rows = SUBLANES * slab

    def weight_copies(e, slot):
        return (pltpu.make_async_copy(wgu_hbm.at[e], wgu_buf.at[slot], wsem.at[0, slot]),
                pltpu.make_async_copy(wdn_hbm.at[e], wdn_buf.at[slot], wsem.at[1, slot]))

    def slab_at(ref, first):
        return ref.at[pl.ds(pl.multiple_of(first, slab), slab), :]

    @pl.when(s == 0)
    def _first_weights():
        for p in range(2):
            for cp in weight_copies(p, p):
                cp.start()

    @pl.when(jnp.logical_and(t == 0, s == 0))
    def _clear():
        xy_ref[...] = jnp.zeros(xy_ref.shape, U32)

    @pl.when(s == 0)
    def _dispatch():
        def group(j, carry):
            for half in range(2):
                p0 = pl.multiple_of((2 * j + half) * 2 * group_rows, 2 * group_rows)
                rows = pltpu.bitcast(hp_ref[pl.ds(p0, 2 * group_rows), :], U32)
                for u in range(SUBLANES):
                    for k in range(TOP_K):
                        slab_at(xy_ref, dest_ref[j, _dest_col(half, k, u)])[...] = (
                            rows[u * slab:(u + 1) * slab, :])
            return carry

        lax.fori_loop(0, tile // (2 * SUBLANES), group, 0)

    @pl.when(s == 0)
    def _experts():
        def chunk_base(e, c):
            return pl.multiple_of((start_ref[t * N_EXPERTS + e] + c * EROWS) * slab, group_rows)

        def load_rows(base, p):
            return _slabs_to_rows(xy_ref[pl.ds(base, EROWS * slab), :], il_ref.at[p], EROWS).astype(BF16)

        def put_rows(p, base, valid):
            def put(j, c2):
                r0 = pl.multiple_of(j * group_rows, group_rows)
                xy_ref[pl.ds(base + r0, group_rows), :] = yp_ref[p, pl.ds(r0, group_rows), :]
                return c2

            lax.fori_loop(0, (jnp.clip(valid, 0, EROWS) + SUBLANES - 1) // SUBLANES, put, 0)

        def expert_pair(i, carry0):
            wslot = lax.rem(i, 2) * 2

            @pl.when(i + 1 < N_EXPERTS // 2)
            def _prefetch():
                for p in range(2):
                    for cp in weight_copies(2 * i + 2 + p, 2 - wslot + p):
                        cp.start()

            for p in range(2):
                for cp in weight_copies(2 * i + p, wslot + p):
                    cp.wait()
            counts = [count_ref[t * N_EXPERTS + 2 * i + p] for p in range(2)]
            bases = [chunk_base(2 * i + p, 0) for p in range(2)]
            xs = [load_rows(bases[p], p) for p in range(2)]
            ys = [_swiglu(xs[p], wgu_buf.at[wslot + p], wdn_buf.at[wslot + p]) for p in range(2)]
            for p in range(2):
                yp_ref[p] = _rows_to_slabs(ys[p], il_ref.at[p])
            for p in range(2):
                put_rows(p, bases[p], counts[p])

            for p in range(2):
                def more(c, carry, p=p):
                    base = chunk_base(2 * i + p, c)
                    y = _swiglu(load_rows(base, p), wgu_buf.at[wslot + p], wdn_buf.at[wslot + p])
                    yp_ref[p] = _rows_to_slabs(y, il_ref.at[p])
                    put_rows(p, base, counts[p] - c * EROWS)
                    return carry

                lax.fori_loop(1, (counts[p] + EROWS - 1) // EROWS, more, 0)
            return carry0

        lax.fori_loop(0, N_EXPERTS // 2, expert_pair, 0)

    @pl.when(s >= 1)
    def _combine():
        sb = s - 1

        shared = _swiglu(hb_ref[...], sgu_ref, sdn_ref)
        for j in range(CROWS // (2 * SUBLANES)):
            g2 = sb * (CROWS // (2 * SUBLANES)) + j
            for half in range(2):
                r0 = j * 2 * SUBLANES + half * SUBLANES
                t0 = pl.multiple_of(sb * CROWS + r0, SUBLANES)
                for k in range(TOP_K):
                    for u in range(SUBLANES):
                        first = (k * SUBLANES + u) * slab
                        stage_ref[half, first:first + slab, :] = (
                            slab_at(xy_ref, dest_ref[g2, _dest_col(half, k, u)])[...])
                rows = _slabs_to_rows(stage_ref[half], zf_ref.at[half], TOP_K * SUBLANES)
                wv = wt_ref[pl.ds(t0, SUBLANES), :]
                acc = wv[:, 0:1] * rows[0:SUBLANES, :]
                for k in range(1, TOP_K):
                    acc = acc + wv[:, k:k + 1] * rows[k * SUBLANES:(k + 1) * SUBLANES, :]
                comb_ref[r0:r0 + SUBLANES, :] = acc
        o_ref[...] = xm_ref[...] + g2_ref[...] * (comb_ref[...] + shared)


def _moe(h, h_slabs, xm, g2, logits_t, router_bias, w_gu, w_dn, s_gu, s_dn, layer, seq):
    n, d = h.shape
    slab = d // (2 * LANES)
    tile = min(MOE_TILE, seq)
    nt = n // tile
    per_b = seq // tile
    csteps = tile // CROWS
    assert seq % tile == 0 and tile % CROWS == 0
    seg_start, seg_count, dest, wts = _route(logits_t, router_bias, tile)
    dest_rows = dest.reshape(nt, SUBLANES, tile // (2 * SUBLANES), 2, SUBLANES) * slab
    dest_rows = dest_rows.transpose(0, 2, 3, 1, 4).reshape(nt, tile // (2 * SUBLANES), -1)
    rows = _moe_rows(tile)
    cst = lambda s: jnp.maximum(s - 1, 0)
    crow = pl.BlockSpec((CROWS, d), lambda t, s, a, b: (t * csteps + cst(s), 0))
    hbm = pl.BlockSpec(memory_space=pl.ANY)
    single = pl.Buffered(1)
    return pl.pallas_call(
        _moe_kernel,
        grid_spec=pltpu.PrefetchScalarGridSpec(
            num_scalar_prefetch=2,
            grid=(nt, 1 + csteps),
            in_specs=[pl.BlockSpec((tile * 2 * slab, LANES), lambda t, s, a, b: (t, 0),
                                   pipeline_mode=single),
                      crow,
                      pl.BlockSpec((None, tile // (2 * SUBLANES), 2 * SUBLANES * SUBLANES),
                                   lambda t, s, a, b: (t, 0, 0), memory_space=pltpu.SMEM),
                      pl.BlockSpec((None, tile, SUBLANES), lambda t, s, a, b: (t, 0, 0),
                                   pipeline_mode=single),
                      hbm, hbm,
                      pl.BlockSpec((None,) + s_gu.shape[1:], lambda t, s, a, b: (layer, 0, 0),
                                   pipeline_mode=single),
                      pl.BlockSpec((None,) + s_dn.shape[1:], lambda t, s, a, b: (layer, 0, 0),
                                   pipeline_mode=single),
                      crow,
                      pl.BlockSpec((None, 1, d), lambda t, s, a, b: (t // per_b, 0, 0))],
            out_specs=crow,
            scratch_shapes=[pltpu.VMEM((rows * slab, LANES), U32),
                            pltpu.VMEM((2, EROWS * 2 * slab, LANES), F32),
                            pltpu.VMEM((2, EROWS * slab, LANES), U32),
                            pltpu.VMEM((2, TOP_K * SUBLANES * slab, LANES), U32),
                            pltpu.VMEM((2, TOP_K * SUBLANES * 2 * slab, LANES), F32),
                            pltpu.VMEM((CROWS, d), F32),
                            pltpu.VMEM((4, d, 2 * EXPERT_DIM), BF16),
                            pltpu.VMEM((4, EXPERT_DIM, d), BF16),
                            pltpu.SemaphoreType.DMA((2, 4))]),
        out_shape=jax.ShapeDtypeStruct((n, d), F32),
        compiler_params=_cparams("arbitrary", "arbitrary"),
    )(seg_start, seg_count, h_slabs, h, dest_rows, jnp.swapaxes(wts, 1, 2),
      w_gu, w_dn, s_gu, s_dn, xm, g2)


def _mla_prep_kernel(x_ref, n1_ref, sc1_ref, sh1_ref, nk_ref, sck_ref, shk_ref,
                     wdq_ref, gql_ref, wuq_ref, wuqs_ref, wdown_ref, gkl_ref, wk_ref, gk_ref, wuv_ref,
                     qa_ref, qb_ref, kc_ref, ksa_ref, ksb_ref, q_ref, k_ref, v_ref):
    xn = _rms(x_ref[...])
    hq = xn * n1_ref[...] * (1.0 + sc1_ref[...]) + sh1_ref[...]
    hk = xn * nk_ref[...] * (1.0 + sck_ref[...]) + shk_ref[...]
    n_heads = q_ref.shape[1] // LANES
    half = ROPE_DIM // 2

    cq = (_rms(jnp.dot(hq.astype(BF16), wdq_ref[...], preferred_element_type=F32)) * gql_ref[...]).astype(BF16)
    q_raw = jnp.dot(cq, wuq_ref[...], preferred_element_type=F32)
    q_swp = jnp.dot(cq, wuqs_ref[...], preferred_element_type=F32)
    qa, qb = qa_ref[...], qb_ref[...]
    for h in range(n_heads):
        sl = slice(h * LANES, (h + 1) * LANES)
        t = q_raw[:, sl]
        r = lax.rsqrt(jnp.sum(t * t, axis=-1, keepdims=True) / QK_DIM + NORM_EPS)
        q_ref[:, sl] = (r * (t * qa + q_swp[:, sl] * qb)).astype(q_ref.dtype)

    down = jnp.dot(hk.astype(BF16), wdown_ref[...], preferred_element_type=F32)
    ckv = (_rms(down[:, :KV_RANK]) * gkl_ref[...]).astype(BF16)
    pe = down[:, KV_RANK:]
    ss_pe = jnp.sum(pe * pe, axis=-1, keepdims=True)
    rot = (pe * kc_ref[...] + pltpu.roll(pe, LANES - half, 1) * ksa_ref[...]
           + pltpu.roll(pe, half, 1) * ksb_ref[...])
    rot_hi = rot.astype(BF16)
    rot_lo = (rot - rot_hi.astype(F32)).astype(BF16)
    k_raw = jnp.dot(jnp.concatenate([ckv, rot_hi, rot_lo], axis=1), wk_ref[...],
                    preferred_element_type=F32)
    nope = lax.broadcasted_iota(jnp.int32, (x_ref.shape[0], LANES), 1) < NOPE_DIM
    for h in range(n_heads):
        sl = slice(h * LANES, (h + 1) * LANES)
        t = k_raw[:, sl]
        ss = jnp.sum(jnp.where(nope, t * t, 0.0), axis=-1, keepdims=True) + ss_pe
        k_ref[:, sl] = (t * lax.rsqrt(ss / QK_DIM + NORM_EPS) * gk_ref[...]).astype(k_ref.dtype)
    v_ref[...] = jnp.dot(ckv, wuv_ref[...], preferred_element_type=F32).astype(v_ref.dtype)


def _pad_heads(w, n_heads, width):
    r = w.shape[0]
    return jnp.pad(w.reshape(r, n_heads, width), ((0, 0), (0, 0), (0, LANES - width))).reshape(r, n_heads * LANES)


def _swap_rope_halves(a):
    half = ROPE_DIM // 2
    lead = a.shape[:-1]
    return jnp.concatenate([jnp.zeros(lead + (NOPE_DIM,), a.dtype), a[..., NOPE_DIM + half:QK_DIM],
                            a[..., NOPE_DIM:NOPE_DIM + half],
                            jnp.zeros(lead + (LANES - QK_DIM,), a.dtype)], axis=-1)


def _mla_prep(xf, pos, n1, sc1, sh1, nk, sck, shk, w_dq, g_ql, w_uq, g_q, w_down, g_kl, w_uk, w_uv,
              g_k, seq):
    n, d = xf.shape
    n_heads = w_uv.shape[1] // V_DIM
    q_rank = w_dq.shape[1]
    hw = n_heads * LANES
    tm = min(ROW_TILE, seq)
    per_b = seq // tm

    half = ROPE_DIM // 2
    inv = jnp.power(ROPE_THETA, -jnp.arange(half, dtype=F32) / half)
    ang = pos.astype(F32).reshape(n, 1) * inv
    cos, sin = jnp.cos(ang), jnp.sin(ang)
    ones = jnp.ones((n, NOPE_DIM), F32)
    zeros = lambda wdt: jnp.zeros((n, wdt), F32)
    tail = LANES - QK_DIM
    pad_gain = lambda g: jnp.pad(g.reshape(1, QK_DIM), ((0, 0), (0, tail)))
    q_mul = QK_DIM ** -0.5 * LOG2E
    qa_t = jnp.concatenate([ones, cos, cos, zeros(tail)], axis=1) * pad_gain(g_q) * q_mul
    qb_t = (jnp.concatenate([zeros(NOPE_DIM), -sin, sin, zeros(tail)], axis=1)
            * _swap_rope_halves(g_q.reshape(1, QK_DIM)) * q_mul)
    gk1, gk2 = g_k[NOPE_DIM:NOPE_DIM + half], g_k[NOPE_DIM + half:]
    rest = zeros(LANES - ROPE_DIM)
    kc_t = jnp.concatenate([cos * gk1, cos * gk2, rest], axis=1)
    ksa_t = jnp.concatenate([-sin * gk2, zeros(half), rest], axis=1)
    ksb_t = jnp.concatenate([zeros(half), sin * gk1, rest], axis=1)
    gk_mask = jnp.concatenate([g_k[:NOPE_DIM], jnp.ones((ROPE_DIM,), F32),
                               jnp.zeros((tail,), F32)]).reshape(1, LANES)

    wuq_p = _pad_heads(w_uq, n_heads, QK_DIM).astype(BF16)
    wuq_s = _swap_rope_halves(w_uq.reshape(q_rank, n_heads, QK_DIM)).reshape(q_rank, hw).astype(BF16)
    wdown_p = jnp.pad(w_down, ((0, 0), (0, LANES - ROPE_DIM))).astype(BF16)
    eye = jnp.pad(jnp.eye(ROPE_DIM, dtype=F32), ((0, LANES - ROPE_DIM), (NOPE_DIM, tail)))
    pe_rows = jnp.tile(eye, (1, n_heads))
    wk = jnp.concatenate([_pad_heads(w_uk, n_heads, NOPE_DIM), pe_rows, pe_rows], axis=0).astype(BF16)

    vec = pl.BlockSpec((None, 1, d), lambda i: (i // per_b, 0, 0))
    full = lambda a: pl.BlockSpec(a.shape, lambda i: (0,) * a.ndim)
    row = lambda wdt: pl.BlockSpec((tm, wdt), lambda i: (i, 0))
    args = [xf, n1, sc1, sh1, nk, sck, shk,
            w_dq.astype(BF16), g_ql.reshape(1, q_rank), wuq_p, wuq_s,
            wdown_p, g_kl.reshape(1, KV_RANK), wk, gk_mask, w_uv.astype(BF16),
            qa_t, qb_t, kc_t, ksa_t, ksb_t]
    specs = [row(d), full(n1), vec, vec, full(nk), vec, vec] + [full(a) for a in args[7:16]] + [row(LANES)] * 5
    return pl.pallas_call(
        _mla_prep_kernel,
        grid=(n // tm,),
        in_specs=specs,
        out_specs=[row(hw), row(hw), row(n_heads * V_DIM)],
        out_shape=[jax.ShapeDtypeStruct((n, hw), BF16), jax.ShapeDtypeStruct((n, hw), BF16),
                   jax.ShapeDtypeStruct((n, n_heads * V_DIM), BF16)],
        compiler_params=_cparams("parallel"),
    )(*args)


def _causal_bias(seq):
    a = np.arange(QBLK)[:, None]
    c = np.arange(seq)[None, :]
    return np.where(c <= (seq - QBLK) + a, 0.0, MASKED).astype(np.float32)


def _mla_attn_kernel(q_ref, k_ref, v_ref, mask_ref, wa_ref, wb_ref, o_ref, wa_out, wb_out,
                     s_ref, p_ref):
    wa_out[...] = wa_ref[...].astype(BF16)
    wb_out[...] = wb_ref[...].astype(BF16)
    seq = q_ref.shape[0]
    lane_lo = lax.broadcasted_iota(jnp.int32, (QBLK, LANES), 1) < V_DIM

    def qk_of(blk, h, w):
        lanes = slice(h * LANES, (h + 1) * LANES)
        return q_ref[w - QBLK:w, lanes], k_ref[0:w, lanes]

    _attention_blocks(seq, qk_of, v_ref, mask_ref, s_ref, p_ref, o_ref, lane_lo)


def _mla_attention(q, k, v, cast_weights, layer):
    bsz, seq, hw = q.shape
    n_pairs = hw // (2 * LANES)
    qk = pl.BlockSpec((None, seq, 2 * LANES), lambda b, h: (b, 0, h))
    vo = pl.BlockSpec((None, seq, LANES), lambda b, h: (b, 0, h))
    grid = (bsz, n_pairs)
    w_in, w_out, w_shapes = _cast_rider(cast_weights, layer, grid)
    return pl.pallas_call(
        _mla_attn_kernel,
        grid=grid,
        in_specs=[qk, qk, vo, pl.BlockSpec((QBLK, seq), lambda b, h: (0, 0))] + w_in,
        out_specs=[vo] + w_out,
        scratch_shapes=_attn_scratch(seq),
        out_shape=[jax.ShapeDtypeStruct(v.shape, BF16)] + w_shapes,
        compiler_params=_cparams("parallel", "parallel"),
    )(q, k, v, jnp.asarray(_causal_bias(seq)), *cast_weights)


def kernel(x, c, pos, ada_w, ada_b, norm1_g, norm2_g, a_w_qkv, a_q_norm, a_k_norm, a_w_o, kv_ada_w, kv_ada_b, kv_norm_g, kv_w_down, kv_latent_norm, kv_w_uk, kv_w_uv, kv_k_norm, b_w_dq, b_q_latent_norm, b_w_uq, b_q_norm, b_w_o, moe_w_router, moe_router_bias, moe_w_gate_up, moe_w_down, moe_shared_gate_up, moe_shared_down):
    bsz, seq, d = x.shape
    n = bsz * seq
    depth = ada_w.shape[0]
    n_a = a_w_qkv.shape[0]
    assert seq % QBLK == 0
    xf = x.reshape(n, d)

    expert_weights = (moe_w_gate_up, moe_w_down)
    s_gu = moe_shared_gate_up.astype(BF16)
    s_dn = moe_shared_down.astype(BF16)
    row = lambda g: g.reshape(1, d)

    for layer in range(depth):
        mod = _modulation(c, ada_w, ada_b, layer).reshape(bsz, 6, 1, d)
        sh1, sc1, g1, sh2, sc2, g2 = [mod[:, i] for i in range(6)]
        if layer < n_a:
            qkv = _qkv_proj(xf, row(norm1_g[layer]), sc1, sh1, a_w_qkv[layer].astype(BF16), seq)
            o, w_gu, w_dn = _dilated_attention(qkv.reshape(bsz, seq, -1), pos, a_q_norm[layer],
                                               a_k_norm[layer], expert_weights, layer)
            w_o = a_w_o[layer]
        else:
            j = layer - n_a
            kmod = _modulation(c, kv_ada_w[None], kv_ada_b[None], 0).reshape(bsz, 2, 1, d)
            q, k, v = _mla_prep(xf, pos, row(norm1_g[layer]), sc1, sh1, row(kv_norm_g), kmod[:, 1],
                                kmod[:, 0], b_w_dq[j], b_q_latent_norm[j], b_w_uq[j], b_q_norm[j],
                                kv_w_down, kv_latent_norm, kv_w_uk, kv_w_uv, kv_k_norm, seq)
            o, w_gu, w_dn = _mla_attention(q.reshape(bsz, seq, -1), k.reshape(bsz, seq, -1),
                                           v.reshape(bsz, seq, -1), expert_weights, layer)
            w_o = b_w_o[j]
        xm, h2, h2_slabs, logits = _post_attn(o.reshape(n, d), w_o.astype(BF16), xf, g1,
                                              row(norm2_g[layer]), sc2, sh2, moe_w_router[layer], seq)
        xf = _moe(h2, h2_slabs, xm, g2, logits, moe_router_bias[layer], w_gu, w_dn, s_gu, s_dn,
                  layer, seq)
    return xf.reshape(bsz, seq, d)
```

```python
import functools
import math

import numpy as np
import jax
import jax.numpy as jnp
from jax import lax
from jax.experimental import pallas as pl
from jax.experimental.pallas import tpu as pltpu

F32 = jnp.float32
BF16 = jnp.bfloat16
U32 = jnp.uint32

NORM_EPS = 1e-6
MASKED = -1e30
LOG2E = math.log2(math.e)

LANES = 128
SUBLANES = 8
VMEM_LIMIT = 56 * 1024 * 1024

A_HEAD_DIM = 64
DILATED_PAIRS = ((128, 1), (512, 4), (2048, 16))
QBLK = 256
NOPE_DIM = 64
ROPE_DIM = 32
QK_DIM = NOPE_DIM + ROPE_DIM
V_DIM = 64
KV_RANK = 256
ROPE_THETA = 10000.0
N_EXPERTS = 64
N_GROUPS = 8
TOPK_GROUPS = 4
TOP_K = 6
EXPERT_DIM = 256
ROUTED_SCALE = 2.5
MOE_TILE = 2048
EROWS = 256
CROWS = 256

ROW_TILE = 512


def _cparams(*sem):
    return pltpu.CompilerParams(dimension_semantics=sem, vmem_limit_bytes=VMEM_LIMIT)


def _silu(v):
    return v * jax.nn.sigmoid(v)


def _rms(v):
    return v * lax.rsqrt(jnp.mean(v * v, axis=-1, keepdims=True) + NORM_EPS)


def _dot_nt(a, b):
    return lax.dot_general(a, b, (((1,), (1,)), ((), ())), preferred_element_type=F32)


def _mod_kernel(c_ref, w_ref, b_ref, o_ref):
    cond = _silu(c_ref[...])
    o_ref[...] = jnp.dot(cond, w_ref[...], preferred_element_type=F32,
                         precision=lax.Precision.HIGHEST) + b_ref[...]


def _modulation(c, w, b, layer):
    bsz, d = c.shape
    m = w.shape[-1]
    tn = 1024
    return pl.pallas_call(
        _mod_kernel,
        grid=(m // tn,),
        in_specs=[pl.BlockSpec((bsz, d), lambda j: (0, 0)),
                  pl.BlockSpec((None, d, tn), lambda j: (layer, 0, j)),
                  pl.BlockSpec((None, 1, tn), lambda j: (layer, 0, j))],
        out_specs=pl.BlockSpec((bsz, tn), lambda j: (0, j)),
        out_shape=jax.ShapeDtypeStruct((bsz, m), F32),
        compiler_params=_cparams("arbitrary"),
    )(c, w, b.reshape(b.shape[0], 1, m))


def _qkv_kernel(x_ref, g_ref, sc_ref, sh_ref, w_ref, o_ref):
    h = _rms(x_ref[...]) * g_ref[...]
    h = h * (1.0 + sc_ref[...]) + sh_ref[...]
    o_ref[...] = jnp.dot(h.astype(BF16), w_ref[...], preferred_element_type=F32).astype(o_ref.dtype)


def _qkv_proj(xf, g, sc, sh, w, seq):
    n, d = xf.shape
    m = w.shape[1]
    tm = min(ROW_TILE, seq)
    per_b = seq // tm
    vec = pl.BlockSpec((None, 1, d), lambda i: (i // per_b, 0, 0))
    return pl.pallas_call(
        _qkv_kernel,
        grid=(n // tm,),
        in_specs=[pl.BlockSpec((tm, d), lambda i: (i, 0)),
                  pl.BlockSpec((1, d), lambda i: (0, 0)),
                  vec, vec,
                  pl.BlockSpec((d, m), lambda i: (0, 0))],
        out_specs=pl.BlockSpec((tm, m), lambda i: (i, 0)),
        out_shape=jax.ShapeDtypeStruct((n, m), BF16),
        compiler_params=_cparams("parallel"),
    )(xf, g, sc, sh, w)


ATT_SLOTS = 4


def _attention_blocks(seq, qk_of, v_ref, bias_ref, s_ref, p_ref, o_ref, lane_lo):
    nblk = seq // QBLK

    def slot(blk, h):
        return (blk % 2) * 2 + h

    def scores(blk):
        w = (blk + 1) * QBLK
        bias = bias_ref[:, seq - w:seq]
        for h in range(2):
            q, k = qk_of(blk, h, w)
            s_ref[slot(blk, h), :, 0:w] = _dot_nt(q, k) + bias

    for blk in range(nblk):
        w = (blk + 1) * QBLK
        scores(blk)
        denom = []
        for h in range(2):
            s = s_ref[slot(blk, h), :, 0:w]
            p = jnp.exp2(s - jnp.max(s, axis=-1, keepdims=True))
            denom.append(jnp.sum(p, axis=-1, keepdims=True))
            p_ref[slot(blk, h), :, 0:w] = p.astype(BF16)
        o_a, o_b = [jnp.dot(p_ref[slot(blk, h), :, 0:w], v_ref[0:w, :],
                            preferred_element_type=F32) / denom[h] for h in range(2)]
        o_ref[blk * QBLK:w, :] = jnp.where(lane_lo, o_a, o_b).astype(o_ref.dtype)


def _attn_scratch(seq):
    return [pltpu.VMEM((ATT_SLOTS, QBLK, seq), F32), pltpu.VMEM((ATT_SLOTS, QBLK, seq), BF16)]


def _cast_rider(weights, layer, grid):
    steps = grid[0] * grid[1]
    in_specs, out_specs, out_shapes = [], [], []
    for w in weights:
        n_exp = w.shape[1]
        assert n_exp % steps == 0
        per = n_exp // steps
        blk = (per,) + w.shape[2:]
        in_specs.append(pl.BlockSpec((None,) + blk, lambda b, h: (layer, b * grid[1] + h, 0, 0)))
        out_specs.append(pl.BlockSpec(blk, lambda b, h: (b * grid[1] + h, 0, 0)))
        out_shapes.append(jax.ShapeDtypeStruct(w.shape[1:], BF16))
    return in_specs, out_specs, out_shapes


def _split3(v):
    p1 = v.astype(BF16).astype(F32)
    p2 = (v - p1).astype(BF16).astype(F32)
    return p1, p2, v - p1 - p2


def _dilated_log_multiplicity(seq):
    a = np.arange(QBLK)[:, None]
    c = np.arange(seq)[None, :]
    d = (seq - QBLK) + a - c
    mult = np.zeros((QBLK, seq), np.int32)
    for win, dil in DILATED_PAIRS:
        mult += ((d >= 0) & (d % dil == 0) & (d <= win)).astype(np.int32)
    return np.where(mult > 0, np.log2(np.maximum(mult, 1)), MASKED).astype(np.float32)


N_FEAT = 9


def _dil_attn_kernel(q_ref, k_ref, v_ref, posc_ref, sfeat_ref, gq_ref, gk_ref, logm_ref,
                     wa_ref, wb_ref, o_ref, wa_out, wb_out, qa_ref, qb_ref, ks_ref, s_ref, p_ref):
    wa_out[...] = wa_ref[...].astype(BF16)
    wb_out[...] = wb_ref[...].astype(BF16)
    seq = q_ref.shape[0]
    lane = lax.broadcasted_iota(jnp.int32, (QBLK, LANES), 1)
    lane_lo = lane < A_HEAD_DIM
    piece = [functools.reduce(jnp.logical_or, [lane == i + 3 * j for j in range(N_FEAT // 3)])
             for i in range(3)]

    def head_norm(ref, r0, g_ref, mul):
        t = ref[r0:r0 + QBLK, :].astype(F32)
        sq = t * t
        ssa = jnp.sum(jnp.where(lane_lo, sq, 0.0), axis=-1, keepdims=True)
        ssb = jnp.sum(jnp.where(lane_lo, 0.0, sq), axis=-1, keepdims=True)
        r = jnp.where(lane_lo, lax.rsqrt(ssa / A_HEAD_DIM + NORM_EPS),
                      lax.rsqrt(ssb / A_HEAD_DIM + NORM_EPS))
        return t * r * (g_ref[...] * mul)

    rel0 = posc_ref[0:1, :]
    for r0 in range(0, seq, QBLK):
        rows = slice(r0, r0 + QBLK)
        qn = head_norm(q_ref, r0, gq_ref, A_HEAD_DIM ** -0.5 * LOG2E)
        qa_ref[rows, :LANES] = jnp.where(lane_lo, qn, 0.0).astype(BF16)
        qb_ref[rows, :LANES] = jnp.where(lane_lo, 0.0, qn).astype(BF16)
        qa_ref[rows, LANES:] = jnp.broadcast_to(sfeat_ref[0:1, :], (QBLK, LANES)).astype(BF16)
        qb_ref[rows, LANES:] = jnp.broadcast_to(sfeat_ref[1:2, :], (QBLK, LANES)).astype(BF16)
        ks_ref[rows, :LANES] = head_norm(k_ref, r0, gk_ref, 1.0).astype(BF16)
        p1, p2, p3 = _split3((posc_ref[rows, :] - rel0).astype(F32))
        feat = jnp.where(piece[0], p1, jnp.where(piece[1], p2, jnp.where(piece[2], p3, 0.0)))
        ks_ref[rows, LANES:] = feat.astype(BF16)

    def qk_of(blk, h, w):
        return (qa_ref, qb_ref)[h][w - QBLK:w, :], ks_ref[0:w, :]

    _attention_blocks(seq, qk_of, v_ref, logm_ref, s_ref, p_ref, o_ref, lane_lo)


def _dilated_attention(qkv, pos, q_norm, k_norm, cast_weights, layer):
    bsz, seq, three_d = qkv.shape
    d = three_d // 3
    n_pairs = d // LANES
    n_heads = d // A_HEAD_DIM
    slopes = jnp.exp2(-8.0 * jnp.arange(1, n_heads + 1, dtype=F32) / n_heads) * LOG2E
    sfeat = jnp.repeat(jnp.stack(_split3(slopes), axis=1), 3, axis=1)
    sfeat = jnp.pad(sfeat, ((0, 0), (0, LANES - N_FEAT))).reshape(n_pairs, 2, LANES)
    logm = jnp.asarray(_dilated_log_multiplicity(seq))
    gq = jnp.tile(q_norm.reshape(1, A_HEAD_DIM), (1, 2))
    gk = jnp.tile(k_norm.reshape(1, A_HEAD_DIM), (1, 2))
    blk = lambda off: pl.BlockSpec((None, seq, LANES), lambda b, h: (b, 0, off + h))
    const2 = pl.BlockSpec((1, LANES), lambda b, h: (0, 0))
    ext = pltpu.VMEM((seq, 2 * LANES), BF16)
    grid = (bsz, n_pairs)
    w_in, w_out, w_shapes = _cast_rider(cast_weights, layer, grid)
    return pl.pallas_call(
        _dil_attn_kernel,
        grid=grid,
        in_specs=[blk(0), blk(n_pairs), blk(2 * n_pairs),
                  pl.BlockSpec((None, seq, 1), lambda b, h: (b, 0, 0)),
                  pl.BlockSpec((None, 2, LANES), lambda b, h: (h, 0, 0)),
                  const2, const2,
                  pl.BlockSpec((QBLK, seq), lambda b, h: (0, 0))] + w_in,
        out_specs=[pl.BlockSpec((None, seq, LANES), lambda b, h: (b, 0, h))] + w_out,
        scratch_shapes=[ext, ext, ext] + _attn_scratch(seq),
        out_shape=[jax.ShapeDtypeStruct((bsz, seq, d), BF16)] + w_shapes,
        compiler_params=_cparams("parallel", "parallel"),
    )(qkv, qkv, qkv, pos.reshape(bsz, seq, 1), sfeat, gq, gk, logm, *cast_weights)


def _post_attn_kernel(o_ref, wo_ref, x_ref, g1_ref, n2_ref, sc_ref, sh_ref, wrh_ref, wrl_ref,
                      xm_ref, h_ref, hs_ref, lg_ref, il_ref):
    y = jnp.dot(o_ref[...], wo_ref[...], preferred_element_type=F32)
    xm = x_ref[...] + g1_ref[...] * y
    xm_ref[...] = xm
    h = _rms(xm) * n2_ref[...]
    h = h * (1.0 + sc_ref[...]) + sh_ref[...]
    h_ref[...] = h.astype(h_ref.dtype)
    hs_ref[...] = _rows_to_slab_rows(h, il_ref)
    h_hi = h.astype(BF16)
    h_lo = (h - h_hi.astype(F32)).astype(BF16)
    lg_ref[...] = (_dot_nt(wrh_ref[...], h_hi) + _dot_nt(wrh_ref[...], h_lo)
                   + _dot_nt(wrl_ref[...], h_hi))


def _post_attn(o, w_o, xf, g1, n2, sc2, sh2, w_router, seq):
    n, d = xf.shape
    ne = w_router.shape[1]
    tm = min(ROW_TILE, seq)
    per_b = seq // tm
    slab_rows = d // LANES
    wr_t = w_router.T.astype(F32)
    wr_hi = wr_t.astype(BF16)
    vec = pl.BlockSpec((None, 1, d), lambda i: (i // per_b, 0, 0))
    row = lambda w: pl.BlockSpec((tm, w), lambda i: (i, 0))
    return pl.pallas_call(
        _post_attn_kernel,
        grid=(n // tm,),
        in_specs=[row(d), pl.BlockSpec((d, d), lambda i: (0, 0)), row(d), vec,
                  pl.BlockSpec((1, d), lambda i: (0, 0)), vec, vec,
                  pl.BlockSpec((ne, d), lambda i: (0, 0)), pl.BlockSpec((ne, d), lambda i: (0, 0))],
        out_specs=[row(d), row(d), pl.BlockSpec((tm * slab_rows, LANES), lambda i: (i, 0)),
                   pl.BlockSpec((ne, tm), lambda i: (0, i))],
        out_shape=[jax.ShapeDtypeStruct((n, d), F32), jax.ShapeDtypeStruct((n, d), BF16),
                   jax.ShapeDtypeStruct((n * slab_rows, LANES), BF16),
                   jax.ShapeDtypeStruct((ne, n), F32)],
        scratch_shapes=[pltpu.VMEM((tm * slab_rows, LANES), F32)],
        compiler_params=_cparams("parallel"),
    )(o, w_o, xf, g1, n2, sc2, sh2, wr_hi, (wr_t - wr_hi.astype(F32)).astype(BF16))


def _moe_rows(tile):
    rows = tile * TOP_K + N_EXPERTS * (SUBLANES - 1) + EROWS
    return -(-rows // SUBLANES) * SUBLANES


def _first_index(hit, index, limit):
    return jnp.min(jnp.where(hit, index, limit), axis=0, keepdims=True)


def _route_kernel(lg_ref, bias_ref, tri_ref, dest_ref, w_ref, seg_ref, cnt_ref):
    t = lg_ref.shape[1]
    per_g = N_EXPERTS // N_GROUPS
    sub = lax.broadcasted_iota(jnp.int32, (per_g, t), 0).astype(F32)
    neg = -jnp.inf
    scores, sel = [], []
    for g in range(N_GROUPS):
        rows = slice(g * per_g, (g + 1) * per_g)
        sc = jax.nn.sigmoid(lg_ref[rows, :])
        scores.append(sc)
        sel.append(sc + bias_ref[rows, :])

    gs = []
    for g in range(N_GROUPS):
        m1 = jnp.max(sel[g], axis=0, keepdims=True)
        first = _first_index(sel[g] == m1, sub, float(per_g))
        m2 = jnp.max(jnp.where(sub == first, neg, sel[g]), axis=0, keepdims=True)
        gs.append(m1 + m2)
    gsc = jnp.concatenate(gs, axis=0)
    gsub = lax.broadcasted_iota(jnp.int32, (N_GROUPS, t), 0).astype(F32)
    gkeep = jnp.zeros((N_GROUPS, t), F32)
    for _ in range(TOPK_GROUPS):
        m = jnp.max(gsc, axis=0, keepdims=True)
        pick = gsub == _first_index(gsc == m, gsub, float(N_GROUPS))
        gkeep = jnp.where(pick, 1.0, gkeep)
        gsc = jnp.where(pick, neg, gsc)

    eidx = [sub + float(g * per_g) for g in range(N_GROUPS)]
    masked = [jnp.where(jnp.broadcast_to(gkeep[g:g + 1, :], (per_g, t)) > 0.5, sel[g], neg)
              for g in range(N_GROUPS)]
    chosen = [jnp.zeros((per_g, t), F32) for _ in range(N_GROUPS)]
    wsel = [jnp.zeros((per_g, t), F32) for _ in range(N_GROUPS)]
    picked = []
    for _ in range(TOP_K):
        m = jnp.max(masked[0], axis=0, keepdims=True)
        for g in range(1, N_GROUPS):
            m = jnp.maximum(m, jnp.max(masked[g], axis=0, keepdims=True))
        e_first = _first_index(masked[0] == m, eidx[0], float(N_EXPERTS))
        for g in range(1, N_GROUPS):
            e_first = jnp.minimum(e_first, _first_index(masked[g] == m, eidx[g], float(N_EXPERTS)))
        picked.append(e_first)
        for g in range(N_GROUPS):
            pick = eidx[g] == e_first
            chosen[g] = jnp.where(pick, 1.0, chosen[g])
            wsel[g] = jnp.where(pick, scores[g], wsel[g])
            masked[g] = jnp.where(pick, neg, masked[g])

    denom = jnp.sum(wsel[0], axis=0, keepdims=True)
    for g in range(1, N_GROUPS):
        denom = denom + jnp.sum(wsel[g], axis=0, keepdims=True)

    c_all = jnp.concatenate(chosen, axis=0)
    rank = jnp.dot(c_all.astype(BF16), tri_ref[...], preferred_element_type=F32)
    counts = jnp.sum(c_all, axis=1, keepdims=True)
    padded = jnp.ceil(counts / SUBLANES) * SUBLANES
    below = (lax.broadcasted_iota(jnp.int32, (N_EXPERTS, N_EXPERTS), 1)
             < lax.broadcasted_iota(jnp.int32, (N_EXPERTS, N_EXPERTS), 0)).astype(F32)
    seg = jnp.dot(below, jnp.broadcast_to(padded, (N_EXPERTS, LANES)), preferred_element_type=F32,
                  precision=lax.Precision.HIGHEST)
    row_of = seg[:, :1] + rank

    dest_rows, w_rows = [], []
    for k in range(TOP_K):
        d_k = jnp.zeros((1, t), F32)
        w_k = jnp.zeros((1, t), F32)
        for g in range(N_GROUPS):
            hit = eidx[g] == picked[k]
            d_k = d_k + jnp.sum(jnp.where(hit, row_of[g * per_g:(g + 1) * per_g, :], 0.0), axis=0,
                                keepdims=True)
            w_k = w_k + jnp.sum(jnp.where(hit, wsel[g], 0.0), axis=0, keepdims=True)
        dest_rows.append(d_k)
        w_rows.append(w_k / denom * ROUTED_SCALE)
    fill = [jnp.zeros((SUBLANES - TOP_K, t), F32)]
    dest_ref[...] = jnp.concatenate(dest_rows + fill, axis=0).astype(jnp.int32)
    w_ref[...] = jnp.concatenate(w_rows + fill, axis=0)
    seg_ref[...] = seg.astype(jnp.int32)
    cnt_ref[...] = jnp.broadcast_to(counts, (N_EXPERTS, LANES)).astype(jnp.int32)


def _route(logits_t, router_bias, tile):
    ne, n = logits_t.shape
    nt = n // tile
    tri = jnp.asarray(np.triu(np.ones((tile, tile), np.float32), 1), dtype=BF16)
    per_tile = lambda rows, width: pl.BlockSpec((None, rows, width), lambda i: (i, 0, 0))
    dest, wts, seg, cnt = pl.pallas_call(
        _route_kernel,
        grid=(nt,),
        in_specs=[pl.BlockSpec((ne, tile), lambda i: (0, i)),
                  pl.BlockSpec((ne, 1), lambda i: (0, 0)),
                  pl.BlockSpec((tile, tile), lambda i: (0, 0))],
        out_specs=[per_tile(SUBLANES, tile), per_tile(SUBLANES, tile),
                   per_tile(ne, LANES), per_tile(ne, LANES)],
        out_shape=[jax.ShapeDtypeStruct((nt, SUBLANES, tile), jnp.int32),
                   jax.ShapeDtypeStruct((nt, SUBLANES, tile), F32),
                   jax.ShapeDtypeStruct((nt, ne, LANES), jnp.int32),
                   jax.ShapeDtypeStruct((nt, ne, LANES), jnp.int32)],
        compiler_params=_cparams("parallel"),
    )(logits_t, router_bias.astype(F32).reshape(ne, 1), tri)
    return seg[:, :, 0].reshape(-1), cnt[:, :, 0].reshape(-1), dest, wts


def _swiglu(xb, wgu_ref, wdn_ref):
    gu = jnp.dot(xb, wgu_ref[...], preferred_element_type=F32)
    half = wgu_ref.shape[1] // 2
    act = _silu(gu[:, :half]) * gu[:, half:]
    return jnp.dot(act.astype(BF16), wdn_ref[...], preferred_element_type=F32)


def _slabs_to_rows(p, il_ref, r):
    n = il_ref.shape[0] // r
    il_ref[...] = pltpu.bitcast(p, BF16).astype(F32)
    return jnp.concatenate([il_ref[pl.ds(2 * c + h, r, stride=n), :]
                            for h in range(2) for c in range(n // 2)], axis=1)


def _rows_to_slab_rows(v, il_ref):
    r = v.shape[0]
    n = il_ref.shape[0] // r
    for h in range(2):
        for c in range(n // 2):
            lane0 = (h * (n // 2) + c) * LANES
            il_ref[pl.ds(2 * c + h, r, stride=n), :] = v[:, lane0:lane0 + LANES]
    return il_ref[...].astype(BF16)


def _rows_to_slabs(v, il_ref):
    return pltpu.bitcast(_rows_to_slab_rows(v, il_ref), U32)


def _dest_col(half, k, u):
    return (half * SUBLANES + k) * SUBLANES + u


def _moe_kernel(start_ref, count_ref, hp_ref, hb_ref, dest_ref, wrow_ref, wgu_hbm, wdn_hbm,
                sgu_ref, sdn_ref, xm_ref, g2_ref, o_ref, xy_ref, il_ref, yp_ref, zf_ref,
                comb_ref, wgu_buf, wdn_buf, wsem):
    t = pl.program_id(0)
    s = pl.program_id(1)
    slab = xm_ref.shape[1] // (2 * LANES)
    tile = hp_ref.shape[0] // (2 * slab)
    group_rows = SUBLANES * slab

    def weight_copies(e, slot):
        return (pltpu.make_async_copy(wgu_hbm.at[e], wgu_buf.at[slot], wsem.at[0, slot]),
                pltpu.make_async_copy(wdn_hbm.at[e], wdn_buf.at[slot], wsem.at[1, slot]))

    def slab_at(ref, first):
        return ref.at[pl.ds(pl.multiple_of(first, slab), slab), :]

    @pl.when(s == 0)
    def _first_weights():
        for p in range(2):
            for cp in weight_copies(p, p):
                cp.start()

    @pl.when(jnp.logical_and(t == 0, s == 0))
    def _clear():
        xy_ref[...] = jnp.zeros(xy_ref.shape, U32)

    @pl.when(s == 0)
    def _dispatch():
        def group(j, carry):
            for half in range(2):
                p0 = pl.multiple_of((2 * j + half) * 2 * group_rows, 2 * group_rows)
                rows = pltpu.bitcast(hp_ref[pl.ds(p0, 2 * group_rows), :], U32)
                for u in range(SUBLANES):
                    for k in range(TOP_K):
                        slab_at(xy_ref, dest_ref[j, _dest_col(half, k, u)])[...] = (
                            rows[u * slab:(u + 1) * slab, :])
            return carry

        lax.fori_loop(0, tile // (2 * SUBLANES), group, 0)

    @pl.when(s == 0)
    def _experts():
        def chunk_base(e, c):
            return pl.multiple_of((start_ref[t * N_EXPERTS + e] + c * EROWS) * slab, group_rows)

        def load_rows(base, p):
            return _slabs_to_rows(xy_ref[pl.ds(base, EROWS * slab), :], il_ref.at[p], EROWS).astype(BF16)

        def put_rows(p, base, valid):
            def put(j, c2):
                r0 = pl.multiple_of(j * group_rows, group_rows)
                xy_ref[pl.ds(base + r0, group_rows), :] = yp_ref[p, pl.ds(r0, group_rows), :]
                return c2

            lax.fori_loop(0, (jnp.clip(valid, 0, EROWS) + SUBLANES - 1) // SUBLANES, put, 0)

        def expert_pair(i, carry0):
            wslot = lax.rem(i, 2) * 2

            @pl.when(i + 1 < N_EXPERTS // 2)
            def _prefetch():
                for p in range(2):
                    for cp in weight_copies(2 * i + 2 + p, 2 - wslot + p):
                        cp.start()

            for p in range(2):
                for cp in weight_copies(2 * i + p, wslot + p):
                    cp.wait()
            counts = [count_ref[t * N_EXPERTS + 2 * i + p] for p in range(2)]
            bases = [chunk_base(2 * i + p, 0) for p in range(2)]
            xs = [load_rows(bases[p], p) for p in range(2)]
            ys = [_swiglu(xs[p], wgu_buf.at[wslot + p], wdn_buf.at[wslot + p]) for p in range(2)]
            for p in range(2):
                yp_ref[p] = _rows_to_slabs(ys[p], il_ref.at[p])
            for p in range(2):
                put_rows(p, bases[p], counts[p])

            for p in range(2):
                def more(c, carry, p=p):
                    base = chunk_base(2 * i + p, c)
                    y = _swiglu(load_rows(base, p), wgu_buf.at[wslot + p], wdn_buf.at[wslot + p])
                    yp_ref[p] = _rows_to_slabs(y, il_ref.at[p])
                    put_rows(p, base, counts[p] - c * EROWS)
                    return carry

                lax.fori_loop(1, (counts[p] + EROWS - 1) // EROWS, more, 0)
            return carry0

        lax.fori_loop(0, N_EXPERTS // 2, expert_pair, 0)

    @pl.when(s >= 1)
    def _combine():
        sb = s - 1

        def group(j, carry):
            g2 = sb * (CROWS // (2 * SUBLANES)) + j
            for half in range(2):
                r0 = pl.multiple_of(j * 2 * SUBLANES + half * SUBLANES, SUBLANES)
                for u in range(SUBLANES):
                    acc = None
                    for k in range(TOP_K):
                        col = _dest_col(half, k, u)
                        z = pltpu.bitcast(slab_at(xy_ref, dest_ref[g2, col])[...], BF16).astype(F32)
                        acc = wrow_ref[g2, col] * z if acc is None else acc + wrow_ref[g2, col] * z
                    zf_ref[half, u * 2 * slab:(u + 1) * 2 * slab, :] = acc
                comb_ref[pl.ds(r0, SUBLANES), :] = jnp.concatenate(
                    [zf_ref[half, pl.ds(2 * c + h, SUBLANES, stride=2 * slab), :]
                     for h in range(2) for c in range(slab)], axis=1)
            return carry

        lax.fori_loop(0, CROWS // (2 * SUBLANES), group, 0)
        shared = _swiglu(hb_ref[...], sgu_ref, sdn_ref)
        o_ref[...] = xm_ref[...] + g2_ref[...] * (comb_ref[...] + shared)


def _moe(h, h_slabs, xm, g2, logits_t, router_bias, w_gu, w_dn, s_gu, s_dn, layer, seq):
    n, d = h.shape
    slab = d // (2 * LANES)
    tile = min(MOE_TILE, seq)
    nt = n // tile
    per_b = seq // tile
    csteps = tile // CROWS
    assert seq % tile == 0 and tile % CROWS == 0
    seg_start, seg_count, dest, wts = _route(logits_t, router_bias, tile)
    def table_rows(a):
        a = a.reshape(nt, SUBLANES, tile // (2 * SUBLANES), 2, SUBLANES)
        return a.transpose(0, 2, 3, 1, 4).reshape(nt, tile // (2 * SUBLANES), -1)

    dest_rows = table_rows(dest * slab)
    w_rows = table_rows(wts)
    rows = _moe_rows(tile)
    table = pl.BlockSpec((None, tile // (2 * SUBLANES), 2 * SUBLANES * SUBLANES),
                         lambda t, s, a, b: (t, 0, 0), memory_space=pltpu.SMEM)
    cst = lambda s: jnp.maximum(s - 1, 0)
    crow = pl.BlockSpec((CROWS, d), lambda t, s, a, b: (t * csteps + cst(s), 0))
    hbm = pl.BlockSpec(memory_space=pl.ANY)
    single = pl.Buffered(1)
    return pl.pallas_call(
        _moe_kernel,
        grid_spec=pltpu.PrefetchScalarGridSpec(
            num_scalar_prefetch=2,
            grid=(nt, 1 + csteps),
            in_specs=[pl.BlockSpec((tile * 2 * slab, LANES), lambda t, s, a, b: (t, 0),
                                   pipeline_mode=single),
                      crow, table, table,
                      hbm, hbm,
                      pl.BlockSpec((None,) + s_gu.shape[1:], lambda t, s, a, b: (layer, 0, 0),
                                   pipeline_mode=single),
                      pl.BlockSpec((None,) + s_dn.shape[1:], lambda t, s, a, b: (layer, 0, 0),
                                   pipeline_mode=single),
                      crow,
                      pl.BlockSpec((None, 1, d), lambda t, s, a, b: (t // per_b, 0, 0))],
            out_specs=crow,
            scratch_shapes=[pltpu.VMEM((rows * slab, LANES), U32),
                            pltpu.VMEM((2, EROWS * 2 * slab, LANES), F32),
                            pltpu.VMEM((2, EROWS * slab, LANES), U32),
                            pltpu.VMEM((2, SUBLANES * 2 * slab, LANES), F32),
                            pltpu.VMEM((CROWS, d), F32),
                            pltpu.VMEM((4, d, 2 * EXPERT_DIM), BF16),
                            pltpu.VMEM((4, EXPERT_DIM, d), BF16),
                            pltpu.SemaphoreType.DMA((2, 4))]),
        out_shape=jax.ShapeDtypeStruct((n, d), F32),
        compiler_params=_cparams("arbitrary", "arbitrary"),
    )(seg_start, seg_count, h_slabs, h, dest_rows, w_rows, w_gu, w_dn, s_gu, s_dn, xm, g2)


def _mla_prep_kernel(x_ref, n1_ref, sc1_ref, sh1_ref, nk_ref, sck_ref, shk_ref,
                     wdq_ref, gql_ref, wuq_ref, wuqs_ref, wdown_ref, gkl_ref, wk_ref, gk_ref, wuv_ref,
                     qa_ref, qb_ref, kc_ref, ksa_ref, ksb_ref, q_ref, k_ref, v_ref):
    xn = _rms(x_ref[...])
    hq = xn * n1_ref[...] * (1.0 + sc1_ref[...]) + sh1_ref[...]
    hk = xn * nk_ref[...] * (1.0 + sck_ref[...]) + shk_ref[...]
    n_heads = q_ref.shape[1] // LANES
    half = ROPE_DIM // 2

    cq = (_rms(jnp.dot(hq.astype(BF16), wdq_ref[...], preferred_element_type=F32)) * gql_ref[...]).astype(BF16)
    q_raw = jnp.dot(cq, wuq_ref[...], preferred_element_type=F32)
    q_swp = jnp.dot(cq, wuqs_ref[...], preferred_element_type=F32)
    qa, qb = qa_ref[...], qb_ref[...]
    for h in range(n_heads):
        sl = slice(h * LANES, (h + 1) * LANES)
        t = q_raw[:, sl]
        r = lax.rsqrt(jnp.sum(t * t, axis=-1, keepdims=True) / QK_DIM + NORM_EPS)
        q_ref[:, sl] = (r * (t * qa + q_swp[:, sl] * qb)).astype(q_ref.dtype)

    down = jnp.dot(hk.astype(BF16), wdown_ref[...], preferred_element_type=F32)
    ckv = (_rms(down[:, :KV_RANK]) * gkl_ref[...]).astype(BF16)
    pe = down[:, KV_RANK:]
    ss_pe = jnp.sum(pe * pe, axis=-1, keepdims=True)
    rot = (pe * kc_ref[...] + pltpu.roll(pe, LANES - half, 1) * ksa_ref[...]
           + pltpu.roll(pe, half, 1) * ksb_ref[...])
    rot_hi = rot.astype(BF16)
    rot_lo = (rot - rot_hi.astype(F32)).astype(BF16)
    k_raw = jnp.dot(jnp.concatenate([ckv, rot_hi, rot_lo], axis=1), wk_ref[...],
                    preferred_element_type=F32)
    nope = lax.broadcasted_iota(jnp.int32, (x_ref.shape[0], LANES), 1) < NOPE_DIM
    for h in range(n_heads):
        sl = slice(h * LANES, (h + 1) * LANES)
        t = k_raw[:, sl]
        ss = jnp.sum(jnp.where(nope, t * t, 0.0), axis=-1, keepdims=True) + ss_pe
        k_ref[:, sl] = (t * lax.rsqrt(ss / QK_DIM + NORM_EPS) * gk_ref[...]).astype(k_ref.dtype)
    v_ref[...] = jnp.dot(ckv, wuv_ref[...], preferred_element_type=F32).astype(v_ref.dtype)


def _pad_heads(w, n_heads, width):
    r = w.shape[0]
    return jnp.pad(w.reshape(r, n_heads, width), ((0, 0), (0, 0), (0, LANES - width))).reshape(r, n_heads * LANES)


def _swap_rope_halves(a):
    half = ROPE_DIM // 2
    lead = a.shape[:-1]
    return jnp.concatenate([jnp.zeros(lead + (NOPE_DIM,), a.dtype), a[..., NOPE_DIM + half:QK_DIM],
                            a[..., NOPE_DIM:NOPE_DIM + half],
                            jnp.zeros(lead + (LANES - QK_DIM,), a.dtype)], axis=-1)


def _mla_prep(xf, pos, n1, sc1, sh1, nk, sck, shk, w_dq, g_ql, w_uq, g_q, w_down, g_kl, w_uk, w_uv,
              g_k, seq):
    n, d = xf.shape
    n_heads = w_uv.shape[1] // V_DIM
    q_rank = w_dq.shape[1]
    hw = n_heads * LANES
    tm = min(ROW_TILE, seq)
    per_b = seq // tm

    half = ROPE_DIM // 2
    inv = jnp.power(ROPE_THETA, -jnp.arange(half, dtype=F32) / half)
    ang = pos.astype(F32).reshape(n, 1) * inv
    cos, sin = jnp.cos(ang), jnp.sin(ang)
    ones = jnp.ones((n, NOPE_DIM), F32)
    zeros = lambda wdt: jnp.zeros((n, wdt), F32)
    tail = LANES - QK_DIM
    pad_gain = lambda g: jnp.pad(g.reshape(1, QK_DIM), ((0, 0), (0, tail)))
    q_mul = QK_DIM ** -0.5 * LOG2E
    qa_t = jnp.concatenate([ones, cos, cos, zeros(tail)], axis=1) * pad_gain(g_q) * q_mul
    qb_t = (jnp.concatenate([zeros(NOPE_DIM), -sin, sin, zeros(tail)], axis=1)
            * _swap_rope_halves(g_q.reshape(1, QK_DIM)) * q_mul)
    gk1, gk2 = g_k[NOPE_DIM:NOPE_DIM + half], g_k[NOPE_DIM + half:]
    rest = zeros(LANES - ROPE_DIM)
    kc_t = jnp.concatenate([cos * gk1, cos * gk2, rest], axis=1)
    ksa_t = jnp.concatenate([-sin * gk2, zeros(half), rest], axis=1)
    ksb_t = jnp.concatenate([zeros(half), sin * gk1, rest], axis=1)
    gk_mask = jnp.concatenate([g_k[:NOPE_DIM], jnp.ones((ROPE_DIM,), F32),
                               jnp.zeros((tail,), F32)]).reshape(1, LANES)

    wuq_p = _pad_heads(w_uq, n_heads, QK_DIM).astype(BF16)
    wuq_s = _swap_rope_halves(w_uq.reshape(q_rank, n_heads, QK_DIM)).reshape(q_rank, hw).astype(BF16)
    wdown_p = jnp.pad(w_down, ((0, 0), (0, LANES - ROPE_DIM))).astype(BF16)
    eye = jnp.pad(jnp.eye(ROPE_DIM, dtype=F32), ((0, LANES - ROPE_DIM), (NOPE_DIM, tail)))
    pe_rows = jnp.tile(eye, (1, n_heads))
    wk = jnp.concatenate([_pad_heads(w_uk, n_heads, NOPE_DIM), pe_rows, pe_rows], axis=0).astype(BF16)

    vec = pl.BlockSpec((None, 1, d), lambda i: (i // per_b, 0, 0))
    full = lambda a: pl.BlockSpec(a.shape, lambda i: (0,) * a.ndim)
    row = lambda wdt: pl.BlockSpec((tm, wdt), lambda i: (i, 0))
    args = [xf, n1, sc1, sh1, nk, sck, shk,
            w_dq.astype(BF16), g_ql.reshape(1, q_rank), wuq_p, wuq_s,
            wdown_p, g_kl.reshape(1, KV_RANK), wk, gk_mask, w_uv.astype(BF16),
            qa_t, qb_t, kc_t, ksa_t, ksb_t]
    specs = [row(d), full(n1), vec, vec, full(nk), vec, vec] + [full(a) for a in args[7:16]] + [row(LANES)] * 5
    return pl.pallas_call(
        _mla_prep_kernel,
        grid=(n // tm,),
        in_specs=specs,
        out_specs=[row(hw), row(hw), row(n_heads * V_DIM)],
        out_shape=[jax.ShapeDtypeStruct((n, hw), BF16), jax.ShapeDtypeStruct((n, hw), BF16),
                   jax.ShapeDtypeStruct((n, n_heads * V_DIM), BF16)],
        compiler_params=_cparams("parallel"),
    )(*args)


def _causal_bias(seq):
    a = np.arange(QBLK)[:, None]
    c = np.arange(seq)[None, :]
    return np.where(c <= (seq - QBLK) + a, 0.0, MASKED).astype(np.float32)


def _mla_attn_kernel(q_ref, k_ref, v_ref, mask_ref, wa_ref, wb_ref, o_ref, wa_out, wb_out,
                     s_ref, p_ref):
    wa_out[...] = wa_ref[...].astype(BF16)
    wb_out[...] = wb_ref[...].astype(BF16)
    seq = q_ref.shape[0]
    lane_lo = lax.broadcasted_iota(jnp.int32, (QBLK, LANES), 1) < V_DIM

    def qk_of(blk, h, w):
        lanes = slice(h * LANES, (h + 1) * LANES)
        return q_ref[w - QBLK:w, lanes], k_ref[0:w, lanes]

    _attention_blocks(seq, qk_of, v_ref, mask_ref, s_ref, p_ref, o_ref, lane_lo)


def _mla_attention(q, k, v, cast_weights, layer):
    bsz, seq, hw = q.shape
    n_pairs = hw // (2 * LANES)
    qk = pl.BlockSpec((None, seq, 2 * LANES), lambda b, h: (b, 0, h))
    vo = pl.BlockSpec((None, seq, LANES), lambda b, h: (b, 0, h))
    grid = (bsz, n_pairs)
    w_in, w_out, w_shapes = _cast_rider(cast_weights, layer, grid)
    return pl.pallas_call(
        _mla_attn_kernel,
        grid=grid,
        in_specs=[qk, qk, vo, pl.BlockSpec((QBLK, seq), lambda b, h: (0, 0))] + w_in,
        out_specs=[vo] + w_out,
        scratch_shapes=_attn_scratch(seq),
        out_shape=[jax.ShapeDtypeStruct(v.shape, BF16)] + w_shapes,
        compiler_params=_cparams("parallel", "parallel"),
    )(q, k, v, jnp.asarray(_causal_bias(seq)), *cast_weights)


def kernel(x, c, pos, ada_w, ada_b, norm1_g, norm2_g, a_w_qkv, a_q_norm, a_k_norm, a_w_o, kv_ada_w, kv_ada_b, kv_norm_g, kv_w_down, kv_latent_norm, kv_w_uk, kv_w_uv, kv_k_norm, b_w_dq, b_q_latent_norm, b_w_uq, b_q_norm, b_w_o, moe_w_router, moe_router_bias, moe_w_gate_up, moe_w_down, moe_shared_gate_up, moe_shared_down):
    bsz, seq, d = x.shape
    n = bsz * seq
    depth = ada_w.shape[0]
    n_a = a_w_qkv.shape[0]
    assert seq % QBLK == 0
    xf = x.reshape(n, d)

    expert_weights = (moe_w_gate_up, moe_w_down)
    s_gu = moe_shared_gate_up.astype(BF16)
    s_dn = moe_shared_down.astype(BF16)
    row = lambda g: g.reshape(1, d)

    for layer in range(depth):
        mod = _modulation(c, ada_w, ada_b, layer).reshape(bsz, 6, 1, d)
        sh1, sc1, g1, sh2, sc2, g2 = [mod[:, i] for i in range(6)]
        if layer < n_a:
            qkv = _qkv_proj(xf, row(norm1_g[layer]), sc1, sh1, a_w_qkv[layer].astype(BF16), seq)
            o, w_gu, w_dn = _dilated_attention(qkv.reshape(bsz, seq, -1), pos, a_q_norm[layer],
                                               a_k_norm[layer], expert_weights, layer)
            w_o = a_w_o[layer]
        else:
            j = layer - n_a
            kmod = _modulation(c, kv_ada_w[None], kv_ada_b[None], 0).reshape(bsz, 2, 1, d)
            q, k, v = _mla_prep(xf, pos, row(norm1_g[layer]), sc1, sh1, row(kv_norm_g), kmod[:, 1],
                                kmod[:, 0], b_w_dq[j], b_q_latent_norm[j], b_w_uq[j], b_q_norm[j],
                                kv_w_down, kv_latent_norm, kv_w_uk, kv_w_uv, kv_k_norm, seq)
            o, w_gu, w_dn = _mla_attention(q.reshape(bsz, seq, -1), k.reshape(bsz, seq, -1),
                                           v.reshape(bsz, seq, -1), expert_weights, layer)
            w_o = b_w_o[j]
        xm, h2, h2_slabs, logits = _post_attn(o.reshape(n, d), w_o.astype(BF16), xf, g1,
                                              row(norm2_g[layer]), sc2, sh2, moe_w_router[layer], seq)
        xf = _moe(h2, h2_slabs, xm, g2, logits, moe_router_bias[layer], w_gu, w_dn, s_gu, s_dn,
                  layer, seq)
    return xf.reshape(bsz, seq, d)
```

```python
import functools
import math

import numpy as np
import jax
import jax.numpy as jnp
from jax import lax
from jax.experimental import pallas as pl
from jax.experimental.pallas import tpu as pltpu

F32 = jnp.float32
BF16 = jnp.bfloat16
U32 = jnp.uint32

NORM_EPS = 1e-6
MASKED = -1e30
LOG2E = math.log2(math.e)

LANES = 128
SUBLANES = 8
VMEM_LIMIT = 56 * 1024 * 1024

A_HEAD_DIM = 64
DILATED_PAIRS = ((128, 1), (512, 4), (2048, 16))
QBLK = 256
NOPE_DIM = 64
ROPE_DIM = 32
QK_DIM = NOPE_DIM + ROPE_DIM
V_DIM = 64
KV_RANK = 256
ROPE_THETA = 10000.0
N_EXPERTS = 64
N_GROUPS = 8
TOPK_GROUPS = 4
TOP_K = 6
EXPERT_DIM = 256
ROUTED_SCALE = 2.5
MOE_TILE = 2048
EROWS = 256
CROWS = 256

ROW_TILE = 512


def _cparams(*sem):
    return pltpu.CompilerParams(dimension_semantics=sem, vmem_limit_bytes=VMEM_LIMIT)


def _silu(v):
    return v * jax.nn.sigmoid(v)


def _rms(v):
    return v * lax.rsqrt(jnp.mean(v * v, axis=-1, keepdims=True) + NORM_EPS)


def _dot_nt(a, b):
    return lax.dot_general(a, b, (((1,), (1,)), ((), ())), preferred_element_type=F32)


def _mod_kernel(c_ref, w_ref, b_ref, o_ref):
    cond = _silu(c_ref[...])
    o_ref[...] = jnp.dot(cond, w_ref[...], preferred_element_type=F32,
                         precision=lax.Precision.HIGHEST) + b_ref[...]


def _modulation(c, w, b, layer):
    bsz, d = c.shape
    m = w.shape[-1]
    tn = 1024
    return pl.pallas_call(
        _mod_kernel,
        grid=(m // tn,),
        in_specs=[pl.BlockSpec((bsz, d), lambda j: (0, 0)),
                  pl.BlockSpec((None, d, tn), lambda j: (layer, 0, j)),
                  pl.BlockSpec((None, 1, tn), lambda j: (layer, 0, j))],
        out_specs=pl.BlockSpec((bsz, tn), lambda j: (0, j)),
        out_shape=jax.ShapeDtypeStruct((bsz, m), F32),
        compiler_params=_cparams("arbitrary"),
    )(c, w, b.reshape(b.shape[0], 1, m))


def _qkv_kernel(x_ref, g_ref, sc_ref, sh_ref, w_ref, o_ref):
    h = _rms(x_ref[...]) * g_ref[...]
    h = h * (1.0 + sc_ref[...]) + sh_ref[...]
    o_ref[...] = jnp.dot(h.astype(BF16), w_ref[...], preferred_element_type=F32).astype(o_ref.dtype)


def _qkv_proj(xf, g, sc, sh, w, seq):
    n, d = xf.shape
    m = w.shape[1]
    tm = min(ROW_TILE, seq)
    per_b = seq // tm
    vec = pl.BlockSpec((None, 1, d), lambda i: (i // per_b, 0, 0))
    return pl.pallas_call(
        _qkv_kernel,
        grid=(n // tm,),
        in_specs=[pl.BlockSpec((tm, d), lambda i: (i, 0)),
                  pl.BlockSpec((1, d), lambda i: (0, 0)),
                  vec, vec,
                  pl.BlockSpec((d, m), lambda i: (0, 0))],
        out_specs=pl.BlockSpec((tm, m), lambda i: (i, 0)),
        out_shape=jax.ShapeDtypeStruct((n, m), BF16),
        compiler_params=_cparams("parallel"),
    )(xf, g, sc, sh, w)


ATT_SLOTS = 4


def _attention_blocks(seq, qk_of, v_ref, bias_ref, s_ref, p_ref, o_ref, lane_lo):
    nblk = seq // QBLK

    def slot(blk, h):
        return (blk % 2) * 2 + h

    def scores(blk):
        w = (blk + 1) * QBLK
        bias = bias_ref[:, seq - w:seq]
        for h in range(2):
            q, k = qk_of(blk, h, w)
            s_ref[slot(blk, h), :, 0:w] = _dot_nt(q, k) + bias

    for blk in range(nblk):
        w = (blk + 1) * QBLK
        scores(blk)
        denom = []
        for h in range(2):
            s = s_ref[slot(blk, h), :, 0:w]
            p = jnp.exp2(s - jnp.max(s, axis=-1, keepdims=True))
            denom.append(jnp.sum(p, axis=-1, keepdims=True))
            p_ref[slot(blk, h), :, 0:w] = p.astype(BF16)
        o_a, o_b = [jnp.dot(p_ref[slot(blk, h), :, 0:w], v_ref[0:w, :],
                            preferred_element_type=F32) / denom[h] for h in range(2)]
        o_ref[blk * QBLK:w, :] = jnp.where(lane_lo, o_a, o_b).astype(o_ref.dtype)


def _attn_scratch(seq):
    return [pltpu.VMEM((ATT_SLOTS, QBLK, seq), F32), pltpu.VMEM((ATT_SLOTS, QBLK, seq), BF16)]


def _cast_rider(weights, layer, grid):
    steps = grid[0] * grid[1]
    in_specs, out_specs, out_shapes = [], [], []
    for w in weights:
        n_exp = w.shape[1]
        assert n_exp % steps == 0
        per = n_exp // steps
        blk = (per,) + w.shape[2:]
        in_specs.append(pl.BlockSpec((None,) + blk, lambda b, h: (layer, b * grid[1] + h, 0, 0)))
        out_specs.append(pl.BlockSpec(blk, lambda b, h: (b * grid[1] + h, 0, 0)))
        out_shapes.append(jax.ShapeDtypeStruct(w.shape[1:], BF16))
    return in_specs, out_specs, out_shapes


def _split3(v):
    p1 = v.astype(BF16).astype(F32)
    p2 = (v - p1).astype(BF16).astype(F32)
    return p1, p2, v - p1 - p2


def _dilated_log_multiplicity(seq):
    a = np.arange(QBLK)[:, None]
    c = np.arange(seq)[None, :]
    d = (seq - QBLK) + a - c
    mult = np.zeros((QBLK, seq), np.int32)
    for win, dil in DILATED_PAIRS:
        mult += ((d >= 0) & (d % dil == 0) & (d <= win)).astype(np.int32)
    return np.where(mult > 0, np.log2(np.maximum(mult, 1)), MASKED).astype(np.float32)


N_FEAT = 9


def _dil_attn_kernel(q_ref, k_ref, v_ref, posc_ref, sfeat_ref, gq_ref, gk_ref, logm_ref,
                     wa_ref, wb_ref, o_ref, wa_out, wb_out, qa_ref, qb_ref, ks_ref, s_ref, p_ref):
    wa_out[...] = wa_ref[...].astype(BF16)
    wb_out[...] = wb_ref[...].astype(BF16)
    seq = q_ref.shape[0]
    lane = lax.broadcasted_iota(jnp.int32, (QBLK, LANES), 1)
    lane_lo = lane < A_HEAD_DIM
    piece = [functools.reduce(jnp.logical_or, [lane == i + 3 * j for j in range(N_FEAT // 3)])
             for i in range(3)]

    def head_norm(ref, r0, g_ref, mul):
        t = ref[r0:r0 + QBLK, :].astype(F32)
        sq = t * t
        ssa = jnp.sum(jnp.where(lane_lo, sq, 0.0), axis=-1, keepdims=True)
        ssb = jnp.sum(jnp.where(lane_lo, 0.0, sq), axis=-1, keepdims=True)
        r = jnp.where(lane_lo, lax.rsqrt(ssa / A_HEAD_DIM + NORM_EPS),
                      lax.rsqrt(ssb / A_HEAD_DIM + NORM_EPS))
        return t * r * (g_ref[...] * mul)

    rel0 = posc_ref[0:1, :]
    for r0 in range(0, seq, QBLK):
        rows = slice(r0, r0 + QBLK)
        qn = head_norm(q_ref, r0, gq_ref, A_HEAD_DIM ** -0.5 * LOG2E)
        qa_ref[rows, :LANES] = jnp.where(lane_lo, qn, 0.0).astype(BF16)
        qb_ref[rows, :LANES] = jnp.where(lane_lo, 0.0, qn).astype(BF16)
        qa_ref[rows, LANES:] = jnp.broadcast_to(sfeat_ref[0:1, :], (QBLK, LANES)).astype(BF16)
        qb_ref[rows, LANES:] = jnp.broadcast_to(sfeat_ref[1:2, :], (QBLK, LANES)).astype(BF16)
        ks_ref[rows, :LANES] = head_norm(k_ref, r0, gk_ref, 1.0).astype(BF16)
        p1, p2, p3 = _split3((posc_ref[rows, :] - rel0).astype(F32))
        feat = jnp.where(piece[0], p1, jnp.where(piece[1], p2, jnp.where(piece[2], p3, 0.0)))
        ks_ref[rows, LANES:] = feat.astype(BF16)

    def qk_of(blk, h, w):
        return (qa_ref, qb_ref)[h][w - QBLK:w, :], ks_ref[0:w, :]

    _attention_blocks(seq, qk_of, v_ref, logm_ref, s_ref, p_ref, o_ref, lane_lo)


def _dilated_attention(qkv, pos, q_norm, k_norm, cast_weights, layer):
    bsz, seq, three_d = qkv.shape
    d = three_d // 3
    n_pairs = d // LANES
    n_heads = d // A_HEAD_DIM
    slopes = jnp.exp2(-8.0 * jnp.arange(1, n_heads + 1, dtype=F32) / n_heads) * LOG2E
    sfeat = jnp.repeat(jnp.stack(_split3(slopes), axis=1), 3, axis=1)
    sfeat = jnp.pad(sfeat, ((0, 0), (0, LANES - N_FEAT))).reshape(n_pairs, 2, LANES)
    logm = jnp.asarray(_dilated_log_multiplicity(seq))
    gq = jnp.tile(q_norm.reshape(1, A_HEAD_DIM), (1, 2))
    gk = jnp.tile(k_norm.reshape(1, A_HEAD_DIM), (1, 2))
    blk = lambda off: pl.BlockSpec((None, seq, LANES), lambda b, h: (b, 0, off + h))
    const2 = pl.BlockSpec((1, LANES), lambda b, h: (0, 0))
    ext = pltpu.VMEM((seq, 2 * LANES), BF16)
    grid = (bsz, n_pairs)
    w_in, w_out, w_shapes = _cast_rider(cast_weights, layer, grid)
    return pl.pallas_call(
        _dil_attn_kernel,
        grid=grid,
        in_specs=[blk(0), blk(n_pairs), blk(2 * n_pairs),
                  pl.BlockSpec((None, seq, 1), lambda b, h: (b, 0, 0)),
                  pl.BlockSpec((None, 2, LANES), lambda b, h: (h, 0, 0)),
                  const2, const2,
                  pl.BlockSpec((QBLK, seq), lambda b, h: (0, 0))] + w_in,
        out_specs=[pl.BlockSpec((None, seq, LANES), lambda b, h: (b, 0, h))] + w_out,
        scratch_shapes=[ext, ext, ext] + _attn_scratch(seq),
        out_shape=[jax.ShapeDtypeStruct((bsz, seq, d), BF16)] + w_shapes,
        compiler_params=_cparams("parallel", "parallel"),
    )(qkv, qkv, qkv, pos.reshape(bsz, seq, 1), sfeat, gq, gk, logm, *cast_weights)


def _post_attn_kernel(o_ref, wo_ref, x_ref, g1_ref, n2_ref, sc_ref, sh_ref, wrh_ref, wrl_ref,
                      xm_ref, h_ref, hs_ref, lg_ref, il_ref):
    y = jnp.dot(o_ref[...], wo_ref[...], preferred_element_type=F32)
    xm = x_ref[...] + g1_ref[...] * y
    xm_ref[...] = xm
    h = _rms(xm) * n2_ref[...]
    h = h * (1.0 + sc_ref[...]) + sh_ref[...]
    h_ref[...] = h.astype(h_ref.dtype)
    hs_ref[...] = _rows_to_slab_rows(h, il_ref)
    h_hi = h.astype(BF16)
    h_lo = (h - h_hi.astype(F32)).astype(BF16)
    lg_ref[...] = (_dot_nt(wrh_ref[...], h_hi) + _dot_nt(wrh_ref[...], h_lo)
                   + _dot_nt(wrl_ref[...], h_hi))


def _post_attn(o, w_o, xf, g1, n2, sc2, sh2, w_router, seq):
    n, d = xf.shape
    ne = w_router.shape[1]
    tm = min(ROW_TILE, seq)
    per_b = seq // tm
    slab_rows = d // LANES
    wr_t = w_router.T.astype(F32)
    wr_hi = wr_t.astype(BF16)
    vec = pl.BlockSpec((None, 1, d), lambda i: (i // per_b, 0, 0))
    row = lambda w: pl.BlockSpec((tm, w), lambda i: (i, 0))
    return pl.pallas_call(
        _post_attn_kernel,
        grid=(n // tm,),
        in_specs=[row(d), pl.BlockSpec((d, d), lambda i: (0, 0)), row(d), vec,
                  pl.BlockSpec((1, d), lambda i: (0, 0)), vec, vec,
                  pl.BlockSpec((ne, d), lambda i: (0, 0)), pl.BlockSpec((ne, d), lambda i: (0, 0))],
        out_specs=[row(d), row(d), pl.BlockSpec((tm * slab_rows, LANES), lambda i: (i, 0)),
                   pl.BlockSpec((ne, tm), lambda i: (0, i))],
        out_shape=[jax.ShapeDtypeStruct((n, d), F32), jax.ShapeDtypeStruct((n, d), BF16),
                   jax.ShapeDtypeStruct((n * slab_rows, LANES), BF16),
                   jax.ShapeDtypeStruct((ne, n), F32)],
        scratch_shapes=[pltpu.VMEM((tm * slab_rows, LANES), F32)],
        compiler_params=_cparams("parallel"),
    )(o, w_o, xf, g1, n2, sc2, sh2, wr_hi, (wr_t - wr_hi.astype(F32)).astype(BF16))


def _moe_rows(tile):
    rows = tile * TOP_K + N_EXPERTS * (SUBLANES - 1) + EROWS
    return -(-rows // SUBLANES) * SUBLANES


def _first_index(hit, index, limit):
    return jnp.min(jnp.where(hit, index, limit), axis=0, keepdims=True)


def _route_kernel(lg_ref, bias_ref, tri_ref, dest_ref, w_ref, seg_ref, cnt_ref, *, row_pitch):
    t = lg_ref.shape[1]
    per_g = N_EXPERTS // N_GROUPS
    sub = lax.broadcasted_iota(jnp.int32, (per_g, t), 0).astype(F32)
    neg = -jnp.inf
    scores, sel = [], []
    for g in range(N_GROUPS):
        rows = slice(g * per_g, (g + 1) * per_g)
        sc = jax.nn.sigmoid(lg_ref[rows, :])
        scores.append(sc)
        sel.append(sc + bias_ref[rows, :])

    gs = []
    for g in range(N_GROUPS):
        m1 = jnp.max(sel[g], axis=0, keepdims=True)
        first = _first_index(sel[g] == m1, sub, float(per_g))
        m2 = jnp.max(jnp.where(sub == first, neg, sel[g]), axis=0, keepdims=True)
        gs.append(m1 + m2)
    gsc = jnp.concatenate(gs, axis=0)
    gsub = lax.broadcasted_iota(jnp.int32, (N_GROUPS, t), 0).astype(F32)
    gkeep = jnp.zeros((N_GROUPS, t), F32)
    for _ in range(TOPK_GROUPS):
        m = jnp.max(gsc, axis=0, keepdims=True)
        pick = gsub == _first_index(gsc == m, gsub, float(N_GROUPS))
        gkeep = jnp.where(pick, 1.0, gkeep)
        gsc = jnp.where(pick, neg, gsc)

    eidx = [sub + float(g * per_g) for g in range(N_GROUPS)]
    masked = [jnp.where(jnp.broadcast_to(gkeep[g:g + 1, :], (per_g, t)) > 0.5, sel[g], neg)
              for g in range(N_GROUPS)]
    chosen = [jnp.zeros((per_g, t), F32) for _ in range(N_GROUPS)]
    wsel = [jnp.zeros((per_g, t), F32) for _ in range(N_GROUPS)]
    picked = []
    for _ in range(TOP_K):
        m = jnp.max(masked[0], axis=0, keepdims=True)
        for g in range(1, N_GROUPS):
            m = jnp.maximum(m, jnp.max(masked[g], axis=0, keepdims=True))
        e_first = _first_index(masked[0] == m, eidx[0], float(N_EXPERTS))
        for g in range(1, N_GROUPS):
            e_first = jnp.minimum(e_first, _first_index(masked[g] == m, eidx[g], float(N_EXPERTS)))
        picked.append(e_first)
        for g in range(N_GROUPS):
            pick = eidx[g] == e_first
            chosen[g] = jnp.where(pick, 1.0, chosen[g])
            wsel[g] = jnp.where(pick, scores[g], wsel[g])
            masked[g] = jnp.where(pick, neg, masked[g])

    denom = jnp.sum(wsel[0], axis=0, keepdims=True)
    for g in range(1, N_GROUPS):
        denom = denom + jnp.sum(wsel[g], axis=0, keepdims=True)

    c_all = jnp.concatenate(chosen, axis=0)
    rank = jnp.dot(c_all.astype(BF16), tri_ref[...], preferred_element_type=F32)
    counts = jnp.sum(c_all, axis=1, keepdims=True)
    padded = jnp.ceil(counts / SUBLANES) * SUBLANES
    below = (lax.broadcasted_iota(jnp.int32, (N_EXPERTS, N_EXPERTS), 1)
             < lax.broadcasted_iota(jnp.int32, (N_EXPERTS, N_EXPERTS), 0)).astype(F32)
    seg = jnp.dot(below, jnp.broadcast_to(padded, (N_EXPERTS, LANES)), preferred_element_type=F32,
                  precision=lax.Precision.HIGHEST)
    row_of = seg[:, :1] + rank

    dest_rows, w_rows = [], []
    for k in range(TOP_K):
        d_k = jnp.zeros((1, t), F32)
        w_k = jnp.zeros((1, t), F32)
        for g in range(N_GROUPS):
            hit = eidx[g] == picked[k]
            d_k = d_k + jnp.sum(jnp.where(hit, row_of[g * per_g:(g + 1) * per_g, :], 0.0), axis=0,
                                keepdims=True)
            w_k = w_k + jnp.sum(jnp.where(hit, wsel[g], 0.0), axis=0, keepdims=True)
        dest_rows.append(d_k)
        w_rows.append(w_k / denom * ROUTED_SCALE)
    fill = [jnp.zeros((SUBLANES - TOP_K, t), F32)]
    dest8 = (jnp.concatenate(dest_rows + fill, axis=0) * row_pitch).astype(jnp.int32)
    w8 = jnp.concatenate(w_rows + fill, axis=0)
    for j in range(t // LANES):
        dest_ref[j] = dest8[:, j * LANES:(j + 1) * LANES]
        w_ref[j] = w8[:, j * LANES:(j + 1) * LANES]
    seg_ref[...] = seg.astype(jnp.int32)
    cnt_ref[...] = jnp.broadcast_to(counts, (N_EXPERTS, LANES)).astype(jnp.int32)


def _route(logits_t, router_bias, tile, row_pitch):
    ne, n = logits_t.shape
    nt = n // tile
    tri = jnp.asarray(np.triu(np.ones((tile, tile), np.float32), 1), dtype=BF16)
    per_tile = lambda rows, width: pl.BlockSpec((None, rows, width), lambda i: (i, 0, 0))
    table = pl.BlockSpec((None, tile // LANES, SUBLANES, LANES), lambda i: (i, 0, 0, 0))
    dest, wts, seg, cnt = pl.pallas_call(
        functools.partial(_route_kernel, row_pitch=row_pitch),
        grid=(nt,),
        in_specs=[pl.BlockSpec((ne, tile), lambda i: (0, i)),
                  pl.BlockSpec((ne, 1), lambda i: (0, 0)),
                  pl.BlockSpec((tile, tile), lambda i: (0, 0))],
        out_specs=[table, table, per_tile(ne, LANES), per_tile(ne, LANES)],
        out_shape=[jax.ShapeDtypeStruct((nt, tile // LANES, SUBLANES, LANES), jnp.int32),
                   jax.ShapeDtypeStruct((nt, tile // LANES, SUBLANES, LANES), F32),
                   jax.ShapeDtypeStruct((nt, ne, LANES), jnp.int32),
                   jax.ShapeDtypeStruct((nt, ne, LANES), jnp.int32)],
        compiler_params=_cparams("parallel"),
    )(logits_t, router_bias.astype(F32).reshape(ne, 1), tri)
    return seg[:, :, 0].reshape(-1), cnt[:, :, 0].reshape(-1), dest, wts


def _swiglu(xb, wgu_ref, wdn_ref):
    gu = jnp.dot(xb, wgu_ref[...], preferred_element_type=F32)
    half = wgu_ref.shape[1] // 2
    act = _silu(gu[:, :half]) * gu[:, half:]
    return jnp.dot(act.astype(BF16), wdn_ref[...], preferred_element_type=F32)


def _slabs_to_rows(p, il_ref, r):
    n = il_ref.shape[0] // r
    il_ref[...] = pltpu.bitcast(p, BF16).astype(F32)
    return jnp.concatenate([il_ref[pl.ds(2 * c + h, r, stride=n), :]
                            for h in range(2) for c in range(n // 2)], axis=1)


def _rows_to_slab_rows(v, il_ref):
    r = v.shape[0]
    n = il_ref.shape[0] // r
    for h in range(2):
        for c in range(n // 2):
            lane0 = (h * (n // 2) + c) * LANES
            il_ref[pl.ds(2 * c + h, r, stride=n), :] = v[:, lane0:lane0 + LANES]
    return il_ref[...].astype(BF16)


def _rows_to_slabs(v, il_ref):
    return pltpu.bitcast(_rows_to_slab_rows(v, il_ref), U32)


def _moe_kernel(start_ref, count_ref, hp_ref, hb_ref, dest_ref, wrow_ref, wgu_hbm, wdn_hbm,
                sgu_ref, sdn_ref, xm_ref, g2_ref, o_ref, xy_ref, il_ref, yp_ref, zf_ref,
                comb_ref, wgu_buf, wdn_buf, wsem):
    t = pl.program_id(0)
    s = pl.program_id(1)
    slab = xm_ref.shape[1] // (2 * LANES)
    tile = hp_ref.shape[0] // (2 * slab)
    group_rows = SUBLANES * slab

    def weight_copies(e, slot):
        return (pltpu.make_async_copy(wgu_hbm.at[e], wgu_buf.at[slot], wsem.at[0, slot]),
                pltpu.make_async_copy(wdn_hbm.at[e], wdn_buf.at[slot], wsem.at[1, slot]))

    def slab_at(ref, first):
        return ref.at[pl.ds(pl.multiple_of(first, slab), slab), :]

    @pl.when(s == 0)
    def _first_weights():
        for p in range(2):
            for cp in weight_copies(p, p):
                cp.start()

    @pl.when(jnp.logical_and(t == 0, s == 0))
    def _clear():
        xy_ref[...] = jnp.zeros(xy_ref.shape, U32)

    @pl.when(s == 0)
    def _dispatch():
        def group(j, carry):
            for sub in range(LANES // SUBLANES):
                p0 = pl.multiple_of((j * (LANES // SUBLANES) + sub) * 2 * group_rows, 2 * group_rows)
                rows = pltpu.bitcast(hp_ref[pl.ds(p0, 2 * group_rows), :], U32)
                for u in range(SUBLANES):
                    for k in range(TOP_K):
                        slab_at(xy_ref, dest_ref[j, k, sub * SUBLANES + u])[...] = (
                            rows[u * slab:(u + 1) * slab, :])
            return carry

        lax.fori_loop(0, tile // LANES, group, 0)

    @pl.when(s == 0)
    def _experts():
        def chunk_base(e, c):
            return pl.multiple_of((start_ref[t * N_EXPERTS + e] + c * EROWS) * slab, group_rows)

        def load_rows(base, p):
            return _slabs_to_rows(xy_ref[pl.ds(base, EROWS * slab), :], il_ref.at[p], EROWS).astype(BF16)

        def put_rows(p, base, valid):
            def put(j, c2):
                r0 = pl.multiple_of(j * group_rows, group_rows)
                xy_ref[pl.ds(base + r0, group_rows), :] = yp_ref[p, pl.ds(r0, group_rows), :]
                return c2

            lax.fori_loop(0, (jnp.clip(valid, 0, EROWS) + SUBLANES - 1) // SUBLANES, put, 0)

        def expert_pair(i, carry0):
            wslot = lax.rem(i, 2) * 2

            @pl.when(i + 1 < N_EXPERTS // 2)
            def _prefetch():
                for p in range(2):
                    for cp in weight_copies(2 * i + 2 + p, 2 - wslot + p):
                        cp.start()

            for p in range(2):
                for cp in weight_copies(2 * i + p, wslot + p):
                    cp.wait()
            counts = [count_ref[t * N_EXPERTS + 2 * i + p] for p in range(2)]
            bases = [chunk_base(2 * i + p, 0) for p in range(2)]
            xs = [load_rows(bases[p], p) for p in range(2)]
            ys = [_swiglu(xs[p], wgu_buf.at[wslot + p], wdn_buf.at[wslot + p]) for p in range(2)]
            for p in range(2):
                yp_ref[p] = _rows_to_slabs(ys[p], il_ref.at[p])
            for p in range(2):
                put_rows(p, bases[p], counts[p])

            for p in range(2):
                def more(c, carry, p=p):
                    base = chunk_base(2 * i + p, c)
                    y = _swiglu(load_rows(base, p), wgu_buf.at[wslot + p], wdn_buf.at[wslot + p])
                    yp_ref[p] = _rows_to_slabs(y, il_ref.at[p])
                    put_rows(p, base, counts[p] - c * EROWS)
                    return carry

                lax.fori_loop(1, (counts[p] + EROWS - 1) // EROWS, more, 0)
            return carry0

        lax.fori_loop(0, N_EXPERTS // 2, expert_pair, 0)

    @pl.when(s >= 1)
    def _combine():
        sb = s - 1

        def group(j, carry):
            tj = sb * (CROWS // LANES) + j
            for sub in range(LANES // SUBLANES):
                r0 = pl.multiple_of(j * LANES + sub * SUBLANES, SUBLANES)
                half = sub % 2
                for u in range(SUBLANES):
                    lane = sub * SUBLANES + u
                    acc = None
                    for k in range(TOP_K):
                        z = pltpu.bitcast(slab_at(xy_ref, dest_ref[tj, k, lane])[...], BF16).astype(F32)
                        acc = wrow_ref[tj, k, lane] * z if acc is None else acc + wrow_ref[tj, k, lane] * z
                    zf_ref[half, u * 2 * slab:(u + 1) * 2 * slab, :] = acc
                comb_ref[pl.ds(r0, SUBLANES), :] = jnp.concatenate(
                    [zf_ref[half, pl.ds(2 * c + h, SUBLANES, stride=2 * slab), :]
                     for h in range(2) for c in range(slab)], axis=1)
            return carry

        lax.fori_loop(0, CROWS // LANES, group, 0)
        shared = _swiglu(hb_ref[...], sgu_ref, sdn_ref)
        o_ref[...] = xm_ref[...] + g2_ref[...] * (comb_ref[...] + shared)


def _moe(h, h_slabs, xm, g2, logits_t, router_bias, w_gu, w_dn, s_gu, s_dn, layer, seq):
    n, d = h.shape
    slab = d // (2 * LANES)
    tile = min(MOE_TILE, seq)
    nt = n // tile
    per_b = seq // tile
    csteps = tile // CROWS
    assert seq % tile == 0 and tile % CROWS == 0
    assert tile % LANES == 0 and CROWS % LANES == 0
    seg_start, seg_count, dest_rows, w_rows = _route(logits_t, router_bias, tile, slab)
    rows = _moe_rows(tile)
    table = pl.BlockSpec((None, tile // LANES, SUBLANES, LANES), lambda t, s, a, b: (t, 0, 0, 0),
                         memory_space=pltpu.SMEM)
    cst = lambda s: jnp.maximum(s - 1, 0)
    crow = pl.BlockSpec((CROWS, d), lambda t, s, a, b: (t * csteps + cst(s), 0))
    hbm = pl.BlockSpec(memory_space=pl.ANY)
    single = pl.Buffered(1)
    return pl.pallas_call(
        _moe_kernel,
        grid_spec=pltpu.PrefetchScalarGridSpec(
            num_scalar_prefetch=2,
            grid=(nt, 1 + csteps),
            in_specs=[pl.BlockSpec((tile * 2 * slab, LANES), lambda t, s, a, b: (t, 0),
                                   pipeline_mode=single),
                      crow, table, table,
                      hbm, hbm,
                      pl.BlockSpec((None,) + s_gu.shape[1:], lambda t, s, a, b: (layer, 0, 0),
                                   pipeline_mode=single),
                      pl.BlockSpec((None,) + s_dn.shape[1:], lambda t, s, a, b: (layer, 0, 0),
                                   pipeline_mode=single),
                      crow,
                      pl.BlockSpec((None, 1, d), lambda t, s, a, b: (t // per_b, 0, 0))],
            out_specs=crow,
            scratch_shapes=[pltpu.VMEM((rows * slab, LANES), U32),
                            pltpu.VMEM((2, EROWS * 2 * slab, LANES), F32),
                            pltpu.VMEM((2, EROWS * slab, LANES), U32),
                            pltpu.VMEM((2, SUBLANES * 2 * slab, LANES), F32),
                            pltpu.VMEM((CROWS, d), F32),
                            pltpu.VMEM((4, d, 2 * EXPERT_DIM), BF16),
                            pltpu.VMEM((4, EXPERT_DIM, d), BF16),
                            pltpu.SemaphoreType.DMA((2, 4))]),
        out_shape=jax.ShapeDtypeStruct((n, d), F32),
        compiler_params=_cparams("arbitrary", "arbitrary"),
    )(seg_start, seg_count, h_slabs, h, dest_rows, w_rows, w_gu, w_dn, s_gu, s_dn, xm, g2)


def _mla_prep_kernel(x_ref, n1_ref, sc1_ref, sh1_ref, nk_ref, sck_ref, shk_ref,
                     wdq_ref, gql_ref, wuq_ref, wuqs_ref, wdown_ref, gkl_ref, wk_ref, gk_ref, wuv_ref,
                     qa_ref, qb_ref, kc_ref, ksa_ref, ksb_ref, q_ref, k_ref, v_ref):
    xn = _rms(x_ref[...])
    hq = xn * n1_ref[...] * (1.0 + sc1_ref[...]) + sh1_ref[...]
    hk = xn * nk_ref[...] * (1.0 + sck_ref[...]) + shk_ref[...]
    n_heads = q_ref.shape[1] // LANES
    half = ROPE_DIM // 2

    cq = (_rms(jnp.dot(hq.astype(BF16), wdq_ref[...], preferred_element_type=F32)) * gql_ref[...]).astype(BF16)
    q_raw = jnp.dot(cq, wuq_ref[...], preferred_element_type=F32)
    q_swp = jnp.dot(cq, wuqs_ref[...], preferred_element_type=F32)
    qa, qb = qa_ref[...], qb_ref[...]
    for h in range(n_heads):
        sl = slice(h * LANES, (h + 1) * LANES)
        t = q_raw[:, sl]
        r = lax.rsqrt(jnp.sum(t * t, axis=-1, keepdims=True) / QK_DIM + NORM_EPS)
        q_ref[:, sl] = (r * (t * qa + q_swp[:, sl] * qb)).astype(q_ref.dtype)

    down = jnp.dot(hk.astype(BF16), wdown_ref[...], preferred_element_type=F32)
    ckv = (_rms(down[:, :KV_RANK]) * gkl_ref[...]).astype(BF16)
    pe = down[:, KV_RANK:]
    ss_pe = jnp.sum(pe * pe, axis=-1, keepdims=True)
    rot = (pe * kc_ref[...] + pltpu.roll(pe, LANES - half, 1) * ksa_ref[...]
           + pltpu.roll(pe, half, 1) * ksb_ref[...])
    rot_hi = rot.astype(BF16)
    rot_lo = (rot - rot_hi.astype(F32)).astype(BF16)
    k_raw = jnp.dot(jnp.concatenate([ckv, rot_hi, rot_lo], axis=1), wk_ref[...],
                    preferred_element_type=F32)
    nope = lax.broadcasted_iota(jnp.int32, (x_ref.shape[0], LANES), 1) < NOPE_DIM
    for h in range(n_heads):
        sl = slice(h * LANES, (h + 1) * LANES)
        t = k_raw[:, sl]
        ss = jnp.sum(jnp.where(nope, t * t, 0.0), axis=-1, keepdims=True) + ss_pe
        k_ref[:, sl] = (t * lax.rsqrt(ss / QK_DIM + NORM_EPS) * gk_ref[...]).astype(k_ref.dtype)
    v_ref[...] = jnp.dot(ckv, wuv_ref[...], preferred_element_type=F32).astype(v_ref.dtype)


def _pad_heads(w, n_heads, width):
    r = w.shape[0]
    return jnp.pad(w.reshape(r, n_heads, width), ((0, 0), (0, 0), (0, LANES - width))).reshape(r, n_heads * LANES)


def _swap_rope_halves(a):
    half = ROPE_DIM // 2
    lead = a.shape[:-1]
    return jnp.concatenate([jnp.zeros(lead + (NOPE_DIM,), a.dtype), a[..., NOPE_DIM + half:QK_DIM],
                            a[..., NOPE_DIM:NOPE_DIM + half],
                            jnp.zeros(lead + (LANES - QK_DIM,), a.dtype)], axis=-1)


def _mla_prep(xf, pos, n1, sc1, sh1, nk, sck, shk, w_dq, g_ql, w_uq, g_q, w_down, g_kl, w_uk, w_uv,
              g_k, seq):
    n, d = xf.shape
    n_heads = w_uv.shape[1] // V_DIM
    q_rank = w_dq.shape[1]
    hw = n_heads * LANES
    tm = min(ROW_TILE, seq)
    per_b = seq // tm

    half = ROPE_DIM // 2
    inv = jnp.power(ROPE_THETA, -jnp.arange(half, dtype=F32) / half)
    ang = pos.astype(F32).reshape(n, 1) * inv
    cos, sin = jnp.cos(ang), jnp.sin(ang)
    ones = jnp.ones((n, NOPE_DIM), F32)
    zeros = lambda wdt: jnp.zeros((n, wdt), F32)
    tail = LANES - QK_DIM
    pad_gain = lambda g: jnp.pad(g.reshape(1, QK_DIM), ((0, 0), (0, tail)))
    q_mul = QK_DIM ** -0.5 * LOG2E
    qa_t = jnp.concatenate([ones, cos, cos, zeros(tail)], axis=1) * pad_gain(g_q) * q_mul
    qb_t = (jnp.concatenate([zeros(NOPE_DIM), -sin, sin, zeros(tail)], axis=1)
            * _swap_rope_halves(g_q.reshape(1, QK_DIM)) * q_mul)
    gk1, gk2 = g_k[NOPE_DIM:NOPE_DIM + half], g_k[NOPE_DIM + half:]
    rest = zeros(LANES - ROPE_DIM)
    kc_t = jnp.concatenate([cos * gk1, cos * gk2, rest], axis=1)
    ksa_t = jnp.concatenate([-sin * gk2, zeros(half), rest], axis=1)
    ksb_t = jnp.concatenate([zeros(half), sin * gk1, rest], axis=1)
    gk_mask = jnp.concatenate([g_k[:NOPE_DIM], jnp.ones((ROPE_DIM,), F32),
                               jnp.zeros((tail,), F32)]).reshape(1, LANES)

    wuq_p = _pad_heads(w_uq, n_heads, QK_DIM).astype(BF16)
    wuq_s = _swap_rope_halves(w_uq.reshape(q_rank, n_heads, QK_DIM)).reshape(q_rank, hw).astype(BF16)
    wdown_p = jnp.pad(w_down, ((0, 0), (0, LANES - ROPE_DIM))).astype(BF16)
    eye = jnp.pad(jnp.eye(ROPE_DIM, dtype=F32), ((0, LANES - ROPE_DIM), (NOPE_DIM, tail)))
    pe_rows = jnp.tile(eye, (1, n_heads))
    wk = jnp.concatenate([_pad_heads(w_uk, n_heads, NOPE_DIM), pe_rows, pe_rows], axis=0).astype(BF16)

    vec = pl.BlockSpec((None, 1, d), lambda i: (i // per_b, 0, 0))
    full = lambda a: pl.BlockSpec(a.shape, lambda i: (0,) * a.ndim)
    row = lambda wdt: pl.BlockSpec((tm, wdt), lambda i: (i, 0))
    args = [xf, n1, sc1, sh1, nk, sck, shk,
            w_dq.astype(BF16), g_ql.reshape(1, q_rank), wuq_p, wuq_s,
            wdown_p, g_kl.reshape(1, KV_RANK), wk, gk_mask, w_uv.astype(BF16),
            qa_t, qb_t, kc_t, ksa_t, ksb_t]
    specs = [row(d), full(n1), vec, vec, full(nk), vec, vec] + [full(a) for a in args[7:16]] + [row(LANES)] * 5
    return pl.pallas_call(
        _mla_prep_kernel,
        grid=(n // tm,),
        in_specs=specs,
        out_specs=[row(hw), row(hw), row(n_heads * V_DIM)],
        out_shape=[jax.ShapeDtypeStruct((n, hw), BF16), jax.ShapeDtypeStruct((n, hw), BF16),
                   jax.ShapeDtypeStruct((n, n_heads * V_DIM), BF16)],
        compiler_params=_cparams("parallel"),
    )(*args)


def _causal_bias(seq):
    a = np.arange(QBLK)[:, None]
    c = np.arange(seq)[None, :]
    return np.where(c <= (seq - QBLK) + a, 0.0, MASKED).astype(np.float32)


def _mla_attn_kernel(q_ref, k_ref, v_ref, mask_ref, wa_ref, wb_ref, o_ref, wa_out, wb_out,
                     s_ref, p_ref):
    wa_out[...] = wa_ref[...].astype(BF16)
    wb_out[...] = wb_ref[...].astype(BF16)
    seq = q_ref.shape[0]
    lane_lo = lax.broadcasted_iota(jnp.int32, (QBLK, LANES), 1) < V_DIM

    def qk_of(blk, h, w):
        lanes = slice(h * LANES, (h + 1) * LANES)
        return q_ref[w - QBLK:w, lanes], k_ref[0:w, lanes]

    _attention_blocks(seq, qk_of, v_ref, mask_ref, s_ref, p_ref, o_ref, lane_lo)


def _mla_attention(q, k, v, cast_weights, layer):
    bsz, seq, hw = q.shape
    n_pairs = hw // (2 * LANES)
    qk = pl.BlockSpec((None, seq, 2 * LANES), lambda b, h: (b, 0, h))
    vo = pl.BlockSpec((None, seq, LANES), lambda b, h: (b, 0, h))
    grid = (bsz, n_pairs)
    w_in, w_out, w_shapes = _cast_rider(cast_weights, layer, grid)
    return pl.pallas_call(
        _mla_attn_kernel,
        grid=grid,
        in_specs=[qk, qk, vo, pl.BlockSpec((QBLK, seq), lambda b, h: (0, 0))] + w_in,
        out_specs=[vo] + w_out,
        scratch_shapes=_attn_scratch(seq),
        out_shape=[jax.ShapeDtypeStruct(v.shape, BF16)] + w_shapes,
        compiler_params=_cparams("parallel", "parallel"),
    )(q, k, v, jnp.asarray(_causal_bias(seq)), *cast_weights)


def kernel(x, c, pos, ada_w, ada_b, norm1_g, norm2_g, a_w_qkv, a_q_norm, a_k_norm, a_w_o, kv_ada_w, kv_ada_b, kv_norm_g, kv_w_down, kv_latent_norm, kv_w_uk, kv_w_uv, kv_k_norm, b_w_dq, b_q_latent_norm, b_w_uq, b_q_norm, b_w_o, moe_w_router, moe_router_bias, moe_w_gate_up, moe_w_down, moe_shared_gate_up, moe_shared_down):
    bsz, seq, d = x.shape
    n = bsz * seq
    depth = ada_w.shape[0]
    n_a = a_w_qkv.shape[0]
    assert seq % QBLK == 0
    xf = x.reshape(n, d)

    expert_weights = (moe_w_gate_up, moe_w_down)
    s_gu = moe_shared_gate_up.astype(BF16)
    s_dn = moe_shared_down.astype(BF16)
    row = lambda g: g.reshape(1, d)

    for layer in range(depth):
        mod = _modulation(c, ada_w, ada_b, layer).reshape(bsz, 6, 1, d)
        sh1, sc1, g1, sh2, sc2, g2 = [mod[:, i] for i in range(6)]
        if layer < n_a:
            qkv = _qkv_proj(xf, row(norm1_g[layer]), sc1, sh1, a_w_qkv[layer].astype(BF16), seq)
            o, w_gu, w_dn = _dilated_attention(qkv.reshape(bsz, seq, -1), pos, a_q_norm[layer],
                                               a_k_norm[layer], expert_weights, layer)
            w_o = a_w_o[layer]
        else:
            j = layer - n_a
            kmod = _modulation(c, kv_ada_w[None], kv_ada_b[None], 0).reshape(bsz, 2, 1, d)
            q, k, v = _mla_prep(xf, pos, row(norm1_g[layer]), sc1, sh1, row(kv_norm_g), kmod[:, 1],
                                kmod[:, 0], b_w_dq[j], b_q_latent_norm[j], b_w_uq[j], b_q_norm[j],
                                kv_w_down, kv_latent_norm, kv_w_uk, kv_w_uv, kv_k_norm, seq)
            o, w_gu, w_dn = _mla_attention(q.reshape(bsz, seq, -1), k.reshape(bsz, seq, -1),
                                           v.reshape(bsz, seq, -1), expert_weights, layer)
            w_o = b_w_o[j]
        xm, h2, h2_slabs, logits = _post_attn(o.reshape(n, d), w_o.astype(BF16), xf, g1,
                                              row(norm2_g[layer]), sc2, sh2, moe_w_router[layer], seq)
        xf = _moe(h2, h2_slabs, xm, g2, logits, moe_router_bias[layer], w_gu, w_dn, s_gu, s_dn,
                  layer, seq)
    return xf.reshape(bsz, seq, d)
```

```python
import functools
import math

import numpy as np
import jax
import jax.numpy as jnp
from jax import lax
from jax.experimental import pallas as pl
from jax.experimental.pallas import tpu as pltpu

F32 = jnp.float32
BF16 = jnp.bfloat16
U32 = jnp.uint32

NORM_EPS = 1e-6
MASKED = -1e30
LOG2E = math.log2(math.e)

LANES = 128
SUBLANES = 8
VMEM_LIMIT = 56 * 1024 * 1024

A_HEAD_DIM = 64
DILATED_PAIRS = ((128, 1), (512, 4), (2048, 16))
QBLK = 256
NOPE_DIM = 64
ROPE_DIM = 32
QK_DIM = NOPE_DIM + ROPE_DIM
V_DIM = 64
KV_RANK = 256
ROPE_THETA = 10000.0
N_EXPERTS = 64
N_GROUPS = 8
TOPK_GROUPS = 4
TOP_K = 6
EXPERT_DIM = 256
ROUTED_SCALE = 2.5
MOE_TILE = 2048
EROWS = 256
CROWS = 256

ROW_TILE = 512


def _cparams(*sem):
    return pltpu.CompilerParams(dimension_semantics=sem, vmem_limit_bytes=VMEM_LIMIT)


def _silu(v):
    return v * jax.nn.sigmoid(v)


def _rms(v):
    return v * lax.rsqrt(jnp.mean(v * v, axis=-1, keepdims=True) + NORM_EPS)


def _dot_nt(a, b):
    return lax.dot_general(a, b, (((1,), (1,)), ((), ())), preferred_element_type=F32)


def _mod_kernel(c_ref, w_ref, b_ref, o_ref):
    cond = _silu(c_ref[...])
    o_ref[...] = jnp.dot(cond, w_ref[...], preferred_element_type=F32,
                         precision=lax.Precision.HIGHEST) + b_ref[...]


def _modulation(c, w, b, layer):
    bsz, d = c.shape
    m = w.shape[-1]
    tn = 1024
    return pl.pallas_call(
        _mod_kernel,
        grid=(m // tn,),
        in_specs=[pl.BlockSpec((bsz, d), lambda j: (0, 0)),
                  pl.BlockSpec((None, d, tn), lambda j: (layer, 0, j)),
                  pl.BlockSpec((None, 1, tn), lambda j: (layer, 0, j))],
        out_specs=pl.BlockSpec((bsz, tn), lambda j: (0, j)),
        out_shape=jax.ShapeDtypeStruct((bsz, m), F32),
        compiler_params=_cparams("arbitrary"),
    )(c, w, b.reshape(b.shape[0], 1, m))


def _qkv_kernel(x_ref, g_ref, sc_ref, sh_ref, w_ref, o_ref):
    h = _rms(x_ref[...]) * g_ref[...]
    h = h * (1.0 + sc_ref[...]) + sh_ref[...]
    o_ref[...] = jnp.dot(h.astype(BF16), w_ref[...], preferred_element_type=F32).astype(o_ref.dtype)


def _qkv_proj(xf, g, sc, sh, w, seq):
    n, d = xf.shape
    m = w.shape[1]
    tm = min(ROW_TILE, seq)
    per_b = seq // tm
    vec = pl.BlockSpec((None, 1, d), lambda i: (i // per_b, 0, 0))
    return pl.pallas_call(
        _qkv_kernel,
        grid=(n // tm,),
        in_specs=[pl.BlockSpec((tm, d), lambda i: (i, 0)),
                  pl.BlockSpec((1, d), lambda i: (0, 0)),
                  vec, vec,
                  pl.BlockSpec((d, m), lambda i: (0, 0))],
        out_specs=pl.BlockSpec((tm, m), lambda i: (i, 0)),
        out_shape=jax.ShapeDtypeStruct((n, m), BF16),
        compiler_params=_cparams("parallel"),
    )(xf, g, sc, sh, w)


ATT_SLOTS = 4


def _attention_blocks(seq, qk_of, v_ref, bias_ref, s_ref, p_ref, ve_ref, o_ref, lane_lo):
    nblk = seq // QBLK

    for r0 in range(0, seq, QBLK):
        v = v_ref[r0:r0 + QBLK, :].astype(F32)
        ve_ref[0, r0:r0 + QBLK, :] = jnp.where(lane_lo, v, 1.0).astype(BF16)
        ve_ref[1, r0:r0 + QBLK, :] = jnp.where(lane_lo, 1.0, v).astype(BF16)

    def slot(blk, h):
        return (blk % 2) * 2 + h

    def scores(blk):
        w = (blk + 1) * QBLK
        bias = bias_ref[:, seq - w:seq]
        for h in range(2):
            q, k = qk_of(blk, h, w)
            s_ref[slot(blk, h), :, 0:w] = _dot_nt(q, k) + bias

    for blk in range(nblk):
        w = (blk + 1) * QBLK
        scores(blk)
        for h in range(2):
            s = s_ref[slot(blk, h), :, 0:w]
            p_ref[slot(blk, h), :, 0:w] = jnp.exp2(s - jnp.max(s, axis=-1, keepdims=True)).astype(BF16)
        acc_a, acc_b = [jnp.dot(p_ref[slot(blk, h), :, 0:w], ve_ref[h, 0:w, :],
                                preferred_element_type=F32) for h in range(2)]
        sums = pltpu.roll(jnp.where(lane_lo, acc_b, acc_a), LANES // 2, 1)
        o_ref[blk * QBLK:w, :] = (jnp.where(lane_lo, acc_a, acc_b) / sums).astype(o_ref.dtype)


def _attn_scratch(seq):
    return [pltpu.VMEM((ATT_SLOTS, QBLK, seq), F32), pltpu.VMEM((ATT_SLOTS, QBLK, seq), BF16),
            pltpu.VMEM((2, seq, LANES), BF16)]


def _cast_rider(weights, layer, grid):
    steps = grid[0] * grid[1]
    in_specs, out_specs, out_shapes = [], [], []
    for w in weights:
        n_exp = w.shape[1]
        assert n_exp % steps == 0
        per = n_exp // steps
        blk = (per,) + w.shape[2:]
        in_specs.append(pl.BlockSpec((None,) + blk, lambda b, h: (layer, b * grid[1] + h, 0, 0)))
        out_specs.append(pl.BlockSpec(blk, lambda b, h: (b * grid[1] + h, 0, 0)))
        out_shapes.append(jax.ShapeDtypeStruct(w.shape[1:], BF16))
    return in_specs, out_specs, out_shapes


def _split3(v):
    p1 = v.astype(BF16).astype(F32)
    p2 = (v - p1).astype(BF16).astype(F32)
    return p1, p2, v - p1 - p2


def _dilated_log_multiplicity(seq):
    a = np.arange(QBLK)[:, None]
    c = np.arange(seq)[None, :]
    d = (seq - QBLK) + a - c
    mult = np.zeros((QBLK, seq), np.int32)
    for win, dil in DILATED_PAIRS:
        mult += ((d >= 0) & (d % dil == 0) & (d <= win)).astype(np.int32)
    return np.where(mult > 0, np.log2(np.maximum(mult, 1)), MASKED).astype(np.float32)


N_FEAT = 9


def _dil_attn_kernel(q_ref, k_ref, v_ref, posc_ref, sfeat_ref, gq_ref, gk_ref, logm_ref,
                     wa_ref, wb_ref, o_ref, wa_out, wb_out, qa_ref, qb_ref, ks_ref, s_ref, p_ref,
                     ve_ref):
    wa_out[...] = wa_ref[...].astype(BF16)
    wb_out[...] = wb_ref[...].astype(BF16)
    seq = q_ref.shape[0]
    lane = lax.broadcasted_iota(jnp.int32, (QBLK, LANES), 1)
    lane_lo = lane < A_HEAD_DIM
    piece = [functools.reduce(jnp.logical_or, [lane == i + 3 * j for j in range(N_FEAT // 3)])
             for i in range(3)]

    def head_norm(ref, r0, g_ref, mul):
        t = ref[r0:r0 + QBLK, :].astype(F32)
        sq = t * t
        ssa = jnp.sum(jnp.where(lane_lo, sq, 0.0), axis=-1, keepdims=True)
        ssb = jnp.sum(jnp.where(lane_lo, 0.0, sq), axis=-1, keepdims=True)
        r = jnp.where(lane_lo, lax.rsqrt(ssa / A_HEAD_DIM + NORM_EPS),
                      lax.rsqrt(ssb / A_HEAD_DIM + NORM_EPS))
        return t * r * (g_ref[...] * mul)

    rel0 = posc_ref[0:1, :]
    for r0 in range(0, seq, QBLK):
        rows = slice(r0, r0 + QBLK)
        qn = head_norm(q_ref, r0, gq_ref, A_HEAD_DIM ** -0.5 * LOG2E)
        qa_ref[rows, :LANES] = jnp.where(lane_lo, qn, 0.0).astype(BF16)
        qb_ref[rows, :LANES] = jnp.where(lane_lo, 0.0, qn).astype(BF16)
        qa_ref[rows, LANES:] = jnp.broadcast_to(sfeat_ref[0:1, :], (QBLK, LANES)).astype(BF16)
        qb_ref[rows, LANES:] = jnp.broadcast_to(sfeat_ref[1:2, :], (QBLK, LANES)).astype(BF16)
        ks_ref[rows, :LANES] = head_norm(k_ref, r0, gk_ref, 1.0).astype(BF16)
        p1, p2, p3 = _split3((posc_ref[rows, :] - rel0).astype(F32))
        feat = jnp.where(piece[0], p1, jnp.where(piece[1], p2, jnp.where(piece[2], p3, 0.0)))
        ks_ref[rows, LANES:] = feat.astype(BF16)

    def qk_of(blk, h, w):
        return (qa_ref, qb_ref)[h][w - QBLK:w, :], ks_ref[0:w, :]

    _attention_blocks(seq, qk_of, v_ref, logm_ref, s_ref, p_ref, ve_ref, o_ref, lane_lo)


def _dilated_attention(qkv, pos, q_norm, k_norm, cast_weights, layer):
    bsz, seq, three_d = qkv.shape
    d = three_d // 3
    n_pairs = d // LANES
    n_heads = d // A_HEAD_DIM
    slopes = jnp.exp2(-8.0 * jnp.arange(1, n_heads + 1, dtype=F32) / n_heads) * LOG2E
    sfeat = jnp.repeat(jnp.stack(_split3(slopes), axis=1), 3, axis=1)
    sfeat = jnp.pad(sfeat, ((0, 0), (0, LANES - N_FEAT))).reshape(n_pairs, 2, LANES)
    logm = jnp.asarray(_dilated_log_multiplicity(seq))
    gq = jnp.tile(q_norm.reshape(1, A_HEAD_DIM), (1, 2))
    gk = jnp.tile(k_norm.reshape(1, A_HEAD_DIM), (1, 2))
    blk = lambda off: pl.BlockSpec((None, seq, LANES), lambda b, h: (b, 0, off + h))
    const2 = pl.BlockSpec((1, LANES), lambda b, h: (0, 0))
    ext = pltpu.VMEM((seq, 2 * LANES), BF16)
    grid = (bsz, n_pairs)
    w_in, w_out, w_shapes = _cast_rider(cast_weights, layer, grid)
    return pl.pallas_call(
        _dil_attn_kernel,
        grid=grid,
        in_specs=[blk(0), blk(n_pairs), blk(2 * n_pairs),
                  pl.BlockSpec((None, seq, 1), lambda b, h: (b, 0, 0)),
                  pl.BlockSpec((None, 2, LANES), lambda b, h: (h, 0, 0)),
                  const2, const2,
                  pl.BlockSpec((QBLK, seq), lambda b, h: (0, 0))] + w_in,
        out_specs=[pl.BlockSpec((None, seq, LANES), lambda b, h: (b, 0, h))] + w_out,
        scratch_shapes=[ext, ext, ext] + _attn_scratch(seq),
        out_shape=[jax.ShapeDtypeStruct((bsz, seq, d), BF16)] + w_shapes,
        compiler_params=_cparams("parallel", "parallel"),
    )(qkv, qkv, qkv, pos.reshape(bsz, seq, 1), sfeat, gq, gk, logm, *cast_weights)


def _post_attn_kernel(o_ref, wo_ref, x_ref, g1_ref, n2_ref, sc_ref, sh_ref, wrh_ref, wrl_ref,
                      xm_ref, h_ref, hs_ref, lg_ref, il_ref):
    y = jnp.dot(o_ref[...], wo_ref[...], preferred_element_type=F32)
    xm = x_ref[...] + g1_ref[...] * y
    xm_ref[...] = xm
    h = _rms(xm) * n2_ref[...]
    h = h * (1.0 + sc_ref[...]) + sh_ref[...]
    h_ref[...] = h.astype(h_ref.dtype)
    hs_ref[...] = _rows_to_slab_rows(h, il_ref)
    h_hi = h.astype(BF16)
    h_lo = (h - h_hi.astype(F32)).astype(BF16)
    lg_ref[...] = (_dot_nt(wrh_ref[...], h_hi) + _dot_nt(wrh_ref[...], h_lo)
                   + _dot_nt(wrl_ref[...], h_hi))


def _post_attn(o, w_o, xf, g1, n2, sc2, sh2, w_router, seq):
    n, d = xf.shape
    ne = w_router.shape[1]
    tm = min(ROW_TILE, seq)
    per_b = seq // tm
    slab_rows = d // LANES
    wr_t = w_router.T.astype(F32)
    wr_hi = wr_t.astype(BF16)
    vec = pl.BlockSpec((None, 1, d), lambda i: (i // per_b, 0, 0))
    row = lambda w: pl.BlockSpec((tm, w), lambda i: (i, 0))
    return pl.pallas_call(
        _post_attn_kernel,
        grid=(n // tm,),
        in_specs=[row(d), pl.BlockSpec((d, d), lambda i: (0, 0)), row(d), vec,
                  pl.BlockSpec((1, d), lambda i: (0, 0)), vec, vec,
                  pl.BlockSpec((ne, d), lambda i: (0, 0)), pl.BlockSpec((ne, d), lambda i: (0, 0))],
        out_specs=[row(d), row(d), pl.BlockSpec((tm * slab_rows, LANES), lambda i: (i, 0)),
                   pl.BlockSpec((ne, tm), lambda i: (0, i))],
        out_shape=[jax.ShapeDtypeStruct((n, d), F32), jax.ShapeDtypeStruct((n, d), BF16),
                   jax.ShapeDtypeStruct((n * slab_rows, LANES), BF16),
                   jax.ShapeDtypeStruct((ne, n), F32)],
        scratch_shapes=[pltpu.VMEM((tm * slab_rows, LANES), F32)],
        compiler_params=_cparams("parallel"),
    )(o, w_o, xf, g1, n2, sc2, sh2, wr_hi, (wr_t - wr_hi.astype(F32)).astype(BF16))


def _moe_rows(tile):
    rows = tile * TOP_K + N_EXPERTS * (SUBLANES - 1) + EROWS
    return -(-rows // SUBLANES) * SUBLANES


def _first_index(hit, index, limit):
    return jnp.min(jnp.where(hit, index, limit), axis=0, keepdims=True)


def _route_kernel(lg_ref, bias_ref, tri_ref, dest_ref, w_ref, seg_ref, cnt_ref, *, row_pitch):
    t = lg_ref.shape[1]
    per_g = N_EXPERTS // N_GROUPS
    sub = lax.broadcasted_iota(jnp.int32, (per_g, t), 0).astype(F32)
    neg = -jnp.inf
    scores, sel = [], []
    for g in range(N_GROUPS):
        rows = slice(g * per_g, (g + 1) * per_g)
        sc = jax.nn.sigmoid(lg_ref[rows, :])
        scores.append(sc)
        sel.append(sc + bias_ref[rows, :])

    gs = []
    for g in range(N_GROUPS):
        m1 = jnp.max(sel[g], axis=0, keepdims=True)
        first = _first_index(sel[g] == m1, sub, float(per_g))
        m2 = jnp.max(jnp.where(sub == first, neg, sel[g]), axis=0, keepdims=True)
        gs.append(m1 + m2)
    gsc = jnp.concatenate(gs, axis=0)
    gsub = lax.broadcasted_iota(jnp.int32, (N_GROUPS, t), 0).astype(F32)
    gkeep = jnp.zeros((N_GROUPS, t), F32)
    for _ in range(TOPK_GROUPS):
        m = jnp.max(gsc, axis=0, keepdims=True)
        pick = gsub == _first_index(gsc == m, gsub, float(N_GROUPS))
        gkeep = jnp.where(pick, 1.0, gkeep)
        gsc = jnp.where(pick, neg, gsc)

    eidx = [sub + float(g * per_g) for g in range(N_GROUPS)]
    masked = [jnp.where(jnp.broadcast_to(gkeep[g:g + 1, :], (per_g, t)) > 0.5, sel[g], neg)
              for g in range(N_GROUPS)]
    chosen = [jnp.zeros((per_g, t), F32) for _ in range(N_GROUPS)]
    wsel = [jnp.zeros((per_g, t), F32) for _ in range(N_GROUPS)]
    picked = []
    for _ in range(TOP_K):
        m = jnp.max(masked[0], axis=0, keepdims=True)
        for g in range(1, N_GROUPS):
            m = jnp.maximum(m, jnp.max(masked[g], axis=0, keepdims=True))
        e_first = _first_index(masked[0] == m, eidx[0], float(N_EXPERTS))
        for g in range(1, N_GROUPS):
            e_first = jnp.minimum(e_first, _first_index(masked[g] == m, eidx[g], float(N_EXPERTS)))
        picked.append(e_first)
        for g in range(N_GROUPS):
            pick = eidx[g] == e_first
            chosen[g] = jnp.where(pick, 1.0, chosen[g])
            wsel[g] = jnp.where(pick, scores[g], wsel[g])
            masked[g] = jnp.where(pick, neg, masked[g])

    denom = jnp.sum(wsel[0], axis=0, keepdims=True)
    for g in range(1, N_GROUPS):
        denom = denom + jnp.sum(wsel[g], axis=0, keepdims=True)

    c_all = jnp.concatenate(chosen, axis=0)
    rank = jnp.dot(c_all.astype(BF16), tri_ref[...], preferred_element_type=F32)
    counts = jnp.sum(c_all, axis=1, keepdims=True)
    padded = jnp.ceil(counts / SUBLANES) * SUBLANES
    below = (lax.broadcasted_iota(jnp.int32, (N_EXPERTS, N_EXPERTS), 1)
             < lax.broadcasted_iota(jnp.int32, (N_EXPERTS, N_EXPERTS), 0)).astype(F32)
    seg = jnp.dot(below, jnp.broadcast_to(padded, (N_EXPERTS, LANES)), preferred_element_type=F32,
                  precision=lax.Precision.HIGHEST)
    row_of = seg[:, :1] + rank

    dest_rows, w_rows = [], []
    for k in range(TOP_K):
        d_k = jnp.zeros((1, t), F32)
        w_k = jnp.zeros((1, t), F32)
        for g in range(N_GROUPS):
            hit = eidx[g] == picked[k]
            d_k = d_k + jnp.sum(jnp.where(hit, row_of[g * per_g:(g + 1) * per_g, :], 0.0), axis=0,
                                keepdims=True)
            w_k = w_k + jnp.sum(jnp.where(hit, wsel[g], 0.0), axis=0, keepdims=True)
        dest_rows.append(d_k)
        w_rows.append(w_k / denom * ROUTED_SCALE)
    fill = [jnp.zeros((SUBLANES - TOP_K, t), F32)]
    dest8 = (jnp.concatenate(dest_rows + fill, axis=0) * row_pitch).astype(jnp.int32)
    w8 = jnp.concatenate(w_rows + fill, axis=0)
    for j in range(t // LANES):
        dest_ref[j] = dest8[:, j * LANES:(j + 1) * LANES]
        w_ref[j] = w8[:, j * LANES:(j + 1) * LANES]
    seg_ref[...] = seg.astype(jnp.int32)
    cnt_ref[...] = jnp.broadcast_to(counts, (N_EXPERTS, LANES)).astype(jnp.int32)


def _route(logits_t, router_bias, tile, row_pitch):
    ne, n = logits_t.shape
    nt = n // tile
    tri = jnp.asarray(np.triu(np.ones((tile, tile), np.float32), 1), dtype=BF16)
    per_tile = lambda rows, width: pl.BlockSpec((None, rows, width), lambda i: (i, 0, 0))
    table = pl.BlockSpec((None, tile // LANES, SUBLANES, LANES), lambda i: (i, 0, 0, 0))
    dest, wts, seg, cnt = pl.pallas_call(
        functools.partial(_route_kernel, row_pitch=row_pitch),
        grid=(nt,),
        in_specs=[pl.BlockSpec((ne, tile), lambda i: (0, i)),
                  pl.BlockSpec((ne, 1), lambda i: (0, 0)),
                  pl.BlockSpec((tile, tile), lambda i: (0, 0))],
        out_specs=[table, table, per_tile(ne, LANES), per_tile(ne, LANES)],
        out_shape=[jax.ShapeDtypeStruct((nt, tile // LANES, SUBLANES, LANES), jnp.int32),
                   jax.ShapeDtypeStruct((nt, tile // LANES, SUBLANES, LANES), F32),
                   jax.ShapeDtypeStruct((nt, ne, LANES), jnp.int32),
                   jax.ShapeDtypeStruct((nt, ne, LANES), jnp.int32)],
        compiler_params=_cparams("parallel"),
    )(logits_t, router_bias.astype(F32).reshape(ne, 1), tri)
    return seg[:, :, 0].reshape(-1), cnt[:, :, 0].reshape(-1), dest, wts


def _swiglu(xb, wgu_ref, wdn_ref):
    gu = jnp.dot(xb, wgu_ref[...], preferred_element_type=F32)
    half = wgu_ref.shape[1] // 2
    act = _silu(gu[:, :half]) * gu[:, half:]
    return jnp.dot(act.astype(BF16), wdn_ref[...], preferred_element_type=F32)


def _slabs_to_rows(p, il_ref, r):
    n = il_ref.shape[0] // r
    il_ref[...] = pltpu.bitcast(p, BF16).astype(F32)
    return jnp.concatenate([il_ref[pl.ds(2 * c + h, r, stride=n), :]
                            for h in range(2) for c in range(n // 2)], axis=1)


def _rows_to_slab_rows(v, il_ref):
    r = v.shape[0]
    n = il_ref.shape[0] // r
    for h in range(2):
        for c in range(n // 2):
            lane0 = (h * (n // 2) + c) * LANES
            il_ref[pl.ds(2 * c + h, r, stride=n), :] = v[:, lane0:lane0 + LANES]
    return il_ref[...].astype(BF16)


def _rows_to_slabs(v, il_ref):
    return pltpu.bitcast(_rows_to_slab_rows(v, il_ref), U32)


def _moe_kernel(start_ref, count_ref, hp_ref, hb_ref, dest_ref, wrow_ref, wgu_hbm, wdn_hbm,
                sgu_ref, sdn_ref, xm_ref, g2_ref, o_ref, xy_ref, il_ref, yp_ref, zf_ref,
                comb_ref, wgu_buf, wdn_buf, wsem):
    t = pl.program_id(0)
    s = pl.program_id(1)
    slab = xm_ref.shape[1] // (2 * LANES)
    tile = hp_ref.shape[0] // (2 * slab)
    group_rows = SUBLANES * slab

    def weight_copies(e, slot):
        return (pltpu.make_async_copy(wgu_hbm.at[e], wgu_buf.at[slot], wsem.at[0, slot]),
                pltpu.make_async_copy(wdn_hbm.at[e], wdn_buf.at[slot], wsem.at[1, slot]))

    def slab_at(ref, first):
        return ref.at[pl.ds(pl.multiple_of(first, slab), slab), :]

    @pl.when(s == 0)
    def _first_weights():
        for p in range(2):
            for cp in weight_copies(p, p):
                cp.start()

    @pl.when(jnp.logical_and(t == 0, s == 0))
    def _clear():
        xy_ref[...] = jnp.zeros(xy_ref.shape, U32)

    @pl.when(s == 0)
    def _dispatch():
        def group(j, carry):
            for sub in range(LANES // SUBLANES):
                p0 = pl.multiple_of((j * (LANES // SUBLANES) + sub) * 2 * group_rows, 2 * group_rows)
                rows = pltpu.bitcast(hp_ref[pl.ds(p0, 2 * group_rows), :], U32)
                for u in range(SUBLANES):
                    for k in range(TOP_K):
                        slab_at(xy_ref, dest_ref[j, k, sub * SUBLANES + u])[...] = (
                            rows[u * slab:(u + 1) * slab, :])
            return carry

        lax.fori_loop(0, tile // LANES, group, 0)

    @pl.when(s == 0)
    def _experts():
        def chunk_base(e, c):
            return pl.multiple_of((start_ref[t * N_EXPERTS + e] + c * EROWS) * slab, group_rows)

        def load_rows(base, p):
            return _slabs_to_rows(xy_ref[pl.ds(base, EROWS * slab), :], il_ref.at[p], EROWS).astype(BF16)

        def put_rows(p, base, valid):
            def put(j, c2):
                r0 = pl.multiple_of(j * group_rows, group_rows)
                xy_ref[pl.ds(base + r0, group_rows), :] = yp_ref[p, pl.ds(r0, group_rows), :]
                return c2

            lax.fori_loop(0, (jnp.clip(valid, 0, EROWS) + SUBLANES - 1) // SUBLANES, put, 0)

        def expert_pair(i, carry0):
            wslot = lax.rem(i, 2) * 2

            @pl.when(i + 1 < N_EXPERTS // 2)
            def _prefetch():
                for p in range(2):
                    for cp in weight_copies(2 * i + 2 + p, 2 - wslot + p):
                        cp.start()

            for p in range(2):
                for cp in weight_copies(2 * i + p, wslot + p):
                    cp.wait()
            counts = [count_ref[t * N_EXPERTS + 2 * i + p] for p in range(2)]
            bases = [chunk_base(2 * i + p, 0) for p in range(2)]
            xs = [load_rows(bases[p], p) for p in range(2)]
            ys = [_swiglu(xs[p], wgu_buf.at[wslot + p], wdn_buf.at[wslot + p]) for p in range(2)]
            for p in range(2):
                yp_ref[p] = _rows_to_slabs(ys[p], il_ref.at[p])
            for p in range(2):
                put_rows(p, bases[p], counts[p])

            for p in range(2):
                def more(c, carry, p=p):
                    base = chunk_base(2 * i + p, c)
                    y = _swiglu(load_rows(base, p), wgu_buf.at[wslot + p], wdn_buf.at[wslot + p])
                    yp_ref[p] = _rows_to_slabs(y, il_ref.at[p])
                    put_rows(p, base, counts[p] - c * EROWS)
                    return carry

                lax.fori_loop(1, (counts[p] + EROWS - 1) // EROWS, more, 0)
            return carry0

        lax.fori_loop(0, N_EXPERTS // 2, expert_pair, 0)

    @pl.when(s >= 1)
    def _combine():
        sb = s - 1

        def group(j, carry):
            tj = sb * (CROWS // LANES) + j
            for sub in range(LANES // SUBLANES):
                r0 = pl.multiple_of(j * LANES + sub * SUBLANES, SUBLANES)
                half = sub % 2
                for u in range(SUBLANES):
                    lane = sub * SUBLANES + u
                    acc = None
                    for k in range(TOP_K):
                        z = pltpu.bitcast(slab_at(xy_ref, dest_ref[tj, k, lane])[...], BF16).astype(F32)
                        acc = wrow_ref[tj, k, lane] * z if acc is None else acc + wrow_ref[tj, k, lane] * z
                    zf_ref[half, u * 2 * slab:(u + 1) * 2 * slab, :] = acc
                comb_ref[pl.ds(r0, SUBLANES), :] = jnp.concatenate(
                    [zf_ref[half, pl.ds(2 * c + h, SUBLANES, stride=2 * slab), :]
                     for h in range(2) for c in range(slab)], axis=1)
            return carry

        lax.fori_loop(0, CROWS // LANES, group, 0)
        shared = _swiglu(hb_ref[...], sgu_ref, sdn_ref)
        o_ref[...] = xm_ref[...] + g2_ref[...] * (comb_ref[...] + shared)


def _moe(h, h_slabs, xm, g2, logits_t, router_bias, w_gu, w_dn, s_gu, s_dn, layer, seq):
    n, d = h.shape
    slab = d // (2 * LANES)
    tile = min(MOE_TILE, seq)
    nt = n // tile
    per_b = seq // tile
    csteps = tile // CROWS
    assert seq % tile == 0 and tile % CROWS == 0
    assert tile % LANES == 0 and CROWS % LANES == 0
    seg_start, seg_count, dest_rows, w_rows = _route(logits_t, router_bias, tile, slab)
    rows = _moe_rows(tile)
    table = pl.BlockSpec((None, tile // LANES, SUBLANES, LANES), lambda t, s, a, b: (t, 0, 0, 0),
                         memory_space=pltpu.SMEM)
    cst = lambda s: jnp.maximum(s - 1, 0)
    crow = pl.BlockSpec((CROWS, d), lambda t, s, a, b: (t * csteps + cst(s), 0))
    hbm = pl.BlockSpec(memory_space=pl.ANY)
    single = pl.Buffered(1)
    return pl.pallas_call(
        _moe_kernel,
        grid_spec=pltpu.PrefetchScalarGridSpec(
            num_scalar_prefetch=2,
            grid=(nt, 1 + csteps),
            in_specs=[pl.BlockSpec((tile * 2 * slab, LANES), lambda t, s, a, b: (t, 0),
                                   pipeline_mode=single),
                      crow, table, table,
                      hbm, hbm,
                      pl.BlockSpec((None,) + s_gu.shape[1:], lambda t, s, a, b: (layer, 0, 0),
                                   pipeline_mode=single),
                      pl.BlockSpec((None,) + s_dn.shape[1:], lambda t, s, a, b: (layer, 0, 0),
                                   pipeline_mode=single),
                      crow,
                      pl.BlockSpec((None, 1, d), lambda t, s, a, b: (t // per_b, 0, 0))],
            out_specs=crow,
            scratch_shapes=[pltpu.VMEM((rows * slab, LANES), U32),
                            pltpu.VMEM((2, EROWS * 2 * slab, LANES), F32),
                            pltpu.VMEM((2, EROWS * slab, LANES), U32),
                            pltpu.VMEM((2, SUBLANES * 2 * slab, LANES), F32),
                            pltpu.VMEM((CROWS, d), F32),
                            pltpu.VMEM((4, d, 2 * EXPERT_DIM), BF16),
                            pltpu.VMEM((4, EXPERT_DIM, d), BF16),
                            pltpu.SemaphoreType.DMA((2, 4))]),
        out_shape=jax.ShapeDtypeStruct((n, d), F32),
        compiler_params=_cparams("arbitrary", "arbitrary"),
    )(seg_start, seg_count, h_slabs, h, dest_rows, w_rows, w_gu, w_dn, s_gu, s_dn, xm, g2)


def _mla_prep_kernel(x_ref, n1_ref, sc1_ref, sh1_ref, nk_ref, sck_ref, shk_ref,
                     wdq_ref, gql_ref, wuq_ref, wuqs_ref, wdown_ref, gkl_ref, wk_ref, gk_ref, wuv_ref,
                     qa_ref, qb_ref, kc_ref, ksa_ref, ksb_ref, q_ref, k_ref, v_ref):
    xn = _rms(x_ref[...])
    hq = xn * n1_ref[...] * (1.0 + sc1_ref[...]) + sh1_ref[...]
    hk = xn * nk_ref[...] * (1.0 + sck_ref[...]) + shk_ref[...]
    n_heads = q_ref.shape[1] // LANES
    half = ROPE_DIM // 2

    cq = (_rms(jnp.dot(hq.astype(BF16), wdq_ref[...], preferred_element_type=F32)) * gql_ref[...]).astype(BF16)
    q_raw = jnp.dot(cq, wuq_ref[...], preferred_element_type=F32)
    q_swp = jnp.dot(cq, wuqs_ref[...], preferred_element_type=F32)
    qa, qb = qa_ref[...], qb_ref[...]
    for h in range(n_heads):
        sl = slice(h * LANES, (h + 1) * LANES)
        t = q_raw[:, sl]
        r = lax.rsqrt(jnp.sum(t * t, axis=-1, keepdims=True) / QK_DIM + NORM_EPS)
        q_ref[:, sl] = (r * (t * qa + q_swp[:, sl] * qb)).astype(q_ref.dtype)

    down = jnp.dot(hk.astype(BF16), wdown_ref[...], preferred_element_type=F32)
    ckv = (_rms(down[:, :KV_RANK]) * gkl_ref[...]).astype(BF16)
    pe = down[:, KV_RANK:]
    ss_pe = jnp.sum(pe * pe, axis=-1, keepdims=True)
    rot = (pe * kc_ref[...] + pltpu.roll(pe, LANES - half, 1) * ksa_ref[...]
           + pltpu.roll(pe, half, 1) * ksb_ref[...])
    rot_hi = rot.astype(BF16)
    rot_lo = (rot - rot_hi.astype(F32)).astype(BF16)
    k_raw = jnp.dot(jnp.concatenate([ckv, rot_hi, rot_lo], axis=1), wk_ref[...],
                    preferred_element_type=F32)
    nope = lax.broadcasted_iota(jnp.int32, (x_ref.shape[0], LANES), 1) < NOPE_DIM
    for h in range(n_heads):
        sl = slice(h * LANES, (h + 1) * LANES)
        t = k_raw[:, sl]
        ss = jnp.sum(jnp.where(nope, t * t, 0.0), axis=-1, keepdims=True) + ss_pe
        k_ref[:, sl] = (t * lax.rsqrt(ss / QK_DIM + NORM_EPS) * gk_ref[...]).astype(k_ref.dtype)
    v_ref[...] = jnp.dot(ckv, wuv_ref[...], preferred_element_type=F32).astype(v_ref.dtype)


def _pad_heads(w, n_heads, width):
    r = w.shape[0]
    return jnp.pad(w.reshape(r, n_heads, width), ((0, 0), (0, 0), (0, LANES - width))).reshape(r, n_heads * LANES)


def _swap_rope_halves(a):
    half = ROPE_DIM // 2
    lead = a.shape[:-1]
    return jnp.concatenate([jnp.zeros(lead + (NOPE_DIM,), a.dtype), a[..., NOPE_DIM + half:QK_DIM],
                            a[..., NOPE_DIM:NOPE_DIM + half],
                            jnp.zeros(lead + (LANES - QK_DIM,), a.dtype)], axis=-1)


def _mla_prep(xf, pos, n1, sc1, sh1, nk, sck, shk, w_dq, g_ql, w_uq, g_q, w_down, g_kl, w_uk, w_uv,
              g_k, seq):
    n, d = xf.shape
    n_heads = w_uv.shape[1] // V_DIM
    q_rank = w_dq.shape[1]
    hw = n_heads * LANES
    tm = min(ROW_TILE, seq)
    per_b = seq // tm

    half = ROPE_DIM // 2
    inv = jnp.power(ROPE_THETA, -jnp.arange(half, dtype=F32) / half)
    ang = pos.astype(F32).reshape(n, 1) * inv
    cos, sin = jnp.cos(ang), jnp.sin(ang)
    ones = jnp.ones((n, NOPE_DIM), F32)
    zeros = lambda wdt: jnp.zeros((n, wdt), F32)
    tail = LANES - QK_DIM
    pad_gain = lambda g: jnp.pad(g.reshape(1, QK_DIM), ((0, 0), (0, tail)))
    q_mul = QK_DIM ** -0.5 * LOG2E
    qa_t = jnp.concatenate([ones, cos, cos, zeros(tail)], axis=1) * pad_gain(g_q) * q_mul
    qb_t = (jnp.concatenate([zeros(NOPE_DIM), -sin, sin, zeros(tail)], axis=1)
            * _swap_rope_halves(g_q.reshape(1, QK_DIM)) * q_mul)
    gk1, gk2 = g_k[NOPE_DIM:NOPE_DIM + half], g_k[NOPE_DIM + half:]
    rest = zeros(LANES - ROPE_DIM)
    kc_t = jnp.concatenate([cos * gk1, cos * gk2, rest], axis=1)
    ksa_t = jnp.concatenate([-sin * gk2, zeros(half), rest], axis=1)
    ksb_t = jnp.concatenate([zeros(half), sin * gk1, rest], axis=1)
    gk_mask = jnp.concatenate([g_k[:NOPE_DIM], jnp.ones((ROPE_DIM,), F32),
                               jnp.zeros((tail,), F32)]).reshape(1, LANES)

    wuq_p = _pad_heads(w_uq, n_heads, QK_DIM).astype(BF16)
    wuq_s = _swap_rope_halves(w_uq.reshape(q_rank, n_heads, QK_DIM)).reshape(q_rank, hw).astype(BF16)
    wdown_p = jnp.pad(w_down, ((0, 0), (0, LANES - ROPE_DIM))).astype(BF16)
    eye = jnp.pad(jnp.eye(ROPE_DIM, dtype=F32), ((0, LANES - ROPE_DIM), (NOPE_DIM, tail)))
    pe_rows = jnp.tile(eye, (1, n_heads))
    wk = jnp.concatenate([_pad_heads(w_uk, n_heads, NOPE_DIM), pe_rows, pe_rows], axis=0).astype(BF16)

    vec = pl.BlockSpec((None, 1, d), lambda i: (i // per_b, 0, 0))
    full = lambda a: pl.BlockSpec(a.shape, lambda i: (0,) * a.ndim)
    row = lambda wdt: pl.BlockSpec((tm, wdt), lambda i: (i, 0))
    args = [xf, n1, sc1, sh1, nk, sck, shk,
            w_dq.astype(BF16), g_ql.reshape(1, q_rank), wuq_p, wuq_s,
            wdown_p, g_kl.reshape(1, KV_RANK), wk, gk_mask, w_uv.astype(BF16),
            qa_t, qb_t, kc_t, ksa_t, ksb_t]
    specs = [row(d), full(n1), vec, vec, full(nk), vec, vec] + [full(a) for a in args[7:16]] + [row(LANES)] * 5
    return pl.pallas_call(
        _mla_prep_kernel,
        grid=(n // tm,),
        in_specs=specs,
        out_specs=[row(hw), row(hw), row(n_heads * V_DIM)],
        out_shape=[jax.ShapeDtypeStruct((n, hw), BF16), jax.ShapeDtypeStruct((n, hw), BF16),
                   jax.ShapeDtypeStruct((n, n_heads * V_DIM), BF16)],
        compiler_params=_cparams("parallel"),
    )(*args)


def _causal_bias(seq):
    a = np.arange(QBLK)[:, None]
    c = np.arange(seq)[None, :]
    return np.where(c <= (seq - QBLK) + a, 0.0, MASKED).astype(np.float32)


def _mla_attn_kernel(q_ref, k_ref, v_ref, mask_ref, wa_ref, wb_ref, o_ref, wa_out, wb_out,
                     s_ref, p_ref, ve_ref):
    wa_out[...] = wa_ref[...].astype(BF16)
    wb_out[...] = wb_ref[...].astype(BF16)
    seq = q_ref.shape[0]
    lane_lo = lax.broadcasted_iota(jnp.int32, (QBLK, LANES), 1) < V_DIM

    def qk_of(blk, h, w):
        lanes = slice(h * LANES, (h + 1) * LANES)
        return q_ref[w - QBLK:w, lanes], k_ref[0:w, lanes]

    _attention_blocks(seq, qk_of, v_ref, mask_ref, s_ref, p_ref, ve_ref, o_ref, lane_lo)


def _mla_attention(q, k, v, cast_weights, layer):
    bsz, seq, hw = q.shape
    n_pairs = hw // (2 * LANES)
    qk = pl.BlockSpec((None, seq, 2 * LANES), lambda b, h: (b, 0, h))
    vo = pl.BlockSpec((None, seq, LANES), lambda b, h: (b, 0, h))
    grid = (bsz, n_pairs)
    w_in, w_out, w_shapes = _cast_rider(cast_weights, layer, grid)
    return pl.pallas_call(
        _mla_attn_kernel,
        grid=grid,
        in_specs=[qk, qk, vo, pl.BlockSpec((QBLK, seq), lambda b, h: (0, 0))] + w_in,
        out_specs=[vo] + w_out,
        scratch_shapes=_attn_scratch(seq),
        out_shape=[jax.ShapeDtypeStruct(v.shape, BF16)] + w_shapes,
        compiler_params=_cparams("parallel", "parallel"),
    )(q, k, v, jnp.asarray(_causal_bias(seq)), *cast_weights)


def kernel(x, c, pos, ada_w, ada_b, norm1_g, norm2_g, a_w_qkv, a_q_norm, a_k_norm, a_w_o, kv_ada_w, kv_ada_b, kv_norm_g, kv_w_down, kv_latent_norm, kv_w_uk, kv_w_uv, kv_k_norm, b_w_dq, b_q_latent_norm, b_w_uq, b_q_norm, b_w_o, moe_w_router, moe_router_bias, moe_w_gate_up, moe_w_down, moe_shared_gate_up, moe_shared_down):
    bsz, seq, d = x.shape
    n = bsz * seq
    depth = ada_w.shape[0]
    n_a = a_w_qkv.shape[0]
    assert seq % QBLK == 0
    xf = x.reshape(n, d)

    expert_weights = (moe_w_gate_up, moe_w_down)
    s_gu = moe_shared_gate_up.astype(BF16)
    s_dn = moe_shared_down.astype(BF16)
    row = lambda g: g.reshape(1, d)

    for layer in range(depth):
        mod = _modulation(c, ada_w, ada_b, layer).reshape(bsz, 6, 1, d)
        sh1, sc1, g1, sh2, sc2, g2 = [mod[:, i] for i in range(6)]
        if layer < n_a:
            qkv = _qkv_proj(xf, row(norm1_g[layer]), sc1, sh1, a_w_qkv[layer].astype(BF16), seq)
            o, w_gu, w_dn = _dilated_attention(qkv.reshape(bsz, seq, -1), pos, a_q_norm[layer],
                                               a_k_norm[layer], expert_weights, layer)
            w_o = a_w_o[layer]
        else:
            j = layer - n_a
            kmod = _modulation(c, kv_ada_w[None], kv_ada_b[None], 0).reshape(bsz, 2, 1, d)
            q, k, v = _mla_prep(xf, pos, row(norm1_g[layer]), sc1, sh1, row(kv_norm_g), kmod[:, 1],
                                kmod[:, 0], b_w_dq[j], b_q_latent_norm[j], b_w_uq[j], b_q_norm[j],
                                kv_w_down, kv_latent_norm, kv_w_uk, kv_w_uv, kv_k_norm, seq)
            o, w_gu, w_dn = _mla_attention(q.reshape(bsz, seq, -1), k.reshape(bsz, seq, -1),
                                           v.reshape(bsz, seq, -1), expert_weights, layer)
            w_o = b_w_o[j]
        xm, h2, h2_slabs, logits = _post_attn(o.reshape(n, d), w_o.astype(BF16), xf, g1,
                                              row(norm2_g[layer]), sc2, sh2, moe_w_router[layer], seq)
        xf = _moe(h2, h2_slabs, xm, g2, logits, moe_router_bias[layer], w_gu, w_dn, s_gu, s_dn,
                  layer, seq)
    return xf.reshape(bsz, seq, d)
```

```python
import functools
import math

import numpy as np
import jax
import jax.numpy as jnp
from jax import lax
from jax.experimental import pallas as pl
from jax.experimental.pallas import tpu as pltpu

F32 = jnp.float32
BF16 = jnp.bfloat16
U32 = jnp.uint32

NORM_EPS = 1e-6
MASKED = -1e30
LOG2E = math.log2(math.e)

LANES = 128
SUBLANES = 8
VMEM_LIMIT = 56 * 1024 * 1024

A_HEAD_DIM = 64
DILATED_PAIRS = ((128, 1), (512, 4), (2048, 16))
QBLK = 256
NOPE_DIM = 64
ROPE_DIM = 32
QK_DIM = NOPE_DIM + ROPE_DIM
V_DIM = 64
KV_RANK = 256
ROPE_THETA = 10000.0
N_EXPERTS = 64
N_GROUPS = 8
TOPK_GROUPS = 4
TOP_K = 6
EXPERT_DIM = 256
ROUTED_SCALE = 2.5
MOE_TILE = 2048
EROWS = 256
CROWS = 256

ROW_TILE = 512


def _cparams(*sem):
    return pltpu.CompilerParams(dimension_semantics=sem, vmem_limit_bytes=VMEM_LIMIT)


def _silu(v):
    return v * jax.nn.sigmoid(v)


def _rms(v):
    return v * lax.rsqrt(jnp.mean(v * v, axis=-1, keepdims=True) + NORM_EPS)


def _dot_nt(a, b):
    return lax.dot_general(a, b, (((1,), (1,)), ((), ())), preferred_element_type=F32)


def _mod_kernel(c_ref, w_ref, b_ref, o_ref):
    cond = _silu(c_ref[...])
    o_ref[...] = jnp.dot(cond, w_ref[...], preferred_element_type=F32,
                         precision=lax.Precision.HIGHEST) + b_ref[...]


def _modulation(c, w, b, layer):
    bsz, d = c.shape
    m = w.shape[-1]
    tn = 1024
    return pl.pallas_call(
        _mod_kernel,
        grid=(m // tn,),
        in_specs=[pl.BlockSpec((bsz, d), lambda j: (0, 0)),
                  pl.BlockSpec((None, d, tn), lambda j: (layer, 0, j)),
                  pl.BlockSpec((None, 1, tn), lambda j: (layer, 0, j))],
        out_specs=pl.BlockSpec((bsz, tn), lambda j: (0, j)),
        out_shape=jax.ShapeDtypeStruct((bsz, m), F32),
        compiler_params=_cparams("arbitrary"),
    )(c, w, b.reshape(b.shape[0], 1, m))


def _qkv_kernel(x_ref, g_ref, sc_ref, sh_ref, w_ref, o_ref):
    h = _rms(x_ref[...]) * g_ref[...]
    h = h * (1.0 + sc_ref[...]) + sh_ref[...]
    o_ref[...] = jnp.dot(h.astype(BF16), w_ref[...], preferred_element_type=F32).astype(o_ref.dtype)


def _qkv_proj(xf, g, sc, sh, w, seq):
    n, d = xf.shape
    m = w.shape[1]
    tm = min(ROW_TILE, seq)
    per_b = seq // tm
    vec = pl.BlockSpec((None, 1, d), lambda i: (i // per_b, 0, 0))
    return pl.pallas_call(
        _qkv_kernel,
        grid=(n // tm,),
        in_specs=[pl.BlockSpec((tm, d), lambda i: (i, 0)),
                  pl.BlockSpec((1, d), lambda i: (0, 0)),
                  vec, vec,
                  pl.BlockSpec((d, m), lambda i: (0, 0))],
        out_specs=pl.BlockSpec((tm, m), lambda i: (i, 0)),
        out_shape=jax.ShapeDtypeStruct((n, m), BF16),
        compiler_params=_cparams("parallel"),
    )(xf, g, sc, sh, w)


ATT_SLOTS = 4


def _attention_blocks(seq, qk_of, v_ref, bias_ref, s_ref, p_ref, ve_ref, o_ref, lane_lo):
    nblk = seq // QBLK

    for r0 in range(0, seq, QBLK):
        v = v_ref[r0:r0 + QBLK, :].astype(F32)
        ve_ref[0, r0:r0 + QBLK, :] = jnp.where(lane_lo, v, 1.0).astype(BF16)
        ve_ref[1, r0:r0 + QBLK, :] = jnp.where(lane_lo, 1.0, v).astype(BF16)

    def slot(blk, h):
        return (blk % 2) * 2 + h

    def scores(blk):
        w = (blk + 1) * QBLK
        bias = bias_ref[:, seq - w:seq]
        for h in range(2):
            q, k = qk_of(blk, h, w)
            s_ref[slot(blk, h), :, 0:w] = _dot_nt(q, k) + bias

    for blk in range(nblk):
        w = (blk + 1) * QBLK
        scores(blk)
        for h in range(2):
            s = s_ref[slot(blk, h), :, 0:w]
            p_ref[slot(blk, h), :, 0:w] = jnp.exp2(s - jnp.max(s, axis=-1, keepdims=True)).astype(BF16)
        acc_a, acc_b = [jnp.dot(p_ref[slot(blk, h), :, 0:w], ve_ref[h, 0:w, :],
                                preferred_element_type=F32) for h in range(2)]
        sums = pltpu.roll(jnp.where(lane_lo, acc_b, acc_a), LANES // 2, 1)
        o_ref[blk * QBLK:w, :] = (jnp.where(lane_lo, acc_a, acc_b) / sums).astype(o_ref.dtype)


def _attn_scratch(seq):
    return [pltpu.VMEM((ATT_SLOTS, QBLK, seq), F32), pltpu.VMEM((ATT_SLOTS, QBLK, seq), BF16),
            pltpu.VMEM((2, seq, LANES), BF16)]


def _cast_rider(weights, layer, grid):
    steps = grid[0] * grid[1]
    in_specs, out_specs, out_shapes = [], [], []
    for w in weights:
        n_exp = w.shape[1]
        assert n_exp % steps == 0
        per = n_exp // steps
        blk = (per,) + w.shape[2:]
        in_specs.append(pl.BlockSpec((None,) + blk, lambda b, h: (layer, b * grid[1] + h, 0, 0)))
        out_specs.append(pl.BlockSpec(blk, lambda b, h: (b * grid[1] + h, 0, 0)))
        out_shapes.append(jax.ShapeDtypeStruct(w.shape[1:], BF16))
    return in_specs, out_specs, out_shapes


def _split3(v):
    p1 = v.astype(BF16).astype(F32)
    p2 = (v - p1).astype(BF16).astype(F32)
    return p1, p2, v - p1 - p2


def _dilated_log_multiplicity(seq):
    a = np.arange(QBLK)[:, None]
    c = np.arange(seq)[None, :]
    d = (seq - QBLK) + a - c
    mult = np.zeros((QBLK, seq), np.int32)
    for win, dil in DILATED_PAIRS:
        mult += ((d >= 0) & (d % dil == 0) & (d <= win)).astype(np.int32)
    return np.where(mult > 0, np.log2(np.maximum(mult, 1)), MASKED).astype(np.float32)


N_FEAT = 9


def _dil_attn_kernel(q_ref, k_ref, v_ref, posc_ref, sfeat_ref, gq_ref, gk_ref, logm_ref,
                     wa_ref, wb_ref, o_ref, wa_out, wb_out, qa_ref, qb_ref, ks_ref, s_ref, p_ref,
                     ve_ref):
    wa_out[...] = wa_ref[...].astype(BF16)
    wb_out[...] = wb_ref[...].astype(BF16)
    seq = q_ref.shape[0]
    lane = lax.broadcasted_iota(jnp.int32, (QBLK, LANES), 1)
    lane_lo = lane < A_HEAD_DIM
    piece = [functools.reduce(jnp.logical_or, [lane == i + 3 * j for j in range(N_FEAT // 3)])
             for i in range(3)]

    def head_norm(ref, r0, g_ref, mul):
        t = ref[r0:r0 + QBLK, :].astype(F32)
        sq = t * t
        ssa = jnp.sum(jnp.where(lane_lo, sq, 0.0), axis=-1, keepdims=True)
        ssb = jnp.sum(jnp.where(lane_lo, 0.0, sq), axis=-1, keepdims=True)
        r = jnp.where(lane_lo, lax.rsqrt(ssa / A_HEAD_DIM + NORM_EPS),
                      lax.rsqrt(ssb / A_HEAD_DIM + NORM_EPS))
        return t * r * (g_ref[...] * mul)

    rel0 = posc_ref[0:1, :]
    for r0 in range(0, seq, QBLK):
        rows = slice(r0, r0 + QBLK)
        qn = head_norm(q_ref, r0, gq_ref, A_HEAD_DIM ** -0.5 * LOG2E)
        qa_ref[rows, :LANES] = jnp.where(lane_lo, qn, 0.0).astype(BF16)
        qb_ref[rows, :LANES] = jnp.where(lane_lo, 0.0, qn).astype(BF16)
        qa_ref[rows, LANES:] = jnp.broadcast_to(sfeat_ref[0:1, :], (QBLK, LANES)).astype(BF16)
        qb_ref[rows, LANES:] = jnp.broadcast_to(sfeat_ref[1:2, :], (QBLK, LANES)).astype(BF16)
        ks_ref[rows, :LANES] = head_norm(k_ref, r0, gk_ref, 1.0).astype(BF16)
        p1, p2, p3 = _split3((posc_ref[rows, :] - rel0).astype(F32))
        feat = jnp.where(piece[0], p1, jnp.where(piece[1], p2, jnp.where(piece[2], p3, 0.0)))
        ks_ref[rows, LANES:] = feat.astype(BF16)

    def qk_of(blk, h, w):
        return (qa_ref, qb_ref)[h][w - QBLK:w, :], ks_ref[0:w, :]

    _attention_blocks(seq, qk_of, v_ref, logm_ref, s_ref, p_ref, ve_ref, o_ref, lane_lo)


def _dilated_attention(qkv, pos, q_norm, k_norm, cast_weights, layer):
    bsz, seq, three_d = qkv.shape
    d = three_d // 3
    n_pairs = d // LANES
    n_heads = d // A_HEAD_DIM
    slopes = jnp.exp2(-8.0 * jnp.arange(1, n_heads + 1, dtype=F32) / n_heads) * LOG2E
    sfeat = jnp.repeat(jnp.stack(_split3(slopes), axis=1), 3, axis=1)
    sfeat = jnp.pad(sfeat, ((0, 0), (0, LANES - N_FEAT))).reshape(n_pairs, 2, LANES)
    logm = jnp.asarray(_dilated_log_multiplicity(seq))
    gq = jnp.tile(q_norm.reshape(1, A_HEAD_DIM), (1, 2))
    gk = jnp.tile(k_norm.reshape(1, A_HEAD_DIM), (1, 2))
    blk = lambda off: pl.BlockSpec((None, seq, LANES), lambda b, h: (b, 0, off + h))
    const2 = pl.BlockSpec((1, LANES), lambda b, h: (0, 0))
    ext = pltpu.VMEM((seq, 2 * LANES), BF16)
    grid = (bsz, n_pairs)
    w_in, w_out, w_shapes = _cast_rider(cast_weights, layer, grid)
    return pl.pallas_call(
        _dil_attn_kernel,
        grid=grid,
        in_specs=[blk(0), blk(n_pairs), blk(2 * n_pairs),
                  pl.BlockSpec((None, seq, 1), lambda b, h: (b, 0, 0)),
                  pl.BlockSpec((None, 2, LANES), lambda b, h: (h, 0, 0)),
                  const2, const2,
                  pl.BlockSpec((QBLK, seq), lambda b, h: (0, 0))] + w_in,
        out_specs=[pl.BlockSpec((None, seq, LANES), lambda b, h: (b, 0, h))] + w_out,
        scratch_shapes=[ext, ext, ext] + _attn_scratch(seq),
        out_shape=[jax.ShapeDtypeStruct((bsz, seq, d), BF16)] + w_shapes,
        compiler_params=_cparams("parallel", "parallel"),
    )(qkv, qkv, qkv, pos.reshape(bsz, seq, 1), sfeat, gq, gk, logm, *cast_weights)


def _post_attn_kernel(o_ref, wo_ref, x_ref, g1_ref, n2_ref, sc_ref, sh_ref, wrh_ref, wrl_ref,
                      xm_ref, h_ref, hs_ref, lg_ref, il_ref):
    y = jnp.dot(o_ref[...], wo_ref[...], preferred_element_type=F32)
    xm = x_ref[...] + g1_ref[...] * y
    xm_ref[...] = xm
    h = _rms(xm) * n2_ref[...]
    h = h * (1.0 + sc_ref[...]) + sh_ref[...]
    h_ref[...] = h.astype(h_ref.dtype)
    hs_ref[...] = _rows_to_slab_rows(h, il_ref)
    h_hi = h.astype(BF16)
    h_lo = (h - h_hi.astype(F32)).astype(BF16)
    lg_ref[...] = (_dot_nt(wrh_ref[...], h_hi) + _dot_nt(wrh_ref[...], h_lo)
                   + _dot_nt(wrl_ref[...], h_hi))


def _post_attn(o, w_o, xf, g1, n2, sc2, sh2, w_router, seq):
    n, d = xf.shape
    ne = w_router.shape[1]
    tm = min(ROW_TILE, seq)
    per_b = seq // tm
    slab_rows = d // LANES
    wr_t = w_router.T.astype(F32)
    wr_hi = wr_t.astype(BF16)
    vec = pl.BlockSpec((None, 1, d), lambda i: (i // per_b, 0, 0))
    row = lambda w: pl.BlockSpec((tm, w), lambda i: (i, 0))
    return pl.pallas_call(
        _post_attn_kernel,
        grid=(n // tm,),
        in_specs=[row(d), pl.BlockSpec((d, d), lambda i: (0, 0)), row(d), vec,
                  pl.BlockSpec((1, d), lambda i: (0, 0)), vec, vec,
                  pl.BlockSpec((ne, d), lambda i: (0, 0)), pl.BlockSpec((ne, d), lambda i: (0, 0))],
        out_specs=[row(d), row(d), pl.BlockSpec((tm * slab_rows, LANES), lambda i: (i, 0)),
                   pl.BlockSpec((ne, tm), lambda i: (0, i))],
        out_shape=[jax.ShapeDtypeStruct((n, d), F32), jax.ShapeDtypeStruct((n, d), BF16),
                   jax.ShapeDtypeStruct((n * slab_rows, LANES), BF16),
                   jax.ShapeDtypeStruct((ne, n), F32)],
        scratch_shapes=[pltpu.VMEM((tm * slab_rows, LANES), F32)],
        compiler_params=_cparams("parallel"),
    )(o, w_o, xf, g1, n2, sc2, sh2, wr_hi, (wr_t - wr_hi.astype(F32)).astype(BF16))


def _moe_rows(tile):
    rows = tile * TOP_K + N_EXPERTS * (SUBLANES - 1) + EROWS
    return -(-rows // SUBLANES) * SUBLANES


def _first_index(hit, index, limit):
    return jnp.min(jnp.where(hit, index, limit), axis=0, keepdims=True)


def _route_kernel(lg_ref, bias_ref, tri_ref, dest_ref, w_ref, seg_ref, cnt_ref, *, row_pitch):
    t = lg_ref.shape[1]
    per_g = N_EXPERTS // N_GROUPS
    sub = lax.broadcasted_iota(jnp.int32, (per_g, t), 0).astype(F32)
    neg = -jnp.inf
    scores, sel = [], []
    for g in range(N_GROUPS):
        rows = slice(g * per_g, (g + 1) * per_g)
        sc = jax.nn.sigmoid(lg_ref[rows, :])
        scores.append(sc)
        sel.append(sc + bias_ref[rows, :])

    gs = []
    for g in range(N_GROUPS):
        m1 = jnp.max(sel[g], axis=0, keepdims=True)
        first = _first_index(sel[g] == m1, sub, float(per_g))
        m2 = jnp.max(jnp.where(sub == first, neg, sel[g]), axis=0, keepdims=True)
        gs.append(m1 + m2)
    gsc = jnp.concatenate(gs, axis=0)
    gsub = lax.broadcasted_iota(jnp.int32, (N_GROUPS, t), 0).astype(F32)
    gkeep = jnp.zeros((N_GROUPS, t), F32)
    for _ in range(TOPK_GROUPS):
        m = jnp.max(gsc, axis=0, keepdims=True)
        pick = gsub == _first_index(gsc == m, gsub, float(N_GROUPS))
        gkeep = jnp.where(pick, 1.0, gkeep)
        gsc = jnp.where(pick, neg, gsc)

    eidx = [sub + float(g * per_g) for g in range(N_GROUPS)]
    masked = [jnp.where(jnp.broadcast_to(gkeep[g:g + 1, :], (per_g, t)) > 0.5, sel[g], neg)
              for g in range(N_GROUPS)]
    chosen = [jnp.zeros((per_g, t), F32) for _ in range(N_GROUPS)]
    wsel = [jnp.zeros((per_g, t), F32) for _ in range(N_GROUPS)]
    picked = []
    for _ in range(TOP_K):
        m = jnp.max(functools.reduce(jnp.maximum, masked), axis=0, keepdims=True)
        cand = functools.reduce(jnp.minimum, [jnp.where(masked[g] == m, eidx[g], float(N_EXPERTS))
                                              for g in range(N_GROUPS)])
        e_first = jnp.min(cand, axis=0, keepdims=True)
        picked.append(e_first)
        for g in range(N_GROUPS):
            pick = eidx[g] == e_first
            chosen[g] = jnp.where(pick, 1.0, chosen[g])
            wsel[g] = jnp.where(pick, scores[g], wsel[g])
            masked[g] = jnp.where(pick, neg, masked[g])

    denom = jnp.sum(functools.reduce(jnp.add, wsel), axis=0, keepdims=True)

    c_all = jnp.concatenate(chosen, axis=0)
    nblk = t // LANES
    stacked = jnp.concatenate([c_all[:, b * LANES:(b + 1) * LANES] for b in range(nblk)], axis=0)
    within = jnp.dot(stacked.astype(BF16), tri_ref[...], preferred_element_type=F32)
    ranks, before = [], jnp.zeros((N_EXPERTS, 1), F32)
    for b in range(nblk):
        rows = slice(b * N_EXPERTS, (b + 1) * N_EXPERTS)
        ranks.append(within[rows, :] + before)
        before = before + within[rows, LANES - 1:LANES] + stacked[rows, LANES - 1:LANES]
    rank = jnp.concatenate(ranks, axis=1)
    counts = before
    padded = jnp.ceil(counts / SUBLANES) * SUBLANES
    below = (lax.broadcasted_iota(jnp.int32, (N_EXPERTS, N_EXPERTS), 1)
             < lax.broadcasted_iota(jnp.int32, (N_EXPERTS, N_EXPERTS), 0)).astype(F32)
    seg = jnp.dot(below, jnp.broadcast_to(padded, (N_EXPERTS, LANES)), preferred_element_type=F32,
                  precision=lax.Precision.HIGHEST)
    row_of = seg[:, :1] + rank

    dest_rows, w_rows = [], []
    for k in range(TOP_K):
        d_k = jnp.zeros((per_g, t), F32)
        w_k = jnp.zeros((per_g, t), F32)
        for g in range(N_GROUPS):
            hit = eidx[g] == picked[k]
            d_k = jnp.where(hit, row_of[g * per_g:(g + 1) * per_g, :], d_k)
            w_k = jnp.where(hit, wsel[g], w_k)
        dest_rows.append(jnp.sum(d_k, axis=0, keepdims=True))
        w_rows.append(jnp.sum(w_k, axis=0, keepdims=True) / denom * ROUTED_SCALE)
    fill = [jnp.zeros((SUBLANES - TOP_K, t), F32)]
    dest8 = (jnp.concatenate(dest_rows + fill, axis=0) * row_pitch).astype(jnp.int32)
    w8 = jnp.concatenate(w_rows + fill, axis=0)
    for j in range(t // LANES):
        dest_ref[j] = dest8[:, j * LANES:(j + 1) * LANES]
        w_ref[j] = w8[:, j * LANES:(j + 1) * LANES]
    seg_ref[...] = seg.astype(jnp.int32)
    cnt_ref[...] = jnp.broadcast_to(counts, (N_EXPERTS, LANES)).astype(jnp.int32)


def _route(logits_t, router_bias, tile, row_pitch):
    ne, n = logits_t.shape
    nt = n // tile
    tri = jnp.asarray(np.triu(np.ones((LANES, LANES), np.float32), 1), dtype=BF16)
    per_tile = lambda rows, width: pl.BlockSpec((None, rows, width), lambda i: (i, 0, 0))
    table = pl.BlockSpec((None, tile // LANES, SUBLANES, LANES), lambda i: (i, 0, 0, 0))
    dest, wts, seg, cnt = pl.pallas_call(
        functools.partial(_route_kernel, row_pitch=row_pitch),
        grid=(nt,),
        in_specs=[pl.BlockSpec((ne, tile), lambda i: (0, i)),
                  pl.BlockSpec((ne, 1), lambda i: (0, 0)),
                  pl.BlockSpec((LANES, LANES), lambda i: (0, 0))],
        out_specs=[table, table, per_tile(ne, LANES), per_tile(ne, LANES)],
        out_shape=[jax.ShapeDtypeStruct((nt, tile // LANES, SUBLANES, LANES), jnp.int32),
                   jax.ShapeDtypeStruct((nt, tile // LANES, SUBLANES, LANES), F32),
                   jax.ShapeDtypeStruct((nt, ne, LANES), jnp.int32),
                   jax.ShapeDtypeStruct((nt, ne, LANES), jnp.int32)],
        compiler_params=_cparams("parallel"),
    )(logits_t, router_bias.astype(F32).reshape(ne, 1), tri)
    return seg[:, :, 0].reshape(-1), cnt[:, :, 0].reshape(-1), dest, wts


def _swiglu(xb, wgu_ref, wdn_ref):
    gu = jnp.dot(xb, wgu_ref[...], preferred_element_type=F32)
    half = wgu_ref.shape[1] // 2
    act = _silu(gu[:, :half]) * gu[:, half:]
    return jnp.dot(act.astype(BF16), wdn_ref[...], preferred_element_type=F32)


def _slabs_to_rows(p, il_ref, r):
    n = il_ref.shape[0] // r
    il_ref[...] = pltpu.bitcast(p, BF16).astype(F32)
    return jnp.concatenate([il_ref[pl.ds(2 * c + h, r, stride=n), :]
                            for h in range(2) for c in range(n // 2)], axis=1)


def _rows_to_slab_rows(v, il_ref):
    r = v.shape[0]
    n = il_ref.shape[0] // r
    for h in range(2):
        for c in range(n // 2):
            lane0 = (h * (n // 2) + c) * LANES
            il_ref[pl.ds(2 * c + h, r, stride=n), :] = v[:, lane0:lane0 + LANES]
    return il_ref[...].astype(BF16)


def _rows_to_slabs(v, il_ref):
    return pltpu.bitcast(_rows_to_slab_rows(v, il_ref), U32)


def _moe_kernel(start_ref, count_ref, hp_ref, hb_ref, dest_ref, wrow_ref, wgu_hbm, wdn_hbm,
                sgu_ref, sdn_ref, xm_ref, g2_ref, o_ref, xy_ref, il_ref, yp_ref, zf_ref,
                comb_ref, wgu_buf, wdn_buf, wsem):
    t = pl.program_id(0)
    s = pl.program_id(1)
    slab = xm_ref.shape[1] // (2 * LANES)
    tile = hp_ref.shape[0] // (2 * slab)
    group_rows = SUBLANES * slab

    def weight_copies(e, slot):
        return (pltpu.make_async_copy(wgu_hbm.at[e], wgu_buf.at[slot], wsem.at[0, slot]),
                pltpu.make_async_copy(wdn_hbm.at[e], wdn_buf.at[slot], wsem.at[1, slot]))

    def slab_at(ref, first):
        return ref.at[pl.ds(pl.multiple_of(first, slab), slab), :]

    @pl.when(s == 0)
    def _first_weights():
        for p in range(2):
            for cp in weight_copies(p, p):
                cp.start()

    @pl.when(jnp.logical_and(t == 0, s == 0))
    def _clear():
        xy_ref[...] = jnp.zeros(xy_ref.shape, U32)

    @pl.when(s == 0)
    def _dispatch():
        def group(j, carry):
            for sub in range(LANES // SUBLANES):
                p0 = pl.multiple_of((j * (LANES // SUBLANES) + sub) * 2 * group_rows, 2 * group_rows)
                rows = pltpu.bitcast(hp_ref[pl.ds(p0, 2 * group_rows), :], U32)
                for u in range(SUBLANES):
                    for k in range(TOP_K):
                        slab_at(xy_ref, dest_ref[j, k, sub * SUBLANES + u])[...] = (
                            rows[u * slab:(u + 1) * slab, :])
            return carry

        lax.fori_loop(0, tile // LANES, group, 0)

    @pl.when(s == 0)
    def _experts():
        def chunk_base(e, c):
            return pl.multiple_of((start_ref[t * N_EXPERTS + e] + c * EROWS) * slab, group_rows)

        def load_rows(base, p):
            return _slabs_to_rows(xy_ref[pl.ds(base, EROWS * slab), :], il_ref.at[p], EROWS).astype(BF16)

        def put_rows(p, base, valid):
            def put(j, c2):
                r0 = pl.multiple_of(j * group_rows, group_rows)
                xy_ref[pl.ds(base + r0, group_rows), :] = yp_ref[p, pl.ds(r0, group_rows), :]
                return c2

            lax.fori_loop(0, (jnp.clip(valid, 0, EROWS) + SUBLANES - 1) // SUBLANES, put, 0)

        def expert_pair(i, carry0):
            wslot = lax.rem(i, 2) * 2

            @pl.when(i + 1 < N_EXPERTS // 2)
            def _prefetch():
                for p in range(2):
                    for cp in weight_copies(2 * i + 2 + p, 2 - wslot + p):
                        cp.start()

            for p in range(2):
                for cp in weight_copies(2 * i + p, wslot + p):
                    cp.wait()
            counts = [count_ref[t * N_EXPERTS + 2 * i + p] for p in range(2)]
            bases = [chunk_base(2 * i + p, 0) for p in range(2)]
            xs = [load_rows(bases[p], p) for p in range(2)]
            ys = [_swiglu(xs[p], wgu_buf.at[wslot + p], wdn_buf.at[wslot + p]) for p in range(2)]
            for p in range(2):
                yp_ref[p] = _rows_to_slabs(ys[p], il_ref.at[p])
            for p in range(2):
                put_rows(p, bases[p], counts[p])

            for p in range(2):
                def more(c, carry, p=p):
                    base = chunk_base(2 * i + p, c)
                    y = _swiglu(load_rows(base, p), wgu_buf.at[wslot + p], wdn_buf.at[wslot + p])
                    yp_ref[p] = _rows_to_slabs(y, il_ref.at[p])
                    put_rows(p, base, counts[p] - c * EROWS)
                    return carry

                lax.fori_loop(1, (counts[p] + EROWS - 1) // EROWS, more, 0)
            return carry0

        lax.fori_loop(0, N_EXPERTS // 2, expert_pair, 0)

    @pl.when(s >= 1)
    def _combine():
        sb = s - 1

        def group(j, carry):
            tj = sb * (CROWS // LANES) + j
            for sub in range(LANES // SUBLANES):
                r0 = pl.multiple_of(j * LANES + sub * SUBLANES, SUBLANES)
                half = sub % 2
                for u in range(SUBLANES):
                    lane = sub * SUBLANES + u
                    acc = None
                    for k in range(TOP_K):
                        z = pltpu.bitcast(slab_at(xy_ref, dest_ref[tj, k, lane])[...], BF16).astype(F32)
                        acc = wrow_ref[tj, k, lane] * z if acc is None else acc + wrow_ref[tj, k, lane] * z
                    zf_ref[half, u * 2 * slab:(u + 1) * 2 * slab, :] = acc
                comb_ref[pl.ds(r0, SUBLANES), :] = jnp.concatenate(
                    [zf_ref[half, pl.ds(2 * c + h, SUBLANES, stride=2 * slab), :]
                     for h in range(2) for c in range(slab)], axis=1)
            return carry

        lax.fori_loop(0, CROWS // LANES, group, 0)
        shared = _swiglu(hb_ref[...], sgu_ref, sdn_ref)
        o_ref[...] = xm_ref[...] + g2_ref[...] * (comb_ref[...] + shared)


def _moe(h, h_slabs, xm, g2, logits_t, router_bias, w_gu, w_dn, s_gu, s_dn, layer, seq):
    n, d = h.shape
    slab = d // (2 * LANES)
    tile = min(MOE_TILE, seq)
    nt = n // tile
    per_b = seq // tile
    csteps = tile // CROWS
    assert seq % tile == 0 and tile % CROWS == 0
    assert tile % LANES == 0 and CROWS % LANES == 0
    seg_start, seg_count, dest_rows, w_rows = _route(logits_t, router_bias, tile, slab)
    rows = _moe_rows(tile)
    table = pl.BlockSpec((None, tile // LANES, SUBLANES, LANES), lambda t, s, a, b: (t, 0, 0, 0),
                         memory_space=pltpu.SMEM)
    cst = lambda s: jnp.maximum(s - 1, 0)
    crow = pl.BlockSpec((CROWS, d), lambda t, s, a, b: (t * csteps + cst(s), 0))
    hbm = pl.BlockSpec(memory_space=pl.ANY)
    single = pl.Buffered(1)
    return pl.pallas_call(
        _moe_kernel,
        grid_spec=pltpu.PrefetchScalarGridSpec(
            num_scalar_prefetch=2,
            grid=(nt, 1 + csteps),
            in_specs=[pl.BlockSpec((tile * 2 * slab, LANES), lambda t, s, a, b: (t, 0),
                                   pipeline_mode=single),
                      crow, table, table,
                      hbm, hbm,
                      pl.BlockSpec((None,) + s_gu.shape[1:], lambda t, s, a, b: (layer, 0, 0),
                                   pipeline_mode=single),
                      pl.BlockSpec((None,) + s_dn.shape[1:], lambda t, s, a, b: (layer, 0, 0),
                                   pipeline_mode=single),
                      crow,
                      pl.BlockSpec((None, 1, d), lambda t, s, a, b: (t // per_b, 0, 0))],
            out_specs=crow,
            scratch_shapes=[pltpu.VMEM((rows * slab, LANES), U32),
                            pltpu.VMEM((2, EROWS * 2 * slab, LANES), F32),
                            pltpu.VMEM((2, EROWS * slab, LANES), U32),
                            pltpu.VMEM((2, SUBLANES * 2 * slab, LANES), F32),
                            pltpu.VMEM((CROWS, d), F32),
                            pltpu.VMEM((4, d, 2 * EXPERT_DIM), BF16),
                            pltpu.VMEM((4, EXPERT_DIM, d), BF16),
                            pltpu.SemaphoreType.DMA((2, 4))]),
        out_shape=jax.ShapeDtypeStruct((n, d), F32),
        compiler_params=_cparams("arbitrary", "arbitrary"),
    )(seg_start, seg_count, h_slabs, h, dest_rows, w_rows, w_gu, w_dn, s_gu, s_dn, xm, g2)


def _mla_prep_kernel(x_ref, n1_ref, sc1_ref, sh1_ref, nk_ref, sck_ref, shk_ref,
                     wdq_ref, gql_ref, wuq_ref, wuqs_ref, wdown_ref, gkl_ref, wk_ref, gk_ref, wuv_ref,
                     qa_ref, qb_ref, kc_ref, ksa_ref, ksb_ref, q_ref, k_ref, v_ref):
    xn = _rms(x_ref[...])
    hq = xn * n1_ref[...] * (1.0 + sc1_ref[...]) + sh1_ref[...]
    hk = xn * nk_ref[...] * (1.0 + sck_ref[...]) + shk_ref[...]
    n_heads = q_ref.shape[1] // LANES
    half = ROPE_DIM // 2

    cq = (_rms(jnp.dot(hq.astype(BF16), wdq_ref[...], preferred_element_type=F32)) * gql_ref[...]).astype(BF16)
    q_raw = jnp.dot(cq, wuq_ref[...], preferred_element_type=F32)
    q_swp = jnp.dot(cq, wuqs_ref[...], preferred_element_type=F32)
    qa, qb = qa_ref[...], qb_ref[...]
    for h in range(n_heads):
        sl = slice(h * LANES, (h + 1) * LANES)
        t = q_raw[:, sl]
        r = lax.rsqrt(jnp.sum(t * t, axis=-1, keepdims=True) / QK_DIM + NORM_EPS)
        q_ref[:, sl] = (r * (t * qa + q_swp[:, sl] * qb)).astype(q_ref.dtype)

    down = jnp.dot(hk.astype(BF16), wdown_ref[...], preferred_element_type=F32)
    ckv = (_rms(down[:, :KV_RANK]) * gkl_ref[...]).astype(BF16)
    pe = down[:, KV_RANK:]
    ss_pe = jnp.sum(pe * pe, axis=-1, keepdims=True)
    rot = (pe * kc_ref[...] + pltpu.roll(pe, LANES - half, 1) * ksa_ref[...]
           + pltpu.roll(pe, half, 1) * ksb_ref[...])
    rot_hi = rot.astype(BF16)
    rot_lo = (rot - rot_hi.astype(F32)).astype(BF16)
    k_raw = jnp.dot(jnp.concatenate([ckv, rot_hi, rot_lo], axis=1), wk_ref[...],
                    preferred_element_type=F32)
    nope = lax.broadcasted_iota(jnp.int32, (x_ref.shape[0], LANES), 1) < NOPE_DIM
    for h in range(n_heads):
        sl = slice(h * LANES, (h + 1) * LANES)
        t = k_raw[:, sl]
        ss = jnp.sum(jnp.where(nope, t * t, 0.0), axis=-1, keepdims=True) + ss_pe
        k_ref[:, sl] = (t * lax.rsqrt(ss / QK_DIM + NORM_EPS) * gk_ref[...]).astype(k_ref.dtype)
    v_ref[...] = jnp.dot(ckv, wuv_ref[...], preferred_element_type=F32).astype(v_ref.dtype)


def _pad_heads(w, n_heads, width):
    r = w.shape[0]
    return jnp.pad(w.reshape(r, n_heads, width), ((0, 0), (0, 0), (0, LANES - width))).reshape(r, n_heads * LANES)


def _swap_rope_halves(a):
    half = ROPE_DIM // 2
    lead = a.shape[:-1]
    return jnp.concatenate([jnp.zeros(lead + (NOPE_DIM,), a.dtype), a[..., NOPE_DIM + half:QK_DIM],
                            a[..., NOPE_DIM:NOPE_DIM + half],
                            jnp.zeros(lead + (LANES - QK_DIM,), a.dtype)], axis=-1)


def _mla_prep(xf, pos, n1, sc1, sh1, nk, sck, shk, w_dq, g_ql, w_uq, g_q, w_down, g_kl, w_uk, w_uv,
              g_k, seq):
    n, d = xf.shape
    n_heads = w_uv.shape[1] // V_DIM
    q_rank = w_dq.shape[1]
    hw = n_heads * LANES
    tm = min(ROW_TILE, seq)
    per_b = seq // tm

    half = ROPE_DIM // 2
    inv = jnp.power(ROPE_THETA, -jnp.arange(half, dtype=F32) / half)
    ang = pos.astype(F32).reshape(n, 1) * inv
    cos, sin = jnp.cos(ang), jnp.sin(ang)
    ones = jnp.ones((n, NOPE_DIM), F32)
    zeros = lambda wdt: jnp.zeros((n, wdt), F32)
    tail = LANES - QK_DIM
    pad_gain = lambda g: jnp.pad(g.reshape(1, QK_DIM), ((0, 0), (0, tail)))
    q_mul = QK_DIM ** -0.5 * LOG2E
    qa_t = jnp.concatenate([ones, cos, cos, zeros(tail)], axis=1) * pad_gain(g_q) * q_mul
    qb_t = (jnp.concatenate([zeros(NOPE_DIM), -sin, sin, zeros(tail)], axis=1)
            * _swap_rope_halves(g_q.reshape(1, QK_DIM)) * q_mul)
    gk1, gk2 = g_k[NOPE_DIM:NOPE_DIM + half], g_k[NOPE_DIM + half:]
    rest = zeros(LANES - ROPE_DIM)
    kc_t = jnp.concatenate([cos * gk1, cos * gk2, rest], axis=1)
    ksa_t = jnp.concatenate([-sin * gk2, zeros(half), rest], axis=1)
    ksb_t = jnp.concatenate([zeros(half), sin * gk1, rest], axis=1)
    gk_mask = jnp.concatenate([g_k[:NOPE_DIM], jnp.ones((ROPE_DIM,), F32),
                               jnp.zeros((tail,), F32)]).reshape(1, LANES)

    wuq_p = _pad_heads(w_uq, n_heads, QK_DIM).astype(BF16)
    wuq_s = _swap_rope_halves(w_uq.reshape(q_rank, n_heads, QK_DIM)).reshape(q_rank, hw).astype(BF16)
    wdown_p = jnp.pad(w_down, ((0, 0), (0, LANES - ROPE_DIM))).astype(BF16)
    eye = jnp.pad(jnp.eye(ROPE_DIM, dtype=F32), ((0, LANES - ROPE_DIM), (NOPE_DIM, tail)))
    pe_rows = jnp.tile(eye, (1, n_heads))
    wk = jnp.concatenate([_pad_heads(w_uk, n_heads, NOPE_DIM), pe_rows, pe_rows], axis=0).astype(BF16)

    vec = pl.BlockSpec((None, 1, d), lambda i: (i // per_b, 0, 0))
    full = lambda a: pl.BlockSpec(a.shape, lambda i: (0,) * a.ndim)
    row = lambda wdt: pl.BlockSpec((tm, wdt), lambda i: (i, 0))
    args = [xf, n1, sc1, sh1, nk, sck, shk,
            w_dq.astype(BF16), g_ql.reshape(1, q_rank), wuq_p, wuq_s,
            wdown_p, g_kl.reshape(1, KV_RANK), wk, gk_mask, w_uv.astype(BF16),
            qa_t, qb_t, kc_t, ksa_t, ksb_t]
    specs = [row(d), full(n1), vec, vec, full(nk), vec, vec] + [full(a) for a in args[7:16]] + [row(LANES)] * 5
    return pl.pallas_call(
        _mla_prep_kernel,
        grid=(n // tm,),
        in_specs=specs,
        out_specs=[row(hw), row(hw), row(n_heads * V_DIM)],
        out_shape=[jax.ShapeDtypeStruct((n, hw), BF16), jax.ShapeDtypeStruct((n, hw), BF16),
                   jax.ShapeDtypeStruct((n, n_heads * V_DIM), BF16)],
        compiler_params=_cparams("parallel"),
    )(*args)


def _causal_bias(seq):
    a = np.arange(QBLK)[:, None]
    c = np.arange(seq)[None, :]
    return np.where(c <= (seq - QBLK) + a, 0.0, MASKED).astype(np.float32)


def _mla_attn_kernel(q_ref, k_ref, v_ref, mask_ref, wa_ref, wb_ref, o_ref, wa_out, wb_out,
                     s_ref, p_ref, ve_ref):
    wa_out[...] = wa_ref[...].astype(BF16)
    wb_out[...] = wb_ref[...].astype(BF16)
    seq = q_ref.shape[0]
    lane_lo = lax.broadcasted_iota(jnp.int32, (QBLK, LANES), 1) < V_DIM

    def qk_of(blk, h, w):
        lanes = slice(h * LANES, (h + 1) * LANES)
        return q_ref[w - QBLK:w, lanes], k_ref[0:w, lanes]

    _attention_blocks(seq, qk_of, v_ref, mask_ref, s_ref, p_ref, ve_ref, o_ref, lane_lo)


def _mla_attention(q, k, v, cast_weights, layer):
    bsz, seq, hw = q.shape
    n_pairs = hw // (2 * LANES)
    qk = pl.BlockSpec((None, seq, 2 * LANES), lambda b, h: (b, 0, h))
    vo = pl.BlockSpec((None, seq, LANES), lambda b, h: (b, 0, h))
    grid = (bsz, n_pairs)
    w_in, w_out, w_shapes = _cast_rider(cast_weights, layer, grid)
    return pl.pallas_call(
        _mla_attn_kernel,
        grid=grid,
        in_specs=[qk, qk, vo, pl.BlockSpec((QBLK, seq), lambda b, h: (0, 0))] + w_in,
        out_specs=[vo] + w_out,
        scratch_shapes=_attn_scratch(seq),
        out_shape=[jax.ShapeDtypeStruct(v.shape, BF16)] + w_shapes,
        compiler_params=_cparams("parallel", "parallel"),
    )(q, k, v, jnp.asarray(_causal_bias(seq)), *cast_weights)


def kernel(x, c, pos, ada_w, ada_b, norm1_g, norm2_g, a_w_qkv, a_q_norm, a_k_norm, a_w_o, kv_ada_w, kv_ada_b, kv_norm_g, kv_w_down, kv_latent_norm, kv_w_uk, kv_w_uv, kv_k_norm, b_w_dq, b_q_latent_norm, b_w_uq, b_q_norm, b_w_o, moe_w_router, moe_router_bias, moe_w_gate_up, moe_w_down, moe_shared_gate_up, moe_shared_down):
    bsz, seq, d = x.shape
    n = bsz * seq
    depth = ada_w.shape[0]
    n_a = a_w_qkv.shape[0]
    assert seq % QBLK == 0
    xf = x.reshape(n, d)

    expert_weights = (moe_w_gate_up, moe_w_down)
    s_gu = moe_shared_gate_up.astype(BF16)
    s_dn = moe_shared_down.astype(BF16)
    row = lambda g: g.reshape(1, d)

    for layer in range(depth):
        mod = _modulation(c, ada_w, ada_b, layer).reshape(bsz, 6, 1, d)
        sh1, sc1, g1, sh2, sc2, g2 = [mod[:, i] for i in range(6)]
        if layer < n_a:
            qkv = _qkv_proj(xf, row(norm1_g[layer]), sc1, sh1, a_w_qkv[layer].astype(BF16), seq)
            o, w_gu, w_dn = _dilated_attention(qkv.reshape(bsz, seq, -1), pos, a_q_norm[layer],
                                               a_k_norm[layer], expert_weights, layer)
            w_o = a_w_o[layer]
        else:
            j = layer - n_a
            kmod = _modulation(c, kv_ada_w[None], kv_ada_b[None], 0).reshape(bsz, 2, 1, d)
            q, k, v = _mla_prep(xf, pos, row(norm1_g[layer]), sc1, sh1, row(kv_norm_g), kmod[:, 1],
                                kmod[:, 0], b_w_dq[j], b_q_latent_norm[j], b_w_uq[j], b_q_norm[j],
                                kv_w_down, kv_latent_norm, kv_w_uk, kv_w_uv, kv_k_norm, seq)
            o, w_gu, w_dn = _mla_attention(q.reshape(bsz, seq, -1), k.reshape(bsz, seq, -1),
                                           v.reshape(bsz, seq, -1), expert_weights, layer)
            w_o = b_w_o[j]
        xm, h2, h2_slabs, logits = _post_attn(o.reshape(n, d), w_o.astype(BF16), xf, g1,
                                              row(norm2_g[layer]), sc2, sh2, moe_w_router[layer], seq)
        xf = _moe(h2, h2_slabs, xm, g2, logits, moe_router_bias[layer], w_gu, w_dn, s_gu, s_dn,
                  layer, seq)
    return xf.reshape(bsz, seq, d)
```

```python
import functools
import math

import numpy as np
import jax
import jax.numpy as jnp
from jax import lax
from jax.experimental import pallas as pl
from jax.experimental.pallas import tpu as pltpu

F32 = jnp.float32
BF16 = jnp.bfloat16
U32 = jnp.uint32

NORM_EPS = 1e-6
MASKED = -1e30
LOG2E = math.log2(math.e)

LANES = 128
SUBLANES = 8
VMEM_LIMIT = 56 * 1024 * 1024

A_HEAD_DIM = 64
DILATED_PAIRS = ((128, 1), (512, 4), (2048, 16))
QBLK = 256
NOPE_DIM = 64
ROPE_DIM = 32
QK_DIM = NOPE_DIM + ROPE_DIM
V_DIM = 64
KV_RANK = 256
ROPE_THETA = 10000.0
N_EXPERTS = 64
N_GROUPS = 8
TOPK_GROUPS = 4
TOP_K = 6
EXPERT_DIM = 256
ROUTED_SCALE = 2.5
MOE_TILE = 2048
EROWS = 256
CROWS = 256

ROW_TILE = 512


def _cparams(*sem):
    return pltpu.CompilerParams(dimension_semantics=sem, vmem_limit_bytes=VMEM_LIMIT)


def _silu(v):
    return v * jax.nn.sigmoid(v)


def _rms(v):
    return v * lax.rsqrt(jnp.mean(v * v, axis=-1, keepdims=True) + NORM_EPS)


def _dot_nt(a, b):
    return lax.dot_general(a, b, (((1,), (1,)), ((), ())), preferred_element_type=F32)


def _mod_kernel(c_ref, w_ref, b_ref, o_ref):
    cond = _silu(c_ref[...])
    o_ref[...] = jnp.dot(cond, w_ref[...], preferred_element_type=F32,
                         precision=lax.Precision.HIGHEST) + b_ref[...]


def _modulation(c, w, b, layer):
    bsz, d = c.shape
    m = w.shape[-1]
    tn = 1024
    return pl.pallas_call(
        _mod_kernel,
        grid=(m // tn,),
        in_specs=[pl.BlockSpec((bsz, d), lambda j: (0, 0)),
                  pl.BlockSpec((None, d, tn), lambda j: (layer, 0, j)),
                  pl.BlockSpec((None, 1, tn), lambda j: (layer, 0, j))],
        out_specs=pl.BlockSpec((bsz, tn), lambda j: (0, j)),
        out_shape=jax.ShapeDtypeStruct((bsz, m), F32),
        compiler_params=_cparams("arbitrary"),
    )(c, w, b.reshape(b.shape[0], 1, m))


def _qkv_kernel(x_ref, g_ref, sc_ref, sh_ref, w_ref, o_ref):
    h = _rms(x_ref[...]) * g_ref[...]
    h = h * (1.0 + sc_ref[...]) + sh_ref[...]
    o_ref[...] = jnp.dot(h.astype(BF16), w_ref[...], preferred_element_type=F32).astype(o_ref.dtype)


def _qkv_proj(xf, g, sc, sh, w, seq):
    n, d = xf.shape
    m = w.shape[1]
    tm = min(ROW_TILE, seq)
    per_b = seq // tm
    vec = pl.BlockSpec((None, 1, d), lambda i: (i // per_b, 0, 0))
    return pl.pallas_call(
        _qkv_kernel,
        grid=(n // tm,),
        in_specs=[pl.BlockSpec((tm, d), lambda i: (i, 0)),
                  pl.BlockSpec((1, d), lambda i: (0, 0)),
                  vec, vec,
                  pl.BlockSpec((d, m), lambda i: (0, 0))],
        out_specs=pl.BlockSpec((tm, m), lambda i: (i, 0)),
        out_shape=jax.ShapeDtypeStruct((n, m), BF16),
        compiler_params=_cparams("parallel"),
    )(xf, g, sc, sh, w)


ATT_SLOTS = 4


def _attention_blocks(seq, qk_of, v_ref, bias_ref, s_ref, p_ref, ve_ref, o_ref, lane_lo,
                      diagonal_bias=False):
    nblk = seq // QBLK

    for r0 in range(0, seq, QBLK):
        v = v_ref[r0:r0 + QBLK, :].astype(F32)
        ve_ref[0, r0:r0 + QBLK, :] = jnp.where(lane_lo, v, 1.0).astype(BF16)
        ve_ref[1, r0:r0 + QBLK, :] = jnp.where(lane_lo, 1.0, v).astype(BF16)

    def slot(blk, h):
        return (blk % 2) * 2 + h

    def scores(blk):
        w = (blk + 1) * QBLK
        lo = w - QBLK if diagonal_bias else 0
        bias = bias_ref[:, seq - w + lo:seq]
        for h in range(2):
            q, k = qk_of(blk, h, w)
            s = _dot_nt(q, k)
            if lo:
                s_ref[slot(blk, h), :, 0:lo] = s[:, :lo]
            s_ref[slot(blk, h), :, lo:w] = s[:, lo:] + bias

    for blk in range(nblk):
        w = (blk + 1) * QBLK
        scores(blk)
        for h in range(2):
            s = s_ref[slot(blk, h), :, 0:w]
            p_ref[slot(blk, h), :, 0:w] = jnp.exp2(s - jnp.max(s, axis=-1, keepdims=True)).astype(BF16)
        acc_a, acc_b = [jnp.dot(p_ref[slot(blk, h), :, 0:w], ve_ref[h, 0:w, :],
                                preferred_element_type=F32) for h in range(2)]
        sums = pltpu.roll(jnp.where(lane_lo, acc_b, acc_a), LANES // 2, 1)
        o_ref[blk * QBLK:w, :] = (jnp.where(lane_lo, acc_a, acc_b) / sums).astype(o_ref.dtype)


def _attn_scratch(seq):
    return [pltpu.VMEM((ATT_SLOTS, QBLK, seq), F32), pltpu.VMEM((ATT_SLOTS, QBLK, seq), BF16),
            pltpu.VMEM((2, seq, LANES), BF16)]


def _cast_rider(weights, layer, grid):
    steps = grid[0] * grid[1]
    in_specs, out_specs, out_shapes = [], [], []
    for w in weights:
        n_exp = w.shape[1]
        assert n_exp % steps == 0
        per = n_exp // steps
        blk = (per,) + w.shape[2:]
        in_specs.append(pl.BlockSpec((None,) + blk, lambda b, h: (layer, b * grid[1] + h, 0, 0)))
        out_specs.append(pl.BlockSpec(blk, lambda b, h: (b * grid[1] + h, 0, 0)))
        out_shapes.append(jax.ShapeDtypeStruct(w.shape[1:], BF16))
    return in_specs, out_specs, out_shapes


def _split3(v):
    p1 = v.astype(BF16).astype(F32)
    p2 = (v - p1).astype(BF16).astype(F32)
    return p1, p2, v - p1 - p2


def _dilated_log_multiplicity(seq):
    a = np.arange(QBLK)[:, None]
    c = np.arange(seq)[None, :]
    d = (seq - QBLK) + a - c
    mult = np.zeros((QBLK, seq), np.int32)
    for win, dil in DILATED_PAIRS:
        mult += ((d >= 0) & (d % dil == 0) & (d <= win)).astype(np.int32)
    return np.where(mult > 0, np.log2(np.maximum(mult, 1)), MASKED).astype(np.float32)


N_FEAT = 9


def _dil_attn_kernel(q_ref, k_ref, v_ref, posc_ref, sfeat_ref, gq_ref, gk_ref, logm_ref,
                     wa_ref, wb_ref, o_ref, wa_out, wb_out, qa_ref, qb_ref, ks_ref, s_ref, p_ref,
                     ve_ref):
    wa_out[...] = wa_ref[...].astype(BF16)
    wb_out[...] = wb_ref[...].astype(BF16)
    seq = q_ref.shape[0]
    lane = lax.broadcasted_iota(jnp.int32, (QBLK, LANES), 1)
    lane_lo = lane < A_HEAD_DIM
    piece = [functools.reduce(jnp.logical_or, [lane == i + 3 * j for j in range(N_FEAT // 3)])
             for i in range(3)]

    def head_norm(ref, r0, g_ref, mul):
        t = ref[r0:r0 + QBLK, :].astype(F32)
        sq = t * t
        ssa = jnp.sum(jnp.where(lane_lo, sq, 0.0), axis=-1, keepdims=True)
        ssb = jnp.sum(jnp.where(lane_lo, 0.0, sq), axis=-1, keepdims=True)
        r = jnp.where(lane_lo, lax.rsqrt(ssa / A_HEAD_DIM + NORM_EPS),
                      lax.rsqrt(ssb / A_HEAD_DIM + NORM_EPS))
        return t * r * (g_ref[...] * mul)

    rel0 = posc_ref[0:1, :]
    for r0 in range(0, seq, QBLK):
        rows = slice(r0, r0 + QBLK)
        qn = head_norm(q_ref, r0, gq_ref, A_HEAD_DIM ** -0.5 * LOG2E)
        qa_ref[rows, :LANES] = jnp.where(lane_lo, qn, 0.0).astype(BF16)
        qb_ref[rows, :LANES] = jnp.where(lane_lo, 0.0, qn).astype(BF16)
        qa_ref[rows, LANES:] = jnp.broadcast_to(sfeat_ref[0:1, :], (QBLK, LANES)).astype(BF16)
        qb_ref[rows, LANES:] = jnp.broadcast_to(sfeat_ref[1:2, :], (QBLK, LANES)).astype(BF16)
        ks_ref[rows, :LANES] = head_norm(k_ref, r0, gk_ref, 1.0).astype(BF16)
        p1, p2, p3 = _split3((posc_ref[rows, :] - rel0).astype(F32))
        feat = jnp.where(piece[0], p1, jnp.where(piece[1], p2, jnp.where(piece[2], p3, 0.0)))
        ks_ref[rows, LANES:] = feat.astype(BF16)

    def qk_of(blk, h, w):
        return (qa_ref, qb_ref)[h][w - QBLK:w, :], ks_ref[0:w, :]

    _attention_blocks(seq, qk_of, v_ref, logm_ref, s_ref, p_ref, ve_ref, o_ref, lane_lo)


def _dilated_attention(qkv, pos, q_norm, k_norm, cast_weights, layer):
    bsz, seq, three_d = qkv.shape
    d = three_d // 3
    n_pairs = d // LANES
    n_heads = d // A_HEAD_DIM
    slopes = jnp.exp2(-8.0 * jnp.arange(1, n_heads + 1, dtype=F32) / n_heads) * LOG2E
    sfeat = jnp.repeat(jnp.stack(_split3(slopes), axis=1), 3, axis=1)
    sfeat = jnp.pad(sfeat, ((0, 0), (0, LANES - N_FEAT))).reshape(n_pairs, 2, LANES)
    logm = jnp.asarray(_dilated_log_multiplicity(seq))
    gq = jnp.tile(q_norm.reshape(1, A_HEAD_DIM), (1, 2))
    gk = jnp.tile(k_norm.reshape(1, A_HEAD_DIM), (1, 2))
    blk = lambda off: pl.BlockSpec((None, seq, LANES), lambda b, h: (b, 0, off + h))
    const2 = pl.BlockSpec((1, LANES), lambda b, h: (0, 0))
    ext = pltpu.VMEM((seq, 2 * LANES), BF16)
    grid = (bsz, n_pairs)
    w_in, w_out, w_shapes = _cast_rider(cast_weights, layer, grid)
    return pl.pallas_call(
        _dil_attn_kernel,
        grid=grid,
        in_specs=[blk(0), blk(n_pairs), blk(2 * n_pairs),
                  pl.BlockSpec((None, seq, 1), lambda b, h: (b, 0, 0)),
                  pl.BlockSpec((None, 2, LANES), lambda b, h: (h, 0, 0)),
                  const2, const2,
                  pl.BlockSpec((QBLK, seq), lambda b, h: (0, 0))] + w_in,
        out_specs=[pl.BlockSpec((None, seq, LANES), lambda b, h: (b, 0, h))] + w_out,
        scratch_shapes=[ext, ext, ext] + _attn_scratch(seq),
        out_shape=[jax.ShapeDtypeStruct((bsz, seq, d), BF16)] + w_shapes,
        compiler_params=_cparams("parallel", "parallel"),
    )(qkv, qkv, qkv, pos.reshape(bsz, seq, 1), sfeat, gq, gk, logm, *cast_weights)


def _post_attn_kernel(o_ref, wo_ref, x_ref, g1_ref, n2_ref, sc_ref, sh_ref, wrh_ref, wrl_ref,
                      xm_ref, h_ref, hs_ref, lg_ref, il_ref):
    y = jnp.dot(o_ref[...], wo_ref[...], preferred_element_type=F32)
    xm = x_ref[...] + g1_ref[...] * y
    xm_ref[...] = xm
    h = _rms(xm) * n2_ref[...]
    h = h * (1.0 + sc_ref[...]) + sh_ref[...]
    h_ref[...] = h.astype(h_ref.dtype)
    hs_ref[...] = _rows_to_slab_rows(h, il_ref)
    h_hi = h.astype(BF16)
    h_lo = (h - h_hi.astype(F32)).astype(BF16)
    lg_ref[...] = (_dot_nt(wrh_ref[...], h_hi) + _dot_nt(wrh_ref[...], h_lo)
                   + _dot_nt(wrl_ref[...], h_hi))


def _post_attn(o, w_o, xf, g1, n2, sc2, sh2, w_router, seq):
    n, d = xf.shape
    ne = w_router.shape[1]
    tm = min(ROW_TILE, seq)
    per_b = seq // tm
    slab_rows = d // LANES
    wr_t = w_router.T.astype(F32)
    wr_hi = wr_t.astype(BF16)
    vec = pl.BlockSpec((None, 1, d), lambda i: (i // per_b, 0, 0))
    row = lambda w: pl.BlockSpec((tm, w), lambda i: (i, 0))
    return pl.pallas_call(
        _post_attn_kernel,
        grid=(n // tm,),
        in_specs=[row(d), pl.BlockSpec((d, d), lambda i: (0, 0)), row(d), vec,
                  pl.BlockSpec((1, d), lambda i: (0, 0)), vec, vec,
                  pl.BlockSpec((ne, d), lambda i: (0, 0)), pl.BlockSpec((ne, d), lambda i: (0, 0))],
        out_specs=[row(d), row(d), pl.BlockSpec((tm * slab_rows, LANES), lambda i: (i, 0)),
                   pl.BlockSpec((ne, tm), lambda i: (0, i))],
        out_shape=[jax.ShapeDtypeStruct((n, d), F32), jax.ShapeDtypeStruct((n, d), BF16),
                   jax.ShapeDtypeStruct((n * slab_rows, LANES), BF16),
                   jax.ShapeDtypeStruct((ne, n), F32)],
        scratch_shapes=[pltpu.VMEM((tm * slab_rows, LANES), F32)],
        compiler_params=_cparams("parallel"),
    )(o, w_o, xf, g1, n2, sc2, sh2, wr_hi, (wr_t - wr_hi.astype(F32)).astype(BF16))


def _moe_rows(tile):
    rows = tile * TOP_K + N_EXPERTS * (SUBLANES - 1) + EROWS
    return -(-rows // SUBLANES) * SUBLANES


def _first_index(hit, index, limit):
    return jnp.min(jnp.where(hit, index, limit), axis=0, keepdims=True)


def _route_kernel(lg_ref, bias_ref, tri_ref, dest_ref, w_ref, seg_ref, cnt_ref, *, row_pitch):
    t = lg_ref.shape[1]
    per_g = N_EXPERTS // N_GROUPS
    sub = lax.broadcasted_iota(jnp.int32, (per_g, t), 0).astype(F32)
    neg = -jnp.inf
    scores, sel = [], []
    for g in range(N_GROUPS):
        rows = slice(g * per_g, (g + 1) * per_g)
        sc = jax.nn.sigmoid(lg_ref[rows, :])
        scores.append(sc)
        sel.append(sc + bias_ref[rows, :])

    gs = []
    for g in range(N_GROUPS):
        m1 = jnp.max(sel[g], axis=0, keepdims=True)
        first = _first_index(sel[g] == m1, sub, float(per_g))
        m2 = jnp.max(jnp.where(sub == first, neg, sel[g]), axis=0, keepdims=True)
        gs.append(m1 + m2)
    gsc = jnp.concatenate(gs, axis=0)
    gsub = lax.broadcasted_iota(jnp.int32, (N_GROUPS, t), 0).astype(F32)
    gkeep = jnp.zeros((N_GROUPS, t), F32)
    for _ in range(TOPK_GROUPS):
        m = jnp.max(gsc, axis=0, keepdims=True)
        pick = gsub == _first_index(gsc == m, gsub, float(N_GROUPS))
        gkeep = jnp.where(pick, 1.0, gkeep)
        gsc = jnp.where(pick, neg, gsc)

    eidx = [sub + float(g * per_g) for g in range(N_GROUPS)]
    masked = [jnp.where(jnp.broadcast_to(gkeep[g:g + 1, :], (per_g, t)) > 0.5, sel[g], neg)
              for g in range(N_GROUPS)]
    chosen = [jnp.zeros((per_g, t), F32) for _ in range(N_GROUPS)]
    wsel = [jnp.zeros((per_g, t), F32) for _ in range(N_GROUPS)]
    picked = []
    for _ in range(TOP_K):
        m = jnp.max(functools.reduce(jnp.maximum, masked), axis=0, keepdims=True)
        cand = functools.reduce(jnp.minimum, [jnp.where(masked[g] == m, eidx[g], float(N_EXPERTS))
                                              for g in range(N_GROUPS)])
        e_first = jnp.min(cand, axis=0, keepdims=True)
        picked.append(e_first)
        for g in range(N_GROUPS):
            pick = eidx[g] == e_first
            chosen[g] = jnp.where(pick, 1.0, chosen[g])
            wsel[g] = jnp.where(pick, scores[g], wsel[g])
            masked[g] = jnp.where(pick, neg, masked[g])

    denom = jnp.sum(functools.reduce(jnp.add, wsel), axis=0, keepdims=True)

    c_all = jnp.concatenate(chosen, axis=0)
    nblk = t // LANES
    stacked = jnp.concatenate([c_all[:, b * LANES:(b + 1) * LANES] for b in range(nblk)], axis=0)
    within = jnp.dot(stacked.astype(BF16), tri_ref[...], preferred_element_type=F32)
    ranks, before = [], jnp.zeros((N_EXPERTS, 1), F32)
    for b in range(nblk):
        rows = slice(b * N_EXPERTS, (b + 1) * N_EXPERTS)
        ranks.append(within[rows, :] + before)
        before = before + within[rows, LANES - 1:LANES] + stacked[rows, LANES - 1:LANES]
    rank = jnp.concatenate(ranks, axis=1)
    counts = before
    padded = jnp.ceil(counts / SUBLANES) * SUBLANES
    below = (lax.broadcasted_iota(jnp.int32, (N_EXPERTS, N_EXPERTS), 1)
             < lax.broadcasted_iota(jnp.int32, (N_EXPERTS, N_EXPERTS), 0)).astype(F32)
    seg = jnp.dot(below, jnp.broadcast_to(padded, (N_EXPERTS, LANES)), preferred_element_type=F32,
                  precision=lax.Precision.HIGHEST)
    row_of = seg[:, :1] + rank

    dest_rows, w_rows = [], []
    for k in range(TOP_K):
        d_k = jnp.zeros((per_g, t), F32)
        w_k = jnp.zeros((per_g, t), F32)
        for g in range(N_GROUPS):
            hit = eidx[g] == picked[k]
            d_k = jnp.where(hit, row_of[g * per_g:(g + 1) * per_g, :], d_k)
            w_k = jnp.where(hit, wsel[g], w_k)
        dest_rows.append(jnp.sum(d_k, axis=0, keepdims=True))
        w_rows.append(jnp.sum(w_k, axis=0, keepdims=True) / denom * ROUTED_SCALE)
    fill = [jnp.zeros((SUBLANES - TOP_K, t), F32)]
    dest8 = (jnp.concatenate(dest_rows + fill, axis=0) * row_pitch).astype(jnp.int32)
    w8 = jnp.concatenate(w_rows + fill, axis=0)
    for j in range(t // LANES):
        dest_ref[j] = dest8[:, j * LANES:(j + 1) * LANES]
        w_ref[j] = w8[:, j * LANES:(j + 1) * LANES]
    seg_ref[...] = seg.astype(jnp.int32)
    cnt_ref[...] = jnp.broadcast_to(counts, (N_EXPERTS, LANES)).astype(jnp.int32)


def _route(logits_t, router_bias, tile, row_pitch):
    ne, n = logits_t.shape
    nt = n // tile
    tri = jnp.asarray(np.triu(np.ones((LANES, LANES), np.float32), 1), dtype=BF16)
    per_tile = lambda rows, width: pl.BlockSpec((None, rows, width), lambda i: (i, 0, 0))
    table = pl.BlockSpec((None, tile // LANES, SUBLANES, LANES), lambda i: (i, 0, 0, 0))
    dest, wts, seg, cnt = pl.pallas_call(
        functools.partial(_route_kernel, row_pitch=row_pitch),
        grid=(nt,),
        in_specs=[pl.BlockSpec((ne, tile), lambda i: (0, i)),
                  pl.BlockSpec((ne, 1), lambda i: (0, 0)),
                  pl.BlockSpec((LANES, LANES), lambda i: (0, 0))],
        out_specs=[table, table, per_tile(ne, LANES), per_tile(ne, LANES)],
        out_shape=[jax.ShapeDtypeStruct((nt, tile // LANES, SUBLANES, LANES), jnp.int32),
                   jax.ShapeDtypeStruct((nt, tile // LANES, SUBLANES, LANES), F32),
                   jax.ShapeDtypeStruct((nt, ne, LANES), jnp.int32),
                   jax.ShapeDtypeStruct((nt, ne, LANES), jnp.int32)],
        compiler_params=_cparams("parallel"),
    )(logits_t, router_bias.astype(F32).reshape(ne, 1), tri)
    return seg[:, :, 0].reshape(-1), cnt[:, :, 0].reshape(-1), dest, wts


def _swiglu(xb, wgu_ref, wdn_ref):
    gu = jnp.dot(xb, wgu_ref[...], preferred_element_type=F32)
    half = wgu_ref.shape[1] // 2
    act = _silu(gu[:, :half]) * gu[:, half:]
    return jnp.dot(act.astype(BF16), wdn_ref[...], preferred_element_type=F32)


def _slabs_to_rows(p, il_ref, r):
    n = il_ref.shape[0] // r
    il_ref[...] = pltpu.bitcast(p, BF16).astype(F32)
    return jnp.concatenate([il_ref[pl.ds(2 * c + h, r, stride=n), :]
                            for h in range(2) for c in range(n // 2)], axis=1)


def _rows_to_slab_rows(v, il_ref):
    r = v.shape[0]
    n = il_ref.shape[0] // r
    for h in range(2):
        for c in range(n // 2):
            lane0 = (h * (n // 2) + c) * LANES
            il_ref[pl.ds(2 * c + h, r, stride=n), :] = v[:, lane0:lane0 + LANES]
    return il_ref[...].astype(BF16)


def _rows_to_slabs(v, il_ref):
    return pltpu.bitcast(_rows_to_slab_rows(v, il_ref), U32)


def _moe_kernel(start_ref, count_ref, hp_ref, hb_ref, dest_ref, wrow_ref, wgu_hbm, wdn_hbm,
                sgu_ref, sdn_ref, xm_ref, g2_ref, o_ref, xy_ref, il_ref, yp_ref, zf_ref,
                comb_ref, wgu_buf, wdn_buf, wsem):
    t = pl.program_id(0)
    s = pl.program_id(1)
    slab = xm_ref.shape[1] // (2 * LANES)
    tile = hp_ref.shape[0] // (2 * slab)
    group_rows = SUBLANES * slab

    def weight_copies(e, slot):
        return (pltpu.make_async_copy(wgu_hbm.at[e], wgu_buf.at[slot], wsem.at[0, slot]),
                pltpu.make_async_copy(wdn_hbm.at[e], wdn_buf.at[slot], wsem.at[1, slot]))

    def slab_at(ref, first):
        return ref.at[pl.ds(pl.multiple_of(first, slab), slab), :]

    @pl.when(s == 0)
    def _first_weights():
        for p in range(2):
            for cp in weight_copies(p, p):
                cp.start()

    @pl.when(jnp.logical_and(t == 0, s == 0))
    def _clear():
        xy_ref[...] = jnp.zeros(xy_ref.shape, U32)

    @pl.when(s == 0)
    def _dispatch():
        def group(j, carry):
            for sub in range(LANES // SUBLANES):
                p0 = pl.multiple_of((j * (LANES // SUBLANES) + sub) * 2 * group_rows, 2 * group_rows)
                rows = pltpu.bitcast(hp_ref[pl.ds(p0, 2 * group_rows), :], U32)
                for u in range(SUBLANES):
                    for k in range(TOP_K):
                        slab_at(xy_ref, dest_ref[j, k, sub * SUBLANES + u])[...] = (
                            rows[u * slab:(u + 1) * slab, :])
            return carry

        lax.fori_loop(0, tile // LANES, group, 0)

    @pl.when(s == 0)
    def _experts():
        def chunk_base(e, c):
            return pl.multiple_of((start_ref[t * N_EXPERTS + e] + c * EROWS) * slab, group_rows)

        def load_rows(base, p):
            return _slabs_to_rows(xy_ref[pl.ds(base, EROWS * slab), :], il_ref.at[p], EROWS).astype(BF16)

        def put_rows(p, base, valid):
            def put(j, c2):
                r0 = pl.multiple_of(j * group_rows, group_rows)
                xy_ref[pl.ds(base + r0, group_rows), :] = yp_ref[p, pl.ds(r0, group_rows), :]
                return c2

            lax.fori_loop(0, (jnp.clip(valid, 0, EROWS) + SUBLANES - 1) // SUBLANES, put, 0)

        def expert_pair(i, carry0):
            wslot = lax.rem(i, 2) * 2

            @pl.when(i + 1 < N_EXPERTS // 2)
            def _prefetch():
                for p in range(2):
                    for cp in weight_copies(2 * i + 2 + p, 2 - wslot + p):
                        cp.start()

            for p in range(2):
                for cp in weight_copies(2 * i + p, wslot + p):
                    cp.wait()
            counts = [count_ref[t * N_EXPERTS + 2 * i + p] for p in range(2)]
            bases = [chunk_base(2 * i + p, 0) for p in range(2)]
            xs = [load_rows(bases[p], p) for p in range(2)]
            ys = [_swiglu(xs[p], wgu_buf.at[wslot + p], wdn_buf.at[wslot + p]) for p in range(2)]
            for p in range(2):
                yp_ref[p] = _rows_to_slabs(ys[p], il_ref.at[p])
            for p in range(2):
                put_rows(p, bases[p], counts[p])

            for p in range(2):
                def more(c, carry, p=p):
                    base = chunk_base(2 * i + p, c)
                    y = _swiglu(load_rows(base, p), wgu_buf.at[wslot + p], wdn_buf.at[wslot + p])
                    yp_ref[p] = _rows_to_slabs(y, il_ref.at[p])
                    put_rows(p, base, counts[p] - c * EROWS)
                    return carry

                lax.fori_loop(1, (counts[p] + EROWS - 1) // EROWS, more, 0)
            return carry0

        lax.fori_loop(0, N_EXPERTS // 2, expert_pair, 0)

    @pl.when(s >= 1)
    def _combine():
        sb = s - 1

        def group(j, carry):
            tj = sb * (CROWS // LANES) + j
            for sub in range(LANES // SUBLANES):
                r0 = pl.multiple_of(j * LANES + sub * SUBLANES, SUBLANES)
                half = sub % 2
                for u in range(SUBLANES):
                    lane = sub * SUBLANES + u
                    acc = None
                    for k in range(TOP_K):
                        z = pltpu.bitcast(slab_at(xy_ref, dest_ref[tj, k, lane])[...], BF16).astype(F32)
                        acc = wrow_ref[tj, k, lane] * z if acc is None else acc + wrow_ref[tj, k, lane] * z
                    zf_ref[half, u * 2 * slab:(u + 1) * 2 * slab, :] = acc
                comb_ref[pl.ds(r0, SUBLANES), :] = jnp.concatenate(
                    [zf_ref[half, pl.ds(2 * c + h, SUBLANES, stride=2 * slab), :]
                     for h in range(2) for c in range(slab)], axis=1)
            return carry

        lax.fori_loop(0, CROWS // LANES, group, 0)
        shared = _swiglu(hb_ref[...], sgu_ref, sdn_ref)
        o_ref[...] = xm_ref[...] + g2_ref[...] * (comb_ref[...] + shared)


def _moe(h, h_slabs, xm, g2, logits_t, router_bias, w_gu, w_dn, s_gu, s_dn, layer, seq):
    n, d = h.shape
    slab = d // (2 * LANES)
    tile = min(MOE_TILE, seq)
    nt = n // tile
    per_b = seq // tile
    csteps = tile // CROWS
    assert seq % tile == 0 and tile % CROWS == 0
    assert tile % LANES == 0 and CROWS % LANES == 0
    seg_start, seg_count, dest_rows, w_rows = _route(logits_t, router_bias, tile, slab)
    rows = _moe_rows(tile)
    table = pl.BlockSpec((None, tile // LANES, SUBLANES, LANES), lambda t, s, a, b: (t, 0, 0, 0),
                         memory_space=pltpu.SMEM)
    cst = lambda s: jnp.maximum(s - 1, 0)
    crow = pl.BlockSpec((CROWS, d), lambda t, s, a, b: (t * csteps + cst(s), 0))
    hbm = pl.BlockSpec(memory_space=pl.ANY)
    single = pl.Buffered(1)
    return pl.pallas_call(
        _moe_kernel,
        grid_spec=pltpu.PrefetchScalarGridSpec(
            num_scalar_prefetch=2,
            grid=(nt, 1 + csteps),
            in_specs=[pl.BlockSpec((tile * 2 * slab, LANES), lambda t, s, a, b: (t, 0),
                                   pipeline_mode=single),
                      crow, table, table,
                      hbm, hbm,
                      pl.BlockSpec((None,) + s_gu.shape[1:], lambda t, s, a, b: (layer, 0, 0),
                                   pipeline_mode=single),
                      pl.BlockSpec((None,) + s_dn.shape[1:], lambda t, s, a, b: (layer, 0, 0),
                                   pipeline_mode=single),
                      crow,
                      pl.BlockSpec((None, 1, d), lambda t, s, a, b: (t // per_b, 0, 0))],
            out_specs=crow,
            scratch_shapes=[pltpu.VMEM((rows * slab, LANES), U32),
                            pltpu.VMEM((2, EROWS * 2 * slab, LANES), F32),
                            pltpu.VMEM((2, EROWS * slab, LANES), U32),
                            pltpu.VMEM((2, SUBLANES * 2 * slab, LANES), F32),
                            pltpu.VMEM((CROWS, d), F32),
                            pltpu.VMEM((4, d, 2 * EXPERT_DIM), BF16),
                            pltpu.VMEM((4, EXPERT_DIM, d), BF16),
                            pltpu.SemaphoreType.DMA((2, 4))]),
        out_shape=jax.ShapeDtypeStruct((n, d), F32),
        compiler_params=_cparams("arbitrary", "arbitrary"),
    )(seg_start, seg_count, h_slabs, h, dest_rows, w_rows, w_gu, w_dn, s_gu, s_dn, xm, g2)


def _mla_prep_kernel(x_ref, n1_ref, sc1_ref, sh1_ref, nk_ref, sck_ref, shk_ref,
                     wdq_ref, gql_ref, wuq_ref, wuqs_ref, wdown_ref, gkl_ref, wk_ref, gk_ref, wuv_ref,
                     qa_ref, qb_ref, kc_ref, ksa_ref, ksb_ref, q_ref, k_ref, v_ref):
    xn = _rms(x_ref[...])
    hq = xn * n1_ref[...] * (1.0 + sc1_ref[...]) + sh1_ref[...]
    hk = xn * nk_ref[...] * (1.0 + sck_ref[...]) + shk_ref[...]
    n_heads = q_ref.shape[1] // LANES
    half = ROPE_DIM // 2

    cq = (_rms(jnp.dot(hq.astype(BF16), wdq_ref[...], preferred_element_type=F32)) * gql_ref[...]).astype(BF16)
    q_raw = jnp.dot(cq, wuq_ref[...], preferred_element_type=F32)
    q_swp = jnp.dot(cq, wuqs_ref[...], preferred_element_type=F32)
    qa, qb = qa_ref[...], qb_ref[...]
    for h in range(n_heads):
        sl = slice(h * LANES, (h + 1) * LANES)
        t = q_raw[:, sl]
        r = lax.rsqrt(jnp.sum(t * t, axis=-1, keepdims=True) / QK_DIM + NORM_EPS)
        q_ref[:, sl] = (r * (t * qa + q_swp[:, sl] * qb)).astype(q_ref.dtype)

    down = jnp.dot(hk.astype(BF16), wdown_ref[...], preferred_element_type=F32)
    ckv = (_rms(down[:, :KV_RANK]) * gkl_ref[...]).astype(BF16)
    pe = down[:, KV_RANK:]
    ss_pe = jnp.sum(pe * pe, axis=-1, keepdims=True)
    rot = (pe * kc_ref[...] + pltpu.roll(pe, LANES - half, 1) * ksa_ref[...]
           + pltpu.roll(pe, half, 1) * ksb_ref[...])
    rot_hi = rot.astype(BF16)
    rot_lo = (rot - rot_hi.astype(F32)).astype(BF16)
    k_raw = jnp.dot(jnp.concatenate([ckv, rot_hi, rot_lo], axis=1), wk_ref[...],
                    preferred_element_type=F32)
    nope = lax.broadcasted_iota(jnp.int32, (x_ref.shape[0], LANES), 1) < NOPE_DIM
    for h in range(n_heads):
        sl = slice(h * LANES, (h + 1) * LANES)
        t = k_raw[:, sl]
        ss = jnp.sum(jnp.where(nope, t * t, 0.0), axis=-1, keepdims=True) + ss_pe
        k_ref[:, sl] = (t * lax.rsqrt(ss / QK_DIM + NORM_EPS) * gk_ref[...]).astype(k_ref.dtype)
    v_ref[...] = jnp.dot(ckv, wuv_ref[...], preferred_element_type=F32).astype(v_ref.dtype)


def _pad_heads(w, n_heads, width):
    r = w.shape[0]
    return jnp.pad(w.reshape(r, n_heads, width), ((0, 0), (0, 0), (0, LANES - width))).reshape(r, n_heads * LANES)


def _swap_rope_halves(a):
    half = ROPE_DIM // 2
    lead = a.shape[:-1]
    return jnp.concatenate([jnp.zeros(lead + (NOPE_DIM,), a.dtype), a[..., NOPE_DIM + half:QK_DIM],
                            a[..., NOPE_DIM:NOPE_DIM + half],
                            jnp.zeros(lead + (LANES - QK_DIM,), a.dtype)], axis=-1)


def _mla_prep(xf, pos, n1, sc1, sh1, nk, sck, shk, w_dq, g_ql, w_uq, g_q, w_down, g_kl, w_uk, w_uv,
              g_k, seq):
    n, d = xf.shape
    n_heads = w_uv.shape[1] // V_DIM
    q_rank = w_dq.shape[1]
    hw = n_heads * LANES
    tm = min(ROW_TILE, seq)
    per_b = seq // tm

    half = ROPE_DIM // 2
    inv = jnp.power(ROPE_THETA, -jnp.arange(half, dtype=F32) / half)
    ang = pos.astype(F32).reshape(n, 1) * inv
    cs = jnp.concatenate([jnp.cos(ang), jnp.sin(ang)], axis=1)
    tail = LANES - QK_DIM
    q_mul = QK_DIM ** -0.5 * LOG2E
    j = np.arange(half)
    sel = np.zeros((ROPE_DIM, 5, LANES), np.float32)
    sel[j, 0, NOPE_DIM + j] = sel[j, 0, NOPE_DIM + half + j] = 1.0
    sel[half + j, 1, NOPE_DIM + j], sel[half + j, 1, NOPE_DIM + half + j] = -1.0, 1.0
    sel[j, 2, j] = sel[j, 2, half + j] = 1.0
    sel[half + j, 3, j] = -1.0
    sel[half + j, 4, half + j] = 1.0
    gq_pad = jnp.pad(g_q, (0, tail)) * q_mul
    gq_swp = _swap_rope_halves(g_q.reshape(1, QK_DIM))[0] * q_mul
    gk1, gk2 = g_k[NOPE_DIM:NOPE_DIM + half], g_k[NOPE_DIM + half:]
    kpad = jnp.zeros((LANES - ROPE_DIM,), F32)
    k_own, k_swp = jnp.concatenate([gk1, gk2, kpad]), jnp.concatenate([gk2, gk1, kpad])
    place = (jnp.asarray(sel) * jnp.stack([gq_pad, gq_swp, k_own, k_swp, k_swp])).reshape(ROPE_DIM, -1)
    const = jnp.pad(gq_pad[:NOPE_DIM], (0, 5 * LANES - NOPE_DIM))
    tables = jnp.dot(cs, place, precision=lax.Precision.HIGHEST) + const
    gk_mask = jnp.concatenate([g_k[:NOPE_DIM], jnp.ones((ROPE_DIM,), F32),
                               jnp.zeros((tail,), F32)]).reshape(1, LANES)

    wuq_p = _pad_heads(w_uq, n_heads, QK_DIM).astype(BF16)
    wuq_s = _swap_rope_halves(w_uq.reshape(q_rank, n_heads, QK_DIM)).reshape(q_rank, hw).astype(BF16)
    wdown_p = jnp.pad(w_down, ((0, 0), (0, LANES - ROPE_DIM))).astype(BF16)
    eye = jnp.pad(jnp.eye(ROPE_DIM, dtype=F32), ((0, LANES - ROPE_DIM), (NOPE_DIM, tail)))
    pe_rows = jnp.tile(eye, (1, n_heads))
    wk = jnp.concatenate([_pad_heads(w_uk, n_heads, NOPE_DIM), pe_rows, pe_rows], axis=0).astype(BF16)

    vec = pl.BlockSpec((None, 1, d), lambda i: (i // per_b, 0, 0))
    full = lambda a: pl.BlockSpec(a.shape, lambda i: (0,) * a.ndim)
    row = lambda wdt: pl.BlockSpec((tm, wdt), lambda i: (i, 0))
    args = [xf, n1, sc1, sh1, nk, sck, shk,
            w_dq.astype(BF16), g_ql.reshape(1, q_rank), wuq_p, wuq_s,
            wdown_p, g_kl.reshape(1, KV_RANK), wk, gk_mask, w_uv.astype(BF16),
            tables, tables, tables, tables, tables]
    table_specs = [pl.BlockSpec((tm, LANES), lambda i, c=c: (i, c)) for c in range(5)]
    specs = [row(d), full(n1), vec, vec, full(nk), vec, vec] + [full(a) for a in args[7:16]] + table_specs
    return pl.pallas_call(
        _mla_prep_kernel,
        grid=(n // tm,),
        in_specs=specs,
        out_specs=[row(hw), row(hw), row(n_heads * V_DIM)],
        out_shape=[jax.ShapeDtypeStruct((n, hw), BF16), jax.ShapeDtypeStruct((n, hw), BF16),
                   jax.ShapeDtypeStruct((n, n_heads * V_DIM), BF16)],
        compiler_params=_cparams("parallel"),
    )(*args)


def _causal_bias(seq):
    a = np.arange(QBLK)[:, None]
    c = np.arange(seq)[None, :]
    return np.where(c <= (seq - QBLK) + a, 0.0, MASKED).astype(np.float32)


def _mla_attn_kernel(q_ref, k_ref, v_ref, mask_ref, wa_ref, wb_ref, o_ref, wa_out, wb_out,
                     s_ref, p_ref, ve_ref):
    wa_out[...] = wa_ref[...].astype(BF16)
    wb_out[...] = wb_ref[...].astype(BF16)
    seq = q_ref.shape[0]
    lane_lo = lax.broadcasted_iota(jnp.int32, (QBLK, LANES), 1) < V_DIM

    def qk_of(blk, h, w):
        lanes = slice(h * LANES, (h + 1) * LANES)
        return q_ref[w - QBLK:w, lanes], k_ref[0:w, lanes]

    _attention_blocks(seq, qk_of, v_ref, mask_ref, s_ref, p_ref, ve_ref, o_ref, lane_lo,
                      diagonal_bias=True)


def _mla_attention(q, k, v, cast_weights, layer):
    bsz, seq, hw = q.shape
    n_pairs = hw // (2 * LANES)
    qk = pl.BlockSpec((None, seq, 2 * LANES), lambda b, h: (b, 0, h))
    vo = pl.BlockSpec((None, seq, LANES), lambda b, h: (b, 0, h))
    grid = (bsz, n_pairs)
    w_in, w_out, w_shapes = _cast_rider(cast_weights, layer, grid)
    return pl.pallas_call(
        _mla_attn_kernel,
        grid=grid,
        in_specs=[qk, qk, vo, pl.BlockSpec((QBLK, seq), lambda b, h: (0, 0))] + w_in,
        out_specs=[vo] + w_out,
        scratch_shapes=_attn_scratch(seq),
        out_shape=[jax.ShapeDtypeStruct(v.shape, BF16)] + w_shapes,
        compiler_params=_cparams("parallel", "parallel"),
    )(q, k, v, jnp.asarray(_causal_bias(seq)), *cast_weights)


def kernel(x, c, pos, ada_w, ada_b, norm1_g, norm2_g, a_w_qkv, a_q_norm, a_k_norm, a_w_o, kv_ada_w, kv_ada_b, kv_norm_g, kv_w_down, kv_latent_norm, kv_w_uk, kv_w_uv, kv_k_norm, b_w_dq, b_q_latent_norm, b_w_uq, b_q_norm, b_w_o, moe_w_router, moe_router_bias, moe_w_gate_up, moe_w_down, moe_shared_gate_up, moe_shared_down):
    bsz, seq, d = x.shape
    n = bsz * seq
    depth = ada_w.shape[0]
    n_a = a_w_qkv.shape[0]
    assert seq % QBLK == 0
    xf = x.reshape(n, d)

    expert_weights = (moe_w_gate_up, moe_w_down)
    s_gu = moe_shared_gate_up.astype(BF16)
    s_dn = moe_shared_down.astype(BF16)
    row = lambda g: g.reshape(1, d)

    for layer in range(depth):
        mod = _modulation(c, ada_w, ada_b, layer).reshape(bsz, 6, 1, d)
        sh1, sc1, g1, sh2, sc2, g2 = [mod[:, i] for i in range(6)]
        if layer < n_a:
            qkv = _qkv_proj(xf, row(norm1_g[layer]), sc1, sh1, a_w_qkv[layer].astype(BF16), seq)
            o, w_gu, w_dn = _dilated_attention(qkv.reshape(bsz, seq, -1), pos, a_q_norm[layer],
                                               a_k_norm[layer], expert_weights, layer)
            w_o = a_w_o[layer]
        else:
            j = layer - n_a
            kmod = _modulation(c, kv_ada_w[None], kv_ada_b[None], 0).reshape(bsz, 2, 1, d)
            q, k, v = _mla_prep(xf, pos, row(norm1_g[layer]), sc1, sh1, row(kv_norm_g), kmod[:, 1],
                                kmod[:, 0], b_w_dq[j], b_q_latent_norm[j], b_w_uq[j], b_q_norm[j],
                                kv_w_down, kv_latent_norm, kv_w_uk, kv_w_uv, kv_k_norm, seq)
            o, w_gu, w_dn = _mla_attention(q.reshape(bsz, seq, -1), k.reshape(bsz, seq, -1),
                                           v.reshape(bsz, seq, -1), expert_weights, layer)
            w_o = b_w_o[j]
        xm, h2, h2_slabs, logits = _post_attn(o.reshape(n, d), w_o.astype(BF16), xf, g1,
                                              row(norm2_g[layer]), sc2, sh2, moe_w_router[layer], seq)
        xf = _moe(h2, h2_slabs, xm, g2, logits, moe_router_bias[layer], w_gu, w_dn, s_gu, s_dn,
                  layer, seq)
    return xf.reshape(bsz, seq, d)
```

```python
import functools
import math

import numpy as np
import jax
import jax.numpy as jnp
from jax import lax
from jax.experimental import pallas as pl
from jax.experimental.pallas import tpu as pltpu

F32 = jnp.float32
BF16 = jnp.bfloat16
U32 = jnp.uint32

NORM_EPS = 1e-6
MASKED = -1e30
LOG2E = math.log2(math.e)

LANES = 128
SUBLANES = 8
VMEM_LIMIT = 56 * 1024 * 1024

A_HEAD_DIM = 64
DILATED_PAIRS = ((128, 1), (512, 4), (2048, 16))
QBLK = 256
NOPE_DIM = 64
ROPE_DIM = 32
QK_DIM = NOPE_DIM + ROPE_DIM
V_DIM = 64
KV_RANK = 256
ROPE_THETA = 10000.0
N_EXPERTS = 64
N_GROUPS = 8
TOPK_GROUPS = 4
TOP_K = 6
EXPERT_DIM = 256
ROUTED_SCALE = 2.5
MOE_TILE = 2048
EROWS = 256
CROWS = 256
WEIGHT_PAIRS_AHEAD = 2

ROW_TILE = 512


def _cparams(*sem):
    return pltpu.CompilerParams(dimension_semantics=sem, vmem_limit_bytes=VMEM_LIMIT)


def _silu(v):
    return v * jax.nn.sigmoid(v)


def _rms(v):
    return v * lax.rsqrt(jnp.mean(v * v, axis=-1, keepdims=True) + NORM_EPS)


def _dot_nt(a, b):
    return lax.dot_general(a, b, (((1,), (1,)), ((), ())), preferred_element_type=F32)


def _mod_kernel(c_ref, w_ref, b_ref, o_ref):
    cond = _silu(c_ref[...])
    o_ref[...] = jnp.dot(cond, w_ref[...], preferred_element_type=F32,
                         precision=lax.Precision.HIGHEST) + b_ref[...]


def _modulation(c, w, b, layer):
    bsz, d = c.shape
    m = w.shape[-1]
    tn = 1024
    return pl.pallas_call(
        _mod_kernel,
        grid=(m // tn,),
        in_specs=[pl.BlockSpec((bsz, d), lambda j: (0, 0)),
                  pl.BlockSpec((None, d, tn), lambda j: (layer, 0, j)),
                  pl.BlockSpec((None, 1, tn), lambda j: (layer, 0, j))],
        out_specs=pl.BlockSpec((bsz, tn), lambda j: (0, j)),
        out_shape=jax.ShapeDtypeStruct((bsz, m), F32),
        compiler_params=_cparams("arbitrary"),
    )(c, w, b.reshape(b.shape[0], 1, m))


def _qkv_kernel(x_ref, g_ref, sc_ref, sh_ref, w_ref, o_ref):
    h = _rms(x_ref[...]) * g_ref[...]
    h = h * (1.0 + sc_ref[...]) + sh_ref[...]
    o_ref[...] = jnp.dot(h.astype(BF16), w_ref[...], preferred_element_type=F32).astype(o_ref.dtype)


def _qkv_proj(xf, g, sc, sh, w, seq):
    n, d = xf.shape
    m = w.shape[1]
    tm = min(ROW_TILE, seq)
    per_b = seq // tm
    vec = pl.BlockSpec((None, 1, d), lambda i: (i // per_b, 0, 0))
    return pl.pallas_call(
        _qkv_kernel,
        grid=(n // tm,),
        in_specs=[pl.BlockSpec((tm, d), lambda i: (i, 0)),
                  pl.BlockSpec((1, d), lambda i: (0, 0)),
                  vec, vec,
                  pl.BlockSpec((d, m), lambda i: (0, 0))],
        out_specs=pl.BlockSpec((tm, m), lambda i: (i, 0)),
        out_shape=jax.ShapeDtypeStruct((n, m), BF16),
        compiler_params=_cparams("parallel"),
    )(xf, g, sc, sh, w)


ATT_SLOTS = 4


def _attention_blocks(seq, qk_of, v_ref, bias_ref, s_ref, p_ref, ve_ref, o_ref, lane_lo,
                      diagonal_bias=False):
    nblk = seq // QBLK

    for r0 in range(0, seq, QBLK):
        v = v_ref[r0:r0 + QBLK, :].astype(F32)
        ve_ref[0, r0:r0 + QBLK, :] = jnp.where(lane_lo, v, 1.0).astype(BF16)
        ve_ref[1, r0:r0 + QBLK, :] = jnp.where(lane_lo, 1.0, v).astype(BF16)

    def slot(blk, h):
        return (blk % 2) * 2 + h

    def scores(blk):
        w = (blk + 1) * QBLK
        lo = w - QBLK if diagonal_bias else 0
        bias = bias_ref[:, seq - w + lo:seq]
        for h in range(2):
            q, k = qk_of(blk, h, w)
            s = _dot_nt(q, k)
            if lo:
                s_ref[slot(blk, h), :, 0:lo] = s[:, :lo]
            s_ref[slot(blk, h), :, lo:w] = s[:, lo:] + bias

    for blk in range(nblk):
        w = (blk + 1) * QBLK
        scores(blk)
        for h in range(2):
            s = s_ref[slot(blk, h), :, 0:w]
            p_ref[slot(blk, h), :, 0:w] = jnp.exp2(s - jnp.max(s, axis=-1, keepdims=True)).astype(BF16)
        acc_a, acc_b = [jnp.dot(p_ref[slot(blk, h), :, 0:w], ve_ref[h, 0:w, :],
                                preferred_element_type=F32) for h in range(2)]
        sums = pltpu.roll(jnp.where(lane_lo, acc_b, acc_a), LANES // 2, 1)
        o_ref[blk * QBLK:w, :] = (jnp.where(lane_lo, acc_a, acc_b) / sums).astype(o_ref.dtype)


def _attn_scratch(seq):
    return [pltpu.VMEM((ATT_SLOTS, QBLK, seq), F32), pltpu.VMEM((ATT_SLOTS, QBLK, seq), BF16),
            pltpu.VMEM((2, seq, LANES), BF16)]


def _cast_rider(weights, layer, grid):
    steps = grid[0] * grid[1]
    in_specs, out_specs, out_shapes = [], [], []
    for w in weights:
        n_exp = w.shape[1]
        assert n_exp % steps == 0
        per = n_exp // steps
        blk = (per,) + w.shape[2:]
        in_specs.append(pl.BlockSpec((None,) + blk, lambda b, h: (layer, b * grid[1] + h, 0, 0)))
        out_specs.append(pl.BlockSpec(blk, lambda b, h: (b * grid[1] + h, 0, 0)))
        out_shapes.append(jax.ShapeDtypeStruct(w.shape[1:], BF16))
    return in_specs, out_specs, out_shapes


def _split3(v):
    p1 = v.astype(BF16).astype(F32)
    p2 = (v - p1).astype(BF16).astype(F32)
    return p1, p2, v - p1 - p2


def _dilated_log_multiplicity(seq):
    a = np.arange(QBLK)[:, None]
    c = np.arange(seq)[None, :]
    d = (seq - QBLK) + a - c
    mult = np.zeros((QBLK, seq), np.int32)
    for win, dil in DILATED_PAIRS:
        mult += ((d >= 0) & (d % dil == 0) & (d <= win)).astype(np.int32)
    return np.where(mult > 0, np.log2(np.maximum(mult, 1)), MASKED).astype(np.float32)


N_FEAT = 9


def _dil_attn_kernel(q_ref, k_ref, v_ref, posc_ref, sfeat_ref, gq_ref, gk_ref, logm_ref,
                     wa_ref, wb_ref, o_ref, wa_out, wb_out, qa_ref, qb_ref, ks_ref, s_ref, p_ref,
                     ve_ref):
    wa_out[...] = wa_ref[...].astype(BF16)
    wb_out[...] = wb_ref[...].astype(BF16)
    seq = q_ref.shape[0]
    lane = lax.broadcasted_iota(jnp.int32, (QBLK, LANES), 1)
    lane_lo = lane < A_HEAD_DIM
    piece = [functools.reduce(jnp.logical_or, [lane == i + 3 * j for j in range(N_FEAT // 3)])
             for i in range(3)]

    def head_norm(ref, r0, g_ref, mul):
        t = ref[r0:r0 + QBLK, :].astype(F32)
        sq = t * t
        ssa = jnp.sum(jnp.where(lane_lo, sq, 0.0), axis=-1, keepdims=True)
        ssb = jnp.sum(jnp.where(lane_lo, 0.0, sq), axis=-1, keepdims=True)
        r = jnp.where(lane_lo, lax.rsqrt(ssa / A_HEAD_DIM + NORM_EPS),
                      lax.rsqrt(ssb / A_HEAD_DIM + NORM_EPS))
        return t * r * (g_ref[...] * mul)

    rel0 = posc_ref[0:1, :]
    for r0 in range(0, seq, QBLK):
        rows = slice(r0, r0 + QBLK)
        qn = head_norm(q_ref, r0, gq_ref, A_HEAD_DIM ** -0.5 * LOG2E)
        qa_ref[rows, :LANES] = jnp.where(lane_lo, qn, 0.0).astype(BF16)
        qb_ref[rows, :LANES] = jnp.where(lane_lo, 0.0, qn).astype(BF16)
        qa_ref[rows, LANES:] = jnp.broadcast_to(sfeat_ref[0:1, :], (QBLK, LANES)).astype(BF16)
        qb_ref[rows, LANES:] = jnp.broadcast_to(sfeat_ref[1:2, :], (QBLK, LANES)).astype(BF16)
        ks_ref[rows, :LANES] = head_norm(k_ref, r0, gk_ref, 1.0).astype(BF16)
        p1, p2, p3 = _split3((posc_ref[rows, :] - rel0).astype(F32))
        feat = jnp.where(piece[0], p1, jnp.where(piece[1], p2, jnp.where(piece[2], p3, 0.0)))
        ks_ref[rows, LANES:] = feat.astype(BF16)

    def qk_of(blk, h, w):
        return (qa_ref, qb_ref)[h][w - QBLK:w, :], ks_ref[0:w, :]

    _attention_blocks(seq, qk_of, v_ref, logm_ref, s_ref, p_ref, ve_ref, o_ref, lane_lo)


def _dilated_attention(qkv, pos, q_norm, k_norm, cast_weights, layer):
    bsz, seq, three_d = qkv.shape
    d = three_d // 3
    n_pairs = d // LANES
    n_heads = d // A_HEAD_DIM
    slopes = jnp.exp2(-8.0 * jnp.arange(1, n_heads + 1, dtype=F32) / n_heads) * LOG2E
    sfeat = jnp.repeat(jnp.stack(_split3(slopes), axis=1), 3, axis=1)
    sfeat = jnp.pad(sfeat, ((0, 0), (0, LANES - N_FEAT))).reshape(n_pairs, 2, LANES)
    logm = jnp.asarray(_dilated_log_multiplicity(seq))
    gq = jnp.tile(q_norm.reshape(1, A_HEAD_DIM), (1, 2))
    gk = jnp.tile(k_norm.reshape(1, A_HEAD_DIM), (1, 2))
    blk = lambda off: pl.BlockSpec((None, seq, LANES), lambda b, h: (b, 0, off + h))
    const2 = pl.BlockSpec((1, LANES), lambda b, h: (0, 0))
    ext = pltpu.VMEM((seq, 2 * LANES), BF16)
    grid = (bsz, n_pairs)
    w_in, w_out, w_shapes = _cast_rider(cast_weights, layer, grid)
    return pl.pallas_call(
        _dil_attn_kernel,
        grid=grid,
        in_specs=[blk(0), blk(n_pairs), blk(2 * n_pairs),
                  pl.BlockSpec((None, seq, 1), lambda b, h: (b, 0, 0)),
                  pl.BlockSpec((None, 2, LANES), lambda b, h: (h, 0, 0)),
                  const2, const2,
                  pl.BlockSpec((QBLK, seq), lambda b, h: (0, 0))] + w_in,
        out_specs=[pl.BlockSpec((None, seq, LANES), lambda b, h: (b, 0, h))] + w_out,
        scratch_shapes=[ext, ext, ext] + _attn_scratch(seq),
        out_shape=[jax.ShapeDtypeStruct((bsz, seq, d), BF16)] + w_shapes,
        compiler_params=_cparams("parallel", "parallel"),
    )(qkv, qkv, qkv, pos.reshape(bsz, seq, 1), sfeat, gq, gk, logm, *cast_weights)


def _post_attn_kernel(o_ref, wo_ref, x_ref, g1_ref, n2_ref, sc_ref, sh_ref, wrh_ref, wrl_ref,
                      xm_ref, h_ref, hs_ref, lg_ref, il_ref):
    y = jnp.dot(o_ref[...], wo_ref[...], preferred_element_type=F32)
    xm = x_ref[...] + g1_ref[...] * y
    xm_ref[...] = xm
    h = _rms(xm) * n2_ref[...]
    h = h * (1.0 + sc_ref[...]) + sh_ref[...]
    h_ref[...] = h.astype(h_ref.dtype)
    hs_ref[...] = _rows_to_slab_rows(h, il_ref)
    h_hi = h.astype(BF16)
    h_lo = (h - h_hi.astype(F32)).astype(BF16)
    lg_ref[...] = (_dot_nt(wrh_ref[...], h_hi) + _dot_nt(wrh_ref[...], h_lo)
                   + _dot_nt(wrl_ref[...], h_hi))


def _post_attn(o, w_o, xf, g1, n2, sc2, sh2, w_router, seq):
    n, d = xf.shape
    ne = w_router.shape[1]
    tm = min(ROW_TILE, seq)
    per_b = seq // tm
    slab_rows = d // LANES
    wr_t = w_router.T.astype(F32)
    wr_hi = wr_t.astype(BF16)
    vec = pl.BlockSpec((None, 1, d), lambda i: (i // per_b, 0, 0))
    row = lambda w: pl.BlockSpec((tm, w), lambda i: (i, 0))
    return pl.pallas_call(
        _post_attn_kernel,
        grid=(n // tm,),
        in_specs=[row(d), pl.BlockSpec((d, d), lambda i: (0, 0)), row(d), vec,
                  pl.BlockSpec((1, d), lambda i: (0, 0)), vec, vec,
                  pl.BlockSpec((ne, d), lambda i: (0, 0)), pl.BlockSpec((ne, d), lambda i: (0, 0))],
        out_specs=[row(d), row(d), pl.BlockSpec((tm * slab_rows, LANES), lambda i: (i, 0)),
                   pl.BlockSpec((ne, tm), lambda i: (0, i))],
        out_shape=[jax.ShapeDtypeStruct((n, d), F32), jax.ShapeDtypeStruct((n, d), BF16),
                   jax.ShapeDtypeStruct((n * slab_rows, LANES), BF16),
                   jax.ShapeDtypeStruct((ne, n), F32)],
        scratch_shapes=[pltpu.VMEM((tm * slab_rows, LANES), F32)],
        compiler_params=_cparams("parallel"),
    )(o, w_o, xf, g1, n2, sc2, sh2, wr_hi, (wr_t - wr_hi.astype(F32)).astype(BF16))


def _moe_rows(tile):
    rows = tile * TOP_K + N_EXPERTS * (SUBLANES - 1) + EROWS
    return -(-rows // SUBLANES) * SUBLANES


def _first_index(hit, index, limit):
    return jnp.min(jnp.where(hit, index, limit), axis=0, keepdims=True)


def _route_kernel(lg_ref, bias_ref, tri_ref, dest_ref, w_ref, seg_ref, cnt_ref, *, row_pitch):
    t = lg_ref.shape[1]
    per_g = N_EXPERTS // N_GROUPS
    sub = lax.broadcasted_iota(jnp.int32, (per_g, t), 0).astype(F32)
    neg = -jnp.inf
    scores, sel = [], []
    for g in range(N_GROUPS):
        rows = slice(g * per_g, (g + 1) * per_g)
        sc = jax.nn.sigmoid(lg_ref[rows, :])
        scores.append(sc)
        sel.append(sc + bias_ref[rows, :])

    gs = []
    for g in range(N_GROUPS):
        m1 = jnp.max(sel[g], axis=0, keepdims=True)
        first = _first_index(sel[g] == m1, sub, float(per_g))
        m2 = jnp.max(jnp.where(sub == first, neg, sel[g]), axis=0, keepdims=True)
        gs.append(m1 + m2)
    gsc = jnp.concatenate(gs, axis=0)
    gsub = lax.broadcasted_iota(jnp.int32, (N_GROUPS, t), 0).astype(F32)
    gkeep = jnp.zeros((N_GROUPS, t), F32)
    for _ in range(TOPK_GROUPS):
        m = jnp.max(gsc, axis=0, keepdims=True)
        pick = gsub == _first_index(gsc == m, gsub, float(N_GROUPS))
        gkeep = jnp.where(pick, 1.0, gkeep)
        gsc = jnp.where(pick, neg, gsc)

    eidx = [sub + float(g * per_g) for g in range(N_GROUPS)]
    masked = [jnp.where(jnp.broadcast_to(gkeep[g:g + 1, :], (per_g, t)) > 0.5, sel[g], neg)
              for g in range(N_GROUPS)]
    chosen = [jnp.zeros((per_g, t), F32) for _ in range(N_GROUPS)]
    wsel = [jnp.zeros((per_g, t), F32) for _ in range(N_GROUPS)]
    picked = []
    for _ in range(TOP_K):
        m = jnp.max(functools.reduce(jnp.maximum, masked), axis=0, keepdims=True)
        cand = functools.reduce(jnp.minimum, [jnp.where(masked[g] == m, eidx[g], float(N_EXPERTS))
                                              for g in range(N_GROUPS)])
        e_first = jnp.min(cand, axis=0, keepdims=True)
        picked.append(e_first)
        for g in range(N_GROUPS):
            pick = eidx[g] == e_first
            chosen[g] = jnp.where(pick, 1.0, chosen[g])
            wsel[g] = jnp.where(pick, scores[g], wsel[g])
            masked[g] = jnp.where(pick, neg, masked[g])

    denom = jnp.sum(functools.reduce(jnp.add, wsel), axis=0, keepdims=True)

    c_all = jnp.concatenate(chosen, axis=0)
    nblk = t // LANES
    stacked = jnp.concatenate([c_all[:, b * LANES:(b + 1) * LANES] for b in range(nblk)], axis=0)
    within = jnp.dot(stacked.astype(BF16), tri_ref[...], preferred_element_type=F32)
    ranks, before = [], jnp.zeros((N_EXPERTS, 1), F32)
    for b in range(nblk):
        rows = slice(b * N_EXPERTS, (b + 1) * N_EXPERTS)
        ranks.append(within[rows, :] + before)
        before = before + within[rows, LANES - 1:LANES] + stacked[rows, LANES - 1:LANES]
    rank = jnp.concatenate(ranks, axis=1)
    counts = before
    padded = jnp.ceil(counts / SUBLANES) * SUBLANES
    below = (lax.broadcasted_iota(jnp.int32, (N_EXPERTS, N_EXPERTS), 1)
             < lax.broadcasted_iota(jnp.int32, (N_EXPERTS, N_EXPERTS), 0)).astype(F32)
    seg = jnp.dot(below, jnp.broadcast_to(padded, (N_EXPERTS, LANES)), preferred_element_type=F32,
                  precision=lax.Precision.HIGHEST)
    row_of = seg[:, :1] + rank

    dest_rows, w_rows = [], []
    for k in range(TOP_K):
        d_k = jnp.zeros((per_g, t), F32)
        w_k = jnp.zeros((per_g, t), F32)
        for g in range(N_GROUPS):
            hit = eidx[g] == picked[k]
            d_k = jnp.where(hit, row_of[g * per_g:(g + 1) * per_g, :], d_k)
            w_k = jnp.where(hit, wsel[g], w_k)
        dest_rows.append(jnp.sum(d_k, axis=0, keepdims=True))
        w_rows.append(jnp.sum(w_k, axis=0, keepdims=True) / denom * ROUTED_SCALE)
    fill = [jnp.zeros((SUBLANES - TOP_K, t), F32)]
    dest8 = (jnp.concatenate(dest_rows + fill, axis=0) * row_pitch).astype(jnp.int32)
    w8 = jnp.concatenate(w_rows + fill, axis=0)
    for j in range(t // LANES):
        dest_ref[j] = dest8[:, j * LANES:(j + 1) * LANES]
        w_ref[j] = w8[:, j * LANES:(j + 1) * LANES]
    seg_ref[...] = seg.astype(jnp.int32)
    cnt_ref[...] = jnp.broadcast_to(counts, (N_EXPERTS, LANES)).astype(jnp.int32)


def _route(logits_t, router_bias, tile, row_pitch):
    ne, n = logits_t.shape
    nt = n // tile
    tri = jnp.asarray(np.triu(np.ones((LANES, LANES), np.float32), 1), dtype=BF16)
    per_tile = lambda rows, width: pl.BlockSpec((None, rows, width), lambda i: (i, 0, 0))
    table = pl.BlockSpec((None, tile // LANES, SUBLANES, LANES), lambda i: (i, 0, 0, 0))
    dest, wts, seg, cnt = pl.pallas_call(
        functools.partial(_route_kernel, row_pitch=row_pitch),
        grid=(nt,),
        in_specs=[pl.BlockSpec((ne, tile), lambda i: (0, i)),
                  pl.BlockSpec((ne, 1), lambda i: (0, 0)),
                  pl.BlockSpec((LANES, LANES), lambda i: (0, 0))],
        out_specs=[table, table, per_tile(ne, LANES), per_tile(ne, LANES)],
        out_shape=[jax.ShapeDtypeStruct((nt, tile // LANES, SUBLANES, LANES), jnp.int32),
                   jax.ShapeDtypeStruct((nt, tile // LANES, SUBLANES, LANES), F32),
                   jax.ShapeDtypeStruct((nt, ne, LANES), jnp.int32),
                   jax.ShapeDtypeStruct((nt, ne, LANES), jnp.int32)],
        compiler_params=_cparams("parallel"),
    )(logits_t, router_bias.astype(F32).reshape(ne, 1), tri)
    return seg[:, :, 0].reshape(-1), cnt[:, :, 0].reshape(-1), dest, wts


def _swiglu(xb, wgu_ref, wdn_ref):
    gu = jnp.dot(xb, wgu_ref[...], preferred_element_type=F32)
    half = wgu_ref.shape[1] // 2
    act = _silu(gu[:, :half]) * gu[:, half:]
    return jnp.dot(act.astype(BF16), wdn_ref[...], preferred_element_type=F32)


def _slabs_to_rows(p, il_ref, r):
    n = il_ref.shape[0] // r
    il_ref[...] = pltpu.bitcast(p, BF16).astype(F32)
    return jnp.concatenate([il_ref[pl.ds(2 * c + h, r, stride=n), :]
                            for h in range(2) for c in range(n // 2)], axis=1)


def _rows_to_slab_rows(v, il_ref):
    r = v.shape[0]
    n = il_ref.shape[0] // r
    for h in range(2):
        for c in range(n // 2):
            lane0 = (h * (n // 2) + c) * LANES
            il_ref[pl.ds(2 * c + h, r, stride=n), :] = v[:, lane0:lane0 + LANES]
    return il_ref[...].astype(BF16)


def _rows_to_slabs(v, il_ref):
    return pltpu.bitcast(_rows_to_slab_rows(v, il_ref), U32)


def _moe_kernel(start_ref, count_ref, hp_ref, hb_ref, dest_ref, wrow_ref, wgu_hbm, wdn_hbm,
                sgu_ref, sdn_ref, xm_ref, g2_ref, o_ref, xy_ref, il_ref, yp_ref, zf_ref,
                comb_ref, wgu_buf, wdn_buf, wsem):
    t = pl.program_id(0)
    s = pl.program_id(1)
    slab = xm_ref.shape[1] // (2 * LANES)
    tile = hp_ref.shape[0] // (2 * slab)
    group_rows = SUBLANES * slab

    def weight_copies(e, slot):
        return (pltpu.make_async_copy(wgu_hbm.at[e], wgu_buf.at[slot], wsem.at[0, slot]),
                pltpu.make_async_copy(wdn_hbm.at[e], wdn_buf.at[slot], wsem.at[1, slot]))

    def slab_at(ref, first):
        return ref.at[pl.ds(pl.multiple_of(first, slab), slab), :]

    @pl.when(s == 0)
    def _first_weights():
        for e in range(2 * WEIGHT_PAIRS_AHEAD):
            for cp in weight_copies(e, e):
                cp.start()

    @pl.when(jnp.logical_and(t == 0, s == 0))
    def _clear():
        xy_ref[...] = jnp.zeros(xy_ref.shape, U32)

    @pl.when(s == 0)
    def _dispatch():
        def group(j, carry):
            for sub in range(LANES // SUBLANES):
                p0 = pl.multiple_of((j * (LANES // SUBLANES) + sub) * 2 * group_rows, 2 * group_rows)
                rows = pltpu.bitcast(hp_ref[pl.ds(p0, 2 * group_rows), :], U32)
                for u in range(SUBLANES):
                    for k in range(TOP_K):
                        slab_at(xy_ref, dest_ref[j, k, sub * SUBLANES + u])[...] = (
                            rows[u * slab:(u + 1) * slab, :])
            return carry

        lax.fori_loop(0, tile // LANES, group, 0)

    @pl.when(s == 0)
    def _experts():
        def chunk_base(e, c):
            return pl.multiple_of((start_ref[t * N_EXPERTS + e] + c * EROWS) * slab, group_rows)

        def load_rows(base, p):
            return _slabs_to_rows(xy_ref[pl.ds(base, EROWS * slab), :], il_ref.at[p], EROWS).astype(BF16)

        def put_rows(p, base, valid):
            def put(j, c2):
                r0 = pl.multiple_of(j * group_rows, group_rows)
                xy_ref[pl.ds(base + r0, group_rows), :] = yp_ref[p, pl.ds(r0, group_rows), :]
                return c2

            lax.fori_loop(0, (jnp.clip(valid, 0, EROWS) + SUBLANES - 1) // SUBLANES, put, 0)

        def expert_pair(i, carry0):
            ring = WEIGHT_PAIRS_AHEAD + 1
            wslot = lax.rem(i, ring) * 2

            @pl.when(i + WEIGHT_PAIRS_AHEAD < N_EXPERTS // 2)
            def _prefetch():
                ahead = i + WEIGHT_PAIRS_AHEAD
                for p in range(2):
                    for cp in weight_copies(2 * ahead + p, lax.rem(ahead, ring) * 2 + p):
                        cp.start()

            for p in range(2):
                for cp in weight_copies(2 * i + p, wslot + p):
                    cp.wait()
            counts = [count_ref[t * N_EXPERTS + 2 * i + p] for p in range(2)]
            bases = [chunk_base(2 * i + p, 0) for p in range(2)]
            xs = [load_rows(bases[p], p) for p in range(2)]
            ys = [_swiglu(xs[p], wgu_buf.at[wslot + p], wdn_buf.at[wslot + p]) for p in range(2)]
            for p in range(2):
                yp_ref[p] = _rows_to_slabs(ys[p], il_ref.at[p])
            for p in range(2):
                put_rows(p, bases[p], counts[p])

            for p in range(2):
                def more(c, carry, p=p):
                    base = chunk_base(2 * i + p, c)
                    y = _swiglu(load_rows(base, p), wgu_buf.at[wslot + p], wdn_buf.at[wslot + p])
                    yp_ref[p] = _rows_to_slabs(y, il_ref.at[p])
                    put_rows(p, base, counts[p] - c * EROWS)
                    return carry

                lax.fori_loop(1, (counts[p] + EROWS - 1) // EROWS, more, 0)
            return carry0

        lax.fori_loop(0, N_EXPERTS // 2, expert_pair, 0)

    @pl.when(s >= 1)
    def _combine():
        sb = s - 1

        def group(j, carry):
            tj = sb * (CROWS // LANES) + j
            for sub in range(LANES // SUBLANES):
                r0 = pl.multiple_of(j * LANES + sub * SUBLANES, SUBLANES)
                half = sub % 2
                for u in range(SUBLANES):
                    lane = sub * SUBLANES + u
                    acc = None
                    for k in range(TOP_K):
                        z = pltpu.bitcast(slab_at(xy_ref, dest_ref[tj, k, lane])[...], BF16).astype(F32)
                        acc = wrow_ref[tj, k, lane] * z if acc is None else acc + wrow_ref[tj, k, lane] * z
                    zf_ref[half, u * 2 * slab:(u + 1) * 2 * slab, :] = acc
                comb_ref[pl.ds(r0, SUBLANES), :] = jnp.concatenate(
                    [zf_ref[half, pl.ds(2 * c + h, SUBLANES, stride=2 * slab), :]
                     for h in range(2) for c in range(slab)], axis=1)
            return carry

        lax.fori_loop(0, CROWS // LANES, group, 0)
        shared = _swiglu(hb_ref[...], sgu_ref, sdn_ref)
        o_ref[...] = xm_ref[...] + g2_ref[...] * (comb_ref[...] + shared)


def _moe(h, h_slabs, xm, g2, logits_t, router_bias, w_gu, w_dn, s_gu, s_dn, layer, seq):
    n, d = h.shape
    slab = d // (2 * LANES)
    tile = min(MOE_TILE, seq)
    nt = n // tile
    per_b = seq // tile
    csteps = tile // CROWS
    assert seq % tile == 0 and tile % CROWS == 0
    assert tile % LANES == 0 and CROWS % LANES == 0
    seg_start, seg_count, dest_rows, w_rows = _route(logits_t, router_bias, tile, slab)
    rows = _moe_rows(tile)
    table = pl.BlockSpec((None, tile // LANES, SUBLANES, LANES), lambda t, s, a, b: (t, 0, 0, 0),
                         memory_space=pltpu.SMEM)
    cst = lambda s: jnp.maximum(s - 1, 0)
    crow = pl.BlockSpec((CROWS, d), lambda t, s, a, b: (t * csteps + cst(s), 0))
    hbm = pl.BlockSpec(memory_space=pl.ANY)
    wslots = 2 * (WEIGHT_PAIRS_AHEAD + 1)
    single = pl.Buffered(1)
    return pl.pallas_call(
        _moe_kernel,
        grid_spec=pltpu.PrefetchScalarGridSpec(
            num_scalar_prefetch=2,
            grid=(nt, 1 + csteps),
            in_specs=[pl.BlockSpec((tile * 2 * slab, LANES), lambda t, s, a, b: (t, 0),
                                   pipeline_mode=single),
                      crow, table, table,
                      hbm, hbm,
                      pl.BlockSpec((None,) + s_gu.shape[1:], lambda t, s, a, b: (layer, 0, 0),
                                   pipeline_mode=single),
                      pl.BlockSpec((None,) + s_dn.shape[1:], lambda t, s, a, b: (layer, 0, 0),
                                   pipeline_mode=single),
                      crow,
                      pl.BlockSpec((None, 1, d), lambda t, s, a, b: (t // per_b, 0, 0))],
            out_specs=crow,
            scratch_shapes=[pltpu.VMEM((rows * slab, LANES), U32),
                            pltpu.VMEM((2, EROWS * 2 * slab, LANES), F32),
                            pltpu.VMEM((2, EROWS * slab, LANES), U32),
                            pltpu.VMEM((2, SUBLANES * 2 * slab, LANES), F32),
                            pltpu.VMEM((CROWS, d), F32),
                            pltpu.VMEM((wslots, d, 2 * EXPERT_DIM), BF16),
                            pltpu.VMEM((wslots, EXPERT_DIM, d), BF16),
                            pltpu.SemaphoreType.DMA((2, wslots))]),
        out_shape=jax.ShapeDtypeStruct((n, d), F32),
        compiler_params=_cparams("arbitrary", "arbitrary"),
    )(seg_start, seg_count, h_slabs, h, dest_rows, w_rows, w_gu, w_dn, s_gu, s_dn, xm, g2)


def _mla_prep_kernel(x_ref, n1_ref, sc1_ref, sh1_ref, nk_ref, sck_ref, shk_ref,
                     wdq_ref, gql_ref, wuq_ref, wuqs_ref, wdown_ref, gkl_ref, wk_ref, gk_ref, wuv_ref,
                     qa_ref, qb_ref, kc_ref, ksa_ref, ksb_ref, q_ref, k_ref, v_ref):
    xn = _rms(x_ref[...])
    hq = xn * n1_ref[...] * (1.0 + sc1_ref[...]) + sh1_ref[...]
    hk = xn * nk_ref[...] * (1.0 + sck_ref[...]) + shk_ref[...]
    n_heads = q_ref.shape[1] // LANES
    half = ROPE_DIM // 2

    cq = (_rms(jnp.dot(hq.astype(BF16), wdq_ref[...], preferred_element_type=F32)) * gql_ref[...]).astype(BF16)
    q_raw = jnp.dot(cq, wuq_ref[...], preferred_element_type=F32)
    q_swp = jnp.dot(cq, wuqs_ref[...], preferred_element_type=F32)
    qa, qb = qa_ref[...], qb_ref[...]
    for h in range(n_heads):
        sl = slice(h * LANES, (h + 1) * LANES)
        t = q_raw[:, sl]
        r = lax.rsqrt(jnp.sum(t * t, axis=-1, keepdims=True) / QK_DIM + NORM_EPS)
        q_ref[:, sl] = (r * (t * qa + q_swp[:, sl] * qb)).astype(q_ref.dtype)

    down = jnp.dot(hk.astype(BF16), wdown_ref[...], preferred_element_type=F32)
    ckv = (_rms(down[:, :KV_RANK]) * gkl_ref[...]).astype(BF16)
    pe = down[:, KV_RANK:]
    ss_pe = jnp.sum(pe * pe, axis=-1, keepdims=True)
    rot = (pe * kc_ref[...] + pltpu.roll(pe, LANES - half, 1) * ksa_ref[...]
           + pltpu.roll(pe, half, 1) * ksb_ref[...])
    rot_hi = rot.astype(BF16)
    rot_lo = (rot - rot_hi.astype(F32)).astype(BF16)
    k_raw = jnp.dot(jnp.concatenate([ckv, rot_hi, rot_lo], axis=1), wk_ref[...],
                    preferred_element_type=F32)
    nope = lax.broadcasted_iota(jnp.int32, (x_ref.shape[0], LANES), 1) < NOPE_DIM
    for h in range(n_heads):
        sl = slice(h * LANES, (h + 1) * LANES)
        t = k_raw[:, sl]
        ss = jnp.sum(jnp.where(nope, t * t, 0.0), axis=-1, keepdims=True) + ss_pe
        k_ref[:, sl] = (t * lax.rsqrt(ss / QK_DIM + NORM_EPS) * gk_ref[...]).astype(k_ref.dtype)
    v_ref[...] = jnp.dot(ckv, wuv_ref[...], preferred_element_type=F32).astype(v_ref.dtype)


def _pad_heads(w, n_heads, width):
    r = w.shape[0]
    return jnp.pad(w.reshape(r, n_heads, width), ((0, 0), (0, 0), (0, LANES - width))).reshape(r, n_heads * LANES)


def _swap_rope_halves(a):
    half = ROPE_DIM // 2
    lead = a.shape[:-1]
    return jnp.concatenate([jnp.zeros(lead + (NOPE_DIM,), a.dtype), a[..., NOPE_DIM + half:QK_DIM],
                            a[..., NOPE_DIM:NOPE_DIM + half],
                            jnp.zeros(lead + (LANES - QK_DIM,), a.dtype)], axis=-1)


def _mla_prep(xf, pos, n1, sc1, sh1, nk, sck, shk, w_dq, g_ql, w_uq, g_q, w_down, g_kl, w_uk, w_uv,
              g_k, seq):
    n, d = xf.shape
    n_heads = w_uv.shape[1] // V_DIM
    q_rank = w_dq.shape[1]
    hw = n_heads * LANES
    tm = min(ROW_TILE, seq)
    per_b = seq // tm

    half = ROPE_DIM // 2
    inv = jnp.power(ROPE_THETA, -jnp.arange(half, dtype=F32) / half)
    ang = pos.astype(F32).reshape(n, 1) * inv
    cs = jnp.concatenate([jnp.cos(ang), jnp.sin(ang)], axis=1)
    tail = LANES - QK_DIM
    q_mul = QK_DIM ** -0.5 * LOG2E
    j = np.arange(half)
    sel = np.zeros((ROPE_DIM, 5, LANES), np.float32)
    sel[j, 0, NOPE_DIM + j] = sel[j, 0, NOPE_DIM + half + j] = 1.0
    sel[half + j, 1, NOPE_DIM + j], sel[half + j, 1, NOPE_DIM + half + j] = -1.0, 1.0
    sel[j, 2, j] = sel[j, 2, half + j] = 1.0
    sel[half + j, 3, j] = -1.0
    sel[half + j, 4, half + j] = 1.0
    gq_pad = jnp.pad(g_q, (0, tail)) * q_mul
    gq_swp = _swap_rope_halves(g_q.reshape(1, QK_DIM))[0] * q_mul
    gk1, gk2 = g_k[NOPE_DIM:NOPE_DIM + half], g_k[NOPE_DIM + half:]
    kpad = jnp.zeros((LANES - ROPE_DIM,), F32)
    k_own, k_swp = jnp.concatenate([gk1, gk2, kpad]), jnp.concatenate([gk2, gk1, kpad])
    place = (jnp.asarray(sel) * jnp.stack([gq_pad, gq_swp, k_own, k_swp, k_swp])).reshape(ROPE_DIM, -1)
    const = jnp.pad(gq_pad[:NOPE_DIM], (0, 5 * LANES - NOPE_DIM))
    tables = jnp.dot(cs, place, precision=lax.Precision.HIGHEST) + const
    gk_mask = jnp.concatenate([g_k[:NOPE_DIM], jnp.ones((ROPE_DIM,), F32),
                               jnp.zeros((tail,), F32)]).reshape(1, LANES)

    wuq_p = _pad_heads(w_uq, n_heads, QK_DIM).astype(BF16)
    wuq_s = _swap_rope_halves(w_uq.reshape(q_rank, n_heads, QK_DIM)).reshape(q_rank, hw).astype(BF16)
    wdown_p = jnp.pad(w_down, ((0, 0), (0, LANES - ROPE_DIM))).astype(BF16)
    eye = jnp.pad(jnp.eye(ROPE_DIM, dtype=F32), ((0, LANES - ROPE_DIM), (NOPE_DIM, tail)))
    pe_rows = jnp.tile(eye, (1, n_heads))
    wk = jnp.concatenate([_pad_heads(w_uk, n_heads, NOPE_DIM), pe_rows, pe_rows], axis=0).astype(BF16)

    vec = pl.BlockSpec((None, 1, d), lambda i: (i // per_b, 0, 0))
    full = lambda a: pl.BlockSpec(a.shape, lambda i: (0,) * a.ndim)
    row = lambda wdt: pl.BlockSpec((tm, wdt), lambda i: (i, 0))
    args = [xf, n1, sc1, sh1, nk, sck, shk,
            w_dq.astype(BF16), g_ql.reshape(1, q_rank), wuq_p, wuq_s,
            wdown_p, g_kl.reshape(1, KV_RANK), wk, gk_mask, w_uv.astype(BF16),
            tables, tables, tables, tables, tables]
    table_specs = [pl.BlockSpec((tm, LANES), lambda i, c=c: (i, c)) for c in range(5)]
    specs = [row(d), full(n1), vec, vec, full(nk), vec, vec] + [full(a) for a in args[7:16]] + table_specs
    return pl.pallas_call(
        _mla_prep_kernel,
        grid=(n // tm,),
        in_specs=specs,
        out_specs=[row(hw), row(hw), row(n_heads * V_DIM)],
        out_shape=[jax.ShapeDtypeStruct((n, hw), BF16), jax.ShapeDtypeStruct((n, hw), BF16),
                   jax.ShapeDtypeStruct((n, n_heads * V_DIM), BF16)],
        compiler_params=_cparams("parallel"),
    )(*args)


def _causal_bias(seq):
    a = np.arange(QBLK)[:, None]
    c = np.arange(seq)[None, :]
    return np.where(c <= (seq - QBLK) + a, 0.0, MASKED).astype(np.float32)


def _mla_attn_kernel(q_ref, k_ref, v_ref, mask_ref, wa_ref, wb_ref, o_ref, wa_out, wb_out,
                     s_ref, p_ref, ve_ref):
    wa_out[...] = wa_ref[...].astype(BF16)
    wb_out[...] = wb_ref[...].astype(BF16)
    seq = q_ref.shape[0]
    lane_lo = lax.broadcasted_iota(jnp.int32, (QBLK, LANES), 1) < V_DIM

    def qk_of(blk, h, w):
        lanes = slice(h * LANES, (h + 1) * LANES)
        return q_ref[w - QBLK:w, lanes], k_ref[0:w, lanes]

    _attention_blocks(seq, qk_of, v_ref, mask_ref, s_ref, p_ref, ve_ref, o_ref, lane_lo,
                      diagonal_bias=True)


def _mla_attention(q, k, v, cast_weights, layer):
    bsz, seq, hw = q.shape
    n_pairs = hw // (2 * LANES)
    qk = pl.BlockSpec((None, seq, 2 * LANES), lambda b, h: (b, 0, h))
    vo = pl.BlockSpec((None, seq, LANES), lambda b, h: (b, 0, h))
    grid = (bsz, n_pairs)
    w_in, w_out, w_shapes = _cast_rider(cast_weights, layer, grid)
    return pl.pallas_call(
        _mla_attn_kernel,
        grid=grid,
        in_specs=[qk, qk, vo, pl.BlockSpec((QBLK, seq), lambda b, h: (0, 0))] + w_in,
        out_specs=[vo] + w_out,
        scratch_shapes=_attn_scratch(seq),
        out_shape=[jax.ShapeDtypeStruct(v.shape, BF16)] + w_shapes,
        compiler_params=_cparams("parallel", "parallel"),
    )(q, k, v, jnp.asarray(_causal_bias(seq)), *cast_weights)


def kernel(x, c, pos, ada_w, ada_b, norm1_g, norm2_g, a_w_qkv, a_q_norm, a_k_norm, a_w_o, kv_ada_w, kv_ada_b, kv_norm_g, kv_w_down, kv_latent_norm, kv_w_uk, kv_w_uv, kv_k_norm, b_w_dq, b_q_latent_norm, b_w_uq, b_q_norm, b_w_o, moe_w_router, moe_router_bias, moe_w_gate_up, moe_w_down, moe_shared_gate_up, moe_shared_down):
    bsz, seq, d = x.shape
    n = bsz * seq
    depth = ada_w.shape[0]
    n_a = a_w_qkv.shape[0]
    assert seq % QBLK == 0
    xf = x.reshape(n, d)

    expert_weights = (moe_w_gate_up, moe_w_down)
    s_gu = moe_shared_gate_up.astype(BF16)
    s_dn = moe_shared_down.astype(BF16)
    row = lambda g: g.reshape(1, d)

    for layer in range(depth):
        mod = _modulation(c, ada_w, ada_b, layer).reshape(bsz, 6, 1, d)
        sh1, sc1, g1, sh2, sc2, g2 = [mod[:, i] for i in range(6)]
        if layer < n_a:
            qkv = _qkv_proj(xf, row(norm1_g[layer]), sc1, sh1, a_w_qkv[layer].astype(BF16), seq)
            o, w_gu, w_dn = _dilated_attention(qkv.reshape(bsz, seq, -1), pos, a_q_norm[layer],
                                               a_k_norm[layer], expert_weights, layer)
            w_o = a_w_o[layer]
        else:
            j = layer - n_a
            kmod = _modulation(c, kv_ada_w[None], kv_ada_b[None], 0).reshape(bsz, 2, 1, d)
            q, k, v = _mla_prep(xf, pos, row(norm1_g[layer]), sc1, sh1, row(kv_norm_g), kmod[:, 1],
                                kmod[:, 0], b_w_dq[j], b_q_latent_norm[j], b_w_uq[j], b_q_norm[j],
                                kv_w_down, kv_latent_norm, kv_w_uk, kv_w_uv, kv_k_norm, seq)
            o, w_gu, w_dn = _mla_attention(q.reshape(bsz, seq, -1), k.reshape(bsz, seq, -1),
                                           v.reshape(bsz, seq, -1), expert_weights, layer)
            w_o = b_w_o[j]
        xm, h2, h2_slabs, logits = _post_attn(o.reshape(n, d), w_o.astype(BF16), xf, g1,
                                              row(norm2_g[layer]), sc2, sh2, moe_w_router[layer], seq)
        xf = _moe(h2, h2_slabs, xm, g2, logits, moe_router_bias[layer], w_gu, w_dn, s_gu, s_dn,
                  layer, seq)
    return xf.reshape(bsz, seq, d)
```

```python
import functools
import math

import numpy as np
import jax
import jax.numpy as jnp
from jax import lax
from jax.experimental import pallas as pl
from jax.experimental.pallas import tpu as pltpu

F32 = jnp.float32
BF16 = jnp.bfloat16
U32 = jnp.uint32

NORM_EPS = 1e-6
MASKED = -1e30
LOG2E = math.log2(math.e)

LANES = 128
SUBLANES = 8
VMEM_LIMIT = 56 * 1024 * 1024

A_HEAD_DIM = 64
DILATED_PAIRS = ((128, 1), (512, 4), (2048, 16))
QBLK = 256
NOPE_DIM = 64
ROPE_DIM = 32
QK_DIM = NOPE_DIM + ROPE_DIM
V_DIM = 64
KV_RANK = 256
ROPE_THETA = 10000.0
N_EXPERTS = 64
N_GROUPS = 8
TOPK_GROUPS = 4
TOP_K = 6
EXPERT_DIM = 256
ROUTED_SCALE = 2.5
MOE_TILE = 2048
EROWS = 256
CROWS = 256

ROW_TILE = 512


def _cparams(*sem):
    return pltpu.CompilerParams(dimension_semantics=sem, vmem_limit_bytes=VMEM_LIMIT)


def _silu(v):
    return v * jax.nn.sigmoid(v)


def _rms(v):
    return v * lax.rsqrt(jnp.mean(v * v, axis=-1, keepdims=True) + NORM_EPS)


def _dot_nt(a, b):
    return lax.dot_general(a, b, (((1,), (1,)), ((), ())), preferred_element_type=F32)


def _mod_kernel(c_ref, w_ref, b_ref, o_ref):
    cond = _silu(c_ref[...])
    o_ref[...] = jnp.dot(cond, w_ref[...], preferred_element_type=F32,
                         precision=lax.Precision.HIGHEST) + b_ref[...]


def _modulation(c, w, b, layer):
    bsz, d = c.shape
    m = w.shape[-1]
    tn = 1024
    return pl.pallas_call(
        _mod_kernel,
        grid=(m // tn,),
        in_specs=[pl.BlockSpec((bsz, d), lambda j: (0, 0)),
                  pl.BlockSpec((None, d, tn), lambda j: (layer, 0, j)),
                  pl.BlockSpec((None, 1, tn), lambda j: (layer, 0, j))],
        out_specs=pl.BlockSpec((bsz, tn), lambda j: (0, j)),
        out_shape=jax.ShapeDtypeStruct((bsz, m), F32),
        compiler_params=_cparams("arbitrary"),
    )(c, w, b.reshape(b.shape[0], 1, m))


def _qkv_kernel(x_ref, g_ref, sc_ref, sh_ref, w_ref, o_ref):
    h = _rms(x_ref[...]) * g_ref[...]
    h = h * (1.0 + sc_ref[...]) + sh_ref[...]
    o_ref[...] = jnp.dot(h.astype(BF16), w_ref[...], preferred_element_type=F32).astype(o_ref.dtype)


def _qkv_proj(xf, g, sc, sh, w, seq):
    n, d = xf.shape
    m = w.shape[1]
    tm = min(ROW_TILE, seq)
    per_b = seq // tm
    vec = pl.BlockSpec((None, 1, d), lambda i: (i // per_b, 0, 0))
    return pl.pallas_call(
        _qkv_kernel,
        grid=(n // tm,),
        in_specs=[pl.BlockSpec((tm, d), lambda i: (i, 0)),
                  pl.BlockSpec((1, d), lambda i: (0, 0)),
                  vec, vec,
                  pl.BlockSpec((d, m), lambda i: (0, 0))],
        out_specs=pl.BlockSpec((tm, m), lambda i: (i, 0)),
        out_shape=jax.ShapeDtypeStruct((n, m), BF16),
        compiler_params=_cparams("parallel"),
    )(xf, g, sc, sh, w)


ATT_SLOTS = 4


def _attention_blocks(seq, qk_of, v_ref, bias_ref, s_ref, p_ref, ve_ref, o_ref, lane_lo,
                      diagonal_bias=False):
    nblk = seq // QBLK

    for r0 in range(0, seq, QBLK):
        v = v_ref[r0:r0 + QBLK, :].astype(F32)
        ve_ref[0, r0:r0 + QBLK, :] = jnp.where(lane_lo, v, 1.0).astype(BF16)
        ve_ref[1, r0:r0 + QBLK, :] = jnp.where(lane_lo, 1.0, v).astype(BF16)

    def slot(blk, h):
        return (blk % 2) * 2 + h

    def scores(blk):
        w = (blk + 1) * QBLK
        lo = w - QBLK if diagonal_bias else 0
        bias = bias_ref[:, seq - w + lo:seq]
        for h in range(2):
            q, k = qk_of(blk, h, w)
            s = _dot_nt(q, k)
            if lo:
                s_ref[slot(blk, h), :, 0:lo] = s[:, :lo]
            s_ref[slot(blk, h), :, lo:w] = s[:, lo:] + bias

    for blk in range(nblk):
        w = (blk + 1) * QBLK
        scores(blk)
        for h in range(2):
            s = s_ref[slot(blk, h), :, 0:w]
            p_ref[slot(blk, h), :, 0:w] = jnp.exp2(s - jnp.max(s, axis=-1, keepdims=True)).astype(BF16)
        acc_a, acc_b = [jnp.dot(p_ref[slot(blk, h), :, 0:w], ve_ref[h, 0:w, :],
                                preferred_element_type=F32) for h in range(2)]
        sums = pltpu.roll(jnp.where(lane_lo, acc_b, acc_a), LANES // 2, 1)
        o_ref[blk * QBLK:w, :] = (jnp.where(lane_lo, acc_a, acc_b) / sums).astype(o_ref.dtype)


def _attn_scratch(seq):
    return [pltpu.VMEM((ATT_SLOTS, QBLK, seq), F32), pltpu.VMEM((ATT_SLOTS, QBLK, seq), BF16),
            pltpu.VMEM((2, seq, LANES), BF16)]


def _cast_rider(weights, layer, grid):
    steps = grid[0] * grid[1]
    in_specs, out_specs, out_shapes = [], [], []
    for w in weights:
        n_exp = w.shape[1]
        assert n_exp % steps == 0
        per = n_exp // steps
        blk = (per,) + w.shape[2:]
        in_specs.append(pl.BlockSpec((None,) + blk, lambda b, h: (layer, b * grid[1] + h, 0, 0)))
        out_specs.append(pl.BlockSpec(blk, lambda b, h: (b * grid[1] + h, 0, 0)))
        out_shapes.append(jax.ShapeDtypeStruct(w.shape[1:], BF16))
    return in_specs, out_specs, out_shapes


def _split3(v):
    p1 = v.astype(BF16).astype(F32)
    p2 = (v - p1).astype(BF16).astype(F32)
    return p1, p2, v - p1 - p2


def _dilated_log_multiplicity(seq):
    a = np.arange(QBLK)[:, None]
    c = np.arange(seq)[None, :]
    d = (seq - QBLK) + a - c
    mult = np.zeros((QBLK, seq), np.int32)
    for win, dil in DILATED_PAIRS:
        mult += ((d >= 0) & (d % dil == 0) & (d <= win)).astype(np.int32)
    return np.where(mult > 0, np.log2(np.maximum(mult, 1)), MASKED).astype(np.float32)


N_FEAT = 9


def _dil_attn_kernel(q_ref, k_ref, v_ref, posc_ref, sfeat_ref, gq_ref, gk_ref, logm_ref,
                     wa_ref, wb_ref, o_ref, wa_out, wb_out, qa_ref, qb_ref, ks_ref, s_ref, p_ref,
                     ve_ref):
    wa_out[...] = wa_ref[...].astype(BF16)
    wb_out[...] = wb_ref[...].astype(BF16)
    seq = q_ref.shape[0]
    lane = lax.broadcasted_iota(jnp.int32, (QBLK, LANES), 1)
    lane_lo = lane < A_HEAD_DIM
    piece = [functools.reduce(jnp.logical_or, [lane == i + 3 * j for j in range(N_FEAT // 3)])
             for i in range(3)]

    def head_norm(ref, r0, g_ref, mul):
        t = ref[r0:r0 + QBLK, :].astype(F32)
        sq = t * t
        ssa = jnp.sum(jnp.where(lane_lo, sq, 0.0), axis=-1, keepdims=True)
        ssb = jnp.sum(jnp.where(lane_lo, 0.0, sq), axis=-1, keepdims=True)
        r = jnp.where(lane_lo, lax.rsqrt(ssa / A_HEAD_DIM + NORM_EPS),
                      lax.rsqrt(ssb / A_HEAD_DIM + NORM_EPS))
        return t * r * (g_ref[...] * mul)

    rel0 = posc_ref[0:1, :]
    for r0 in range(0, seq, QBLK):
        rows = slice(r0, r0 + QBLK)
        qn = head_norm(q_ref, r0, gq_ref, A_HEAD_DIM ** -0.5 * LOG2E)
        qa_ref[rows, :LANES] = jnp.where(lane_lo, qn, 0.0).astype(BF16)
        qb_ref[rows, :LANES] = jnp.where(lane_lo, 0.0, qn).astype(BF16)
        qa_ref[rows, LANES:] = jnp.broadcast_to(sfeat_ref[0:1, :], (QBLK, LANES)).astype(BF16)
        qb_ref[rows, LANES:] = jnp.broadcast_to(sfeat_ref[1:2, :], (QBLK, LANES)).astype(BF16)
        ks_ref[rows, :LANES] = head_norm(k_ref, r0, gk_ref, 1.0).astype(BF16)
        p1, p2, p3 = _split3((posc_ref[rows, :] - rel0).astype(F32))
        feat = jnp.where(piece[0], p1, jnp.where(piece[1], p2, jnp.where(piece[2], p3, 0.0)))
        ks_ref[rows, LANES:] = feat.astype(BF16)

    def qk_of(blk, h, w):
        return (qa_ref, qb_ref)[h][w - QBLK:w, :], ks_ref[0:w, :]

    _attention_blocks(seq, qk_of, v_ref, logm_ref, s_ref, p_ref, ve_ref, o_ref, lane_lo)


def _dilated_attention(qkv, pos, q_norm, k_norm, cast_weights, layer):
    bsz, seq, three_d = qkv.shape
    d = three_d // 3
    n_pairs = d // LANES
    n_heads = d // A_HEAD_DIM
    slopes = jnp.exp2(-8.0 * jnp.arange(1, n_heads + 1, dtype=F32) / n_heads) * LOG2E
    sfeat = jnp.repeat(jnp.stack(_split3(slopes), axis=1), 3, axis=1)
    sfeat = jnp.pad(sfeat, ((0, 0), (0, LANES - N_FEAT))).reshape(n_pairs, 2, LANES)
    logm = jnp.asarray(_dilated_log_multiplicity(seq))
    gq = jnp.tile(q_norm.reshape(1, A_HEAD_DIM), (1, 2))
    gk = jnp.tile(k_norm.reshape(1, A_HEAD_DIM), (1, 2))
    blk = lambda off: pl.BlockSpec((None, seq, LANES), lambda b, h: (b, 0, off + h))
    const2 = pl.BlockSpec((1, LANES), lambda b, h: (0, 0))
    ext = pltpu.VMEM((seq, 2 * LANES), BF16)
    grid = (bsz, n_pairs)
    w_in, w_out, w_shapes = _cast_rider(cast_weights, layer, grid)
    return pl.pallas_call(
        _dil_attn_kernel,
        grid=grid,
        in_specs=[blk(0), blk(n_pairs), blk(2 * n_pairs),
                  pl.BlockSpec((None, seq, 1), lambda b, h: (b, 0, 0)),
                  pl.BlockSpec((None, 2, LANES), lambda b, h: (h, 0, 0)),
                  const2, const2,
                  pl.BlockSpec((QBLK, seq), lambda b, h: (0, 0))] + w_in,
        out_specs=[pl.BlockSpec((None, seq, LANES), lambda b, h: (b, 0, h))] + w_out,
        scratch_shapes=[ext, ext, ext] + _attn_scratch(seq),
        out_shape=[jax.ShapeDtypeStruct((bsz, seq, d), BF16)] + w_shapes,
        compiler_params=_cparams("parallel", "parallel"),
    )(qkv, qkv, qkv, pos.reshape(bsz, seq, 1), sfeat, gq, gk, logm, *cast_weights)


def _post_attn_kernel(o_ref, wo_ref, x_ref, g1_ref, n2_ref, sc_ref, sh_ref, wrh_ref, wrl_ref,
                      xm_ref, h_ref, hs_ref, lg_ref, il_ref):
    y = jnp.dot(o_ref[...], wo_ref[...], preferred_element_type=F32)
    xm = x_ref[...] + g1_ref[...] * y
    xm_ref[...] = xm
    h = _rms(xm) * n2_ref[...]
    h = h * (1.0 + sc_ref[...]) + sh_ref[...]
    h_ref[...] = h.astype(h_ref.dtype)
    hs_ref[...] = _rows_to_slab_rows(h, il_ref)
    h_hi = h.astype(BF16)
    h_lo = (h - h_hi.astype(F32)).astype(BF16)
    lg_ref[...] = (_dot_nt(wrh_ref[...], h_hi) + _dot_nt(wrh_ref[...], h_lo)
                   + _dot_nt(wrl_ref[...], h_hi))


def _post_attn(o, w_o, xf, g1, n2, sc2, sh2, w_router, seq):
    n, d = xf.shape
    ne = w_router.shape[1]
    tm = min(ROW_TILE, seq)
    per_b = seq // tm
    slab_rows = d // LANES
    wr_t = w_router.T.astype(F32)
    wr_hi = wr_t.astype(BF16)
    vec = pl.BlockSpec((None, 1, d), lambda i: (i // per_b, 0, 0))
    row = lambda w: pl.BlockSpec((tm, w), lambda i: (i, 0))
    return pl.pallas_call(
        _post_attn_kernel,
        grid=(n // tm,),
        in_specs=[row(d), pl.BlockSpec((d, d), lambda i: (0, 0)), row(d), vec,
                  pl.BlockSpec((1, d), lambda i: (0, 0)), vec, vec,
                  pl.BlockSpec((ne, d), lambda i: (0, 0)), pl.BlockSpec((ne, d), lambda i: (0, 0))],
        out_specs=[row(d), row(d), pl.BlockSpec((tm * slab_rows, LANES), lambda i: (i, 0)),
                   pl.BlockSpec((ne, tm), lambda i: (0, i))],
        out_shape=[jax.ShapeDtypeStruct((n, d), F32), jax.ShapeDtypeStruct((n, d), BF16),
                   jax.ShapeDtypeStruct((n * slab_rows, LANES), BF16),
                   jax.ShapeDtypeStruct((ne, n), F32)],
        scratch_shapes=[pltpu.VMEM((tm * slab_rows, LANES), F32)],
        compiler_params=_cparams("parallel"),
    )(o, w_o, xf, g1, n2, sc2, sh2, wr_hi, (wr_t - wr_hi.astype(F32)).astype(BF16))


def _moe_rows(tile):
    rows = tile * TOP_K + N_EXPERTS * (SUBLANES - 1) + EROWS
    return -(-rows // SUBLANES) * SUBLANES


def _first_index(hit, index, limit):
    return jnp.min(jnp.where(hit, index, limit), axis=0, keepdims=True)


def _route_kernel(lg_ref, bias_ref, tri_ref, dest_ref, w_ref, seg_ref, cnt_ref, *, row_pitch):
    t = lg_ref.shape[1]
    per_g = N_EXPERTS // N_GROUPS
    sub = lax.broadcasted_iota(jnp.int32, (per_g, t), 0).astype(F32)
    neg = -jnp.inf
    scores, sel = [], []
    for g in range(N_GROUPS):
        rows = slice(g * per_g, (g + 1) * per_g)
        sc = jax.nn.sigmoid(lg_ref[rows, :])
        scores.append(sc)
        sel.append(sc + bias_ref[rows, :])

    gs = []
    for g in range(N_GROUPS):
        m1 = jnp.max(sel[g], axis=0, keepdims=True)
        first = _first_index(sel[g] == m1, sub, float(per_g))
        m2 = jnp.max(jnp.where(sub == first, neg, sel[g]), axis=0, keepdims=True)
        gs.append(m1 + m2)
    gsc = jnp.concatenate(gs, axis=0)
    gsub = lax.broadcasted_iota(jnp.int32, (N_GROUPS, t), 0).astype(F32)
    gkeep = jnp.zeros((N_GROUPS, t), F32)
    for _ in range(TOPK_GROUPS):
        m = jnp.max(gsc, axis=0, keepdims=True)
        pick = gsub == _first_index(gsc == m, gsub, float(N_GROUPS))
        gkeep = jnp.where(pick, 1.0, gkeep)
        gsc = jnp.where(pick, neg, gsc)

    eidx = [sub + float(g * per_g) for g in range(N_GROUPS)]
    masked = [jnp.where(jnp.broadcast_to(gkeep[g:g + 1, :], (per_g, t)) > 0.5, sel[g], neg)
              for g in range(N_GROUPS)]
    chosen = [jnp.zeros((per_g, t), F32) for _ in range(N_GROUPS)]
    wsel = [jnp.zeros((per_g, t), F32) for _ in range(N_GROUPS)]
    picked = []
    for _ in range(TOP_K):
        m = jnp.max(functools.reduce(jnp.maximum, masked), axis=0, keepdims=True)
        cand = functools.reduce(jnp.minimum, [jnp.where(masked[g] == m, eidx[g], float(N_EXPERTS))
                                              for g in range(N_GROUPS)])
        e_first = jnp.min(cand, axis=0, keepdims=True)
        picked.append(e_first)
        for g in range(N_GROUPS):
            pick = eidx[g] == e_first
            chosen[g] = jnp.where(pick, 1.0, chosen[g])
            wsel[g] = jnp.where(pick, scores[g], wsel[g])
            masked[g] = jnp.where(pick, neg, masked[g])

    denom = jnp.sum(functools.reduce(jnp.add, wsel), axis=0, keepdims=True)

    c_all = jnp.concatenate(chosen, axis=0)
    nblk = t // LANES
    stacked = jnp.concatenate([c_all[:, b * LANES:(b + 1) * LANES] for b in range(nblk)], axis=0)
    within = jnp.dot(stacked.astype(BF16), tri_ref[...], preferred_element_type=F32)
    ranks, before = [], jnp.zeros((N_EXPERTS, 1), F32)
    for b in range(nblk):
        rows = slice(b * N_EXPERTS, (b + 1) * N_EXPERTS)
        ranks.append(within[rows, :] + before)
        before = before + within[rows, LANES - 1:LANES] + stacked[rows, LANES - 1:LANES]
    rank = jnp.concatenate(ranks, axis=1)
    counts = before
    padded = jnp.ceil(counts / SUBLANES) * SUBLANES
    below = (lax.broadcasted_iota(jnp.int32, (N_EXPERTS, N_EXPERTS), 1)
             < lax.broadcasted_iota(jnp.int32, (N_EXPERTS, N_EXPERTS), 0)).astype(F32)
    seg = jnp.dot(below, jnp.broadcast_to(padded, (N_EXPERTS, LANES)), preferred_element_type=F32,
                  precision=lax.Precision.HIGHEST)
    row_of = seg[:, :1] + rank

    dest_rows, w_rows = [], []
    for k in range(TOP_K):
        d_k = jnp.zeros((per_g, t), F32)
        w_k = jnp.zeros((per_g, t), F32)
        for g in range(N_GROUPS):
            hit = eidx[g] == picked[k]
            d_k = jnp.where(hit, row_of[g * per_g:(g + 1) * per_g, :], d_k)
            w_k = jnp.where(hit, wsel[g], w_k)
        dest_rows.append(jnp.sum(d_k, axis=0, keepdims=True))
        w_rows.append(jnp.sum(w_k, axis=0, keepdims=True) / denom * ROUTED_SCALE)
    fill = [jnp.zeros((SUBLANES - TOP_K, t), F32)]
    dest8 = (jnp.concatenate(dest_rows + fill, axis=0) * row_pitch).astype(jnp.int32)
    w8 = jnp.concatenate(w_rows + fill, axis=0)
    for j in range(t // LANES):
        dest_ref[j] = dest8[:, j * LANES:(j + 1) * LANES]
        w_ref[j] = w8[:, j * LANES:(j + 1) * LANES]
    seg_ref[...] = seg.astype(jnp.int32)
    cnt_ref[...] = jnp.broadcast_to(counts, (N_EXPERTS, LANES)).astype(jnp.int32)


def _route(logits_t, router_bias, tile, row_pitch):
    ne, n = logits_t.shape
    nt = n // tile
    tri = jnp.asarray(np.triu(np.ones((LANES, LANES), np.float32), 1), dtype=BF16)
    per_tile = lambda rows, width: pl.BlockSpec((None, rows, width), lambda i: (i, 0, 0))
    table = pl.BlockSpec((None, tile // LANES, SUBLANES, LANES), lambda i: (i, 0, 0, 0))
    dest, wts, seg, cnt = pl.pallas_call(
        functools.partial(_route_kernel, row_pitch=row_pitch),
        grid=(nt,),
        in_specs=[pl.BlockSpec((ne, tile), lambda i: (0, i)),
                  pl.BlockSpec((ne, 1), lambda i: (0, 0)),
                  pl.BlockSpec((LANES, LANES), lambda i: (0, 0))],
        out_specs=[table, table, per_tile(ne, LANES), per_tile(ne, LANES)],
        out_shape=[jax.ShapeDtypeStruct((nt, tile // LANES, SUBLANES, LANES), jnp.int32),
                   jax.ShapeDtypeStruct((nt, tile // LANES, SUBLANES, LANES), F32),
                   jax.ShapeDtypeStruct((nt, ne, LANES), jnp.int32),
                   jax.ShapeDtypeStruct((nt, ne, LANES), jnp.int32)],
        compiler_params=_cparams("parallel"),
    )(logits_t, router_bias.astype(F32).reshape(ne, 1), tri)
    return seg[:, :, 0].reshape(-1), cnt[:, :, 0].reshape(-1), dest, wts


def _swiglu(xb, wgu_ref, wdn_ref):
    gu = jnp.dot(xb, wgu_ref[...], preferred_element_type=F32)
    half = wgu_ref.shape[1] // 2
    act = _silu(gu[:, :half]) * gu[:, half:]
    return jnp.dot(act.astype(BF16), wdn_ref[...], preferred_element_type=F32)


def _slabs_to_rows(p, il_ref, r):
    n = il_ref.shape[0] // r
    z = pltpu.bitcast(p, BF16).astype(F32).reshape(r, n, LANES)
    t = pltpu.einshape("rsl->srl", z)
    return jnp.concatenate([t[2 * c + h] for h in range(2) for c in range(n // 2)], axis=1)


def _rows_to_slab_rows(v, il_ref):
    r = v.shape[0]
    n = il_ref.shape[0] // r
    lane0 = lambda j: ((j % 2) * (n // 2) + j // 2) * LANES
    t = jnp.stack([v[:, lane0(j):lane0(j) + LANES] for j in range(n)], axis=0)
    return pltpu.einshape("srl->rsl", t).reshape(r * n, LANES).astype(BF16)


def _rows_to_slabs(v, il_ref):
    return pltpu.bitcast(_rows_to_slab_rows(v, il_ref), U32)


def _moe_kernel(start_ref, count_ref, hp_ref, hb_ref, dest_ref, wrow_ref, wgu_hbm, wdn_hbm,
                sgu_ref, sdn_ref, xm_ref, g2_ref, o_ref, xy_ref, il_ref, yp_ref, zf_ref,
                comb_ref, wgu_buf, wdn_buf, wsem):
    t = pl.program_id(0)
    s = pl.program_id(1)
    slab = xm_ref.shape[1] // (2 * LANES)
    tile = hp_ref.shape[0] // (2 * slab)
    group_rows = SUBLANES * slab

    def weight_copies(e, slot):
        return (pltpu.make_async_copy(wgu_hbm.at[e], wgu_buf.at[slot], wsem.at[0, slot]),
                pltpu.make_async_copy(wdn_hbm.at[e], wdn_buf.at[slot], wsem.at[1, slot]))

    def slab_at(ref, first):
        return ref.at[pl.ds(pl.multiple_of(first, slab), slab), :]

    @pl.when(s == 0)
    def _first_weights():
        for p in range(2):
            for cp in weight_copies(p, p):
                cp.start()

    @pl.when(jnp.logical_and(t == 0, s == 0))
    def _clear():
        xy_ref[...] = jnp.zeros(xy_ref.shape, U32)

    @pl.when(s == 0)
    def _dispatch():
        def group(j, carry):
            for sub in range(LANES // SUBLANES):
                p0 = pl.multiple_of((j * (LANES // SUBLANES) + sub) * 2 * group_rows, 2 * group_rows)
                rows = pltpu.bitcast(hp_ref[pl.ds(p0, 2 * group_rows), :], U32)
                for u in range(SUBLANES):
                    for k in range(TOP_K):
                        slab_at(xy_ref, dest_ref[j, k, sub * SUBLANES + u])[...] = (
                            rows[u * slab:(u + 1) * slab, :])
            return carry

        lax.fori_loop(0, tile // LANES, group, 0)

    @pl.when(s == 0)
    def _experts():
        def chunk_base(e, c):
            return pl.multiple_of((start_ref[t * N_EXPERTS + e] + c * EROWS) * slab, group_rows)

        def load_rows(base, p):
            return _slabs_to_rows(xy_ref[pl.ds(base, EROWS * slab), :], il_ref.at[p], EROWS).astype(BF16)

        def put_rows(p, base, valid):
            def put(j, c2):
                r0 = pl.multiple_of(j * group_rows, group_rows)
                xy_ref[pl.ds(base + r0, group_rows), :] = yp_ref[p, pl.ds(r0, group_rows), :]
                return c2

            lax.fori_loop(0, (jnp.clip(valid, 0, EROWS) + SUBLANES - 1) // SUBLANES, put, 0)

        def expert_pair(i, carry0):
            wslot = lax.rem(i, 2) * 2

            @pl.when(i + 1 < N_EXPERTS // 2)
            def _prefetch():
                for p in range(2):
                    for cp in weight_copies(2 * i + 2 + p, 2 - wslot + p):
                        cp.start()

            for p in range(2):
                for cp in weight_copies(2 * i + p, wslot + p):
                    cp.wait()
            counts = [count_ref[t * N_EXPERTS + 2 * i + p] for p in range(2)]
            bases = [chunk_base(2 * i + p, 0) for p in range(2)]
            xs = [load_rows(bases[p], p) for p in range(2)]
            ys = [_swiglu(xs[p], wgu_buf.at[wslot + p], wdn_buf.at[wslot + p]) for p in range(2)]
            for p in range(2):
                yp_ref[p] = _rows_to_slabs(ys[p], il_ref.at[p])
            for p in range(2):
                put_rows(p, bases[p], counts[p])

            for p in range(2):
                def more(c, carry, p=p):
                    base = chunk_base(2 * i + p, c)
                    y = _swiglu(load_rows(base, p), wgu_buf.at[wslot + p], wdn_buf.at[wslot + p])
                    yp_ref[p] = _rows_to_slabs(y, il_ref.at[p])
                    put_rows(p, base, counts[p] - c * EROWS)
                    return carry

                lax.fori_loop(1, (counts[p] + EROWS - 1) // EROWS, more, 0)
            return carry0

        lax.fori_loop(0, N_EXPERTS // 2, expert_pair, 0)

    @pl.when(s >= 1)
    def _combine():
        sb = s - 1

        def group(j, carry):
            tj = sb * (CROWS // LANES) + j
            for sub in range(LANES // SUBLANES):
                r0 = pl.multiple_of(j * LANES + sub * SUBLANES, SUBLANES)
                half = sub % 2
                for u in range(SUBLANES):
                    lane = sub * SUBLANES + u
                    acc = None
                    for k in range(TOP_K):
                        z = pltpu.bitcast(slab_at(xy_ref, dest_ref[tj, k, lane])[...], BF16).astype(F32)
                        acc = wrow_ref[tj, k, lane] * z if acc is None else acc + wrow_ref[tj, k, lane] * z
                    zf_ref[half, u * 2 * slab:(u + 1) * 2 * slab, :] = acc
                comb_ref[pl.ds(r0, SUBLANES), :] = jnp.concatenate(
                    [zf_ref[half, pl.ds(2 * c + h, SUBLANES, stride=2 * slab), :]
                     for h in range(2) for c in range(slab)], axis=1)
            return carry

        lax.fori_loop(0, CROWS // LANES, group, 0)
        shared = _swiglu(hb_ref[...], sgu_ref, sdn_ref)
        o_ref[...] = xm_ref[...] + g2_ref[...] * (comb_ref[...] + shared)


def _moe(h, h_slabs, xm, g2, logits_t, router_bias, w_gu, w_dn, s_gu, s_dn, layer, seq):
    n, d = h.shape
    slab = d // (2 * LANES)
    tile = min(MOE_TILE, seq)
    nt = n // tile
    per_b = seq // tile
    csteps = tile // CROWS
    assert seq % tile == 0 and tile % CROWS == 0
    assert tile % LANES == 0 and CROWS % LANES == 0
    seg_start, seg_count, dest_rows, w_rows = _route(logits_t, router_bias, tile, slab)
    rows = _moe_rows(tile)
    table = pl.BlockSpec((None, tile // LANES, SUBLANES, LANES), lambda t, s, a, b: (t, 0, 0, 0),
                         memory_space=pltpu.SMEM)
    cst = lambda s: jnp.maximum(s - 1, 0)
    crow = pl.BlockSpec((CROWS, d), lambda t, s, a, b: (t * csteps + cst(s), 0))
    hbm = pl.BlockSpec(memory_space=pl.ANY)
    single = pl.Buffered(1)
    return pl.pallas_call(
        _moe_kernel,
        grid_spec=pltpu.PrefetchScalarGridSpec(
            num_scalar_prefetch=2,
            grid=(nt, 1 + csteps),
            in_specs=[pl.BlockSpec((tile * 2 * slab, LANES), lambda t, s, a, b: (t, 0),
                                   pipeline_mode=single),
                      crow, table, table,
                      hbm, hbm,
                      pl.BlockSpec((None,) + s_gu.shape[1:], lambda t, s, a, b: (layer, 0, 0),
                                   pipeline_mode=single),
                      pl.BlockSpec((None,) + s_dn.shape[1:], lambda t, s, a, b: (layer, 0, 0),
                                   pipeline_mode=single),
                      crow,
                      pl.BlockSpec((None, 1, d), lambda t, s, a, b: (t // per_b, 0, 0))],
            out_specs=crow,
            scratch_shapes=[pltpu.VMEM((rows * slab, LANES), U32),
                            pltpu.VMEM((2, EROWS * 2 * slab, LANES), F32),
                            pltpu.VMEM((2, EROWS * slab, LANES), U32),
                            pltpu.VMEM((2, SUBLANES * 2 * slab, LANES), F32),
                            pltpu.VMEM((CROWS, d), F32),
                            pltpu.VMEM((4, d, 2 * EXPERT_DIM), BF16),
                            pltpu.VMEM((4, EXPERT_DIM, d), BF16),
                            pltpu.SemaphoreType.DMA((2, 4))]),
        out_shape=jax.ShapeDtypeStruct((n, d), F32),
        compiler_params=_cparams("arbitrary", "arbitrary"),
    )(seg_start, seg_count, h_slabs, h, dest_rows, w_rows, w_gu, w_dn, s_gu, s_dn, xm, g2)


def _mla_prep_kernel(x_ref, n1_ref, sc1_ref, sh1_ref, nk_ref, sck_ref, shk_ref,
                     wdq_ref, gql_ref, wuq_ref, wuqs_ref, wdown_ref, gkl_ref, wk_ref, gk_ref, wuv_ref,
                     qa_ref, qb_ref, kc_ref, ksa_ref, ksb_ref, q_ref, k_ref, v_ref):
    xn = _rms(x_ref[...])
    hq = xn * n1_ref[...] * (1.0 + sc1_ref[...]) + sh1_ref[...]
    hk = xn * nk_ref[...] * (1.0 + sck_ref[...]) + shk_ref[...]
    n_heads = q_ref.shape[1] // LANES
    half = ROPE_DIM // 2

    cq = (_rms(jnp.dot(hq.astype(BF16), wdq_ref[...], preferred_element_type=F32)) * gql_ref[...]).astype(BF16)
    q_raw = jnp.dot(cq, wuq_ref[...], preferred_element_type=F32)
    q_swp = jnp.dot(cq, wuqs_ref[...], preferred_element_type=F32)
    qa, qb = qa_ref[...], qb_ref[...]
    for h in range(n_heads):
        sl = slice(h * LANES, (h + 1) * LANES)
        t = q_raw[:, sl]
        r = lax.rsqrt(jnp.sum(t * t, axis=-1, keepdims=True) / QK_DIM + NORM_EPS)
        q_ref[:, sl] = (r * (t * qa + q_swp[:, sl] * qb)).astype(q_ref.dtype)

    down = jnp.dot(hk.astype(BF16), wdown_ref[...], preferred_element_type=F32)
    ckv = (_rms(down[:, :KV_RANK]) * gkl_ref[...]).astype(BF16)
    pe = down[:, KV_RANK:]
    ss_pe = jnp.sum(pe * pe, axis=-1, keepdims=True)
    rot = (pe * kc_ref[...] + pltpu.roll(pe, LANES - half, 1) * ksa_ref[...]
           + pltpu.roll(pe, half, 1) * ksb_ref[...])
    rot_hi = rot.astype(BF16)
    rot_lo = (rot - rot_hi.astype(F32)).astype(BF16)
    k_raw = jnp.dot(jnp.concatenate([ckv, rot_hi, rot_lo], axis=1), wk_ref[...],
                    preferred_element_type=F32)
    nope = lax.broadcasted_iota(jnp.int32, (x_ref.shape[0], LANES), 1) < NOPE_DIM
    for h in range(n_heads):
        sl = slice(h * LANES, (h + 1) * LANES)
        t = k_raw[:, sl]
        ss = jnp.sum(jnp.where(nope, t * t, 0.0), axis=-1, keepdims=True) + ss_pe
        k_ref[:, sl] = (t * lax.rsqrt(ss / QK_DIM + NORM_EPS) * gk_ref[...]).astype(k_ref.dtype)
    v_ref[...] = jnp.dot(ckv, wuv_ref[...], preferred_element_type=F32).astype(v_ref.dtype)


def _pad_heads(w, n_heads, width):
    r = w.shape[0]
    return jnp.pad(w.reshape(r, n_heads, width), ((0, 0), (0, 0), (0, LANES - width))).reshape(r, n_heads * LANES)


def _swap_rope_halves(a):
    half = ROPE_DIM // 2
    lead = a.shape[:-1]
    return jnp.concatenate([jnp.zeros(lead + (NOPE_DIM,), a.dtype), a[..., NOPE_DIM + half:QK_DIM],
                            a[..., NOPE_DIM:NOPE_DIM + half],
                            jnp.zeros(lead + (LANES - QK_DIM,), a.dtype)], axis=-1)


def _mla_prep(xf, pos, n1, sc1, sh1, nk, sck, shk, w_dq, g_ql, w_uq, g_q, w_down, g_kl, w_uk, w_uv,
              g_k, seq):
    n, d = xf.shape
    n_heads = w_uv.shape[1] // V_DIM
    q_rank = w_dq.shape[1]
    hw = n_heads * LANES
    tm = min(ROW_TILE, seq)
    per_b = seq // tm

    half = ROPE_DIM // 2
    inv = jnp.power(ROPE_THETA, -jnp.arange(half, dtype=F32) / half)
    ang = pos.astype(F32).reshape(n, 1) * inv
    cs = jnp.concatenate([jnp.cos(ang), jnp.sin(ang)], axis=1)
    tail = LANES - QK_DIM
    q_mul = QK_DIM ** -0.5 * LOG2E
    j = np.arange(half)
    sel = np.zeros((ROPE_DIM, 5, LANES), np.float32)
    sel[j, 0, NOPE_DIM + j] = sel[j, 0, NOPE_DIM + half + j] = 1.0
    sel[half + j, 1, NOPE_DIM + j], sel[half + j, 1, NOPE_DIM + half + j] = -1.0, 1.0
    sel[j, 2, j] = sel[j, 2, half + j] = 1.0
    sel[half + j, 3, j] = -1.0
    sel[half + j, 4, half + j] = 1.0
    gq_pad = jnp.pad(g_q, (0, tail)) * q_mul
    gq_swp = _swap_rope_halves(g_q.reshape(1, QK_DIM))[0] * q_mul
    gk1, gk2 = g_k[NOPE_DIM:NOPE_DIM + half], g_k[NOPE_DIM + half:]
    kpad = jnp.zeros((LANES - ROPE_DIM,), F32)
    k_own, k_swp = jnp.concatenate([gk1, gk2, kpad]), jnp.concatenate([gk2, gk1, kpad])
    place = (jnp.asarray(sel) * jnp.stack([gq_pad, gq_swp, k_own, k_swp, k_swp])).reshape(ROPE_DIM, -1)
    const = jnp.pad(gq_pad[:NOPE_DIM], (0, 5 * LANES - NOPE_DIM))
    tables = jnp.dot(cs, place, precision=lax.Precision.HIGHEST) + const
    gk_mask = jnp.concatenate([g_k[:NOPE_DIM], jnp.ones((ROPE_DIM,), F32),
                               jnp.zeros((tail,), F32)]).reshape(1, LANES)

    wuq_p = _pad_heads(w_uq, n_heads, QK_DIM).astype(BF16)
    wuq_s = _swap_rope_halves(w_uq.reshape(q_rank, n_heads, QK_DIM)).reshape(q_rank, hw).astype(BF16)
    wdown_p = jnp.pad(w_down, ((0, 0), (0, LANES - ROPE_DIM))).astype(BF16)
    eye = jnp.pad(jnp.eye(ROPE_DIM, dtype=F32), ((0, LANES - ROPE_DIM), (NOPE_DIM, tail)))
    pe_rows = jnp.tile(eye, (1, n_heads))
    wk = jnp.concatenate([_pad_heads(w_uk, n_heads, NOPE_DIM), pe_rows, pe_rows], axis=0).astype(BF16)

    vec = pl.BlockSpec((None, 1, d), lambda i: (i // per_b, 0, 0))
    full = lambda a: pl.BlockSpec(a.shape, lambda i: (0,) * a.ndim)
    row = lambda wdt: pl.BlockSpec((tm, wdt), lambda i: (i, 0))
    args = [xf, n1, sc1, sh1, nk, sck, shk,
            w_dq.astype(BF16), g_ql.reshape(1, q_rank), wuq_p, wuq_s,
            wdown_p, g_kl.reshape(1, KV_RANK), wk, gk_mask, w_uv.astype(BF16),
            tables, tables, tables, tables, tables]
    table_specs = [pl.BlockSpec((tm, LANES), lambda i, c=c: (i, c)) for c in range(5)]
    specs = [row(d), full(n1), vec, vec, full(nk), vec, vec] + [full(a) for a in args[7:16]] + table_specs
    return pl.pallas_call(
        _mla_prep_kernel,
        grid=(n // tm,),
        in_specs=specs,
        out_specs=[row(hw), row(hw), row(n_heads * V_DIM)],
        out_shape=[jax.ShapeDtypeStruct((n, hw), BF16), jax.ShapeDtypeStruct((n, hw), BF16),
                   jax.ShapeDtypeStruct((n, n_heads * V_DIM), BF16)],
        compiler_params=_cparams("parallel"),
    )(*args)


def _causal_bias(seq):
    a = np.arange(QBLK)[:, None]
    c = np.arange(seq)[None, :]
    return np.where(c <= (seq - QBLK) + a, 0.0, MASKED).astype(np.float32)


def _mla_attn_kernel(q_ref, k_ref, v_ref, mask_ref, wa_ref, wb_ref, o_ref, wa_out, wb_out,
                     s_ref, p_ref, ve_ref):
    wa_out[...] = wa_ref[...].astype(BF16)
    wb_out[...] = wb_ref[...].astype(BF16)
    seq = q_ref.shape[0]
    lane_lo = lax.broadcasted_iota(jnp.int32, (QBLK, LANES), 1) < V_DIM

    def qk_of(blk, h, w):
        lanes = slice(h * LANES, (h + 1) * LANES)
        return q_ref[w - QBLK:w, lanes], k_ref[0:w, lanes]

    _attention_blocks(seq, qk_of, v_ref, mask_ref, s_ref, p_ref, ve_ref, o_ref, lane_lo,
                      diagonal_bias=True)


def _mla_attention(q, k, v, cast_weights, layer):
    bsz, seq, hw = q.shape
    n_pairs = hw // (2 * LANES)
    qk = pl.BlockSpec((None, seq, 2 * LANES), lambda b, h: (b, 0, h))
    vo = pl.BlockSpec((None, seq, LANES), lambda b, h: (b, 0, h))
    grid = (bsz, n_pairs)
    w_in, w_out, w_shapes = _cast_rider(cast_weights, layer, grid)
    return pl.pallas_call(
        _mla_attn_kernel,
        grid=grid,
        in_specs=[qk, qk, vo, pl.BlockSpec((QBLK, seq), lambda b, h: (0, 0))] + w_in,
        out_specs=[vo] + w_out,
        scratch_shapes=_attn_scratch(seq),
        out_shape=[jax.ShapeDtypeStruct(v.shape, BF16)] + w_shapes,
        compiler_params=_cparams("parallel", "parallel"),
    )(q, k, v, jnp.asarray(_causal_bias(seq)), *cast_weights)


def kernel(x, c, pos, ada_w, ada_b, norm1_g, norm2_g, a_w_qkv, a_q_norm, a_k_norm, a_w_o, kv_ada_w, kv_ada_b, kv_norm_g, kv_w_down, kv_latent_norm, kv_w_uk, kv_w_uv, kv_k_norm, b_w_dq, b_q_latent_norm, b_w_uq, b_q_norm, b_w_o, moe_w_router, moe_router_bias, moe_w_gate_up, moe_w_down, moe_shared_gate_up, moe_shared_down):
    bsz, seq, d = x.shape
    n = bsz * seq
    depth = ada_w.shape[0]
    n_a = a_w_qkv.shape[0]
    assert seq % QBLK == 0
    xf = x.reshape(n, d)

    expert_weights = (moe_w_gate_up, moe_w_down)
    s_gu = moe_shared_gate_up.astype(BF16)
    s_dn = moe_shared_down.astype(BF16)
    row = lambda g: g.reshape(1, d)

    for layer in range(depth):
        mod = _modulation(c, ada_w, ada_b, layer).reshape(bsz, 6, 1, d)
        sh1, sc1, g1, sh2, sc2, g2 = [mod[:, i] for i in range(6)]
        if layer < n_a:
            qkv = _qkv_proj(xf, row(norm1_g[layer]), sc1, sh1, a_w_qkv[layer].astype(BF16), seq)
            o, w_gu, w_dn = _dilated_attention(qkv.reshape(bsz, seq, -1), pos, a_q_norm[layer],
                                               a_k_norm[layer], expert_weights, layer)
            w_o = a_w_o[layer]
        else:
            j = layer - n_a
            kmod = _modulation(c, kv_ada_w[None], kv_ada_b[None], 0).reshape(bsz, 2, 1, d)
            q, k, v = _mla_prep(xf, pos, row(norm1_g[layer]), sc1, sh1, row(kv_norm_g), kmod[:, 1],
                                kmod[:, 0], b_w_dq[j], b_q_latent_norm[j], b_w_uq[j], b_q_norm[j],
                                kv_w_down, kv_latent_norm, kv_w_uk, kv_w_uv, kv_k_norm, seq)
            o, w_gu, w_dn = _mla_attention(q.reshape(bsz, seq, -1), k.reshape(bsz, seq, -1),
                                           v.reshape(bsz, seq, -1), expert_weights, layer)
            w_o = b_w_o[j]
        xm, h2, h2_slabs, logits = _post_attn(o.reshape(n, d), w_o.astype(BF16), xf, g1,
                                              row(norm2_g[layer]), sc2, sh2, moe_w_router[layer], seq)
        xf = _moe(h2, h2_slabs, xm, g2, logits, moe_router_bias[layer], w_gu, w_dn, s_gu, s_dn,
                  layer, seq)
    return xf.reshape(bsz, seq, d)
```

```python
import functools
import math

import numpy as np
import jax
import jax.numpy as jnp
from jax import lax
from jax.experimental import pallas as pl
from jax.experimental.pallas import tpu as pltpu

F32 = jnp.float32
BF16 = jnp.bfloat16
U32 = jnp.uint32

NORM_EPS = 1e-6
MASKED = -1e30
LOG2E = math.log2(math.e)

LANES = 128
SUBLANES = 8
VMEM_LIMIT = 56 * 1024 * 1024

A_HEAD_DIM = 64
DILATED_PAIRS = ((128, 1), (512, 4), (2048, 16))
QBLK = 256
NOPE_DIM = 64
ROPE_DIM = 32
QK_DIM = NOPE_DIM + ROPE_DIM
V_DIM = 64
KV_RANK = 256
ROPE_THETA = 10000.0
N_EXPERTS = 64
N_GROUPS = 8
TOPK_GROUPS = 4
TOP_K = 6
EXPERT_DIM = 256
ROUTED_SCALE = 2.5
MOE_TILE = 2048
EROWS = 256
CROWS = 256

ROW_TILE = 512


def _cparams(*sem):
    return pltpu.CompilerParams(dimension_semantics=sem, vmem_limit_bytes=VMEM_LIMIT)


def _silu(v):
    return v * jax.nn.sigmoid(v)


def _rms(v):
    return v * lax.rsqrt(jnp.mean(v * v, axis=-1, keepdims=True) + NORM_EPS)


def _dot_nt(a, b):
    return lax.dot_general(a, b, (((1,), (1,)), ((), ())), preferred_element_type=F32)


def _mod_kernel(c_ref, w_ref, b_ref, o_ref):
    cond = _silu(c_ref[...])
    o_ref[...] = jnp.dot(cond, w_ref[...], preferred_element_type=F32,
                         precision=lax.Precision.HIGHEST) + b_ref[...]


def _modulation(c, w, b, layer):
    bsz, d = c.shape
    m = w.shape[-1]
    tn = 1024
    return pl.pallas_call(
        _mod_kernel,
        grid=(m // tn,),
        in_specs=[pl.BlockSpec((bsz, d), lambda j: (0, 0)),
                  pl.BlockSpec((None, d, tn), lambda j: (layer, 0, j)),
                  pl.BlockSpec((None, 1, tn), lambda j: (layer, 0, j))],
        out_specs=pl.BlockSpec((bsz, tn), lambda j: (0, j)),
        out_shape=jax.ShapeDtypeStruct((bsz, m), F32),
        compiler_params=_cparams("arbitrary"),
    )(c, w, b.reshape(b.shape[0], 1, m))


def _qkv_kernel(x_ref, g_ref, sc_ref, sh_ref, w_ref, gq_ref, gk_ref, o_ref):
    h = _rms(x_ref[...]) * g_ref[...]
    h = h * (1.0 + sc_ref[...]) + sh_ref[...]
    acc = jnp.dot(h.astype(BF16), w_ref[...], preferred_element_type=F32)
    d = x_ref.shape[1]
    lane_lo = lax.broadcasted_iota(jnp.int32, (x_ref.shape[0], LANES), 1) < A_HEAD_DIM
    for blk in range(2 * d // LANES):
        cols = slice(blk * LANES, (blk + 1) * LANES)
        gain = gq_ref[...] * (A_HEAD_DIM ** -0.5 * LOG2E) if blk < d // LANES else gk_ref[...]
        t = acc[:, cols]
        sq = t * t
        ssa = jnp.sum(jnp.where(lane_lo, sq, 0.0), axis=-1, keepdims=True)
        ssb = jnp.sum(jnp.where(lane_lo, 0.0, sq), axis=-1, keepdims=True)
        r = jnp.where(lane_lo, lax.rsqrt(ssa / A_HEAD_DIM + NORM_EPS),
                      lax.rsqrt(ssb / A_HEAD_DIM + NORM_EPS))
        o_ref[:, cols] = (t * r * gain).astype(o_ref.dtype)
    o_ref[:, 2 * d:] = acc[:, 2 * d:].astype(o_ref.dtype)


def _qkv_proj(xf, g, sc, sh, w, q_norm, k_norm, seq):
    n, d = xf.shape
    m = w.shape[1]
    tm = min(ROW_TILE, seq)
    per_b = seq // tm
    vec = pl.BlockSpec((None, 1, d), lambda i: (i // per_b, 0, 0))
    pair_gain = lambda gn: jnp.tile(gn.reshape(1, A_HEAD_DIM), (1, LANES // A_HEAD_DIM))
    const = pl.BlockSpec((1, LANES), lambda i: (0, 0))
    return pl.pallas_call(
        _qkv_kernel,
        grid=(n // tm,),
        in_specs=[pl.BlockSpec((tm, d), lambda i: (i, 0)),
                  pl.BlockSpec((1, d), lambda i: (0, 0)),
                  vec, vec,
                  pl.BlockSpec((d, m), lambda i: (0, 0)), const, const],
        out_specs=pl.BlockSpec((tm, m), lambda i: (i, 0)),
        out_shape=jax.ShapeDtypeStruct((n, m), BF16),
        compiler_params=_cparams("parallel"),
    )(xf, g, sc, sh, w, pair_gain(q_norm), pair_gain(k_norm))


ATT_SLOTS = 4


def _attention_blocks(seq, qk_of, v_ref, bias_ref, s_ref, p_ref, ve_ref, o_ref, lane_lo,
                      diagonal_bias=False):
    nblk = seq // QBLK

    for r0 in range(0, seq, QBLK):
        v = v_ref[r0:r0 + QBLK, :].astype(F32)
        ve_ref[0, r0:r0 + QBLK, :] = jnp.where(lane_lo, v, 1.0).astype(BF16)
        ve_ref[1, r0:r0 + QBLK, :] = jnp.where(lane_lo, 1.0, v).astype(BF16)

    def slot(blk, h):
        return (blk % 2) * 2 + h

    def scores(blk):
        w = (blk + 1) * QBLK
        lo = w - QBLK if diagonal_bias else 0
        bias = bias_ref[:, seq - w + lo:seq]
        for h in range(2):
            q, k = qk_of(blk, h, w)
            s = _dot_nt(q, k)
            if lo:
                s_ref[slot(blk, h), :, 0:lo] = s[:, :lo]
            s_ref[slot(blk, h), :, lo:w] = s[:, lo:] + bias

    for blk in range(nblk):
        w = (blk + 1) * QBLK
        scores(blk)
        for h in range(2):
            s = s_ref[slot(blk, h), :, 0:w]
            p_ref[slot(blk, h), :, 0:w] = jnp.exp2(s - jnp.max(s, axis=-1, keepdims=True)).astype(BF16)
        acc_a, acc_b = [jnp.dot(p_ref[slot(blk, h), :, 0:w], ve_ref[h, 0:w, :],
                                preferred_element_type=F32) for h in range(2)]
        sums = pltpu.roll(jnp.where(lane_lo, acc_b, acc_a), LANES // 2, 1)
        o_ref[blk * QBLK:w, :] = (jnp.where(lane_lo, acc_a, acc_b) / sums).astype(o_ref.dtype)


def _attn_scratch(seq):
    return [pltpu.VMEM((ATT_SLOTS, QBLK, seq), F32), pltpu.VMEM((ATT_SLOTS, QBLK, seq), BF16),
            pltpu.VMEM((2, seq, LANES), BF16)]


def _cast_rider(weights, layer, grid):
    steps = grid[0] * grid[1]
    in_specs, out_specs, out_shapes = [], [], []
    for w in weights:
        n_exp = w.shape[1]
        assert n_exp % steps == 0
        per = n_exp // steps
        blk = (per,) + w.shape[2:]
        in_specs.append(pl.BlockSpec((None,) + blk, lambda b, h: (layer, b * grid[1] + h, 0, 0)))
        out_specs.append(pl.BlockSpec(blk, lambda b, h: (b * grid[1] + h, 0, 0)))
        out_shapes.append(jax.ShapeDtypeStruct(w.shape[1:], BF16))
    return in_specs, out_specs, out_shapes


def _split3(v):
    p1 = v.astype(BF16).astype(F32)
    p2 = (v - p1).astype(BF16).astype(F32)
    return p1, p2, v - p1 - p2


def _dilated_log_multiplicity(seq):
    a = np.arange(QBLK)[:, None]
    c = np.arange(seq)[None, :]
    d = (seq - QBLK) + a - c
    mult = np.zeros((QBLK, seq), np.int32)
    for win, dil in DILATED_PAIRS:
        mult += ((d >= 0) & (d % dil == 0) & (d <= win)).astype(np.int32)
    return np.where(mult > 0, np.log2(np.maximum(mult, 1)), MASKED).astype(np.float32)


N_FEAT = 9


def _dil_attn_kernel(q_ref, k_ref, v_ref, posc_ref, sfeat_ref, logm_ref,
                     wa_ref, wb_ref, o_ref, wa_out, wb_out, qa_ref, qb_ref, ks_ref, s_ref, p_ref,
                     ve_ref):
    wa_out[...] = wa_ref[...].astype(BF16)
    wb_out[...] = wb_ref[...].astype(BF16)
    seq = q_ref.shape[0]
    lane = lax.broadcasted_iota(jnp.int32, (QBLK, LANES), 1)
    lane_lo = lane < A_HEAD_DIM
    piece = [functools.reduce(jnp.logical_or, [lane == i + 3 * j for j in range(N_FEAT // 3)])
             for i in range(3)]

    rel0 = posc_ref[0:1, :]
    for r0 in range(0, seq, QBLK):
        rows = slice(r0, r0 + QBLK)
        qn = q_ref[rows, :].astype(F32)
        qa_ref[rows, :LANES] = jnp.where(lane_lo, qn, 0.0).astype(BF16)
        qb_ref[rows, :LANES] = jnp.where(lane_lo, 0.0, qn).astype(BF16)
        qa_ref[rows, LANES:] = jnp.broadcast_to(sfeat_ref[0:1, :], (QBLK, LANES)).astype(BF16)
        qb_ref[rows, LANES:] = jnp.broadcast_to(sfeat_ref[1:2, :], (QBLK, LANES)).astype(BF16)
        ks_ref[rows, :LANES] = k_ref[rows, :]
        p1, p2, p3 = _split3((posc_ref[rows, :] - rel0).astype(F32))
        feat = jnp.where(piece[0], p1, jnp.where(piece[1], p2, jnp.where(piece[2], p3, 0.0)))
        ks_ref[rows, LANES:] = feat.astype(BF16)

    def qk_of(blk, h, w):
        return (qa_ref, qb_ref)[h][w - QBLK:w, :], ks_ref[0:w, :]

    _attention_blocks(seq, qk_of, v_ref, logm_ref, s_ref, p_ref, ve_ref, o_ref, lane_lo)


def _dilated_attention(qkv, pos, cast_weights, layer):
    bsz, seq, three_d = qkv.shape
    d = three_d // 3
    n_pairs = d // LANES
    n_heads = d // A_HEAD_DIM
    slopes = jnp.exp2(-8.0 * jnp.arange(1, n_heads + 1, dtype=F32) / n_heads) * LOG2E
    sfeat = jnp.repeat(jnp.stack(_split3(slopes), axis=1), 3, axis=1)
    sfeat = jnp.pad(sfeat, ((0, 0), (0, LANES - N_FEAT))).reshape(n_pairs, 2, LANES)
    logm = jnp.asarray(_dilated_log_multiplicity(seq))
    blk = lambda off: pl.BlockSpec((None, seq, LANES), lambda b, h: (b, 0, off + h))
    ext = pltpu.VMEM((seq, 2 * LANES), BF16)
    grid = (bsz, n_pairs)
    w_in, w_out, w_shapes = _cast_rider(cast_weights, layer, grid)
    return pl.pallas_call(
        _dil_attn_kernel,
        grid=grid,
        in_specs=[blk(0), blk(n_pairs), blk(2 * n_pairs),
                  pl.BlockSpec((None, seq, 1), lambda b, h: (b, 0, 0)),
                  pl.BlockSpec((None, 2, LANES), lambda b, h: (h, 0, 0)),
                  pl.BlockSpec((QBLK, seq), lambda b, h: (0, 0))] + w_in,
        out_specs=[pl.BlockSpec((None, seq, LANES), lambda b, h: (b, 0, h))] + w_out,
        scratch_shapes=[ext, ext, ext] + _attn_scratch(seq),
        out_shape=[jax.ShapeDtypeStruct((bsz, seq, d), BF16)] + w_shapes,
        compiler_params=_cparams("parallel", "parallel"),
    )(qkv, qkv, qkv, pos.reshape(bsz, seq, 1), sfeat, logm, *cast_weights)


def _post_attn_kernel(o_ref, wo_ref, x_ref, g1_ref, n2_ref, sc_ref, sh_ref, wrh_ref, wrl_ref,
                      xm_ref, h_ref, hs_ref, lg_ref, il_ref):
    y = jnp.dot(o_ref[...], wo_ref[...], preferred_element_type=F32)
    xm = x_ref[...] + g1_ref[...] * y
    xm_ref[...] = xm
    h = _rms(xm) * n2_ref[...]
    h = h * (1.0 + sc_ref[...]) + sh_ref[...]
    h_ref[...] = h.astype(h_ref.dtype)
    hs_ref[...] = _rows_to_slab_rows(h, il_ref)
    h_hi = h.astype(BF16)
    h_lo = (h - h_hi.astype(F32)).astype(BF16)
    lg_ref[...] = (_dot_nt(wrh_ref[...], h_hi) + _dot_nt(wrh_ref[...], h_lo)
                   + _dot_nt(wrl_ref[...], h_hi))


def _post_attn(o, w_o, xf, g1, n2, sc2, sh2, w_router, seq):
    n, d = xf.shape
    ne = w_router.shape[1]
    tm = min(ROW_TILE, seq)
    per_b = seq // tm
    slab_rows = d // LANES
    wr_t = w_router.T.astype(F32)
    wr_hi = wr_t.astype(BF16)
    vec = pl.BlockSpec((None, 1, d), lambda i: (i // per_b, 0, 0))
    row = lambda w: pl.BlockSpec((tm, w), lambda i: (i, 0))
    return pl.pallas_call(
        _post_attn_kernel,
        grid=(n // tm,),
        in_specs=[row(d), pl.BlockSpec((d, d), lambda i: (0, 0)), row(d), vec,
                  pl.BlockSpec((1, d), lambda i: (0, 0)), vec, vec,
                  pl.BlockSpec((ne, d), lambda i: (0, 0)), pl.BlockSpec((ne, d), lambda i: (0, 0))],
        out_specs=[row(d), row(d), pl.BlockSpec((tm * slab_rows, LANES), lambda i: (i, 0)),
                   pl.BlockSpec((ne, tm), lambda i: (0, i))],
        out_shape=[jax.ShapeDtypeStruct((n, d), F32), jax.ShapeDtypeStruct((n, d), BF16),
                   jax.ShapeDtypeStruct((n * slab_rows, LANES), BF16),
                   jax.ShapeDtypeStruct((ne, n), F32)],
        scratch_shapes=[pltpu.VMEM((tm * slab_rows, LANES), F32)],
        compiler_params=_cparams("parallel"),
    )(o, w_o, xf, g1, n2, sc2, sh2, wr_hi, (wr_t - wr_hi.astype(F32)).astype(BF16))


def _moe_rows(tile):
    rows = tile * TOP_K + N_EXPERTS * (SUBLANES - 1) + EROWS
    return -(-rows // SUBLANES) * SUBLANES


def _first_index(hit, index, limit):
    return jnp.min(jnp.where(hit, index, limit), axis=0, keepdims=True)


def _route_kernel(lg_ref, bias_ref, tri_ref, dest_ref, w_ref, seg_ref, cnt_ref, *, row_pitch):
    t = lg_ref.shape[1]
    per_g = N_EXPERTS // N_GROUPS
    sub = lax.broadcasted_iota(jnp.int32, (per_g, t), 0).astype(F32)
    neg = -jnp.inf
    scores, sel = [], []
    for g in range(N_GROUPS):
        rows = slice(g * per_g, (g + 1) * per_g)
        sc = jax.nn.sigmoid(lg_ref[rows, :])
        scores.append(sc)
        sel.append(sc + bias_ref[rows, :])

    gs = []
    for g in range(N_GROUPS):
        m1 = jnp.max(sel[g], axis=0, keepdims=True)
        first = _first_index(sel[g] == m1, sub, float(per_g))
        m2 = jnp.max(jnp.where(sub == first, neg, sel[g]), axis=0, keepdims=True)
        gs.append(m1 + m2)
    gsc = jnp.concatenate(gs, axis=0)
    gsub = lax.broadcasted_iota(jnp.int32, (N_GROUPS, t), 0).astype(F32)
    gkeep = jnp.zeros((N_GROUPS, t), F32)
    for _ in range(TOPK_GROUPS):
        m = jnp.max(gsc, axis=0, keepdims=True)
        pick = gsub == _first_index(gsc == m, gsub, float(N_GROUPS))
        gkeep = jnp.where(pick, 1.0, gkeep)
        gsc = jnp.where(pick, neg, gsc)

    eidx = [sub + float(g * per_g) for g in range(N_GROUPS)]
    masked = [jnp.where(jnp.broadcast_to(gkeep[g:g + 1, :], (per_g, t)) > 0.5, sel[g], neg)
              for g in range(N_GROUPS)]
    chosen = [jnp.zeros((per_g, t), F32) for _ in range(N_GROUPS)]
    wsel = [jnp.zeros((per_g, t), F32) for _ in range(N_GROUPS)]
    picked = []
    for _ in range(TOP_K):
        m = jnp.max(functools.reduce(jnp.maximum, masked), axis=0, keepdims=True)
        cand = functools.reduce(jnp.minimum, [jnp.where(masked[g] == m, eidx[g], float(N_EXPERTS))
                                              for g in range(N_GROUPS)])
        e_first = jnp.min(cand, axis=0, keepdims=True)
        picked.append(e_first)
        for g in range(N_GROUPS):
            pick = eidx[g] == e_first
            chosen[g] = jnp.where(pick, 1.0, chosen[g])
            wsel[g] = jnp.where(pick, scores[g], wsel[g])
            masked[g] = jnp.where(pick, neg, masked[g])

    denom = jnp.sum(functools.reduce(jnp.add, wsel), axis=0, keepdims=True)

    c_all = jnp.concatenate(chosen, axis=0)
    nblk = t // LANES
    stacked = jnp.concatenate([c_all[:, b * LANES:(b + 1) * LANES] for b in range(nblk)], axis=0)
    within = jnp.dot(stacked.astype(BF16), tri_ref[...], preferred_element_type=F32)
    ranks, before = [], jnp.zeros((N_EXPERTS, 1), F32)
    for b in range(nblk):
        rows = slice(b * N_EXPERTS, (b + 1) * N_EXPERTS)
        ranks.append(within[rows, :] + before)
        before = before + within[rows, LANES - 1:LANES] + stacked[rows, LANES - 1:LANES]
    rank = jnp.concatenate(ranks, axis=1)
    counts = before
    padded = jnp.ceil(counts / SUBLANES) * SUBLANES
    below = (lax.broadcasted_iota(jnp.int32, (N_EXPERTS, N_EXPERTS), 1)
             < lax.broadcasted_iota(jnp.int32, (N_EXPERTS, N_EXPERTS), 0)).astype(F32)
    seg = jnp.dot(below, jnp.broadcast_to(padded, (N_EXPERTS, LANES)), preferred_element_type=F32,
                  precision=lax.Precision.HIGHEST)
    row_of = seg[:, :1] + rank

    dest_rows, w_rows = [], []
    for k in range(TOP_K):
        d_k = jnp.zeros((per_g, t), F32)
        w_k = jnp.zeros((per_g, t), F32)
        for g in range(N_GROUPS):
            hit = eidx[g] == picked[k]
            d_k = jnp.where(hit, row_of[g * per_g:(g + 1) * per_g, :], d_k)
            w_k = jnp.where(hit, wsel[g], w_k)
        dest_rows.append(jnp.sum(d_k, axis=0, keepdims=True))
        w_rows.append(jnp.sum(w_k, axis=0, keepdims=True) / denom * ROUTED_SCALE)
    fill = [jnp.zeros((SUBLANES - TOP_K, t), F32)]
    dest8 = (jnp.concatenate(dest_rows + fill, axis=0) * row_pitch).astype(jnp.int32)
    w8 = jnp.concatenate(w_rows + fill, axis=0)
    for j in range(t // LANES):
        dest_ref[j] = dest8[:, j * LANES:(j + 1) * LANES]
        w_ref[j] = w8[:, j * LANES:(j + 1) * LANES]
    seg_ref[...] = seg.astype(jnp.int32)
    cnt_ref[...] = jnp.broadcast_to(counts, (N_EXPERTS, LANES)).astype(jnp.int32)


def _route(logits_t, router_bias, tile, row_pitch):
    ne, n = logits_t.shape
    nt = n // tile
    tri = jnp.asarray(np.triu(np.ones((LANES, LANES), np.float32), 1), dtype=BF16)
    per_tile = lambda rows, width: pl.BlockSpec((None, rows, width), lambda i: (i, 0, 0))
    table = pl.BlockSpec((None, tile // LANES, SUBLANES, LANES), lambda i: (i, 0, 0, 0))
    dest, wts, seg, cnt = pl.pallas_call(
        functools.partial(_route_kernel, row_pitch=row_pitch),
        grid=(nt,),
        in_specs=[pl.BlockSpec((ne, tile), lambda i: (0, i)),
                  pl.BlockSpec((ne, 1), lambda i: (0, 0)),
                  pl.BlockSpec((LANES, LANES), lambda i: (0, 0))],
        out_specs=[table, table, per_tile(ne, LANES), per_tile(ne, LANES)],
        out_shape=[jax.ShapeDtypeStruct((nt, tile // LANES, SUBLANES, LANES), jnp.int32),
                   jax.ShapeDtypeStruct((nt, tile // LANES, SUBLANES, LANES), F32),
                   jax.ShapeDtypeStruct((nt, ne, LANES), jnp.int32),
                   jax.ShapeDtypeStruct((nt, ne, LANES), jnp.int32)],
        compiler_params=_cparams("parallel"),
    )(logits_t, router_bias.astype(F32).reshape(ne, 1), tri)
    return seg[:, :, 0].reshape(-1), cnt[:, :, 0].reshape(-1), dest, wts


def _swiglu(xb, wgu_ref, wdn_ref):
    gu = jnp.dot(xb, wgu_ref[...], preferred_element_type=F32)
    half = wgu_ref.shape[1] // 2
    act = _silu(gu[:, :half]) * gu[:, half:]
    return jnp.dot(act.astype(BF16), wdn_ref[...], preferred_element_type=F32)


def _slabs_to_rows(p, il_ref, r):
    n = il_ref.shape[0] // r
    il_ref[...] = pltpu.bitcast(p, BF16).astype(F32)
    return jnp.concatenate([il_ref[pl.ds(2 * c + h, r, stride=n), :]
                            for h in range(2) for c in range(n // 2)], axis=1)


def _rows_to_slab_rows(v, il_ref):
    r = v.shape[0]
    n = il_ref.shape[0] // r
    for h in range(2):
        for c in range(n // 2):
            lane0 = (h * (n // 2) + c) * LANES
            il_ref[pl.ds(2 * c + h, r, stride=n), :] = v[:, lane0:lane0 + LANES]
    return il_ref[...].astype(BF16)


def _rows_to_slabs(v, il_ref):
    return pltpu.bitcast(_rows_to_slab_rows(v, il_ref), U32)


def _moe_kernel(start_ref, count_ref, hp_ref, hb_ref, dest_ref, wrow_ref, wgu_hbm, wdn_hbm,
                sgu_ref, sdn_ref, xm_ref, g2_ref, o_ref, xy_ref, il_ref, yp_ref, zf_ref,
                comb_ref, wgu_buf, wdn_buf, wsem):
    t = pl.program_id(0)
    s = pl.program_id(1)
    slab = xm_ref.shape[1] // (2 * LANES)
    tile = hp_ref.shape[0] // (2 * slab)
    group_rows = SUBLANES * slab

    def weight_copies(e, slot):
        return (pltpu.make_async_copy(wgu_hbm.at[e], wgu_buf.at[slot], wsem.at[0, slot]),
                pltpu.make_async_copy(wdn_hbm.at[e], wdn_buf.at[slot], wsem.at[1, slot]))

    def slab_at(ref, first):
        return ref.at[pl.ds(pl.multiple_of(first, slab), slab), :]

    @pl.when(s == 0)
    def _first_weights():
        for p in range(2):
            for cp in weight_copies(p, p):
                cp.start()

    @pl.when(jnp.logical_and(t == 0, s == 0))
    def _clear():
        xy_ref[...] = jnp.zeros(xy_ref.shape, U32)

    @pl.when(s == 0)
    def _dispatch():
        def group(j, carry):
            for sub in range(LANES // SUBLANES):
                p0 = pl.multiple_of((j * (LANES // SUBLANES) + sub) * 2 * group_rows, 2 * group_rows)
                rows = pltpu.bitcast(hp_ref[pl.ds(p0, 2 * group_rows), :], U32)
                for u in range(SUBLANES):
                    for k in range(TOP_K):
                        slab_at(xy_ref, dest_ref[j, k, sub * SUBLANES + u])[...] = (
                            rows[u * slab:(u + 1) * slab, :])
            return carry

        lax.fori_loop(0, tile // LANES, group, 0)

    @pl.when(s == 0)
    def _experts():
        def chunk_base(e, c):
            return pl.multiple_of((start_ref[t * N_EXPERTS + e] + c * EROWS) * slab, group_rows)

        def load_rows(base, p):
            return _slabs_to_rows(xy_ref[pl.ds(base, EROWS * slab), :], il_ref.at[p], EROWS).astype(BF16)

        def put_rows(p, base, valid):
            def put(j, c2):
                r0 = pl.multiple_of(j * group_rows, group_rows)
                xy_ref[pl.ds(base + r0, group_rows), :] = yp_ref[p, pl.ds(r0, group_rows), :]
                return c2

            lax.fori_loop(0, (jnp.clip(valid, 0, EROWS) + SUBLANES - 1) // SUBLANES, put, 0)

        def expert_pair(i, carry0):
            wslot = lax.rem(i, 2) * 2

            @pl.when(i + 1 < N_EXPERTS // 2)
            def _prefetch():
                for p in range(2):
                    for cp in weight_copies(2 * i + 2 + p, 2 - wslot + p):
                        cp.start()

            for p in range(2):
                for cp in weight_copies(2 * i + p, wslot + p):
                    cp.wait()
            counts = [count_ref[t * N_EXPERTS + 2 * i + p] for p in range(2)]
            bases = [chunk_base(2 * i + p, 0) for p in range(2)]
            xs = [load_rows(bases[p], p) for p in range(2)]
            ys = [_swiglu(xs[p], wgu_buf.at[wslot + p], wdn_buf.at[wslot + p]) for p in range(2)]
            for p in range(2):
                yp_ref[p] = _rows_to_slabs(ys[p], il_ref.at[p])
            for p in range(2):
                put_rows(p, bases[p], counts[p])

            for p in range(2):
                def more(c, carry, p=p):
                    base = chunk_base(2 * i + p, c)
                    y = _swiglu(load_rows(base, p), wgu_buf.at[wslot + p], wdn_buf.at[wslot + p])
                    yp_ref[p] = _rows_to_slabs(y, il_ref.at[p])
                    put_rows(p, base, counts[p] - c * EROWS)
                    return carry

                lax.fori_loop(1, (counts[p] + EROWS - 1) // EROWS, more, 0)
            return carry0

        lax.fori_loop(0, N_EXPERTS // 2, expert_pair, 0)

    @pl.when(s >= 1)
    def _combine():
        sb = s - 1

        def group(j, carry):
            tj = sb * (CROWS // LANES) + j
            for sub in range(LANES // SUBLANES):
                r0 = pl.multiple_of(j * LANES + sub * SUBLANES, SUBLANES)
                half = sub % 2
                for u in range(SUBLANES):
                    lane = sub * SUBLANES + u
                    acc = None
                    for k in range(TOP_K):
                        z = pltpu.bitcast(slab_at(xy_ref, dest_ref[tj, k, lane])[...], BF16).astype(F32)
                        acc = wrow_ref[tj, k, lane] * z if acc is None else acc + wrow_ref[tj, k, lane] * z
                    zf_ref[half, u * 2 * slab:(u + 1) * 2 * slab, :] = acc
                comb_ref[pl.ds(r0, SUBLANES), :] = jnp.concatenate(
                    [zf_ref[half, pl.ds(2 * c + h, SUBLANES, stride=2 * slab), :]
                     for h in range(2) for c in range(slab)], axis=1)
            return carry

        lax.fori_loop(0, CROWS // LANES, group, 0)
        shared = _swiglu(hb_ref[...], sgu_ref, sdn_ref)
        o_ref[...] = xm_ref[...] + g2_ref[...] * (comb_ref[...] + shared)


def _moe(h, h_slabs, xm, g2, logits_t, router_bias, w_gu, w_dn, s_gu, s_dn, layer, seq):
    n, d = h.shape
    slab = d // (2 * LANES)
    tile = min(MOE_TILE, seq)
    nt = n // tile
    per_b = seq // tile
    csteps = tile // CROWS
    assert seq % tile == 0 and tile % CROWS == 0
    assert tile % LANES == 0 and CROWS % LANES == 0
    seg_start, seg_count, dest_rows, w_rows = _route(logits_t, router_bias, tile, slab)
    rows = _moe_rows(tile)
    table = pl.BlockSpec((None, tile // LANES, SUBLANES, LANES), lambda t, s, a, b: (t, 0, 0, 0),
                         memory_space=pltpu.SMEM)
    cst = lambda s: jnp.maximum(s - 1, 0)
    crow = pl.BlockSpec((CROWS, d), lambda t, s, a, b: (t * csteps + cst(s), 0))
    hbm = pl.BlockSpec(memory_space=pl.ANY)
    single = pl.Buffered(1)
    return pl.pallas_call(
        _moe_kernel,
        grid_spec=pltpu.PrefetchScalarGridSpec(
            num_scalar_prefetch=2,
            grid=(nt, 1 + csteps),
            in_specs=[pl.BlockSpec((tile * 2 * slab, LANES), lambda t, s, a, b: (t, 0),
                                   pipeline_mode=single),
                      crow, table, table,
                      hbm, hbm,
                      pl.BlockSpec((None,) + s_gu.shape[1:], lambda t, s, a, b: (layer, 0, 0),
                                   pipeline_mode=single),
                      pl.BlockSpec((None,) + s_dn.shape[1:], lambda t, s, a, b: (layer, 0, 0),
                                   pipeline_mode=single),
                      crow,
                      pl.BlockSpec((None, 1, d), lambda t, s, a, b: (t // per_b, 0, 0))],
            out_specs=crow,
            scratch_shapes=[pltpu.VMEM((rows * slab, LANES), U32),
                            pltpu.VMEM((2, EROWS * 2 * slab, LANES), F32),
                            pltpu.VMEM((2, EROWS * slab, LANES), U32),
                            pltpu.VMEM((2, SUBLANES * 2 * slab, LANES), F32),
                            pltpu.VMEM((CROWS, d), F32),
                            pltpu.VMEM((4, d, 2 * EXPERT_DIM), BF16),
                            pltpu.VMEM((4, EXPERT_DIM, d), BF16),
                            pltpu.SemaphoreType.DMA((2, 4))]),
        out_shape=jax.ShapeDtypeStruct((n, d), F32),
        compiler_params=_cparams("arbitrary", "arbitrary"),
    )(seg_start, seg_count, h_slabs, h, dest_rows, w_rows, w_gu, w_dn, s_gu, s_dn, xm, g2)


def _mla_prep_kernel(x_ref, n1_ref, sc1_ref, sh1_ref, nk_ref, sck_ref, shk_ref,
                     wdq_ref, gql_ref, wuq_ref, wuqs_ref, wdown_ref, gkl_ref, wk_ref, gk_ref, wuv_ref,
                     qa_ref, qb_ref, kc_ref, ksa_ref, ksb_ref, q_ref, k_ref, v_ref):
    xn = _rms(x_ref[...])
    hq = xn * n1_ref[...] * (1.0 + sc1_ref[...]) + sh1_ref[...]
    hk = xn * nk_ref[...] * (1.0 + sck_ref[...]) + shk_ref[...]
    n_heads = q_ref.shape[1] // LANES
    half = ROPE_DIM // 2

    cq = (_rms(jnp.dot(hq.astype(BF16), wdq_ref[...], preferred_element_type=F32)) * gql_ref[...]).astype(BF16)
    q_raw = jnp.dot(cq, wuq_ref[...], preferred_element_type=F32)
    q_swp = jnp.dot(cq, wuqs_ref[...], preferred_element_type=F32)
    qa, qb = qa_ref[...], qb_ref[...]
    for h in range(n_heads):
        sl = slice(h * LANES, (h + 1) * LANES)
        t = q_raw[:, sl]
        r = lax.rsqrt(jnp.sum(t * t, axis=-1, keepdims=True) / QK_DIM + NORM_EPS)
        q_ref[:, sl] = (r * (t * qa + q_swp[:, sl] * qb)).astype(q_ref.dtype)

    down = jnp.dot(hk.astype(BF16), wdown_ref[...], preferred_element_type=F32)
    ckv = (_rms(down[:, :KV_RANK]) * gkl_ref[...]).astype(BF16)
    pe = down[:, KV_RANK:]
    ss_pe = jnp.sum(pe * pe, axis=-1, keepdims=True)
    rot = (pe * kc_ref[...] + pltpu.roll(pe, LANES - half, 1) * ksa_ref[...]
           + pltpu.roll(pe, half, 1) * ksb_ref[...])
    rot_hi = rot.astype(BF16)
    rot_lo = (rot - rot_hi.astype(F32)).astype(BF16)
    k_raw = jnp.dot(jnp.concatenate([ckv, rot_hi, rot_lo], axis=1), wk_ref[...],
                    preferred_element_type=F32)
    nope = lax.broadcasted_iota(jnp.int32, (x_ref.shape[0], LANES), 1) < NOPE_DIM
    for h in range(n_heads):
        sl = slice(h * LANES, (h + 1) * LANES)
        t = k_raw[:, sl]
        ss = jnp.sum(jnp.where(nope, t * t, 0.0), axis=-1, keepdims=True) + ss_pe
        k_ref[:, sl] = (t * lax.rsqrt(ss / QK_DIM + NORM_EPS) * gk_ref[...]).astype(k_ref.dtype)
    v_ref[...] = jnp.dot(ckv, wuv_ref[...], preferred_element_type=F32).astype(v_ref.dtype)


def _pad_heads(w, n_heads, width):
    r = w.shape[0]
    return jnp.pad(w.reshape(r, n_heads, width), ((0, 0), (0, 0), (0, LANES - width))).reshape(r, n_heads * LANES)


def _swap_rope_halves(a):
    half = ROPE_DIM // 2
    lead = a.shape[:-1]
    return jnp.concatenate([jnp.zeros(lead + (NOPE_DIM,), a.dtype), a[..., NOPE_DIM + half:QK_DIM],
                            a[..., NOPE_DIM:NOPE_DIM + half],
                            jnp.zeros(lead + (LANES - QK_DIM,), a.dtype)], axis=-1)


def _mla_prep(xf, pos, n1, sc1, sh1, nk, sck, shk, w_dq, g_ql, w_uq, g_q, w_down, g_kl, w_uk, w_uv,
              g_k, seq):
    n, d = xf.shape
    n_heads = w_uv.shape[1] // V_DIM
    q_rank = w_dq.shape[1]
    hw = n_heads * LANES
    tm = min(ROW_TILE, seq)
    per_b = seq // tm

    half = ROPE_DIM // 2
    inv = jnp.power(ROPE_THETA, -jnp.arange(half, dtype=F32) / half)
    ang = pos.astype(F32).reshape(n, 1) * inv
    cs = jnp.concatenate([jnp.cos(ang), jnp.sin(ang)], axis=1)
    tail = LANES - QK_DIM
    q_mul = QK_DIM ** -0.5 * LOG2E
    j = np.arange(half)
    sel = np.zeros((ROPE_DIM, 5, LANES), np.float32)
    sel[j, 0, NOPE_DIM + j] = sel[j, 0, NOPE_DIM + half + j] = 1.0
    sel[half + j, 1, NOPE_DIM + j], sel[half + j, 1, NOPE_DIM + half + j] = -1.0, 1.0
    sel[j, 2, j] = sel[j, 2, half + j] = 1.0
    sel[half + j, 3, j] = -1.0
    sel[half + j, 4, half + j] = 1.0
    gq_pad = jnp.pad(g_q, (0, tail)) * q_mul
    gq_swp = _swap_rope_halves(g_q.reshape(1, QK_DIM))[0] * q_mul
    gk1, gk2 = g_k[NOPE_DIM:NOPE_DIM + half], g_k[NOPE_DIM + half:]
    kpad = jnp.zeros((LANES - ROPE_DIM,), F32)
    k_own, k_swp = jnp.concatenate([gk1, gk2, kpad]), jnp.concatenate([gk2, gk1, kpad])
    place = (jnp.asarray(sel) * jnp.stack([gq_pad, gq_swp, k_own, k_swp, k_swp])).reshape(ROPE_DIM, -1)
    const = jnp.pad(gq_pad[:NOPE_DIM], (0, 5 * LANES - NOPE_DIM))
    tables = jnp.dot(cs, place, precision=lax.Precision.HIGHEST) + const
    gk_mask = jnp.concatenate([g_k[:NOPE_DIM], jnp.ones((ROPE_DIM,), F32),
                               jnp.zeros((tail,), F32)]).reshape(1, LANES)

    wuq_p = _pad_heads(w_uq, n_heads, QK_DIM).astype(BF16)
    wuq_s = _swap_rope_halves(w_uq.reshape(q_rank, n_heads, QK_DIM)).reshape(q_rank, hw).astype(BF16)
    wdown_p = jnp.pad(w_down, ((0, 0), (0, LANES - ROPE_DIM))).astype(BF16)
    eye = jnp.pad(jnp.eye(ROPE_DIM, dtype=F32), ((0, LANES - ROPE_DIM), (NOPE_DIM, tail)))
    pe_rows = jnp.tile(eye, (1, n_heads))
    wk = jnp.concatenate([_pad_heads(w_uk, n_heads, NOPE_DIM), pe_rows, pe_rows], axis=0).astype(BF16)

    vec = pl.BlockSpec((None, 1, d), lambda i: (i // per_b, 0, 0))
    full = lambda a: pl.BlockSpec(a.shape, lambda i: (0,) * a.ndim)
    row = lambda wdt: pl.BlockSpec((tm, wdt), lambda i: (i, 0))
    args = [xf, n1, sc1, sh1, nk, sck, shk,
            w_dq.astype(BF16), g_ql.reshape(1, q_rank), wuq_p, wuq_s,
            wdown_p, g_kl.reshape(1, KV_RANK), wk, gk_mask, w_uv.astype(BF16),
            tables, tables, tables, tables, tables]
    table_specs = [pl.BlockSpec((tm, LANES), lambda i, c=c: (i, c)) for c in range(5)]
    specs = [row(d), full(n1), vec, vec, full(nk), vec, vec] + [full(a) for a in args[7:16]] + table_specs
    return pl.pallas_call(
        _mla_prep_kernel,
        grid=(n // tm,),
        in_specs=specs,
        out_specs=[row(hw), row(hw), row(n_heads * V_DIM)],
        out_shape=[jax.ShapeDtypeStruct((n, hw), BF16), jax.ShapeDtypeStruct((n, hw), BF16),
                   jax.ShapeDtypeStruct((n, n_heads * V_DIM), BF16)],
        compiler_params=_cparams("parallel"),
    )(*args)


def _causal_bias(seq):
    a = np.arange(QBLK)[:, None]
    c = np.arange(seq)[None, :]
    return np.where(c <= (seq - QBLK) + a, 0.0, MASKED).astype(np.float32)


def _mla_attn_kernel(q_ref, k_ref, v_ref, mask_ref, wa_ref, wb_ref, o_ref, wa_out, wb_out,
                     s_ref, p_ref, ve_ref):
    wa_out[...] = wa_ref[...].astype(BF16)
    wb_out[...] = wb_ref[...].astype(BF16)
    seq = q_ref.shape[0]
    lane_lo = lax.broadcasted_iota(jnp.int32, (QBLK, LANES), 1) < V_DIM

    def qk_of(blk, h, w):
        lanes = slice(h * LANES, (h + 1) * LANES)
        return q_ref[w - QBLK:w, lanes], k_ref[0:w, lanes]

    _attention_blocks(seq, qk_of, v_ref, mask_ref, s_ref, p_ref, ve_ref, o_ref, lane_lo,
                      diagonal_bias=True)


def _mla_attention(q, k, v, cast_weights, layer):
    bsz, seq, hw = q.shape
    n_pairs = hw // (2 * LANES)
    qk = pl.BlockSpec((None, seq, 2 * LANES), lambda b, h: (b, 0, h))
    vo = pl.BlockSpec((None, seq, LANES), lambda b, h: (b, 0, h))
    grid = (bsz, n_pairs)
    w_in, w_out, w_shapes = _cast_rider(cast_weights, layer, grid)
    return pl.pallas_call(
        _mla_attn_kernel,
        grid=grid,
        in_specs=[qk, qk, vo, pl.BlockSpec((QBLK, seq), lambda b, h: (0, 0))] + w_in,
        out_specs=[vo] + w_out,
        scratch_shapes=_attn_scratch(seq),
        out_shape=[jax.ShapeDtypeStruct(v.shape, BF16)] + w_shapes,
        compiler_params=_cparams("parallel", "parallel"),
    )(q, k, v, jnp.asarray(_causal_bias(seq)), *cast_weights)


def kernel(x, c, pos, ada_w, ada_b, norm1_g, norm2_g, a_w_qkv, a_q_norm, a_k_norm, a_w_o, kv_ada_w, kv_ada_b, kv_norm_g, kv_w_down, kv_latent_norm, kv_w_uk, kv_w_uv, kv_k_norm, b_w_dq, b_q_latent_norm, b_w_uq, b_q_norm, b_w_o, moe_w_router, moe_router_bias, moe_w_gate_up, moe_w_down, moe_shared_gate_up, moe_shared_down):
    bsz, seq, d = x.shape
    n = bsz * seq
    depth = ada_w.shape[0]
    n_a = a_w_qkv.shape[0]
    assert seq % QBLK == 0
    xf = x.reshape(n, d)

    expert_weights = (moe_w_gate_up, moe_w_down)
    s_gu = moe_shared_gate_up.astype(BF16)
    s_dn = moe_shared_down.astype(BF16)
    row = lambda g: g.reshape(1, d)

    for layer in range(depth):
        mod = _modulation(c, ada_w, ada_b, layer).reshape(bsz, 6, 1, d)
        sh1, sc1, g1, sh2, sc2, g2 = [mod[:, i] for i in range(6)]
        if layer < n_a:
            qkv = _qkv_proj(xf, row(norm1_g[layer]), sc1, sh1, a_w_qkv[layer].astype(BF16),
                            a_q_norm[layer], a_k_norm[layer], seq)
            o, w_gu, w_dn = _dilated_attention(qkv.reshape(bsz, seq, -1), pos, expert_weights, layer)
            w_o = a_w_o[layer]
        else:
            j = layer - n_a
            kmod = _modulation(c, kv_ada_w[None], kv_ada_b[None], 0).reshape(bsz, 2, 1, d)
            q, k, v = _mla_prep(xf, pos, row(norm1_g[layer]), sc1, sh1, row(kv_norm_g), kmod[:, 1],
                                kmod[:, 0], b_w_dq[j], b_q_latent_norm[j], b_w_uq[j], b_q_norm[j],
                                kv_w_down, kv_latent_norm, kv_w_uk, kv_w_uv, kv_k_norm, seq)
            o, w_gu, w_dn = _mla_attention(q.reshape(bsz, seq, -1), k.reshape(bsz, seq, -1),
                                           v.reshape(bsz, seq, -1), expert_weights, layer)
            w_o = b_w_o[j]
        xm, h2, h2_slabs, logits = _post_attn(o.reshape(n, d), w_o.astype(BF16), xf, g1,
                                              row(norm2_g[layer]), sc2, sh2, moe_w_router[layer], seq)
        xf = _moe(h2, h2_slabs, xm, g2, logits, moe_router_bias[layer], w_gu, w_dn, s_gu, s_dn,
                  layer, seq)
    return xf.reshape(bsz, seq, d)
```

```python
import functools
import math

import numpy as np
import jax
import jax.numpy as jnp
from jax import lax
from jax.experimental import pallas as pl
from jax.experimental.pallas import tpu as pltpu

F32 = jnp.float32
BF16 = jnp.bfloat16
U32 = jnp.uint32

NORM_EPS = 1e-6
MASKED = -1e30
LOG2E = math.log2(math.e)

LANES = 128
SUBLANES = 8
VMEM_LIMIT = 56 * 1024 * 1024

A_HEAD_DIM = 64
DILATED_PAIRS = ((128, 1), (512, 4), (2048, 16))
QBLK = 256
NOPE_DIM = 64
ROPE_DIM = 32
QK_DIM = NOPE_DIM + ROPE_DIM
V_DIM = 64
KV_RANK = 256
ROPE_THETA = 10000.0
N_EXPERTS = 64
N_GROUPS = 8
TOPK_GROUPS = 4
TOP_K = 6
EXPERT_DIM = 256
ROUTED_SCALE = 2.5
MOE_TILE = 2048
EROWS = 256
CROWS = 256
EXPERT_GROUP = 4

ROW_TILE = 512


def _cparams(*sem):
    return pltpu.CompilerParams(dimension_semantics=sem, vmem_limit_bytes=VMEM_LIMIT)


def _silu(v):
    return v * jax.nn.sigmoid(v)


def _rms(v):
    return v * lax.rsqrt(jnp.mean(v * v, axis=-1, keepdims=True) + NORM_EPS)


def _dot_nt(a, b):
    return lax.dot_general(a, b, (((1,), (1,)), ((), ())), preferred_element_type=F32)


def _mod_kernel(c_ref, w_ref, b_ref, o_ref):
    cond = _silu(c_ref[...])
    o_ref[...] = jnp.dot(cond, w_ref[...], preferred_element_type=F32,
                         precision=lax.Precision.HIGHEST) + b_ref[...]


def _modulation(c, w, b, layer):
    bsz, d = c.shape
    m = w.shape[-1]
    tn = 1024
    return pl.pallas_call(
        _mod_kernel,
        grid=(m // tn,),
        in_specs=[pl.BlockSpec((bsz, d), lambda j: (0, 0)),
                  pl.BlockSpec((None, d, tn), lambda j: (layer, 0, j)),
                  pl.BlockSpec((None, 1, tn), lambda j: (layer, 0, j))],
        out_specs=pl.BlockSpec((bsz, tn), lambda j: (0, j)),
        out_shape=jax.ShapeDtypeStruct((bsz, m), F32),
        compiler_params=_cparams("arbitrary"),
    )(c, w, b.reshape(b.shape[0], 1, m))


def _qkv_kernel(x_ref, g_ref, sc_ref, sh_ref, w_ref, gq_ref, gk_ref, o_ref):
    h = _rms(x_ref[...]) * g_ref[...]
    h = h * (1.0 + sc_ref[...]) + sh_ref[...]
    acc = jnp.dot(h.astype(BF16), w_ref[...], preferred_element_type=F32)
    d = x_ref.shape[1]
    lane_lo = lax.broadcasted_iota(jnp.int32, (x_ref.shape[0], LANES), 1) < A_HEAD_DIM
    for blk in range(2 * d // LANES):
        cols = slice(blk * LANES, (blk + 1) * LANES)
        gain = gq_ref[...] * (A_HEAD_DIM ** -0.5 * LOG2E) if blk < d // LANES else gk_ref[...]
        t = acc[:, cols]
        sq = t * t
        ssa = jnp.sum(jnp.where(lane_lo, sq, 0.0), axis=-1, keepdims=True)
        ssb = jnp.sum(jnp.where(lane_lo, 0.0, sq), axis=-1, keepdims=True)
        r = jnp.where(lane_lo, lax.rsqrt(ssa / A_HEAD_DIM + NORM_EPS),
                      lax.rsqrt(ssb / A_HEAD_DIM + NORM_EPS))
        o_ref[:, cols] = (t * r * gain).astype(o_ref.dtype)
    o_ref[:, 2 * d:] = acc[:, 2 * d:].astype(o_ref.dtype)


def _qkv_proj(xf, g, sc, sh, w, q_norm, k_norm, seq):
    n, d = xf.shape
    m = w.shape[1]
    tm = min(ROW_TILE, seq)
    per_b = seq // tm
    vec = pl.BlockSpec((None, 1, d), lambda i: (i // per_b, 0, 0))
    pair_gain = lambda gn: jnp.tile(gn.reshape(1, A_HEAD_DIM), (1, LANES // A_HEAD_DIM))
    const = pl.BlockSpec((1, LANES), lambda i: (0, 0))
    return pl.pallas_call(
        _qkv_kernel,
        grid=(n // tm,),
        in_specs=[pl.BlockSpec((tm, d), lambda i: (i, 0)),
                  pl.BlockSpec((1, d), lambda i: (0, 0)),
                  vec, vec,
                  pl.BlockSpec((d, m), lambda i: (0, 0)), const, const],
        out_specs=pl.BlockSpec((tm, m), lambda i: (i, 0)),
        out_shape=jax.ShapeDtypeStruct((n, m), BF16),
        compiler_params=_cparams("parallel"),
    )(xf, g, sc, sh, w, pair_gain(q_norm), pair_gain(k_norm))


ATT_SLOTS = 4


def _attention_blocks(seq, qk_of, v_ref, bias_ref, s_ref, p_ref, ve_ref, o_ref, lane_lo,
                      diagonal_bias=False):
    nblk = seq // QBLK

    for r0 in range(0, seq, QBLK):
        v = v_ref[r0:r0 + QBLK, :].astype(F32)
        ve_ref[0, r0:r0 + QBLK, :] = jnp.where(lane_lo, v, 1.0).astype(BF16)
        ve_ref[1, r0:r0 + QBLK, :] = jnp.where(lane_lo, 1.0, v).astype(BF16)

    def slot(blk, h):
        return (blk % 2) * 2 + h

    def scores(blk):
        w = (blk + 1) * QBLK
        lo = w - QBLK if diagonal_bias else 0
        bias = bias_ref[:, seq - w + lo:seq]
        for h in range(2):
            q, k = qk_of(blk, h, w)
            s = _dot_nt(q, k)
            if lo:
                s_ref[slot(blk, h), :, 0:lo] = s[:, :lo]
            s_ref[slot(blk, h), :, lo:w] = s[:, lo:] + bias

    for blk in range(nblk):
        w = (blk + 1) * QBLK
        scores(blk)
        for h in range(2):
            s = s_ref[slot(blk, h), :, 0:w]
            p_ref[slot(blk, h), :, 0:w] = jnp.exp2(s - jnp.max(s, axis=-1, keepdims=True)).astype(BF16)
        acc_a, acc_b = [jnp.dot(p_ref[slot(blk, h), :, 0:w], ve_ref[h, 0:w, :],
                                preferred_element_type=F32) for h in range(2)]
        sums = pltpu.roll(jnp.where(lane_lo, acc_b, acc_a), LANES // 2, 1)
        o_ref[blk * QBLK:w, :] = (jnp.where(lane_lo, acc_a, acc_b) / sums).astype(o_ref.dtype)


def _attn_scratch(seq):
    return [pltpu.VMEM((ATT_SLOTS, QBLK, seq), F32), pltpu.VMEM((ATT_SLOTS, QBLK, seq), BF16),
            pltpu.VMEM((2, seq, LANES), BF16)]


def _cast_rider(weights, layer, grid):
    steps = grid[0] * grid[1]
    in_specs, out_specs, out_shapes = [], [], []
    for w in weights:
        n_exp = w.shape[1]
        assert n_exp % steps == 0
        per = n_exp // steps
        blk = (per,) + w.shape[2:]
        in_specs.append(pl.BlockSpec((None,) + blk, lambda b, h: (layer, b * grid[1] + h, 0, 0)))
        out_specs.append(pl.BlockSpec(blk, lambda b, h: (b * grid[1] + h, 0, 0)))
        out_shapes.append(jax.ShapeDtypeStruct(w.shape[1:], BF16))
    return in_specs, out_specs, out_shapes


def _split3(v):
    p1 = v.astype(BF16).astype(F32)
    p2 = (v - p1).astype(BF16).astype(F32)
    return p1, p2, v - p1 - p2


def _dilated_log_multiplicity(seq):
    a = np.arange(QBLK)[:, None]
    c = np.arange(seq)[None, :]
    d = (seq - QBLK) + a - c
    mult = np.zeros((QBLK, seq), np.int32)
    for win, dil in DILATED_PAIRS:
        mult += ((d >= 0) & (d % dil == 0) & (d <= win)).astype(np.int32)
    return np.where(mult > 0, np.log2(np.maximum(mult, 1)), MASKED).astype(np.float32)


N_FEAT = 9


def _dil_attn_kernel(q_ref, k_ref, v_ref, posc_ref, sfeat_ref, logm_ref,
                     wa_ref, wb_ref, o_ref, wa_out, wb_out, qa_ref, qb_ref, ks_ref, s_ref, p_ref,
                     ve_ref):
    wa_out[...] = wa_ref[...].astype(BF16)
    wb_out[...] = wb_ref[...].astype(BF16)
    seq = q_ref.shape[0]
    lane = lax.broadcasted_iota(jnp.int32, (QBLK, LANES), 1)
    lane_lo = lane < A_HEAD_DIM
    piece = [functools.reduce(jnp.logical_or, [lane == i + 3 * j for j in range(N_FEAT // 3)])
             for i in range(3)]

    rel0 = posc_ref[0:1, :]
    for r0 in range(0, seq, QBLK):
        rows = slice(r0, r0 + QBLK)
        qn = q_ref[rows, :].astype(F32)
        qa_ref[rows, :LANES] = jnp.where(lane_lo, qn, 0.0).astype(BF16)
        qb_ref[rows, :LANES] = jnp.where(lane_lo, 0.0, qn).astype(BF16)
        qa_ref[rows, LANES:] = jnp.broadcast_to(sfeat_ref[0:1, :], (QBLK, LANES)).astype(BF16)
        qb_ref[rows, LANES:] = jnp.broadcast_to(sfeat_ref[1:2, :], (QBLK, LANES)).astype(BF16)
        ks_ref[rows, :LANES] = k_ref[rows, :]
        p1, p2, p3 = _split3((posc_ref[rows, :] - rel0).astype(F32))
        feat = jnp.where(piece[0], p1, jnp.where(piece[1], p2, jnp.where(piece[2], p3, 0.0)))
        ks_ref[rows, LANES:] = feat.astype(BF16)

    def qk_of(blk, h, w):
        return (qa_ref, qb_ref)[h][w - QBLK:w, :], ks_ref[0:w, :]

    _attention_blocks(seq, qk_of, v_ref, logm_ref, s_ref, p_ref, ve_ref, o_ref, lane_lo)


def _dilated_attention(qkv, pos, cast_weights, layer):
    bsz, seq, three_d = qkv.shape
    d = three_d // 3
    n_pairs = d // LANES
    n_heads = d // A_HEAD_DIM
    slopes = jnp.exp2(-8.0 * jnp.arange(1, n_heads + 1, dtype=F32) / n_heads) * LOG2E
    sfeat = jnp.repeat(jnp.stack(_split3(slopes), axis=1), 3, axis=1)
    sfeat = jnp.pad(sfeat, ((0, 0), (0, LANES - N_FEAT))).reshape(n_pairs, 2, LANES)
    logm = jnp.asarray(_dilated_log_multiplicity(seq))
    blk = lambda off: pl.BlockSpec((None, seq, LANES), lambda b, h: (b, 0, off + h))
    ext = pltpu.VMEM((seq, 2 * LANES), BF16)
    grid = (bsz, n_pairs)
    w_in, w_out, w_shapes = _cast_rider(cast_weights, layer, grid)
    return pl.pallas_call(
        _dil_attn_kernel,
        grid=grid,
        in_specs=[blk(0), blk(n_pairs), blk(2 * n_pairs),
                  pl.BlockSpec((None, seq, 1), lambda b, h: (b, 0, 0)),
                  pl.BlockSpec((None, 2, LANES), lambda b, h: (h, 0, 0)),
                  pl.BlockSpec((QBLK, seq), lambda b, h: (0, 0))] + w_in,
        out_specs=[pl.BlockSpec((None, seq, LANES), lambda b, h: (b, 0, h))] + w_out,
        scratch_shapes=[ext, ext, ext] + _attn_scratch(seq),
        out_shape=[jax.ShapeDtypeStruct((bsz, seq, d), BF16)] + w_shapes,
        compiler_params=_cparams("parallel", "parallel"),
    )(qkv, qkv, qkv, pos.reshape(bsz, seq, 1), sfeat, logm, *cast_weights)


def _post_attn_kernel(o_ref, wo_ref, x_ref, g1_ref, n2_ref, sc_ref, sh_ref, wrh_ref, wrl_ref,
                      xm_ref, h_ref, hs_ref, lg_ref, il_ref):
    y = jnp.dot(o_ref[...], wo_ref[...], preferred_element_type=F32)
    xm = x_ref[...] + g1_ref[...] * y
    xm_ref[...] = xm
    h = _rms(xm) * n2_ref[...]
    h = h * (1.0 + sc_ref[...]) + sh_ref[...]
    h_ref[...] = h.astype(h_ref.dtype)
    hs_ref[...] = _rows_to_slab_rows(h, il_ref)
    h_hi = h.astype(BF16)
    h_lo = (h - h_hi.astype(F32)).astype(BF16)
    lg_ref[...] = (_dot_nt(wrh_ref[...], h_hi) + _dot_nt(wrh_ref[...], h_lo)
                   + _dot_nt(wrl_ref[...], h_hi))


def _post_attn(o, w_o, xf, g1, n2, sc2, sh2, w_router, seq):
    n, d = xf.shape
    ne = w_router.shape[1]
    tm = min(ROW_TILE, seq)
    per_b = seq // tm
    slab_rows = d // LANES
    wr_t = w_router.T.astype(F32)
    wr_hi = wr_t.astype(BF16)
    vec = pl.BlockSpec((None, 1, d), lambda i: (i // per_b, 0, 0))
    row = lambda w: pl.BlockSpec((tm, w), lambda i: (i, 0))
    return pl.pallas_call(
        _post_attn_kernel,
        grid=(n // tm,),
        in_specs=[row(d), pl.BlockSpec((d, d), lambda i: (0, 0)), row(d), vec,
                  pl.BlockSpec((1, d), lambda i: (0, 0)), vec, vec,
                  pl.BlockSpec((ne, d), lambda i: (0, 0)), pl.BlockSpec((ne, d), lambda i: (0, 0))],
        out_specs=[row(d), row(d), pl.BlockSpec((tm * slab_rows, LANES), lambda i: (i, 0)),
                   pl.BlockSpec((ne, tm), lambda i: (0, i))],
        out_shape=[jax.ShapeDtypeStruct((n, d), F32), jax.ShapeDtypeStruct((n, d), BF16),
                   jax.ShapeDtypeStruct((n * slab_rows, LANES), BF16),
                   jax.ShapeDtypeStruct((ne, n), F32)],
        scratch_shapes=[pltpu.VMEM((tm * slab_rows, LANES), F32)],
        compiler_params=_cparams("parallel"),
    )(o, w_o, xf, g1, n2, sc2, sh2, wr_hi, (wr_t - wr_hi.astype(F32)).astype(BF16))


def _moe_rows(tile):
    rows = tile * TOP_K + N_EXPERTS * (SUBLANES - 1) + EROWS
    return -(-rows // SUBLANES) * SUBLANES


def _first_index(hit, index, limit):
    return jnp.min(jnp.where(hit, index, limit), axis=0, keepdims=True)


def _route_kernel(lg_ref, bias_ref, tri_ref, dest_ref, w_ref, seg_ref, cnt_ref, *, row_pitch):
    t = lg_ref.shape[1]
    per_g = N_EXPERTS // N_GROUPS
    sub = lax.broadcasted_iota(jnp.int32, (per_g, t), 0).astype(F32)
    neg = -jnp.inf
    scores, sel = [], []
    for g in range(N_GROUPS):
        rows = slice(g * per_g, (g + 1) * per_g)
        sc = jax.nn.sigmoid(lg_ref[rows, :])
        scores.append(sc)
        sel.append(sc + bias_ref[rows, :])

    gs = []
    for g in range(N_GROUPS):
        m1 = jnp.max(sel[g], axis=0, keepdims=True)
        first = _first_index(sel[g] == m1, sub, float(per_g))
        m2 = jnp.max(jnp.where(sub == first, neg, sel[g]), axis=0, keepdims=True)
        gs.append(m1 + m2)
    gsc = jnp.concatenate(gs, axis=0)
    gsub = lax.broadcasted_iota(jnp.int32, (N_GROUPS, t), 0).astype(F32)
    gkeep = jnp.zeros((N_GROUPS, t), F32)
    for _ in range(TOPK_GROUPS):
        m = jnp.max(gsc, axis=0, keepdims=True)
        pick = gsub == _first_index(gsc == m, gsub, float(N_GROUPS))
        gkeep = jnp.where(pick, 1.0, gkeep)
        gsc = jnp.where(pick, neg, gsc)

    eidx = [sub + float(g * per_g) for g in range(N_GROUPS)]
    masked = [jnp.where(jnp.broadcast_to(gkeep[g:g + 1, :], (per_g, t)) > 0.5, sel[g], neg)
              for g in range(N_GROUPS)]
    chosen = [jnp.zeros((per_g, t), F32) for _ in range(N_GROUPS)]
    wsel = [jnp.zeros((per_g, t), F32) for _ in range(N_GROUPS)]
    picked = []
    for _ in range(TOP_K):
        m = jnp.max(functools.reduce(jnp.maximum, masked), axis=0, keepdims=True)
        cand = functools.reduce(jnp.minimum, [jnp.where(masked[g] == m, eidx[g], float(N_EXPERTS))
                                              for g in range(N_GROUPS)])
        e_first = jnp.min(cand, axis=0, keepdims=True)
        picked.append(e_first)
        for g in range(N_GROUPS):
            pick = eidx[g] == e_first
            chosen[g] = jnp.where(pick, 1.0, chosen[g])
            wsel[g] = jnp.where(pick, scores[g], wsel[g])
            masked[g] = jnp.where(pick, neg, masked[g])

    denom = jnp.sum(functools.reduce(jnp.add, wsel), axis=0, keepdims=True)

    c_all = jnp.concatenate(chosen, axis=0)
    nblk = t // LANES
    stacked = jnp.concatenate([c_all[:, b * LANES:(b + 1) * LANES] for b in range(nblk)], axis=0)
    within = jnp.dot(stacked.astype(BF16), tri_ref[...], preferred_element_type=F32)
    ranks, before = [], jnp.zeros((N_EXPERTS, 1), F32)
    for b in range(nblk):
        rows = slice(b * N_EXPERTS, (b + 1) * N_EXPERTS)
        ranks.append(within[rows, :] + before)
        before = before + within[rows, LANES - 1:LANES] + stacked[rows, LANES - 1:LANES]
    rank = jnp.concatenate(ranks, axis=1)
    counts = before
    padded = jnp.ceil(counts / SUBLANES) * SUBLANES
    below = (lax.broadcasted_iota(jnp.int32, (N_EXPERTS, N_EXPERTS), 1)
             < lax.broadcasted_iota(jnp.int32, (N_EXPERTS, N_EXPERTS), 0)).astype(F32)
    seg = jnp.dot(below, jnp.broadcast_to(padded, (N_EXPERTS, LANES)), preferred_element_type=F32,
                  precision=lax.Precision.HIGHEST)
    row_of = seg[:, :1] + rank

    dest_rows, w_rows = [], []
    for k in range(TOP_K):
        d_k = jnp.zeros((per_g, t), F32)
        w_k = jnp.zeros((per_g, t), F32)
        for g in range(N_GROUPS):
            hit = eidx[g] == picked[k]
            d_k = jnp.where(hit, row_of[g * per_g:(g + 1) * per_g, :], d_k)
            w_k = jnp.where(hit, wsel[g], w_k)
        dest_rows.append(jnp.sum(d_k, axis=0, keepdims=True))
        w_rows.append(jnp.sum(w_k, axis=0, keepdims=True) / denom * ROUTED_SCALE)
    fill = [jnp.zeros((SUBLANES - TOP_K, t), F32)]
    dest8 = (jnp.concatenate(dest_rows + fill, axis=0) * row_pitch).astype(jnp.int32)
    w8 = jnp.concatenate(w_rows + fill, axis=0)
    for j in range(t // LANES):
        dest_ref[j] = dest8[:, j * LANES:(j + 1) * LANES]
        w_ref[j] = w8[:, j * LANES:(j + 1) * LANES]
    seg_ref[...] = seg.astype(jnp.int32)
    cnt_ref[...] = jnp.broadcast_to(counts, (N_EXPERTS, LANES)).astype(jnp.int32)


def _route(logits_t, router_bias, tile, row_pitch):
    ne, n = logits_t.shape
    nt = n // tile
    tri = jnp.asarray(np.triu(np.ones((LANES, LANES), np.float32), 1), dtype=BF16)
    per_tile = lambda rows, width: pl.BlockSpec((None, rows, width), lambda i: (i, 0, 0))
    table = pl.BlockSpec((None, tile // LANES, SUBLANES, LANES), lambda i: (i, 0, 0, 0))
    dest, wts, seg, cnt = pl.pallas_call(
        functools.partial(_route_kernel, row_pitch=row_pitch),
        grid=(nt,),
        in_specs=[pl.BlockSpec((ne, tile), lambda i: (0, i)),
                  pl.BlockSpec((ne, 1), lambda i: (0, 0)),
                  pl.BlockSpec((LANES, LANES), lambda i: (0, 0))],
        out_specs=[table, table, per_tile(ne, LANES), per_tile(ne, LANES)],
        out_shape=[jax.ShapeDtypeStruct((nt, tile // LANES, SUBLANES, LANES), jnp.int32),
                   jax.ShapeDtypeStruct((nt, tile // LANES, SUBLANES, LANES), F32),
                   jax.ShapeDtypeStruct((nt, ne, LANES), jnp.int32),
                   jax.ShapeDtypeStruct((nt, ne, LANES), jnp.int32)],
        compiler_params=_cparams("parallel"),
    )(logits_t, router_bias.astype(F32).reshape(ne, 1), tri)
    return seg[:, :, 0].reshape(-1), cnt[:, :, 0].reshape(-1), dest, wts


def _swiglu(xb, wgu_ref, wdn_ref):
    gu = jnp.dot(xb, wgu_ref[...], preferred_element_type=F32)
    half = wgu_ref.shape[1] // 2
    act = _silu(gu[:, :half]) * gu[:, half:]
    return jnp.dot(act.astype(BF16), wdn_ref[...], preferred_element_type=F32)


def _slabs_to_rows(p, il_ref, r):
    n = il_ref.shape[0] // r
    il_ref[...] = pltpu.bitcast(p, BF16).astype(F32)
    return jnp.concatenate([il_ref[pl.ds(2 * c + h, r, stride=n), :]
                            for h in range(2) for c in range(n // 2)], axis=1)


def _rows_to_slab_rows(v, il_ref):
    r = v.shape[0]
    n = il_ref.shape[0] // r
    for h in range(2):
        for c in range(n // 2):
            lane0 = (h * (n // 2) + c) * LANES
            il_ref[pl.ds(2 * c + h, r, stride=n), :] = v[:, lane0:lane0 + LANES]
    return il_ref[...].astype(BF16)


def _rows_to_slabs(v, il_ref):
    return pltpu.bitcast(_rows_to_slab_rows(v, il_ref), U32)


def _moe_kernel(start_ref, count_ref, hp_ref, hb_ref, dest_ref, wrow_ref, wgu_hbm, wdn_hbm,
                sgu_ref, sdn_ref, xm_ref, g2_ref, o_ref, xy_ref, il_ref, yp_ref, zf_ref,
                comb_ref, wgu_buf, wdn_buf, wsem):
    t = pl.program_id(0)
    s = pl.program_id(1)
    slab = xm_ref.shape[1] // (2 * LANES)
    tile = hp_ref.shape[0] // (2 * slab)
    group_rows = SUBLANES * slab

    def weight_copies(e, slot):
        return (pltpu.make_async_copy(wgu_hbm.at[e], wgu_buf.at[slot], wsem.at[0, slot]),
                pltpu.make_async_copy(wdn_hbm.at[e], wdn_buf.at[slot], wsem.at[1, slot]))

    def slab_at(ref, first):
        return ref.at[pl.ds(pl.multiple_of(first, slab), slab), :]

    @pl.when(s == 0)
    def _first_weights():
        for p in range(EXPERT_GROUP):
            for cp in weight_copies(p, p):
                cp.start()

    @pl.when(jnp.logical_and(t == 0, s == 0))
    def _clear():
        xy_ref[...] = jnp.zeros(xy_ref.shape, U32)

    @pl.when(s == 0)
    def _dispatch():
        def group(j, carry):
            for sub in range(LANES // SUBLANES):
                p0 = pl.multiple_of((j * (LANES // SUBLANES) + sub) * 2 * group_rows, 2 * group_rows)
                rows = pltpu.bitcast(hp_ref[pl.ds(p0, 2 * group_rows), :], U32)
                for u in range(SUBLANES):
                    for k in range(TOP_K):
                        slab_at(xy_ref, dest_ref[j, k, sub * SUBLANES + u])[...] = (
                            rows[u * slab:(u + 1) * slab, :])
            return carry

        lax.fori_loop(0, tile // LANES, group, 0)

    @pl.when(s == 0)
    def _experts():
        def chunk_base(e, c):
            return pl.multiple_of((start_ref[t * N_EXPERTS + e] + c * EROWS) * slab, group_rows)

        def load_rows(base, p):
            return _slabs_to_rows(xy_ref[pl.ds(base, EROWS * slab), :], il_ref.at[p], EROWS).astype(BF16)

        def put_rows(p, base, valid):
            def put(j, c2):
                r0 = pl.multiple_of(j * group_rows, group_rows)
                xy_ref[pl.ds(base + r0, group_rows), :] = yp_ref[p, pl.ds(r0, group_rows), :]
                return c2

            lax.fori_loop(0, (jnp.clip(valid, 0, EROWS) + SUBLANES - 1) // SUBLANES, put, 0)

        grp = EXPERT_GROUP

        def expert_group(i, carry0):
            wslot = lax.rem(i, 2) * grp

            @pl.when(i + 1 < N_EXPERTS // grp)
            def _prefetch():
                for p in range(grp):
                    for cp in weight_copies(grp * (i + 1) + p, grp - wslot + p):
                        cp.start()

            for p in range(grp):
                for cp in weight_copies(grp * i + p, wslot + p):
                    cp.wait()
            counts = [count_ref[t * N_EXPERTS + grp * i + p] for p in range(grp)]
            bases = [chunk_base(grp * i + p, 0) for p in range(grp)]
            xs = [load_rows(bases[p], p) for p in range(grp)]
            ys = [_swiglu(xs[p], wgu_buf.at[wslot + p], wdn_buf.at[wslot + p]) for p in range(grp)]
            for p in range(grp):
                yp_ref[p] = _rows_to_slabs(ys[p], il_ref.at[p])
            for p in range(grp):
                put_rows(p, bases[p], counts[p])

            for p in range(grp):
                def more(c, carry, p=p):
                    base = chunk_base(grp * i + p, c)
                    y = _swiglu(load_rows(base, p), wgu_buf.at[wslot + p], wdn_buf.at[wslot + p])
                    yp_ref[p] = _rows_to_slabs(y, il_ref.at[p])
                    put_rows(p, base, counts[p] - c * EROWS)
                    return carry

                lax.fori_loop(1, (counts[p] + EROWS - 1) // EROWS, more, 0)
            return carry0

        lax.fori_loop(0, N_EXPERTS // grp, expert_group, 0)

    @pl.when(s >= 1)
    def _combine():
        sb = s - 1

        def group(j, carry):
            tj = sb * (CROWS // LANES) + j
            for sub in range(LANES // SUBLANES):
                r0 = pl.multiple_of(j * LANES + sub * SUBLANES, SUBLANES)
                half = sub % 2
                for u in range(SUBLANES):
                    lane = sub * SUBLANES + u
                    acc = None
                    for k in range(TOP_K):
                        z = pltpu.bitcast(slab_at(xy_ref, dest_ref[tj, k, lane])[...], BF16).astype(F32)
                        acc = wrow_ref[tj, k, lane] * z if acc is None else acc + wrow_ref[tj, k, lane] * z
                    zf_ref[half, u * 2 * slab:(u + 1) * 2 * slab, :] = acc
                comb_ref[pl.ds(r0, SUBLANES), :] = jnp.concatenate(
                    [zf_ref[half, pl.ds(2 * c + h, SUBLANES, stride=2 * slab), :]
                     for h in range(2) for c in range(slab)], axis=1)
            return carry

        lax.fori_loop(0, CROWS // LANES, group, 0)
        shared = _swiglu(hb_ref[...], sgu_ref, sdn_ref)
        o_ref[...] = xm_ref[...] + g2_ref[...] * (comb_ref[...] + shared)


def _moe(h, h_slabs, xm, g2, logits_t, router_bias, w_gu, w_dn, s_gu, s_dn, layer, seq):
    n, d = h.shape
    slab = d // (2 * LANES)
    tile = min(MOE_TILE, seq)
    nt = n // tile
    per_b = seq // tile
    csteps = tile // CROWS
    assert seq % tile == 0 and tile % CROWS == 0
    assert tile % LANES == 0 and CROWS % LANES == 0
    seg_start, seg_count, dest_rows, w_rows = _route(logits_t, router_bias, tile, slab)
    rows = _moe_rows(tile)
    table = pl.BlockSpec((None, tile // LANES, SUBLANES, LANES), lambda t, s, a, b: (t, 0, 0, 0),
                         memory_space=pltpu.SMEM)
    cst = lambda s: jnp.maximum(s - 1, 0)
    crow = pl.BlockSpec((CROWS, d), lambda t, s, a, b: (t * csteps + cst(s), 0))
    hbm = pl.BlockSpec(memory_space=pl.ANY)
    single = pl.Buffered(1)
    return pl.pallas_call(
        _moe_kernel,
        grid_spec=pltpu.PrefetchScalarGridSpec(
            num_scalar_prefetch=2,
            grid=(nt, 1 + csteps),
            in_specs=[pl.BlockSpec((tile * 2 * slab, LANES), lambda t, s, a, b: (t, 0),
                                   pipeline_mode=single),
                      crow, table, table,
                      hbm, hbm,
                      pl.BlockSpec((None,) + s_gu.shape[1:], lambda t, s, a, b: (layer, 0, 0),
                                   pipeline_mode=single),
                      pl.BlockSpec((None,) + s_dn.shape[1:], lambda t, s, a, b: (layer, 0, 0),
                                   pipeline_mode=single),
                      crow,
                      pl.BlockSpec((None, 1, d), lambda t, s, a, b: (t // per_b, 0, 0))],
            out_specs=crow,
            scratch_shapes=[pltpu.VMEM((rows * slab, LANES), U32),
                            pltpu.VMEM((EXPERT_GROUP, EROWS * 2 * slab, LANES), F32),
                            pltpu.VMEM((EXPERT_GROUP, EROWS * slab, LANES), U32),
                            pltpu.VMEM((2, SUBLANES * 2 * slab, LANES), F32),
                            pltpu.VMEM((CROWS, d), F32),
                            pltpu.VMEM((2 * EXPERT_GROUP, d, 2 * EXPERT_DIM), BF16),
                            pltpu.VMEM((2 * EXPERT_GROUP, EXPERT_DIM, d), BF16),
                            pltpu.SemaphoreType.DMA((2, 2 * EXPERT_GROUP))]),
        out_shape=jax.ShapeDtypeStruct((n, d), F32),
        compiler_params=_cparams("arbitrary", "arbitrary"),
    )(seg_start, seg_count, h_slabs, h, dest_rows, w_rows, w_gu, w_dn, s_gu, s_dn, xm, g2)


def _mla_prep_kernel(x_ref, n1_ref, sc1_ref, sh1_ref, nk_ref, sck_ref, shk_ref,
                     wdq_ref, gql_ref, wuq_ref, wuqs_ref, wdown_ref, gkl_ref, wk_ref, gk_ref, wuv_ref,
                     qa_ref, qb_ref, kc_ref, ksa_ref, ksb_ref, q_ref, k_ref, v_ref):
    xn = _rms(x_ref[...])
    hq = xn * n1_ref[...] * (1.0 + sc1_ref[...]) + sh1_ref[...]
    hk = xn * nk_ref[...] * (1.0 + sck_ref[...]) + shk_ref[...]
    n_heads = q_ref.shape[1] // LANES
    half = ROPE_DIM // 2

    cq = (_rms(jnp.dot(hq.astype(BF16), wdq_ref[...], preferred_element_type=F32)) * gql_ref[...]).astype(BF16)
    q_raw = jnp.dot(cq, wuq_ref[...], preferred_element_type=F32)
    q_swp = jnp.dot(cq, wuqs_ref[...], preferred_element_type=F32)
    qa, qb = qa_ref[...], qb_ref[...]
    for h in range(n_heads):
        sl = slice(h * LANES, (h + 1) * LANES)
        t = q_raw[:, sl]
        r = lax.rsqrt(jnp.sum(t * t, axis=-1, keepdims=True) / QK_DIM + NORM_EPS)
        q_ref[:, sl] = (r * (t * qa + q_swp[:, sl] * qb)).astype(q_ref.dtype)

    down = jnp.dot(hk.astype(BF16), wdown_ref[...], preferred_element_type=F32)
    ckv = (_rms(down[:, :KV_RANK]) * gkl_ref[...]).astype(BF16)
    pe = down[:, KV_RANK:]
    ss_pe = jnp.sum(pe * pe, axis=-1, keepdims=True)
    rot = (pe * kc_ref[...] + pltpu.roll(pe, LANES - half, 1) * ksa_ref[...]
           + pltpu.roll(pe, half, 1) * ksb_ref[...])
    rot_hi = rot.astype(BF16)
    rot_lo = (rot - rot_hi.astype(F32)).astype(BF16)
    k_raw = jnp.dot(jnp.concatenate([ckv, rot_hi, rot_lo], axis=1), wk_ref[...],
                    preferred_element_type=F32)
    nope = lax.broadcasted_iota(jnp.int32, (x_ref.shape[0], LANES), 1) < NOPE_DIM
    for h in range(n_heads):
        sl = slice(h * LANES, (h + 1) * LANES)
        t = k_raw[:, sl]
        ss = jnp.sum(jnp.where(nope, t * t, 0.0), axis=-1, keepdims=True) + ss_pe
        k_ref[:, sl] = (t * lax.rsqrt(ss / QK_DIM + NORM_EPS) * gk_ref[...]).astype(k_ref.dtype)
    v_ref[...] = jnp.dot(ckv, wuv_ref[...], preferred_element_type=F32).astype(v_ref.dtype)


def _pad_heads(w, n_heads, width):
    r = w.shape[0]
    return jnp.pad(w.reshape(r, n_heads, width), ((0, 0), (0, 0), (0, LANES - width))).reshape(r, n_heads * LANES)


def _swap_rope_halves(a):
    half = ROPE_DIM // 2
    lead = a.shape[:-1]
    return jnp.concatenate([jnp.zeros(lead + (NOPE_DIM,), a.dtype), a[..., NOPE_DIM + half:QK_DIM],
                            a[..., NOPE_DIM:NOPE_DIM + half],
                            jnp.zeros(lead + (LANES - QK_DIM,), a.dtype)], axis=-1)


def _mla_prep(xf, pos, n1, sc1, sh1, nk, sck, shk, w_dq, g_ql, w_uq, g_q, w_down, g_kl, w_uk, w_uv,
              g_k, seq):
    n, d = xf.shape
    n_heads = w_uv.shape[1] // V_DIM
    q_rank = w_dq.shape[1]
    hw = n_heads * LANES
    tm = min(ROW_TILE, seq)
    per_b = seq // tm

    half = ROPE_DIM // 2
    inv = jnp.power(ROPE_THETA, -jnp.arange(half, dtype=F32) / half)
    ang = pos.astype(F32).reshape(n, 1) * inv
    cs = jnp.concatenate([jnp.cos(ang), jnp.sin(ang)], axis=1)
    tail = LANES - QK_DIM
    q_mul = QK_DIM ** -0.5 * LOG2E
    j = np.arange(half)
    sel = np.zeros((ROPE_DIM, 5, LANES), np.float32)
    sel[j, 0, NOPE_DIM + j] = sel[j, 0, NOPE_DIM + half + j] = 1.0
    sel[half + j, 1, NOPE_DIM + j], sel[half + j, 1, NOPE_DIM + half + j] = -1.0, 1.0
    sel[j, 2, j] = sel[j, 2, half + j] = 1.0
    sel[half + j, 3, j] = -1.0
    sel[half + j, 4, half + j] = 1.0
    gq_pad = jnp.pad(g_q, (0, tail)) * q_mul
    gq_swp = _swap_rope_halves(g_q.reshape(1, QK_DIM))[0] * q_mul
    gk1, gk2 = g_k[NOPE_DIM:NOPE_DIM + half], g_k[NOPE_DIM + half:]
    kpad = jnp.zeros((LANES - ROPE_DIM,), F32)
    k_own, k_swp = jnp.concatenate([gk1, gk2, kpad]), jnp.concatenate([gk2, gk1, kpad])
    place = (jnp.asarray(sel) * jnp.stack([gq_pad, gq_swp, k_own, k_swp, k_swp])).reshape(ROPE_DIM, -1)
    const = jnp.pad(gq_pad[:NOPE_DIM], (0, 5 * LANES - NOPE_DIM))
    tables = jnp.dot(cs, place, precision=lax.Precision.HIGHEST) + const
    gk_mask = jnp.concatenate([g_k[:NOPE_DIM], jnp.ones((ROPE_DIM,), F32),
                               jnp.zeros((tail,), F32)]).reshape(1, LANES)

    wuq_p = _pad_heads(w_uq, n_heads, QK_DIM).astype(BF16)
    wuq_s = _swap_rope_halves(w_uq.reshape(q_rank, n_heads, QK_DIM)).reshape(q_rank, hw).astype(BF16)
    wdown_p = jnp.pad(w_down, ((0, 0), (0, LANES - ROPE_DIM))).astype(BF16)
    eye = jnp.pad(jnp.eye(ROPE_DIM, dtype=F32), ((0, LANES - ROPE_DIM), (NOPE_DIM, tail)))
    pe_rows = jnp.tile(eye, (1, n_heads))
    wk = jnp.concatenate([_pad_heads(w_uk, n_heads, NOPE_DIM), pe_rows, pe_rows], axis=0).astype(BF16)

    vec = pl.BlockSpec((None, 1, d), lambda i: (i // per_b, 0, 0))
    full = lambda a: pl.BlockSpec(a.shape, lambda i: (0,) * a.ndim)
    row = lambda wdt: pl.BlockSpec((tm, wdt), lambda i: (i, 0))
    args = [xf, n1, sc1, sh1, nk, sck, shk,
            w_dq.astype(BF16), g_ql.reshape(1, q_rank), wuq_p, wuq_s,
            wdown_p, g_kl.reshape(1, KV_RANK), wk, gk_mask, w_uv.astype(BF16),
            tables, tables, tables, tables, tables]
    table_specs = [pl.BlockSpec((tm, LANES), lambda i, c=c: (i, c)) for c in range(5)]
    specs = [row(d), full(n1), vec, vec, full(nk), vec, vec] + [full(a) for a in args[7:16]] + table_specs
    return pl.pallas_call(
        _mla_prep_kernel,
        grid=(n // tm,),
        in_specs=specs,
        out_specs=[row(hw), row(hw), row(n_heads * V_DIM)],
        out_shape=[jax.ShapeDtypeStruct((n, hw), BF16), jax.ShapeDtypeStruct((n, hw), BF16),
                   jax.ShapeDtypeStruct((n, n_heads * V_DIM), BF16)],
        compiler_params=_cparams("parallel"),
    )(*args)


def _causal_bias(seq):
    a = np.arange(QBLK)[:, None]
    c = np.arange(seq)[None, :]
    return np.where(c <= (seq - QBLK) + a, 0.0, MASKED).astype(np.float32)


def _mla_attn_kernel(q_ref, k_ref, v_ref, mask_ref, wa_ref, wb_ref, o_ref, wa_out, wb_out,
                     s_ref, p_ref, ve_ref):
    wa_out[...] = wa_ref[...].astype(BF16)
    wb_out[...] = wb_ref[...].astype(BF16)
    seq = q_ref.shape[0]
    lane_lo = lax.broadcasted_iota(jnp.int32, (QBLK, LANES), 1) < V_DIM

    def qk_of(blk, h, w):
        lanes = slice(h * LANES, (h + 1) * LANES)
        return q_ref[w - QBLK:w, lanes], k_ref[0:w, lanes]

    _attention_blocks(seq, qk_of, v_ref, mask_ref, s_ref, p_ref, ve_ref, o_ref, lane_lo,
                      diagonal_bias=True)


def _mla_attention(q, k, v, cast_weights, layer):
    bsz, seq, hw = q.shape
    n_pairs = hw // (2 * LANES)
    qk = pl.BlockSpec((None, seq, 2 * LANES), lambda b, h: (b, 0, h))
    vo = pl.BlockSpec((None, seq, LANES), lambda b, h: (b, 0, h))
    grid = (bsz, n_pairs)
    w_in, w_out, w_shapes = _cast_rider(cast_weights, layer, grid)
    return pl.pallas_call(
        _mla_attn_kernel,
        grid=grid,
        in_specs=[qk, qk, vo, pl.BlockSpec((QBLK, seq), lambda b, h: (0, 0))] + w_in,
        out_specs=[vo] + w_out,
        scratch_shapes=_attn_scratch(seq),
        out_shape=[jax.ShapeDtypeStruct(v.shape, BF16)] + w_shapes,
        compiler_params=_cparams("parallel", "parallel"),
    )(q, k, v, jnp.asarray(_causal_bias(seq)), *cast_weights)


def kernel(x, c, pos, ada_w, ada_b, norm1_g, norm2_g, a_w_qkv, a_q_norm, a_k_norm, a_w_o, kv_ada_w, kv_ada_b, kv_norm_g, kv_w_down, kv_latent_norm, kv_w_uk, kv_w_uv, kv_k_norm, b_w_dq, b_q_latent_norm, b_w_uq, b_q_norm, b_w_o, moe_w_router, moe_router_bias, moe_w_gate_up, moe_w_down, moe_shared_gate_up, moe_shared_down):
    bsz, seq, d = x.shape
    n = bsz * seq
    depth = ada_w.shape[0]
    n_a = a_w_qkv.shape[0]
    assert seq % QBLK == 0
    xf = x.reshape(n, d)

    expert_weights = (moe_w_gate_up, moe_w_down)
    s_gu = moe_shared_gate_up.astype(BF16)
    s_dn = moe_shared_down.astype(BF16)
    row = lambda g: g.reshape(1, d)

    for layer in range(depth):
        mod = _modulation(c, ada_w, ada_b, layer).reshape(bsz, 6, 1, d)
        sh1, sc1, g1, sh2, sc2, g2 = [mod[:, i] for i in range(6)]
        if layer < n_a:
            qkv = _qkv_proj(xf, row(norm1_g[layer]), sc1, sh1, a_w_qkv[layer].astype(BF16),
                            a_q_norm[layer], a_k_norm[layer], seq)
            o, w_gu, w_dn = _dilated_attention(qkv.reshape(bsz, seq, -1), pos, expert_weights, layer)
            w_o = a_w_o[layer]
        else:
            j = layer - n_a
            kmod = _modulation(c, kv_ada_w[None], kv_ada_b[None], 0).reshape(bsz, 2, 1, d)
            q, k, v = _mla_prep(xf, pos, row(norm1_g[layer]), sc1, sh1, row(kv_norm_g), kmod[:, 1],
                                kmod[:, 0], b_w_dq[j], b_q_latent_norm[j], b_w_uq[j], b_q_norm[j],
                                kv_w_down, kv_latent_norm, kv_w_uk, kv_w_uv, kv_k_norm, seq)
            o, w_gu, w_dn = _mla_attention(q.reshape(bsz, seq, -1), k.reshape(bsz, seq, -1),
                                           v.reshape(bsz, seq, -1), expert_weights, layer)
            w_o = b_w_o[j]
        xm, h2, h2_slabs, logits = _post_attn(o.reshape(n, d), w_o.astype(BF16), xf, g1,
                                              row(norm2_g[layer]), sc2, sh2, moe_w_router[layer], seq)
        xf = _moe(h2, h2_slabs, xm, g2, logits, moe_router_bias[layer], w_gu, w_dn, s_gu, s_dn,
                  layer, seq)
    return xf.reshape(bsz, seq, d)
```

```python
import functools
import math

import numpy as np
import jax
import jax.numpy as jnp
from jax import lax
from jax.experimental import pallas as pl
from jax.experimental.pallas import tpu as pltpu

F32 = jnp.float32
BF16 = jnp.bfloat16
U32 = jnp.uint32

NORM_EPS = 1e-6
MASKED = -1e30
LOG2E = math.log2(math.e)

LANES = 128
SUBLANES = 8
VMEM_LIMIT = 56 * 1024 * 1024

A_HEAD_DIM = 64
DILATED_PAIRS = ((128, 1), (512, 4), (2048, 16))
QBLK = 256
NOPE_DIM = 64
ROPE_DIM = 32
QK_DIM = NOPE_DIM + ROPE_DIM
V_DIM = 64
KV_RANK = 256
ROPE_THETA = 10000.0
N_EXPERTS = 64
N_GROUPS = 8
TOPK_GROUPS = 4
TOP_K = 6
EXPERT_DIM = 256
ROUTED_SCALE = 2.5
MOE_TILE = 2048
EROWS = 256
CROWS = 256
EXPERT_GROUP = 4

ROW_TILE = 512


def _cparams(*sem):
    return pltpu.CompilerParams(dimension_semantics=sem, vmem_limit_bytes=VMEM_LIMIT)


def _silu(v):
    return v * jax.nn.sigmoid(v)


def _rms(v):
    return v * lax.rsqrt(jnp.mean(v * v, axis=-1, keepdims=True) + NORM_EPS)


def _dot_nt(a, b):
    return lax.dot_general(a, b, (((1,), (1,)), ((), ())), preferred_element_type=F32)


def _mod_kernel(c_ref, w_ref, b_ref, o_ref):
    cond = _silu(c_ref[...])
    o_ref[...] = jnp.dot(cond, w_ref[...], preferred_element_type=F32,
                         precision=lax.Precision.HIGHEST) + b_ref[...]


def _modulation(c, w, b, layer):
    bsz, d = c.shape
    m = w.shape[-1]
    tn = 1024
    return pl.pallas_call(
        _mod_kernel,
        grid=(m // tn,),
        in_specs=[pl.BlockSpec((bsz, d), lambda j: (0, 0)),
                  pl.BlockSpec((None, d, tn), lambda j: (layer, 0, j)),
                  pl.BlockSpec((None, 1, tn), lambda j: (layer, 0, j))],
        out_specs=pl.BlockSpec((bsz, tn), lambda j: (0, j)),
        out_shape=jax.ShapeDtypeStruct((bsz, m), F32),
        compiler_params=_cparams("arbitrary"),
    )(c, w, b.reshape(b.shape[0], 1, m))


def _qkv_kernel(x_ref, g_ref, sc_ref, sh_ref, w_ref, gq_ref, gk_ref, o_ref):
    h = _rms(x_ref[...]) * g_ref[...]
    h = h * (1.0 + sc_ref[...]) + sh_ref[...]
    acc = jnp.dot(h.astype(BF16), w_ref[...], preferred_element_type=F32)
    d = x_ref.shape[1]
    lane_lo = lax.broadcasted_iota(jnp.int32, (x_ref.shape[0], LANES), 1) < A_HEAD_DIM
    for blk in range(2 * d // LANES):
        cols = slice(blk * LANES, (blk + 1) * LANES)
        gain = gq_ref[...] * (A_HEAD_DIM ** -0.5 * LOG2E) if blk < d // LANES else gk_ref[...]
        t = acc[:, cols]
        sq = t * t
        ssa = jnp.sum(jnp.where(lane_lo, sq, 0.0), axis=-1, keepdims=True)
        ssb = jnp.sum(jnp.where(lane_lo, 0.0, sq), axis=-1, keepdims=True)
        r = jnp.where(lane_lo, lax.rsqrt(ssa / A_HEAD_DIM + NORM_EPS),
                      lax.rsqrt(ssb / A_HEAD_DIM + NORM_EPS))
        o_ref[:, cols] = (t * r * gain).astype(o_ref.dtype)
    o_ref[:, 2 * d:] = acc[:, 2 * d:].astype(o_ref.dtype)


def _qkv_proj(xf, g, sc, sh, w, q_norm, k_norm, seq):
    n, d = xf.shape
    m = w.shape[1]
    tm = min(ROW_TILE, seq)
    per_b = seq // tm
    vec = pl.BlockSpec((None, 1, d), lambda i: (i // per_b, 0, 0))
    pair_gain = lambda gn: jnp.tile(gn.reshape(1, A_HEAD_DIM), (1, LANES // A_HEAD_DIM))
    const = pl.BlockSpec((1, LANES), lambda i: (0, 0))
    return pl.pallas_call(
        _qkv_kernel,
        grid=(n // tm,),
        in_specs=[pl.BlockSpec((tm, d), lambda i: (i, 0)),
                  pl.BlockSpec((1, d), lambda i: (0, 0)),
                  vec, vec,
                  pl.BlockSpec((d, m), lambda i: (0, 0)), const, const],
        out_specs=pl.BlockSpec((tm, m), lambda i: (i, 0)),
        out_shape=jax.ShapeDtypeStruct((n, m), BF16),
        compiler_params=_cparams("parallel"),
    )(xf, g, sc, sh, w, pair_gain(q_norm), pair_gain(k_norm))


ATT_SLOTS = 4


def _attention_blocks(seq, qk_of, v_ref, bias_ref, s_ref, p_ref, ve_ref, o_ref, lane_lo,
                      diagonal_bias=False):
    nblk = seq // QBLK

    for r0 in range(0, seq, QBLK):
        v = v_ref[r0:r0 + QBLK, :].astype(F32)
        ve_ref[0, r0:r0 + QBLK, :] = jnp.where(lane_lo, v, 1.0).astype(BF16)
        ve_ref[1, r0:r0 + QBLK, :] = jnp.where(lane_lo, 1.0, v).astype(BF16)

    def slot(blk, h):
        return (blk % 2) * 2 + h

    def scores(blk):
        w = (blk + 1) * QBLK
        lo = w - QBLK if diagonal_bias else 0
        bias = bias_ref[:, seq - w + lo:seq]
        for h in range(2):
            q, k = qk_of(blk, h, w)
            s = _dot_nt(q, k)
            if lo:
                s_ref[slot(blk, h), :, 0:lo] = s[:, :lo]
            s_ref[slot(blk, h), :, lo:w] = s[:, lo:] + bias

    for blk in range(nblk):
        w = (blk + 1) * QBLK
        scores(blk)
        for h in range(2):
            s = s_ref[slot(blk, h), :, 0:w]
            p_ref[slot(blk, h), :, 0:w] = jnp.exp2(s - jnp.max(s, axis=-1, keepdims=True)).astype(BF16)
        acc_a, acc_b = [jnp.dot(p_ref[slot(blk, h), :, 0:w], ve_ref[h, 0:w, :],
                                preferred_element_type=F32) for h in range(2)]
        sums = pltpu.roll(jnp.where(lane_lo, acc_b, acc_a), LANES // 2, 1)
        o_ref[blk * QBLK:w, :] = (jnp.where(lane_lo, acc_a, acc_b) / sums).astype(o_ref.dtype)


def _attn_scratch(seq):
    return [pltpu.VMEM((ATT_SLOTS, QBLK, seq), F32), pltpu.VMEM((ATT_SLOTS, QBLK, seq), BF16),
            pltpu.VMEM((2, seq, LANES), BF16)]


def _cast_rider(weights, layer, grid):
    steps = grid[0] * grid[1]
    in_specs, out_specs, out_shapes = [], [], []
    for w in weights:
        n_exp = w.shape[1]
        assert n_exp % steps == 0
        per = n_exp // steps
        blk = (per,) + w.shape[2:]
        in_specs.append(pl.BlockSpec((None,) + blk, lambda b, h: (layer, b * grid[1] + h, 0, 0)))
        out_specs.append(pl.BlockSpec(blk, lambda b, h: (b * grid[1] + h, 0, 0)))
        out_shapes.append(jax.ShapeDtypeStruct(w.shape[1:], BF16))
    return in_specs, out_specs, out_shapes


def _split3(v):
    p1 = v.astype(BF16).astype(F32)
    p2 = (v - p1).astype(BF16).astype(F32)
    return p1, p2, v - p1 - p2


def _dilated_log_multiplicity(seq):
    a = np.arange(QBLK)[:, None]
    c = np.arange(seq)[None, :]
    d = (seq - QBLK) + a - c
    mult = np.zeros((QBLK, seq), np.int32)
    for win, dil in DILATED_PAIRS:
        mult += ((d >= 0) & (d % dil == 0) & (d <= win)).astype(np.int32)
    return np.where(mult > 0, np.log2(np.maximum(mult, 1)), MASKED).astype(np.float32)


N_FEAT = 9


def _dil_attn_kernel(q_ref, k_ref, v_ref, posc_ref, sfeat_ref, logm_ref,
                     wa_ref, wb_ref, o_ref, wa_out, wb_out, qa_ref, qb_ref, ks_ref, s_ref, p_ref,
                     ve_ref):
    wa_out[...] = wa_ref[...].astype(BF16)
    wb_out[...] = wb_ref[...].astype(BF16)
    seq = q_ref.shape[0]
    lane = lax.broadcasted_iota(jnp.int32, (QBLK, LANES), 1)
    lane_lo = lane < A_HEAD_DIM
    piece = [functools.reduce(jnp.logical_or, [lane == i + 3 * j for j in range(N_FEAT // 3)])
             for i in range(3)]

    rel0 = posc_ref[0:1, :]
    for r0 in range(0, seq, QBLK):
        rows = slice(r0, r0 + QBLK)
        qn = q_ref[rows, :].astype(F32)
        qa_ref[rows, :LANES] = jnp.where(lane_lo, qn, 0.0).astype(BF16)
        qb_ref[rows, :LANES] = jnp.where(lane_lo, 0.0, qn).astype(BF16)
        qa_ref[rows, LANES:] = jnp.broadcast_to(sfeat_ref[0:1, :], (QBLK, LANES)).astype(BF16)
        qb_ref[rows, LANES:] = jnp.broadcast_to(sfeat_ref[1:2, :], (QBLK, LANES)).astype(BF16)
        ks_ref[rows, :LANES] = k_ref[rows, :]
        p1, p2, p3 = _split3((posc_ref[rows, :] - rel0).astype(F32))
        feat = jnp.where(piece[0], p1, jnp.where(piece[1], p2, jnp.where(piece[2], p3, 0.0)))
        ks_ref[rows, LANES:] = feat.astype(BF16)

    def qk_of(blk, h, w):
        return (qa_ref, qb_ref)[h][w - QBLK:w, :], ks_ref[0:w, :]

    _attention_blocks(seq, qk_of, v_ref, logm_ref, s_ref, p_ref, ve_ref, o_ref, lane_lo)


def _dilated_attention(qkv, pos, cast_weights, layer):
    bsz, seq, three_d = qkv.shape
    d = three_d // 3
    n_pairs = d // LANES
    n_heads = d // A_HEAD_DIM
    slopes = jnp.exp2(-8.0 * jnp.arange(1, n_heads + 1, dtype=F32) / n_heads) * LOG2E
    sfeat = jnp.repeat(jnp.stack(_split3(slopes), axis=1), 3, axis=1)
    sfeat = jnp.pad(sfeat, ((0, 0), (0, LANES - N_FEAT))).reshape(n_pairs, 2, LANES)
    logm = jnp.asarray(_dilated_log_multiplicity(seq))
    blk = lambda off: pl.BlockSpec((None, seq, LANES), lambda b, h: (b, 0, off + h))
    ext = pltpu.VMEM((seq, 2 * LANES), BF16)
    grid = (bsz, n_pairs)
    w_in, w_out, w_shapes = _cast_rider(cast_weights, layer, grid)
    return pl.pallas_call(
        _dil_attn_kernel,
        grid=grid,
        in_specs=[blk(0), blk(n_pairs), blk(2 * n_pairs),
                  pl.BlockSpec((None, seq, 1), lambda b, h: (b, 0, 0)),
                  pl.BlockSpec((None, 2, LANES), lambda b, h: (h, 0, 0)),
                  pl.BlockSpec((QBLK, seq), lambda b, h: (0, 0))] + w_in,
        out_specs=[pl.BlockSpec((None, seq, LANES), lambda b, h: (b, 0, h))] + w_out,
        scratch_shapes=[ext, ext, ext] + _attn_scratch(seq),
        out_shape=[jax.ShapeDtypeStruct((bsz, seq, d), BF16)] + w_shapes,
        compiler_params=_cparams("parallel", "parallel"),
    )(qkv, qkv, qkv, pos.reshape(bsz, seq, 1), sfeat, logm, *cast_weights)


def _post_attn_kernel(o_ref, wo_ref, x_ref, g1_ref, n2_ref, sc_ref, sh_ref, wrh_ref, wrl_ref,
                      xm_ref, h_ref, hs_ref, lg_ref, il_ref):
    y = jnp.dot(o_ref[...], wo_ref[...], preferred_element_type=F32)
    xm = x_ref[...] + g1_ref[...] * y
    xm_ref[...] = xm
    h = _rms(xm) * n2_ref[...]
    h = h * (1.0 + sc_ref[...]) + sh_ref[...]
    h_ref[...] = h.astype(h_ref.dtype)
    hs_ref[...] = _rows_to_slab_rows(h, il_ref)
    h_hi = h.astype(BF16)
    h_lo = (h - h_hi.astype(F32)).astype(BF16)
    lg_ref[...] = (_dot_nt(wrh_ref[...], h_hi) + _dot_nt(wrh_ref[...], h_lo)
                   + _dot_nt(wrl_ref[...], h_hi))


def _post_attn(o, w_o, xf, g1, n2, sc2, sh2, w_router, seq):
    n, d = xf.shape
    ne = w_router.shape[1]
    tm = min(2 * ROW_TILE, seq)
    per_b = seq // tm
    slab_rows = d // LANES
    wr_t = w_router.T.astype(F32)
    wr_hi = wr_t.astype(BF16)
    vec = pl.BlockSpec((None, 1, d), lambda i: (i // per_b, 0, 0))
    row = lambda w: pl.BlockSpec((tm, w), lambda i: (i, 0))
    return pl.pallas_call(
        _post_attn_kernel,
        grid=(n // tm,),
        in_specs=[row(d), pl.BlockSpec((d, d), lambda i: (0, 0)), row(d), vec,
                  pl.BlockSpec((1, d), lambda i: (0, 0)), vec, vec,
                  pl.BlockSpec((ne, d), lambda i: (0, 0)), pl.BlockSpec((ne, d), lambda i: (0, 0))],
        out_specs=[row(d), row(d), pl.BlockSpec((tm * slab_rows, LANES), lambda i: (i, 0)),
                   pl.BlockSpec((ne, tm), lambda i: (0, i))],
        out_shape=[jax.ShapeDtypeStruct((n, d), F32), jax.ShapeDtypeStruct((n, d), BF16),
                   jax.ShapeDtypeStruct((n * slab_rows, LANES), BF16),
                   jax.ShapeDtypeStruct((ne, n), F32)],
        scratch_shapes=[pltpu.VMEM((tm * slab_rows, LANES), F32)],
        compiler_params=_cparams("parallel"),
    )(o, w_o, xf, g1, n2, sc2, sh2, wr_hi, (wr_t - wr_hi.astype(F32)).astype(BF16))


def _moe_rows(tile):
    rows = tile * TOP_K + N_EXPERTS * (SUBLANES - 1) + EROWS
    return -(-rows // SUBLANES) * SUBLANES


def _first_index(hit, index, limit):
    return jnp.min(jnp.where(hit, index, limit), axis=0, keepdims=True)


def _route_kernel(lg_ref, bias_ref, tri_ref, dest_ref, w_ref, seg_ref, cnt_ref, *, row_pitch):
    t = lg_ref.shape[1]
    per_g = N_EXPERTS // N_GROUPS
    sub = lax.broadcasted_iota(jnp.int32, (per_g, t), 0).astype(F32)
    neg = -jnp.inf
    scores, sel = [], []
    for g in range(N_GROUPS):
        rows = slice(g * per_g, (g + 1) * per_g)
        sc = jax.nn.sigmoid(lg_ref[rows, :])
        scores.append(sc)
        sel.append(sc + bias_ref[rows, :])

    gs = []
    for g in range(N_GROUPS):
        m1 = jnp.max(sel[g], axis=0, keepdims=True)
        first = _first_index(sel[g] == m1, sub, float(per_g))
        m2 = jnp.max(jnp.where(sub == first, neg, sel[g]), axis=0, keepdims=True)
        gs.append(m1 + m2)
    gsc = jnp.concatenate(gs, axis=0)
    gsub = lax.broadcasted_iota(jnp.int32, (N_GROUPS, t), 0).astype(F32)
    gkeep = jnp.zeros((N_GROUPS, t), F32)
    for _ in range(TOPK_GROUPS):
        m = jnp.max(gsc, axis=0, keepdims=True)
        pick = gsub == _first_index(gsc == m, gsub, float(N_GROUPS))
        gkeep = jnp.where(pick, 1.0, gkeep)
        gsc = jnp.where(pick, neg, gsc)

    eidx = [sub + float(g * per_g) for g in range(N_GROUPS)]
    masked = [jnp.where(jnp.broadcast_to(gkeep[g:g + 1, :], (per_g, t)) > 0.5, sel[g], neg)
              for g in range(N_GROUPS)]
    chosen = [jnp.zeros((per_g, t), F32) for _ in range(N_GROUPS)]
    wsel = [jnp.zeros((per_g, t), F32) for _ in range(N_GROUPS)]
    picked = []
    for _ in range(TOP_K):
        m = jnp.max(functools.reduce(jnp.maximum, masked), axis=0, keepdims=True)
        cand = functools.reduce(jnp.minimum, [jnp.where(masked[g] == m, eidx[g], float(N_EXPERTS))
                                              for g in range(N_GROUPS)])
        e_first = jnp.min(cand, axis=0, keepdims=True)
        picked.append(e_first)
        for g in range(N_GROUPS):
            pick = eidx[g] == e_first
            chosen[g] = jnp.where(pick, 1.0, chosen[g])
            wsel[g] = jnp.where(pick, scores[g], wsel[g])
            masked[g] = jnp.where(pick, neg, masked[g])

    denom = jnp.sum(functools.reduce(jnp.add, wsel), axis=0, keepdims=True)

    c_all = jnp.concatenate(chosen, axis=0)
    nblk = t // LANES
    stacked = jnp.concatenate([c_all[:, b * LANES:(b + 1) * LANES] for b in range(nblk)], axis=0)
    within = jnp.dot(stacked.astype(BF16), tri_ref[...], preferred_element_type=F32)
    ranks, before = [], jnp.zeros((N_EXPERTS, 1), F32)
    for b in range(nblk):
        rows = slice(b * N_EXPERTS, (b + 1) * N_EXPERTS)
        ranks.append(within[rows, :] + before)
        before = before + within[rows, LANES - 1:LANES] + stacked[rows, LANES - 1:LANES]
    rank = jnp.concatenate(ranks, axis=1)
    counts = before
    padded = jnp.ceil(counts / SUBLANES) * SUBLANES
    below = (lax.broadcasted_iota(jnp.int32, (N_EXPERTS, N_EXPERTS), 1)
             < lax.broadcasted_iota(jnp.int32, (N_EXPERTS, N_EXPERTS), 0)).astype(F32)
    seg = jnp.dot(below, jnp.broadcast_to(padded, (N_EXPERTS, LANES)), preferred_element_type=F32,
                  precision=lax.Precision.HIGHEST)
    row_of = seg[:, :1] + rank

    dest_rows, w_rows = [], []
    for k in range(TOP_K):
        d_k = jnp.zeros((per_g, t), F32)
        w_k = jnp.zeros((per_g, t), F32)
        for g in range(N_GROUPS):
            hit = eidx[g] == picked[k]
            d_k = jnp.where(hit, row_of[g * per_g:(g + 1) * per_g, :], d_k)
            w_k = jnp.where(hit, wsel[g], w_k)
        dest_rows.append(jnp.sum(d_k, axis=0, keepdims=True))
        w_rows.append(jnp.sum(w_k, axis=0, keepdims=True) / denom * ROUTED_SCALE)
    fill = [jnp.zeros((SUBLANES - TOP_K, t), F32)]
    dest8 = (jnp.concatenate(dest_rows + fill, axis=0) * row_pitch).astype(jnp.int32)
    w8 = jnp.concatenate(w_rows + fill, axis=0)
    for j in range(t // LANES):
        dest_ref[j] = dest8[:, j * LANES:(j + 1) * LANES]
        w_ref[j] = w8[:, j * LANES:(j + 1) * LANES]
    seg_ref[...] = seg.astype(jnp.int32)
    cnt_ref[...] = jnp.broadcast_to(counts, (N_EXPERTS, LANES)).astype(jnp.int32)


def _route(logits_t, router_bias, tile, row_pitch):
    ne, n = logits_t.shape
    nt = n // tile
    tri = jnp.asarray(np.triu(np.ones((LANES, LANES), np.float32), 1), dtype=BF16)
    per_tile = lambda rows, width: pl.BlockSpec((None, rows, width), lambda i: (i, 0, 0))
    table = pl.BlockSpec((None, tile // LANES, SUBLANES, LANES), lambda i: (i, 0, 0, 0))
    dest, wts, seg, cnt = pl.pallas_call(
        functools.partial(_route_kernel, row_pitch=row_pitch),
        grid=(nt,),
        in_specs=[pl.BlockSpec((ne, tile), lambda i: (0, i)),
                  pl.BlockSpec((ne, 1), lambda i: (0, 0)),
                  pl.BlockSpec((LANES, LANES), lambda i: (0, 0))],
        out_specs=[table, table, per_tile(ne, LANES), per_tile(ne, LANES)],
        out_shape=[jax.ShapeDtypeStruct((nt, tile // LANES, SUBLANES, LANES), jnp.int32),
                   jax.ShapeDtypeStruct((nt, tile // LANES, SUBLANES, LANES), F32),
                   jax.ShapeDtypeStruct((nt, ne, LANES), jnp.int32),
                   jax.ShapeDtypeStruct((nt, ne, LANES), jnp.int32)],
        compiler_params=_cparams("parallel"),
    )(logits_t, router_bias.astype(F32).reshape(ne, 1), tri)
    return seg[:, :, 0].reshape(-1), cnt[:, :, 0].reshape(-1), dest, wts


def _swiglu(xb, wgu_ref, wdn_ref):
    gu = jnp.dot(xb, wgu_ref[...], preferred_element_type=F32)
    half = wgu_ref.shape[1] // 2
    act = _silu(gu[:, :half]) * gu[:, half:]
    return jnp.dot(act.astype(BF16), wdn_ref[...], preferred_element_type=F32)


def _slabs_to_rows(p, il_ref, r):
    n = il_ref.shape[0] // r
    il_ref[...] = pltpu.bitcast(p, BF16).astype(F32)
    return jnp.concatenate([il_ref[pl.ds(2 * c + h, r, stride=n), :]
                            for h in range(2) for c in range(n // 2)], axis=1)


def _rows_to_slab_rows(v, il_ref):
    r = v.shape[0]
    n = il_ref.shape[0] // r
    for h in range(2):
        for c in range(n // 2):
            lane0 = (h * (n // 2) + c) * LANES
            il_ref[pl.ds(2 * c + h, r, stride=n), :] = v[:, lane0:lane0 + LANES]
    return il_ref[...].astype(BF16)


def _rows_to_slabs(v, il_ref):
    return pltpu.bitcast(_rows_to_slab_rows(v, il_ref), U32)


def _moe_kernel(start_ref, count_ref, hp_ref, hb_ref, dest_ref, wrow_ref, wgu_hbm, wdn_hbm,
                sgu_ref, sdn_ref, xm_ref, g2_ref, o_ref, xy_ref, il_ref, yp_ref, zf_ref,
                comb_ref, wgu_buf, wdn_buf, wsem):
    t = pl.program_id(0)
    s = pl.program_id(1)
    slab = xm_ref.shape[1] // (2 * LANES)
    tile = hp_ref.shape[0] // (2 * slab)
    group_rows = SUBLANES * slab

    def weight_copies(e, slot):
        return (pltpu.make_async_copy(wgu_hbm.at[e], wgu_buf.at[slot], wsem.at[0, slot]),
                pltpu.make_async_copy(wdn_hbm.at[e], wdn_buf.at[slot], wsem.at[1, slot]))

    def slab_at(ref, first):
        return ref.at[pl.ds(pl.multiple_of(first, slab), slab), :]

    @pl.when(s == 0)
    def _first_weights():
        for p in range(EXPERT_GROUP):
            for cp in weight_copies(p, p):
                cp.start()

    @pl.when(jnp.logical_and(t == 0, s == 0))
    def _clear():
        xy_ref[...] = jnp.zeros(xy_ref.shape, U32)

    @pl.when(s == 0)
    def _dispatch():
        def group(j, carry):
            for sub in range(LANES // SUBLANES):
                p0 = pl.multiple_of((j * (LANES // SUBLANES) + sub) * 2 * group_rows, 2 * group_rows)
                rows = pltpu.bitcast(hp_ref[pl.ds(p0, 2 * group_rows), :], U32)
                for u in range(SUBLANES):
                    for k in range(TOP_K):
                        slab_at(xy_ref, dest_ref[j, k, sub * SUBLANES + u])[...] = (
                            rows[u * slab:(u + 1) * slab, :])
            return carry

        lax.fori_loop(0, tile // LANES, group, 0)

    @pl.when(s == 0)
    def _experts():
        def chunk_base(e, c):
            return pl.multiple_of((start_ref[t * N_EXPERTS + e] + c * EROWS) * slab, group_rows)

        def load_rows(base, p):
            return _slabs_to_rows(xy_ref[pl.ds(base, EROWS * slab), :], il_ref.at[p], EROWS).astype(BF16)

        def put_rows(p, base, valid):
            def put(j, c2):
                r0 = pl.multiple_of(j * group_rows, group_rows)
                xy_ref[pl.ds(base + r0, group_rows), :] = yp_ref[p, pl.ds(r0, group_rows), :]
                return c2

            lax.fori_loop(0, (jnp.clip(valid, 0, EROWS) + SUBLANES - 1) // SUBLANES, put, 0)

        grp = EXPERT_GROUP

        def expert_group(i, carry0):
            wslot = lax.rem(i, 2) * grp

            @pl.when(i + 1 < N_EXPERTS // grp)
            def _prefetch():
                for p in range(grp):
                    for cp in weight_copies(grp * (i + 1) + p, grp - wslot + p):
                        cp.start()

            for p in range(grp):
                for cp in weight_copies(grp * i + p, wslot + p):
                    cp.wait()
            counts = [count_ref[t * N_EXPERTS + grp * i + p] for p in range(grp)]
            bases = [chunk_base(grp * i + p, 0) for p in range(grp)]
            xs = [load_rows(bases[p], p) for p in range(grp)]
            ys = [_swiglu(xs[p], wgu_buf.at[wslot + p], wdn_buf.at[wslot + p]) for p in range(grp)]
            for p in range(grp):
                yp_ref[p] = _rows_to_slabs(ys[p], il_ref.at[p])
            for p in range(grp):
                put_rows(p, bases[p], counts[p])

            for p in range(grp):
                def more(c, carry, p=p):
                    base = chunk_base(grp * i + p, c)
                    y = _swiglu(load_rows(base, p), wgu_buf.at[wslot + p], wdn_buf.at[wslot + p])
                    yp_ref[p] = _rows_to_slabs(y, il_ref.at[p])
                    put_rows(p, base, counts[p] - c * EROWS)
                    return carry

                lax.fori_loop(1, (counts[p] + EROWS - 1) // EROWS, more, 0)
            return carry0

        lax.fori_loop(0, N_EXPERTS // grp, expert_group, 0)

    @pl.when(s >= 1)
    def _combine():
        sb = s - 1

        def group(j, carry):
            tj = sb * (CROWS // LANES) + j
            for sub in range(LANES // SUBLANES):
                r0 = pl.multiple_of(j * LANES + sub * SUBLANES, SUBLANES)
                half = sub % 2
                for u in range(SUBLANES):
                    lane = sub * SUBLANES + u
                    acc = None
                    for k in range(TOP_K):
                        z = pltpu.bitcast(slab_at(xy_ref, dest_ref[tj, k, lane])[...], BF16).astype(F32)
                        acc = wrow_ref[tj, k, lane] * z if acc is None else acc + wrow_ref[tj, k, lane] * z
                    zf_ref[half, u * 2 * slab:(u + 1) * 2 * slab, :] = acc
                comb_ref[pl.ds(r0, SUBLANES), :] = jnp.concatenate(
                    [zf_ref[half, pl.ds(2 * c + h, SUBLANES, stride=2 * slab), :]
                     for h in range(2) for c in range(slab)], axis=1)
            return carry

        lax.fori_loop(0, CROWS // LANES, group, 0)
        shared = _swiglu(hb_ref[...], sgu_ref, sdn_ref)
        o_ref[...] = xm_ref[...] + g2_ref[...] * (comb_ref[...] + shared)


def _moe(h, h_slabs, xm, g2, logits_t, router_bias, w_gu, w_dn, s_gu, s_dn, layer, seq):
    n, d = h.shape
    slab = d // (2 * LANES)
    tile = min(MOE_TILE, seq)
    nt = n // tile
    per_b = seq // tile
    csteps = tile // CROWS
    assert seq % tile == 0 and tile % CROWS == 0
    assert tile % LANES == 0 and CROWS % LANES == 0
    seg_start, seg_count, dest_rows, w_rows = _route(logits_t, router_bias, tile, slab)
    rows = _moe_rows(tile)
    table = pl.BlockSpec((None, tile // LANES, SUBLANES, LANES), lambda t, s, a, b: (t, 0, 0, 0),
                         memory_space=pltpu.SMEM)
    cst = lambda s: jnp.maximum(s - 1, 0)
    crow = pl.BlockSpec((CROWS, d), lambda t, s, a, b: (t * csteps + cst(s), 0))
    hbm = pl.BlockSpec(memory_space=pl.ANY)
    single = pl.Buffered(1)
    return pl.pallas_call(
        _moe_kernel,
        grid_spec=pltpu.PrefetchScalarGridSpec(
            num_scalar_prefetch=2,
            grid=(nt, 1 + csteps),
            in_specs=[pl.BlockSpec((tile * 2 * slab, LANES), lambda t, s, a, b: (t, 0),
                                   pipeline_mode=single),
                      crow, table, table,
                      hbm, hbm,
                      pl.BlockSpec((None,) + s_gu.shape[1:], lambda t, s, a, b: (layer, 0, 0),
                                   pipeline_mode=single),
                      pl.BlockSpec((None,) + s_dn.shape[1:], lambda t, s, a, b: (layer, 0, 0),
                                   pipeline_mode=single),
                      crow,
                      pl.BlockSpec((None, 1, d), lambda t, s, a, b: (t // per_b, 0, 0))],
            out_specs=crow,
            scratch_shapes=[pltpu.VMEM((rows * slab, LANES), U32),
                            pltpu.VMEM((EXPERT_GROUP, EROWS * 2 * slab, LANES), F32),
                            pltpu.VMEM((EXPERT_GROUP, EROWS * slab, LANES), U32),
                            pltpu.VMEM((2, SUBLANES * 2 * slab, LANES), F32),
                            pltpu.VMEM((CROWS, d), F32),
                            pltpu.VMEM((2 * EXPERT_GROUP, d, 2 * EXPERT_DIM), BF16),
                            pltpu.VMEM((2 * EXPERT_GROUP, EXPERT_DIM, d), BF16),
                            pltpu.SemaphoreType.DMA((2, 2 * EXPERT_GROUP))]),
        out_shape=jax.ShapeDtypeStruct((n, d), F32),
        compiler_params=_cparams("arbitrary", "arbitrary"),
    )(seg_start, seg_count, h_slabs, h, dest_rows, w_rows, w_gu, w_dn, s_gu, s_dn, xm, g2)


def _mla_prep_kernel(x_ref, n1_ref, sc1_ref, sh1_ref, nk_ref, sck_ref, shk_ref,
                     wdq_ref, gql_ref, wuq_ref, wuqs_ref, wdown_ref, gkl_ref, wk_ref, gk_ref, wuv_ref,
                     qa_ref, qb_ref, kc_ref, ksa_ref, ksb_ref, q_ref, k_ref, v_ref):
    xn = _rms(x_ref[...])
    hq = xn * n1_ref[...] * (1.0 + sc1_ref[...]) + sh1_ref[...]
    hk = xn * nk_ref[...] * (1.0 + sck_ref[...]) + shk_ref[...]
    n_heads = q_ref.shape[1] // LANES
    half = ROPE_DIM // 2

    cq = (_rms(jnp.dot(hq.astype(BF16), wdq_ref[...], preferred_element_type=F32)) * gql_ref[...]).astype(BF16)
    q_raw = jnp.dot(cq, wuq_ref[...], preferred_element_type=F32)
    q_swp = jnp.dot(cq, wuqs_ref[...], preferred_element_type=F32)
    qa, qb = qa_ref[...], qb_ref[...]
    for h in range(n_heads):
        sl = slice(h * LANES, (h + 1) * LANES)
        t = q_raw[:, sl]
        r = lax.rsqrt(jnp.sum(t * t, axis=-1, keepdims=True) / QK_DIM + NORM_EPS)
        q_ref[:, sl] = (r * (t * qa + q_swp[:, sl] * qb)).astype(q_ref.dtype)

    down = jnp.dot(hk.astype(BF16), wdown_ref[...], preferred_element_type=F32)
    ckv = (_rms(down[:, :KV_RANK]) * gkl_ref[...]).astype(BF16)
    pe = down[:, KV_RANK:]
    ss_pe = jnp.sum(pe * pe, axis=-1, keepdims=True)
    rot = (pe * kc_ref[...] + pltpu.roll(pe, LANES - half, 1) * ksa_ref[...]
           + pltpu.roll(pe, half, 1) * ksb_ref[...])
    rot_hi = rot.astype(BF16)
    rot_lo = (rot - rot_hi.astype(F32)).astype(BF16)
    k_raw = jnp.dot(jnp.concatenate([ckv, rot_hi, rot_lo], axis=1), wk_ref[...],
                    preferred_element_type=F32)
    nope = lax.broadcasted_iota(jnp.int32, (x_ref.shape[0], LANES), 1) < NOPE_DIM
    for h in range(n_heads):
        sl = slice(h * LANES, (h + 1) * LANES)
        t = k_raw[:, sl]
        ss = jnp.sum(jnp.where(nope, t * t, 0.0), axis=-1, keepdims=True) + ss_pe
        k_ref[:, sl] = (t * lax.rsqrt(ss / QK_DIM + NORM_EPS) * gk_ref[...]).astype(k_ref.dtype)
    v_ref[...] = jnp.dot(ckv, wuv_ref[...], preferred_element_type=F32).astype(v_ref.dtype)


def _pad_heads(w, n_heads, width):
    r = w.shape[0]
    return jnp.pad(w.reshape(r, n_heads, width), ((0, 0), (0, 0), (0, LANES - width))).reshape(r, n_heads * LANES)


def _swap_rope_halves(a):
    half = ROPE_DIM // 2
    lead = a.shape[:-1]
    return jnp.concatenate([jnp.zeros(lead + (NOPE_DIM,), a.dtype), a[..., NOPE_DIM + half:QK_DIM],
                            a[..., NOPE_DIM:NOPE_DIM + half],
                            jnp.zeros(lead + (LANES - QK_DIM,), a.dtype)], axis=-1)


def _mla_prep(xf, pos, n1, sc1, sh1, nk, sck, shk, w_dq, g_ql, w_uq, g_q, w_down, g_kl, w_uk, w_uv,
              g_k, seq):
    n, d = xf.shape
    n_heads = w_uv.shape[1] // V_DIM
    q_rank = w_dq.shape[1]
    hw = n_heads * LANES
    tm = min(ROW_TILE, seq)
    per_b = seq // tm

    half = ROPE_DIM // 2
    inv = jnp.power(ROPE_THETA, -jnp.arange(half, dtype=F32) / half)
    ang = pos.astype(F32).reshape(n, 1) * inv
    cs = jnp.concatenate([jnp.cos(ang), jnp.sin(ang)], axis=1)
    tail = LANES - QK_DIM
    q_mul = QK_DIM ** -0.5 * LOG2E
    j = np.arange(half)
    sel = np.zeros((ROPE_DIM, 5, LANES), np.float32)
    sel[j, 0, NOPE_DIM + j] = sel[j, 0, NOPE_DIM + half + j] = 1.0
    sel[half + j, 1, NOPE_DIM + j], sel[half + j, 1, NOPE_DIM + half + j] = -1.0, 1.0
    sel[j, 2, j] = sel[j, 2, half + j] = 1.0
    sel[half + j, 3, j] = -1.0
    sel[half + j, 4, half + j] = 1.0
    gq_pad = jnp.pad(g_q, (0, tail)) * q_mul
    gq_swp = _swap_rope_halves(g_q.reshape(1, QK_DIM))[0] * q_mul
    gk1, gk2 = g_k[NOPE_DIM:NOPE_DIM + half], g_k[NOPE_DIM + half:]
    kpad = jnp.zeros((LANES - ROPE_DIM,), F32)
    k_own, k_swp = jnp.concatenate([gk1, gk2, kpad]), jnp.concatenate([gk2, gk1, kpad])
    place = (jnp.asarray(sel) * jnp.stack([gq_pad, gq_swp, k_own, k_swp, k_swp])).reshape(ROPE_DIM, -1)
    const = jnp.pad(gq_pad[:NOPE_DIM], (0, 5 * LANES - NOPE_DIM))
    tables = jnp.dot(cs, place, precision=lax.Precision.HIGHEST) + const
    gk_mask = jnp.concatenate([g_k[:NOPE_DIM], jnp.ones((ROPE_DIM,), F32),
                               jnp.zeros((tail,), F32)]).reshape(1, LANES)

    wuq_p = _pad_heads(w_uq, n_heads, QK_DIM).astype(BF16)
    wuq_s = _swap_rope_halves(w_uq.reshape(q_rank, n_heads, QK_DIM)).reshape(q_rank, hw).astype(BF16)
    wdown_p = jnp.pad(w_down, ((0, 0), (0, LANES - ROPE_DIM))).astype(BF16)
    eye = jnp.pad(jnp.eye(ROPE_DIM, dtype=F32), ((0, LANES - ROPE_DIM), (NOPE_DIM, tail)))
    pe_rows = jnp.tile(eye, (1, n_heads))
    wk = jnp.concatenate([_pad_heads(w_uk, n_heads, NOPE_DIM), pe_rows, pe_rows], axis=0).astype(BF16)

    vec = pl.BlockSpec((None, 1, d), lambda i: (i // per_b, 0, 0))
    full = lambda a: pl.BlockSpec(a.shape, lambda i: (0,) * a.ndim)
    row = lambda wdt: pl.BlockSpec((tm, wdt), lambda i: (i, 0))
    args = [xf, n1, sc1, sh1, nk, sck, shk,
            w_dq.astype(BF16), g_ql.reshape(1, q_rank), wuq_p, wuq_s,
            wdown_p, g_kl.reshape(1, KV_RANK), wk, gk_mask, w_uv.astype(BF16),
            tables, tables, tables, tables, tables]
    table_specs = [pl.BlockSpec((tm, LANES), lambda i, c=c: (i, c)) for c in range(5)]
    specs = [row(d), full(n1), vec, vec, full(nk), vec, vec] + [full(a) for a in args[7:16]] + table_specs
    return pl.pallas_call(
        _mla_prep_kernel,
        grid=(n // tm,),
        in_specs=specs,
        out_specs=[row(hw), row(hw), row(n_heads * V_DIM)],
        out_shape=[jax.ShapeDtypeStruct((n, hw), BF16), jax.ShapeDtypeStruct((n, hw), BF16),
                   jax.ShapeDtypeStruct((n, n_heads * V_DIM), BF16)],
        compiler_params=_cparams("parallel"),
    )(*args)


def _causal_bias(seq):
    a = np.arange(QBLK)[:, None]
    c = np.arange(seq)[None, :]
    return np.where(c <= (seq - QBLK) + a, 0.0, MASKED).astype(np.float32)


def _mla_attn_kernel(q_ref, k_ref, v_ref, mask_ref, wa_ref, wb_ref, o_ref, wa_out, wb_out,
                     s_ref, p_ref, ve_ref):
    wa_out[...] = wa_ref[...].astype(BF16)
    wb_out[...] = wb_ref[...].astype(BF16)
    seq = q_ref.shape[0]
    lane_lo = lax.broadcasted_iota(jnp.int32, (QBLK, LANES), 1) < V_DIM

    def qk_of(blk, h, w):
        lanes = slice(h * LANES, (h + 1) * LANES)
        return q_ref[w - QBLK:w, lanes], k_ref[0:w, lanes]

    _attention_blocks(seq, qk_of, v_ref, mask_ref, s_ref, p_ref, ve_ref, o_ref, lane_lo,
                      diagonal_bias=True)


def _mla_attention(q, k, v, cast_weights, layer):
    bsz, seq, hw = q.shape
    n_pairs = hw // (2 * LANES)
    qk = pl.BlockSpec((None, seq, 2 * LANES), lambda b, h: (b, 0, h))
    vo = pl.BlockSpec((None, seq, LANES), lambda b, h: (b, 0, h))
    grid = (bsz, n_pairs)
    w_in, w_out, w_shapes = _cast_rider(cast_weights, layer, grid)
    return pl.pallas_call(
        _mla_attn_kernel,
        grid=grid,
        in_specs=[qk, qk, vo, pl.BlockSpec((QBLK, seq), lambda b, h: (0, 0))] + w_in,
        out_specs=[vo] + w_out,
        scratch_shapes=_attn_scratch(seq),
        out_shape=[jax.ShapeDtypeStruct(v.shape, BF16)] + w_shapes,
        compiler_params=_cparams("parallel", "parallel"),
    )(q, k, v, jnp.asarray(_causal_bias(seq)), *cast_weights)


def kernel(x, c, pos, ada_w, ada_b, norm1_g, norm2_g, a_w_qkv, a_q_norm, a_k_norm, a_w_o, kv_ada_w, kv_ada_b, kv_norm_g, kv_w_down, kv_latent_norm, kv_w_uk, kv_w_uv, kv_k_norm, b_w_dq, b_q_latent_norm, b_w_uq, b_q_norm, b_w_o, moe_w_router, moe_router_bias, moe_w_gate_up, moe_w_down, moe_shared_gate_up, moe_shared_down):
    bsz, seq, d = x.shape
    n = bsz * seq
    depth = ada_w.shape[0]
    n_a = a_w_qkv.shape[0]
    assert seq % QBLK == 0
    xf = x.reshape(n, d)

    expert_weights = (moe_w_gate_up, moe_w_down)
    s_gu = moe_shared_gate_up.astype(BF16)
    s_dn = moe_shared_down.astype(BF16)
    row = lambda g: g.reshape(1, d)

    for layer in range(depth):
        mod = _modulation(c, ada_w, ada_b, layer).reshape(bsz, 6, 1, d)
        sh1, sc1, g1, sh2, sc2, g2 = [mod[:, i] for i in range(6)]
        if layer < n_a:
            qkv = _qkv_proj(xf, row(norm1_g[layer]), sc1, sh1, a_w_qkv[layer].astype(BF16),
                            a_q_norm[layer], a_k_norm[layer], seq)
            o, w_gu, w_dn = _dilated_attention(qkv.reshape(bsz, seq, -1), pos, expert_weights, layer)
            w_o = a_w_o[layer]
        else:
            j = layer - n_a
            kmod = _modulation(c, kv_ada_w[None], kv_ada_b[None], 0).reshape(bsz, 2, 1, d)
            q, k, v = _mla_prep(xf, pos, row(norm1_g[layer]), sc1, sh1, row(kv_norm_g), kmod[:, 1],
                                kmod[:, 0], b_w_dq[j], b_q_latent_norm[j], b_w_uq[j], b_q_norm[j],
                                kv_w_down, kv_latent_norm, kv_w_uk, kv_w_uv, kv_k_norm, seq)
            o, w_gu, w_dn = _mla_attention(q.reshape(bsz, seq, -1), k.reshape(bsz, seq, -1),
                                           v.reshape(bsz, seq, -1), expert_weights, layer)
            w_o = b_w_o[j]
        xm, h2, h2_slabs, logits = _post_attn(o.reshape(n, d), w_o.astype(BF16), xf, g1,
                                              row(norm2_g[layer]), sc2, sh2, moe_w_router[layer], seq)
        xf = _moe(h2, h2_slabs, xm, g2, logits, moe_router_bias[layer], w_gu, w_dn, s_gu, s_dn,
                  layer, seq)
    return xf.reshape(bsz, seq, d)
```
